```python
import jax, jax.numpy as jnp
from jax import lax
import numpy as np

D_MODEL = 2048
BATCH = 8
SEQ = 8192
DEPTH = 1

CHUNK = 64
EPS = 1e-5
N_BRANCHES = 2
SGU_WIDTH = D_MODEL
SGU_BLOCK = 128
SGU_GROUPS = 16
SGU_GROUP_DIM = SGU_WIDTH // SGU_GROUPS
SSD_WIDTH = D_MODEL
SSD_HEADDIM = 64
SSD_HEADS = SSD_WIDTH // SSD_HEADDIM
SSD_GROUPS = 4
SSD_HEADS_PER_GROUP = SSD_HEADS // SSD_GROUPS
SSD_STATE = 128
CONV_WIDTH = 4
XBC_WIDTH = SSD_WIDTH + 2 * SSD_GROUPS * SSD_STATE
IN_SPLITS = (SGU_WIDTH, SGU_WIDTH, SGU_WIDTH, SSD_WIDTH, XBC_WIDTH, SSD_HEADS, N_BRANCHES * D_MODEL)
IN_PROJ_WIDTH = sum(IN_SPLITS)
IN_OFFSETS = tuple(int(v) for v in np.cumsum(IN_SPLITS)[:-1])

kernel_name = "hybrid_sgu_ssd_gated_block"


def rmsnorm(x, w):
    xf = x.astype(jnp.float32)
    xf = xf * lax.rsqrt(jnp.mean(xf * xf, axis=-1, keepdims=True) + EPS)
    return (xf * w.astype(jnp.float32)).astype(x.dtype)


def layernorm(x, g, b):
    xf = x.astype(jnp.float32)
    mu = jnp.mean(xf, axis=-1, keepdims=True)
    var = jnp.mean(jnp.square(xf - mu), axis=-1, keepdims=True)
    y = (xf - mu) * lax.rsqrt(var + EPS) * g.astype(jnp.float32) + b.astype(jnp.float32)
    return y.astype(x.dtype)


def gated_group_rmsnorm(y, z, w):
    h = (y * jax.nn.silu(z)).astype(jnp.float32)
    shp = h.shape
    h = h.reshape(shp[:-1] + (SSD_GROUPS, shp[-1] // SSD_GROUPS))
    h = h * lax.rsqrt(jnp.mean(h * h, axis=-1, keepdims=True) + EPS)
    return (h.reshape(shp) * w.astype(jnp.float32)).astype(y.dtype)


def causal_depthwise_conv(x, w, b):
    S = x.shape[1]
    xp = jnp.pad(x, ((0, 0), (CONV_WIDTH - 1, 0), (0, 0)))
    out = b
    for k in range(CONV_WIDTH):
        out = out + xp[:, k:k + S, :] * w[k]
    return out


def sgu_mixer(u, v, z, norm_g, norm_b, w_s, b_s):
    Bsz, S, _ = u.shape
    nb = S // SGU_BLOCK
    vn = layernorm(v, norm_g, norm_b).reshape(Bsz, nb, SGU_BLOCK, SGU_GROUPS, SGU_GROUP_DIM)
    pos_chunk = jnp.arange(SGU_BLOCK) // CHUNK
    mask = pos_chunk[None, :] <= pos_chunk[:, None]
    w = jnp.where(mask[None], w_s, jnp.zeros_like(w_s))
    mixed = jnp.einsum("gij,bnjgc->bnigc", w, vn) + b_s.T[None, None, :, :, None]
    mixed = mixed.reshape(Bsz, S, SGU_WIDTH)
    return u * mixed * jax.nn.silu(z)


def ssd_mixer(xbc, z, dt_raw, conv_w, conv_b, dt_bias, A_log, D_skip, norm_w):
    Bsz, S, _ = xbc.shape
    nc = S // CHUNK
    G, R, P, N, L = SSD_GROUPS, SSD_HEADS_PER_GROUP, SSD_HEADDIM, SSD_STATE, CHUNK
    xbc = jax.nn.silu(causal_depthwise_conv(xbc, conv_w, conv_b))
    xs, Bm, Cm = jnp.split(xbc, (SSD_WIDTH, SSD_WIDTH + G * N), axis=-1)
    x = xs.reshape(Bsz, nc, L, G, R, P)
    Bm = Bm.reshape(Bsz, nc, L, G, N)
    Cm = Cm.reshape(Bsz, nc, L, G, N)
    dt = jax.nn.softplus(dt_raw.astype(jnp.float32) + dt_bias.astype(jnp.float32))
    dt = dt.reshape(Bsz, nc, L, G, R)
    A = -jnp.exp(A_log.astype(jnp.float32)).reshape(G, R)
    a = jnp.transpose(dt * A, (0, 1, 3, 4, 2))
    acs = jnp.cumsum(a, axis=-1)
    idx = jnp.arange(L)
    causal = idx[:, None] >= idx[None, :]
    seg = acs[..., :, None] - acs[..., None, :]
    Lmat = jnp.exp(jnp.where(causal, seg, -jnp.inf))
    x_dt = x.astype(jnp.float32) * dt[..., None]
    cb = jnp.einsum("bclgn,bcsgn->bcgls", Cm, Bm)
    y_diag = jnp.einsum("bcgls,bcgrls,bcsgrp->bclgrp", cb, Lmat, x_dt)
    decay_states = jnp.exp(acs[..., -1:] - acs)
    states = jnp.einsum("bclgn,bcgrl,bclgrp->bcgrpn", Bm, decay_states, x_dt)
    chunk_decay = jnp.exp(acs[..., -1])

    def step(h, inp):
        dec, st = inp
        return dec[..., None, None] * h + st, h

    h0 = jnp.zeros((Bsz, G, R, P, N), dtype=states.dtype)
    _, prev = lax.scan(step, h0, (jnp.moveaxis(chunk_decay, 1, 0), jnp.moveaxis(states, 1, 0)))
    prev = jnp.moveaxis(prev, 0, 1)
    y_off = jnp.einsum("bclgn,bcgrl,bcgrpn->bclgrp", Cm, jnp.exp(acs), prev)
    y = y_diag + y_off + x.astype(jnp.float32) * D_skip.astype(jnp.float32).reshape(G, R)[..., None]
    y = y.reshape(Bsz, S, SSD_WIDTH).astype(xbc.dtype)
    return gated_group_rmsnorm(y, z, norm_w)


def _fwd_setup_inputs(seed: int = 0) -> dict:
    key = jax.random.key(seed)
    ks = jax.random.split(key, 20)
    f32 = jnp.float32
    nrm = lambda k, shp, s: jax.random.normal(k, shp, f32) * s
    dt_init = jnp.exp(jax.random.uniform(ks[9], (DEPTH, SSD_HEADS), f32, np.log(1e-3), np.log(1e-1)))
    return {
        "x": jax.random.normal(ks[0], (BATCH, SEQ, D_MODEL), f32),
        "norm_w": 1.0 + nrm(ks[1], (DEPTH, D_MODEL), 0.02),
        "w_in": nrm(ks[2], (DEPTH, D_MODEL, IN_PROJ_WIDTH), D_MODEL ** -0.5),
        "gate_b": nrm(ks[3], (DEPTH, N_BRANCHES * D_MODEL), 0.1),
        "sgu_norm_g": 1.0 + nrm(ks[4], (DEPTH, SGU_WIDTH), 0.02),
        "sgu_norm_b": nrm(ks[5], (DEPTH, SGU_WIDTH), 0.02),
        "sgu_w": nrm(ks[6], (DEPTH, SGU_GROUPS, SGU_BLOCK, SGU_BLOCK), SGU_BLOCK ** -0.5),
        "sgu_b": 1.0 + nrm(ks[7], (DEPTH, SGU_GROUPS, SGU_BLOCK), 0.1),
        "conv_w": nrm(ks[8], (DEPTH, CONV_WIDTH, XBC_WIDTH), CONV_WIDTH ** -0.5),
        "conv_b": nrm(ks[10], (DEPTH, XBC_WIDTH), 0.02),
        "dt_bias": dt_init + jnp.log(-jnp.expm1(-dt_init)),
        "A_log": jnp.log(jax.random.uniform(ks[11], (DEPTH, SSD_HEADS), f32, 1.0, 16.0)),
        "D_skip": 1.0 + nrm(ks[12], (DEPTH, SSD_HEADS), 0.1),
        "ssd_norm_w": 1.0 + nrm(ks[13], (DEPTH, SSD_WIDTH), 0.02),
        "w_out": nrm(ks[14], (DEPTH, D_MODEL, D_MODEL), D_MODEL ** -0.5),
        "final_norm_w": 1.0 + nrm(ks[15], (D_MODEL,), 0.02),
    }


def _fwd_reference(x, norm_w, w_in, gate_b, sgu_norm_g, sgu_norm_b, sgu_w, sgu_b, conv_w, conv_b,
              dt_bias, A_log, D_skip, ssd_norm_w, w_out, final_norm_w):
    h = x
    Bsz, S, _ = x.shape
    for l in range(DEPTH):
        xn = rmsnorm(h, norm_w[l])
        proj = jnp.einsum("bsd,de->bse", xn, w_in[l])
        u_a, v_a, z_a, z_b, xbc, dt_raw, gate_logits = jnp.split(proj, IN_OFFSETS, axis=-1)
        y_a = sgu_mixer(u_a, v_a, z_a, sgu_norm_g[l], sgu_norm_b[l], sgu_w[l], sgu_b[l])
        y_b = ssd_mixer(xbc, z_b, dt_raw, conv_w[l], conv_b[l], dt_bias[l], A_log[l], D_skip[l], ssd_norm_w[l])
        gates = jax.nn.sigmoid(gate_logits + gate_b[l]).reshape(Bsz, S, N_BRANCHES, D_MODEL)
        merged = gates[:, :, 0, :] * y_a + gates[:, :, 1, :] * y_b
        h = h + jnp.einsum("bsd,de->bse", merged, w_out[l])
    return rmsnorm(h, final_norm_w)


import jax as _jax
import jax.numpy as _jnp

TWIN_FORMAT = 'train_step'
FWD_PARAMS = ['x', 'norm_w', 'w_in', 'gate_b', 'sgu_norm_g', 'sgu_norm_b', 'sgu_w', 'sgu_b', 'conv_w', 'conv_b', 'dt_bias', 'A_log', 'D_skip', 'ssd_norm_w', 'w_out', 'final_norm_w']
TWIN_WEIGHTS = ['norm_w', 'w_in', 'gate_b', 'sgu_norm_g', 'sgu_norm_b', 'sgu_w', 'sgu_b', 'conv_w', 'conv_b', 'dt_bias', 'A_log', 'D_skip', 'ssd_norm_w', 'w_out', 'final_norm_w']
TWIN_DIFF_INPUT = 'x'
TWIN_INPUTS = ['x', 'norm_w', 'w_in', 'gate_b', 'sgu_norm_g', 'sgu_norm_b', 'sgu_w', 'sgu_b', 'conv_w', 'conv_b', 'dt_bias', 'A_log', 'D_skip', 'ssd_norm_w', 'w_out', 'final_norm_w', 'loss_target', 'm_norm_w', 'm_w_in', 'm_gate_b', 'm_sgu_norm_g', 'm_sgu_norm_b', 'm_sgu_w', 'm_sgu_b', 'm_conv_w', 'm_conv_b', 'm_dt_bias', 'm_A_log', 'm_D_skip', 'm_ssd_norm_w', 'm_w_out', 'm_final_norm_w', 'v_norm_w', 'v_w_in', 'v_gate_b', 'v_sgu_norm_g', 'v_sgu_norm_b', 'v_sgu_w', 'v_sgu_b', 'v_conv_w', 'v_conv_b', 'v_dt_bias', 'v_A_log', 'v_D_skip', 'v_ssd_norm_w', 'v_w_out', 'v_final_norm_w']
TWIN_OUTPUTS = ['loss', 'grad_x', 'grad_norm_w', 'grad_w_in', 'grad_gate_b', 'grad_sgu_norm_g', 'grad_sgu_norm_b', 'grad_sgu_w', 'grad_sgu_b', 'grad_conv_w', 'grad_conv_b', 'grad_dt_bias', 'grad_A_log', 'grad_D_skip', 'grad_ssd_norm_w', 'grad_w_out', 'grad_final_norm_w', 'delta_norm_w', 'delta_w_in', 'delta_gate_b', 'delta_sgu_norm_g', 'delta_sgu_norm_b', 'delta_sgu_w', 'delta_sgu_b', 'delta_conv_w', 'delta_conv_b', 'delta_dt_bias', 'delta_A_log', 'delta_D_skip', 'delta_ssd_norm_w', 'delta_w_out', 'delta_final_norm_w', 'new_m_norm_w', 'new_m_w_in', 'new_m_gate_b', 'new_m_sgu_norm_g', 'new_m_sgu_norm_b', 'new_m_sgu_w', 'new_m_sgu_b', 'new_m_conv_w', 'new_m_conv_b', 'new_m_dt_bias', 'new_m_A_log', 'new_m_D_skip', 'new_m_ssd_norm_w', 'new_m_w_out', 'new_m_final_norm_w', 'new_v_norm_w', 'new_v_w_in', 'new_v_gate_b', 'new_v_sgu_norm_g', 'new_v_sgu_norm_b', 'new_v_sgu_w', 'new_v_sgu_b', 'new_v_conv_w', 'new_v_conv_b', 'new_v_dt_bias', 'new_v_A_log', 'new_v_D_skip', 'new_v_ssd_norm_w', 'new_v_w_out', 'new_v_final_norm_w']
TWIN_LEAF_KINDS = {'loss': 'loss', 'grad_x': 'grad_x', 'grad_norm_w': 'grad_w', 'grad_w_in': 'grad_w', 'grad_gate_b': 'grad_w', 'grad_sgu_norm_g': 'grad_w', 'grad_sgu_norm_b': 'grad_w', 'grad_sgu_w': 'grad_w', 'grad_sgu_b': 'grad_w', 'grad_conv_w': 'grad_w', 'grad_conv_b': 'grad_w', 'grad_dt_bias': 'grad_w', 'grad_A_log': 'grad_w', 'grad_D_skip': 'grad_w', 'grad_ssd_norm_w': 'grad_w', 'grad_w_out': 'grad_w', 'grad_final_norm_w': 'grad_w', 'delta_norm_w': 'delta_w', 'delta_w_in': 'delta_w', 'delta_gate_b': 'delta_w', 'delta_sgu_norm_g': 'delta_w', 'delta_sgu_norm_b': 'delta_w', 'delta_sgu_w': 'delta_w', 'delta_sgu_b': 'delta_w', 'delta_conv_w': 'delta_w', 'delta_conv_b': 'delta_w', 'delta_dt_bias': 'delta_w', 'delta_A_log': 'delta_w', 'delta_D_skip': 'delta_w', 'delta_ssd_norm_w': 'delta_w', 'delta_w_out': 'delta_w', 'delta_final_norm_w': 'delta_w', 'new_m_norm_w': 'new_m', 'new_m_w_in': 'new_m', 'new_m_gate_b': 'new_m', 'new_m_sgu_norm_g': 'new_m', 'new_m_sgu_norm_b': 'new_m', 'new_m_sgu_w': 'new_m', 'new_m_sgu_b': 'new_m', 'new_m_conv_w': 'new_m', 'new_m_conv_b': 'new_m', 'new_m_dt_bias': 'new_m', 'new_m_A_log': 'new_m', 'new_m_D_skip': 'new_m', 'new_m_ssd_norm_w': 'new_m', 'new_m_w_out': 'new_m', 'new_m_final_norm_w': 'new_m', 'new_v_norm_w': 'new_v', 'new_v_w_in': 'new_v', 'new_v_gate_b': 'new_v', 'new_v_sgu_norm_g': 'new_v', 'new_v_sgu_norm_b': 'new_v', 'new_v_sgu_w': 'new_v', 'new_v_sgu_b': 'new_v', 'new_v_conv_w': 'new_v', 'new_v_conv_b': 'new_v', 'new_v_dt_bias': 'new_v', 'new_v_A_log': 'new_v', 'new_v_D_skip': 'new_v', 'new_v_ssd_norm_w': 'new_v', 'new_v_w_out': 'new_v', 'new_v_final_norm_w': 'new_v'}


def _forward(args):
    return _fwd_reference(*[args[k] for k in FWD_PARAMS])


def _output_shape():
    def fwd():
        inp = _fwd_setup_inputs(0)
        return _fwd_reference(*[inp[k] for k in FWD_PARAMS])
    out = _jax.eval_shape(fwd)
    return out.shape, out.dtype

N_MICROBATCH = 1
ADAM_LR = 0.001
ADAM_B1 = 0.9
ADAM_B2 = 0.999
ADAM_EPS = 1e-08
ADAM_WD = 0.01
ADAM_STEP = 10
PER_EXAMPLE_BATCH_AXIS = {'x': 0, 'loss_target': 0}
SHARED_INPUTS = []
_WEIGHT_DTYPES = {'norm_w': _jnp.float32, 'w_in': _jnp.float32, 'gate_b': _jnp.float32, 'sgu_norm_g': _jnp.float32, 'sgu_norm_b': _jnp.float32, 'sgu_w': _jnp.float32, 'sgu_b': _jnp.float32, 'conv_w': _jnp.float32, 'conv_b': _jnp.float32, 'dt_bias': _jnp.float32, 'A_log': _jnp.float32, 'D_skip': _jnp.float32, 'ssd_norm_w': _jnp.float32, 'w_out': _jnp.float32, 'final_norm_w': _jnp.float32}
MOMENT_SCALE = {'norm_w': 1.151321e-01, 'w_in': 4.113750e-02, 'gate_b': 2.133321e-02, 'sgu_norm_g': 3.005907e-02, 'sgu_norm_b': 2.805533e-02, 'sgu_w': 2.868186e-02, 'sgu_b': 3.423964e-02, 'conv_w': 5.100377e-02, 'conv_b': 7.099041e-02, 'dt_bias': 2.155813e-01, 'A_log': 2.180861e-01, 'D_skip': 3.354107e-01, 'ssd_norm_w': 6.225429e-02, 'w_out': 7.146048e-02, 'final_norm_w': 3.194046e+01}


def _to_microbatches(a, axis):
    t = _jnp.moveaxis(a, axis, 0)
    t = t.reshape((N_MICROBATCH, t.shape[0] // N_MICROBATCH) + t.shape[1:])
    return _jnp.moveaxis(t, 1, axis + 1)


def setup_inputs(seed: int = 0) -> dict:
    inp = _fwd_setup_inputs(seed)
    key = _jax.random.fold_in(_jax.random.key(seed), 7919)
    shape, _ = _output_shape()
    out = dict(inp)
    out["loss_target"] = _jax.random.normal(_jax.random.fold_in(key, 0), shape, _jnp.float32)
    for i, name in enumerate(TWIN_WEIGHTS):
        w = inp[name].astype(_jnp.float32)
        if MOMENT_SCALE is None:
            s = _jnp.sqrt(_jnp.mean(_jnp.square(w)) + 1e-30)
        else:
            s = MOMENT_SCALE[name]
        km, kv = _jax.random.split(_jax.random.fold_in(key, i + 1))
        out[name] = w
        out["m_" + name] = s * _jax.random.normal(km, w.shape, _jnp.float32)
        out["v_" + name] = (s * s) * _jax.random.uniform(kv, w.shape, _jnp.float32, 0.5, 1.5)
    if N_MICROBATCH > 1:
        for name, axis in PER_EXAMPLE_BATCH_AXIS.items():
            out[name] = _to_microbatches(out[name], axis)
    return {'x': out['x'], 'norm_w': out['norm_w'], 'w_in': out['w_in'], 'gate_b': out['gate_b'], 'sgu_norm_g': out['sgu_norm_g'], 'sgu_norm_b': out['sgu_norm_b'], 'sgu_w': out['sgu_w'], 'sgu_b': out['sgu_b'], 'conv_w': out['conv_w'], 'conv_b': out['conv_b'], 'dt_bias': out['dt_bias'], 'A_log': out['A_log'], 'D_skip': out['D_skip'], 'ssd_norm_w': out['ssd_norm_w'], 'w_out': out['w_out'], 'final_norm_w': out['final_norm_w'], 'loss_target': out['loss_target'], 'm_norm_w': out['m_norm_w'], 'm_w_in': out['m_w_in'], 'm_gate_b': out['m_gate_b'], 'm_sgu_norm_g': out['m_sgu_norm_g'], 'm_sgu_norm_b': out['m_sgu_norm_b'], 'm_sgu_w': out['m_sgu_w'], 'm_sgu_b': out['m_sgu_b'], 'm_conv_w': out['m_conv_w'], 'm_conv_b': out['m_conv_b'], 'm_dt_bias': out['m_dt_bias'], 'm_A_log': out['m_A_log'], 'm_D_skip': out['m_D_skip'], 'm_ssd_norm_w': out['m_ssd_norm_w'], 'm_w_out': out['m_w_out'], 'm_final_norm_w': out['m_final_norm_w'], 'v_norm_w': out['v_norm_w'], 'v_w_in': out['v_w_in'], 'v_gate_b': out['v_gate_b'], 'v_sgu_norm_g': out['v_sgu_norm_g'], 'v_sgu_norm_b': out['v_sgu_norm_b'], 'v_sgu_w': out['v_sgu_w'], 'v_sgu_b': out['v_sgu_b'], 'v_conv_w': out['v_conv_w'], 'v_conv_b': out['v_conv_b'], 'v_dt_bias': out['v_dt_bias'], 'v_A_log': out['v_A_log'], 'v_D_skip': out['v_D_skip'], 'v_ssd_norm_w': out['v_ssd_norm_w'], 'v_w_out': out['v_w_out'], 'v_final_norm_w': out['v_final_norm_w']}


def _loss(weights, diff, rest, loss_target):
    with _jax.named_scope("forward"):
        args = {**rest, TWIN_DIFF_INPUT: diff, **{k: w.astype(_WEIGHT_DTYPES[k]) for k, w in weights.items()}}
        y = _forward(args)
    with _jax.named_scope("loss_head"):
        err = _jnp.square(y.astype(_jnp.float32) - loss_target)
        return 0.5 * _jnp.sum(_jnp.mean(err, axis=-1)) if err.ndim else 0.5 * err


def _adamw(w, g, m, v):
    m = ADAM_B1 * m + (1.0 - ADAM_B1) * g
    v = ADAM_B2 * v + (1.0 - ADAM_B2) * _jnp.square(g)
    m_hat = m / (1.0 - ADAM_B1 ** ADAM_STEP)
    v_hat = v / (1.0 - ADAM_B2 ** ADAM_STEP)
    delta = -ADAM_LR * (m_hat / (_jnp.sqrt(v_hat) + ADAM_EPS) + ADAM_WD * w)
    return delta, m, v


def reference(x, norm_w, w_in, gate_b, sgu_norm_g, sgu_norm_b, sgu_w, sgu_b, conv_w, conv_b, dt_bias, A_log, D_skip, ssd_norm_w, w_out, final_norm_w, loss_target, m_norm_w, m_w_in, m_gate_b, m_sgu_norm_g, m_sgu_norm_b, m_sgu_w, m_sgu_b, m_conv_w, m_conv_b, m_dt_bias, m_A_log, m_D_skip, m_ssd_norm_w, m_w_out, m_final_norm_w, v_norm_w, v_w_in, v_gate_b, v_sgu_norm_g, v_sgu_norm_b, v_sgu_w, v_sgu_b, v_conv_w, v_conv_b, v_dt_bias, v_A_log, v_D_skip, v_ssd_norm_w, v_w_out, v_final_norm_w):
    given = dict(x=x, norm_w=norm_w, w_in=w_in, gate_b=gate_b, sgu_norm_g=sgu_norm_g, sgu_norm_b=sgu_norm_b, sgu_w=sgu_w, sgu_b=sgu_b, conv_w=conv_w, conv_b=conv_b, dt_bias=dt_bias, A_log=A_log, D_skip=D_skip, ssd_norm_w=ssd_norm_w, w_out=w_out, final_norm_w=final_norm_w, loss_target=loss_target, m_norm_w=m_norm_w, m_w_in=m_w_in, m_gate_b=m_gate_b, m_sgu_norm_g=m_sgu_norm_g, m_sgu_norm_b=m_sgu_norm_b, m_sgu_w=m_sgu_w, m_sgu_b=m_sgu_b, m_conv_w=m_conv_w, m_conv_b=m_conv_b, m_dt_bias=m_dt_bias, m_A_log=m_A_log, m_D_skip=m_D_skip, m_ssd_norm_w=m_ssd_norm_w, m_w_out=m_w_out, m_final_norm_w=m_final_norm_w, v_norm_w=v_norm_w, v_w_in=v_w_in, v_gate_b=v_gate_b, v_sgu_norm_g=v_sgu_norm_g, v_sgu_norm_b=v_sgu_norm_b, v_sgu_w=v_sgu_w, v_sgu_b=v_sgu_b, v_conv_w=v_conv_w, v_conv_b=v_conv_b, v_dt_bias=v_dt_bias, v_A_log=v_A_log, v_D_skip=v_D_skip, v_ssd_norm_w=v_ssd_norm_w, v_w_out=v_w_out, v_final_norm_w=v_final_norm_w)
    weights = {n: given[n] for n in TWIN_WEIGHTS}
    shared = {n: given[n] for n in SHARED_INPUTS}
    per_example = {n: given[n] for n in ['x']}
    grad_fn = _jax.value_and_grad(_loss, argnums=(0, 1))

    def one_microbatch(ex, loss_target):
        ex = dict(ex)
        diff = ex.pop(TWIN_DIFF_INPUT)
        return grad_fn(weights, diff, {**shared, **ex}, loss_target)

    if N_MICROBATCH == 1:
        loss, (grad_w, grad_x) = one_microbatch(per_example, given["loss_target"])
    else:
        def body(carry, xs):
            loss_sum, grad_sum = carry
            l_k, (gw_k, gx_k) = one_microbatch(xs[0], xs[1])
            with _jax.named_scope("update"):
                return (loss_sum + l_k, _jax.tree.map(_jnp.add, grad_sum, gw_k)), gx_k

        init = (_jnp.zeros((), _jnp.float32), _jax.tree.map(_jnp.zeros_like, weights))
        (loss, grad_w), grad_x = _jax.lax.scan(body, init, (per_example, given["loss_target"]))
    with _jax.named_scope("update"):
        delta_w, new_m, new_v = {}, {}, {}
        for n in TWIN_WEIGHTS:
            delta_w[n], new_m[n], new_v[n] = _adamw(weights[n], grad_w[n], given["m_" + n], given["v_" + n])
    return (loss, grad_x, *[grad_w[n] for n in TWIN_WEIGHTS], *[delta_w[n] for n in TWIN_WEIGHTS],
            *[new_m[n] for n in TWIN_WEIGHTS], *[new_v[n] for n in TWIN_WEIGHTS])
```

```python
import functools

import jax
import jax.numpy as jnp
from jax import lax
from jax.experimental import pallas as pl
from jax.experimental.pallas import tpu as pltpu

F32 = jnp.float32
MXU_DTYPE = jnp.bfloat16

D_MODEL = 2048
EPS = 1e-5
CHUNK = 64
SGU_BLOCK = 128
SGU_GROUPS = 16
SSD_GROUPS = 4
SSD_GW = 512
SSD_STATE = 128
HEADDIM = 64
IN_W = 15392
SHARD_W = IN_W // 4
FW_B = 1792
BW_B = 1408
NA = 10240

ADAM_LR = 0.001
ADAM_B1 = 0.9
ADAM_B2 = 0.999
ADAM_EPS = 1e-08
ADAM_WD = 0.01
ADAM_STEP = 10

T_SSD = 256
T_TOK = 128
T_OUT = 256
T_ROW = 512
TM_MM = 1024
TK_DW = 1024
VMEM_CAP = 60 * 1024 * 1024
MESH = pl.DeviceIdType.MESH


def _cparams(sem, est_bytes):
    lim = int(min(VMEM_CAP, max(32 * 1024 * 1024, est_bytes + 12 * 1024 * 1024)))
    return pltpu.CompilerParams(dimension_semantics=sem, vmem_limit_bytes=lim)


def _c(x):
    return x.astype(MXU_DTYPE)


def _dot(a, b):
    return jnp.dot(a, b, preferred_element_type=F32)


def _dot_nt(a, b):
    return lax.dot_general(a, b, (((1,), (1,)), ((), ())), preferred_element_type=F32)


def _dot_tn(a, b):
    return lax.dot_general(a, b, (((0,), (0,)), ((), ())), preferred_element_type=F32)


def _split(x, n):
    parts, r = [], x
    for _ in range(n):
        p = _c(r)
        parts.append(p)
        r = r - p.astype(F32)
    return parts


def _dot01_l(m01, x, n):
    acc = None
    for p in _split(x, n):
        t = _dot(m01, p)
        acc = t if acc is None else acc + t
    return acc


def _dot01_r(x, m01, n):
    acc = None
    for p in _split(x, n):
        t = _dot(p, m01)
        acc = t if acc is None else acc + t
    return acc


def _sigmoid(x):
    return 1.0 / (1.0 + jnp.exp(-x))


def _fold8(x):
    r, w = x.shape
    return jnp.sum(x.reshape(r // 8, 8, w), axis=0)


def _iota(shape, dim):
    return lax.broadcasted_iota(jnp.int32, shape, dim)


def _ssd_masks():
    l64 = _iota((CHUNK, SSD_GW), 0)
    s64 = jnp.bitwise_and(_iota((CHUNK, SSD_GW), 1), CHUNK - 1)
    diag = l64 == s64
    causal = l64 >= s64
    row_last = l64 == CHUNK - 1
    r4 = lax.shift_right_logical(_iota((256, 256), 0), 6)
    c4 = lax.shift_right_logical(_iota((256, 256), 1), 6)
    mask4 = r4 == c4
    tl = _iota((CHUNK, CHUNK), 0)
    tc = _iota((CHUNK, CHUNK), 1)
    tri = _c(jnp.where(tc <= tl, 1.0, 0.0))
    trit = _c(jnp.where(tc >= tl, 1.0, 0.0))
    return diag, causal, row_last, mask4, tri, trit


def _ssd_common(xs, bm, cm, dt, a_neg, masks):
    diag, causal, row_last, mask4, tri, trit = masks
    a = dt * a_neg
    acs = _dot01_l(tri, a, 3)
    row_e = jnp.sum(jnp.where(diag, acs, 0.0), axis=0, keepdims=True)
    seg = acs - row_e
    lm = jnp.exp(jnp.where(causal, seg, -1e30))
    bb, cb = _c(bm), _c(cm)
    brep = jnp.concatenate([bb] * 8, axis=0)
    cbrep = _dot_nt(cb, brep)
    m = cbrep * lm
    xdt = xs * dt
    acs_last = jnp.sum(jnp.where(row_last, acs, 0.0), axis=0, keepdims=True)
    dec = jnp.exp(acs_last - acs)
    eacs = jnp.exp(acs)
    cd = jnp.exp(acs_last)
    return dict(lm=lm, bb=bb, cb=cb, brep=brep, m=m, xdt=xdt, dec=dec, eacs=eacs, cd=cd)


def _blockdiag4(xb, mask4):
    return jnp.where(mask4, jnp.concatenate([xb] * 4, axis=0), jnp.zeros((), xb.dtype))


def _ssd_chunk_fwd(xs, bm, cm, dt, a_neg, d_skip, ht, masks):
    q = _ssd_common(xs, bm, cm, dt, a_neg, masks)
    mask4 = masks[3]
    mb, xdtb = _c(q["m"]), _c(q["xdt"])
    yd = []
    for blk in range(2):
        sl = slice(256 * blk, 256 * blk + 256)
        yd.append(_dot(mb[:, sl], _blockdiag4(xdtb[:, sl], mask4)))
    y_diag = jnp.concatenate(yd, axis=1)
    p = _dot(q["cb"], _c(ht))
    y = y_diag + p * q["eacs"] + xs * d_skip
    st = _dot_tn(q["bb"], _c(q["xdt"] * q["dec"]))
    return y, ht * q["cd"] + st


def _ssd_chunk_bwd(xs, bm, cm, dt, sig, a_neg, d_skip, hprev, dht, dy, masks):
    diag, causal, row_last, mask4, tri, trit = masks
    q = _ssd_common(xs, bm, cm, dt, a_neg, masks)
    lm, bb, cb, brep, m, xdt, dec, eacs, cd = (q[k] for k in ("lm", "bb", "cb", "brep", "m", "xdt", "dec", "eacs", "cd"))
    hb = _c(hprev)
    yoff = _dot(cb, hb) * eacs
    dyb = _c(dy)
    dpb = _c(dy * eacs)
    d_c = _dot_nt(dpb, hb)
    dh_y = _dot_tn(cb, dpb)
    mb, xdtb = _c(m), _c(xdt)
    dm_parts, dxdt_parts = [], []
    for blk in range(2):
        sl = slice(256 * blk, 256 * blk + 256)
        bd = _blockdiag4(xdtb[:, sl], mask4)
        dm_parts.append(_dot_nt(dyb[:, sl], bd))
        dxf = jnp.where(mask4, _dot_tn(mb[:, sl], dyb[:, sl]), 0.0)
        dxdt_parts.append(dxf[0:64] + dxf[64:128] + dxf[128:192] + dxf[192:256])
    dm = jnp.concatenate(dm_parts, axis=1)
    dxdt = jnp.concatenate(dxdt_parts, axis=1)
    dcbb = _c(dm * lm)
    g = dm * m
    d_c = d_c + _dot(dcbb, brep)
    dbrep = _dot_tn(dcbb, cb)
    d_b = dbrep[0:64]
    for r in range(1, 8):
        d_b = d_b + dbrep[64 * r:64 * r + 64]
    dhtb = _c(dht)
    dxd = _dot(bb, dhtb)
    xd = xdt * dec
    dxdt = dxdt + dxd * dec
    tq = dxd * xd
    d_b = d_b + _dot_nt(_c(xd), dhtb)
    dcd = jnp.sum(dht * hprev, axis=0, keepdims=True)
    col_g = jnp.sum(g, axis=0, keepdims=True)
    last = jnp.sum(tq, axis=0, keepdims=True) + dcd * cd
    qq = g - jnp.where(diag, col_g, 0.0) + dy * yoff - tq + jnp.where(row_last, last, 0.0)
    bd4 = _c(jnp.where(mask4, 1.0, 0.0))
    dacs = jnp.concatenate([_dot01_r(qq[:, 256 * b:256 * b + 256], bd4, 2) for b in range(2)], axis=1)
    da = _dot01_l(trit, dacs, 2)
    ddt = dxdt * xs + da * (a_neg * (1.0 / HEADDIM))
    dxs = dxdt * dt + dy * d_skip
    return dxs, d_b, d_c, ddt * sig, dht * cd + dh_y, da * dt, dy * xs


def _softplus(x):
    return jnp.maximum(x, 0.0) + jnp.log1p(jnp.exp(-jnp.abs(x)))


def _conv_taps(xpad, t):
    taps = []
    for k in range(4):
        sh = 3 - k
        v = xpad if sh == 0 else pltpu.roll(xpad, sh, 0)
        taps.append(v[8:8 + t])
    return taps


def _norm_call(x, norm_w, tm):
    s = x.shape[0]

    def body(x_ref, w_ref, xn_ref, xnt_ref):
        xv = x_ref[...]
        r = lax.rsqrt(jnp.mean(xv * xv, axis=-1, keepdims=True) + EPS)
        xn = xv * r * w_ref[...]
        xn_ref[...] = _c(xn)
        xnt_ref[...] = _c(xn.T)

    return pl.pallas_call(
        body, name="rmsnorm_in",
        grid=(s // tm,),
        in_specs=[pl.BlockSpec((tm, D_MODEL), lambda i: (i, 0)), pl.BlockSpec((1, D_MODEL), lambda i: (0, 0))],
        out_specs=[pl.BlockSpec((tm, D_MODEL), lambda i: (i, 0)), pl.BlockSpec((D_MODEL, tm), lambda i: (0, i))],
        out_shape=[jax.ShapeDtypeStruct((s, D_MODEL), MXU_DTYPE), jax.ShapeDtypeStruct((D_MODEL, s), MXU_DTYPE)],
        compiler_params=_cparams(("parallel",), 2 * tm * D_MODEL * 12),
    )(x, norm_w)


def _mm(a, b, *, tm, tn, tk, name):
    m, k = a.shape
    n = b.shape[1]
    nk = k // tk
    assert m % tm == 0 and n % tn == 0 and k % tk == 0, (a.shape, b.shape, tm, tn, tk)

    def body(a_ref, b_ref, o_ref):
        if nk == 1:
            o_ref[...] = _dot(a_ref[...], b_ref[...])
        else:
            @pl.when(pl.program_id(2) == 0)
            def _():
                o_ref[...] = jnp.zeros_like(o_ref)

            o_ref[...] += _dot(a_ref[...], b_ref[...])

    isz = jnp.dtype(a.dtype).itemsize
    est = 2 * (tm * tk + tk * tn) * isz + 2 * tm * tn * 4
    return pl.pallas_call(
        body, name=name,
        grid=(m // tm, n // tn, nk),
        in_specs=[pl.BlockSpec((tm, tk), lambda i, j, kk: (i, kk)), pl.BlockSpec((tk, tn), lambda i, j, kk: (kk, j))],
        out_specs=pl.BlockSpec((tm, tn), lambda i, j, kk: (i, j)),
        out_shape=jax.ShapeDtypeStruct((m, n), F32),
        compiler_params=_cparams(("parallel", "parallel", "arbitrary"), est),
    )(a, b)


def _dx_mm(dpa, wta, dpb, wtb, *, tm):
    s = dpa.shape[0]
    tka, tkb = 1024, BW_B
    nka, nkb = dpa.shape[1] // tka, dpb.shape[1] // tkb

    def body(a_ref, wa_ref, b_ref, wb_ref, o_ref):
        kk = pl.program_id(1)

        @pl.when(kk == 0)
        def _():
            o_ref[...] = jnp.zeros_like(o_ref)

        @pl.when(kk < nka)
        def _():
            o_ref[...] += _dot(a_ref[...], wa_ref[...])

        @pl.when(kk >= nka)
        def _():
            o_ref[...] += _dot(b_ref[...], wb_ref[...])

    isz = jnp.dtype(dpa.dtype).itemsize
    est = 2 * isz * (tm * tka + tka * D_MODEL + tm * tkb + tkb * D_MODEL) + 2 * tm * D_MODEL * 4
    return pl.pallas_call(
        body, name="dx_matmul",
        grid=(s // tm, nka + nkb),
        in_specs=[
            pl.BlockSpec((tm, tka), lambda i, kk: (i, jnp.minimum(kk, nka - 1))),
            pl.BlockSpec((tka, D_MODEL), lambda i, kk: (jnp.minimum(kk, nka - 1), 0)),
            pl.BlockSpec((tm, tkb), lambda i, kk: (i, jnp.maximum(kk - nka, 0))),
            pl.BlockSpec((tkb, D_MODEL), lambda i, kk: (jnp.maximum(kk - nka, 0), 0)),
        ],
        out_specs=pl.BlockSpec((tm, D_MODEL), lambda i, kk: (i, 0)),
        out_shape=jax.ShapeDtypeStruct((s, D_MODEL), F32),
        compiler_params=_cparams(("parallel", "arbitrary"), est),
    )(dpa, wta, dpb, wtb)


def _gradx_call(x, dxn, dh, norm_w, tm):
    s = x.shape[0]

    def body(x_ref, g_ref, dh_ref, w_ref, gx_ref, dw_ref):
        @pl.when(pl.program_id(0) == 0)
        def _():
            dw_ref[...] = jnp.zeros_like(dw_ref)

        xv, gv = x_ref[...], g_ref[...]
        r = lax.rsqrt(jnp.mean(xv * xv, axis=-1, keepdims=True) + EPS)
        gw = gv * w_ref[...]
        gx_ref[...] = r * gw - xv * (r * r * r) * jnp.mean(xv * gw, axis=-1, keepdims=True) + dh_ref[...]
        dw_ref[...] += _fold8(gv * (xv * r))

    row = pl.BlockSpec((tm, D_MODEL), lambda i: (i, 0))
    return pl.pallas_call(
        body, name="grad_x",
        grid=(s // tm,),
        in_specs=[row, row, row, pl.BlockSpec((1, D_MODEL), lambda i: (0, 0))],
        out_specs=[row, pl.BlockSpec((8, D_MODEL), lambda i: (0, 0))],
        out_shape=[jax.ShapeDtypeStruct((s, D_MODEL), F32), jax.ShapeDtypeStruct((8, D_MODEL), F32)],
        compiler_params=_cparams(("arbitrary",), 2 * tm * D_MODEL * 16),
    )(x, dxn, dh, norm_w)


def _layernorm_stats(v):
    mu = jnp.mean(v, axis=-1, keepdims=True)
    vc = v - mu
    var = jnp.mean(vc * vc, axis=-1, keepdims=True)
    return vc * lax.rsqrt(var + EPS), lax.rsqrt(var + EPS)


def _tok_fwd_call(proj_a, y_b, gate_b, sgu_g, sgu_beta, wm, bias_full, t):
    s = proj_a.shape[0]

    def body(pa_ref, yb_ref, gb_ref, g_ref, be_ref, wm_ref, bf_ref, ya_ref, mg_ref, mgt_ref, mix_ref):
        u = pa_ref[:, 0:2048]
        v = pa_ref[:, 2048:4096]
        za = pa_ref[:, 4096:6144]
        xhat, _ = _layernorm_stats(v)
        vnb = _c(xhat * g_ref[...] + be_ref[...])
        for gi in range(SGU_GROUPS):
            sl = slice(128 * gi, 128 * gi + 128)
            mix_ref[:, sl] = _dot(wm_ref[gi], vnb[:, sl])
        mixed = mix_ref[...] + bf_ref[...]
        y_a = u * mixed * (za * _sigmoid(za))
        g0 = _sigmoid(pa_ref[:, 6144:8192] + gb_ref[:, 0:2048])
        g1 = _sigmoid(pa_ref[:, 8192:10240] + gb_ref[:, 2048:4096])
        merged = g0 * y_a + g1 * yb_ref[...]
        ya_ref[...] = y_a
        mg_ref[...] = _c(merged)
        mgt_ref[...] = _c(merged.T)

    row = pl.BlockSpec((t, D_MODEL), lambda i: (i, 0))
    vec = lambda w: pl.BlockSpec((1, w), lambda i: (0, 0))
    return pl.pallas_call(
        body, name="tok_fwd",
        grid=(s // t,),
        in_specs=[pl.BlockSpec((t, NA), lambda i: (i, 0)), row, vec(4096), vec(2048), vec(2048),
                  pl.BlockSpec((SGU_GROUPS, 128, 128), lambda i: (0, 0, 0)), pl.BlockSpec((128, D_MODEL), lambda i: (0, 0))],
        out_specs=[row, row, pl.BlockSpec((D_MODEL, t), lambda i: (0, i))],
        out_shape=[jax.ShapeDtypeStruct((s, D_MODEL), F32), jax.ShapeDtypeStruct((s, D_MODEL), MXU_DTYPE),
                   jax.ShapeDtypeStruct((D_MODEL, s), MXU_DTYPE)],
        scratch_shapes=[pltpu.VMEM((t, D_MODEL), F32)],
        compiler_params=_cparams(("parallel",), 2 * t * NA * 4 + 12 * t * D_MODEL * 4),
    )(proj_a, y_b, gate_b, sgu_g, sgu_beta, wm, bias_full)


def _tok_bwd_call(proj_a, dmerged, y_a, y_b, gate_b, sgu_g, sgu_beta, wm, wmt, bias_full, t):
    s = proj_a.shape[0]

    def body(pa_ref, dm_ref, ya_ref, yb_ref, gb_ref, g_ref, be_ref, wm_ref, wmt_ref, bf_ref,
             dpa_ref, dyb_ref, dgb_ref, dgam_ref, dbeta_ref, dbf_ref, dws_ref, mix_ref, dvn_ref):
        @pl.when(pl.program_id(0) == 0)
        def _():
            dgb_ref[...] = jnp.zeros_like(dgb_ref)
            dgam_ref[...] = jnp.zeros_like(dgam_ref)
            dbeta_ref[...] = jnp.zeros_like(dbeta_ref)
            dbf_ref[...] = jnp.zeros_like(dbf_ref)
            dws_ref[...] = jnp.zeros_like(dws_ref)

        u = pa_ref[:, 0:2048]
        v = pa_ref[:, 2048:4096]
        za = pa_ref[:, 4096:6144]
        xhat, rstd = _layernorm_stats(v)
        vnb = _c(xhat * g_ref[...] + be_ref[...])
        for gi in range(SGU_GROUPS):
            sl = slice(128 * gi, 128 * gi + 128)
            mix_ref[:, sl] = _dot(wm_ref[gi], vnb[:, sl])
        mixed = mix_ref[...] + bf_ref[...]
        sig = _sigmoid(za)
        sz = za * sig
        dm = dm_ref[...]
        y_a = ya_ref[...]
        g0 = _sigmoid(pa_ref[:, 6144:8192] + gb_ref[:, 0:2048])
        g1 = _sigmoid(pa_ref[:, 8192:10240] + gb_ref[:, 2048:4096])
        dgl0 = dm * y_a * g0 * (1.0 - g0)
        dgl1 = dm * yb_ref[...] * g1 * (1.0 - g1)
        dyb_ref[...] = dm * g1
        dya = dm * g0
        dpa_ref[:, 6144:8192] = _c(dgl0)
        dpa_ref[:, 8192:10240] = _c(dgl1)
        dgb_ref[:, 0:2048] += _fold8(dgl0)
        dgb_ref[:, 2048:4096] += _fold8(dgl1)
        dpa_ref[:, 0:2048] = _c(dya * mixed * sz)
        dpa_ref[:, 4096:6144] = _c(dya * (u * mixed) * (sig * (1.0 + za * (1.0 - sig))))
        dmixed = dya * u * sz
        dbf_ref[...] += dmixed
        dmb = _c(dmixed)
        for gi in range(SGU_GROUPS):
            sl = slice(128 * gi, 128 * gi + 128)
            dvn_ref[:, sl] = _dot(wmt_ref[gi], dmb[:, sl])
            dws_ref[gi] += _dot_nt(dmb[:, sl], vnb[:, sl])
        dvn = dvn_ref[...]
        dgam_ref[...] += _fold8(dvn * xhat)
        dbeta_ref[...] += _fold8(dvn)
        dxh = dvn * g_ref[...]
        dv = rstd * (dxh - jnp.mean(dxh, axis=-1, keepdims=True) - xhat * jnp.mean(dxh * xhat, axis=-1, keepdims=True))
        dpa_ref[:, 2048:4096] = _c(dv)

    row = pl.BlockSpec((t, D_MODEL), lambda i: (i, 0))
    vec = lambda w: pl.BlockSpec((1, w), lambda i: (0, 0))
    acc = lambda w: pl.BlockSpec((8, w), lambda i: (0, 0))
    wspec = pl.BlockSpec((SGU_GROUPS, 128, 128), lambda i: (0, 0, 0))
    return pl.pallas_call(
        body, name="tok_bwd",
        grid=(s // t,),
        in_specs=[pl.BlockSpec((t, NA), lambda i: (i, 0)), row, row, row, vec(4096), vec(2048), vec(2048),
                  wspec, wspec, pl.BlockSpec((128, D_MODEL), lambda i: (0, 0))],
        out_specs=[pl.BlockSpec((t, NA), lambda i: (i, 0)), row, acc(4096), acc(2048), acc(2048),
                   pl.BlockSpec((128, D_MODEL), lambda i: (0, 0)), wspec],
        out_shape=[jax.ShapeDtypeStruct((s, NA), MXU_DTYPE), jax.ShapeDtypeStruct((s, D_MODEL), F32),
                   jax.ShapeDtypeStruct((8, 4096), F32), jax.ShapeDtypeStruct((8, 2048), F32),
                   jax.ShapeDtypeStruct((8, 2048), F32), jax.ShapeDtypeStruct((128, D_MODEL), F32),
                   jax.ShapeDtypeStruct((SGU_GROUPS, 128, 128), F32)],
        scratch_shapes=[pltpu.VMEM((t, D_MODEL), F32), pltpu.VMEM((t, D_MODEL), F32)],
        compiler_params=_cparams(("arbitrary",), 2 * t * NA * 6 + 16 * t * D_MODEL * 4),
    )(proj_a, dmerged, y_a, y_b, gate_b, sgu_g, sgu_beta, wm, wmt, bias_full)


def _out_call(merged, x, target, w_out, fnw, t):
    s = x.shape[0]
    nt = s // t

    def body(mg_ref, x_ref, t_ref, w_ref, fw_ref, dh_ref, dhb_ref, dmg_ref, loss_ref, dfw_ref):
        @pl.when(pl.program_id(0) == 0)
        def _():
            dfw_ref[...] = jnp.zeros_like(dfw_ref)

        h = x_ref[...] + _dot(mg_ref[...], w_ref[...])
        r = lax.rsqrt(jnp.mean(h * h, axis=-1, keepdims=True) + EPS)
        hn = h * r
        err = hn * fw_ref[...] - t_ref[...]
        loss_ref[...] = jnp.full(loss_ref.shape, 0.5 * jnp.sum(jnp.mean(err * err, axis=-1, keepdims=True)), F32)
        dy = err * (1.0 / D_MODEL)
        dfw_ref[...] += _fold8(dy * hn)
        gw = dy * fw_ref[...]
        dh = r * gw - h * (r * r * r) * jnp.mean(h * gw, axis=-1, keepdims=True)
        dh_ref[...] = dh
        dhb = _c(dh)
        dhb_ref[...] = dhb
        dmg_ref[...] = _dot_nt(dhb, w_ref[...])

    row = pl.BlockSpec((t, D_MODEL), lambda i: (i, 0))
    return pl.pallas_call(
        body, name="out_proj_loss",
        grid=(nt,),
        in_specs=[row, row, row, pl.BlockSpec((D_MODEL, D_MODEL), lambda i: (0, 0)), pl.BlockSpec((1, D_MODEL), lambda i: (0, 0))],
        out_specs=[row, row, row, pl.BlockSpec((1, 8, 128), lambda i: (i, 0, 0)), pl.BlockSpec((8, D_MODEL), lambda i: (0, 0))],
        out_shape=[jax.ShapeDtypeStruct((s, D_MODEL), F32), jax.ShapeDtypeStruct((s, D_MODEL), MXU_DTYPE),
                   jax.ShapeDtypeStruct((s, D_MODEL), F32), jax.ShapeDtypeStruct((nt, 8, 128), F32),
                   jax.ShapeDtypeStruct((8, D_MODEL), F32)],
        compiler_params=_cparams(("arbitrary",), 2 * D_MODEL * D_MODEL * 2 + 2 * t * D_MODEL * 24),
    )(merged, x, target, w_out, fnw)


def _ssd_fwd_call(proj_b, dtb, alog, dsk, cw, cb, nw, t):
    s = proj_b.shape[0]
    nt, nch = s // t, t // CHUNK

    def body(pb_ref, halo_ref, dtb_ref, al_ref, ds_ref, cw_ref, cb_ref, nw_ref, y_ref, yb_ref, hp_ref, ht_ref, act_ref):
        i = pl.program_id(1)

        @pl.when(i == 0)
        def _():
            ht_ref[...] = jnp.zeros_like(ht_ref)

        halo = jnp.where(i == 0, 0.0, halo_ref[:, 0:768])
        taps = _conv_taps(jnp.concatenate([halo, pb_ref[:, 0:768]], axis=0), t)
        pre = cb_ref[...]
        for k in range(4):
            pre = pre + taps[k] * cw_ref[k:k + 1, :]
        act_ref[...] = pre * _sigmoid(pre)
        masks = _ssd_masks()
        a_neg = -jnp.exp(al_ref[...])

        def chunk(c, carry):
            rows = pl.ds(pl.multiple_of(c * CHUNK, CHUNK), CHUNK)
            dt = _softplus(pb_ref[rows, 768:1280] + dtb_ref[...])
            ht = ht_ref[...]
            hp_ref[c] = ht
            y, ht_new = _ssd_chunk_fwd(act_ref[rows, 0:512], act_ref[rows, 512:640], act_ref[rows, 640:768],
                                       dt, a_neg, ds_ref[...], ht, masks)
            y_ref[rows, :] = y
            ht_ref[...] = ht_new
            return carry

        lax.fori_loop(0, nch, chunk, 0)
        zb = pb_ref[:, 1280:1792]
        hh = y_ref[...] * (zb * _sigmoid(zb))
        rr = lax.rsqrt(jnp.mean(hh * hh, axis=-1, keepdims=True) + EPS)
        yb_ref[...] = hh * rr * nw_ref[...]

    gvec = lambda w: pl.BlockSpec((1, w), lambda g, i: (0, g))
    return pl.pallas_call(
        body, name="ssd_fwd",
        grid=(SSD_GROUPS, nt),
        in_specs=[pl.BlockSpec((t, FW_B), lambda g, i: (i, g)),
                  pl.BlockSpec((8, FW_B), lambda g, i: (jnp.maximum(i * (t // 8) - 1, 0), g)),
                  gvec(512), gvec(512), gvec(512),
                  pl.BlockSpec((4, 768), lambda g, i: (0, g)), gvec(768), gvec(512)],
        out_specs=[pl.BlockSpec((t, SSD_GW), lambda g, i: (i, g)), pl.BlockSpec((t, SSD_GW), lambda g, i: (i, g)),
                   pl.BlockSpec((nch, SSD_STATE, SSD_GW), lambda g, i: (i, 0, g))],
        out_shape=[jax.ShapeDtypeStruct((s, D_MODEL), F32), jax.ShapeDtypeStruct((s, D_MODEL), F32),
                   jax.ShapeDtypeStruct((s // CHUNK, SSD_STATE, D_MODEL), F32)],
        scratch_shapes=[pltpu.VMEM((SSD_STATE, SSD_GW), F32), pltpu.VMEM((t, 768), F32)],
        compiler_params=_cparams(("parallel", "arbitrary"), 2 * t * FW_B * 4 + 6 * t * SSD_GW * 4 + 24 * 1024 * 1024),
    )(proj_b, proj_b, dtb, alog, dsk, cw, cb, nw)


def _ssd_bwd_call(proj_b, dyb, y, hprev, dtb, alog, dsk, cw, cb, nw, t):
    s = proj_b.shape[0]
    nt, nch = s // t, t // CHUNK

    def body(pb_ref, halo_ref, dyb_ref, y_ref, hp_ref, dtb_ref, al_ref, ds_ref, cw_ref, cb_ref, nw_ref,
             dpb_ref, a512_ref, a768_ref, dht_ref, act_ref, pre_ref, dact_ref, dy_ref, nxt_ref):
        i = pl.program_id(1)
        tile = nt - 1 - i

        @pl.when(i == 0)
        def _():
            dht_ref[...] = jnp.zeros_like(dht_ref)
            nxt_ref[...] = jnp.zeros_like(nxt_ref)
            a512_ref[...] = jnp.zeros_like(a512_ref)
            a768_ref[...] = jnp.zeros_like(a768_ref)

        halo = jnp.where(tile == 0, 0.0, halo_ref[:, 0:768])
        taps = _conv_taps(jnp.concatenate([halo, pb_ref[:, 0:768]], axis=0), t)
        pre = cb_ref[...]
        for k in range(4):
            pre = pre + taps[k] * cw_ref[k:k + 1, :]
        pre_ref[...] = pre
        act_ref[...] = pre * _sigmoid(pre)

        zb = pb_ref[:, 1280:1792]
        yv = y_ref[...]
        sgz = _sigmoid(zb)
        sz = zb * sgz
        hh = yv * sz
        rr = lax.rsqrt(jnp.mean(hh * hh, axis=-1, keepdims=True) + EPS)
        dyb = dyb_ref[...]
        a512_ref[0, 0] += _fold8(dyb * (hh * rr))
        tt = dyb * nw_ref[...]
        dhh = rr * tt - hh * (rr * rr * rr) * jnp.mean(hh * tt, axis=-1, keepdims=True)
        dy_ref[...] = dhh * sz
        dpb_ref[:, 0:512] = _c(dhh * yv * (sgz * (1.0 + zb * (1.0 - sgz))))

        masks = _ssd_masks()
        a_neg = -jnp.exp(al_ref[...])
        rsel = _c(jnp.where(lax.shift_right_logical(_iota((SSD_GW, 128), 0), 6) == _iota((SSD_GW, 128), 1), 1.0, 0.0))

        def chunk(cc, carry):
            c = nch - 1 - cc
            rows = pl.ds(pl.multiple_of(c * CHUNK, CHUNK), CHUNK)
            z = pb_ref[rows, 768:1280] + dtb_ref[...]
            xs = act_ref[rows, 0:512]
            dxs, d_b, d_c, ddtr, dht_prev, dadt, dyxs = _ssd_chunk_bwd(
                xs, act_ref[rows, 512:640], act_ref[rows, 640:768], _softplus(z), _sigmoid(z),
                a_neg, ds_ref[...], hp_ref[c], dht_ref[...], dy_ref[rows, :], masks)
            dht_ref[...] = dht_prev
            dact_ref[rows, 0:512] = dxs
            dact_ref[rows, 512:640] = d_b
            dact_ref[rows, 640:768] = d_c
            dpb_ref[rows, 1280:1408] = _c(_dot01_r(ddtr, rsel, 2))
            a512_ref[0, 1] += _fold8(dyxs)
            a512_ref[0, 2] += _fold8(dadt)
            a512_ref[0, 3] += _fold8(ddtr)
            return carry

        lax.fori_loop(0, nch, chunk, 0)

        pre = pre_ref[...]
        sp = _sigmoid(pre)
        dpre = dact_ref[...] * (sp * (1.0 + pre * (1.0 - sp)))
        a768_ref[0, 4] += _fold8(dpre)
        for k in range(4):
            a768_ref[0, k] += _fold8(dpre * taps[k])
        dpad = jnp.concatenate([dpre, nxt_ref[...]], axis=0)
        dx = dpre * cw_ref[3:4, :]
        for k in range(3):
            dx = dx + pltpu.roll(dpad, t + 8 - (3 - k), 0)[0:t] * cw_ref[k:k + 1, :]
        nxt_ref[...] = dpre[0:8]
        dpb_ref[:, 512:1280] = _c(dx)

    gvec = lambda w: pl.BlockSpec((1, w), lambda g, i: (0, g))
    rev = lambda w: pl.BlockSpec((t, w), lambda g, i: (nt - 1 - i, g))
    return pl.pallas_call(
        body, name="ssd_bwd",
        grid=(SSD_GROUPS, nt),
        in_specs=[rev(FW_B),
                  pl.BlockSpec((8, FW_B), lambda g, i: (jnp.maximum((nt - 1 - i) * (t // 8) - 1, 0), g)),
                  rev(SSD_GW), rev(SSD_GW),
                  pl.BlockSpec((nch, SSD_STATE, SSD_GW), lambda g, i: (nt - 1 - i, 0, g)),
                  gvec(512), gvec(512), gvec(512),
                  pl.BlockSpec((4, 768), lambda g, i: (0, g)), gvec(768), gvec(512)],
        out_specs=[rev(BW_B),
                   pl.BlockSpec((1, 4, 8, 512), lambda g, i: (g, 0, 0, 0)),
                   pl.BlockSpec((1, 5, 8, 768), lambda g, i: (g, 0, 0, 0))],
        out_shape=[jax.ShapeDtypeStruct((s, SSD_GROUPS * BW_B), MXU_DTYPE),
                   jax.ShapeDtypeStruct((SSD_GROUPS, 4, 8, 512), F32),
                   jax.ShapeDtypeStruct((SSD_GROUPS, 5, 8, 768), F32)],
        scratch_shapes=[pltpu.VMEM((SSD_STATE, SSD_GW), F32), pltpu.VMEM((t, 768), F32), pltpu.VMEM((t, 768), F32),
                        pltpu.VMEM((t, 768), F32), pltpu.VMEM((t, SSD_GW), F32), pltpu.VMEM((8, 768), F32)],
        compiler_params=_cparams(("parallel", "arbitrary"), 2 * t * FW_B * 4 + 10 * t * SSD_GW * 4 + 28 * 1024 * 1024),
    )(proj_b, proj_b, dyb, y, hprev, dtb, alog, dsk, cw, cb, nw)


def _rows_call(body, ins, outs, tr, name):
    r = ins[0].shape[0]
    spec = lambda a: pl.BlockSpec((tr, a.shape[1]), lambda i: (i, 0))
    est = 2 * tr * sum(a.shape[1] * jnp.dtype(a.dtype).itemsize for a in list(ins) + list(outs))
    return pl.pallas_call(
        body, name=name, grid=(r // tr,),
        in_specs=[spec(a) for a in ins], out_specs=[spec(o) for o in outs], out_shape=list(outs),
        compiler_params=_cparams(("parallel",), est),
    )(*ins)


def _add_pair(a, b, tr, name, with_bf16):
    def body(a_ref, b_ref, *o_refs):
        v = a_ref[...] + b_ref[...]
        o_refs[0][...] = v
        if with_bf16:
            o_refs[1][...] = v.astype(jnp.bfloat16)

    outs = [jax.ShapeDtypeStruct(a.shape, F32)]
    if with_bf16:
        outs.append(jax.ShapeDtypeStruct(a.shape, jnp.bfloat16))
    return _rows_call(body, [a, b], outs, tr, name)


def _sum_own_recv(own, recv, tr, name):
    r, c = own.shape

    def body(o_ref, r_ref, out_ref):
        v = o_ref[...]
        for j in range(3):
            v = v + r_ref[j].astype(F32)
        out_ref[...] = v

    return pl.pallas_call(
        body, name=name, grid=(r // tr,),
        in_specs=[pl.BlockSpec((tr, c), lambda i: (i, 0)), pl.BlockSpec((3, tr, c), lambda i: (0, i, 0))],
        out_specs=pl.BlockSpec((tr, c), lambda i: (i, 0)),
        out_shape=jax.ShapeDtypeStruct((r, c), F32),
        compiler_params=_cparams(("parallel",), 2 * tr * c * 14),
    )(own, recv)


def _sum_chips(abs4, tr, name):
    r = abs4.shape[1]

    def body(a_ref, out_ref):
        out_ref[...] = ((a_ref[0] + a_ref[1]) + a_ref[2]) + a_ref[3]

    return pl.pallas_call(
        body, name=name, grid=(r // tr,),
        in_specs=[pl.BlockSpec((4, tr, 128), lambda i: (0, i, 0))],
        out_specs=pl.BlockSpec((tr, 128), lambda i: (i, 0)),
        out_shape=jax.ShapeDtypeStruct((r, 128), F32),
        compiler_params=_cparams(("parallel",), 2 * tr * 128 * 20),
    )(abs4)


def _adamw(w, g, m, v, tr, name):
    def body(w_ref, g_ref, m_ref, v_ref, d_ref, nm_ref, nv_ref):
        gv = g_ref[...]
        nm = ADAM_B1 * m_ref[...] + (1.0 - ADAM_B1) * gv
        nv = ADAM_B2 * v_ref[...] + (1.0 - ADAM_B2) * (gv * gv)
        m_hat = nm / (1.0 - ADAM_B1 ** ADAM_STEP)
        v_hat = nv / (1.0 - ADAM_B2 ** ADAM_STEP)
        d_ref[...] = -ADAM_LR * (m_hat / (jnp.sqrt(v_hat) + ADAM_EPS) + ADAM_WD * w_ref[...])
        nm_ref[...] = nm
        nv_ref[...] = nv

    o = jax.ShapeDtypeStruct(w.shape, F32)
    return _rows_call(body, [w, g, m, v], [o, o, o], tr, name)


ANY = pl.BlockSpec(memory_space=pl.ANY)


def _place():
    x, y, c = lax.axis_index("x"), lax.axis_index("y"), lax.axis_index("c")
    others = [(1 - x, y), (x, 1 - y), (1 - x, 1 - y)]
    return x, y, c, 2 * x + y, others


def _remote(src, dst, send, recv, k, to):
    return pltpu.make_async_remote_copy(src_ref=src, dst_ref=dst, send_sem=send.at[k], recv_sem=recv.at[k],
                                        device_id=to, device_id_type=MESH)


def _gather_call(win_b, wout_b, cw8):
    big = [win_b, wout_b]

    def body(win, wout, cw, g_in, g_out, g_cw, send, recv, lsem):
        x, y, c, me, others = _place()
        sib = (x, y, 1 - c)
        loc = [pltpu.make_async_copy(win, g_in.at[me], lsem.at[0]),
               pltpu.make_async_copy(wout, g_out.at[me], lsem.at[1]),
               pltpu.make_async_copy(cw, g_cw.at[me], lsem.at[2])]
        for cp in loc:
            cp.start()
        started = []
        for a, (src, dst) in enumerate(((win, g_in), (wout, g_out))):
            half = src.shape[0] // 2
            mine = pl.ds(c * half, half)
            for j, chip in enumerate(others):
                cp = _remote(src.at[mine], dst.at[me, mine], send, recv, 6 * a + j, (*chip, c))
                cp.start()
                started.append(cp)
        for j, chip in enumerate(others):
            cp = _remote(cw, g_cw.at[me], send, recv, 12 + j, (*chip, c))
            cp.start()
            started.append(cp)
        for a, dst in enumerate((g_in, g_out)):
            half = dst.shape[1] // 2
            mine = pl.ds(c * half, half)
            for j, chip in enumerate(others):
                kj = 2 * chip[0] + chip[1]
                _remote(dst.at[kj, mine], dst.at[kj, mine], send, recv, 6 * a + j, (*chip, c)).wait_recv()
                cp = _remote(dst.at[kj, mine], dst.at[kj, mine], send, recv, 6 * a + 3 + j, sib)
                cp.start()
                started.append(cp)
        for a, dst in enumerate((g_in, g_out)):
            half = dst.shape[1] // 2
            theirs = pl.ds((1 - c) * half, half)
            for j, chip in enumerate(others):
                kj = 2 * chip[0] + chip[1]
                _remote(dst.at[kj, theirs], dst.at[kj, theirs], send, recv, 6 * a + 3 + j, sib).wait_recv()
        for j, chip in enumerate(others):
            kj = 2 * chip[0] + chip[1]
            _remote(cw, g_cw.at[kj], send, recv, 12 + j, (*chip, c)).wait_recv()
        for cp in started:
            cp.wait_send()
        for cp in loc:
            cp.wait()

    outs = [jax.ShapeDtypeStruct((4,) + a.shape, a.dtype) for a in (win_b, wout_b, cw8)]
    return pl.pallas_call(
        body, name="gather_weights",
        in_specs=[ANY, ANY, ANY], out_specs=[ANY, ANY, ANY], out_shape=outs,
        scratch_shapes=[pltpu.SemaphoreType.DMA((15,)), pltpu.SemaphoreType.DMA((15,)), pltpu.SemaphoreType.DMA((3,))],
    )(win_b, wout_b, cw8)


def _rs_sibling_call(p_in, p_out, vsmall):
    def body(pin, pout, vs, own_in, sib_in, own_out, sib_out, sib_v, send, recv, lsem):
        x, y, c, me, others = _place()
        sib = (x, y, 1 - c)
        cps, loc = [], []
        for a, (p, own, sb) in enumerate(((pin, own_in, sib_in), (pout, own_out, sib_out))):
            half = p.shape[1] // 2
            lc = pltpu.make_async_copy(p.at[:, pl.ds(c * half, half)], own, lsem.at[a])
            lc.start()
            loc.append(lc)
            cp = _remote(p.at[:, pl.ds((1 - c) * half, half)], sb, send, recv, a, sib)
            cp.start()
            cps.append(cp)
        cp = _remote(vs, sib_v, send, recv, 2, sib)
        cp.start()
        cps.append(cp)
        for cp in cps:
            cp.wait_recv()
        for cp in cps:
            cp.wait_send()
        for lc in loc:
            lc.wait()

    def halves(p):
        return jax.ShapeDtypeStruct((4, p.shape[1] // 2, p.shape[2]), p.dtype)

    outs = [halves(p_in), halves(p_in), halves(p_out), halves(p_out), jax.ShapeDtypeStruct(vsmall.shape, vsmall.dtype)]
    return pl.pallas_call(
        body, name="rs_sibling",
        in_specs=[ANY] * 3, out_specs=[ANY] * 5, out_shape=outs,
        scratch_shapes=[pltpu.SemaphoreType.DMA((3,)), pltpu.SemaphoreType.DMA((3,)), pltpu.SemaphoreType.DMA((2,))],
    )(p_in, p_out, vsmall)


def _rs_chips_call(s_in, sb_in, s_out, sb_out, chip_v):
    def body(sin, sbin, sout, sbout, cv, own_in, rc_in, own_out, rc_out, abs_v, send, recv, lsem):
        x, y, c, me, others = _place()
        loc = [pltpu.make_async_copy(sin.at[me], own_in, lsem.at[0]),
               pltpu.make_async_copy(sout.at[me], own_out, lsem.at[1]),
               pltpu.make_async_copy(cv, abs_v.at[me], lsem.at[2])]
        for lc in loc:
            lc.start()
        cps = []
        for j, chip in enumerate(others):
            kj = 2 * chip[0] + chip[1]
            to = (*chip, c)
            cps.append(_remote(sbin.at[kj], rc_in.at[j], send, recv, j, to))
            cps.append(_remote(sbout.at[kj], rc_out.at[j], send, recv, 3 + j, to))
            cps.append(_remote(cv, abs_v.at[me], send, recv, 6 + j, to))
        for cp in cps:
            cp.start()
        for j, chip in enumerate(others):
            kj = 2 * chip[0] + chip[1]
            cps[3 * j].wait_recv()
            cps[3 * j + 1].wait_recv()
            _remote(cv, abs_v.at[kj], send, recv, 6 + j, (*chip, c)).wait_recv()
        for cp in cps:
            cp.wait_send()
        for lc in loc:
            lc.wait()

    hs_in, hs_out = s_in.shape[1:], s_out.shape[1:]
    outs = [jax.ShapeDtypeStruct(hs_in, F32), jax.ShapeDtypeStruct((3,) + hs_in, sb_in.dtype),
            jax.ShapeDtypeStruct(hs_out, F32), jax.ShapeDtypeStruct((3,) + hs_out, sb_out.dtype),
            jax.ShapeDtypeStruct((4,) + chip_v.shape, F32)]
    return pl.pallas_call(
        body, name="rs_chips",
        in_specs=[ANY] * 5, out_specs=[ANY] * 5, out_shape=outs,
        scratch_shapes=[pltpu.SemaphoreType.DMA((9,)), pltpu.SemaphoreType.DMA((9,)), pltpu.SemaphoreType.DMA((3,))],
    )(s_in, sb_in, s_out, sb_out, chip_v)


def _rs_join_call(f_in, f_out):
    def body(fin, fout, full_in, full_out, send, recv, lsem):
        x, y, c, me, others = _place()
        sib = (x, y, 1 - c)
        cps, loc = [], []
        for a, (f, full) in enumerate(((fin, full_in), (fout, full_out))):
            half = f.shape[0]
            mine = pl.ds(c * half, half)
            lc = pltpu.make_async_copy(f, full.at[mine], lsem.at[a])
            lc.start()
            loc.append(lc)
            cp = _remote(f, full.at[mine], send, recv, a, sib)
            cp.start()
            cps.append(cp)
        for a, (f, full) in enumerate(((fin, full_in), (fout, full_out))):
            half = f.shape[0]
            theirs = pl.ds((1 - c) * half, half)
            _remote(f, full.at[theirs], send, recv, a, sib).wait_recv()
        for cp in cps:
            cp.wait_send()
        for lc in loc:
            lc.wait()

    outs = [jax.ShapeDtypeStruct((2 * f.shape[0], f.shape[1]), F32) for f in (f_in, f_out)]
    return pl.pallas_call(
        body, name="rs_join",
        in_specs=[ANY] * 2, out_specs=[ANY] * 2, out_shape=outs,
        scratch_shapes=[pltpu.SemaphoreType.DMA((2,)), pltpu.SemaphoreType.DMA((2,)), pltpu.SemaphoreType.DMA((2,))],
    )(f_in, f_out)


def _pack(arrs):
    parts = []
    for a in arrs:
        f = a.reshape(-1).astype(F32)
        pad = (-f.shape[0]) % 1024
        parts.append(jnp.pad(f, (0, pad)).reshape(-1, 128))
    return jnp.concatenate(parts, axis=0)


def _unpack(packed, shapes):
    out, row = [], 0
    for shp in shapes:
        n = 1
        for d in shp:
            n *= d
        rows = (n + 1023) // 1024 * 8
        out.append(packed[row:row + rows].reshape(-1)[:n].reshape(shp))
        row += rows
    return out


def _expand_heads(v32):
    return jnp.repeat(v32.reshape(32), HEADDIM).reshape(1, D_MODEL)


def kernel(x, norm_w, w_in, gate_b, sgu_norm_g, sgu_norm_b, sgu_w, sgu_b, conv_w, conv_b, dt_bias, A_log, D_skip, ssd_norm_w, w_out, final_norm_w, loss_target, m_norm_w, m_w_in, m_gate_b, m_sgu_norm_g, m_sgu_norm_b, m_sgu_w, m_sgu_b, m_conv_w, m_conv_b, m_dt_bias, m_A_log, m_D_skip, m_ssd_norm_w, m_w_out, m_final_norm_w, v_norm_w, v_w_in, v_gate_b, v_sgu_norm_g, v_sgu_norm_b, v_sgu_w, v_sgu_b, v_conv_w, v_conv_b, v_dt_bias, v_A_log, v_D_skip, v_ssd_norm_w, v_w_out, v_final_norm_w):
    s = x.shape[1]
    x2 = x.reshape(s, D_MODEL)
    tgt = loss_target.reshape(s, D_MODEL)
    t_ssd, t_tok, t_out, t_row = min(T_SSD, s), min(T_TOK, s), min(T_OUT, s), min(T_ROW, s)
    tm_mm, tk_dw = min(TM_MM, s), min(TK_DW, s)
    chip = 2 * lax.axis_index("x") + lax.axis_index("y")

    cw8 = jnp.pad(conv_w[0], ((0, 4), (0, 0)))
    g_in, g_out, g_cw = _gather_call(_c(w_in[0]), _c(w_out[0]), cw8)
    wref = jnp.transpose(g_in, (1, 0, 2)).reshape(D_MODEL, IN_W)
    w_out_full = g_out.reshape(D_MODEL, D_MODEL)
    conv_w_full = jnp.transpose(g_cw[:, 0:4, :], (1, 0, 2)).reshape(4, 3072)

    w_a = jnp.concatenate([wref[:, 0:6144], wref[:, 11296:15392]], axis=1)
    fw, bw = [], []
    for g in range(SSD_GROUPS):
        xs_g = wref[:, 8192 + 512 * g:8192 + 512 * g + 512]
        b_g = wref[:, 10240 + 128 * g:10240 + 128 * g + 128]
        c_g = wref[:, 10752 + 128 * g:10752 + 128 * g + 128]
        zb_g = wref[:, 6144 + 512 * g:6144 + 512 * g + 512]
        dt_g = wref[:, 11264 + 8 * g:11264 + 8 * g + 8]
        fw += [xs_g, b_g, c_g, jnp.repeat(dt_g, HEADDIM, axis=1), zb_g]
        bw += [zb_g, xs_g, b_g, c_g, jnp.pad(dt_g, ((0, 0), (0, 120)))]
    w_b = jnp.concatenate(fw, axis=1)
    wt_a = w_a.T
    wt_b = jnp.concatenate(bw, axis=1).T

    def group_cols(full_xs, full_bc):
        parts = []
        for g in range(SSD_GROUPS):
            parts += [full_xs[:, 512 * g:512 * g + 512], full_bc[:, 128 * g:128 * g + 128], full_bc[:, 512 + 128 * g:512 + 128 * g + 128]]
        return jnp.concatenate(parts, axis=1)

    cw_g = group_cols(conv_w_full[:, 0:2048], conv_w_full[:, 2048:3072])
    cb_g = group_cols(conv_b[:, 0:2048], conv_b[:, 2048:3072])
    dtb_e, alog_e, dsk_e = _expand_heads(dt_bias), _expand_heads(A_log), _expand_heads(D_skip)

    pos_chunk = jnp.arange(SGU_BLOCK) // CHUNK
    smask = pos_chunk[None, :] <= pos_chunk[:, None]
    wm_f = jnp.where(smask[None], sgu_w[0], 0.0)
    wm = _c(wm_f)
    wmt = _c(jnp.transpose(wm_f, (0, 2, 1)))
    bias_full = jnp.repeat(sgu_b[0].T, D_MODEL // SGU_GROUPS, axis=1)
    fnw = final_norm_w.reshape(1, D_MODEL)

    xn, xnt = _norm_call(x2, norm_w, t_row)
    proj_a = _mm(xn, w_a, tm=tm_mm, tn=1024, tk=D_MODEL, name="in_proj_a")
    proj_b = _mm(xn, w_b, tm=tm_mm, tn=1024, tk=D_MODEL, name="in_proj_b")
    y_ssd, y_b, hprev = _ssd_fwd_call(proj_b, dtb_e, alog_e, dsk_e, cw_g, cb_g, ssd_norm_w, t_ssd)
    y_a, merged, merged_t = _tok_fwd_call(proj_a, y_b, gate_b, sgu_norm_g, sgu_norm_b, wm, bias_full, t_tok)
    dh, dh_b, dmerged, loss_t, dfw8 = _out_call(merged, x2, tgt, w_out_full, fnw, t_out)

    dproj_a, dy_b, dgb8, dgam8, dbeta8, dbfull, dws = _tok_bwd_call(
        proj_a, dmerged, y_a, y_b, gate_b, sgu_norm_g, sgu_norm_b, wm, wmt, bias_full, t_tok)
    dproj_b, a512, a768 = _ssd_bwd_call(proj_b, dy_b, y_ssd, hprev, dtb_e, alog_e, dsk_e, cw_g, cb_g, ssd_norm_w, t_ssd)
    dw_a = _mm(xnt, dproj_a, tm=D_MODEL, tn=1024, tk=tk_dw, name="dw_in_a")
    dw_b = _mm(xnt, dproj_b, tm=D_MODEL, tn=BW_B, tk=tk_dw, name="dw_in_b")
    dw_out_p = _mm(merged_t, dh_b, tm=D_MODEL, tn=1024, tk=tk_dw, name="dw_out")
    dxn = _dx_mm(dproj_a, wt_a, dproj_b, wt_b, tm=t_row)
    grad_x, dnw8 = _gradx_call(x2, dxn, dh, norm_w, t_row)

    gb = lambda a, b: jnp.concatenate([dw_b[:, BW_B * g + a:BW_B * g + b] for g in range(SSD_GROUPS)], axis=1)
    dw_ref = jnp.concatenate([dw_a[:, 0:6144], gb(0, 512), gb(512, 1024), gb(1024, 1152), gb(1152, 1280),
                              gb(1280, 1288), dw_a[:, 6144:10240]], axis=1)
    p_in = jnp.transpose(dw_ref.reshape(D_MODEL, 4, SHARD_W), (1, 0, 2))
    p_out = dw_out_p.reshape(4, D_MODEL // 4, D_MODEL)

    s512 = jnp.sum(a512, axis=2)
    heads = lambda v: jnp.sum(v.reshape(32, HEADDIM), axis=1).reshape(1, 32)
    d_ssd_nw = s512[:, 0].reshape(1, D_MODEL)
    d_dskip = heads(s512[:, 1].reshape(D_MODEL))
    d_alog = heads(s512[:, 2].reshape(D_MODEL)) * (1.0 / HEADDIM) * (-jnp.exp(A_log))
    d_dtb = heads(s512[:, 3].reshape(D_MODEL))
    s768 = jnp.sum(a768, axis=2)
    ungroup = lambda v: jnp.concatenate([v[g, :, 0:512] for g in range(4)] + [v[g, :, 512:640] for g in range(4)]
                                        + [v[g, :, 640:768] for g in range(4)], axis=1)
    d_cw = ungroup(s768[:, 0:4])
    d_cb = ungroup(s768[:, 4:5])
    d_sgu_b = jnp.sum(dbfull.reshape(128, SGU_GROUPS, 128), axis=2).T.reshape(1, SGU_GROUPS, 128)
    d_sgu_w = jnp.where(smask[None], dws, 0.0).reshape(1, SGU_GROUPS, 128, 128)
    fold = lambda a8: jnp.sum(a8, axis=0, keepdims=True)
    small_local = [fold(dnw8), fold(dgb8), fold(dgam8), fold(dbeta8), d_sgu_w, d_sgu_b, d_cw, d_cb,
                   d_dtb, d_alog, d_dskip, d_ssd_nw, fold(dfw8).reshape(D_MODEL), jnp.sum(loss_t[:, 0, 0]).reshape(1)]
    small_shapes = [a.shape for a in small_local]
    v_local = _pack(small_local)

    own_i, sib_i, own_o, sib_o, sib_v = _rs_sibling_call(p_in, p_out, v_local)
    hr_i, hr_o = D_MODEL // 2, D_MODEL // 8
    s1_i, s1b_i = _add_pair(own_i.reshape(4 * hr_i, SHARD_W), sib_i.reshape(4 * hr_i, SHARD_W), 256, "rs_add_in", True)
    s1_o, s1b_o = _add_pair(own_o.reshape(4 * hr_o, D_MODEL), sib_o.reshape(4 * hr_o, D_MODEL), 256, "rs_add_out", True)
    (chip_v,) = _add_pair(v_local, sib_v, v_local.shape[0], "ar_add_small", False)
    o_i, r_i, o_o, r_o, abs_v = _rs_chips_call(s1_i.reshape(4, hr_i, SHARD_W), s1b_i.reshape(4, hr_i, SHARD_W),
                                               s1_o.reshape(4, hr_o, D_MODEL), s1b_o.reshape(4, hr_o, D_MODEL), chip_v)
    f_i = _sum_own_recv(o_i, r_i, 256, "rs_sum_in")
    f_o = _sum_own_recv(o_o, r_o, 256, "rs_sum_out")
    g_w_in, g_w_out = _rs_join_call(f_i, f_o)
    total_v = _sum_chips(abs_v, abs_v.shape[1], "ar_sum_small")
    (g_nw, g_gb, g_gam, g_beta, g_sw, g_sb, g_cw_full, g_cb, g_dtb, g_alog, g_dsk, g_snw, g_fnw, loss1) = _unpack(total_v, small_shapes)
    g_cw_shard = lax.dynamic_slice(g_cw_full, (0, chip * 768), (4, 768)).reshape(1, 4, 768)
    loss = loss1.reshape(())

    d_win, nm_win, nv_win = _adamw(w_in[0], g_w_in, m_w_in[0], v_w_in[0], 128, "adamw_w_in")
    d_wout, nm_wout, nv_wout = _adamw(w_out[0], g_w_out, m_w_out[0], v_w_out[0], 128, "adamw_w_out")
    small_w = [norm_w, gate_b, sgu_norm_g, sgu_norm_b, sgu_w, sgu_b, conv_w, conv_b, dt_bias, A_log, D_skip, ssd_norm_w, final_norm_w]
    small_m = [m_norm_w, m_gate_b, m_sgu_norm_g, m_sgu_norm_b, m_sgu_w, m_sgu_b, m_conv_w, m_conv_b, m_dt_bias, m_A_log, m_D_skip, m_ssd_norm_w, m_final_norm_w]
    small_v = [v_norm_w, v_gate_b, v_sgu_norm_g, v_sgu_norm_b, v_sgu_w, v_sgu_b, v_conv_w, v_conv_b, v_dt_bias, v_A_log, v_D_skip, v_ssd_norm_w, v_final_norm_w]
    small_g = [g_nw, g_gb, g_gam, g_beta, g_sw, g_sb, g_cw_shard, g_cb, g_dtb, g_alog, g_dsk, g_snw, g_fnw]
    shapes_w = [a.shape for a in small_w]
    small_g = [a.reshape(shp) for a, shp in zip(small_g, shapes_w)]
    pw = _pack(small_w)
    pd, pm, pv = _adamw(pw, _pack(small_g), _pack(small_m), _pack(small_v), pw.shape[0], "adamw_small")
    d_small, nm_small, nv_small = _unpack(pd, shapes_w), _unpack(pm, shapes_w), _unpack(pv, shapes_w)

    def with_big(small, win, wout):
        o = list(small)
        return o[0:1] + [win.reshape(1, D_MODEL, SHARD_W)] + o[1:12] + [wout.reshape(1, D_MODEL // 4, D_MODEL)] + o[12:13]

    grads = with_big(small_g, g_w_in, g_w_out)
    deltas = with_big(d_small, d_win, d_wout)
    new_m = with_big(nm_small, nm_win, nm_wout)
    new_v = with_big(nv_small, nv_win, nv_wout)
    return (loss, grad_x.reshape(1, s, D_MODEL), *grads, *deltas, *new_m, *new_v)
```

```python
import functools

import jax
import jax.numpy as jnp
from jax import lax
from jax.experimental import pallas as pl
from jax.experimental.pallas import tpu as pltpu

F32 = jnp.float32
MXU_DTYPE = jnp.bfloat16

D_MODEL = 2048
EPS = 1e-5
CHUNK = 64
SGU_BLOCK = 128
SGU_GROUPS = 16
SSD_GROUPS = 4
SSD_GW = 512
SSD_STATE = 128
HEADDIM = 64
IN_W = 15392
SHARD_W = IN_W // 4
FW_B = 1792
BW_B = 1408
NA = 10240

ADAM_LR = 0.001
ADAM_B1 = 0.9
ADAM_B2 = 0.999
ADAM_EPS = 1e-08
ADAM_WD = 0.01
ADAM_STEP = 10

T_SSD = 256
T_TOK = 128
T_OUT = 256
T_ROW = 512
TM_MM = 1024
TK_DW = 1024
VMEM_CAP = 60 * 1024 * 1024
MESH = pl.DeviceIdType.MESH


def _cparams(sem, est_bytes):
    lim = int(min(VMEM_CAP, max(32 * 1024 * 1024, est_bytes + 12 * 1024 * 1024)))
    return pltpu.CompilerParams(dimension_semantics=sem, vmem_limit_bytes=lim)


def _c(x):
    return x.astype(MXU_DTYPE)


def _dot(a, b):
    return jnp.dot(a, b, preferred_element_type=F32)


def _dot_nt(a, b):
    return lax.dot_general(a, b, (((1,), (1,)), ((), ())), preferred_element_type=F32)


def _dot_tn(a, b):
    return lax.dot_general(a, b, (((0,), (0,)), ((), ())), preferred_element_type=F32)


def _split(x, n):
    parts, r = [], x
    for _ in range(n):
        p = _c(r)
        parts.append(p)
        r = r - p.astype(F32)
    return parts


def _dot01_l(m01, x, n):
    acc = None
    for p in _split(x, n):
        t = _dot(m01, p)
        acc = t if acc is None else acc + t
    return acc


def _dot01_r(x, m01, n):
    acc = None
    for p in _split(x, n):
        t = _dot(p, m01)
        acc = t if acc is None else acc + t
    return acc


def _sigmoid(x):
    return 1.0 / (1.0 + jnp.exp(-x))


def _fold8(x):
    r, w = x.shape
    return jnp.sum(x.reshape(r // 8, 8, w), axis=0)


def _iota(shape, dim):
    return lax.broadcasted_iota(jnp.int32, shape, dim)


def _ssd_masks():
    l64 = _iota((CHUNK, SSD_GW), 0)
    s64 = jnp.bitwise_and(_iota((CHUNK, SSD_GW), 1), CHUNK - 1)
    diag = l64 == s64
    causal = l64 >= s64
    row_last = l64 == CHUNK - 1
    r4 = lax.shift_right_logical(_iota((256, 256), 0), 6)
    c4 = lax.shift_right_logical(_iota((256, 256), 1), 6)
    mask4 = r4 == c4
    tl = _iota((CHUNK, CHUNK), 0)
    tc = _iota((CHUNK, CHUNK), 1)
    tri = _c(jnp.where(tc <= tl, 1.0, 0.0))
    trit = _c(jnp.where(tc >= tl, 1.0, 0.0))
    return diag, causal, row_last, mask4, tri, trit


def _ssd_common(xs, bm, cm, dt, a_neg, masks):
    diag, causal, row_last, mask4, tri, trit = masks
    a = dt * a_neg
    acs = _dot01_l(tri, a, 3)
    row_e = jnp.sum(jnp.where(diag, acs, 0.0), axis=0, keepdims=True)
    seg = acs - row_e
    lm = jnp.exp(jnp.where(causal, seg, -1e30))
    bb, cb = _c(bm), _c(cm)
    brep = jnp.concatenate([bb] * 8, axis=0)
    cbrep = _dot_nt(cb, brep)
    m = cbrep * lm
    xdt = xs * dt
    acs_last = jnp.sum(jnp.where(row_last, acs, 0.0), axis=0, keepdims=True)
    dec = jnp.exp(acs_last - acs)
    eacs = jnp.exp(acs)
    cd = jnp.exp(acs_last)
    return dict(lm=lm, bb=bb, cb=cb, brep=brep, m=m, xdt=xdt, dec=dec, eacs=eacs, cd=cd)


def _blockdiag4(xb, mask4):
    return jnp.where(mask4, jnp.concatenate([xb] * 4, axis=0), jnp.zeros((), xb.dtype))


def _ssd_chunk_fwd(xs, bm, cm, dt, a_neg, d_skip, ht, masks):
    q = _ssd_common(xs, bm, cm, dt, a_neg, masks)
    mask4 = masks[3]
    mb, xdtb = _c(q["m"]), _c(q["xdt"])
    yd = []
    for blk in range(2):
        sl = slice(256 * blk, 256 * blk + 256)
        yd.append(_dot(mb[:, sl], _blockdiag4(xdtb[:, sl], mask4)))
    y_diag = jnp.concatenate(yd, axis=1)
    p = _dot(q["cb"], _c(ht))
    y = y_diag + p * q["eacs"] + xs * d_skip
    st = _dot_tn(q["bb"], _c(q["xdt"] * q["dec"]))
    return y, ht * q["cd"] + st


def _ssd_chunk_bwd(xs, bm, cm, dt, sig, a_neg, d_skip, hprev, dht, dy, masks):
    diag, causal, row_last, mask4, tri, trit = masks
    q = _ssd_common(xs, bm, cm, dt, a_neg, masks)
    lm, bb, cb, brep, m, xdt, dec, eacs, cd = (q[k] for k in ("lm", "bb", "cb", "brep", "m", "xdt", "dec", "eacs", "cd"))
    hb = _c(hprev)
    yoff = _dot(cb, hb) * eacs
    dyb = _c(dy)
    dpb = _c(dy * eacs)
    d_c = _dot_nt(dpb, hb)
    dh_y = _dot_tn(cb, dpb)
    mb, xdtb = _c(m), _c(xdt)
    dm_parts, dxdt_parts = [], []
    for blk in range(2):
        sl = slice(256 * blk, 256 * blk + 256)
        bd = _blockdiag4(xdtb[:, sl], mask4)
        dm_parts.append(_dot_nt(dyb[:, sl], bd))
        dxf = jnp.where(mask4, _dot_tn(mb[:, sl], dyb[:, sl]), 0.0)
        dxdt_parts.append(dxf[0:64] + dxf[64:128] + dxf[128:192] + dxf[192:256])
    dm = jnp.concatenate(dm_parts, axis=1)
    dxdt = jnp.concatenate(dxdt_parts, axis=1)
    dcbb = _c(dm * lm)
    g = dm * m
    d_c = d_c + _dot(dcbb, brep)
    dbrep = _dot_tn(dcbb, cb)
    d_b = dbrep[0:64]
    for r in range(1, 8):
        d_b = d_b + dbrep[64 * r:64 * r + 64]
    dhtb = _c(dht)
    dxd = _dot(bb, dhtb)
    xd = xdt * dec
    dxdt = dxdt + dxd * dec
    tq = dxd * xd
    d_b = d_b + _dot_nt(_c(xd), dhtb)
    dcd = jnp.sum(dht * hprev, axis=0, keepdims=True)
    col_g = jnp.sum(g, axis=0, keepdims=True)
    last = jnp.sum(tq, axis=0, keepdims=True) + dcd * cd
    qq = g - jnp.where(diag, col_g, 0.0) + dy * yoff - tq + jnp.where(row_last, last, 0.0)
    bd4 = _c(jnp.where(mask4, 1.0, 0.0))
    dacs = jnp.concatenate([_dot01_r(qq[:, 256 * b:256 * b + 256], bd4, 2) for b in range(2)], axis=1)
    da = _dot01_l(trit, dacs, 2)
    ddt = dxdt * xs + da * (a_neg * (1.0 / HEADDIM))
    dxs = dxdt * dt + dy * d_skip
    return dxs, d_b, d_c, ddt * sig, dht * cd + dh_y, da * dt, dy * xs


def _softplus(x):
    return jnp.maximum(x, 0.0) + jnp.log1p(jnp.exp(-jnp.abs(x)))


def _conv_taps(xpad, t):
    taps = []
    for k in range(4):
        sh = 3 - k
        v = xpad if sh == 0 else pltpu.roll(xpad, sh, 0)
        taps.append(v[8:8 + t])
    return taps


def _norm_call(x, norm_w, tm):
    s = x.shape[0]

    def body(x_ref, w_ref, xn_ref, xnt_ref):
        xv = x_ref[...]
        r = lax.rsqrt(jnp.mean(xv * xv, axis=-1, keepdims=True) + EPS)
        xn = xv * r * w_ref[...]
        xn_ref[...] = _c(xn)
        xnt_ref[...] = _c(xn.T)

    return pl.pallas_call(
        body, name="rmsnorm_in",
        grid=(s // tm,),
        in_specs=[pl.BlockSpec((tm, D_MODEL), lambda i: (i, 0)), pl.BlockSpec((1, D_MODEL), lambda i: (0, 0))],
        out_specs=[pl.BlockSpec((tm, D_MODEL), lambda i: (i, 0)), pl.BlockSpec((D_MODEL, tm), lambda i: (0, i))],
        out_shape=[jax.ShapeDtypeStruct((s, D_MODEL), MXU_DTYPE), jax.ShapeDtypeStruct((D_MODEL, s), MXU_DTYPE)],
        compiler_params=_cparams(("parallel",), 2 * tm * D_MODEL * 12),
    )(x, norm_w)


def _mm(a, b, *, tm, tn, tk, name):
    m, k = a.shape
    n = b.shape[1]
    nk = k // tk
    assert m % tm == 0 and n % tn == 0 and k % tk == 0, (a.shape, b.shape, tm, tn, tk)

    def body(a_ref, b_ref, o_ref):
        if nk == 1:
            o_ref[...] = _dot(a_ref[...], b_ref[...])
        else:
            @pl.when(pl.program_id(2) == 0)
            def _():
                o_ref[...] = jnp.zeros_like(o_ref)

            o_ref[...] += _dot(a_ref[...], b_ref[...])

    isz = jnp.dtype(a.dtype).itemsize
    est = 2 * (tm * tk + tk * tn) * isz + 2 * tm * tn * 4
    return pl.pallas_call(
        body, name=name,
        grid=(m // tm, n // tn, nk),
        in_specs=[pl.BlockSpec((tm, tk), lambda i, j, kk: (i, kk)), pl.BlockSpec((tk, tn), lambda i, j, kk: (kk, j))],
        out_specs=pl.BlockSpec((tm, tn), lambda i, j, kk: (i, j)),
        out_shape=jax.ShapeDtypeStruct((m, n), F32),
        compiler_params=_cparams(("parallel", "parallel", "arbitrary"), est),
    )(a, b)


def _dx_mm(dpa, wta, dpb, wtb, *, tm):
    s = dpa.shape[0]
    tka, tkb = 1024, BW_B
    nka, nkb = dpa.shape[1] // tka, dpb.shape[1] // tkb

    def body(a_ref, wa_ref, b_ref, wb_ref, o_ref):
        kk = pl.program_id(1)

        @pl.when(kk == 0)
        def _():
            o_ref[...] = jnp.zeros_like(o_ref)

        @pl.when(kk < nka)
        def _():
            o_ref[...] += _dot(a_ref[...], wa_ref[...])

        @pl.when(kk >= nka)
        def _():
            o_ref[...] += _dot(b_ref[...], wb_ref[...])

    isz = jnp.dtype(dpa.dtype).itemsize
    est = 2 * isz * (tm * tka + tka * D_MODEL + tm * tkb + tkb * D_MODEL) + 2 * tm * D_MODEL * 4
    return pl.pallas_call(
        body, name="dx_matmul",
        grid=(s // tm, nka + nkb),
        in_specs=[
            pl.BlockSpec((tm, tka), lambda i, kk: (i, jnp.minimum(kk, nka - 1))),
            pl.BlockSpec((tka, D_MODEL), lambda i, kk: (jnp.minimum(kk, nka - 1), 0)),
            pl.BlockSpec((tm, tkb), lambda i, kk: (i, jnp.maximum(kk - nka, 0))),
            pl.BlockSpec((tkb, D_MODEL), lambda i, kk: (jnp.maximum(kk - nka, 0), 0)),
        ],
        out_specs=pl.BlockSpec((tm, D_MODEL), lambda i, kk: (i, 0)),
        out_shape=jax.ShapeDtypeStruct((s, D_MODEL), F32),
        compiler_params=_cparams(("parallel", "arbitrary"), est),
    )(dpa, wta, dpb, wtb)


def _gradx_call(x, dxn, dh, norm_w, tm):
    s = x.shape[0]

    def body(x_ref, g_ref, dh_ref, w_ref, gx_ref, dw_ref):
        @pl.when(pl.program_id(0) == 0)
        def _():
            dw_ref[...] = jnp.zeros_like(dw_ref)

        xv, gv = x_ref[...], g_ref[...]
        r = lax.rsqrt(jnp.mean(xv * xv, axis=-1, keepdims=True) + EPS)
        gw = gv * w_ref[...]
        gx_ref[...] = r * gw - xv * (r * r * r) * jnp.mean(xv * gw, axis=-1, keepdims=True) + dh_ref[...]
        dw_ref[...] += _fold8(gv * (xv * r))

    row = pl.BlockSpec((tm, D_MODEL), lambda i: (i, 0))
    return pl.pallas_call(
        body, name="grad_x",
        grid=(s // tm,),
        in_specs=[row, row, row, pl.BlockSpec((1, D_MODEL), lambda i: (0, 0))],
        out_specs=[row, pl.BlockSpec((8, D_MODEL), lambda i: (0, 0))],
        out_shape=[jax.ShapeDtypeStruct((s, D_MODEL), F32), jax.ShapeDtypeStruct((8, D_MODEL), F32)],
        compiler_params=_cparams(("arbitrary",), 2 * tm * D_MODEL * 16),
    )(x, dxn, dh, norm_w)


def _layernorm_stats(v):
    mu = jnp.mean(v, axis=-1, keepdims=True)
    vc = v - mu
    var = jnp.mean(vc * vc, axis=-1, keepdims=True)
    return vc * lax.rsqrt(var + EPS), lax.rsqrt(var + EPS)


def _tok_fwd_call(proj_a, y_b, gate_b, sgu_g, sgu_beta, wm, bias_full, t):
    s = proj_a.shape[0]

    def body(pa_ref, yb_ref, gb_ref, g_ref, be_ref, wm_ref, bf_ref, ya_ref, mg_ref, mgt_ref, mix_ref):
        u = pa_ref[:, 0:2048]
        v = pa_ref[:, 2048:4096]
        za = pa_ref[:, 4096:6144]
        xhat, _ = _layernorm_stats(v)
        vnb = _c(xhat * g_ref[...] + be_ref[...])
        for gi in range(SGU_GROUPS):
            sl = slice(128 * gi, 128 * gi + 128)
            mix_ref[:, sl] = _dot(wm_ref[gi], vnb[:, sl])
        mixed = mix_ref[...] + bf_ref[...]
        y_a = u * mixed * (za * _sigmoid(za))
        g0 = _sigmoid(pa_ref[:, 6144:8192] + gb_ref[:, 0:2048])
        g1 = _sigmoid(pa_ref[:, 8192:10240] + gb_ref[:, 2048:4096])
        merged = g0 * y_a + g1 * yb_ref[...]
        ya_ref[...] = y_a
        mg_ref[...] = _c(merged)
        mgt_ref[...] = _c(merged.T)

    row = pl.BlockSpec((t, D_MODEL), lambda i: (i, 0))
    vec = lambda w: pl.BlockSpec((1, w), lambda i: (0, 0))
    return pl.pallas_call(
        body, name="tok_fwd",
        grid=(s // t,),
        in_specs=[pl.BlockSpec((t, NA), lambda i: (i, 0)), row, vec(4096), vec(2048), vec(2048),
                  pl.BlockSpec((SGU_GROUPS, 128, 128), lambda i: (0, 0, 0)), pl.BlockSpec((128, D_MODEL), lambda i: (0, 0))],
        out_specs=[row, row, pl.BlockSpec((D_MODEL, t), lambda i: (0, i))],
        out_shape=[jax.ShapeDtypeStruct((s, D_MODEL), F32), jax.ShapeDtypeStruct((s, D_MODEL), MXU_DTYPE),
                   jax.ShapeDtypeStruct((D_MODEL, s), MXU_DTYPE)],
        scratch_shapes=[pltpu.VMEM((t, D_MODEL), F32)],
        compiler_params=_cparams(("parallel",), 2 * t * NA * 4 + 12 * t * D_MODEL * 4),
    )(proj_a, y_b, gate_b, sgu_g, sgu_beta, wm, bias_full)


def _tok_bwd_call(proj_a, dmerged, y_a, y_b, gate_b, sgu_g, sgu_beta, wm, wmt, bias_full, t):
    s = proj_a.shape[0]

    def body(pa_ref, dm_ref, ya_ref, yb_ref, gb_ref, g_ref, be_ref, wm_ref, wmt_ref, bf_ref,
             dpa_ref, dyb_ref, dgb_ref, dgam_ref, dbeta_ref, dbf_ref, dws_ref, mix_ref, dvn_ref):
        @pl.when(pl.program_id(0) == 0)
        def _():
            dgb_ref[...] = jnp.zeros_like(dgb_ref)
            dgam_ref[...] = jnp.zeros_like(dgam_ref)
            dbeta_ref[...] = jnp.zeros_like(dbeta_ref)
            dbf_ref[...] = jnp.zeros_like(dbf_ref)
            dws_ref[...] = jnp.zeros_like(dws_ref)

        u = pa_ref[:, 0:2048]
        v = pa_ref[:, 2048:4096]
        za = pa_ref[:, 4096:6144]
        xhat, rstd = _layernorm_stats(v)
        vnb = _c(xhat * g_ref[...] + be_ref[...])
        for gi in range(SGU_GROUPS):
            sl = slice(128 * gi, 128 * gi + 128)
            mix_ref[:, sl] = _dot(wm_ref[gi], vnb[:, sl])
        mixed = mix_ref[...] + bf_ref[...]
        sig = _sigmoid(za)
        sz = za * sig
        dm = dm_ref[...]
        y_a = ya_ref[...]
        g0 = _sigmoid(pa_ref[:, 6144:8192] + gb_ref[:, 0:2048])
        g1 = _sigmoid(pa_ref[:, 8192:10240] + gb_ref[:, 2048:4096])
        dgl0 = dm * y_a * g0 * (1.0 - g0)
        dgl1 = dm * yb_ref[...] * g1 * (1.0 - g1)
        dyb_ref[...] = dm * g1
        dya = dm * g0
        dpa_ref[:, 6144:8192] = _c(dgl0)
        dpa_ref[:, 8192:10240] = _c(dgl1)
        dgb_ref[:, 0:2048] += _fold8(dgl0)
        dgb_ref[:, 2048:4096] += _fold8(dgl1)
        dpa_ref[:, 0:2048] = _c(dya * mixed * sz)
        dpa_ref[:, 4096:6144] = _c(dya * (u * mixed) * (sig * (1.0 + za * (1.0 - sig))))
        dmixed = dya * u * sz
        dbf_ref[...] += dmixed
        dmb = _c(dmixed)
        for gi in range(SGU_GROUPS):
            sl = slice(128 * gi, 128 * gi + 128)
            dvn_ref[:, sl] = _dot(wmt_ref[gi], dmb[:, sl])
            dws_ref[gi] += _dot_nt(dmb[:, sl], vnb[:, sl])
        dvn = dvn_ref[...]
        dgam_ref[...] += _fold8(dvn * xhat)
        dbeta_ref[...] += _fold8(dvn)
        dxh = dvn * g_ref[...]
        dv = rstd * (dxh - jnp.mean(dxh, axis=-1, keepdims=True) - xhat * jnp.mean(dxh * xhat, axis=-1, keepdims=True))
        dpa_ref[:, 2048:4096] = _c(dv)

    row = pl.BlockSpec((t, D_MODEL), lambda i: (i, 0))
    vec = lambda w: pl.BlockSpec((1, w), lambda i: (0, 0))
    acc = lambda w: pl.BlockSpec((8, w), lambda i: (0, 0))
    wspec = pl.BlockSpec((SGU_GROUPS, 128, 128), lambda i: (0, 0, 0))
    return pl.pallas_call(
        body, name="tok_bwd",
        grid=(s // t,),
        in_specs=[pl.BlockSpec((t, NA), lambda i: (i, 0)), row, row, row, vec(4096), vec(2048), vec(2048),
                  wspec, wspec, pl.BlockSpec((128, D_MODEL), lambda i: (0, 0))],
        out_specs=[pl.BlockSpec((t, NA), lambda i: (i, 0)), row, acc(4096), acc(2048), acc(2048),
                   pl.BlockSpec((128, D_MODEL), lambda i: (0, 0)), wspec],
        out_shape=[jax.ShapeDtypeStruct((s, NA), MXU_DTYPE), jax.ShapeDtypeStruct((s, D_MODEL), F32),
                   jax.ShapeDtypeStruct((8, 4096), F32), jax.ShapeDtypeStruct((8, 2048), F32),
                   jax.ShapeDtypeStruct((8, 2048), F32), jax.ShapeDtypeStruct((128, D_MODEL), F32),
                   jax.ShapeDtypeStruct((SGU_GROUPS, 128, 128), F32)],
        scratch_shapes=[pltpu.VMEM((t, D_MODEL), F32), pltpu.VMEM((t, D_MODEL), F32)],
        compiler_params=_cparams(("arbitrary",), 2 * t * NA * 6 + 16 * t * D_MODEL * 4),
    )(proj_a, dmerged, y_a, y_b, gate_b, sgu_g, sgu_beta, wm, wmt, bias_full)


def _out_call(merged, x, target, w_out, fnw, t):
    s = x.shape[0]
    nt = s // t

    def body(mg_ref, x_ref, t_ref, w_ref, fw_ref, dh_ref, dhb_ref, dmg_ref, loss_ref, dfw_ref):
        @pl.when(pl.program_id(0) == 0)
        def _():
            dfw_ref[...] = jnp.zeros_like(dfw_ref)

        h = x_ref[...] + _dot(mg_ref[...], w_ref[...])
        r = lax.rsqrt(jnp.mean(h * h, axis=-1, keepdims=True) + EPS)
        hn = h * r
        err = hn * fw_ref[...] - t_ref[...]
        loss_ref[...] = jnp.full(loss_ref.shape, 0.5 * jnp.sum(jnp.mean(err * err, axis=-1, keepdims=True)), F32)
        dy = err * (1.0 / D_MODEL)
        dfw_ref[...] += _fold8(dy * hn)
        gw = dy * fw_ref[...]
        dh = r * gw - h * (r * r * r) * jnp.mean(h * gw, axis=-1, keepdims=True)
        dh_ref[...] = dh
        dhb = _c(dh)
        dhb_ref[...] = dhb
        dmg_ref[...] = _dot_nt(dhb, w_ref[...])

    row = pl.BlockSpec((t, D_MODEL), lambda i: (i, 0))
    return pl.pallas_call(
        body, name="out_proj_loss",
        grid=(nt,),
        in_specs=[row, row, row, pl.BlockSpec((D_MODEL, D_MODEL), lambda i: (0, 0)), pl.BlockSpec((1, D_MODEL), lambda i: (0, 0))],
        out_specs=[row, row, row, pl.BlockSpec((1, 8, 128), lambda i: (i, 0, 0)), pl.BlockSpec((8, D_MODEL), lambda i: (0, 0))],
        out_shape=[jax.ShapeDtypeStruct((s, D_MODEL), F32), jax.ShapeDtypeStruct((s, D_MODEL), MXU_DTYPE),
                   jax.ShapeDtypeStruct((s, D_MODEL), F32), jax.ShapeDtypeStruct((nt, 8, 128), F32),
                   jax.ShapeDtypeStruct((8, D_MODEL), F32)],
        compiler_params=_cparams(("arbitrary",), 2 * D_MODEL * D_MODEL * 2 + 2 * t * D_MODEL * 24),
    )(merged, x, target, w_out, fnw)


def _ssd_fwd_call(proj_b, dtb, alog, dsk, cw, cb, nw, t):
    s = proj_b.shape[0]
    nt, nch = s // t, t // CHUNK

    def body(pb_ref, halo_ref, dtb_ref, al_ref, ds_ref, cw_ref, cb_ref, nw_ref, y_ref, yb_ref, hp_ref, ht_ref, act_ref):
        i = pl.program_id(1)

        @pl.when(i == 0)
        def _():
            ht_ref[...] = jnp.zeros_like(ht_ref)

        halo = jnp.where(i == 0, 0.0, halo_ref[:, 0:768])
        taps = _conv_taps(jnp.concatenate([halo, pb_ref[:, 0:768]], axis=0), t)
        pre = cb_ref[...]
        for k in range(4):
            pre = pre + taps[k] * cw_ref[k:k + 1, :]
        act_ref[...] = pre * _sigmoid(pre)
        masks = _ssd_masks()
        a_neg = -jnp.exp(al_ref[...])

        def chunk(c, carry):
            rows = pl.ds(pl.multiple_of(c * CHUNK, CHUNK), CHUNK)
            dt = _softplus(pb_ref[rows, 768:1280] + dtb_ref[...])
            ht = ht_ref[...]
            hp_ref[c] = ht
            y, ht_new = _ssd_chunk_fwd(act_ref[rows, 0:512], act_ref[rows, 512:640], act_ref[rows, 640:768],
                                       dt, a_neg, ds_ref[...], ht, masks)
            y_ref[rows, :] = y
            ht_ref[...] = ht_new
            return carry

        lax.fori_loop(0, nch, chunk, 0)
        zb = pb_ref[:, 1280:1792]
        hh = y_ref[...] * (zb * _sigmoid(zb))
        rr = lax.rsqrt(jnp.mean(hh * hh, axis=-1, keepdims=True) + EPS)
        yb_ref[...] = hh * rr * nw_ref[...]

    gvec = lambda w: pl.BlockSpec((1, w), lambda g, i: (0, g))
    return pl.pallas_call(
        body, name="ssd_fwd",
        grid=(SSD_GROUPS, nt),
        in_specs=[pl.BlockSpec((t, FW_B), lambda g, i: (i, g)),
                  pl.BlockSpec((8, FW_B), lambda g, i: (jnp.maximum(i * (t // 8) - 1, 0), g)),
                  gvec(512), gvec(512), gvec(512),
                  pl.BlockSpec((4, 768), lambda g, i: (0, g)), gvec(768), gvec(512)],
        out_specs=[pl.BlockSpec((t, SSD_GW), lambda g, i: (i, g)), pl.BlockSpec((t, SSD_GW), lambda g, i: (i, g)),
                   pl.BlockSpec((nch, SSD_STATE, SSD_GW), lambda g, i: (i, 0, g))],
        out_shape=[jax.ShapeDtypeStruct((s, D_MODEL), F32), jax.ShapeDtypeStruct((s, D_MODEL), F32),
                   jax.ShapeDtypeStruct((s // CHUNK, SSD_STATE, D_MODEL), F32)],
        scratch_shapes=[pltpu.VMEM((SSD_STATE, SSD_GW), F32), pltpu.VMEM((t, 768), F32)],
        compiler_params=_cparams(("parallel", "arbitrary"), 2 * t * FW_B * 4 + 6 * t * SSD_GW * 4 + 24 * 1024 * 1024),
    )(proj_b, proj_b, dtb, alog, dsk, cw, cb, nw)


def _ssd_bwd_call(proj_b, dyb, y, hprev, dtb, alog, dsk, cw, cb, nw, t):
    s = proj_b.shape[0]
    nt, nch = s // t, t // CHUNK

    def body(pb_ref, halo_ref, dyb_ref, y_ref, hp_ref, dtb_ref, al_ref, ds_ref, cw_ref, cb_ref, nw_ref,
             dpb_ref, a512_ref, a768_ref, dht_ref, act_ref, pre_ref, dact_ref, dy_ref, nxt_ref):
        i = pl.program_id(1)
        tile = nt - 1 - i

        @pl.when(i == 0)
        def _():
            dht_ref[...] = jnp.zeros_like(dht_ref)
            nxt_ref[...] = jnp.zeros_like(nxt_ref)
            a512_ref[...] = jnp.zeros_like(a512_ref)
            a768_ref[...] = jnp.zeros_like(a768_ref)

        halo = jnp.where(tile == 0, 0.0, halo_ref[:, 0:768])
        taps = _conv_taps(jnp.concatenate([halo, pb_ref[:, 0:768]], axis=0), t)
        pre = cb_ref[...]
        for k in range(4):
            pre = pre + taps[k] * cw_ref[k:k + 1, :]
        pre_ref[...] = pre
        act_ref[...] = pre * _sigmoid(pre)

        zb = pb_ref[:, 1280:1792]
        yv = y_ref[...]
        sgz = _sigmoid(zb)
        sz = zb * sgz
        hh = yv * sz
        rr = lax.rsqrt(jnp.mean(hh * hh, axis=-1, keepdims=True) + EPS)
        dyb = dyb_ref[...]
        a512_ref[0, 0] += _fold8(dyb * (hh * rr))
        tt = dyb * nw_ref[...]
        dhh = rr * tt - hh * (rr * rr * rr) * jnp.mean(hh * tt, axis=-1, keepdims=True)
        dy_ref[...] = dhh * sz
        dpb_ref[:, 0:512] = _c(dhh * yv * (sgz * (1.0 + zb * (1.0 - sgz))))

        masks = _ssd_masks()
        a_neg = -jnp.exp(al_ref[...])
        rsel = _c(jnp.where(lax.shift_right_logical(_iota((SSD_GW, 128), 0), 6) == _iota((SSD_GW, 128), 1), 1.0, 0.0))

        def chunk(cc, carry):
            c = nch - 1 - cc
            rows = pl.ds(pl.multiple_of(c * CHUNK, CHUNK), CHUNK)
            z = pb_ref[rows, 768:1280] + dtb_ref[...]
            xs = act_ref[rows, 0:512]
            dxs, d_b, d_c, ddtr, dht_prev, dadt, dyxs = _ssd_chunk_bwd(
                xs, act_ref[rows, 512:640], act_ref[rows, 640:768], _softplus(z), _sigmoid(z),
                a_neg, ds_ref[...], hp_ref[c], dht_ref[...], dy_ref[rows, :], masks)
            dht_ref[...] = dht_prev
            dact_ref[rows, 0:512] = dxs
            dact_ref[rows, 512:640] = d_b
            dact_ref[rows, 640:768] = d_c
            dpb_ref[rows, 1280:1408] = _c(_dot01_r(ddtr, rsel, 2))
            a512_ref[0, 1] += _fold8(dyxs)
            a512_ref[0, 2] += _fold8(dadt)
            a512_ref[0, 3] += _fold8(ddtr)
            return carry

        lax.fori_loop(0, nch, chunk, 0)

        pre = pre_ref[...]
        sp = _sigmoid(pre)
        dpre = dact_ref[...] * (sp * (1.0 + pre * (1.0 - sp)))
        a768_ref[0, 4] += _fold8(dpre)
        for k in range(4):
            a768_ref[0, k] += _fold8(dpre * taps[k])
        dpad = jnp.concatenate([dpre, nxt_ref[...]], axis=0)
        dx = dpre * cw_ref[3:4, :]
        for k in range(3):
            dx = dx + pltpu.roll(dpad, t + 8 - (3 - k), 0)[0:t] * cw_ref[k:k + 1, :]
        nxt_ref[...] = dpre[0:8]
        dpb_ref[:, 512:1280] = _c(dx)

    gvec = lambda w: pl.BlockSpec((1, w), lambda g, i: (0, g))
    rev = lambda w: pl.BlockSpec((t, w), lambda g, i: (nt - 1 - i, g))
    return pl.pallas_call(
        body, name="ssd_bwd",
        grid=(SSD_GROUPS, nt),
        in_specs=[rev(FW_B),
                  pl.BlockSpec((8, FW_B), lambda g, i: (jnp.maximum((nt - 1 - i) * (t // 8) - 1, 0), g)),
                  rev(SSD_GW), rev(SSD_GW),
                  pl.BlockSpec((nch, SSD_STATE, SSD_GW), lambda g, i: (nt - 1 - i, 0, g)),
                  gvec(512), gvec(512), gvec(512),
                  pl.BlockSpec((4, 768), lambda g, i: (0, g)), gvec(768), gvec(512)],
        out_specs=[rev(BW_B),
                   pl.BlockSpec((1, 4, 8, 512), lambda g, i: (g, 0, 0, 0)),
                   pl.BlockSpec((1, 5, 8, 768), lambda g, i: (g, 0, 0, 0))],
        out_shape=[jax.ShapeDtypeStruct((s, SSD_GROUPS * BW_B), MXU_DTYPE),
                   jax.ShapeDtypeStruct((SSD_GROUPS, 4, 8, 512), F32),
                   jax.ShapeDtypeStruct((SSD_GROUPS, 5, 8, 768), F32)],
        scratch_shapes=[pltpu.VMEM((SSD_STATE, SSD_GW), F32), pltpu.VMEM((t, 768), F32), pltpu.VMEM((t, 768), F32),
                        pltpu.VMEM((t, 768), F32), pltpu.VMEM((t, SSD_GW), F32), pltpu.VMEM((8, 768), F32)],
        compiler_params=_cparams(("parallel", "arbitrary"), 2 * t * FW_B * 4 + 10 * t * SSD_GW * 4 + 28 * 1024 * 1024),
    )(proj_b, proj_b, dyb, y, hprev, dtb, alog, dsk, cw, cb, nw)


def _rows_call(body, ins, outs, tr, name):
    r = ins[0].shape[0]
    spec = lambda a: pl.BlockSpec((tr, a.shape[1]), lambda i: (i, 0))
    est = 2 * tr * sum(a.shape[1] * jnp.dtype(a.dtype).itemsize for a in list(ins) + list(outs))
    return pl.pallas_call(
        body, name=name, grid=(r // tr,),
        in_specs=[spec(a) for a in ins], out_specs=[spec(o) for o in outs], out_shape=list(outs),
        compiler_params=_cparams(("parallel",), est),
    )(*ins)


def _add_pair(a, b, tr, name, with_bf16):
    def body(a_ref, b_ref, *o_refs):
        v = a_ref[...] + b_ref[...]
        o_refs[0][...] = v
        if with_bf16:
            o_refs[1][...] = v.astype(jnp.bfloat16)

    outs = [jax.ShapeDtypeStruct(a.shape, F32)]
    if with_bf16:
        outs.append(jax.ShapeDtypeStruct(a.shape, jnp.bfloat16))
    return _rows_call(body, [a, b], outs, tr, name)


def _sum_own_recv(own, recv, tr, name):
    r, c = own.shape

    def body(o_ref, r_ref, out_ref):
        v = o_ref[...]
        for j in range(3):
            v = v + r_ref[j].astype(F32)
        out_ref[...] = v

    return pl.pallas_call(
        body, name=name, grid=(r // tr,),
        in_specs=[pl.BlockSpec((tr, c), lambda i: (i, 0)), pl.BlockSpec((3, tr, c), lambda i: (0, i, 0))],
        out_specs=pl.BlockSpec((tr, c), lambda i: (i, 0)),
        out_shape=jax.ShapeDtypeStruct((r, c), F32),
        compiler_params=_cparams(("parallel",), 2 * tr * c * 14),
    )(own, recv)


def _sum_chips(abs4, tr, name):
    r = abs4.shape[1]

    def body(a_ref, out_ref):
        out_ref[...] = ((a_ref[0] + a_ref[1]) + a_ref[2]) + a_ref[3]

    return pl.pallas_call(
        body, name=name, grid=(r // tr,),
        in_specs=[pl.BlockSpec((4, tr, 128), lambda i: (0, i, 0))],
        out_specs=pl.BlockSpec((tr, 128), lambda i: (i, 0)),
        out_shape=jax.ShapeDtypeStruct((r, 128), F32),
        compiler_params=_cparams(("parallel",), 2 * tr * 128 * 20),
    )(abs4)


def _adamw(w, g, m, v, tr, name):
    def body(w_ref, g_ref, m_ref, v_ref, d_ref, nm_ref, nv_ref):
        gv = g_ref[...]
        nm = ADAM_B1 * m_ref[...] + (1.0 - ADAM_B1) * gv
        nv = ADAM_B2 * v_ref[...] + (1.0 - ADAM_B2) * (gv * gv)
        m_hat = nm / (1.0 - ADAM_B1 ** ADAM_STEP)
        v_hat = nv / (1.0 - ADAM_B2 ** ADAM_STEP)
        d_ref[...] = -ADAM_LR * (m_hat / (jnp.sqrt(v_hat) + ADAM_EPS) + ADAM_WD * w_ref[...])
        nm_ref[...] = nm
        nv_ref[...] = nv

    o = jax.ShapeDtypeStruct(w.shape, F32)
    return _rows_call(body, [w, g, m, v], [o, o, o], tr, name)


ANY = pl.BlockSpec(memory_space=pl.ANY)


def _place():
    x, y, c = lax.axis_index("x"), lax.axis_index("y"), lax.axis_index("c")
    others = [(1 - x, y), (x, 1 - y), (1 - x, 1 - y)]
    return x, y, c, 2 * x + y, others


def _remote(src, dst, send, recv, k, to):
    return pltpu.make_async_remote_copy(src_ref=src, dst_ref=dst, send_sem=send.at[k], recv_sem=recv.at[k],
                                        device_id=to, device_id_type=MESH)


def _gather_call(win_b, wout_b, cw8):
    big = [win_b, wout_b]

    def body(win, wout, cw, g_in, g_out, g_cw, send, recv):
        x, y, c, me, others = _place()
        sib = (x, y, 1 - c)
        started = []
        for a, (src, dst) in enumerate(((win, g_in), (wout, g_out))):
            half = src.shape[0] // 2
            mine = pl.ds(c * half, half)
            for j, chip in enumerate(others):
                cp = _remote(src.at[mine], dst.at[me, mine], send, recv, 6 * a + j, (*chip, c))
                cp.start()
                started.append(cp)
        for j, chip in enumerate(others):
            cp = _remote(cw, g_cw.at[me], send, recv, 12 + j, (*chip, c))
            cp.start()
            started.append(cp)
        for a, dst in enumerate((g_in, g_out)):
            half = dst.shape[1] // 2
            mine = pl.ds(c * half, half)
            for j, chip in enumerate(others):
                kj = 2 * chip[0] + chip[1]
                _remote(dst.at[kj, mine], dst.at[kj, mine], send, recv, 6 * a + j, (*chip, c)).wait_recv()
                cp = _remote(dst.at[kj, mine], dst.at[kj, mine], send, recv, 6 * a + 3 + j, sib)
                cp.start()
                started.append(cp)
        for a, dst in enumerate((g_in, g_out)):
            half = dst.shape[1] // 2
            theirs = pl.ds((1 - c) * half, half)
            for j, chip in enumerate(others):
                kj = 2 * chip[0] + chip[1]
                _remote(dst.at[kj, theirs], dst.at[kj, theirs], send, recv, 6 * a + 3 + j, sib).wait_recv()
        for j, chip in enumerate(others):
            kj = 2 * chip[0] + chip[1]
            _remote(cw, g_cw.at[kj], send, recv, 12 + j, (*chip, c)).wait_recv()
        for cp in started:
            cp.wait_send()

    outs = [jax.ShapeDtypeStruct((4,) + a.shape, a.dtype) for a in (win_b, wout_b, cw8)]
    return pl.pallas_call(
        body, name="gather_weights",
        in_specs=[ANY, ANY, ANY], out_specs=[ANY, ANY, ANY], out_shape=outs,
        scratch_shapes=[pltpu.SemaphoreType.DMA((15,)), pltpu.SemaphoreType.DMA((15,))],
    )(win_b, wout_b, cw8)


def _rs_sibling_call(p_in, p_out, vsmall):
    def body(pin, pout, vs, sib_in, sib_out, sib_v, send, recv):
        x, y, c, me, others = _place()
        sib = (x, y, 1 - c)
        cps = []
        for a, (p, sb) in enumerate(((pin, sib_in), (pout, sib_out))):
            half = p.shape[1] // 2
            cps.append(_remote(p.at[:, pl.ds((1 - c) * half, half)], sb, send, recv, a, sib))
        cps.append(_remote(vs, sib_v, send, recv, 2, sib))
        for cp in cps:
            cp.start()
        for cp in cps:
            cp.wait_recv()
        for cp in cps:
            cp.wait_send()

    def halves(p):
        return jax.ShapeDtypeStruct((4, p.shape[1] // 2, p.shape[2]), p.dtype)

    outs = [halves(p_in), halves(p_out), jax.ShapeDtypeStruct(vsmall.shape, vsmall.dtype)]
    return pl.pallas_call(
        body, name="rs_sibling",
        in_specs=[ANY] * 3, out_specs=[ANY] * 3, out_shape=outs,
        scratch_shapes=[pltpu.SemaphoreType.DMA((3,)), pltpu.SemaphoreType.DMA((3,))],
    )(p_in, p_out, vsmall)


def _rs_chips_call(sb_in, sb_out, chip_v):
    def body(sbin, sbout, cv, rc_in, rc_out, abs_v, send, recv):
        x, y, c, me, others = _place()
        cps = []
        for j, chip in enumerate(others):
            kj = 2 * chip[0] + chip[1]
            to = (*chip, c)
            cps.append(_remote(sbin.at[kj], rc_in.at[j], send, recv, j, to))
            cps.append(_remote(sbout.at[kj], rc_out.at[j], send, recv, 3 + j, to))
            cps.append(_remote(cv, abs_v.at[me], send, recv, 6 + j, to))
        for cp in cps:
            cp.start()
        for j, chip in enumerate(others):
            kj = 2 * chip[0] + chip[1]
            cps[3 * j].wait_recv()
            cps[3 * j + 1].wait_recv()
            _remote(cv, abs_v.at[kj], send, recv, 6 + j, (*chip, c)).wait_recv()
        for cp in cps:
            cp.wait_send()

    outs = [jax.ShapeDtypeStruct((3,) + sb_in.shape[1:], sb_in.dtype), jax.ShapeDtypeStruct((3,) + sb_out.shape[1:], sb_out.dtype),
            jax.ShapeDtypeStruct((4,) + chip_v.shape, F32)]
    return pl.pallas_call(
        body, name="rs_chips",
        in_specs=[ANY] * 3, out_specs=[ANY] * 3, out_shape=outs,
        scratch_shapes=[pltpu.SemaphoreType.DMA((9,)), pltpu.SemaphoreType.DMA((9,))],
    )(sb_in, sb_out, chip_v)


def _rs_join_call(f_in, f_out):
    def body(fin, fout, full_in, full_out, send, recv):
        x, y, c, me, others = _place()
        sib = (x, y, 1 - c)
        cps = []
        for a, (f, full) in enumerate(((fin, full_in), (fout, full_out))):
            half = f.shape[0]
            cp = _remote(f, full.at[pl.ds(c * half, half)], send, recv, a, sib)
            cp.start()
            cps.append(cp)
        for a, (f, full) in enumerate(((fin, full_in), (fout, full_out))):
            half = f.shape[0]
            _remote(f, full.at[pl.ds((1 - c) * half, half)], send, recv, a, sib).wait_recv()
        for cp in cps:
            cp.wait_send()

    outs = [jax.ShapeDtypeStruct((2 * f.shape[0], f.shape[1]), F32) for f in (f_in, f_out)]
    return pl.pallas_call(
        body, name="rs_join",
        in_specs=[ANY] * 2, out_specs=[ANY] * 2, out_shape=outs,
        scratch_shapes=[pltpu.SemaphoreType.DMA((2,)), pltpu.SemaphoreType.DMA((2,))],
    )(f_in, f_out)


def _pack(arrs):
    parts = []
    for a in arrs:
        f = a.reshape(-1).astype(F32)
        pad = (-f.shape[0]) % 1024
        parts.append(jnp.pad(f, (0, pad)).reshape(-1, 128))
    return jnp.concatenate(parts, axis=0)


def _unpack(packed, shapes):
    out, row = [], 0
    for shp in shapes:
        n = 1
        for d in shp:
            n *= d
        rows = (n + 1023) // 1024 * 8
        out.append(packed[row:row + rows].reshape(-1)[:n].reshape(shp))
        row += rows
    return out


def _expand_heads(v32):
    return jnp.repeat(v32.reshape(32), HEADDIM).reshape(1, D_MODEL)


def kernel(x, norm_w, w_in, gate_b, sgu_norm_g, sgu_norm_b, sgu_w, sgu_b, conv_w, conv_b, dt_bias, A_log, D_skip, ssd_norm_w, w_out, final_norm_w, loss_target, m_norm_w, m_w_in, m_gate_b, m_sgu_norm_g, m_sgu_norm_b, m_sgu_w, m_sgu_b, m_conv_w, m_conv_b, m_dt_bias, m_A_log, m_D_skip, m_ssd_norm_w, m_w_out, m_final_norm_w, v_norm_w, v_w_in, v_gate_b, v_sgu_norm_g, v_sgu_norm_b, v_sgu_w, v_sgu_b, v_conv_w, v_conv_b, v_dt_bias, v_A_log, v_D_skip, v_ssd_norm_w, v_w_out, v_final_norm_w):
    s = x.shape[1]
    x2 = x.reshape(s, D_MODEL)
    tgt = loss_target.reshape(s, D_MODEL)
    t_ssd, t_tok, t_out, t_row = min(T_SSD, s), min(T_TOK, s), min(T_OUT, s), min(T_ROW, s)
    tm_mm, tk_dw = min(TM_MM, s), min(TK_DW, s)
    chip = 2 * lax.axis_index("x") + lax.axis_index("y")

    cw8 = jnp.pad(conv_w[0], ((0, 4), (0, 0)))
    win_b, wout_b = _c(w_in[0]), _c(w_out[0])
    g_in, g_out, g_cw = _gather_call(win_b, wout_b, cw8)
    g_in = lax.dynamic_update_index_in_dim(g_in, win_b, chip, 0)
    g_out = lax.dynamic_update_index_in_dim(g_out, wout_b, chip, 0)
    g_cw = lax.dynamic_update_index_in_dim(g_cw, cw8, chip, 0)
    wref = jnp.transpose(g_in, (1, 0, 2)).reshape(D_MODEL, IN_W)
    w_out_full = g_out.reshape(D_MODEL, D_MODEL)
    conv_w_full = jnp.transpose(g_cw[:, 0:4, :], (1, 0, 2)).reshape(4, 3072)

    w_a = jnp.concatenate([wref[:, 0:6144], wref[:, 11296:15392]], axis=1)
    fw, bw = [], []
    for g in range(SSD_GROUPS):
        xs_g = wref[:, 8192 + 512 * g:8192 + 512 * g + 512]
        b_g = wref[:, 10240 + 128 * g:10240 + 128 * g + 128]
        c_g = wref[:, 10752 + 128 * g:10752 + 128 * g + 128]
        zb_g = wref[:, 6144 + 512 * g:6144 + 512 * g + 512]
        dt_g = wref[:, 11264 + 8 * g:11264 + 8 * g + 8]
        fw += [xs_g, b_g, c_g, jnp.repeat(dt_g, HEADDIM, axis=1), zb_g]
        bw += [zb_g, xs_g, b_g, c_g, jnp.pad(dt_g, ((0, 0), (0, 120)))]
    w_b = jnp.concatenate(fw, axis=1)
    wt_a = w_a.T
    wt_b = jnp.concatenate(bw, axis=1).T

    def group_cols(full_xs, full_bc):
        parts = []
        for g in range(SSD_GROUPS):
            parts += [full_xs[:, 512 * g:512 * g + 512], full_bc[:, 128 * g:128 * g + 128], full_bc[:, 512 + 128 * g:512 + 128 * g + 128]]
        return jnp.concatenate(parts, axis=1)

    cw_g = group_cols(conv_w_full[:, 0:2048], conv_w_full[:, 2048:3072])
    cb_g = group_cols(conv_b[:, 0:2048], conv_b[:, 2048:3072])
    dtb_e, alog_e, dsk_e = _expand_heads(dt_bias), _expand_heads(A_log), _expand_heads(D_skip)

    pos_chunk = jnp.arange(SGU_BLOCK) // CHUNK
    smask = pos_chunk[None, :] <= pos_chunk[:, None]
    wm_f = jnp.where(smask[None], sgu_w[0], 0.0)
    wm = _c(wm_f)
    wmt = _c(jnp.transpose(wm_f, (0, 2, 1)))
    bias_full = jnp.repeat(sgu_b[0].T, D_MODEL // SGU_GROUPS, axis=1)
    fnw = final_norm_w.reshape(1, D_MODEL)

    xn, xnt = _norm_call(x2, norm_w, t_row)
    proj_a = _mm(xn, w_a, tm=tm_mm, tn=1024, tk=D_MODEL, name="in_proj_a")
    proj_b = _mm(xn, w_b, tm=tm_mm, tn=1024, tk=D_MODEL, name="in_proj_b")
    y_ssd, y_b, hprev = _ssd_fwd_call(proj_b, dtb_e, alog_e, dsk_e, cw_g, cb_g, ssd_norm_w, t_ssd)
    y_a, merged, merged_t = _tok_fwd_call(proj_a, y_b, gate_b, sgu_norm_g, sgu_norm_b, wm, bias_full, t_tok)
    dh, dh_b, dmerged, loss_t, dfw8 = _out_call(merged, x2, tgt, w_out_full, fnw, t_out)

    dproj_a, dy_b, dgb8, dgam8, dbeta8, dbfull, dws = _tok_bwd_call(
        proj_a, dmerged, y_a, y_b, gate_b, sgu_norm_g, sgu_norm_b, wm, wmt, bias_full, t_tok)
    dproj_b, a512, a768 = _ssd_bwd_call(proj_b, dy_b, y_ssd, hprev, dtb_e, alog_e, dsk_e, cw_g, cb_g, ssd_norm_w, t_ssd)
    dw_a = _mm(xnt, dproj_a, tm=D_MODEL, tn=1024, tk=tk_dw, name="dw_in_a")
    dw_b = _mm(xnt, dproj_b, tm=D_MODEL, tn=BW_B, tk=tk_dw, name="dw_in_b")
    dw_out_p = _mm(merged_t, dh_b, tm=D_MODEL, tn=1024, tk=tk_dw, name="dw_out")
    dxn = _dx_mm(dproj_a, wt_a, dproj_b, wt_b, tm=t_row)
    grad_x, dnw8 = _gradx_call(x2, dxn, dh, norm_w, t_row)

    gb = lambda a, b: jnp.concatenate([dw_b[:, BW_B * g + a:BW_B * g + b] for g in range(SSD_GROUPS)], axis=1)
    dw_ref = jnp.concatenate([dw_a[:, 0:6144], gb(0, 512), gb(512, 1024), gb(1024, 1152), gb(1152, 1280),
                              gb(1280, 1288), dw_a[:, 6144:10240]], axis=1)
    p_in = jnp.transpose(dw_ref.reshape(D_MODEL, 4, SHARD_W), (1, 0, 2))
    p_out = dw_out_p.reshape(4, D_MODEL // 4, D_MODEL)

    s512 = jnp.sum(a512, axis=2)
    heads = lambda v: jnp.sum(v.reshape(32, HEADDIM), axis=1).reshape(1, 32)
    d_ssd_nw = s512[:, 0].reshape(1, D_MODEL)
    d_dskip = heads(s512[:, 1].reshape(D_MODEL))
    d_alog = heads(s512[:, 2].reshape(D_MODEL)) * (1.0 / HEADDIM) * (-jnp.exp(A_log))
    d_dtb = heads(s512[:, 3].reshape(D_MODEL))
    s768 = jnp.sum(a768, axis=2)
    ungroup = lambda v: jnp.concatenate([v[g, :, 0:512] for g in range(4)] + [v[g, :, 512:640] for g in range(4)]
                                        + [v[g, :, 640:768] for g in range(4)], axis=1)
    d_cw = ungroup(s768[:, 0:4])
    d_cb = ungroup(s768[:, 4:5])
    d_sgu_b = jnp.sum(dbfull.reshape(128, SGU_GROUPS, 128), axis=2).T.reshape(1, SGU_GROUPS, 128)
    d_sgu_w = jnp.where(smask[None], dws, 0.0).reshape(1, SGU_GROUPS, 128, 128)
    fold = lambda a8: jnp.sum(a8, axis=0, keepdims=True)
    small_local = [fold(dnw8), fold(dgb8), fold(dgam8), fold(dbeta8), d_sgu_w, d_sgu_b, d_cw, d_cb,
                   d_dtb, d_alog, d_dskip, d_ssd_nw, fold(dfw8).reshape(D_MODEL), jnp.sum(loss_t[:, 0, 0]).reshape(1)]
    small_shapes = [a.shape for a in small_local]
    v_local = _pack(small_local)

    core = lax.axis_index("c")
    hr_i, hr_o = D_MODEL // 2, D_MODEL // 8
    sib_i, sib_o, sib_v = _rs_sibling_call(p_in, p_out, v_local)
    own_i = lax.dynamic_slice_in_dim(p_in, core * hr_i, hr_i, axis=1)
    own_o = lax.dynamic_slice_in_dim(p_out, core * hr_o, hr_o, axis=1)
    s1_i, s1b_i = _add_pair(own_i.reshape(4 * hr_i, SHARD_W), sib_i.reshape(4 * hr_i, SHARD_W), 256, "rs_add_in", True)
    s1_o, s1b_o = _add_pair(own_o.reshape(4 * hr_o, D_MODEL), sib_o.reshape(4 * hr_o, D_MODEL), 256, "rs_add_out", True)
    (chip_v,) = _add_pair(v_local, sib_v, v_local.shape[0], "ar_add_small", False)
    r_i, r_o, abs_v = _rs_chips_call(s1b_i.reshape(4, hr_i, SHARD_W), s1b_o.reshape(4, hr_o, D_MODEL), chip_v)
    abs_v = lax.dynamic_update_index_in_dim(abs_v, chip_v, chip, 0)
    o_i = lax.dynamic_index_in_dim(s1_i.reshape(4, hr_i, SHARD_W), chip, 0, keepdims=False)
    o_o = lax.dynamic_index_in_dim(s1_o.reshape(4, hr_o, D_MODEL), chip, 0, keepdims=False)
    f_i = _sum_own_recv(o_i, r_i, 256, "rs_sum_in")
    f_o = _sum_own_recv(o_o, r_o, 256, "rs_sum_out")
    g_w_in, g_w_out = _rs_join_call(f_i, f_o)
    g_w_in = lax.dynamic_update_slice_in_dim(g_w_in, f_i, core * hr_i, axis=0)
    g_w_out = lax.dynamic_update_slice_in_dim(g_w_out, f_o, core * hr_o, axis=0)
    total_v = _sum_chips(abs_v, abs_v.shape[1], "ar_sum_small")
    (g_nw, g_gb, g_gam, g_beta, g_sw, g_sb, g_cw_full, g_cb, g_dtb, g_alog, g_dsk, g_snw, g_fnw, loss1) = _unpack(total_v, small_shapes)
    g_cw_shard = lax.dynamic_slice(g_cw_full, (0, chip * 768), (4, 768)).reshape(1, 4, 768)
    loss = loss1.reshape(())

    d_win, nm_win, nv_win = _adamw(w_in[0], g_w_in, m_w_in[0], v_w_in[0], 128, "adamw_w_in")
    d_wout, nm_wout, nv_wout = _adamw(w_out[0], g_w_out, m_w_out[0], v_w_out[0], 128, "adamw_w_out")
    small_w = [norm_w, gate_b, sgu_norm_g, sgu_norm_b, sgu_w, sgu_b, conv_w, conv_b, dt_bias, A_log, D_skip, ssd_norm_w, final_norm_w]
    small_m = [m_norm_w, m_gate_b, m_sgu_norm_g, m_sgu_norm_b, m_sgu_w, m_sgu_b, m_conv_w, m_conv_b, m_dt_bias, m_A_log, m_D_skip, m_ssd_norm_w, m_final_norm_w]
    small_v = [v_norm_w, v_gate_b, v_sgu_norm_g, v_sgu_norm_b, v_sgu_w, v_sgu_b, v_conv_w, v_conv_b, v_dt_bias, v_A_log, v_D_skip, v_ssd_norm_w, v_final_norm_w]
    small_g = [g_nw, g_gb, g_gam, g_beta, g_sw, g_sb, g_cw_shard, g_cb, g_dtb, g_alog, g_dsk, g_snw, g_fnw]
    shapes_w = [a.shape for a in small_w]
    small_g = [a.reshape(shp) for a, shp in zip(small_g, shapes_w)]
    pw = _pack(small_w)
    pd, pm, pv = _adamw(pw, _pack(small_g), _pack(small_m), _pack(small_v), pw.shape[0], "adamw_small")
    d_small, nm_small, nv_small = _unpack(pd, shapes_w), _unpack(pm, shapes_w), _unpack(pv, shapes_w)

    def with_big(small, win, wout):
        o = list(small)
        return o[0:1] + [win.reshape(1, D_MODEL, SHARD_W)] + o[1:12] + [wout.reshape(1, D_MODEL // 4, D_MODEL)] + o[12:13]

    grads = with_big(small_g, g_w_in, g_w_out)
    deltas = with_big(d_small, d_win, d_wout)
    new_m = with_big(nm_small, nm_win, nm_wout)
    new_v = with_big(nv_small, nv_win, nv_wout)
    return (loss, grad_x.reshape(1, s, D_MODEL), *grads, *deltas, *new_m, *new_v)
```

```python
import functools

import jax
import jax.numpy as jnp
from jax import lax
from jax.experimental import pallas as pl
from jax.experimental.pallas import tpu as pltpu

F32 = jnp.float32
MXU_DTYPE = jnp.bfloat16

D_MODEL = 2048
EPS = 1e-5
CHUNK = 64
SGU_BLOCK = 128
SGU_GROUPS = 16
SSD_GROUPS = 4
SSD_GW = 512
SSD_STATE = 128
HEADDIM = 64
IN_W = 15392
SHARD_W = IN_W // 4
FW_B = 1792
BW_B = 1408
NA = 10240

ADAM_LR = 0.001
ADAM_B1 = 0.9
ADAM_B2 = 0.999
ADAM_EPS = 1e-08
ADAM_WD = 0.01
ADAM_STEP = 10

T_SSD = 256
T_TOK = 128
T_OUT = 256
T_ROW = 512
TM_MM = 1024
TK_DW = 1024
VMEM_CAP = 60 * 1024 * 1024
MESH = pl.DeviceIdType.MESH


def _cparams(sem, est_bytes):
    lim = int(min(VMEM_CAP, max(32 * 1024 * 1024, est_bytes + 12 * 1024 * 1024)))
    return pltpu.CompilerParams(dimension_semantics=sem, vmem_limit_bytes=lim)


def _c(x):
    return x.astype(MXU_DTYPE)


def _dot(a, b):
    return jnp.dot(a, b, preferred_element_type=F32)


def _dot_nt(a, b):
    return lax.dot_general(a, b, (((1,), (1,)), ((), ())), preferred_element_type=F32)


def _dot_tn(a, b):
    return lax.dot_general(a, b, (((0,), (0,)), ((), ())), preferred_element_type=F32)


def _split(x, n):
    parts, r = [], x
    for _ in range(n):
        p = _c(r)
        parts.append(p)
        r = r - p.astype(F32)
    return parts


def _dot01_l(m01, x, n):
    acc = None
    for p in _split(x, n):
        t = _dot(m01, p)
        acc = t if acc is None else acc + t
    return acc


def _dot01_r(x, m01, n):
    acc = None
    for p in _split(x, n):
        t = _dot(p, m01)
        acc = t if acc is None else acc + t
    return acc


def _sigmoid(x):
    return 1.0 / (1.0 + jnp.exp(-x))


def _fold8(x):
    r, w = x.shape
    return jnp.sum(x.reshape(r // 8, 8, w), axis=0)


def _iota(shape, dim):
    return lax.broadcasted_iota(jnp.int32, shape, dim)


def _ssd_masks():
    l64 = _iota((CHUNK, SSD_GW), 0)
    s64 = jnp.bitwise_and(_iota((CHUNK, SSD_GW), 1), CHUNK - 1)
    diag = l64 == s64
    causal = l64 >= s64
    row_last = l64 == CHUNK - 1
    r4 = lax.shift_right_logical(_iota((256, 256), 0), 6)
    c4 = lax.shift_right_logical(_iota((256, 256), 1), 6)
    mask4 = r4 == c4
    tl = _iota((CHUNK, CHUNK), 0)
    tc = _iota((CHUNK, CHUNK), 1)
    tri = _c(jnp.where(tc <= tl, 1.0, 0.0))
    trit = _c(jnp.where(tc >= tl, 1.0, 0.0))
    return diag, causal, row_last, mask4, tri, trit


def _ssd_common(xs, bm, cm, dt, a_neg, masks):
    diag, causal, row_last, mask4, tri, trit = masks
    a = dt * a_neg
    acs = _dot01_l(tri, a, 3)
    row_e = jnp.sum(jnp.where(diag, acs, 0.0), axis=0, keepdims=True)
    seg = acs - row_e
    lm = jnp.exp(jnp.where(causal, seg, -1e30))
    bb, cb = _c(bm), _c(cm)
    brep = jnp.concatenate([bb] * 8, axis=0)
    cbrep = _dot_nt(cb, brep)
    m = cbrep * lm
    xdt = xs * dt
    acs_last = jnp.sum(jnp.where(row_last, acs, 0.0), axis=0, keepdims=True)
    dec = jnp.exp(acs_last - acs)
    eacs = jnp.exp(acs)
    cd = jnp.exp(acs_last)
    return dict(lm=lm, bb=bb, cb=cb, brep=brep, m=m, xdt=xdt, dec=dec, eacs=eacs, cd=cd)


def _blockdiag4(xb, mask4):
    return jnp.where(mask4, jnp.concatenate([xb] * 4, axis=0), jnp.zeros((), xb.dtype))


def _ssd_chunk_fwd(xs, bm, cm, dt, a_neg, d_skip, ht, masks):
    q = _ssd_common(xs, bm, cm, dt, a_neg, masks)
    mask4 = masks[3]
    mb, xdtb = _c(q["m"]), _c(q["xdt"])
    yd = []
    for blk in range(2):
        sl = slice(256 * blk, 256 * blk + 256)
        yd.append(_dot(mb[:, sl], _blockdiag4(xdtb[:, sl], mask4)))
    y_diag = jnp.concatenate(yd, axis=1)
    p = _dot(q["cb"], _c(ht))
    y = y_diag + p * q["eacs"] + xs * d_skip
    st = _dot_tn(q["bb"], _c(q["xdt"] * q["dec"]))
    return y, ht * q["cd"] + st


def _ssd_chunk_bwd(xs, bm, cm, dt, sig, a_neg, d_skip, hprev, dht, dy, masks):
    diag, causal, row_last, mask4, tri, trit = masks
    q = _ssd_common(xs, bm, cm, dt, a_neg, masks)
    lm, bb, cb, brep, m, xdt, dec, eacs, cd = (q[k] for k in ("lm", "bb", "cb", "brep", "m", "xdt", "dec", "eacs", "cd"))
    hb = _c(hprev)
    yoff = _dot(cb, hb) * eacs
    dyb = _c(dy)
    dpb = _c(dy * eacs)
    d_c = _dot_nt(dpb, hb)
    dh_y = _dot_tn(cb, dpb)
    mb, xdtb = _c(m), _c(xdt)
    dm_parts, dxdt_parts = [], []
    for blk in range(2):
        sl = slice(256 * blk, 256 * blk + 256)
        bd = _blockdiag4(xdtb[:, sl], mask4)
        dm_parts.append(_dot_nt(dyb[:, sl], bd))
        dxf = jnp.where(mask4, _dot_tn(mb[:, sl], dyb[:, sl]), 0.0)
        dxdt_parts.append(dxf[0:64] + dxf[64:128] + dxf[128:192] + dxf[192:256])
    dm = jnp.concatenate(dm_parts, axis=1)
    dxdt = jnp.concatenate(dxdt_parts, axis=1)
    dcbb = _c(dm * lm)
    g = dm * m
    d_c = d_c + _dot(dcbb, brep)
    dbrep = _dot_tn(dcbb, cb)
    d_b = dbrep[0:64]
    for r in range(1, 8):
        d_b = d_b + dbrep[64 * r:64 * r + 64]
    dhtb = _c(dht)
    dxd = _dot(bb, dhtb)
    xd = xdt * dec
    dxdt = dxdt + dxd * dec
    tq = dxd * xd
    d_b = d_b + _dot_nt(_c(xd), dhtb)
    dcd = jnp.sum(dht * hprev, axis=0, keepdims=True)
    col_g = jnp.sum(g, axis=0, keepdims=True)
    last = jnp.sum(tq, axis=0, keepdims=True) + dcd * cd
    qq = g - jnp.where(diag, col_g, 0.0) + dy * yoff - tq + jnp.where(row_last, last, 0.0)
    bd4 = _c(jnp.where(mask4, 1.0, 0.0))
    dacs = jnp.concatenate([_dot01_r(qq[:, 256 * b:256 * b + 256], bd4, 2) for b in range(2)], axis=1)
    da = _dot01_l(trit, dacs, 2)
    ddt = dxdt * xs + da * (a_neg * (1.0 / HEADDIM))
    dxs = dxdt * dt + dy * d_skip
    return dxs, d_b, d_c, ddt * sig, dht * cd + dh_y, da * dt, dy * xs


def _softplus(x):
    return jnp.maximum(x, 0.0) + jnp.log1p(jnp.exp(-jnp.abs(x)))


def _conv_taps(xpad, t):
    taps = []
    for k in range(4):
        sh = 3 - k
        v = xpad if sh == 0 else pltpu.roll(xpad, sh, 0)
        taps.append(v[8:8 + t])
    return taps


def _norm_call(x, norm_w, tm):
    s = x.shape[0]

    def body(x_ref, w_ref, xn_ref, xnt_ref):
        xv = x_ref[...]
        r = lax.rsqrt(jnp.mean(xv * xv, axis=-1, keepdims=True) + EPS)
        xn = xv * r * w_ref[...]
        xn_ref[...] = _c(xn)
        xnt_ref[...] = _c(xn.T)

    return pl.pallas_call(
        body, name="rmsnorm_in",
        grid=(s // tm,),
        in_specs=[pl.BlockSpec((tm, D_MODEL), lambda i: (i, 0)), pl.BlockSpec((1, D_MODEL), lambda i: (0, 0))],
        out_specs=[pl.BlockSpec((tm, D_MODEL), lambda i: (i, 0)), pl.BlockSpec((D_MODEL, tm), lambda i: (0, i))],
        out_shape=[jax.ShapeDtypeStruct((s, D_MODEL), MXU_DTYPE), jax.ShapeDtypeStruct((D_MODEL, s), MXU_DTYPE)],
        compiler_params=_cparams(("parallel",), 2 * tm * D_MODEL * 12),
    )(x, norm_w)


def _mm(a, b, *, tm, tn, tk, name):
    m, k = a.shape
    n = b.shape[1]
    nk = k // tk
    assert m % tm == 0 and n % tn == 0 and k % tk == 0, (a.shape, b.shape, tm, tn, tk)

    def body(a_ref, b_ref, o_ref):
        if nk == 1:
            o_ref[...] = _dot(a_ref[...], b_ref[...])
        else:
            @pl.when(pl.program_id(2) == 0)
            def _():
                o_ref[...] = jnp.zeros_like(o_ref)

            o_ref[...] += _dot(a_ref[...], b_ref[...])

    isz = jnp.dtype(a.dtype).itemsize
    est = 2 * (tm * tk + tk * tn) * isz + 2 * tm * tn * 4
    return pl.pallas_call(
        body, name=name,
        grid=(m // tm, n // tn, nk),
        in_specs=[pl.BlockSpec((tm, tk), lambda i, j, kk: (i, kk)), pl.BlockSpec((tk, tn), lambda i, j, kk: (kk, j))],
        out_specs=pl.BlockSpec((tm, tn), lambda i, j, kk: (i, j)),
        out_shape=jax.ShapeDtypeStruct((m, n), F32),
        compiler_params=_cparams(("parallel", "parallel", "arbitrary"), est),
    )(a, b)


def _dx_rs_call(dpa, wta, dpb, wtb, sb_in, sb_out, chip_v, *, tm):
    s = dpa.shape[0]
    tka, tkb = 1024, BW_B
    nka, nkb = dpa.shape[1] // tka, dpb.shape[1] // tkb
    ni, nk = s // tm, nka + nkb

    def body(a_ref, wa_ref, b_ref, wb_ref, sbin, sbout, cv, o_ref, rc_in, rc_out, abs_v, send, recv):
        i, kk = pl.program_id(0), pl.program_id(1)

        def copies():
            x, y, c, me, others = _place()
            sends, recvs = [], []
            for j, chip in enumerate(others):
                kj = 2 * chip[0] + chip[1]
                to = (*chip, c)
                sends += [_remote(sbin.at[kj], rc_in.at[j], send, recv, j, to),
                          _remote(sbout.at[kj], rc_out.at[j], send, recv, 3 + j, to),
                          _remote(cv, abs_v.at[me], send, recv, 6 + j, to)]
                recvs += [sends[-3], sends[-2], _remote(cv, abs_v.at[kj], send, recv, 6 + j, to)]
            return sends, recvs

        @pl.when((i == 0) & (kk == 0))
        def _():
            for cp in copies()[0]:
                cp.start()

        @pl.when(kk == 0)
        def _():
            o_ref[...] = jnp.zeros_like(o_ref)

        @pl.when(kk < nka)
        def _():
            o_ref[...] += _dot(a_ref[...], wa_ref[...])

        @pl.when(kk >= nka)
        def _():
            o_ref[...] += _dot(b_ref[...], wb_ref[...])

        @pl.when((i == ni - 1) & (kk == nk - 1))
        def _():
            sends, recvs = copies()
            for cp in recvs:
                cp.wait_recv()
            for cp in sends:
                cp.wait_send()

    isz = jnp.dtype(dpa.dtype).itemsize
    est = 2 * isz * (tm * tka + tka * D_MODEL + tm * tkb + tkb * D_MODEL) + 2 * tm * D_MODEL * 4
    outs = [jax.ShapeDtypeStruct((s, D_MODEL), F32),
            jax.ShapeDtypeStruct((3,) + sb_in.shape[1:], sb_in.dtype), jax.ShapeDtypeStruct((3,) + sb_out.shape[1:], sb_out.dtype),
            jax.ShapeDtypeStruct((4,) + chip_v.shape, F32)]
    return pl.pallas_call(
        body, name="dx_matmul_rs_chips",
        grid=(ni, nk),
        in_specs=[
            pl.BlockSpec((tm, tka), lambda i, kk: (i, jnp.minimum(kk, nka - 1))),
            pl.BlockSpec((tka, D_MODEL), lambda i, kk: (jnp.minimum(kk, nka - 1), 0)),
            pl.BlockSpec((tm, tkb), lambda i, kk: (i, jnp.maximum(kk - nka, 0))),
            pl.BlockSpec((tkb, D_MODEL), lambda i, kk: (jnp.maximum(kk - nka, 0), 0)),
            ANY, ANY, ANY,
        ],
        out_specs=[pl.BlockSpec((tm, D_MODEL), lambda i, kk: (i, 0)), ANY, ANY, ANY],
        out_shape=outs,
        scratch_shapes=[pltpu.SemaphoreType.DMA((9,)), pltpu.SemaphoreType.DMA((9,))],
        compiler_params=_cparams(("arbitrary", "arbitrary"), est),
    )(dpa, wta, dpb, wtb, sb_in, sb_out, chip_v)


def _gradx_call(x, dxn, dh, norm_w, tm):
    s = x.shape[0]

    def body(x_ref, g_ref, dh_ref, w_ref, gx_ref, dw_ref):
        @pl.when(pl.program_id(0) == 0)
        def _():
            dw_ref[...] = jnp.zeros_like(dw_ref)

        xv, gv = x_ref[...], g_ref[...]
        r = lax.rsqrt(jnp.mean(xv * xv, axis=-1, keepdims=True) + EPS)
        gw = gv * w_ref[...]
        gx_ref[...] = r * gw - xv * (r * r * r) * jnp.mean(xv * gw, axis=-1, keepdims=True) + dh_ref[...]
        dw_ref[...] += _fold8(gv * (xv * r))

    row = pl.BlockSpec((tm, D_MODEL), lambda i: (i, 0))
    return pl.pallas_call(
        body, name="grad_x",
        grid=(s // tm,),
        in_specs=[row, row, row, pl.BlockSpec((1, D_MODEL), lambda i: (0, 0))],
        out_specs=[row, pl.BlockSpec((8, D_MODEL), lambda i: (0, 0))],
        out_shape=[jax.ShapeDtypeStruct((s, D_MODEL), F32), jax.ShapeDtypeStruct((8, D_MODEL), F32)],
        compiler_params=_cparams(("arbitrary",), 2 * tm * D_MODEL * 16),
    )(x, dxn, dh, norm_w)


def _layernorm_stats(v):
    mu = jnp.mean(v, axis=-1, keepdims=True)
    vc = v - mu
    var = jnp.mean(vc * vc, axis=-1, keepdims=True)
    return vc * lax.rsqrt(var + EPS), lax.rsqrt(var + EPS)


def _tok_fwd_call(proj_a, y_b, gate_b, sgu_g, sgu_beta, wm, bias_full, t):
    s = proj_a.shape[0]

    def body(pa_ref, yb_ref, gb_ref, g_ref, be_ref, wm_ref, bf_ref, ya_ref, mg_ref, mgt_ref, mix_ref):
        u = pa_ref[:, 0:2048]
        v = pa_ref[:, 2048:4096]
        za = pa_ref[:, 4096:6144]
        xhat, _ = _layernorm_stats(v)
        vnb = _c(xhat * g_ref[...] + be_ref[...])
        for gi in range(SGU_GROUPS):
            sl = slice(128 * gi, 128 * gi + 128)
            mix_ref[:, sl] = _dot(wm_ref[gi], vnb[:, sl])
        mixed = mix_ref[...] + bf_ref[...]
        y_a = u * mixed * (za * _sigmoid(za))
        g0 = _sigmoid(pa_ref[:, 6144:8192] + gb_ref[:, 0:2048])
        g1 = _sigmoid(pa_ref[:, 8192:10240] + gb_ref[:, 2048:4096])
        merged = g0 * y_a + g1 * yb_ref[...]
        ya_ref[...] = y_a
        mg_ref[...] = _c(merged)
        mgt_ref[...] = _c(merged.T)

    row = pl.BlockSpec((t, D_MODEL), lambda i: (i, 0))
    vec = lambda w: pl.BlockSpec((1, w), lambda i: (0, 0))
    return pl.pallas_call(
        body, name="tok_fwd",
        grid=(s // t,),
        in_specs=[pl.BlockSpec((t, NA), lambda i: (i, 0)), row, vec(4096), vec(2048), vec(2048),
                  pl.BlockSpec((SGU_GROUPS, 128, 128), lambda i: (0, 0, 0)), pl.BlockSpec((128, D_MODEL), lambda i: (0, 0))],
        out_specs=[row, row, pl.BlockSpec((D_MODEL, t), lambda i: (0, i))],
        out_shape=[jax.ShapeDtypeStruct((s, D_MODEL), F32), jax.ShapeDtypeStruct((s, D_MODEL), MXU_DTYPE),
                   jax.ShapeDtypeStruct((D_MODEL, s), MXU_DTYPE)],
        scratch_shapes=[pltpu.VMEM((t, D_MODEL), F32)],
        compiler_params=_cparams(("parallel",), 2 * t * NA * 4 + 12 * t * D_MODEL * 4),
    )(proj_a, y_b, gate_b, sgu_g, sgu_beta, wm, bias_full)


def _tok_bwd_call(proj_a, dmerged, y_a, y_b, gate_b, sgu_g, sgu_beta, wm, wmt, bias_full, t):
    s = proj_a.shape[0]

    def body(pa_ref, dm_ref, ya_ref, yb_ref, gb_ref, g_ref, be_ref, wm_ref, wmt_ref, bf_ref,
             dpa_ref, dyb_ref, dgb_ref, dgam_ref, dbeta_ref, dbf_ref, dws_ref, mix_ref, dvn_ref):
        @pl.when(pl.program_id(0) == 0)
        def _():
            dgb_ref[...] = jnp.zeros_like(dgb_ref)
            dgam_ref[...] = jnp.zeros_like(dgam_ref)
            dbeta_ref[...] = jnp.zeros_like(dbeta_ref)
            dbf_ref[...] = jnp.zeros_like(dbf_ref)
            dws_ref[...] = jnp.zeros_like(dws_ref)

        u = pa_ref[:, 0:2048]
        v = pa_ref[:, 2048:4096]
        za = pa_ref[:, 4096:6144]
        xhat, rstd = _layernorm_stats(v)
        vnb = _c(xhat * g_ref[...] + be_ref[...])
        for gi in range(SGU_GROUPS):
            sl = slice(128 * gi, 128 * gi + 128)
            mix_ref[:, sl] = _dot(wm_ref[gi], vnb[:, sl])
        mixed = mix_ref[...] + bf_ref[...]
        sig = _sigmoid(za)
        sz = za * sig
        dm = dm_ref[...]
        y_a = ya_ref[...]
        g0 = _sigmoid(pa_ref[:, 6144:8192] + gb_ref[:, 0:2048])
        g1 = _sigmoid(pa_ref[:, 8192:10240] + gb_ref[:, 2048:4096])
        dgl0 = dm * y_a * g0 * (1.0 - g0)
        dgl1 = dm * yb_ref[...] * g1 * (1.0 - g1)
        dyb_ref[...] = dm * g1
        dya = dm * g0
        dpa_ref[:, 6144:8192] = _c(dgl0)
        dpa_ref[:, 8192:10240] = _c(dgl1)
        dgb_ref[:, 0:2048] += _fold8(dgl0)
        dgb_ref[:, 2048:4096] += _fold8(dgl1)
        dpa_ref[:, 0:2048] = _c(dya * mixed * sz)
        dpa_ref[:, 4096:6144] = _c(dya * (u * mixed) * (sig * (1.0 + za * (1.0 - sig))))
        dmixed = dya * u * sz
        dbf_ref[...] += dmixed
        dmb = _c(dmixed)
        for gi in range(SGU_GROUPS):
            sl = slice(128 * gi, 128 * gi + 128)
            dvn_ref[:, sl] = _dot(wmt_ref[gi], dmb[:, sl])
            dws_ref[gi] += _dot_nt(dmb[:, sl], vnb[:, sl])
        dvn = dvn_ref[...]
        dgam_ref[...] += _fold8(dvn * xhat)
        dbeta_ref[...] += _fold8(dvn)
        dxh = dvn * g_ref[...]
        dv = rstd * (dxh - jnp.mean(dxh, axis=-1, keepdims=True) - xhat * jnp.mean(dxh * xhat, axis=-1, keepdims=True))
        dpa_ref[:, 2048:4096] = _c(dv)

    row = pl.BlockSpec((t, D_MODEL), lambda i: (i, 0))
    vec = lambda w: pl.BlockSpec((1, w), lambda i: (0, 0))
    acc = lambda w: pl.BlockSpec((8, w), lambda i: (0, 0))
    wspec = pl.BlockSpec((SGU_GROUPS, 128, 128), lambda i: (0, 0, 0))
    return pl.pallas_call(
        body, name="tok_bwd",
        grid=(s // t,),
        in_specs=[pl.BlockSpec((t, NA), lambda i: (i, 0)), row, row, row, vec(4096), vec(2048), vec(2048),
                  wspec, wspec, pl.BlockSpec((128, D_MODEL), lambda i: (0, 0))],
        out_specs=[pl.BlockSpec((t, NA), lambda i: (i, 0)), row, acc(4096), acc(2048), acc(2048),
                   pl.BlockSpec((128, D_MODEL), lambda i: (0, 0)), wspec],
        out_shape=[jax.ShapeDtypeStruct((s, NA), MXU_DTYPE), jax.ShapeDtypeStruct((s, D_MODEL), F32),
                   jax.ShapeDtypeStruct((8, 4096), F32), jax.ShapeDtypeStruct((8, 2048), F32),
                   jax.ShapeDtypeStruct((8, 2048), F32), jax.ShapeDtypeStruct((128, D_MODEL), F32),
                   jax.ShapeDtypeStruct((SGU_GROUPS, 128, 128), F32)],
        scratch_shapes=[pltpu.VMEM((t, D_MODEL), F32), pltpu.VMEM((t, D_MODEL), F32)],
        compiler_params=_cparams(("arbitrary",), 2 * t * NA * 6 + 16 * t * D_MODEL * 4),
    )(proj_a, dmerged, y_a, y_b, gate_b, sgu_g, sgu_beta, wm, wmt, bias_full)


def _out_call(merged, x, target, w_out, fnw, t):
    s = x.shape[0]
    nt = s // t

    def body(mg_ref, x_ref, t_ref, w_ref, fw_ref, dh_ref, dhb_ref, dmg_ref, loss_ref, dfw_ref):
        @pl.when(pl.program_id(0) == 0)
        def _():
            dfw_ref[...] = jnp.zeros_like(dfw_ref)

        h = x_ref[...] + _dot(mg_ref[...], w_ref[...])
        r = lax.rsqrt(jnp.mean(h * h, axis=-1, keepdims=True) + EPS)
        hn = h * r
        err = hn * fw_ref[...] - t_ref[...]
        loss_ref[...] = jnp.full(loss_ref.shape, 0.5 * jnp.sum(jnp.mean(err * err, axis=-1, keepdims=True)), F32)
        dy = err * (1.0 / D_MODEL)
        dfw_ref[...] += _fold8(dy * hn)
        gw = dy * fw_ref[...]
        dh = r * gw - h * (r * r * r) * jnp.mean(h * gw, axis=-1, keepdims=True)
        dh_ref[...] = dh
        dhb = _c(dh)
        dhb_ref[...] = dhb
        dmg_ref[...] = _dot_nt(dhb, w_ref[...])

    row = pl.BlockSpec((t, D_MODEL), lambda i: (i, 0))
    return pl.pallas_call(
        body, name="out_proj_loss",
        grid=(nt,),
        in_specs=[row, row, row, pl.BlockSpec((D_MODEL, D_MODEL), lambda i: (0, 0)), pl.BlockSpec((1, D_MODEL), lambda i: (0, 0))],
        out_specs=[row, row, row, pl.BlockSpec((1, 8, 128), lambda i: (i, 0, 0)), pl.BlockSpec((8, D_MODEL), lambda i: (0, 0))],
        out_shape=[jax.ShapeDtypeStruct((s, D_MODEL), F32), jax.ShapeDtypeStruct((s, D_MODEL), MXU_DTYPE),
                   jax.ShapeDtypeStruct((s, D_MODEL), F32), jax.ShapeDtypeStruct((nt, 8, 128), F32),
                   jax.ShapeDtypeStruct((8, D_MODEL), F32)],
        compiler_params=_cparams(("arbitrary",), 2 * D_MODEL * D_MODEL * 2 + 2 * t * D_MODEL * 24),
    )(merged, x, target, w_out, fnw)


def _ssd_fwd_call(proj_b, dtb, alog, dsk, cw, cb, nw, t):
    s = proj_b.shape[0]
    nt, nch = s // t, t // CHUNK

    def body(pb_ref, halo_ref, dtb_ref, al_ref, ds_ref, cw_ref, cb_ref, nw_ref, y_ref, yb_ref, hp_ref, ht_ref, act_ref):
        i = pl.program_id(1)

        @pl.when(i == 0)
        def _():
            ht_ref[...] = jnp.zeros_like(ht_ref)

        halo = jnp.where(i == 0, 0.0, halo_ref[:, 0:768])
        taps = _conv_taps(jnp.concatenate([halo, pb_ref[:, 0:768]], axis=0), t)
        pre = cb_ref[...]
        for k in range(4):
            pre = pre + taps[k] * cw_ref[k:k + 1, :]
        act_ref[...] = pre * _sigmoid(pre)
        masks = _ssd_masks()
        a_neg = -jnp.exp(al_ref[...])

        def chunk(c, carry):
            rows = pl.ds(pl.multiple_of(c * CHUNK, CHUNK), CHUNK)
            dt = _softplus(pb_ref[rows, 768:1280] + dtb_ref[...])
            ht = ht_ref[...]
            hp_ref[c] = ht
            y, ht_new = _ssd_chunk_fwd(act_ref[rows, 0:512], act_ref[rows, 512:640], act_ref[rows, 640:768],
                                       dt, a_neg, ds_ref[...], ht, masks)
            y_ref[rows, :] = y
            ht_ref[...] = ht_new
            return carry

        lax.fori_loop(0, nch, chunk, 0)
        zb = pb_ref[:, 1280:1792]
        hh = y_ref[...] * (zb * _sigmoid(zb))
        rr = lax.rsqrt(jnp.mean(hh * hh, axis=-1, keepdims=True) + EPS)
        yb_ref[...] = hh * rr * nw_ref[...]

    gvec = lambda w: pl.BlockSpec((1, w), lambda g, i: (0, g))
    return pl.pallas_call(
        body, name="ssd_fwd",
        grid=(SSD_GROUPS, nt),
        in_specs=[pl.BlockSpec((t, FW_B), lambda g, i: (i, g)),
                  pl.BlockSpec((8, FW_B), lambda g, i: (jnp.maximum(i * (t // 8) - 1, 0), g)),
                  gvec(512), gvec(512), gvec(512),
                  pl.BlockSpec((4, 768), lambda g, i: (0, g)), gvec(768), gvec(512)],
        out_specs=[pl.BlockSpec((t, SSD_GW), lambda g, i: (i, g)), pl.BlockSpec((t, SSD_GW), lambda g, i: (i, g)),
                   pl.BlockSpec((nch, SSD_STATE, SSD_GW), lambda g, i: (i, 0, g))],
        out_shape=[jax.ShapeDtypeStruct((s, D_MODEL), F32), jax.ShapeDtypeStruct((s, D_MODEL), F32),
                   jax.ShapeDtypeStruct((s // CHUNK, SSD_STATE, D_MODEL), F32)],
        scratch_shapes=[pltpu.VMEM((SSD_STATE, SSD_GW), F32), pltpu.VMEM((t, 768), F32)],
        compiler_params=_cparams(("parallel", "arbitrary"), 2 * t * FW_B * 4 + 6 * t * SSD_GW * 4 + 24 * 1024 * 1024),
    )(proj_b, proj_b, dtb, alog, dsk, cw, cb, nw)


def _ssd_bwd_call(proj_b, dyb, y, hprev, dtb, alog, dsk, cw, cb, nw, t):
    s = proj_b.shape[0]
    nt, nch = s // t, t // CHUNK

    def body(pb_ref, halo_ref, dyb_ref, y_ref, hp_ref, dtb_ref, al_ref, ds_ref, cw_ref, cb_ref, nw_ref,
             dpb_ref, a512_ref, a768_ref, dht_ref, act_ref, pre_ref, dact_ref, dy_ref, nxt_ref):
        i = pl.program_id(1)
        tile = nt - 1 - i

        @pl.when(i == 0)
        def _():
            dht_ref[...] = jnp.zeros_like(dht_ref)
            nxt_ref[...] = jnp.zeros_like(nxt_ref)
            a512_ref[...] = jnp.zeros_like(a512_ref)
            a768_ref[...] = jnp.zeros_like(a768_ref)

        halo = jnp.where(tile == 0, 0.0, halo_ref[:, 0:768])
        taps = _conv_taps(jnp.concatenate([halo, pb_ref[:, 0:768]], axis=0), t)
        pre = cb_ref[...]
        for k in range(4):
            pre = pre + taps[k] * cw_ref[k:k + 1, :]
        pre_ref[...] = pre
        act_ref[...] = pre * _sigmoid(pre)

        zb = pb_ref[:, 1280:1792]
        yv = y_ref[...]
        sgz = _sigmoid(zb)
        sz = zb * sgz
        hh = yv * sz
        rr = lax.rsqrt(jnp.mean(hh * hh, axis=-1, keepdims=True) + EPS)
        dyb = dyb_ref[...]
        a512_ref[0, 0] += _fold8(dyb * (hh * rr))
        tt = dyb * nw_ref[...]
        dhh = rr * tt - hh * (rr * rr * rr) * jnp.mean(hh * tt, axis=-1, keepdims=True)
        dy_ref[...] = dhh * sz
        dpb_ref[:, 0:512] = _c(dhh * yv * (sgz * (1.0 + zb * (1.0 - sgz))))

        masks = _ssd_masks()
        a_neg = -jnp.exp(al_ref[...])
        rsel = _c(jnp.where(lax.shift_right_logical(_iota((SSD_GW, 128), 0), 6) == _iota((SSD_GW, 128), 1), 1.0, 0.0))

        def chunk(cc, carry):
            c = nch - 1 - cc
            rows = pl.ds(pl.multiple_of(c * CHUNK, CHUNK), CHUNK)
            z = pb_ref[rows, 768:1280] + dtb_ref[...]
            xs = act_ref[rows, 0:512]
            dxs, d_b, d_c, ddtr, dht_prev, dadt, dyxs = _ssd_chunk_bwd(
                xs, act_ref[rows, 512:640], act_ref[rows, 640:768], _softplus(z), _sigmoid(z),
                a_neg, ds_ref[...], hp_ref[c], dht_ref[...], dy_ref[rows, :], masks)
            dht_ref[...] = dht_prev
            dact_ref[rows, 0:512] = dxs
            dact_ref[rows, 512:640] = d_b
            dact_ref[rows, 640:768] = d_c
            dpb_ref[rows, 1280:1408] = _c(_dot01_r(ddtr, rsel, 2))
            a512_ref[0, 1] += _fold8(dyxs)
            a512_ref[0, 2] += _fold8(dadt)
            a512_ref[0, 3] += _fold8(ddtr)
            return carry

        lax.fori_loop(0, nch, chunk, 0)

        pre = pre_ref[...]
        sp = _sigmoid(pre)
        dpre = dact_ref[...] * (sp * (1.0 + pre * (1.0 - sp)))
        a768_ref[0, 4] += _fold8(dpre)
        for k in range(4):
            a768_ref[0, k] += _fold8(dpre * taps[k])
        dpad = jnp.concatenate([dpre, nxt_ref[...]], axis=0)
        dx = dpre * cw_ref[3:4, :]
        for k in range(3):
            dx = dx + pltpu.roll(dpad, t + 8 - (3 - k), 0)[0:t] * cw_ref[k:k + 1, :]
        nxt_ref[...] = dpre[0:8]
        dpb_ref[:, 512:1280] = _c(dx)

    gvec = lambda w: pl.BlockSpec((1, w), lambda g, i: (0, g))
    rev = lambda w: pl.BlockSpec((t, w), lambda g, i: (nt - 1 - i, g))
    return pl.pallas_call(
        body, name="ssd_bwd",
        grid=(SSD_GROUPS, nt),
        in_specs=[rev(FW_B),
                  pl.BlockSpec((8, FW_B), lambda g, i: (jnp.maximum((nt - 1 - i) * (t // 8) - 1, 0), g)),
                  rev(SSD_GW), rev(SSD_GW),
                  pl.BlockSpec((nch, SSD_STATE, SSD_GW), lambda g, i: (nt - 1 - i, 0, g)),
                  gvec(512), gvec(512), gvec(512),
                  pl.BlockSpec((4, 768), lambda g, i: (0, g)), gvec(768), gvec(512)],
        out_specs=[rev(BW_B),
                   pl.BlockSpec((1, 4, 8, 512), lambda g, i: (g, 0, 0, 0)),
                   pl.BlockSpec((1, 5, 8, 768), lambda g, i: (g, 0, 0, 0))],
        out_shape=[jax.ShapeDtypeStruct((s, SSD_GROUPS * BW_B), MXU_DTYPE),
                   jax.ShapeDtypeStruct((SSD_GROUPS, 4, 8, 512), F32),
                   jax.ShapeDtypeStruct((SSD_GROUPS, 5, 8, 768), F32)],
        scratch_shapes=[pltpu.VMEM((SSD_STATE, SSD_GW), F32), pltpu.VMEM((t, 768), F32), pltpu.VMEM((t, 768), F32),
                        pltpu.VMEM((t, 768), F32), pltpu.VMEM((t, SSD_GW), F32), pltpu.VMEM((8, 768), F32)],
        compiler_params=_cparams(("parallel", "arbitrary"), 2 * t * FW_B * 4 + 10 * t * SSD_GW * 4 + 28 * 1024 * 1024),
    )(proj_b, proj_b, dyb, y, hprev, dtb, alog, dsk, cw, cb, nw)


def _rows_call(body, ins, outs, tr, name):
    r = ins[0].shape[0]
    spec = lambda a: pl.BlockSpec((tr, a.shape[1]), lambda i: (i, 0))
    est = 2 * tr * sum(a.shape[1] * jnp.dtype(a.dtype).itemsize for a in list(ins) + list(outs))
    return pl.pallas_call(
        body, name=name, grid=(r // tr,),
        in_specs=[spec(a) for a in ins], out_specs=[spec(o) for o in outs], out_shape=list(outs),
        compiler_params=_cparams(("parallel",), est),
    )(*ins)


def _add_pair(a, b, tr, name):
    def body(a_ref, b_ref, o_ref):
        o_ref[...] = a_ref[...] + b_ref[...]

    return _rows_call(body, [a, b], [jax.ShapeDtypeStruct(a.shape, F32)], tr, name)[0]


def _rs_add(p, sib, place, tr, name):
    _, r, c = p.shape
    half = r // 2
    nb = half // tr

    def body(pl_ref, p_ref, s_ref, b_ref, own_ref):
        v = p_ref[0] + s_ref[0]
        b_ref[0] = v.astype(jnp.bfloat16)

        @pl.when(pl.program_id(1) == pl_ref[0])
        def _():
            own_ref[...] = v

    return pl.pallas_call(
        body, name=name,
        grid_spec=pltpu.PrefetchScalarGridSpec(
            num_scalar_prefetch=1, grid=(nb, 4),
            in_specs=[pl.BlockSpec((1, tr, c), lambda i, k, pr: (k, pr[1] * nb + i, 0)),
                      pl.BlockSpec((1, tr, c), lambda i, k, pr: (k, i, 0))],
            out_specs=[pl.BlockSpec((1, tr, c), lambda i, k, pr: (k, i, 0)),
                       pl.BlockSpec((tr, c), lambda i, k, pr: (i, 0))]),
        out_shape=[jax.ShapeDtypeStruct((4, half, c), jnp.bfloat16), jax.ShapeDtypeStruct((half, c), F32)],
        compiler_params=_cparams(("parallel", "arbitrary"), 2 * tr * c * 14),
    )(place, p, sib)


def _sum_own_recv(own, recv, tr, name):
    r, c = own.shape

    def body(o_ref, r_ref, out_ref):
        v = o_ref[...]
        for j in range(3):
            v = v + r_ref[j].astype(F32)
        out_ref[...] = v

    return pl.pallas_call(
        body, name=name, grid=(r // tr,),
        in_specs=[pl.BlockSpec((tr, c), lambda i: (i, 0)), pl.BlockSpec((3, tr, c), lambda i: (0, i, 0))],
        out_specs=pl.BlockSpec((tr, c), lambda i: (i, 0)),
        out_shape=jax.ShapeDtypeStruct((r, c), F32),
        compiler_params=_cparams(("parallel",), 2 * tr * c * 14),
    )(own, recv)


def _sum_slots(stack, name):
    n, r, w = stack.shape

    def body(a_ref, out_ref):
        v = a_ref[0]
        for k in range(1, n):
            v = v + a_ref[k]
        out_ref[...] = v

    return pl.pallas_call(
        body, name=name, grid=(1,),
        in_specs=[pl.BlockSpec((n, r, w), lambda i: (0, 0, 0))],
        out_specs=pl.BlockSpec((r, w), lambda i: (0, 0)),
        out_shape=jax.ShapeDtypeStruct((r, w), F32),
        compiler_params=_cparams(("arbitrary",), 2 * (n + 1) * r * w * 4),
    )(stack)


def _adamw(w, g, m, v, tr, name):
    def body(w_ref, g_ref, m_ref, v_ref, d_ref, nm_ref, nv_ref):
        gv = g_ref[...]
        nm = ADAM_B1 * m_ref[...] + (1.0 - ADAM_B1) * gv
        nv = ADAM_B2 * v_ref[...] + (1.0 - ADAM_B2) * (gv * gv)
        m_hat = nm / (1.0 - ADAM_B1 ** ADAM_STEP)
        v_hat = nv / (1.0 - ADAM_B2 ** ADAM_STEP)
        d_ref[...] = -ADAM_LR * (m_hat / (jnp.sqrt(v_hat) + ADAM_EPS) + ADAM_WD * w_ref[...])
        nm_ref[...] = nm
        nv_ref[...] = nv

    o = jax.ShapeDtypeStruct(w.shape, F32)
    return _rows_call(body, [w, g, m, v], [o, o, o], tr, name)


ANY = pl.BlockSpec(memory_space=pl.ANY)


def _place():
    x, y, c = lax.axis_index("x"), lax.axis_index("y"), lax.axis_index("c")
    others = [(1 - x, y), (x, 1 - y), (1 - x, 1 - y)]
    return x, y, c, 2 * x + y, others


def _remote(src, dst, send, recv, k, to):
    return pltpu.make_async_remote_copy(src_ref=src, dst_ref=dst, send_sem=send.at[k], recv_sem=recv.at[k],
                                        device_id=to, device_id_type=MESH)


def _gather_call(win_b, wout_b, cw8):
    big = [win_b, wout_b]

    def body(win, wout, cw, g_in, g_out, g_cw, send, recv):
        x, y, c, me, others = _place()
        sib = (x, y, 1 - c)
        started = []
        for a, (src, dst) in enumerate(((win, g_in), (wout, g_out))):
            half = src.shape[0] // 2
            mine = pl.ds(c * half, half)
            for j, chip in enumerate(others):
                cp = _remote(src.at[mine], dst.at[me, mine], send, recv, 6 * a + j, (*chip, c))
                cp.start()
                started.append(cp)
        for j, chip in enumerate(others):
            cp = _remote(cw, g_cw.at[me], send, recv, 12 + j, (*chip, c))
            cp.start()
            started.append(cp)
        for a, dst in enumerate((g_in, g_out)):
            half = dst.shape[1] // 2
            mine = pl.ds(c * half, half)
            for j, chip in enumerate(others):
                kj = 2 * chip[0] + chip[1]
                _remote(dst.at[kj, mine], dst.at[kj, mine], send, recv, 6 * a + j, (*chip, c)).wait_recv()
                cp = _remote(dst.at[kj, mine], dst.at[kj, mine], send, recv, 6 * a + 3 + j, sib)
                cp.start()
                started.append(cp)
        for a, dst in enumerate((g_in, g_out)):
            half = dst.shape[1] // 2
            theirs = pl.ds((1 - c) * half, half)
            for j, chip in enumerate(others):
                kj = 2 * chip[0] + chip[1]
                _remote(dst.at[kj, theirs], dst.at[kj, theirs], send, recv, 6 * a + 3 + j, sib).wait_recv()
        for j, chip in enumerate(others):
            kj = 2 * chip[0] + chip[1]
            _remote(cw, g_cw.at[kj], send, recv, 12 + j, (*chip, c)).wait_recv()
        for cp in started:
            cp.wait_send()

    outs = [jax.ShapeDtypeStruct((4,) + a.shape, a.dtype) for a in (win_b, wout_b, cw8)]
    return pl.pallas_call(
        body, name="gather_weights",
        in_specs=[ANY, ANY, ANY], out_specs=[ANY, ANY, ANY], out_shape=outs,
        scratch_shapes=[pltpu.SemaphoreType.DMA((15,)), pltpu.SemaphoreType.DMA((15,))],
    )(win_b, wout_b, cw8)


def _rs_sibling_call(p_in, p_out, vsmall):
    def body(pin, pout, vs, sib_in, sib_out, sib_v, send, recv):
        x, y, c, me, others = _place()
        sib = (x, y, 1 - c)
        cps = []
        for a, (p, sb) in enumerate(((pin, sib_in), (pout, sib_out))):
            half = p.shape[1] // 2
            cps.append(_remote(p.at[:, pl.ds((1 - c) * half, half)], sb, send, recv, a, sib))
        cps.append(_remote(vs, sib_v, send, recv, 2, sib))
        for cp in cps:
            cp.start()
        for cp in cps:
            cp.wait_recv()
        for cp in cps:
            cp.wait_send()

    def halves(p):
        return jax.ShapeDtypeStruct((4, p.shape[1] // 2, p.shape[2]), p.dtype)

    outs = [halves(p_in), halves(p_out), jax.ShapeDtypeStruct(vsmall.shape, vsmall.dtype)]
    return pl.pallas_call(
        body, name="rs_sibling",
        in_specs=[ANY] * 3, out_specs=[ANY] * 3, out_shape=outs,
        scratch_shapes=[pltpu.SemaphoreType.DMA((3,)), pltpu.SemaphoreType.DMA((3,))],
    )(p_in, p_out, vsmall)


def _rs_join_call(f_in, f_out, nw8):
    def body(fin, fout, nw, full_in, full_out, all_nw, send, recv):
        x, y, c, me, others = _place()
        sib = (x, y, 1 - c)
        cps = []
        for a, (f, full) in enumerate(((fin, full_in), (fout, full_out))):
            half = f.shape[0]
            cps.append(_remote(f, full.at[pl.ds(c * half, half)], send, recv, a, sib))
        mine = 4 * x + 2 * y + c
        peers = []
        for r in range(1, 8):
            px, py, pc = (1 - x if r & 4 else x), (1 - y if r & 2 else y), (1 - c if r & 1 else c)
            peers.append((r, (px, py, pc), 4 * px + 2 * py + pc))
            cps.append(_remote(nw, all_nw.at[mine], send, recv, 1 + r, (px, py, pc)))
        for cp in cps:
            cp.start()
        for a, (f, full) in enumerate(((fin, full_in), (fout, full_out))):
            half = f.shape[0]
            _remote(f, full.at[pl.ds((1 - c) * half, half)], send, recv, a, sib).wait_recv()
        for r, peer, idx in peers:
            _remote(nw, all_nw.at[idx], send, recv, 1 + r, peer).wait_recv()
        for cp in cps:
            cp.wait_send()

    outs = [jax.ShapeDtypeStruct((2 * f.shape[0], f.shape[1]), F32) for f in (f_in, f_out)]
    outs.append(jax.ShapeDtypeStruct((8,) + nw8.shape, F32))
    return pl.pallas_call(
        body, name="rs_join",
        in_specs=[ANY] * 3, out_specs=[ANY] * 3, out_shape=outs,
        scratch_shapes=[pltpu.SemaphoreType.DMA((9,)), pltpu.SemaphoreType.DMA((9,))],
    )(f_in, f_out, nw8)


def _pack(arrs):
    parts = []
    for a in arrs:
        f = a.reshape(-1).astype(F32)
        pad = (-f.shape[0]) % 1024
        parts.append(jnp.pad(f, (0, pad)).reshape(-1, 128))
    return jnp.concatenate(parts, axis=0)


def _unpack(packed, shapes):
    out, row = [], 0
    for shp in shapes:
        n = 1
        for d in shp:
            n *= d
        rows = (n + 1023) // 1024 * 8
        out.append(packed[row:row + rows].reshape(-1)[:n].reshape(shp))
        row += rows
    return out


def _expand_heads(v32):
    return jnp.repeat(v32.reshape(32), HEADDIM).reshape(1, D_MODEL)


def kernel(x, norm_w, w_in, gate_b, sgu_norm_g, sgu_norm_b, sgu_w, sgu_b, conv_w, conv_b, dt_bias, A_log, D_skip, ssd_norm_w, w_out, final_norm_w, loss_target, m_norm_w, m_w_in, m_gate_b, m_sgu_norm_g, m_sgu_norm_b, m_sgu_w, m_sgu_b, m_conv_w, m_conv_b, m_dt_bias, m_A_log, m_D_skip, m_ssd_norm_w, m_w_out, m_final_norm_w, v_norm_w, v_w_in, v_gate_b, v_sgu_norm_g, v_sgu_norm_b, v_sgu_w, v_sgu_b, v_conv_w, v_conv_b, v_dt_bias, v_A_log, v_D_skip, v_ssd_norm_w, v_w_out, v_final_norm_w):
    s = x.shape[1]
    x2 = x.reshape(s, D_MODEL)
    tgt = loss_target.reshape(s, D_MODEL)
    t_ssd, t_tok, t_out, t_row = min(T_SSD, s), min(T_TOK, s), min(T_OUT, s), min(T_ROW, s)
    tm_mm, tk_dw = min(TM_MM, s), min(TK_DW, s)
    chip = 2 * lax.axis_index("x") + lax.axis_index("y")

    cw8 = jnp.pad(conv_w[0], ((0, 4), (0, 0)))
    win_b, wout_b = _c(w_in[0]), _c(w_out[0])
    g_in, g_out, g_cw = _gather_call(win_b, wout_b, cw8)
    g_in = lax.dynamic_update_index_in_dim(g_in, win_b, chip, 0)
    g_out = lax.dynamic_update_index_in_dim(g_out, wout_b, chip, 0)
    g_cw = lax.dynamic_update_index_in_dim(g_cw, cw8, chip, 0)
    wref = jnp.transpose(g_in, (1, 0, 2)).reshape(D_MODEL, IN_W)
    w_out_full = g_out.reshape(D_MODEL, D_MODEL)
    conv_w_full = jnp.transpose(g_cw[:, 0:4, :], (1, 0, 2)).reshape(4, 3072)

    w_a = jnp.concatenate([wref[:, 0:6144], wref[:, 11296:15392]], axis=1)
    fw, bw = [], []
    for g in range(SSD_GROUPS):
        xs_g = wref[:, 8192 + 512 * g:8192 + 512 * g + 512]
        b_g = wref[:, 10240 + 128 * g:10240 + 128 * g + 128]
        c_g = wref[:, 10752 + 128 * g:10752 + 128 * g + 128]
        zb_g = wref[:, 6144 + 512 * g:6144 + 512 * g + 512]
        dt_g = wref[:, 11264 + 8 * g:11264 + 8 * g + 8]
        fw += [xs_g, b_g, c_g, jnp.repeat(dt_g, HEADDIM, axis=1), zb_g]
        bw += [zb_g, xs_g, b_g, c_g, jnp.pad(dt_g, ((0, 0), (0, 120)))]
    w_b = jnp.concatenate(fw, axis=1)
    wt_a = w_a.T
    wt_b = jnp.concatenate(bw, axis=1).T

    def group_cols(full_xs, full_bc):
        parts = []
        for g in range(SSD_GROUPS):
            parts += [full_xs[:, 512 * g:512 * g + 512], full_bc[:, 128 * g:128 * g + 128], full_bc[:, 512 + 128 * g:512 + 128 * g + 128]]
        return jnp.concatenate(parts, axis=1)

    cw_g = group_cols(conv_w_full[:, 0:2048], conv_w_full[:, 2048:3072])
    cb_g = group_cols(conv_b[:, 0:2048], conv_b[:, 2048:3072])
    dtb_e, alog_e, dsk_e = _expand_heads(dt_bias), _expand_heads(A_log), _expand_heads(D_skip)

    pos_chunk = jnp.arange(SGU_BLOCK) // CHUNK
    smask = pos_chunk[None, :] <= pos_chunk[:, None]
    wm_f = jnp.where(smask[None], sgu_w[0], 0.0)
    wm = _c(wm_f)
    wmt = _c(jnp.transpose(wm_f, (0, 2, 1)))
    bias_full = jnp.repeat(sgu_b[0].T, D_MODEL // SGU_GROUPS, axis=1)
    fnw = final_norm_w.reshape(1, D_MODEL)

    xn, xnt = _norm_call(x2, norm_w, t_row)
    proj_a = _mm(xn, w_a, tm=tm_mm, tn=1024, tk=D_MODEL, name="in_proj_a")
    proj_b = _mm(xn, w_b, tm=tm_mm, tn=1024, tk=D_MODEL, name="in_proj_b")
    y_ssd, y_b, hprev = _ssd_fwd_call(proj_b, dtb_e, alog_e, dsk_e, cw_g, cb_g, ssd_norm_w, t_ssd)
    y_a, merged, merged_t = _tok_fwd_call(proj_a, y_b, gate_b, sgu_norm_g, sgu_norm_b, wm, bias_full, t_tok)
    dh, dh_b, dmerged, loss_t, dfw8 = _out_call(merged, x2, tgt, w_out_full, fnw, t_out)

    dproj_a, dy_b, dgb8, dgam8, dbeta8, dbfull, dws = _tok_bwd_call(
        proj_a, dmerged, y_a, y_b, gate_b, sgu_norm_g, sgu_norm_b, wm, wmt, bias_full, t_tok)
    dproj_b, a512, a768 = _ssd_bwd_call(proj_b, dy_b, y_ssd, hprev, dtb_e, alog_e, dsk_e, cw_g, cb_g, ssd_norm_w, t_ssd)
    dw_a = _mm(xnt, dproj_a, tm=D_MODEL, tn=1024, tk=tk_dw, name="dw_in_a")
    dw_b = _mm(xnt, dproj_b, tm=D_MODEL, tn=BW_B, tk=tk_dw, name="dw_in_b")
    dw_out_p = _mm(merged_t, dh_b, tm=D_MODEL, tn=1024, tk=tk_dw, name="dw_out")

    gb = lambda a, b: jnp.concatenate([dw_b[:, BW_B * g + a:BW_B * g + b] for g in range(SSD_GROUPS)], axis=1)
    dw_ref = jnp.concatenate([dw_a[:, 0:6144], gb(0, 512), gb(512, 1024), gb(1024, 1152), gb(1152, 1280),
                              gb(1280, 1288), dw_a[:, 6144:10240]], axis=1)
    p_in = jnp.transpose(dw_ref.reshape(D_MODEL, 4, SHARD_W), (1, 0, 2))
    p_out = dw_out_p.reshape(4, D_MODEL // 4, D_MODEL)

    s512 = jnp.sum(a512, axis=2)
    heads = lambda v: jnp.sum(v.reshape(32, HEADDIM), axis=1).reshape(1, 32)
    d_ssd_nw = s512[:, 0].reshape(1, D_MODEL)
    d_dskip = heads(s512[:, 1].reshape(D_MODEL))
    d_alog = heads(s512[:, 2].reshape(D_MODEL)) * (1.0 / HEADDIM) * (-jnp.exp(A_log))
    d_dtb = heads(s512[:, 3].reshape(D_MODEL))
    s768 = jnp.sum(a768, axis=2)
    ungroup = lambda v: jnp.concatenate([v[g, :, 0:512] for g in range(4)] + [v[g, :, 512:640] for g in range(4)]
                                        + [v[g, :, 640:768] for g in range(4)], axis=1)
    d_cw = ungroup(s768[:, 0:4])
    d_cb = ungroup(s768[:, 4:5])
    d_sgu_b = jnp.sum(dbfull.reshape(128, SGU_GROUPS, 128), axis=2).T.reshape(1, SGU_GROUPS, 128)
    d_sgu_w = jnp.where(smask[None], dws, 0.0).reshape(1, SGU_GROUPS, 128, 128)
    fold = lambda a8: jnp.sum(a8, axis=0, keepdims=True)
    small_local = [fold(dgb8), fold(dgam8), fold(dbeta8), d_sgu_w, d_sgu_b, d_cw, d_cb,
                   d_dtb, d_alog, d_dskip, d_ssd_nw, fold(dfw8).reshape(D_MODEL), jnp.sum(loss_t[:, 0, 0]).reshape(1)]
    small_shapes = [a.shape for a in small_local]
    v_local = _pack(small_local)

    core = lax.axis_index("c")
    place = jnp.stack([chip, core]).astype(jnp.int32)
    hr_i, hr_o = D_MODEL // 2, D_MODEL // 8
    sib_i, sib_o, sib_v = _rs_sibling_call(p_in, p_out, v_local)
    s1b_i, o_i = _rs_add(p_in, sib_i, place, 256, "rs_add_in")
    s1b_o, o_o = _rs_add(p_out, sib_o, place, 256, "rs_add_out")
    chip_v = _add_pair(v_local, sib_v, v_local.shape[0], "ar_add_small")
    dxn, r_i, r_o, abs_v = _dx_rs_call(dproj_a, wt_a, dproj_b, wt_b, s1b_i, s1b_o, chip_v, tm=tm_mm)
    grad_x, dnw8 = _gradx_call(x2, dxn, dh, norm_w, t_row)
    abs_v = lax.dynamic_update_index_in_dim(abs_v, chip_v, chip, 0)
    f_i = _sum_own_recv(o_i, r_i, 256, "rs_sum_in")
    f_o = _sum_own_recv(o_o, r_o, 256, "rs_sum_out")
    g_w_in, g_w_out, all_nw = _rs_join_call(f_i, f_o, dnw8)
    g_w_in = lax.dynamic_update_slice_in_dim(g_w_in, f_i, core * hr_i, axis=0)
    g_w_out = lax.dynamic_update_slice_in_dim(g_w_out, f_o, core * hr_o, axis=0)
    all_nw = lax.dynamic_update_index_in_dim(all_nw, dnw8, 2 * chip + core, 0)
    g_nw = fold(_sum_slots(all_nw, "ar_sum_norm_w"))
    total_v = _sum_slots(abs_v, "ar_sum_small")
    (g_gb, g_gam, g_beta, g_sw, g_sb, g_cw_full, g_cb, g_dtb, g_alog, g_dsk, g_snw, g_fnw, loss1) = _unpack(total_v, small_shapes)
    g_cw_shard = lax.dynamic_slice(g_cw_full, (0, chip * 768), (4, 768)).reshape(1, 4, 768)
    loss = loss1.reshape(())

    tr_in = 296 if SHARD_W % 296 == 0 else 8
    d_win, nm_win, nv_win = (a.T for a in _adamw(w_in[0].T, g_w_in.T, m_w_in[0].T, v_w_in[0].T, tr_in, "adamw_w_in"))
    d_wout, nm_wout, nv_wout = _adamw(w_out[0], g_w_out, m_w_out[0], v_w_out[0], 128, "adamw_w_out")
    small_w = [norm_w, gate_b, sgu_norm_g, sgu_norm_b, sgu_w, sgu_b, conv_w, conv_b, dt_bias, A_log, D_skip, ssd_norm_w, final_norm_w]
    small_m = [m_norm_w, m_gate_b, m_sgu_norm_g, m_sgu_norm_b, m_sgu_w, m_sgu_b, m_conv_w, m_conv_b, m_dt_bias, m_A_log, m_D_skip, m_ssd_norm_w, m_final_norm_w]
    small_v = [v_norm_w, v_gate_b, v_sgu_norm_g, v_sgu_norm_b, v_sgu_w, v_sgu_b, v_conv_w, v_conv_b, v_dt_bias, v_A_log, v_D_skip, v_ssd_norm_w, v_final_norm_w]
    small_g = [g_nw, g_gb, g_gam, g_beta, g_sw, g_sb, g_cw_shard, g_cb, g_dtb, g_alog, g_dsk, g_snw, g_fnw]
    shapes_w = [a.shape for a in small_w]
    small_g = [a.reshape(shp) for a, shp in zip(small_g, shapes_w)]
    pw = _pack(small_w)
    pd, pm, pv = _adamw(pw, _pack(small_g), _pack(small_m), _pack(small_v), pw.shape[0], "adamw_small")
    d_small, nm_small, nv_small = _unpack(pd, shapes_w), _unpack(pm, shapes_w), _unpack(pv, shapes_w)

    def with_big(small, win, wout):
        o = list(small)
        return o[0:1] + [win.reshape(1, D_MODEL, SHARD_W)] + o[1:12] + [wout.reshape(1, D_MODEL // 4, D_MODEL)] + o[12:13]

    grads = with_big(small_g, g_w_in, g_w_out)
    deltas = with_big(d_small, d_win, d_wout)
    new_m = with_big(nm_small, nm_win, nm_wout)
    new_v = with_big(nv_small, nv_win, nv_wout)
    return (loss, grad_x.reshape(1, s, D_MODEL), *grads, *deltas, *new_m, *new_v)
```

```python
import functools

import jax
import jax.numpy as jnp
from jax import lax
from jax.experimental import pallas as pl
from jax.experimental.pallas import tpu as pltpu

F32 = jnp.float32
MXU_DTYPE = jnp.bfloat16

D_MODEL = 2048
EPS = 1e-5
CHUNK = 64
SGU_BLOCK = 128
SGU_GROUPS = 16
SSD_GROUPS = 4
SSD_GW = 512
SSD_STATE = 128
HEADDIM = 64
IN_W = 15392
SHARD_W = IN_W // 4
FW_B = 1792
BW_B = 1408
NA = 10240

ADAM_LR = 0.001
ADAM_B1 = 0.9
ADAM_B2 = 0.999
ADAM_EPS = 1e-08
ADAM_WD = 0.01
ADAM_STEP = 10

T_SSD = 256
NG_SSD = 2
T_TOK = 128
T_OUT = 256
T_ROW = 512
TM_MM = 1024
TK_DW = 1024
VMEM_CAP = 60 * 1024 * 1024
MESH = pl.DeviceIdType.MESH


def _cparams(sem, est_bytes):
    lim = int(min(VMEM_CAP, max(32 * 1024 * 1024, est_bytes + 12 * 1024 * 1024)))
    return pltpu.CompilerParams(dimension_semantics=sem, vmem_limit_bytes=lim)


def _c(x):
    return x.astype(MXU_DTYPE)


def _dot(a, b):
    return jnp.dot(a, b, preferred_element_type=F32)


def _dot_nt(a, b):
    return lax.dot_general(a, b, (((1,), (1,)), ((), ())), preferred_element_type=F32)


def _dot_tn(a, b):
    return lax.dot_general(a, b, (((0,), (0,)), ((), ())), preferred_element_type=F32)


def _split(x, n):
    parts, r = [], x
    for _ in range(n):
        p = _c(r)
        parts.append(p)
        r = r - p.astype(F32)
    return parts


def _dot01_l(m01, x, n):
    acc = None
    for p in _split(x, n):
        t = _dot(m01, p)
        acc = t if acc is None else acc + t
    return acc


def _dot01_r(x, m01, n):
    acc = None
    for p in _split(x, n):
        t = _dot(p, m01)
        acc = t if acc is None else acc + t
    return acc


def _sigmoid(x):
    return 1.0 / (1.0 + jnp.exp(-x))


def _fold8(x):
    r, w = x.shape
    return jnp.sum(x.reshape(r // 8, 8, w), axis=0)


def _iota(shape, dim):
    return lax.broadcasted_iota(jnp.int32, shape, dim)


def _ssd_masks():
    l64 = _iota((CHUNK, SSD_GW), 0)
    s64 = jnp.bitwise_and(_iota((CHUNK, SSD_GW), 1), CHUNK - 1)
    diag = l64 == s64
    causal = l64 >= s64
    row_last = l64 == CHUNK - 1
    r4 = lax.shift_right_logical(_iota((256, 256), 0), 6)
    c4 = lax.shift_right_logical(_iota((256, 256), 1), 6)
    mask4 = r4 == c4
    tl = _iota((CHUNK, CHUNK), 0)
    tc = _iota((CHUNK, CHUNK), 1)
    tri = _c(jnp.where(tc <= tl, 1.0, 0.0))
    trit = _c(jnp.where(tc >= tl, 1.0, 0.0))
    return diag, causal, row_last, mask4, tri, trit


def _ssd_common(xs, bm, cm, dt, a_neg, masks):
    diag, causal, row_last, mask4, tri, trit = masks
    a = dt * a_neg
    acs = _dot01_l(tri, a, 3)
    row_e = jnp.sum(jnp.where(diag, acs, 0.0), axis=0, keepdims=True)
    seg = acs - row_e
    lm = jnp.exp(jnp.where(causal, seg, -1e30))
    bb, cb = _c(bm), _c(cm)
    brep = jnp.concatenate([bb] * 8, axis=0)
    cbrep = _dot_nt(cb, brep)
    m = cbrep * lm
    xdt = xs * dt
    acs_last = jnp.sum(jnp.where(row_last, acs, 0.0), axis=0, keepdims=True)
    dec = jnp.exp(acs_last - acs)
    eacs = jnp.exp(acs)
    cd = jnp.exp(acs_last)
    return dict(lm=lm, bb=bb, cb=cb, brep=brep, m=m, xdt=xdt, dec=dec, eacs=eacs, cd=cd)


def _blockdiag4(xb, mask4):
    return jnp.where(mask4, jnp.concatenate([xb] * 4, axis=0), jnp.zeros((), xb.dtype))


def _ssd_chunk_fwd(xs, bm, cm, dt, a_neg, d_skip, ht, masks):
    q = _ssd_common(xs, bm, cm, dt, a_neg, masks)
    mask4 = masks[3]
    mb, xdtb = _c(q["m"]), _c(q["xdt"])
    yd = []
    for blk in range(2):
        sl = slice(256 * blk, 256 * blk + 256)
        yd.append(_dot(mb[:, sl], _blockdiag4(xdtb[:, sl], mask4)))
    y_diag = jnp.concatenate(yd, axis=1)
    p = _dot(q["cb"], _c(ht))
    y = y_diag + p * q["eacs"] + xs * d_skip
    st = _dot_tn(q["bb"], _c(q["xdt"] * q["dec"]))
    return y, ht * q["cd"] + st


def _ssd_chunk_bwd(xs, bm, cm, dt, sig, a_neg, d_skip, hprev, dht, dy, masks):
    diag, causal, row_last, mask4, tri, trit = masks
    q = _ssd_common(xs, bm, cm, dt, a_neg, masks)
    lm, bb, cb, brep, m, xdt, dec, eacs, cd = (q[k] for k in ("lm", "bb", "cb", "brep", "m", "xdt", "dec", "eacs", "cd"))
    hb = _c(hprev)
    yoff = _dot(cb, hb) * eacs
    dyb = _c(dy)
    dpb = _c(dy * eacs)
    d_c = _dot_nt(dpb, hb)
    dh_y = _dot_tn(cb, dpb)
    mb, xdtb = _c(m), _c(xdt)
    dm_parts, dxdt_parts = [], []
    for blk in range(2):
        sl = slice(256 * blk, 256 * blk + 256)
        bd = _blockdiag4(xdtb[:, sl], mask4)
        dm_parts.append(_dot_nt(dyb[:, sl], bd))
        dxf = jnp.where(mask4, _dot_tn(mb[:, sl], dyb[:, sl]), 0.0)
        dxdt_parts.append(dxf[0:64] + dxf[64:128] + dxf[128:192] + dxf[192:256])
    dm = jnp.concatenate(dm_parts, axis=1)
    dxdt = jnp.concatenate(dxdt_parts, axis=1)
    dcbb = _c(dm * lm)
    g = dm * m
    d_c = d_c + _dot(dcbb, brep)
    dbrep = _dot_tn(dcbb, cb)
    d_b = dbrep[0:64]
    for r in range(1, 8):
        d_b = d_b + dbrep[64 * r:64 * r + 64]
    dhtb = _c(dht)
    dxd = _dot(bb, dhtb)
    xd = xdt * dec
    dxdt = dxdt + dxd * dec
    tq = dxd * xd
    d_b = d_b + _dot_nt(_c(xd), dhtb)
    dcd = jnp.sum(dht * hprev, axis=0, keepdims=True)
    col_g = jnp.sum(g, axis=0, keepdims=True)
    last = jnp.sum(tq, axis=0, keepdims=True) + dcd * cd
    qq = g - jnp.where(diag, col_g, 0.0) + dy * yoff - tq + jnp.where(row_last, last, 0.0)
    bd4 = _c(jnp.where(mask4, 1.0, 0.0))
    dacs = jnp.concatenate([_dot01_r(qq[:, 256 * b:256 * b + 256], bd4, 2) for b in range(2)], axis=1)
    da = _dot01_l(trit, dacs, 2)
    ddt = dxdt * xs + da * (a_neg * (1.0 / HEADDIM))
    dxs = dxdt * dt + dy * d_skip
    return dxs, d_b, d_c, ddt * sig, dht * cd + dh_y, da * dt, dy * xs


def _softplus(x):
    return jnp.maximum(x, 0.0) + jnp.log1p(jnp.exp(-jnp.abs(x)))


def _conv_taps(xpad, t):
    taps = []
    for k in range(4):
        sh = 3 - k
        v = xpad if sh == 0 else pltpu.roll(xpad, sh, 0)
        taps.append(v[8:8 + t])
    return taps


def _norm_call(x, norm_w, tm):
    s = x.shape[0]

    def body(x_ref, w_ref, xn_ref, xnt_ref):
        xv = x_ref[...]
        r = lax.rsqrt(jnp.mean(xv * xv, axis=-1, keepdims=True) + EPS)
        xn = xv * r * w_ref[...]
        xn_ref[...] = _c(xn)
        xnt_ref[...] = _c(xn.T)

    return pl.pallas_call(
        body, name="rmsnorm_in",
        grid=(s // tm,),
        in_specs=[pl.BlockSpec((tm, D_MODEL), lambda i: (i, 0)), pl.BlockSpec((1, D_MODEL), lambda i: (0, 0))],
        out_specs=[pl.BlockSpec((tm, D_MODEL), lambda i: (i, 0)), pl.BlockSpec((D_MODEL, tm), lambda i: (0, i))],
        out_shape=[jax.ShapeDtypeStruct((s, D_MODEL), MXU_DTYPE), jax.ShapeDtypeStruct((D_MODEL, s), MXU_DTYPE)],
        compiler_params=_cparams(("parallel",), 2 * tm * D_MODEL * 12),
    )(x, norm_w)


def _mm(a, b, *, tm, tn, tk, name):
    m, k = a.shape
    n = b.shape[1]
    nk = k // tk
    assert m % tm == 0 and n % tn == 0 and k % tk == 0, (a.shape, b.shape, tm, tn, tk)

    def body(a_ref, b_ref, o_ref):
        if nk == 1:
            o_ref[...] = _dot(a_ref[...], b_ref[...])
        else:
            @pl.when(pl.program_id(2) == 0)
            def _():
                o_ref[...] = jnp.zeros_like(o_ref)

            o_ref[...] += _dot(a_ref[...], b_ref[...])

    isz = jnp.dtype(a.dtype).itemsize
    est = 2 * (tm * tk + tk * tn) * isz + 2 * tm * tn * 4
    return pl.pallas_call(
        body, name=name,
        grid=(m // tm, n // tn, nk),
        in_specs=[pl.BlockSpec((tm, tk), lambda i, j, kk: (i, kk)), pl.BlockSpec((tk, tn), lambda i, j, kk: (kk, j))],
        out_specs=pl.BlockSpec((tm, tn), lambda i, j, kk: (i, j)),
        out_shape=jax.ShapeDtypeStruct((m, n), F32),
        compiler_params=_cparams(("parallel", "parallel", "arbitrary"), est),
    )(a, b)


def _dx_rs_call(dpa, wta, dpb, wtb, sb_in, sb_out, chip_v, *, tm):
    s = dpa.shape[0]
    tka, tkb = 1024, BW_B
    nka, nkb = dpa.shape[1] // tka, dpb.shape[1] // tkb
    ni, nk = s // tm, nka + nkb

    def body(a_ref, wa_ref, b_ref, wb_ref, sbin, sbout, cv, o_ref, rc_in, rc_out, abs_v, send, recv):
        i, kk = pl.program_id(0), pl.program_id(1)

        def copies():
            x, y, c, me, others = _place()
            sends, recvs = [], []
            for j, chip in enumerate(others):
                kj = 2 * chip[0] + chip[1]
                to = (*chip, c)
                sends += [_remote(sbin.at[kj], rc_in.at[j], send, recv, j, to),
                          _remote(sbout.at[kj], rc_out.at[j], send, recv, 3 + j, to),
                          _remote(cv, abs_v.at[me], send, recv, 6 + j, to)]
                recvs += [sends[-3], sends[-2], _remote(cv, abs_v.at[kj], send, recv, 6 + j, to)]
            return sends, recvs

        @pl.when((i == 0) & (kk == 0))
        def _():
            for cp in copies()[0]:
                cp.start()

        @pl.when(kk == 0)
        def _():
            o_ref[...] = jnp.zeros_like(o_ref)

        @pl.when(kk < nka)
        def _():
            o_ref[...] += _dot(a_ref[...], wa_ref[...])

        @pl.when(kk >= nka)
        def _():
            o_ref[...] += _dot(b_ref[...], wb_ref[...])

        @pl.when((i == ni - 1) & (kk == nk - 1))
        def _():
            sends, recvs = copies()
            for cp in recvs:
                cp.wait_recv()
            for cp in sends:
                cp.wait_send()

    isz = jnp.dtype(dpa.dtype).itemsize
    est = 2 * isz * (tm * tka + tka * D_MODEL + tm * tkb + tkb * D_MODEL) + 2 * tm * D_MODEL * 4
    outs = [jax.ShapeDtypeStruct((s, D_MODEL), F32),
            jax.ShapeDtypeStruct((3,) + sb_in.shape[1:], sb_in.dtype), jax.ShapeDtypeStruct((3,) + sb_out.shape[1:], sb_out.dtype),
            jax.ShapeDtypeStruct((4,) + chip_v.shape, F32)]
    return pl.pallas_call(
        body, name="dx_matmul_rs_chips",
        grid=(ni, nk),
        in_specs=[
            pl.BlockSpec((tm, tka), lambda i, kk: (i, jnp.minimum(kk, nka - 1))),
            pl.BlockSpec((tka, D_MODEL), lambda i, kk: (jnp.minimum(kk, nka - 1), 0)),
            pl.BlockSpec((tm, tkb), lambda i, kk: (i, jnp.maximum(kk - nka, 0))),
            pl.BlockSpec((tkb, D_MODEL), lambda i, kk: (jnp.maximum(kk - nka, 0), 0)),
            ANY, ANY, ANY,
        ],
        out_specs=[pl.BlockSpec((tm, D_MODEL), lambda i, kk: (i, 0)), ANY, ANY, ANY],
        out_shape=outs,
        scratch_shapes=[pltpu.SemaphoreType.DMA((9,)), pltpu.SemaphoreType.DMA((9,))],
        compiler_params=_cparams(("arbitrary", "arbitrary"), est),
    )(dpa, wta, dpb, wtb, sb_in, sb_out, chip_v)


def _gradx_call(x, dxn, dh, norm_w, tm):
    s = x.shape[0]

    def body(x_ref, g_ref, dh_ref, w_ref, gx_ref, dw_ref):
        @pl.when(pl.program_id(0) == 0)
        def _():
            dw_ref[...] = jnp.zeros_like(dw_ref)

        xv, gv = x_ref[...], g_ref[...]
        r = lax.rsqrt(jnp.mean(xv * xv, axis=-1, keepdims=True) + EPS)
        gw = gv * w_ref[...]
        gx_ref[...] = r * gw - xv * (r * r * r) * jnp.mean(xv * gw, axis=-1, keepdims=True) + dh_ref[...]
        dw_ref[...] += _fold8(gv * (xv * r))

    row = pl.BlockSpec((tm, D_MODEL), lambda i: (i, 0))
    return pl.pallas_call(
        body, name="grad_x",
        grid=(s // tm,),
        in_specs=[row, row, row, pl.BlockSpec((1, D_MODEL), lambda i: (0, 0))],
        out_specs=[row, pl.BlockSpec((8, D_MODEL), lambda i: (0, 0))],
        out_shape=[jax.ShapeDtypeStruct((s, D_MODEL), F32), jax.ShapeDtypeStruct((8, D_MODEL), F32)],
        compiler_params=_cparams(("arbitrary",), 2 * tm * D_MODEL * 16),
    )(x, dxn, dh, norm_w)


def _layernorm_stats(v):
    mu = jnp.mean(v, axis=-1, keepdims=True)
    vc = v - mu
    var = jnp.mean(vc * vc, axis=-1, keepdims=True)
    return vc * lax.rsqrt(var + EPS), lax.rsqrt(var + EPS)


def _tok_fwd_call(proj_a, y_b, gate_b, sgu_g, sgu_beta, wm, bias_full, t):
    s = proj_a.shape[0]

    def body(pa_ref, yb_ref, gb_ref, g_ref, be_ref, wm_ref, bf_ref, ya_ref, mg_ref, mgt_ref, mix_ref):
        u = pa_ref[:, 0:2048]
        v = pa_ref[:, 2048:4096]
        za = pa_ref[:, 4096:6144]
        xhat, _ = _layernorm_stats(v)
        vnb = _c(xhat * g_ref[...] + be_ref[...])
        for gi in range(SGU_GROUPS):
            sl = slice(128 * gi, 128 * gi + 128)
            mix_ref[:, sl] = _dot(wm_ref[gi], vnb[:, sl])
        mixed = mix_ref[...] + bf_ref[...]
        y_a = u * mixed * (za * _sigmoid(za))
        g0 = _sigmoid(pa_ref[:, 6144:8192] + gb_ref[:, 0:2048])
        g1 = _sigmoid(pa_ref[:, 8192:10240] + gb_ref[:, 2048:4096])
        merged = g0 * y_a + g1 * yb_ref[...]
        ya_ref[...] = y_a
        mg_ref[...] = _c(merged)
        mgt_ref[...] = _c(merged.T)

    row = pl.BlockSpec((t, D_MODEL), lambda i: (i, 0))
    vec = lambda w: pl.BlockSpec((1, w), lambda i: (0, 0))
    return pl.pallas_call(
        body, name="tok_fwd",
        grid=(s // t,),
        in_specs=[pl.BlockSpec((t, NA), lambda i: (i, 0)), row, vec(4096), vec(2048), vec(2048),
                  pl.BlockSpec((SGU_GROUPS, 128, 128), lambda i: (0, 0, 0)), pl.BlockSpec((128, D_MODEL), lambda i: (0, 0))],
        out_specs=[row, row, pl.BlockSpec((D_MODEL, t), lambda i: (0, i))],
        out_shape=[jax.ShapeDtypeStruct((s, D_MODEL), F32), jax.ShapeDtypeStruct((s, D_MODEL), MXU_DTYPE),
                   jax.ShapeDtypeStruct((D_MODEL, s), MXU_DTYPE)],
        scratch_shapes=[pltpu.VMEM((t, D_MODEL), F32)],
        compiler_params=_cparams(("parallel",), 2 * t * NA * 4 + 12 * t * D_MODEL * 4),
    )(proj_a, y_b, gate_b, sgu_g, sgu_beta, wm, bias_full)


def _tok_bwd_call(proj_a, dmerged, y_a, y_b, gate_b, sgu_g, sgu_beta, wm, wmt, bias_full, t):
    s = proj_a.shape[0]

    def body(pa_ref, dm_ref, ya_ref, yb_ref, gb_ref, g_ref, be_ref, wm_ref, wmt_ref, bf_ref,
             dpa_ref, dyb_ref, dgb_ref, dgam_ref, dbeta_ref, dbf_ref, dws_ref, mix_ref, dvn_ref):
        @pl.when(pl.program_id(0) == 0)
        def _():
            dgb_ref[...] = jnp.zeros_like(dgb_ref)
            dgam_ref[...] = jnp.zeros_like(dgam_ref)
            dbeta_ref[...] = jnp.zeros_like(dbeta_ref)
            dbf_ref[...] = jnp.zeros_like(dbf_ref)
            dws_ref[...] = jnp.zeros_like(dws_ref)

        u = pa_ref[:, 0:2048]
        v = pa_ref[:, 2048:4096]
        za = pa_ref[:, 4096:6144]
        xhat, rstd = _layernorm_stats(v)
        vnb = _c(xhat * g_ref[...] + be_ref[...])
        for gi in range(SGU_GROUPS):
            sl = slice(128 * gi, 128 * gi + 128)
            mix_ref[:, sl] = _dot(wm_ref[gi], vnb[:, sl])
        mixed = mix_ref[...] + bf_ref[...]
        sig = _sigmoid(za)
        sz = za * sig
        dm = dm_ref[...]
        y_a = ya_ref[...]
        g0 = _sigmoid(pa_ref[:, 6144:8192] + gb_ref[:, 0:2048])
        g1 = _sigmoid(pa_ref[:, 8192:10240] + gb_ref[:, 2048:4096])
        dgl0 = dm * y_a * g0 * (1.0 - g0)
        dgl1 = dm * yb_ref[...] * g1 * (1.0 - g1)
        dyb_ref[...] = dm * g1
        dya = dm * g0
        dpa_ref[:, 6144:8192] = _c(dgl0)
        dpa_ref[:, 8192:10240] = _c(dgl1)
        dgb_ref[:, 0:2048] += _fold8(dgl0)
        dgb_ref[:, 2048:4096] += _fold8(dgl1)
        dpa_ref[:, 0:2048] = _c(dya * mixed * sz)
        dpa_ref[:, 4096:6144] = _c(dya * (u * mixed) * (sig * (1.0 + za * (1.0 - sig))))
        dmixed = dya * u * sz
        dbf_ref[...] += dmixed
        dmb = _c(dmixed)
        for gi in range(SGU_GROUPS):
            sl = slice(128 * gi, 128 * gi + 128)
            dvn_ref[:, sl] = _dot(wmt_ref[gi], dmb[:, sl])
            dws_ref[gi] += _dot_nt(dmb[:, sl], vnb[:, sl])
        dvn = dvn_ref[...]
        dgam_ref[...] += _fold8(dvn * xhat)
        dbeta_ref[...] += _fold8(dvn)
        dxh = dvn * g_ref[...]
        dv = rstd * (dxh - jnp.mean(dxh, axis=-1, keepdims=True) - xhat * jnp.mean(dxh * xhat, axis=-1, keepdims=True))
        dpa_ref[:, 2048:4096] = _c(dv)

    row = pl.BlockSpec((t, D_MODEL), lambda i: (i, 0))
    vec = lambda w: pl.BlockSpec((1, w), lambda i: (0, 0))
    acc = lambda w: pl.BlockSpec((8, w), lambda i: (0, 0))
    wspec = pl.BlockSpec((SGU_GROUPS, 128, 128), lambda i: (0, 0, 0))
    return pl.pallas_call(
        body, name="tok_bwd",
        grid=(s // t,),
        in_specs=[pl.BlockSpec((t, NA), lambda i: (i, 0)), row, row, row, vec(4096), vec(2048), vec(2048),
                  wspec, wspec, pl.BlockSpec((128, D_MODEL), lambda i: (0, 0))],
        out_specs=[pl.BlockSpec((t, NA), lambda i: (i, 0)), row, acc(4096), acc(2048), acc(2048),
                   pl.BlockSpec((128, D_MODEL), lambda i: (0, 0)), wspec],
        out_shape=[jax.ShapeDtypeStruct((s, NA), MXU_DTYPE), jax.ShapeDtypeStruct((s, D_MODEL), F32),
                   jax.ShapeDtypeStruct((8, 4096), F32), jax.ShapeDtypeStruct((8, 2048), F32),
                   jax.ShapeDtypeStruct((8, 2048), F32), jax.ShapeDtypeStruct((128, D_MODEL), F32),
                   jax.ShapeDtypeStruct((SGU_GROUPS, 128, 128), F32)],
        scratch_shapes=[pltpu.VMEM((t, D_MODEL), F32), pltpu.VMEM((t, D_MODEL), F32)],
        compiler_params=_cparams(("arbitrary",), 2 * t * NA * 6 + 16 * t * D_MODEL * 4),
    )(proj_a, dmerged, y_a, y_b, gate_b, sgu_g, sgu_beta, wm, wmt, bias_full)


def _out_call(merged, x, target, w_out, fnw, t):
    s = x.shape[0]
    nt = s // t

    def body(mg_ref, x_ref, t_ref, w_ref, fw_ref, dh_ref, dhb_ref, dmg_ref, loss_ref, dfw_ref):
        @pl.when(pl.program_id(0) == 0)
        def _():
            dfw_ref[...] = jnp.zeros_like(dfw_ref)

        h = x_ref[...] + _dot(mg_ref[...], w_ref[...])
        r = lax.rsqrt(jnp.mean(h * h, axis=-1, keepdims=True) + EPS)
        hn = h * r
        err = hn * fw_ref[...] - t_ref[...]
        loss_ref[...] = jnp.full(loss_ref.shape, 0.5 * jnp.sum(jnp.mean(err * err, axis=-1, keepdims=True)), F32)
        dy = err * (1.0 / D_MODEL)
        dfw_ref[...] += _fold8(dy * hn)
        gw = dy * fw_ref[...]
        dh = r * gw - h * (r * r * r) * jnp.mean(h * gw, axis=-1, keepdims=True)
        dh_ref[...] = dh
        dhb = _c(dh)
        dhb_ref[...] = dhb
        dmg_ref[...] = _dot_nt(dhb, w_ref[...])

    row = pl.BlockSpec((t, D_MODEL), lambda i: (i, 0))
    return pl.pallas_call(
        body, name="out_proj_loss",
        grid=(nt,),
        in_specs=[row, row, row, pl.BlockSpec((D_MODEL, D_MODEL), lambda i: (0, 0)), pl.BlockSpec((1, D_MODEL), lambda i: (0, 0))],
        out_specs=[row, row, row, pl.BlockSpec((1, 8, 128), lambda i: (i, 0, 0)), pl.BlockSpec((8, D_MODEL), lambda i: (0, 0))],
        out_shape=[jax.ShapeDtypeStruct((s, D_MODEL), F32), jax.ShapeDtypeStruct((s, D_MODEL), MXU_DTYPE),
                   jax.ShapeDtypeStruct((s, D_MODEL), F32), jax.ShapeDtypeStruct((nt, 8, 128), F32),
                   jax.ShapeDtypeStruct((8, D_MODEL), F32)],
        compiler_params=_cparams(("arbitrary",), 2 * D_MODEL * D_MODEL * 2 + 2 * t * D_MODEL * 24),
    )(merged, x, target, w_out, fnw)


def _ssd_fwd_call(proj_b, dtb, alog, dsk, cw, cb, nw, t, ng):
    s = proj_b.shape[0]
    nt, nch = s // t, t // CHUNK

    def body(pb_ref, halo_ref, dtb_ref, al_ref, ds_ref, cw_ref, cb_ref, nw_ref, y_ref, yb_ref, hp_ref, pre_ref, ht_ref, act_ref):
        i = pl.program_id(1)

        @pl.when(i == 0)
        def _():
            ht_ref[...] = jnp.zeros_like(ht_ref)

        for gi in range(ng):
            fo, co = FW_B * gi, 768 * gi
            halo = jnp.where(i == 0, 0.0, halo_ref[:, fo:fo + 768])
            taps = _conv_taps(jnp.concatenate([halo, pb_ref[:, fo:fo + 768]], axis=0), t)
            pre = cb_ref[:, co:co + 768]
            for k in range(4):
                pre = pre + taps[k] * cw_ref[k:k + 1, co:co + 768]
            pre_ref[:, co:co + 768] = pre
            act_ref[:, co:co + 768] = pre * _sigmoid(pre)
        masks = _ssd_masks()
        a_neg = -jnp.exp(al_ref[...])

        def chunk(c, carry):
            rows = pl.ds(pl.multiple_of(c * CHUNK, CHUNK), CHUNK)
            for gi in range(ng):
                fo, co, go = FW_B * gi, 768 * gi, SSD_GW * gi
                dt = _softplus(pb_ref[rows, fo + 768:fo + 1280] + dtb_ref[:, go:go + 512])
                ht = ht_ref[gi]
                hp_ref[c, :, go:go + 512] = ht
                y, ht_new = _ssd_chunk_fwd(act_ref[rows, co:co + 512], act_ref[rows, co + 512:co + 640],
                                           act_ref[rows, co + 640:co + 768], dt, a_neg[:, go:go + 512],
                                           ds_ref[:, go:go + 512], ht, masks)
                y_ref[rows, go:go + 512] = y
                ht_ref[gi] = ht_new
            return carry

        lax.fori_loop(0, nch, chunk, 0)
        for gi in range(ng):
            fo, go = FW_B * gi, SSD_GW * gi
            zb = pb_ref[:, fo + 1280:fo + 1792]
            hh = y_ref[:, go:go + 512] * (zb * _sigmoid(zb))
            rr = lax.rsqrt(jnp.mean(hh * hh, axis=-1, keepdims=True) + EPS)
            yb_ref[:, go:go + 512] = hh * rr * nw_ref[:, go:go + 512]

    gvec = lambda w: pl.BlockSpec((1, ng * w), lambda g, i: (0, g))
    return pl.pallas_call(
        body, name="ssd_fwd",
        grid=(SSD_GROUPS // ng, nt),
        in_specs=[pl.BlockSpec((t, ng * FW_B), lambda g, i: (i, g)),
                  pl.BlockSpec((8, ng * FW_B), lambda g, i: (jnp.maximum(i * (t // 8) - 1, 0), g)),
                  gvec(512), gvec(512), gvec(512),
                  pl.BlockSpec((4, ng * 768), lambda g, i: (0, g)), gvec(768), gvec(512)],
        out_specs=[pl.BlockSpec((t, ng * SSD_GW), lambda g, i: (i, g)), pl.BlockSpec((t, ng * SSD_GW), lambda g, i: (i, g)),
                   pl.BlockSpec((nch, SSD_STATE, ng * SSD_GW), lambda g, i: (i, 0, g)),
                   pl.BlockSpec((t, ng * 768), lambda g, i: (i, g))],
        out_shape=[jax.ShapeDtypeStruct((s, D_MODEL), F32), jax.ShapeDtypeStruct((s, D_MODEL), F32),
                   jax.ShapeDtypeStruct((s // CHUNK, SSD_STATE, D_MODEL), F32),
                   jax.ShapeDtypeStruct((s, SSD_GROUPS * 768), F32)],
        scratch_shapes=[pltpu.VMEM((ng, SSD_STATE, SSD_GW), F32), pltpu.VMEM((t, ng * 768), F32)],
        compiler_params=_cparams(("parallel", "arbitrary"), ng * (2 * t * FW_B * 4 + 10 * t * SSD_GW * 4) + 24 * 1024 * 1024),
    )(proj_b, proj_b, dtb, alog, dsk, cw, cb, nw)


def _ssd_bwd_call(proj_b, pre_all, dyb, y, hprev, dtb, alog, dsk, cw, nw, t, ng):
    s = proj_b.shape[0]
    nt, nch = s // t, t // CHUNK

    def body(pb_ref, pre_ref, dyb_ref, y_ref, hp_ref, dtb_ref, al_ref, ds_ref, cw_ref, nw_ref,
             dpb_ref, a512_ref, a768_ref, dht_ref, act_ref, dsl_ref, dact_ref, dy_ref, nxt_ref):
        i = pl.program_id(1)

        @pl.when(i == 0)
        def _():
            dht_ref[...] = jnp.zeros_like(dht_ref)
            nxt_ref[...] = jnp.zeros_like(nxt_ref)
            a512_ref[...] = jnp.zeros_like(a512_ref)
            a768_ref[...] = jnp.zeros_like(a768_ref)

        for gi in range(ng):
            fo, co, go, bo = FW_B * gi, 768 * gi, SSD_GW * gi, BW_B * gi
            pre = pre_ref[:, co:co + 768]
            sp = _sigmoid(pre)
            act_ref[:, co:co + 768] = pre * sp
            dsl_ref[:, co:co + 768] = sp * (1.0 + pre * (1.0 - sp))
            zb = pb_ref[:, fo + 1280:fo + 1792]
            yv = y_ref[:, go:go + 512]
            sgz = _sigmoid(zb)
            sz = zb * sgz
            hh = yv * sz
            rr = lax.rsqrt(jnp.mean(hh * hh, axis=-1, keepdims=True) + EPS)
            dyb = dyb_ref[:, go:go + 512]
            a512_ref[gi, 0] += _fold8(dyb * (hh * rr))
            tt = dyb * nw_ref[:, go:go + 512]
            dhh = rr * tt - hh * (rr * rr * rr) * jnp.mean(hh * tt, axis=-1, keepdims=True)
            dy_ref[:, go:go + 512] = dhh * sz
            dpb_ref[:, bo:bo + 512] = _c(dhh * yv * (sgz * (1.0 + zb * (1.0 - sgz))))

        masks = _ssd_masks()
        a_neg = -jnp.exp(al_ref[...])
        rsel = _c(jnp.where(lax.shift_right_logical(_iota((SSD_GW, 128), 0), 6) == _iota((SSD_GW, 128), 1), 1.0, 0.0))

        def chunk(cc, carry):
            c = nch - 1 - cc
            rows = pl.ds(pl.multiple_of(c * CHUNK, CHUNK), CHUNK)
            for gi in range(ng):
                fo, co, go, bo = FW_B * gi, 768 * gi, SSD_GW * gi, BW_B * gi
                z = pb_ref[rows, fo + 768:fo + 1280] + dtb_ref[:, go:go + 512]
                dxs, d_b, d_c, ddtr, dht_prev, dadt, dyxs = _ssd_chunk_bwd(
                    act_ref[rows, co:co + 512], act_ref[rows, co + 512:co + 640], act_ref[rows, co + 640:co + 768],
                    _softplus(z), _sigmoid(z), a_neg[:, go:go + 512], ds_ref[:, go:go + 512],
                    hp_ref[c, :, go:go + 512], dht_ref[gi], dy_ref[rows, go:go + 512], masks)
                dht_ref[gi] = dht_prev
                dact_ref[rows, co:co + 512] = dxs
                dact_ref[rows, co + 512:co + 640] = d_b
                dact_ref[rows, co + 640:co + 768] = d_c
                dpb_ref[rows, bo + 1280:bo + 1408] = _c(_dot01_r(ddtr, rsel, 2))
                a512_ref[gi, 1] += _fold8(dyxs)
                a512_ref[gi, 2] += _fold8(dadt)
                a512_ref[gi, 3] += _fold8(ddtr)
            return carry

        lax.fori_loop(0, nch, chunk, 0)

        for gi in range(ng):
            fo, co, bo = FW_B * gi, 768 * gi, BW_B * gi
            dpre = dact_ref[:, co:co + 768] * dsl_ref[:, co:co + 768]
            xbc = pb_ref[:, fo:fo + 768]
            a768_ref[gi, 4] += _fold8(dpre)
            a768_ref[gi, 3] += _fold8(dpre * xbc)
            dpad = jnp.concatenate([dpre, nxt_ref[:, co:co + 768]], axis=0)
            dx = dpre * cw_ref[3:4, co:co + 768]
            for k in range(3):
                d_k = pltpu.roll(dpad, t + 8 - (3 - k), 0)[0:t]
                dx = dx + d_k * cw_ref[k:k + 1, co:co + 768]
                a768_ref[gi, k] += _fold8(d_k * xbc)
            nxt_ref[:, co:co + 768] = dpre[0:8]
            dpb_ref[:, bo + 512:bo + 1280] = _c(dx)

    gvec = lambda w: pl.BlockSpec((1, ng * w), lambda g, i: (0, g))
    rev = lambda w: pl.BlockSpec((t, ng * w), lambda g, i: (nt - 1 - i, g))
    return pl.pallas_call(
        body, name="ssd_bwd",
        grid=(SSD_GROUPS // ng, nt),
        in_specs=[rev(FW_B), rev(768), rev(SSD_GW), rev(SSD_GW),
                  pl.BlockSpec((nch, SSD_STATE, ng * SSD_GW), lambda g, i: (nt - 1 - i, 0, g)),
                  gvec(512), gvec(512), gvec(512),
                  pl.BlockSpec((4, ng * 768), lambda g, i: (0, g)), gvec(512)],
        out_specs=[rev(BW_B),
                   pl.BlockSpec((ng, 4, 8, 512), lambda g, i: (g, 0, 0, 0)),
                   pl.BlockSpec((ng, 5, 8, 768), lambda g, i: (g, 0, 0, 0))],
        out_shape=[jax.ShapeDtypeStruct((s, SSD_GROUPS * BW_B), MXU_DTYPE),
                   jax.ShapeDtypeStruct((SSD_GROUPS, 4, 8, 512), F32),
                   jax.ShapeDtypeStruct((SSD_GROUPS, 5, 8, 768), F32)],
        scratch_shapes=[pltpu.VMEM((ng, SSD_STATE, SSD_GW), F32), pltpu.VMEM((t, ng * 768), F32), pltpu.VMEM((t, ng * 768), F32),
                        pltpu.VMEM((t, ng * 768), F32), pltpu.VMEM((t, ng * SSD_GW), F32), pltpu.VMEM((8, ng * 768), F32)],
        compiler_params=_cparams(("parallel", "arbitrary"), ng * (2 * t * FW_B * 4 + 16 * t * SSD_GW * 4) + 24 * 1024 * 1024),
    )(proj_b, pre_all, dyb, y, hprev, dtb, alog, dsk, cw, nw)


def _rows_call(body, ins, outs, tr, name):
    r = ins[0].shape[0]
    spec = lambda a: pl.BlockSpec((tr, a.shape[1]), lambda i: (i, 0))
    est = 2 * tr * sum(a.shape[1] * jnp.dtype(a.dtype).itemsize for a in list(ins) + list(outs))
    return pl.pallas_call(
        body, name=name, grid=(r // tr,),
        in_specs=[spec(a) for a in ins], out_specs=[spec(o) for o in outs], out_shape=list(outs),
        compiler_params=_cparams(("parallel",), est),
    )(*ins)


def _add_pair(a, b, tr, name):
    def body(a_ref, b_ref, o_ref):
        o_ref[...] = a_ref[...] + b_ref[...]

    return _rows_call(body, [a, b], [jax.ShapeDtypeStruct(a.shape, F32)], tr, name)[0]


def _rs_add(p, sib, place, tr, name):
    _, r, c = p.shape
    half = r // 2
    nb = half // tr

    def body(pl_ref, p_ref, s_ref, b_ref, own_ref):
        v = p_ref[0] + s_ref[0]
        b_ref[0] = v.astype(jnp.bfloat16)

        @pl.when(pl.program_id(1) == pl_ref[0])
        def _():
            own_ref[...] = v

    return pl.pallas_call(
        body, name=name,
        grid_spec=pltpu.PrefetchScalarGridSpec(
            num_scalar_prefetch=1, grid=(nb, 4),
            in_specs=[pl.BlockSpec((1, tr, c), lambda i, k, pr: (k, pr[1] * nb + i, 0)),
                      pl.BlockSpec((1, tr, c), lambda i, k, pr: (k, i, 0))],
            out_specs=[pl.BlockSpec((1, tr, c), lambda i, k, pr: (k, i, 0)),
                       pl.BlockSpec((tr, c), lambda i, k, pr: (i, 0))]),
        out_shape=[jax.ShapeDtypeStruct((4, half, c), jnp.bfloat16), jax.ShapeDtypeStruct((half, c), F32)],
        compiler_params=_cparams(("parallel", "arbitrary"), 2 * tr * c * 14),
    )(place, p, sib)


def _sum_own_recv(own, recv, tr, name):
    r, c = own.shape

    def body(o_ref, r_ref, out_ref):
        v = o_ref[...]
        for j in range(3):
            v = v + r_ref[j].astype(F32)
        out_ref[...] = v

    return pl.pallas_call(
        body, name=name, grid=(r // tr,),
        in_specs=[pl.BlockSpec((tr, c), lambda i: (i, 0)), pl.BlockSpec((3, tr, c), lambda i: (0, i, 0))],
        out_specs=pl.BlockSpec((tr, c), lambda i: (i, 0)),
        out_shape=jax.ShapeDtypeStruct((r, c), F32),
        compiler_params=_cparams(("parallel",), 2 * tr * c * 14),
    )(own, recv)


def _sum_slots(stack, name):
    n, r, w = stack.shape

    def body(a_ref, out_ref):
        v = a_ref[0]
        for k in range(1, n):
            v = v + a_ref[k]
        out_ref[...] = v

    return pl.pallas_call(
        body, name=name, grid=(1,),
        in_specs=[pl.BlockSpec((n, r, w), lambda i: (0, 0, 0))],
        out_specs=pl.BlockSpec((r, w), lambda i: (0, 0)),
        out_shape=jax.ShapeDtypeStruct((r, w), F32),
        compiler_params=_cparams(("arbitrary",), 2 * (n + 1) * r * w * 4),
    )(stack)


def _adamw(w, g, m, v, tr, name):
    def body(w_ref, g_ref, m_ref, v_ref, d_ref, nm_ref, nv_ref):
        d_ref[...], nm_ref[...], nv_ref[...] = _adam_math(w_ref[...], g_ref[...], m_ref[...], v_ref[...])

    o = jax.ShapeDtypeStruct(w.shape, F32)
    return _rows_call(body, [w, g, m, v], [o, o, o], tr, name)


def _adam_math(w, g, m, v):
    nm = ADAM_B1 * m + (1.0 - ADAM_B1) * g
    nv = ADAM_B2 * v + (1.0 - ADAM_B2) * (g * g)
    m_hat = nm / (1.0 - ADAM_B1 ** ADAM_STEP)
    v_hat = nv / (1.0 - ADAM_B2 ** ADAM_STEP)
    return -ADAM_LR * (m_hat / (jnp.sqrt(v_hat) + ADAM_EPS) + ADAM_WD * w), nm, nv


def _adamw_halves(w, g_own, g_sib, m, v, place, tr, name):
    r, c = w.shape

    def body(pl_ref, w_ref, go_ref, gs_ref, m_ref, v_ref, g_ref, d_ref, nm_ref, nv_ref):
        first = pl_ref[1] == 0
        own, sib = go_ref[...], gs_ref[...]
        g = jnp.concatenate([jnp.where(first, own, sib), jnp.where(first, sib, own)], axis=1)
        g_ref[...] = g
        d_ref[...], nm_ref[...], nv_ref[...] = _adam_math(w_ref[...], g, m_ref[...], v_ref[...])

    full = pl.BlockSpec((tr, c), lambda i, pr: (i, 0))
    half = pl.BlockSpec((tr, c // 2), lambda i, pr: (i, 0))
    o = jax.ShapeDtypeStruct((r, c), F32)
    return pl.pallas_call(
        body, name=name,
        grid_spec=pltpu.PrefetchScalarGridSpec(num_scalar_prefetch=1, grid=(r // tr,),
                                               in_specs=[full, half, half, full, full], out_specs=[full] * 4),
        out_shape=[o] * 4,
        compiler_params=_cparams(("parallel",), 2 * tr * c * 4 * 8),
    )(place, w, g_own, g_sib, m, v)


ANY = pl.BlockSpec(memory_space=pl.ANY)


def _place():
    x, y, c = lax.axis_index("x"), lax.axis_index("y"), lax.axis_index("c")
    others = [(1 - x, y), (x, 1 - y), (1 - x, 1 - y)]
    return x, y, c, 2 * x + y, others


def _remote(src, dst, send, recv, k, to):
    return pltpu.make_async_remote_copy(src_ref=src, dst_ref=dst, send_sem=send.at[k], recv_sem=recv.at[k],
                                        device_id=to, device_id_type=MESH)


def _gather_call(win_b, wout_b, cw8):
    big = [win_b, wout_b]

    def body(win, wout, cw, g_in, g_out, g_cw, send, recv):
        x, y, c, me, others = _place()
        sib = (x, y, 1 - c)
        started = []
        for a, (src, dst) in enumerate(((win, g_in), (wout, g_out))):
            half = src.shape[0] // 2
            mine = pl.ds(c * half, half)
            for j, chip in enumerate(others):
                cp = _remote(src.at[mine], dst.at[me, mine], send, recv, 6 * a + j, (*chip, c))
                cp.start()
                started.append(cp)
        for j, chip in enumerate(others):
            cp = _remote(cw, g_cw.at[me], send, recv, 12 + j, (*chip, c))
            cp.start()
            started.append(cp)
        for a, dst in enumerate((g_in, g_out)):
            half = dst.shape[1] // 2
            mine = pl.ds(c * half, half)
            for j, chip in enumerate(others):
                kj = 2 * chip[0] + chip[1]
                _remote(dst.at[kj, mine], dst.at[kj, mine], send, recv, 6 * a + j, (*chip, c)).wait_recv()
                cp = _remote(dst.at[kj, mine], dst.at[kj, mine], send, recv, 6 * a + 3 + j, sib)
                cp.start()
                started.append(cp)
        for a, dst in enumerate((g_in, g_out)):
            half = dst.shape[1] // 2
            theirs = pl.ds((1 - c) * half, half)
            for j, chip in enumerate(others):
                kj = 2 * chip[0] + chip[1]
                _remote(dst.at[kj, theirs], dst.at[kj, theirs], send, recv, 6 * a + 3 + j, sib).wait_recv()
        for j, chip in enumerate(others):
            kj = 2 * chip[0] + chip[1]
            _remote(cw, g_cw.at[kj], send, recv, 12 + j, (*chip, c)).wait_recv()
        for cp in started:
            cp.wait_send()

    outs = [jax.ShapeDtypeStruct((4,) + a.shape, a.dtype) for a in (win_b, wout_b, cw8)]
    return pl.pallas_call(
        body, name="gather_weights",
        in_specs=[ANY, ANY, ANY], out_specs=[ANY, ANY, ANY], out_shape=outs,
        scratch_shapes=[pltpu.SemaphoreType.DMA((15,)), pltpu.SemaphoreType.DMA((15,))],
    )(win_b, wout_b, cw8)


def _rs_sibling_call(p_in, p_out, vsmall):
    def body(pin, pout, vs, sib_in, sib_out, sib_v, send, recv):
        x, y, c, me, others = _place()
        sib = (x, y, 1 - c)
        cps = []
        for a, (p, sb) in enumerate(((pin, sib_in), (pout, sib_out))):
            half = p.shape[1] // 2
            cps.append(_remote(p.at[:, pl.ds((1 - c) * half, half)], sb, send, recv, a, sib))
        cps.append(_remote(vs, sib_v, send, recv, 2, sib))
        for cp in cps:
            cp.start()
        for cp in cps:
            cp.wait_recv()
        for cp in cps:
            cp.wait_send()

    def halves(p):
        return jax.ShapeDtypeStruct((4, p.shape[1] // 2, p.shape[2]), p.dtype)

    outs = [halves(p_in), halves(p_out), jax.ShapeDtypeStruct(vsmall.shape, vsmall.dtype)]
    return pl.pallas_call(
        body, name="rs_sibling",
        in_specs=[ANY] * 3, out_specs=[ANY] * 3, out_shape=outs,
        scratch_shapes=[pltpu.SemaphoreType.DMA((3,)), pltpu.SemaphoreType.DMA((3,))],
    )(p_in, p_out, vsmall)


def _rs_join_call(f_in, f_out, nw8):
    def body(fin, fout, nw, sib_in, full_out, all_nw, send, recv):
        x, y, c, me, others = _place()
        sib = (x, y, 1 - c)
        half = fout.shape[0]
        cps = [_remote(fin, sib_in, send, recv, 0, sib),
               _remote(fout, full_out.at[pl.ds(c * half, half)], send, recv, 1, sib)]
        mine = 4 * x + 2 * y + c
        peers = []
        for r in range(1, 8):
            px, py, pc = (1 - x if r & 4 else x), (1 - y if r & 2 else y), (1 - c if r & 1 else c)
            peers.append((r, (px, py, pc), 4 * px + 2 * py + pc))
            cps.append(_remote(nw, all_nw.at[mine], send, recv, 1 + r, (px, py, pc)))
        for cp in cps:
            cp.start()
        cps[0].wait_recv()
        _remote(fout, full_out.at[pl.ds((1 - c) * half, half)], send, recv, 1, sib).wait_recv()
        for r, peer, idx in peers:
            _remote(nw, all_nw.at[idx], send, recv, 1 + r, peer).wait_recv()
        for cp in cps:
            cp.wait_send()

    outs = [jax.ShapeDtypeStruct(f_in.shape, F32), jax.ShapeDtypeStruct((2 * f_out.shape[0], f_out.shape[1]), F32),
            jax.ShapeDtypeStruct((8,) + nw8.shape, F32)]
    return pl.pallas_call(
        body, name="rs_join",
        in_specs=[ANY] * 3, out_specs=[ANY] * 3, out_shape=outs,
        scratch_shapes=[pltpu.SemaphoreType.DMA((9,)), pltpu.SemaphoreType.DMA((9,))],
    )(f_in, f_out, nw8)


def _pack(arrs):
    parts = []
    for a in arrs:
        f = a.reshape(-1).astype(F32)
        pad = (-f.shape[0]) % 1024
        parts.append(jnp.pad(f, (0, pad)).reshape(-1, 128))
    return jnp.concatenate(parts, axis=0)


def _unpack(packed, shapes):
    out, row = [], 0
    for shp in shapes:
        n = 1
        for d in shp:
            n *= d
        rows = (n + 1023) // 1024 * 8
        out.append(packed[row:row + rows].reshape(-1)[:n].reshape(shp))
        row += rows
    return out


def _expand_heads(v32):
    return jnp.repeat(v32.reshape(32), HEADDIM).reshape(1, D_MODEL)


def kernel(x, norm_w, w_in, gate_b, sgu_norm_g, sgu_norm_b, sgu_w, sgu_b, conv_w, conv_b, dt_bias, A_log, D_skip, ssd_norm_w, w_out, final_norm_w, loss_target, m_norm_w, m_w_in, m_gate_b, m_sgu_norm_g, m_sgu_norm_b, m_sgu_w, m_sgu_b, m_conv_w, m_conv_b, m_dt_bias, m_A_log, m_D_skip, m_ssd_norm_w, m_w_out, m_final_norm_w, v_norm_w, v_w_in, v_gate_b, v_sgu_norm_g, v_sgu_norm_b, v_sgu_w, v_sgu_b, v_conv_w, v_conv_b, v_dt_bias, v_A_log, v_D_skip, v_ssd_norm_w, v_w_out, v_final_norm_w):
    s = x.shape[1]
    x2 = x.reshape(s, D_MODEL)
    tgt = loss_target.reshape(s, D_MODEL)
    t_ssd, t_tok, t_out, t_row = min(T_SSD, s), min(T_TOK, s), min(T_OUT, s), min(T_ROW, s)
    tm_mm, tk_dw = min(TM_MM, s), min(TK_DW, s)
    chip = 2 * lax.axis_index("x") + lax.axis_index("y")

    cw8 = jnp.pad(conv_w[0], ((0, 4), (0, 0)))
    win_b, wout_b = _c(w_in[0]), _c(w_out[0])
    g_in, g_out, g_cw = _gather_call(win_b, wout_b, cw8)
    g_in = lax.dynamic_update_index_in_dim(g_in, win_b, chip, 0)
    g_out = lax.dynamic_update_index_in_dim(g_out, wout_b, chip, 0)
    g_cw = lax.dynamic_update_index_in_dim(g_cw, cw8, chip, 0)
    wref = jnp.transpose(g_in, (1, 0, 2)).reshape(D_MODEL, IN_W)
    w_out_full = g_out.reshape(D_MODEL, D_MODEL)
    conv_w_full = jnp.transpose(g_cw[:, 0:4, :], (1, 0, 2)).reshape(4, 3072)

    w_a = jnp.concatenate([wref[:, 0:6144], wref[:, 11296:15392]], axis=1)
    fw, bw = [], []
    for g in range(SSD_GROUPS):
        xs_g = wref[:, 8192 + 512 * g:8192 + 512 * g + 512]
        b_g = wref[:, 10240 + 128 * g:10240 + 128 * g + 128]
        c_g = wref[:, 10752 + 128 * g:10752 + 128 * g + 128]
        zb_g = wref[:, 6144 + 512 * g:6144 + 512 * g + 512]
        dt_g = wref[:, 11264 + 8 * g:11264 + 8 * g + 8]
        fw += [xs_g, b_g, c_g, jnp.repeat(dt_g, HEADDIM, axis=1), zb_g]
        bw += [zb_g, xs_g, b_g, c_g, jnp.pad(dt_g, ((0, 0), (0, 120)))]
    w_b = jnp.concatenate(fw, axis=1)
    wt_a = w_a.T
    wt_b = jnp.concatenate(bw, axis=1).T

    def group_cols(full_xs, full_bc):
        parts = []
        for g in range(SSD_GROUPS):
            parts += [full_xs[:, 512 * g:512 * g + 512], full_bc[:, 128 * g:128 * g + 128], full_bc[:, 512 + 128 * g:512 + 128 * g + 128]]
        return jnp.concatenate(parts, axis=1)

    cw_g = group_cols(conv_w_full[:, 0:2048], conv_w_full[:, 2048:3072])
    cb_g = group_cols(conv_b[:, 0:2048], conv_b[:, 2048:3072])
    dtb_e, alog_e, dsk_e = _expand_heads(dt_bias), _expand_heads(A_log), _expand_heads(D_skip)

    pos_chunk = jnp.arange(SGU_BLOCK) // CHUNK
    smask = pos_chunk[None, :] <= pos_chunk[:, None]
    wm_f = jnp.where(smask[None], sgu_w[0], 0.0)
    wm = _c(wm_f)
    wmt = _c(jnp.transpose(wm_f, (0, 2, 1)))
    bias_full = jnp.repeat(sgu_b[0].T, D_MODEL // SGU_GROUPS, axis=1)
    fnw = final_norm_w.reshape(1, D_MODEL)

    xn, xnt = _norm_call(x2, norm_w, t_row)
    proj_a = _mm(xn, w_a, tm=tm_mm, tn=1024, tk=D_MODEL, name="in_proj_a")
    proj_b = _mm(xn, w_b, tm=tm_mm, tn=1024, tk=D_MODEL, name="in_proj_b")
    y_ssd, y_b, hprev, pre_all = _ssd_fwd_call(proj_b, dtb_e, alog_e, dsk_e, cw_g, cb_g, ssd_norm_w, t_ssd, NG_SSD)
    y_a, merged, merged_t = _tok_fwd_call(proj_a, y_b, gate_b, sgu_norm_g, sgu_norm_b, wm, bias_full, t_tok)
    dh, dh_b, dmerged, loss_t, dfw8 = _out_call(merged, x2, tgt, w_out_full, fnw, t_out)

    dproj_a, dy_b, dgb8, dgam8, dbeta8, dbfull, dws = _tok_bwd_call(
        proj_a, dmerged, y_a, y_b, gate_b, sgu_norm_g, sgu_norm_b, wm, wmt, bias_full, t_tok)
    dproj_b, a512, a768 = _ssd_bwd_call(proj_b, pre_all, dy_b, y_ssd, hprev, dtb_e, alog_e, dsk_e, cw_g, ssd_norm_w, t_ssd, NG_SSD)
    dw_a = _mm(xnt, dproj_a, tm=D_MODEL, tn=1024, tk=tk_dw, name="dw_in_a")
    dw_b = _mm(xnt, dproj_b, tm=D_MODEL, tn=BW_B, tk=tk_dw, name="dw_in_b")
    dw_out_p = _mm(merged_t, dh_b, tm=D_MODEL, tn=1024, tk=tk_dw, name="dw_out")

    gb = lambda a, b: jnp.concatenate([dw_b[:, BW_B * g + a:BW_B * g + b] for g in range(SSD_GROUPS)], axis=1)
    dw_ref = jnp.concatenate([dw_a[:, 0:6144], gb(0, 512), gb(512, 1024), gb(1024, 1152), gb(1152, 1280),
                              gb(1280, 1288), dw_a[:, 6144:10240]], axis=1)
    p_in = jnp.transpose(dw_ref.reshape(D_MODEL, 4, SHARD_W), (1, 0, 2))
    p_out = dw_out_p.reshape(4, D_MODEL // 4, D_MODEL)

    s512 = jnp.sum(a512, axis=2)
    heads = lambda v: jnp.sum(v.reshape(32, HEADDIM), axis=1).reshape(1, 32)
    d_ssd_nw = s512[:, 0].reshape(1, D_MODEL)
    d_dskip = heads(s512[:, 1].reshape(D_MODEL))
    d_alog = heads(s512[:, 2].reshape(D_MODEL)) * (1.0 / HEADDIM) * (-jnp.exp(A_log))
    d_dtb = heads(s512[:, 3].reshape(D_MODEL))
    s768 = jnp.sum(a768, axis=2)
    ungroup = lambda v: jnp.concatenate([v[g, :, 0:512] for g in range(4)] + [v[g, :, 512:640] for g in range(4)]
                                        + [v[g, :, 640:768] for g in range(4)], axis=1)
    d_cw = ungroup(s768[:, 0:4])
    d_cb = ungroup(s768[:, 4:5])
    d_sgu_b = jnp.sum(dbfull.reshape(128, SGU_GROUPS, 128), axis=2).T.reshape(1, SGU_GROUPS, 128)
    d_sgu_w = jnp.where(smask[None], dws, 0.0).reshape(1, SGU_GROUPS, 128, 128)
    fold = lambda a8: jnp.sum(a8, axis=0, keepdims=True)
    small_local = [fold(dgb8), fold(dgam8), fold(dbeta8), d_sgu_w, d_sgu_b, d_cw, d_cb,
                   d_dtb, d_alog, d_dskip, d_ssd_nw, fold(dfw8).reshape(D_MODEL), jnp.sum(loss_t[:, 0, 0]).reshape(1)]
    small_shapes = [a.shape for a in small_local]
    v_local = _pack(small_local)

    core = lax.axis_index("c")
    place = jnp.stack([chip, core]).astype(jnp.int32)
    hr_i, hr_o = D_MODEL // 2, D_MODEL // 8
    sib_i, sib_o, sib_v = _rs_sibling_call(p_in, p_out, v_local)
    s1b_i, o_i = _rs_add(p_in, sib_i, place, 256, "rs_add_in")
    s1b_o, o_o = _rs_add(p_out, sib_o, place, 256, "rs_add_out")
    chip_v = _add_pair(v_local, sib_v, v_local.shape[0], "ar_add_small")
    dxn, r_i, r_o, abs_v = _dx_rs_call(dproj_a, wt_a, dproj_b, wt_b, s1b_i, s1b_o, chip_v, tm=tm_mm)
    grad_x, dnw8 = _gradx_call(x2, dxn, dh, norm_w, t_row)
    abs_v = lax.dynamic_update_index_in_dim(abs_v, chip_v, chip, 0)
    f_i = _sum_own_recv(o_i, r_i, 256, "rs_sum_in")
    f_o = _sum_own_recv(o_o, r_o, 256, "rs_sum_out")
    sib_f_i, g_w_out, all_nw = _rs_join_call(f_i, f_o, dnw8)
    g_w_out = lax.dynamic_update_slice_in_dim(g_w_out, f_o, core * hr_o, axis=0)
    all_nw = lax.dynamic_update_index_in_dim(all_nw, dnw8, 2 * chip + core, 0)
    g_nw = fold(_sum_slots(all_nw, "ar_sum_norm_w"))
    total_v = _sum_slots(abs_v, "ar_sum_small")
    (g_gb, g_gam, g_beta, g_sw, g_sb, g_cw_full, g_cb, g_dtb, g_alog, g_dsk, g_snw, g_fnw, loss1) = _unpack(total_v, small_shapes)
    g_cw_shard = lax.dynamic_slice(g_cw_full, (0, chip * 768), (4, 768)).reshape(1, 4, 768)
    loss = loss1.reshape(())

    g_w_in, d_win, nm_win, nv_win = (a.T for a in _adamw_halves(w_in[0].T, f_i.T, sib_f_i.T, m_w_in[0].T, v_w_in[0].T,
                                                                place, 296, "adamw_w_in"))
    d_wout, nm_wout, nv_wout = _adamw(w_out[0], g_w_out, m_w_out[0], v_w_out[0], 128, "adamw_w_out")
    small_w = [norm_w, gate_b, sgu_norm_g, sgu_norm_b, sgu_w, sgu_b, conv_w, conv_b, dt_bias, A_log, D_skip, ssd_norm_w, final_norm_w]
    small_m = [m_norm_w, m_gate_b, m_sgu_norm_g, m_sgu_norm_b, m_sgu_w, m_sgu_b, m_conv_w, m_conv_b, m_dt_bias, m_A_log, m_D_skip, m_ssd_norm_w, m_final_norm_w]
    small_v = [v_norm_w, v_gate_b, v_sgu_norm_g, v_sgu_norm_b, v_sgu_w, v_sgu_b, v_conv_w, v_conv_b, v_dt_bias, v_A_log, v_D_skip, v_ssd_norm_w, v_final_norm_w]
    small_g = [g_nw, g_gb, g_gam, g_beta, g_sw, g_sb, g_cw_shard, g_cb, g_dtb, g_alog, g_dsk, g_snw, g_fnw]
    shapes_w = [a.shape for a in small_w]
    small_g = [a.reshape(shp) for a, shp in zip(small_g, shapes_w)]
    pw = _pack(small_w)
    pd, pm, pv = _adamw(pw, _pack(small_g), _pack(small_m), _pack(small_v), pw.shape[0], "adamw_small")
    d_small, nm_small, nv_small = _unpack(pd, shapes_w), _unpack(pm, shapes_w), _unpack(pv, shapes_w)

    def with_big(small, win, wout):
        o = list(small)
        return o[0:1] + [win.reshape(1, D_MODEL, SHARD_W)] + o[1:12] + [wout.reshape(1, D_MODEL // 4, D_MODEL)] + o[12:13]

    grads = with_big(small_g, g_w_in, g_w_out)
    deltas = with_big(d_small, d_win, d_wout)
    new_m = with_big(nm_small, nm_win, nm_wout)
    new_v = with_big(nv_small, nv_win, nv_wout)
    return (loss, grad_x.reshape(1, s, D_MODEL), *grads, *deltas, *new_m, *new_v)
```

```python
import functools

import jax
import jax.numpy as jnp
from jax import lax
from jax.experimental import pallas as pl
from jax.experimental.pallas import tpu as pltpu

F32 = jnp.float32
MXU_DTYPE = jnp.bfloat16

D_MODEL = 2048
EPS = 1e-5
CHUNK = 64
SGU_BLOCK = 128
SGU_GROUPS = 16
SSD_GROUPS = 4
SSD_GW = 512
SSD_STATE = 128
HEADDIM = 64
IN_W = 15392
SHARD_W = IN_W // 4
FW_B = 1792
BW_B = 1408
NA = 10240

ADAM_LR = 0.001
ADAM_B1 = 0.9
ADAM_B2 = 0.999
ADAM_EPS = 1e-08
ADAM_WD = 0.01
ADAM_STEP = 10

T_SSD = 256
NG_SSD = 2
T_TOK = 128
T_OUT = 256
T_ROW = 512
TM_MM = 1024
TK_DW = 1024
VMEM_CAP = 60 * 1024 * 1024
MESH = pl.DeviceIdType.MESH


def _cparams(sem, est_bytes):
    lim = int(min(VMEM_CAP, max(32 * 1024 * 1024, est_bytes + 12 * 1024 * 1024)))
    return pltpu.CompilerParams(dimension_semantics=sem, vmem_limit_bytes=lim)


def _c(x):
    return x.astype(MXU_DTYPE)


def _dot(a, b):
    return jnp.dot(a, b, preferred_element_type=F32)


def _dot_nt(a, b):
    return lax.dot_general(a, b, (((1,), (1,)), ((), ())), preferred_element_type=F32)


def _dot_tn(a, b):
    return lax.dot_general(a, b, (((0,), (0,)), ((), ())), preferred_element_type=F32)


def _split(x, n):
    parts, r = [], x
    for _ in range(n):
        p = _c(r)
        parts.append(p)
        r = r - p.astype(F32)
    return parts


def _dot01_l(m01, x, n):
    acc = None
    for p in _split(x, n):
        t = _dot(m01, p)
        acc = t if acc is None else acc + t
    return acc


def _dot01_r(x, m01, n):
    acc = None
    for p in _split(x, n):
        t = _dot(p, m01)
        acc = t if acc is None else acc + t
    return acc


def _sigmoid(x):
    return 1.0 / (1.0 + jnp.exp(-x))


def _fold8(x):
    r, w = x.shape
    return jnp.sum(x.reshape(r // 8, 8, w), axis=0)


def _iota(shape, dim):
    return lax.broadcasted_iota(jnp.int32, shape, dim)


def _ssd_masks():
    l64 = _iota((CHUNK, SSD_GW), 0)
    s64 = jnp.bitwise_and(_iota((CHUNK, SSD_GW), 1), CHUNK - 1)
    diag = l64 == s64
    causal = l64 >= s64
    row_last = l64 == CHUNK - 1
    r4 = lax.shift_right_logical(_iota((256, 256), 0), 6)
    c4 = lax.shift_right_logical(_iota((256, 256), 1), 6)
    mask4 = r4 == c4
    return diag, causal, row_last, mask4


def _cumsum_mats(t):
    r, c = _iota((t, t), 0), _iota((t, t), 1)
    same = lax.shift_right_logical(r, 6) == lax.shift_right_logical(c, 6)
    tri = _c(jnp.where(same, jnp.where(c <= r, 1.0, 0.0), 0.0))
    trit = _c(jnp.where(same, jnp.where(c >= r, 1.0, 0.0), 0.0))
    return tri, trit


def _ssd_common(xs, bm, cm, dt, acs, masks):
    diag, causal, row_last, mask4 = masks
    row_e = jnp.sum(jnp.where(diag, acs, 0.0), axis=0, keepdims=True)
    seg = acs - row_e
    lm = jnp.exp(jnp.where(causal, seg, -1e30))
    bb, cb = _c(bm), _c(cm)
    brep = jnp.concatenate([bb] * 8, axis=0)
    cbrep = _dot_nt(cb, brep)
    m = cbrep * lm
    xdt = xs * dt
    acs_last = jnp.sum(jnp.where(row_last, acs, 0.0), axis=0, keepdims=True)
    dec = jnp.exp(acs_last - acs)
    eacs = jnp.exp(acs)
    cd = jnp.exp(acs_last)
    return dict(lm=lm, bb=bb, cb=cb, brep=brep, m=m, xdt=xdt, dec=dec, eacs=eacs, cd=cd)


def _blockdiag4(xb, mask4):
    return jnp.where(mask4, jnp.concatenate([xb] * 4, axis=0), jnp.zeros((), xb.dtype))


def _ssd_chunk_fwd(xs, bm, cm, dt, acs, d_skip, ht, masks):
    q = _ssd_common(xs, bm, cm, dt, acs, masks)
    mask4 = masks[3]
    mb, xdtb = _c(q["m"]), _c(q["xdt"])
    yd = []
    for blk in range(2):
        sl = slice(256 * blk, 256 * blk + 256)
        yd.append(_dot(mb[:, sl], _blockdiag4(xdtb[:, sl], mask4)))
    y_diag = jnp.concatenate(yd, axis=1)
    p = _dot(q["cb"], _c(ht))
    y = y_diag + p * q["eacs"] + xs * d_skip
    st = _dot_tn(q["bb"], _c(q["xdt"] * q["dec"]))
    return y, ht * q["cd"] + st


def _ssd_chunk_bwd(xs, bm, cm, dt, acs, d_skip, hprev, dht, dy, masks):
    diag, causal, row_last, mask4 = masks
    q = _ssd_common(xs, bm, cm, dt, acs, masks)
    lm, bb, cb, brep, m, xdt, dec, eacs, cd = (q[k] for k in ("lm", "bb", "cb", "brep", "m", "xdt", "dec", "eacs", "cd"))
    hb = _c(hprev)
    yoff = _dot(cb, hb) * eacs
    dyb = _c(dy)
    dpb = _c(dy * eacs)
    d_c = _dot_nt(dpb, hb)
    dh_y = _dot_tn(cb, dpb)
    mb, xdtb = _c(m), _c(xdt)
    dm_parts, dxdt_parts = [], []
    for blk in range(2):
        sl = slice(256 * blk, 256 * blk + 256)
        bd = _blockdiag4(xdtb[:, sl], mask4)
        dm_parts.append(_dot_nt(dyb[:, sl], bd))
        dxf = jnp.where(mask4, _dot_tn(mb[:, sl], dyb[:, sl]), 0.0)
        dxdt_parts.append(dxf[0:64] + dxf[64:128] + dxf[128:192] + dxf[192:256])
    dm = jnp.concatenate(dm_parts, axis=1)
    dxdt = jnp.concatenate(dxdt_parts, axis=1)
    dcbb = _c(dm * lm)
    g = dm * m
    d_c = d_c + _dot(dcbb, brep)
    dbrep = _dot_tn(dcbb, cb)
    d_b = dbrep[0:64]
    for r in range(1, 8):
        d_b = d_b + dbrep[64 * r:64 * r + 64]
    dhtb = _c(dht)
    dxd = _dot(bb, dhtb)
    xd = xdt * dec
    dxdt = dxdt + dxd * dec
    tq = dxd * xd
    d_b = d_b + _dot_nt(_c(xd), dhtb)
    dcd = jnp.sum(dht * hprev, axis=0, keepdims=True)
    col_g = jnp.sum(g, axis=0, keepdims=True)
    last = jnp.sum(tq, axis=0, keepdims=True) + dcd * cd
    qq = g - jnp.where(diag, col_g, 0.0) + dy * yoff - tq + jnp.where(row_last, last, 0.0)
    dxs = dxdt * dt + dy * d_skip
    return dxs, d_b, d_c, dht * cd + dh_y, dy * xs, qq, dxdt * xs


def _ssd_finish_dt(qq, p1, dt, sig, a_neg, trit, mask4):
    bd4 = _c(jnp.where(mask4, 1.0, 0.0))
    dacs = jnp.concatenate([_dot01_r(qq[:, 256 * b:256 * b + 256], bd4, 2) for b in range(2)], axis=1)
    da = _dot01_l(trit, dacs, 2)
    ddt = p1 + da * (a_neg * (1.0 / HEADDIM))
    return ddt * sig, da * dt


def _softplus(x):
    return jnp.maximum(x, 0.0) + jnp.log1p(jnp.exp(-jnp.abs(x)))


def _conv_taps(xpad, t):
    taps = []
    for k in range(4):
        sh = 3 - k
        v = xpad if sh == 0 else pltpu.roll(xpad, sh, 0)
        taps.append(v[8:8 + t])
    return taps


def _norm_call(x, norm_w, tm):
    s = x.shape[0]

    def body(x_ref, w_ref, xn_ref, xnt_ref):
        xv = x_ref[...]
        r = lax.rsqrt(jnp.mean(xv * xv, axis=-1, keepdims=True) + EPS)
        xn = xv * r * w_ref[...]
        xn_ref[...] = _c(xn)
        xnt_ref[...] = _c(xn.T)

    return pl.pallas_call(
        body, name="rmsnorm_in",
        grid=(s // tm,),
        in_specs=[pl.BlockSpec((tm, D_MODEL), lambda i: (i, 0)), pl.BlockSpec((1, D_MODEL), lambda i: (0, 0))],
        out_specs=[pl.BlockSpec((tm, D_MODEL), lambda i: (i, 0)), pl.BlockSpec((D_MODEL, tm), lambda i: (0, i))],
        out_shape=[jax.ShapeDtypeStruct((s, D_MODEL), MXU_DTYPE), jax.ShapeDtypeStruct((D_MODEL, s), MXU_DTYPE)],
        compiler_params=_cparams(("parallel",), 2 * tm * D_MODEL * 12),
    )(x, norm_w)


def _mm(a, b, *, tm, tn, tk, name, out_dtype=F32):
    m, k = a.shape
    n = b.shape[1]
    nk = k // tk
    assert m % tm == 0 and n % tn == 0 and k % tk == 0, (a.shape, b.shape, tm, tn, tk)
    assert nk == 1 or out_dtype == F32

    def body(a_ref, b_ref, o_ref):
        if nk == 1:
            o_ref[...] = _dot(a_ref[...], b_ref[...]).astype(out_dtype)
        else:
            @pl.when(pl.program_id(2) == 0)
            def _():
                o_ref[...] = jnp.zeros_like(o_ref)

            o_ref[...] += _dot(a_ref[...], b_ref[...])

    isz = jnp.dtype(a.dtype).itemsize
    est = 2 * (tm * tk + tk * tn) * isz + 2 * tm * tn * 4
    return pl.pallas_call(
        body, name=name,
        grid=(m // tm, n // tn, nk),
        in_specs=[pl.BlockSpec((tm, tk), lambda i, j, kk: (i, kk)), pl.BlockSpec((tk, tn), lambda i, j, kk: (kk, j))],
        out_specs=pl.BlockSpec((tm, tn), lambda i, j, kk: (i, j)),
        out_shape=jax.ShapeDtypeStruct((m, n), out_dtype),
        compiler_params=_cparams(("parallel", "parallel", "arbitrary"), est),
    )(a, b)


def _dx_rs_call(dpa, wta, dpb, wtb, sb_in, sb_out, chip_v, *, tm):
    s = dpa.shape[0]
    tka, tkb = 1024, BW_B
    nka, nkb = dpa.shape[1] // tka, dpb.shape[1] // tkb
    ni, nk = s // tm, nka + nkb

    def body(a_ref, wa_ref, b_ref, wb_ref, sbin, sbout, cv, o_ref, rc_in, rc_out, abs_v, send, recv):
        i, kk = pl.program_id(0), pl.program_id(1)

        def copies():
            x, y, c, me, others = _place()
            sends, recvs = [], []
            for j, chip in enumerate(others):
                kj = 2 * chip[0] + chip[1]
                to = (*chip, c)
                sends += [_remote(sbin.at[kj], rc_in.at[j], send, recv, j, to),
                          _remote(sbout.at[kj], rc_out.at[j], send, recv, 3 + j, to),
                          _remote(cv, abs_v.at[me], send, recv, 6 + j, to)]
                recvs += [sends[-3], sends[-2], _remote(cv, abs_v.at[kj], send, recv, 6 + j, to)]
            return sends, recvs

        @pl.when((i == 0) & (kk == 0))
        def _():
            for cp in copies()[0]:
                cp.start()

        @pl.when(kk == 0)
        def _():
            o_ref[...] = jnp.zeros_like(o_ref)

        @pl.when(kk < nka)
        def _():
            o_ref[...] += _dot(a_ref[...], wa_ref[...])

        @pl.when(kk >= nka)
        def _():
            o_ref[...] += _dot(b_ref[...], wb_ref[...])

        @pl.when((i == ni - 1) & (kk == nk - 1))
        def _():
            sends, recvs = copies()
            for cp in recvs:
                cp.wait_recv()
            for cp in sends:
                cp.wait_send()

    isz = jnp.dtype(dpa.dtype).itemsize
    est = 2 * isz * (tm * tka + tka * D_MODEL + tm * tkb + tkb * D_MODEL) + 2 * tm * D_MODEL * 4
    outs = [jax.ShapeDtypeStruct((s, D_MODEL), F32),
            jax.ShapeDtypeStruct((3,) + sb_in.shape[1:], sb_in.dtype), jax.ShapeDtypeStruct((3,) + sb_out.shape[1:], sb_out.dtype),
            jax.ShapeDtypeStruct((4,) + chip_v.shape, F32)]
    return pl.pallas_call(
        body, name="dx_matmul_rs_chips",
        grid=(ni, nk),
        in_specs=[
            pl.BlockSpec((tm, tka), lambda i, kk: (i, jnp.minimum(kk, nka - 1))),
            pl.BlockSpec((tka, D_MODEL), lambda i, kk: (jnp.minimum(kk, nka - 1), 0)),
            pl.BlockSpec((tm, tkb), lambda i, kk: (i, jnp.maximum(kk - nka, 0))),
            pl.BlockSpec((tkb, D_MODEL), lambda i, kk: (jnp.maximum(kk - nka, 0), 0)),
            ANY, ANY, ANY,
        ],
        out_specs=[pl.BlockSpec((tm, D_MODEL), lambda i, kk: (i, 0)), ANY, ANY, ANY],
        out_shape=outs,
        scratch_shapes=[pltpu.SemaphoreType.DMA((9,)), pltpu.SemaphoreType.DMA((9,))],
        compiler_params=_cparams(("arbitrary", "arbitrary"), est),
    )(dpa, wta, dpb, wtb, sb_in, sb_out, chip_v)


def _gradx_call(x, dxn, dh, norm_w, tm):
    s = x.shape[0]

    def body(x_ref, g_ref, dh_ref, w_ref, gx_ref, dw_ref):
        @pl.when(pl.program_id(0) == 0)
        def _():
            dw_ref[...] = jnp.zeros_like(dw_ref)

        xv, gv = x_ref[...], g_ref[...]
        r = lax.rsqrt(jnp.mean(xv * xv, axis=-1, keepdims=True) + EPS)
        gw = gv * w_ref[...]
        gx_ref[...] = r * gw - xv * (r * r * r) * jnp.mean(xv * gw, axis=-1, keepdims=True) + dh_ref[...]
        dw_ref[...] += _fold8(gv * (xv * r))

    row = pl.BlockSpec((tm, D_MODEL), lambda i: (i, 0))
    return pl.pallas_call(
        body, name="grad_x",
        grid=(s // tm,),
        in_specs=[row, row, row, pl.BlockSpec((1, D_MODEL), lambda i: (0, 0))],
        out_specs=[row, pl.BlockSpec((8, D_MODEL), lambda i: (0, 0))],
        out_shape=[jax.ShapeDtypeStruct((s, D_MODEL), F32), jax.ShapeDtypeStruct((8, D_MODEL), F32)],
        compiler_params=_cparams(("arbitrary",), 2 * tm * D_MODEL * 16),
    )(x, dxn, dh, norm_w)


def _layernorm_stats(v):
    mu = jnp.mean(v, axis=-1, keepdims=True)
    vc = v - mu
    var = jnp.mean(vc * vc, axis=-1, keepdims=True)
    return vc * lax.rsqrt(var + EPS), lax.rsqrt(var + EPS)


def _tok_fwd_call(proj_a, y_b, gate_b, sgu_g, sgu_beta, wm, bias_full, t):
    s = proj_a.shape[0]

    def body(pa_ref, yb_ref, gb_ref, g_ref, be_ref, wm_ref, bf_ref, ya_ref, mg_ref, mgt_ref, mix_ref):
        u = pa_ref[:, 0:2048].astype(F32)
        v = pa_ref[:, 2048:4096].astype(F32)
        za = pa_ref[:, 4096:6144].astype(F32)
        xhat, _ = _layernorm_stats(v)
        vnb = _c(xhat * g_ref[...] + be_ref[...])
        for gi in range(SGU_GROUPS):
            sl = slice(128 * gi, 128 * gi + 128)
            mix_ref[:, sl] = _dot(wm_ref[gi], vnb[:, sl])
        mixed = mix_ref[...] + bf_ref[...]
        y_a = u * mixed * (za * _sigmoid(za))
        g0 = _sigmoid(pa_ref[:, 6144:8192].astype(F32) + gb_ref[:, 0:2048])
        g1 = _sigmoid(pa_ref[:, 8192:10240].astype(F32) + gb_ref[:, 2048:4096])
        merged = g0 * y_a + g1 * yb_ref[...].astype(F32)
        ya_ref[...] = _c(y_a)
        mg_ref[...] = _c(merged)
        mgt_ref[...] = _c(merged.T)

    row = pl.BlockSpec((t, D_MODEL), lambda i: (i, 0))
    vec = lambda w: pl.BlockSpec((1, w), lambda i: (0, 0))
    return pl.pallas_call(
        body, name="tok_fwd",
        grid=(s // t,),
        in_specs=[pl.BlockSpec((t, NA), lambda i: (i, 0)), row, vec(4096), vec(2048), vec(2048),
                  pl.BlockSpec((SGU_GROUPS, 128, 128), lambda i: (0, 0, 0)), pl.BlockSpec((128, D_MODEL), lambda i: (0, 0))],
        out_specs=[row, row, pl.BlockSpec((D_MODEL, t), lambda i: (0, i))],
        out_shape=[jax.ShapeDtypeStruct((s, D_MODEL), MXU_DTYPE), jax.ShapeDtypeStruct((s, D_MODEL), MXU_DTYPE),
                   jax.ShapeDtypeStruct((D_MODEL, s), MXU_DTYPE)],
        scratch_shapes=[pltpu.VMEM((t, D_MODEL), F32)],
        compiler_params=_cparams(("parallel",), 2 * t * NA * 4 + 12 * t * D_MODEL * 4),
    )(proj_a, y_b, gate_b, sgu_g, sgu_beta, wm, bias_full)


def _tok_bwd_call(proj_a, dmerged, y_a, y_b, gate_b, sgu_g, sgu_beta, wm, wmt, bias_full, t):
    s = proj_a.shape[0]

    def body(pa_ref, dm_ref, ya_ref, yb_ref, gb_ref, g_ref, be_ref, wm_ref, wmt_ref, bf_ref,
             dpa_ref, dyb_ref, dgb_ref, dgam_ref, dbeta_ref, dbf_ref, dws_ref, mix_ref, dvn_ref):
        @pl.when(pl.program_id(0) == 0)
        def _():
            dgb_ref[...] = jnp.zeros_like(dgb_ref)
            dgam_ref[...] = jnp.zeros_like(dgam_ref)
            dbeta_ref[...] = jnp.zeros_like(dbeta_ref)
            dbf_ref[...] = jnp.zeros_like(dbf_ref)
            dws_ref[...] = jnp.zeros_like(dws_ref)

        u = pa_ref[:, 0:2048].astype(F32)
        v = pa_ref[:, 2048:4096].astype(F32)
        za = pa_ref[:, 4096:6144].astype(F32)
        xhat, rstd = _layernorm_stats(v)
        vnb = _c(xhat * g_ref[...] + be_ref[...])
        for gi in range(SGU_GROUPS):
            sl = slice(128 * gi, 128 * gi + 128)
            mix_ref[:, sl] = _dot(wm_ref[gi], vnb[:, sl])
        mixed = mix_ref[...] + bf_ref[...]
        sig = _sigmoid(za)
        sz = za * sig
        dm = dm_ref[...].astype(F32)
        y_a = ya_ref[...].astype(F32)
        g0 = _sigmoid(pa_ref[:, 6144:8192].astype(F32) + gb_ref[:, 0:2048])
        g1 = _sigmoid(pa_ref[:, 8192:10240].astype(F32) + gb_ref[:, 2048:4096])
        dgl0 = dm * y_a * g0 * (1.0 - g0)
        dgl1 = dm * yb_ref[...].astype(F32) * g1 * (1.0 - g1)
        dyb_ref[...] = _c(dm * g1)
        dya = dm * g0
        dpa_ref[:, 6144:8192] = _c(dgl0)
        dpa_ref[:, 8192:10240] = _c(dgl1)
        dgb_ref[:, 0:2048] += _fold8(dgl0)
        dgb_ref[:, 2048:4096] += _fold8(dgl1)
        dpa_ref[:, 0:2048] = _c(dya * mixed * sz)
        dpa_ref[:, 4096:6144] = _c(dya * (u * mixed) * (sig * (1.0 + za * (1.0 - sig))))
        dmixed = dya * u * sz
        dbf_ref[...] += dmixed
        dmb = _c(dmixed)
        for gi in range(SGU_GROUPS):
            sl = slice(128 * gi, 128 * gi + 128)
            dvn_ref[:, sl] = _dot(wmt_ref[gi], dmb[:, sl])
            dws_ref[gi] += _dot_nt(dmb[:, sl], vnb[:, sl])
        dvn = dvn_ref[...]
        dgam_ref[...] += _fold8(dvn * xhat)
        dbeta_ref[...] += _fold8(dvn)
        dxh = dvn * g_ref[...]
        dv = rstd * (dxh - jnp.mean(dxh, axis=-1, keepdims=True) - xhat * jnp.mean(dxh * xhat, axis=-1, keepdims=True))
        dpa_ref[:, 2048:4096] = _c(dv)

    row = pl.BlockSpec((t, D_MODEL), lambda i: (i, 0))
    vec = lambda w: pl.BlockSpec((1, w), lambda i: (0, 0))
    acc = lambda w: pl.BlockSpec((8, w), lambda i: (0, 0))
    wspec = pl.BlockSpec((SGU_GROUPS, 128, 128), lambda i: (0, 0, 0))
    return pl.pallas_call(
        body, name="tok_bwd",
        grid=(s // t,),
        in_specs=[pl.BlockSpec((t, NA), lambda i: (i, 0)), row, row, row, vec(4096), vec(2048), vec(2048),
                  wspec, wspec, pl.BlockSpec((128, D_MODEL), lambda i: (0, 0))],
        out_specs=[pl.BlockSpec((t, NA), lambda i: (i, 0)), row, acc(4096), acc(2048), acc(2048),
                   pl.BlockSpec((128, D_MODEL), lambda i: (0, 0)), wspec],
        out_shape=[jax.ShapeDtypeStruct((s, NA), MXU_DTYPE), jax.ShapeDtypeStruct((s, D_MODEL), MXU_DTYPE),
                   jax.ShapeDtypeStruct((8, 4096), F32), jax.ShapeDtypeStruct((8, 2048), F32),
                   jax.ShapeDtypeStruct((8, 2048), F32), jax.ShapeDtypeStruct((128, D_MODEL), F32),
                   jax.ShapeDtypeStruct((SGU_GROUPS, 128, 128), F32)],
        scratch_shapes=[pltpu.VMEM((t, D_MODEL), F32), pltpu.VMEM((t, D_MODEL), F32)],
        compiler_params=_cparams(("arbitrary",), 2 * t * NA * 6 + 16 * t * D_MODEL * 4),
    )(proj_a, dmerged, y_a, y_b, gate_b, sgu_g, sgu_beta, wm, wmt, bias_full)


def _out_call(merged, x, target, w_out, fnw, t):
    s = x.shape[0]
    nt = s // t

    def body(mg_ref, x_ref, t_ref, w_ref, fw_ref, dh_ref, dhb_ref, dmg_ref, loss_ref, dfw_ref):
        @pl.when(pl.program_id(0) == 0)
        def _():
            dfw_ref[...] = jnp.zeros_like(dfw_ref)

        h = x_ref[...] + _dot(mg_ref[...], w_ref[...])
        r = lax.rsqrt(jnp.mean(h * h, axis=-1, keepdims=True) + EPS)
        hn = h * r
        err = hn * fw_ref[...] - t_ref[...]
        loss_ref[...] = jnp.full(loss_ref.shape, 0.5 * jnp.sum(jnp.mean(err * err, axis=-1, keepdims=True)), F32)
        dy = err * (1.0 / D_MODEL)
        dfw_ref[...] += _fold8(dy * hn)
        gw = dy * fw_ref[...]
        dh = r * gw - h * (r * r * r) * jnp.mean(h * gw, axis=-1, keepdims=True)
        dh_ref[...] = dh
        dhb = _c(dh)
        dhb_ref[...] = dhb
        dmg_ref[...] = _c(_dot_nt(dhb, w_ref[...]))

    row = pl.BlockSpec((t, D_MODEL), lambda i: (i, 0))
    return pl.pallas_call(
        body, name="out_proj_loss",
        grid=(nt,),
        in_specs=[row, row, row, pl.BlockSpec((D_MODEL, D_MODEL), lambda i: (0, 0)), pl.BlockSpec((1, D_MODEL), lambda i: (0, 0))],
        out_specs=[row, row, row, pl.BlockSpec((1, 8, 128), lambda i: (i, 0, 0)), pl.BlockSpec((8, D_MODEL), lambda i: (0, 0))],
        out_shape=[jax.ShapeDtypeStruct((s, D_MODEL), F32), jax.ShapeDtypeStruct((s, D_MODEL), MXU_DTYPE),
                   jax.ShapeDtypeStruct((s, D_MODEL), MXU_DTYPE), jax.ShapeDtypeStruct((nt, 8, 128), F32),
                   jax.ShapeDtypeStruct((8, D_MODEL), F32)],
        compiler_params=_cparams(("arbitrary",), 2 * D_MODEL * D_MODEL * 2 + 2 * t * D_MODEL * 24),
    )(merged, x, target, w_out, fnw)


def _ssd_fwd_call(proj_b, dtb, alog, dsk, cw, cb, nw, t, ng):
    s = proj_b.shape[0]
    nt, nch = s // t, t // CHUNK

    def body(pb_ref, halo_ref, dtb_ref, al_ref, ds_ref, cw_ref, cb_ref, nw_ref, y_ref, yb_ref, hp_ref, pre_ref,
             ht_ref, act_ref, dt_ref, acs_ref):
        i = pl.program_id(1)

        @pl.when(i == 0)
        def _():
            ht_ref[...] = jnp.zeros_like(ht_ref)

        tri, _ = _cumsum_mats(t)
        a_neg = -jnp.exp(al_ref[...])
        for gi in range(ng):
            fo, co, go = FW_B * gi, 768 * gi, SSD_GW * gi
            halo = jnp.where(i == 0, 0.0, halo_ref[:, fo:fo + 768])
            taps = _conv_taps(jnp.concatenate([halo, pb_ref[:, fo:fo + 768]], axis=0), t)
            pre = cb_ref[:, co:co + 768]
            for k in range(4):
                pre = pre + taps[k] * cw_ref[k:k + 1, co:co + 768]
            pre_ref[:, co:co + 768] = pre
            act_ref[:, co:co + 768] = pre * _sigmoid(pre)
            dt = _softplus(pb_ref[:, fo + 768:fo + 1280] + dtb_ref[:, go:go + 512])
            dt_ref[:, go:go + 512] = dt
            acs_ref[:, go:go + 512] = _dot01_l(tri, dt * a_neg[:, go:go + 512], 3)
        masks = _ssd_masks()

        def chunk(c, carry):
            rows = pl.ds(pl.multiple_of(c * CHUNK, CHUNK), CHUNK)
            for gi in range(ng):
                co, go = 768 * gi, SSD_GW * gi
                ht = ht_ref[gi]
                hp_ref[c, :, go:go + 512] = ht
                y, ht_new = _ssd_chunk_fwd(act_ref[rows, co:co + 512], act_ref[rows, co + 512:co + 640],
                                           act_ref[rows, co + 640:co + 768], dt_ref[rows, go:go + 512],
                                           acs_ref[rows, go:go + 512], ds_ref[:, go:go + 512], ht, masks)
                y_ref[rows, go:go + 512] = y
                ht_ref[gi] = ht_new
            return carry

        lax.fori_loop(0, nch, chunk, 0)
        for gi in range(ng):
            fo, go = FW_B * gi, SSD_GW * gi
            zb = pb_ref[:, fo + 1280:fo + 1792]
            hh = y_ref[:, go:go + 512] * (zb * _sigmoid(zb))
            rr = lax.rsqrt(jnp.mean(hh * hh, axis=-1, keepdims=True) + EPS)
            yb_ref[:, go:go + 512] = _c(hh * rr * nw_ref[:, go:go + 512])

    gvec = lambda w: pl.BlockSpec((1, ng * w), lambda g, i: (0, g))
    return pl.pallas_call(
        body, name="ssd_fwd",
        grid=(SSD_GROUPS // ng, nt),
        in_specs=[pl.BlockSpec((t, ng * FW_B), lambda g, i: (i, g)),
                  pl.BlockSpec((8, ng * FW_B), lambda g, i: (jnp.maximum(i * (t // 8) - 1, 0), g)),
                  gvec(512), gvec(512), gvec(512),
                  pl.BlockSpec((4, ng * 768), lambda g, i: (0, g)), gvec(768), gvec(512)],
        out_specs=[pl.BlockSpec((t, ng * SSD_GW), lambda g, i: (i, g)), pl.BlockSpec((t, ng * SSD_GW), lambda g, i: (i, g)),
                   pl.BlockSpec((nch, SSD_STATE, ng * SSD_GW), lambda g, i: (i, 0, g)),
                   pl.BlockSpec((t, ng * 768), lambda g, i: (i, g))],
        out_shape=[jax.ShapeDtypeStruct((s, D_MODEL), F32), jax.ShapeDtypeStruct((s, D_MODEL), MXU_DTYPE),
                   jax.ShapeDtypeStruct((s // CHUNK, SSD_STATE, D_MODEL), F32),
                   jax.ShapeDtypeStruct((s, SSD_GROUPS * 768), F32)],
        scratch_shapes=[pltpu.VMEM((ng, SSD_STATE, SSD_GW), F32), pltpu.VMEM((t, ng * 768), F32),
                        pltpu.VMEM((t, ng * SSD_GW), F32), pltpu.VMEM((t, ng * SSD_GW), F32)],
        compiler_params=_cparams(("parallel", "arbitrary"), ng * (2 * t * FW_B * 4 + 12 * t * SSD_GW * 4) + 24 * 1024 * 1024),
    )(proj_b, proj_b, dtb, alog, dsk, cw, cb, nw)


def _ssd_bwd_call(proj_b, pre_all, dyb, y, hprev, dtb, alog, dsk, cw, nw, t, ng):
    s = proj_b.shape[0]
    nt, nch = s // t, t // CHUNK

    def body(pb_ref, pre_ref, dyb_ref, y_ref, hp_ref, dtb_ref, al_ref, ds_ref, cw_ref, nw_ref,
             dpb_ref, a512_ref, a768_ref, dht_ref, act_ref, dsl_ref, dact_ref, dy_ref, nxt_ref, dt_ref, acs_ref, q_ref, p1_ref):
        i = pl.program_id(1)

        @pl.when(i == 0)
        def _():
            dht_ref[...] = jnp.zeros_like(dht_ref)
            nxt_ref[...] = jnp.zeros_like(nxt_ref)
            a512_ref[...] = jnp.zeros_like(a512_ref)
            a768_ref[...] = jnp.zeros_like(a768_ref)

        tri, trit = _cumsum_mats(t)
        a_neg = -jnp.exp(al_ref[...])
        for gi in range(ng):
            fo, co, go, bo = FW_B * gi, 768 * gi, SSD_GW * gi, BW_B * gi
            dt = _softplus(pb_ref[:, fo + 768:fo + 1280] + dtb_ref[:, go:go + 512])
            dt_ref[:, go:go + 512] = dt
            acs_ref[:, go:go + 512] = _dot01_l(tri, dt * a_neg[:, go:go + 512], 3)
            pre = pre_ref[:, co:co + 768]
            sp = _sigmoid(pre)
            act_ref[:, co:co + 768] = pre * sp
            dsl_ref[:, co:co + 768] = sp * (1.0 + pre * (1.0 - sp))
            zb = pb_ref[:, fo + 1280:fo + 1792]
            yv = y_ref[:, go:go + 512]
            sgz = _sigmoid(zb)
            sz = zb * sgz
            hh = yv * sz
            rr = lax.rsqrt(jnp.mean(hh * hh, axis=-1, keepdims=True) + EPS)
            dyb = dyb_ref[:, go:go + 512].astype(F32)
            a512_ref[gi, 0] += _fold8(dyb * (hh * rr))
            tt = dyb * nw_ref[:, go:go + 512]
            dhh = rr * tt - hh * (rr * rr * rr) * jnp.mean(hh * tt, axis=-1, keepdims=True)
            dy_ref[:, go:go + 512] = dhh * sz
            dpb_ref[:, bo:bo + 512] = _c(dhh * yv * (sgz * (1.0 + zb * (1.0 - sgz))))

        masks = _ssd_masks()

        def chunk(cc, carry):
            c = nch - 1 - cc
            rows = pl.ds(pl.multiple_of(c * CHUNK, CHUNK), CHUNK)
            for gi in range(ng):
                co, go = 768 * gi, SSD_GW * gi
                dxs, d_b, d_c, dht_prev, dyxs, qq, p1 = _ssd_chunk_bwd(
                    act_ref[rows, co:co + 512], act_ref[rows, co + 512:co + 640], act_ref[rows, co + 640:co + 768],
                    dt_ref[rows, go:go + 512], acs_ref[rows, go:go + 512], ds_ref[:, go:go + 512],
                    hp_ref[c, :, go:go + 512], dht_ref[gi], dy_ref[rows, go:go + 512], masks)
                dht_ref[gi] = dht_prev
                dact_ref[rows, co:co + 512] = dxs
                dact_ref[rows, co + 512:co + 640] = d_b
                dact_ref[rows, co + 640:co + 768] = d_c
                q_ref[rows, go:go + 512] = qq
                p1_ref[rows, go:go + 512] = p1
                a512_ref[gi, 1] += _fold8(dyxs)
            return carry

        lax.fori_loop(0, nch, chunk, 0)

        rsel = _c(jnp.where(lax.shift_right_logical(_iota((SSD_GW, 128), 0), 6) == _iota((SSD_GW, 128), 1), 1.0, 0.0))
        for gi in range(ng):
            fo, go, bo = FW_B * gi, SSD_GW * gi, BW_B * gi
            sig = _sigmoid(pb_ref[:, fo + 768:fo + 1280] + dtb_ref[:, go:go + 512])
            ddtr, dadt = _ssd_finish_dt(q_ref[:, go:go + 512], p1_ref[:, go:go + 512], dt_ref[:, go:go + 512], sig,
                                        a_neg[:, go:go + 512], trit, masks[3])
            dpb_ref[:, bo + 1280:bo + 1408] = _c(_dot01_r(ddtr, rsel, 2))
            a512_ref[gi, 2] += _fold8(dadt)
            a512_ref[gi, 3] += _fold8(ddtr)

        for gi in range(ng):
            fo, co, bo = FW_B * gi, 768 * gi, BW_B * gi
            dpre = dact_ref[:, co:co + 768] * dsl_ref[:, co:co + 768]
            xbc = pb_ref[:, fo:fo + 768]
            a768_ref[gi, 4] += _fold8(dpre)
            a768_ref[gi, 3] += _fold8(dpre * xbc)
            dpad = jnp.concatenate([dpre, nxt_ref[:, co:co + 768]], axis=0)
            dx = dpre * cw_ref[3:4, co:co + 768]
            for k in range(3):
                d_k = pltpu.roll(dpad, t + 8 - (3 - k), 0)[0:t]
                dx = dx + d_k * cw_ref[k:k + 1, co:co + 768]
                a768_ref[gi, k] += _fold8(d_k * xbc)
            nxt_ref[:, co:co + 768] = dpre[0:8]
            dpb_ref[:, bo + 512:bo + 1280] = _c(dx)

    gvec = lambda w: pl.BlockSpec((1, ng * w), lambda g, i: (0, g))
    rev = lambda w: pl.BlockSpec((t, ng * w), lambda g, i: (nt - 1 - i, g))
    return pl.pallas_call(
        body, name="ssd_bwd",
        grid=(SSD_GROUPS // ng, nt),
        in_specs=[rev(FW_B), rev(768), rev(SSD_GW), rev(SSD_GW),
                  pl.BlockSpec((nch, SSD_STATE, ng * SSD_GW), lambda g, i: (nt - 1 - i, 0, g)),
                  gvec(512), gvec(512), gvec(512),
                  pl.BlockSpec((4, ng * 768), lambda g, i: (0, g)), gvec(512)],
        out_specs=[rev(BW_B),
                   pl.BlockSpec((ng, 4, 8, 512), lambda g, i: (g, 0, 0, 0)),
                   pl.BlockSpec((ng, 5, 8, 768), lambda g, i: (g, 0, 0, 0))],
        out_shape=[jax.ShapeDtypeStruct((s, SSD_GROUPS * BW_B), MXU_DTYPE),
                   jax.ShapeDtypeStruct((SSD_GROUPS, 4, 8, 512), F32),
                   jax.ShapeDtypeStruct((SSD_GROUPS, 5, 8, 768), F32)],
        scratch_shapes=[pltpu.VMEM((ng, SSD_STATE, SSD_GW), F32), pltpu.VMEM((t, ng * 768), F32), pltpu.VMEM((t, ng * 768), F32),
                        pltpu.VMEM((t, ng * 768), F32), pltpu.VMEM((t, ng * SSD_GW), F32), pltpu.VMEM((8, ng * 768), F32)]
        + [pltpu.VMEM((t, ng * SSD_GW), F32)] * 4,
        compiler_params=_cparams(("parallel", "arbitrary"), ng * (2 * t * FW_B * 4 + 20 * t * SSD_GW * 4) + 24 * 1024 * 1024),
    )(proj_b, pre_all, dyb, y, hprev, dtb, alog, dsk, cw, nw)


def _rows_call(body, ins, outs, tr, name):
    r = ins[0].shape[0]
    spec = lambda a: pl.BlockSpec((tr, a.shape[1]), lambda i: (i, 0))
    est = 2 * tr * sum(a.shape[1] * jnp.dtype(a.dtype).itemsize for a in list(ins) + list(outs))
    return pl.pallas_call(
        body, name=name, grid=(r // tr,),
        in_specs=[spec(a) for a in ins], out_specs=[spec(o) for o in outs], out_shape=list(outs),
        compiler_params=_cparams(("parallel",), est),
    )(*ins)


def _add_pair(a, b, tr, name):
    def body(a_ref, b_ref, o_ref):
        o_ref[...] = a_ref[...] + b_ref[...]

    return _rows_call(body, [a, b], [jax.ShapeDtypeStruct(a.shape, F32)], tr, name)[0]


def _rs_add(p, sib, place, tr, name):
    _, r, c = p.shape
    half = r // 2
    nb = half // tr

    def body(pl_ref, p_ref, s_ref, b_ref, own_ref):
        v = p_ref[0] + s_ref[0]
        b_ref[0] = v.astype(jnp.bfloat16)

        @pl.when(pl.program_id(1) == pl_ref[0])
        def _():
            own_ref[...] = v

    return pl.pallas_call(
        body, name=name,
        grid_spec=pltpu.PrefetchScalarGridSpec(
            num_scalar_prefetch=1, grid=(nb, 4),
            in_specs=[pl.BlockSpec((1, tr, c), lambda i, k, pr: (k, pr[1] * nb + i, 0)),
                      pl.BlockSpec((1, tr, c), lambda i, k, pr: (k, i, 0))],
            out_specs=[pl.BlockSpec((1, tr, c), lambda i, k, pr: (k, i, 0)),
                       pl.BlockSpec((tr, c), lambda i, k, pr: (i, 0))]),
        out_shape=[jax.ShapeDtypeStruct((4, half, c), jnp.bfloat16), jax.ShapeDtypeStruct((half, c), F32)],
        compiler_params=_cparams(("parallel", "arbitrary"), 2 * tr * c * 14),
    )(place, p, sib)


def _sum_own_recv(own, recv, tr, name):
    r, c = own.shape

    def body(o_ref, r_ref, out_ref):
        v = o_ref[...]
        for j in range(3):
            v = v + r_ref[j].astype(F32)
        out_ref[...] = v

    return pl.pallas_call(
        body, name=name, grid=(r // tr,),
        in_specs=[pl.BlockSpec((tr, c), lambda i: (i, 0)), pl.BlockSpec((3, tr, c), lambda i: (0, i, 0))],
        out_specs=pl.BlockSpec((tr, c), lambda i: (i, 0)),
        out_shape=jax.ShapeDtypeStruct((r, c), F32),
        compiler_params=_cparams(("parallel",), 2 * tr * c * 14),
    )(own, recv)


def _sum_slots(stack, name):
    n, r, w = stack.shape

    def body(a_ref, out_ref):
        v = a_ref[0]
        for k in range(1, n):
            v = v + a_ref[k]
        out_ref[...] = v

    return pl.pallas_call(
        body, name=name, grid=(1,),
        in_specs=[pl.BlockSpec((n, r, w), lambda i: (0, 0, 0))],
        out_specs=pl.BlockSpec((r, w), lambda i: (0, 0)),
        out_shape=jax.ShapeDtypeStruct((r, w), F32),
        compiler_params=_cparams(("arbitrary",), 2 * (n + 1) * r * w * 4),
    )(stack)


def _adamw(w, g, m, v, tr, name):
    def body(w_ref, g_ref, m_ref, v_ref, d_ref, nm_ref, nv_ref):
        d_ref[...], nm_ref[...], nv_ref[...] = _adam_math(w_ref[...], g_ref[...], m_ref[...], v_ref[...])

    o = jax.ShapeDtypeStruct(w.shape, F32)
    return _rows_call(body, [w, g, m, v], [o, o, o], tr, name)


def _adam_math(w, g, m, v):
    nm = ADAM_B1 * m + (1.0 - ADAM_B1) * g
    nv = ADAM_B2 * v + (1.0 - ADAM_B2) * (g * g)
    m_hat = nm / (1.0 - ADAM_B1 ** ADAM_STEP)
    v_hat = nv / (1.0 - ADAM_B2 ** ADAM_STEP)
    return -ADAM_LR * (m_hat / (jnp.sqrt(v_hat) + ADAM_EPS) + ADAM_WD * w), nm, nv


def _adamw_halves(w, g_own, g_sib, m, v, place, tr, name):
    r, c = w.shape

    def body(pl_ref, w_ref, go_ref, gs_ref, m_ref, v_ref, g_ref, d_ref, nm_ref, nv_ref):
        first = pl_ref[1] == 0
        own, sib = go_ref[...], gs_ref[...]
        g = jnp.concatenate([jnp.where(first, own, sib), jnp.where(first, sib, own)], axis=1)
        g_ref[...] = g
        d_ref[...], nm_ref[...], nv_ref[...] = _adam_math(w_ref[...], g, m_ref[...], v_ref[...])

    full = pl.BlockSpec((tr, c), lambda i, pr: (i, 0))
    half = pl.BlockSpec((tr, c // 2), lambda i, pr: (i, 0))
    o = jax.ShapeDtypeStruct((r, c), F32)
    return pl.pallas_call(
        body, name=name,
        grid_spec=pltpu.PrefetchScalarGridSpec(num_scalar_prefetch=1, grid=(r // tr,),
                                               in_specs=[full, half, half, full, full], out_specs=[full] * 4),
        out_shape=[o] * 4,
        compiler_params=_cparams(("parallel",), 2 * tr * c * 4 * 8),
    )(place, w, g_own, g_sib, m, v)


ANY = pl.BlockSpec(memory_space=pl.ANY)


def _place():
    x, y, c = lax.axis_index("x"), lax.axis_index("y"), lax.axis_index("c")
    others = [(1 - x, y), (x, 1 - y), (1 - x, 1 - y)]
    return x, y, c, 2 * x + y, others


def _remote(src, dst, send, recv, k, to):
    return pltpu.make_async_remote_copy(src_ref=src, dst_ref=dst, send_sem=send.at[k], recv_sem=recv.at[k],
                                        device_id=to, device_id_type=MESH)


def _gather_call(win_b, wout_b, cw8):
    big = [win_b, wout_b]

    def body(win, wout, cw, g_in, g_out, g_cw, send, recv):
        x, y, c, me, others = _place()
        sib = (x, y, 1 - c)
        started = []
        for a, (src, dst) in enumerate(((win, g_in), (wout, g_out))):
            half = src.shape[0] // 2
            mine = pl.ds(c * half, half)
            for j, chip in enumerate(others):
                cp = _remote(src.at[mine], dst.at[me, mine], send, recv, 6 * a + j, (*chip, c))
                cp.start()
                started.append(cp)
        for j, chip in enumerate(others):
            cp = _remote(cw, g_cw.at[me], send, recv, 12 + j, (*chip, c))
            cp.start()
            started.append(cp)
        for a, dst in enumerate((g_in, g_out)):
            half = dst.shape[1] // 2
            mine = pl.ds(c * half, half)
            for j, chip in enumerate(others):
                kj = 2 * chip[0] + chip[1]
                _remote(dst.at[kj, mine], dst.at[kj, mine], send, recv, 6 * a + j, (*chip, c)).wait_recv()
                cp = _remote(dst.at[kj, mine], dst.at[kj, mine], send, recv, 6 * a + 3 + j, sib)
                cp.start()
                started.append(cp)
        for a, dst in enumerate((g_in, g_out)):
            half = dst.shape[1] // 2
            theirs = pl.ds((1 - c) * half, half)
            for j, chip in enumerate(others):
                kj = 2 * chip[0] + chip[1]
                _remote(dst.at[kj, theirs], dst.at[kj, theirs], send, recv, 6 * a + 3 + j, sib).wait_recv()
        for j, chip in enumerate(others):
            kj = 2 * chip[0] + chip[1]
            _remote(cw, g_cw.at[kj], send, recv, 12 + j, (*chip, c)).wait_recv()
        for cp in started:
            cp.wait_send()

    outs = [jax.ShapeDtypeStruct((4,) + a.shape, a.dtype) for a in (win_b, wout_b, cw8)]
    return pl.pallas_call(
        body, name="gather_weights",
        in_specs=[ANY, ANY, ANY], out_specs=[ANY, ANY, ANY], out_shape=outs,
        scratch_shapes=[pltpu.SemaphoreType.DMA((15,)), pltpu.SemaphoreType.DMA((15,))],
    )(win_b, wout_b, cw8)


def _rs_sibling_call(p_in, p_out, vsmall):
    def body(pin, pout, vs, sib_in, sib_out, sib_v, send, recv):
        x, y, c, me, others = _place()
        sib = (x, y, 1 - c)
        cps = []
        for a, (p, sb) in enumerate(((pin, sib_in), (pout, sib_out))):
            half = p.shape[1] // 2
            cps.append(_remote(p.at[:, pl.ds((1 - c) * half, half)], sb, send, recv, a, sib))
        cps.append(_remote(vs, sib_v, send, recv, 2, sib))
        for cp in cps:
            cp.start()
        for cp in cps:
            cp.wait_recv()
        for cp in cps:
            cp.wait_send()

    def halves(p):
        return jax.ShapeDtypeStruct((4, p.shape[1] // 2, p.shape[2]), p.dtype)

    outs = [halves(p_in), halves(p_out), jax.ShapeDtypeStruct(vsmall.shape, vsmall.dtype)]
    return pl.pallas_call(
        body, name="rs_sibling",
        in_specs=[ANY] * 3, out_specs=[ANY] * 3, out_shape=outs,
        scratch_shapes=[pltpu.SemaphoreType.DMA((3,)), pltpu.SemaphoreType.DMA((3,))],
    )(p_in, p_out, vsmall)


def _rs_join_call(f_in, f_out, nw8):
    def body(fin, fout, nw, sib_in, full_out, all_nw, send, recv):
        x, y, c, me, others = _place()
        sib = (x, y, 1 - c)
        half = fout.shape[0]
        cps = [_remote(fin, sib_in, send, recv, 0, sib),
               _remote(fout, full_out.at[pl.ds(c * half, half)], send, recv, 1, sib)]
        mine = 4 * x + 2 * y + c
        peers = []
        for r in range(1, 8):
            px, py, pc = (1 - x if r & 4 else x), (1 - y if r & 2 else y), (1 - c if r & 1 else c)
            peers.append((r, (px, py, pc), 4 * px + 2 * py + pc))
            cps.append(_remote(nw, all_nw.at[mine], send, recv, 1 + r, (px, py, pc)))
        for cp in cps:
            cp.start()
        cps[0].wait_recv()
        _remote(fout, full_out.at[pl.ds((1 - c) * half, half)], send, recv, 1, sib).wait_recv()
        for r, peer, idx in peers:
            _remote(nw, all_nw.at[idx], send, recv, 1 + r, peer).wait_recv()
        for cp in cps:
            cp.wait_send()

    outs = [jax.ShapeDtypeStruct(f_in.shape, F32), jax.ShapeDtypeStruct((2 * f_out.shape[0], f_out.shape[1]), F32),
            jax.ShapeDtypeStruct((8,) + nw8.shape, F32)]
    return pl.pallas_call(
        body, name="rs_join",
        in_specs=[ANY] * 3, out_specs=[ANY] * 3, out_shape=outs,
        scratch_shapes=[pltpu.SemaphoreType.DMA((9,)), pltpu.SemaphoreType.DMA((9,))],
    )(f_in, f_out, nw8)


def _pack(arrs):
    parts = []
    for a in arrs:
        f = a.reshape(-1).astype(F32)
        pad = (-f.shape[0]) % 1024
        parts.append(jnp.pad(f, (0, pad)).reshape(-1, 128))
    return jnp.concatenate(parts, axis=0)


def _unpack(packed, shapes):
    out, row = [], 0
    for shp in shapes:
        n = 1
        for d in shp:
            n *= d
        rows = (n + 1023) // 1024 * 8
        out.append(packed[row:row + rows].reshape(-1)[:n].reshape(shp))
        row += rows
    return out


def _expand_heads(v32):
    return jnp.repeat(v32.reshape(32), HEADDIM).reshape(1, D_MODEL)


def kernel(x, norm_w, w_in, gate_b, sgu_norm_g, sgu_norm_b, sgu_w, sgu_b, conv_w, conv_b, dt_bias, A_log, D_skip, ssd_norm_w, w_out, final_norm_w, loss_target, m_norm_w, m_w_in, m_gate_b, m_sgu_norm_g, m_sgu_norm_b, m_sgu_w, m_sgu_b, m_conv_w, m_conv_b, m_dt_bias, m_A_log, m_D_skip, m_ssd_norm_w, m_w_out, m_final_norm_w, v_norm_w, v_w_in, v_gate_b, v_sgu_norm_g, v_sgu_norm_b, v_sgu_w, v_sgu_b, v_conv_w, v_conv_b, v_dt_bias, v_A_log, v_D_skip, v_ssd_norm_w, v_w_out, v_final_norm_w):
    s = x.shape[1]
    x2 = x.reshape(s, D_MODEL)
    tgt = loss_target.reshape(s, D_MODEL)
    t_ssd, t_tok, t_out, t_row = min(T_SSD, s), min(T_TOK, s), min(T_OUT, s), min(T_ROW, s)
    tm_mm, tk_dw = min(TM_MM, s), min(TK_DW, s)
    chip = 2 * lax.axis_index("x") + lax.axis_index("y")

    cw8 = jnp.pad(conv_w[0], ((0, 4), (0, 0)))
    win_b, wout_b = _c(w_in[0]), _c(w_out[0])
    g_in, g_out, g_cw = _gather_call(win_b, wout_b, cw8)
    g_in = lax.dynamic_update_index_in_dim(g_in, win_b, chip, 0)
    g_out = lax.dynamic_update_index_in_dim(g_out, wout_b, chip, 0)
    g_cw = lax.dynamic_update_index_in_dim(g_cw, cw8, chip, 0)
    wref = jnp.transpose(g_in, (1, 0, 2)).reshape(D_MODEL, IN_W)
    w_out_full = g_out.reshape(D_MODEL, D_MODEL)
    conv_w_full = jnp.transpose(g_cw[:, 0:4, :], (1, 0, 2)).reshape(4, 3072)

    w_a = jnp.concatenate([wref[:, 0:6144], wref[:, 11296:15392]], axis=1)
    fw, bw = [], []
    for g in range(SSD_GROUPS):
        xs_g = wref[:, 8192 + 512 * g:8192 + 512 * g + 512]
        b_g = wref[:, 10240 + 128 * g:10240 + 128 * g + 128]
        c_g = wref[:, 10752 + 128 * g:10752 + 128 * g + 128]
        zb_g = wref[:, 6144 + 512 * g:6144 + 512 * g + 512]
        dt_g = wref[:, 11264 + 8 * g:11264 + 8 * g + 8]
        fw += [xs_g, b_g, c_g, jnp.repeat(dt_g, HEADDIM, axis=1), zb_g]
        bw += [zb_g, xs_g, b_g, c_g, jnp.pad(dt_g, ((0, 0), (0, 120)))]
    w_b = jnp.concatenate(fw, axis=1)
    wt_a = w_a.T
    wt_b = jnp.concatenate(bw, axis=1).T

    def group_cols(full_xs, full_bc):
        parts = []
        for g in range(SSD_GROUPS):
            parts += [full_xs[:, 512 * g:512 * g + 512], full_bc[:, 128 * g:128 * g + 128], full_bc[:, 512 + 128 * g:512 + 128 * g + 128]]
        return jnp.concatenate(parts, axis=1)

    cw_g = group_cols(conv_w_full[:, 0:2048], conv_w_full[:, 2048:3072])
    cb_g = group_cols(conv_b[:, 0:2048], conv_b[:, 2048:3072])
    dtb_e, alog_e, dsk_e = _expand_heads(dt_bias), _expand_heads(A_log), _expand_heads(D_skip)

    pos_chunk = jnp.arange(SGU_BLOCK) // CHUNK
    smask = pos_chunk[None, :] <= pos_chunk[:, None]
    wm_f = jnp.where(smask[None], sgu_w[0], 0.0)
    wm = _c(wm_f)
    wmt = _c(jnp.transpose(wm_f, (0, 2, 1)))
    bias_full = jnp.repeat(sgu_b[0].T, D_MODEL // SGU_GROUPS, axis=1)
    fnw = final_norm_w.reshape(1, D_MODEL)

    xn, xnt = _norm_call(x2, norm_w, t_row)
    proj_a = _mm(xn, w_a, tm=tm_mm, tn=1024, tk=D_MODEL, name="in_proj_a", out_dtype=MXU_DTYPE)
    proj_b = _mm(xn, w_b, tm=tm_mm, tn=1024, tk=D_MODEL, name="in_proj_b")
    y_ssd, y_b, hprev, pre_all = _ssd_fwd_call(proj_b, dtb_e, alog_e, dsk_e, cw_g, cb_g, ssd_norm_w, t_ssd, NG_SSD)
    y_a, merged, merged_t = _tok_fwd_call(proj_a, y_b, gate_b, sgu_norm_g, sgu_norm_b, wm, bias_full, t_tok)
    dh, dh_b, dmerged, loss_t, dfw8 = _out_call(merged, x2, tgt, w_out_full, fnw, t_out)

    dproj_a, dy_b, dgb8, dgam8, dbeta8, dbfull, dws = _tok_bwd_call(
        proj_a, dmerged, y_a, y_b, gate_b, sgu_norm_g, sgu_norm_b, wm, wmt, bias_full, t_tok)
    dproj_b, a512, a768 = _ssd_bwd_call(proj_b, pre_all, dy_b, y_ssd, hprev, dtb_e, alog_e, dsk_e, cw_g, ssd_norm_w, t_ssd, NG_SSD)
    dw_a = _mm(xnt, dproj_a, tm=D_MODEL, tn=1024, tk=tk_dw, name="dw_in_a")
    dw_b = _mm(xnt, dproj_b, tm=D_MODEL, tn=BW_B, tk=tk_dw, name="dw_in_b")
    dw_out_p = _mm(merged_t, dh_b, tm=D_MODEL, tn=1024, tk=tk_dw, name="dw_out")

    gb = lambda a, b: jnp.concatenate([dw_b[:, BW_B * g + a:BW_B * g + b] for g in range(SSD_GROUPS)], axis=1)
    dw_ref = jnp.concatenate([dw_a[:, 0:6144], gb(0, 512), gb(512, 1024), gb(1024, 1152), gb(1152, 1280),
                              gb(1280, 1288), dw_a[:, 6144:10240]], axis=1)
    p_in = jnp.transpose(dw_ref.reshape(D_MODEL, 4, SHARD_W), (1, 0, 2))
    p_out = dw_out_p.reshape(4, D_MODEL // 4, D_MODEL)

    s512 = jnp.sum(a512, axis=2)
    heads = lambda v: jnp.sum(v.reshape(32, HEADDIM), axis=1).reshape(1, 32)
    d_ssd_nw = s512[:, 0].reshape(1, D_MODEL)
    d_dskip = heads(s512[:, 1].reshape(D_MODEL))
    d_alog = heads(s512[:, 2].reshape(D_MODEL)) * (1.0 / HEADDIM) * (-jnp.exp(A_log))
    d_dtb = heads(s512[:, 3].reshape(D_MODEL))
    s768 = jnp.sum(a768, axis=2)
    ungroup = lambda v: jnp.concatenate([v[g, :, 0:512] for g in range(4)] + [v[g, :, 512:640] for g in range(4)]
                                        + [v[g, :, 640:768] for g in range(4)], axis=1)
    d_cw = ungroup(s768[:, 0:4])
    d_cb = ungroup(s768[:, 4:5])
    d_sgu_b = jnp.sum(dbfull.reshape(128, SGU_GROUPS, 128), axis=2).T.reshape(1, SGU_GROUPS, 128)
    d_sgu_w = jnp.where(smask[None], dws, 0.0).reshape(1, SGU_GROUPS, 128, 128)
    fold = lambda a8: jnp.sum(a8, axis=0, keepdims=True)
    small_local = [fold(dgb8), fold(dgam8), fold(dbeta8), d_sgu_w, d_sgu_b, d_cw, d_cb,
                   d_dtb, d_alog, d_dskip, d_ssd_nw, fold(dfw8).reshape(D_MODEL), jnp.sum(loss_t[:, 0, 0]).reshape(1)]
    small_shapes = [a.shape for a in small_local]
    v_local = _pack(small_local)

    core = lax.axis_index("c")
    place = jnp.stack([chip, core]).astype(jnp.int32)
    hr_i, hr_o = D_MODEL // 2, D_MODEL // 8
    sib_i, sib_o, sib_v = _rs_sibling_call(p_in, p_out, v_local)
    s1b_i, o_i = _rs_add(p_in, sib_i, place, 256, "rs_add_in")
    s1b_o, o_o = _rs_add(p_out, sib_o, place, 256, "rs_add_out")
    chip_v = _add_pair(v_local, sib_v, v_local.shape[0], "ar_add_small")
    dxn, r_i, r_o, abs_v = _dx_rs_call(dproj_a, wt_a, dproj_b, wt_b, s1b_i, s1b_o, chip_v, tm=tm_mm)
    grad_x, dnw8 = _gradx_call(x2, dxn, dh, norm_w, t_row)
    abs_v = lax.dynamic_update_index_in_dim(abs_v, chip_v, chip, 0)
    f_i = _sum_own_recv(o_i, r_i, 256, "rs_sum_in")
    f_o = _sum_own_recv(o_o, r_o, 256, "rs_sum_out")
    sib_f_i, g_w_out, all_nw = _rs_join_call(f_i, f_o, dnw8)
    g_w_out = lax.dynamic_update_slice_in_dim(g_w_out, f_o, core * hr_o, axis=0)
    all_nw = lax.dynamic_update_index_in_dim(all_nw, dnw8, 2 * chip + core, 0)
    g_nw = fold(_sum_slots(all_nw, "ar_sum_norm_w"))
    total_v = _sum_slots(abs_v, "ar_sum_small")
    (g_gb, g_gam, g_beta, g_sw, g_sb, g_cw_full, g_cb, g_dtb, g_alog, g_dsk, g_snw, g_fnw, loss1) = _unpack(total_v, small_shapes)
    g_cw_shard = lax.dynamic_slice(g_cw_full, (0, chip * 768), (4, 768)).reshape(1, 4, 768)
    loss = loss1.reshape(())

    g_w_in, d_win, nm_win, nv_win = (a.T for a in _adamw_halves(w_in[0].T, f_i.T, sib_f_i.T, m_w_in[0].T, v_w_in[0].T,
                                                                place, 296, "adamw_w_in"))
    d_wout, nm_wout, nv_wout = _adamw(w_out[0], g_w_out, m_w_out[0], v_w_out[0], 128, "adamw_w_out")
    small_w = [norm_w, gate_b, sgu_norm_g, sgu_norm_b, sgu_w, sgu_b, conv_w, conv_b, dt_bias, A_log, D_skip, ssd_norm_w, final_norm_w]
    small_m = [m_norm_w, m_gate_b, m_sgu_norm_g, m_sgu_norm_b, m_sgu_w, m_sgu_b, m_conv_w, m_conv_b, m_dt_bias, m_A_log, m_D_skip, m_ssd_norm_w, m_final_norm_w]
    small_v = [v_norm_w, v_gate_b, v_sgu_norm_g, v_sgu_norm_b, v_sgu_w, v_sgu_b, v_conv_w, v_conv_b, v_dt_bias, v_A_log, v_D_skip, v_ssd_norm_w, v_final_norm_w]
    small_g = [g_nw, g_gb, g_gam, g_beta, g_sw, g_sb, g_cw_shard, g_cb, g_dtb, g_alog, g_dsk, g_snw, g_fnw]
    shapes_w = [a.shape for a in small_w]
    small_g = [a.reshape(shp) for a, shp in zip(small_g, shapes_w)]
    pw = _pack(small_w)
    pd, pm, pv = _adamw(pw, _pack(small_g), _pack(small_m), _pack(small_v), pw.shape[0], "adamw_small")
    d_small, nm_small, nv_small = _unpack(pd, shapes_w), _unpack(pm, shapes_w), _unpack(pv, shapes_w)

    def with_big(small, win, wout):
        o = list(small)
        return o[0:1] + [win.reshape(1, D_MODEL, SHARD_W)] + o[1:12] + [wout.reshape(1, D_MODEL // 4, D_MODEL)] + o[12:13]

    grads = with_big(small_g, g_w_in, g_w_out)
    deltas = with_big(d_small, d_win, d_wout)
    new_m = with_big(nm_small, nm_win, nm_wout)
    new_v = with_big(nv_small, nv_win, nv_wout)
    return (loss, grad_x.reshape(1, s, D_MODEL), *grads, *deltas, *new_m, *new_v)
```

```python
import functools

import jax
import jax.numpy as jnp
from jax import lax
from jax.experimental import pallas as pl
from jax.experimental.pallas import tpu as pltpu

F32 = jnp.float32
MXU_DTYPE = jnp.bfloat16

D_MODEL = 2048
EPS = 1e-5
CHUNK = 64
SGU_BLOCK = 128
SGU_GROUPS = 16
SSD_GROUPS = 4
SSD_GW = 512
SSD_STATE = 128
HEADDIM = 64
IN_W = 15392
SHARD_W = IN_W // 4
FW_B = 1792
BW_B = 1408
NA = 10240

ADAM_LR = 0.001
ADAM_B1 = 0.9
ADAM_B2 = 0.999
ADAM_EPS = 1e-08
ADAM_WD = 0.01
ADAM_STEP = 10

T_SSD = 256
NG_SSD = 2
T_TOK = 128
T_OUT = 256
T_ROW = 512
TM_MM = 1024
TK_DW = 1024
VMEM_CAP = 60 * 1024 * 1024
MESH = pl.DeviceIdType.MESH


def _cparams(sem, est_bytes):
    lim = int(min(VMEM_CAP, max(32 * 1024 * 1024, est_bytes + 12 * 1024 * 1024)))
    return pltpu.CompilerParams(dimension_semantics=sem, vmem_limit_bytes=lim)


def _c(x):
    return x.astype(MXU_DTYPE)


def _dot(a, b):
    return jnp.dot(a, b, preferred_element_type=F32)


def _dot_nt(a, b):
    return lax.dot_general(a, b, (((1,), (1,)), ((), ())), preferred_element_type=F32)


def _dot_tn(a, b):
    return lax.dot_general(a, b, (((0,), (0,)), ((), ())), preferred_element_type=F32)


def _split(x, n):
    parts, r = [], x
    for _ in range(n):
        p = _c(r)
        parts.append(p)
        r = r - p.astype(F32)
    return parts


def _dot01_l(m01, x, n):
    acc = None
    for p in _split(x, n):
        t = _dot(m01, p)
        acc = t if acc is None else acc + t
    return acc


def _dot01_r(x, m01, n):
    acc = None
    for p in _split(x, n):
        t = _dot(p, m01)
        acc = t if acc is None else acc + t
    return acc


def _sigmoid(x):
    return 1.0 / (1.0 + jnp.exp(-x))


def _fold8(x):
    r, w = x.shape
    return jnp.sum(x.reshape(r // 8, 8, w), axis=0)


def _iota(shape, dim):
    return lax.broadcasted_iota(jnp.int32, shape, dim)


def _ssd_masks():
    l64 = _iota((CHUNK, SSD_GW), 0)
    s64 = jnp.bitwise_and(_iota((CHUNK, SSD_GW), 1), CHUNK - 1)
    diag = l64 == s64
    causal = l64 >= s64
    row_last = l64 == CHUNK - 1
    r4 = lax.shift_right_logical(_iota((256, 256), 0), 6)
    c4 = lax.shift_right_logical(_iota((256, 256), 1), 6)
    mask4 = r4 == c4
    return diag, causal, row_last, mask4


def _cumsum_mats(t):
    r, c = _iota((t, t), 0), _iota((t, t), 1)
    same = lax.shift_right_logical(r, 6) == lax.shift_right_logical(c, 6)
    tri = _c(jnp.where(same, jnp.where(c <= r, 1.0, 0.0), 0.0))
    trit = _c(jnp.where(same, jnp.where(c >= r, 1.0, 0.0), 0.0))
    return tri, trit


def _ssd_common(xs, bm, cm, dt, acs, masks):
    diag, causal, row_last, mask4 = masks
    row_e = jnp.sum(jnp.where(diag, acs, 0.0), axis=0, keepdims=True)
    seg = acs - row_e
    lm = jnp.exp(jnp.where(causal, seg, -1e30))
    bb, cb = _c(bm), _c(cm)
    brep = jnp.concatenate([bb] * 8, axis=0)
    cbrep = _dot_nt(cb, brep)
    m = cbrep * lm
    xdt = xs * dt
    acs_last = jnp.sum(jnp.where(row_last, acs, 0.0), axis=0, keepdims=True)
    dec = jnp.exp(acs_last - acs)
    eacs = jnp.exp(acs)
    cd = jnp.exp(acs_last)
    return dict(lm=lm, bb=bb, cb=cb, brep=brep, m=m, xdt=xdt, dec=dec, eacs=eacs, cd=cd)


def _blockdiag4(xb, mask4):
    return jnp.where(mask4, jnp.concatenate([xb] * 4, axis=0), jnp.zeros((), xb.dtype))


def _ssd_chunk_fwd(xs, bm, cm, dt, acs, d_skip, ht, masks):
    q = _ssd_common(xs, bm, cm, dt, acs, masks)
    mask4 = masks[3]
    mb, xdtb = _c(q["m"]), _c(q["xdt"])
    yd = []
    for blk in range(2):
        sl = slice(256 * blk, 256 * blk + 256)
        yd.append(_dot(mb[:, sl], _blockdiag4(xdtb[:, sl], mask4)))
    y_diag = jnp.concatenate(yd, axis=1)
    p = _dot(q["cb"], _c(ht))
    y = y_diag + p * q["eacs"] + xs * d_skip
    st = _dot_tn(q["bb"], _c(q["xdt"] * q["dec"]))
    return y, ht * q["cd"] + st


def _ssd_chunk_bwd(xs, bm, cm, dt, acs, d_skip, hprev, dht, dy, masks):
    diag, causal, row_last, mask4 = masks
    q = _ssd_common(xs, bm, cm, dt, acs, masks)
    lm, bb, cb, brep, m, xdt, dec, eacs, cd = (q[k] for k in ("lm", "bb", "cb", "brep", "m", "xdt", "dec", "eacs", "cd"))
    hb = _c(hprev)
    yoff = _dot(cb, hb) * eacs
    dyb = _c(dy)
    dpb = _c(dy * eacs)
    d_c = _dot_nt(dpb, hb)
    dh_y = _dot_tn(cb, dpb)
    mb, xdtb = _c(m), _c(xdt)
    dm_parts, dxdt_parts = [], []
    for blk in range(2):
        sl = slice(256 * blk, 256 * blk + 256)
        bd = _blockdiag4(xdtb[:, sl], mask4)
        dm_parts.append(_dot_nt(dyb[:, sl], bd))
        dxf = jnp.where(mask4, _dot_tn(mb[:, sl], dyb[:, sl]), 0.0)
        dxdt_parts.append(dxf[0:64] + dxf[64:128] + dxf[128:192] + dxf[192:256])
    dm = jnp.concatenate(dm_parts, axis=1)
    dxdt = jnp.concatenate(dxdt_parts, axis=1)
    dcbb = _c(dm * lm)
    g = dm * m
    d_c = d_c + _dot(dcbb, brep)
    dbrep = _dot_tn(dcbb, cb)
    d_b = dbrep[0:64]
    for r in range(1, 8):
        d_b = d_b + dbrep[64 * r:64 * r + 64]
    dhtb = _c(dht)
    dxd = _dot(bb, dhtb)
    xd = xdt * dec
    dxdt = dxdt + dxd * dec
    tq = dxd * xd
    d_b = d_b + _dot_nt(_c(xd), dhtb)
    dcd = jnp.sum(dht * hprev, axis=0, keepdims=True)
    col_g = jnp.sum(g, axis=0, keepdims=True)
    last = jnp.sum(tq, axis=0, keepdims=True) + dcd * cd
    qq = g - jnp.where(diag, col_g, 0.0) + dy * yoff - tq + jnp.where(row_last, last, 0.0)
    dxs = dxdt * dt + dy * d_skip
    return dxs, d_b, d_c, dht * cd + dh_y, dy * xs, qq, dxdt * xs


def _ssd_finish_dt(qq, p1, dt, sig, a_neg, trit, mask4):
    bd4 = _c(jnp.where(mask4, 1.0, 0.0))
    dacs = jnp.concatenate([_dot01_r(qq[:, 256 * b:256 * b + 256], bd4, 2) for b in range(2)], axis=1)
    da = _dot01_l(trit, dacs, 2)
    ddt = p1 + da * (a_neg * (1.0 / HEADDIM))
    return ddt * sig, da * dt


def _softplus(x):
    return jnp.maximum(x, 0.0) + jnp.log1p(jnp.exp(-jnp.abs(x)))


def _conv_taps(xpad, t):
    taps = []
    for k in range(4):
        sh = 3 - k
        v = xpad if sh == 0 else pltpu.roll(xpad, sh, 0)
        taps.append(v[8:8 + t])
    return taps


def _norm_call(x, norm_w, tm):
    s = x.shape[0]

    def body(x_ref, w_ref, xn_ref, xnt_ref):
        xv = x_ref[...]
        r = lax.rsqrt(jnp.mean(xv * xv, axis=-1, keepdims=True) + EPS)
        xn = xv * r * w_ref[...]
        xn_ref[...] = _c(xn)
        xnt_ref[...] = _c(xn.T)

    return pl.pallas_call(
        body, name="rmsnorm_in",
        grid=(s // tm,),
        in_specs=[pl.BlockSpec((tm, D_MODEL), lambda i: (i, 0)), pl.BlockSpec((1, D_MODEL), lambda i: (0, 0))],
        out_specs=[pl.BlockSpec((tm, D_MODEL), lambda i: (i, 0)), pl.BlockSpec((D_MODEL, tm), lambda i: (0, i))],
        out_shape=[jax.ShapeDtypeStruct((s, D_MODEL), MXU_DTYPE), jax.ShapeDtypeStruct((D_MODEL, s), MXU_DTYPE)],
        compiler_params=_cparams(("parallel",), 2 * tm * D_MODEL * 12),
    )(x, norm_w)


def _mm(a, b, *, tm, tn, tk, name, out_dtype=F32):
    m, k = a.shape
    n = b.shape[1]
    nk = k // tk
    assert m % tm == 0 and n % tn == 0 and k % tk == 0, (a.shape, b.shape, tm, tn, tk)
    assert nk == 1 or out_dtype == F32

    def body(a_ref, b_ref, o_ref):
        if nk == 1:
            o_ref[...] = _dot(a_ref[...], b_ref[...]).astype(out_dtype)
        else:
            @pl.when(pl.program_id(2) == 0)
            def _():
                o_ref[...] = jnp.zeros_like(o_ref)

            o_ref[...] += _dot(a_ref[...], b_ref[...])

    isz = jnp.dtype(a.dtype).itemsize
    est = 2 * (tm * tk + tk * tn) * isz + 2 * tm * tn * 4
    return pl.pallas_call(
        body, name=name,
        grid=(m // tm, n // tn, nk),
        in_specs=[pl.BlockSpec((tm, tk), lambda i, j, kk: (i, kk)), pl.BlockSpec((tk, tn), lambda i, j, kk: (kk, j))],
        out_specs=pl.BlockSpec((tm, tn), lambda i, j, kk: (i, j)),
        out_shape=jax.ShapeDtypeStruct((m, n), out_dtype),
        compiler_params=_cparams(("parallel", "parallel", "arbitrary"), est),
    )(a, b)


def _dx_rs_call(dpa, wta, dpb, wtb, sb_in, sb_out, chip_v, *, tm):
    s = dpa.shape[0]
    tka, tkb = 1024, BW_B
    nka, nkb = dpa.shape[1] // tka, dpb.shape[1] // tkb
    ni, nk = s // tm, nka + nkb

    def body(a_ref, wa_ref, b_ref, wb_ref, sbin, sbout, cv, o_ref, rc_in, rc_out, abs_v, send, recv):
        i, kk = pl.program_id(0), pl.program_id(1)

        def copies():
            x, y, c, me, others = _place()
            sends, recvs = [], []
            for j, chip in enumerate(others):
                kj = 2 * chip[0] + chip[1]
                to = (*chip, c)
                sends += [_remote(sbin.at[kj], rc_in.at[j], send, recv, j, to),
                          _remote(sbout.at[kj], rc_out.at[j], send, recv, 3 + j, to),
                          _remote(cv, abs_v.at[me], send, recv, 6 + j, to)]
                recvs += [sends[-3], sends[-2], _remote(cv, abs_v.at[kj], send, recv, 6 + j, to)]
            return sends, recvs

        @pl.when((i == 0) & (kk == 0))
        def _():
            for cp in copies()[0]:
                cp.start()

        @pl.when(kk == 0)
        def _():
            o_ref[...] = jnp.zeros_like(o_ref)

        @pl.when(kk < nka)
        def _():
            o_ref[...] += _dot(a_ref[...], wa_ref[...])

        @pl.when(kk >= nka)
        def _():
            o_ref[...] += _dot(b_ref[...], wb_ref[...])

        @pl.when((i == ni - 1) & (kk == nk - 1))
        def _():
            sends, recvs = copies()
            for cp in recvs:
                cp.wait_recv()
            for cp in sends:
                cp.wait_send()

    isz = jnp.dtype(dpa.dtype).itemsize
    est = 2 * isz * (tm * tka + tka * D_MODEL + tm * tkb + tkb * D_MODEL) + 2 * tm * D_MODEL * 4
    outs = [jax.ShapeDtypeStruct((s, D_MODEL), F32),
            jax.ShapeDtypeStruct((3,) + sb_in.shape[1:], sb_in.dtype), jax.ShapeDtypeStruct((3,) + sb_out.shape[1:], sb_out.dtype),
            jax.ShapeDtypeStruct((4,) + chip_v.shape, F32)]
    return pl.pallas_call(
        body, name="dx_matmul_rs_chips",
        grid=(ni, nk),
        in_specs=[
            pl.BlockSpec((tm, tka), lambda i, kk: (i, jnp.minimum(kk, nka - 1))),
            pl.BlockSpec((tka, D_MODEL), lambda i, kk: (jnp.minimum(kk, nka - 1), 0)),
            pl.BlockSpec((tm, tkb), lambda i, kk: (i, jnp.maximum(kk - nka, 0))),
            pl.BlockSpec((tkb, D_MODEL), lambda i, kk: (jnp.maximum(kk - nka, 0), 0)),
            ANY, ANY, ANY,
        ],
        out_specs=[pl.BlockSpec((tm, D_MODEL), lambda i, kk: (i, 0)), ANY, ANY, ANY],
        out_shape=outs,
        scratch_shapes=[pltpu.SemaphoreType.DMA((9,)), pltpu.SemaphoreType.DMA((9,))],
        compiler_params=_cparams(("arbitrary", "arbitrary"), est),
    )(dpa, wta, dpb, wtb, sb_in, sb_out, chip_v)


def _gradx_call(x, dxn, dh, norm_w, tm):
    s = x.shape[0]

    def body(x_ref, g_ref, dh_ref, w_ref, gx_ref, dw_ref):
        @pl.when(pl.program_id(0) == 0)
        def _():
            dw_ref[...] = jnp.zeros_like(dw_ref)

        xv, gv = x_ref[...], g_ref[...]
        r = lax.rsqrt(jnp.mean(xv * xv, axis=-1, keepdims=True) + EPS)
        gw = gv * w_ref[...]
        gx_ref[...] = r * gw - xv * (r * r * r) * jnp.mean(xv * gw, axis=-1, keepdims=True) + dh_ref[...]
        dw_ref[...] += _fold8(gv * (xv * r))

    row = pl.BlockSpec((tm, D_MODEL), lambda i: (i, 0))
    return pl.pallas_call(
        body, name="grad_x",
        grid=(s // tm,),
        in_specs=[row, row, row, pl.BlockSpec((1, D_MODEL), lambda i: (0, 0))],
        out_specs=[row, pl.BlockSpec((8, D_MODEL), lambda i: (0, 0))],
        out_shape=[jax.ShapeDtypeStruct((s, D_MODEL), F32), jax.ShapeDtypeStruct((8, D_MODEL), F32)],
        compiler_params=_cparams(("arbitrary",), 2 * tm * D_MODEL * 16),
    )(x, dxn, dh, norm_w)


def _layernorm_stats(v):
    mu = jnp.mean(v, axis=-1, keepdims=True)
    vc = v - mu
    var = jnp.mean(vc * vc, axis=-1, keepdims=True)
    return vc * lax.rsqrt(var + EPS), lax.rsqrt(var + EPS)


def _tok_fwd_call(proj_a, y_b, gate_b, sgu_g, sgu_beta, wm, bias_full, t):
    s = proj_a.shape[0]

    def body(pa_ref, yb_ref, gb_ref, g_ref, be_ref, wm_ref, bf_ref, ya_ref, mg_ref, mgt_ref, mix_ref):
        u = pa_ref[:, 0:2048].astype(F32)
        v = pa_ref[:, 2048:4096].astype(F32)
        za = pa_ref[:, 4096:6144].astype(F32)
        xhat, _ = _layernorm_stats(v)
        vnb = _c(xhat * g_ref[...] + be_ref[...])
        for gi in range(SGU_GROUPS):
            sl = slice(128 * gi, 128 * gi + 128)
            mix_ref[:, sl] = _dot(wm_ref[gi], vnb[:, sl])
        mixed = mix_ref[...] + bf_ref[...]
        y_a = u * mixed * (za * _sigmoid(za))
        g0 = _sigmoid(pa_ref[:, 6144:8192].astype(F32) + gb_ref[:, 0:2048])
        g1 = _sigmoid(pa_ref[:, 8192:10240].astype(F32) + gb_ref[:, 2048:4096])
        merged = g0 * y_a + g1 * yb_ref[...].astype(F32)
        ya_ref[...] = _c(y_a)
        mg_ref[...] = _c(merged)
        mgt_ref[...] = _c(merged.T)

    row = pl.BlockSpec((t, D_MODEL), lambda i: (i, 0))
    vec = lambda w: pl.BlockSpec((1, w), lambda i: (0, 0))
    return pl.pallas_call(
        body, name="tok_fwd",
        grid=(s // t,),
        in_specs=[pl.BlockSpec((t, NA), lambda i: (i, 0)), row, vec(4096), vec(2048), vec(2048),
                  pl.BlockSpec((SGU_GROUPS, 128, 128), lambda i: (0, 0, 0)), pl.BlockSpec((128, D_MODEL), lambda i: (0, 0))],
        out_specs=[row, row, pl.BlockSpec((D_MODEL, t), lambda i: (0, i))],
        out_shape=[jax.ShapeDtypeStruct((s, D_MODEL), MXU_DTYPE), jax.ShapeDtypeStruct((s, D_MODEL), MXU_DTYPE),
                   jax.ShapeDtypeStruct((D_MODEL, s), MXU_DTYPE)],
        scratch_shapes=[pltpu.VMEM((t, D_MODEL), F32)],
        compiler_params=_cparams(("parallel",), 2 * t * NA * 4 + 12 * t * D_MODEL * 4),
    )(proj_a, y_b, gate_b, sgu_g, sgu_beta, wm, bias_full)


def _tok_bwd_call(proj_a, dmerged, y_a, y_b, gate_b, sgu_g, sgu_beta, wm, wmt, bias_full, t):
    s = proj_a.shape[0]

    def body(pa_ref, dm_ref, ya_ref, yb_ref, gb_ref, g_ref, be_ref, wm_ref, wmt_ref, bf_ref,
             dpa_ref, dyb_ref, dgb_ref, dgam_ref, dbeta_ref, dbf_ref, dws_ref, mix_ref, dvn_ref):
        @pl.when(pl.program_id(0) == 0)
        def _():
            dgb_ref[...] = jnp.zeros_like(dgb_ref)
            dgam_ref[...] = jnp.zeros_like(dgam_ref)
            dbeta_ref[...] = jnp.zeros_like(dbeta_ref)
            dbf_ref[...] = jnp.zeros_like(dbf_ref)
            dws_ref[...] = jnp.zeros_like(dws_ref)

        u = pa_ref[:, 0:2048].astype(F32)
        v = pa_ref[:, 2048:4096].astype(F32)
        za = pa_ref[:, 4096:6144].astype(F32)
        xhat, rstd = _layernorm_stats(v)
        vnb = _c(xhat * g_ref[...] + be_ref[...])
        for gi in range(SGU_GROUPS):
            sl = slice(128 * gi, 128 * gi + 128)
            mix_ref[:, sl] = _dot(wm_ref[gi], vnb[:, sl])
        mixed = mix_ref[...] + bf_ref[...]
        sig = _sigmoid(za)
        sz = za * sig
        dm = dm_ref[...].astype(F32)
        y_a = ya_ref[...].astype(F32)
        g0 = _sigmoid(pa_ref[:, 6144:8192].astype(F32) + gb_ref[:, 0:2048])
        g1 = _sigmoid(pa_ref[:, 8192:10240].astype(F32) + gb_ref[:, 2048:4096])
        dgl0 = dm * y_a * g0 * (1.0 - g0)
        dgl1 = dm * yb_ref[...].astype(F32) * g1 * (1.0 - g1)
        dyb_ref[...] = _c(dm * g1)
        dya = dm * g0
        dpa_ref[:, 6144:8192] = _c(dgl0)
        dpa_ref[:, 8192:10240] = _c(dgl1)
        dgb_ref[:, 0:2048] += _fold8(dgl0)
        dgb_ref[:, 2048:4096] += _fold8(dgl1)
        dpa_ref[:, 0:2048] = _c(dya * mixed * sz)
        dpa_ref[:, 4096:6144] = _c(dya * (u * mixed) * (sig * (1.0 + za * (1.0 - sig))))
        dmixed = dya * u * sz
        dbf_ref[...] += dmixed
        dmb = _c(dmixed)
        for gi in range(SGU_GROUPS):
            sl = slice(128 * gi, 128 * gi + 128)
            dvn_ref[:, sl] = _dot(wmt_ref[gi], dmb[:, sl])
            dws_ref[gi] += _dot_nt(dmb[:, sl], vnb[:, sl])
        dvn = dvn_ref[...]
        dgam_ref[...] += _fold8(dvn * xhat)
        dbeta_ref[...] += _fold8(dvn)
        dxh = dvn * g_ref[...]
        dv = rstd * (dxh - jnp.mean(dxh, axis=-1, keepdims=True) - xhat * jnp.mean(dxh * xhat, axis=-1, keepdims=True))
        dpa_ref[:, 2048:4096] = _c(dv)

    row = pl.BlockSpec((t, D_MODEL), lambda i: (i, 0))
    vec = lambda w: pl.BlockSpec((1, w), lambda i: (0, 0))
    acc = lambda w: pl.BlockSpec((8, w), lambda i: (0, 0))
    wspec = pl.BlockSpec((SGU_GROUPS, 128, 128), lambda i: (0, 0, 0))
    return pl.pallas_call(
        body, name="tok_bwd",
        grid=(s // t,),
        in_specs=[pl.BlockSpec((t, NA), lambda i: (i, 0)), row, row, row, vec(4096), vec(2048), vec(2048),
                  wspec, wspec, pl.BlockSpec((128, D_MODEL), lambda i: (0, 0))],
        out_specs=[pl.BlockSpec((t, NA), lambda i: (i, 0)), row, acc(4096), acc(2048), acc(2048),
                   pl.BlockSpec((128, D_MODEL), lambda i: (0, 0)), wspec],
        out_shape=[jax.ShapeDtypeStruct((s, NA), MXU_DTYPE), jax.ShapeDtypeStruct((s, D_MODEL), MXU_DTYPE),
                   jax.ShapeDtypeStruct((8, 4096), F32), jax.ShapeDtypeStruct((8, 2048), F32),
                   jax.ShapeDtypeStruct((8, 2048), F32), jax.ShapeDtypeStruct((128, D_MODEL), F32),
                   jax.ShapeDtypeStruct((SGU_GROUPS, 128, 128), F32)],
        scratch_shapes=[pltpu.VMEM((t, D_MODEL), F32), pltpu.VMEM((t, D_MODEL), F32)],
        compiler_params=_cparams(("arbitrary",), 2 * t * NA * 6 + 16 * t * D_MODEL * 4),
    )(proj_a, dmerged, y_a, y_b, gate_b, sgu_g, sgu_beta, wm, wmt, bias_full)


def _out_call(merged, x, target, w_out, fnw, t):
    s = x.shape[0]
    nt = s // t

    def body(mg_ref, x_ref, t_ref, w_ref, fw_ref, dh_ref, dhb_ref, dmg_ref, loss_ref, dfw_ref):
        @pl.when(pl.program_id(0) == 0)
        def _():
            dfw_ref[...] = jnp.zeros_like(dfw_ref)

        h = x_ref[...] + _dot(mg_ref[...], w_ref[...])
        r = lax.rsqrt(jnp.mean(h * h, axis=-1, keepdims=True) + EPS)
        hn = h * r
        err = hn * fw_ref[...] - t_ref[...]
        loss_ref[...] = jnp.full(loss_ref.shape, 0.5 * jnp.sum(jnp.mean(err * err, axis=-1, keepdims=True)), F32)
        dy = err * (1.0 / D_MODEL)
        dfw_ref[...] += _fold8(dy * hn)
        gw = dy * fw_ref[...]
        dh = r * gw - h * (r * r * r) * jnp.mean(h * gw, axis=-1, keepdims=True)
        dh_ref[...] = dh
        dhb = _c(dh)
        dhb_ref[...] = dhb
        dmg_ref[...] = _c(_dot_nt(dhb, w_ref[...]))

    row = pl.BlockSpec((t, D_MODEL), lambda i: (i, 0))
    return pl.pallas_call(
        body, name="out_proj_loss",
        grid=(nt,),
        in_specs=[row, row, row, pl.BlockSpec((D_MODEL, D_MODEL), lambda i: (0, 0)), pl.BlockSpec((1, D_MODEL), lambda i: (0, 0))],
        out_specs=[row, row, row, pl.BlockSpec((1, 8, 128), lambda i: (i, 0, 0)), pl.BlockSpec((8, D_MODEL), lambda i: (0, 0))],
        out_shape=[jax.ShapeDtypeStruct((s, D_MODEL), F32), jax.ShapeDtypeStruct((s, D_MODEL), MXU_DTYPE),
                   jax.ShapeDtypeStruct((s, D_MODEL), MXU_DTYPE), jax.ShapeDtypeStruct((nt, 8, 128), F32),
                   jax.ShapeDtypeStruct((8, D_MODEL), F32)],
        compiler_params=_cparams(("arbitrary",), 2 * D_MODEL * D_MODEL * 2 + 2 * t * D_MODEL * 24),
    )(merged, x, target, w_out, fnw)


def _ssd_fwd_call(proj_b, dtb, alog, dsk, cw, cb, nw, t, ng):
    s = proj_b.shape[0]
    nt, nch = s // t, t // CHUNK

    def body(pb_ref, halo_ref, dtb_ref, al_ref, ds_ref, cw_ref, cb_ref, nw_ref, y_ref, yb_ref, hp_ref, pre_ref,
             dt_ref, acs_ref, ht_ref, act_ref):
        i = pl.program_id(1)

        @pl.when(i == 0)
        def _():
            ht_ref[...] = jnp.zeros_like(ht_ref)

        for gi in range(ng):
            fo, co = FW_B * gi, 768 * gi
            halo = jnp.where(i == 0, 0.0, halo_ref[:, fo:fo + 768])
            taps = _conv_taps(jnp.concatenate([halo, pb_ref[:, fo:fo + 768]], axis=0), t)
            pre = cb_ref[:, co:co + 768]
            for k in range(4):
                pre = pre + taps[k] * cw_ref[k:k + 1, co:co + 768]
            pre_ref[:, co:co + 768] = pre
            act_ref[:, co:co + 768] = pre * _sigmoid(pre)
        masks = _ssd_masks()
        tri, _ = _cumsum_mats(CHUNK)
        a_neg = -jnp.exp(al_ref[...])

        def chunk(c, carry):
            rows = pl.ds(pl.multiple_of(c * CHUNK, CHUNK), CHUNK)
            for gi in range(ng):
                fo, co, go = FW_B * gi, 768 * gi, SSD_GW * gi
                dt = _softplus(pb_ref[rows, fo + 768:fo + 1280] + dtb_ref[:, go:go + 512])
                acs = _dot01_l(tri, dt * a_neg[:, go:go + 512], 3)
                dt_ref[rows, go:go + 512] = dt
                acs_ref[rows, go:go + 512] = acs
                ht = ht_ref[gi]
                hp_ref[c, :, go:go + 512] = ht
                y, ht_new = _ssd_chunk_fwd(act_ref[rows, co:co + 512], act_ref[rows, co + 512:co + 640],
                                           act_ref[rows, co + 640:co + 768], dt, acs, ds_ref[:, go:go + 512], ht, masks)
                y_ref[rows, go:go + 512] = y
                ht_ref[gi] = ht_new
            return carry

        lax.fori_loop(0, nch, chunk, 0)
        for gi in range(ng):
            fo, go = FW_B * gi, SSD_GW * gi
            zb = pb_ref[:, fo + 1280:fo + 1792]
            hh = y_ref[:, go:go + 512] * (zb * _sigmoid(zb))
            rr = lax.rsqrt(jnp.mean(hh * hh, axis=-1, keepdims=True) + EPS)
            yb_ref[:, go:go + 512] = _c(hh * rr * nw_ref[:, go:go + 512])

    gvec = lambda w: pl.BlockSpec((1, ng * w), lambda g, i: (0, g))
    return pl.pallas_call(
        body, name="ssd_fwd",
        grid=(SSD_GROUPS // ng, nt),
        in_specs=[pl.BlockSpec((t, ng * FW_B), lambda g, i: (i, g)),
                  pl.BlockSpec((8, ng * FW_B), lambda g, i: (jnp.maximum(i * (t // 8) - 1, 0), g)),
                  gvec(512), gvec(512), gvec(512),
                  pl.BlockSpec((4, ng * 768), lambda g, i: (0, g)), gvec(768), gvec(512)],
        out_specs=[pl.BlockSpec((t, ng * SSD_GW), lambda g, i: (i, g)), pl.BlockSpec((t, ng * SSD_GW), lambda g, i: (i, g)),
                   pl.BlockSpec((nch, SSD_STATE, ng * SSD_GW), lambda g, i: (i, 0, g)),
                   pl.BlockSpec((t, ng * 768), lambda g, i: (i, g)),
                   pl.BlockSpec((t, ng * SSD_GW), lambda g, i: (i, g)), pl.BlockSpec((t, ng * SSD_GW), lambda g, i: (i, g))],
        out_shape=[jax.ShapeDtypeStruct((s, D_MODEL), F32), jax.ShapeDtypeStruct((s, D_MODEL), MXU_DTYPE),
                   jax.ShapeDtypeStruct((s // CHUNK, SSD_STATE, D_MODEL), F32),
                   jax.ShapeDtypeStruct((s, SSD_GROUPS * 768), F32),
                   jax.ShapeDtypeStruct((s, D_MODEL), F32), jax.ShapeDtypeStruct((s, D_MODEL), F32)],
        scratch_shapes=[pltpu.VMEM((ng, SSD_STATE, SSD_GW), F32), pltpu.VMEM((t, ng * 768), F32)],
        compiler_params=_cparams(("parallel", "arbitrary"), ng * (2 * t * FW_B * 4 + 16 * t * SSD_GW * 4) + 24 * 1024 * 1024),
    )(proj_b, proj_b, dtb, alog, dsk, cw, cb, nw)


def _ssd_bwd_call(proj_b, pre_all, dt_all, acs_all, dyb, y, hprev, dtb, alog, dsk, cw, nw, t, ng):
    s = proj_b.shape[0]
    nt, nch = s // t, t // CHUNK

    def body(pb_ref, pre_ref, dt_ref, acs_ref, dyb_ref, y_ref, hp_ref, dtb_ref, al_ref, ds_ref, cw_ref, nw_ref,
             dpb_ref, a512_ref, a768_ref, dht_ref, act_ref, dsl_ref, dact_ref, dy_ref, nxt_ref, q_ref, p1_ref):
        i = pl.program_id(1)

        @pl.when(i == 0)
        def _():
            dht_ref[...] = jnp.zeros_like(dht_ref)
            nxt_ref[...] = jnp.zeros_like(nxt_ref)
            a512_ref[...] = jnp.zeros_like(a512_ref)
            a768_ref[...] = jnp.zeros_like(a768_ref)

        _, trit = _cumsum_mats(t)
        a_neg = -jnp.exp(al_ref[...])
        for gi in range(ng):
            fo, co, go, bo = FW_B * gi, 768 * gi, SSD_GW * gi, BW_B * gi
            pre = pre_ref[:, co:co + 768]
            sp = _sigmoid(pre)
            act_ref[:, co:co + 768] = pre * sp
            dsl_ref[:, co:co + 768] = sp * (1.0 + pre * (1.0 - sp))
            zb = pb_ref[:, fo + 1280:fo + 1792]
            yv = y_ref[:, go:go + 512]
            sgz = _sigmoid(zb)
            sz = zb * sgz
            hh = yv * sz
            rr = lax.rsqrt(jnp.mean(hh * hh, axis=-1, keepdims=True) + EPS)
            dyb = dyb_ref[:, go:go + 512].astype(F32)
            a512_ref[gi, 0] += _fold8(dyb * (hh * rr))
            tt = dyb * nw_ref[:, go:go + 512]
            dhh = rr * tt - hh * (rr * rr * rr) * jnp.mean(hh * tt, axis=-1, keepdims=True)
            dy_ref[:, go:go + 512] = dhh * sz
            dpb_ref[:, bo:bo + 512] = _c(dhh * yv * (sgz * (1.0 + zb * (1.0 - sgz))))

        masks = _ssd_masks()

        def chunk(cc, carry):
            c = nch - 1 - cc
            rows = pl.ds(pl.multiple_of(c * CHUNK, CHUNK), CHUNK)
            for gi in range(ng):
                co, go = 768 * gi, SSD_GW * gi
                dxs, d_b, d_c, dht_prev, dyxs, qq, p1 = _ssd_chunk_bwd(
                    act_ref[rows, co:co + 512], act_ref[rows, co + 512:co + 640], act_ref[rows, co + 640:co + 768],
                    dt_ref[rows, go:go + 512], acs_ref[rows, go:go + 512], ds_ref[:, go:go + 512],
                    hp_ref[c, :, go:go + 512], dht_ref[gi], dy_ref[rows, go:go + 512], masks)
                dht_ref[gi] = dht_prev
                dact_ref[rows, co:co + 512] = dxs
                dact_ref[rows, co + 512:co + 640] = d_b
                dact_ref[rows, co + 640:co + 768] = d_c
                q_ref[rows, go:go + 512] = qq
                p1_ref[rows, go:go + 512] = p1
                a512_ref[gi, 1] += _fold8(dyxs)
            return carry

        lax.fori_loop(0, nch, chunk, 0)

        rsel = _c(jnp.where(lax.shift_right_logical(_iota((SSD_GW, 128), 0), 6) == _iota((SSD_GW, 128), 1), 1.0, 0.0))
        for gi in range(ng):
            fo, go, bo = FW_B * gi, SSD_GW * gi, BW_B * gi
            sig = _sigmoid(pb_ref[:, fo + 768:fo + 1280] + dtb_ref[:, go:go + 512])
            ddtr, dadt = _ssd_finish_dt(q_ref[:, go:go + 512], p1_ref[:, go:go + 512], dt_ref[:, go:go + 512], sig,
                                        a_neg[:, go:go + 512], trit, masks[3])
            dpb_ref[:, bo + 1280:bo + 1408] = _c(_dot01_r(ddtr, rsel, 2))
            a512_ref[gi, 2] += _fold8(dadt)
            a512_ref[gi, 3] += _fold8(ddtr)

        for gi in range(ng):
            fo, co, bo = FW_B * gi, 768 * gi, BW_B * gi
            dpre = dact_ref[:, co:co + 768] * dsl_ref[:, co:co + 768]
            xbc = pb_ref[:, fo:fo + 768]
            a768_ref[gi, 4] += _fold8(dpre)
            a768_ref[gi, 3] += _fold8(dpre * xbc)
            dpad = jnp.concatenate([dpre, nxt_ref[:, co:co + 768]], axis=0)
            dx = dpre * cw_ref[3:4, co:co + 768]
            for k in range(3):
                d_k = pltpu.roll(dpad, t + 8 - (3 - k), 0)[0:t]
                dx = dx + d_k * cw_ref[k:k + 1, co:co + 768]
                a768_ref[gi, k] += _fold8(d_k * xbc)
            nxt_ref[:, co:co + 768] = dpre[0:8]
            dpb_ref[:, bo + 512:bo + 1280] = _c(dx)

    gvec = lambda w: pl.BlockSpec((1, ng * w), lambda g, i: (0, g))
    rev = lambda w: pl.BlockSpec((t, ng * w), lambda g, i: (nt - 1 - i, g))
    return pl.pallas_call(
        body, name="ssd_bwd",
        grid=(SSD_GROUPS // ng, nt),
        in_specs=[rev(FW_B), rev(768), rev(SSD_GW), rev(SSD_GW), rev(SSD_GW), rev(SSD_GW),
                  pl.BlockSpec((nch, SSD_STATE, ng * SSD_GW), lambda g, i: (nt - 1 - i, 0, g)),
                  gvec(512), gvec(512), gvec(512),
                  pl.BlockSpec((4, ng * 768), lambda g, i: (0, g)), gvec(512)],
        out_specs=[rev(BW_B),
                   pl.BlockSpec((ng, 4, 8, 512), lambda g, i: (g, 0, 0, 0)),
                   pl.BlockSpec((ng, 5, 8, 768), lambda g, i: (g, 0, 0, 0))],
        out_shape=[jax.ShapeDtypeStruct((s, SSD_GROUPS * BW_B), MXU_DTYPE),
                   jax.ShapeDtypeStruct((SSD_GROUPS, 4, 8, 512), F32),
                   jax.ShapeDtypeStruct((SSD_GROUPS, 5, 8, 768), F32)],
        scratch_shapes=[pltpu.VMEM((ng, SSD_STATE, SSD_GW), F32), pltpu.VMEM((t, ng * 768), F32), pltpu.VMEM((t, ng * 768), F32),
                        pltpu.VMEM((t, ng * 768), F32), pltpu.VMEM((t, ng * SSD_GW), F32), pltpu.VMEM((8, ng * 768), F32)]
        + [pltpu.VMEM((t, ng * SSD_GW), F32)] * 2,
        compiler_params=_cparams(("parallel", "arbitrary"), ng * (2 * t * FW_B * 4 + 22 * t * SSD_GW * 4) + 24 * 1024 * 1024),
    )(proj_b, pre_all, dt_all, acs_all, dyb, y, hprev, dtb, alog, dsk, cw, nw)


def _rows_call(body, ins, outs, tr, name):
    r = ins[0].shape[0]
    spec = lambda a: pl.BlockSpec((tr, a.shape[1]), lambda i: (i, 0))
    est = 2 * tr * sum(a.shape[1] * jnp.dtype(a.dtype).itemsize for a in list(ins) + list(outs))
    return pl.pallas_call(
        body, name=name, grid=(r // tr,),
        in_specs=[spec(a) for a in ins], out_specs=[spec(o) for o in outs], out_shape=list(outs),
        compiler_params=_cparams(("parallel",), est),
    )(*ins)


def _add_pair(a, b, tr, name):
    def body(a_ref, b_ref, o_ref):
        o_ref[...] = a_ref[...] + b_ref[...]

    return _rows_call(body, [a, b], [jax.ShapeDtypeStruct(a.shape, F32)], tr, name)[0]


def _rs_add(p, sib, place, tr, name):
    _, r, c = p.shape
    half = r // 2
    nb = half // tr

    def body(pl_ref, p_ref, s_ref, b_ref, own_ref):
        v = p_ref[0] + s_ref[0]
        b_ref[0] = v.astype(jnp.bfloat16)

        @pl.when(pl.program_id(1) == pl_ref[0])
        def _():
            own_ref[...] = v

    return pl.pallas_call(
        body, name=name,
        grid_spec=pltpu.PrefetchScalarGridSpec(
            num_scalar_prefetch=1, grid=(nb, 4),
            in_specs=[pl.BlockSpec((1, tr, c), lambda i, k, pr: (k, pr[1] * nb + i, 0)),
                      pl.BlockSpec((1, tr, c), lambda i, k, pr: (k, i, 0))],
            out_specs=[pl.BlockSpec((1, tr, c), lambda i, k, pr: (k, i, 0)),
                       pl.BlockSpec((tr, c), lambda i, k, pr: (i, 0))]),
        out_shape=[jax.ShapeDtypeStruct((4, half, c), jnp.bfloat16), jax.ShapeDtypeStruct((half, c), F32)],
        compiler_params=_cparams(("parallel", "arbitrary"), 2 * tr * c * 14),
    )(place, p, sib)


def _sum_own_recv(own, recv, tr, name):
    r, c = own.shape

    def body(o_ref, r_ref, out_ref):
        v = o_ref[...]
        for j in range(3):
            v = v + r_ref[j].astype(F32)
        out_ref[...] = v

    return pl.pallas_call(
        body, name=name, grid=(r // tr,),
        in_specs=[pl.BlockSpec((tr, c), lambda i: (i, 0)), pl.BlockSpec((3, tr, c), lambda i: (0, i, 0))],
        out_specs=pl.BlockSpec((tr, c), lambda i: (i, 0)),
        out_shape=jax.ShapeDtypeStruct((r, c), F32),
        compiler_params=_cparams(("parallel",), 2 * tr * c * 14),
    )(own, recv)


def _sum_slots(stack, name):
    n, r, w = stack.shape

    def body(a_ref, out_ref):
        v = a_ref[0]
        for k in range(1, n):
            v = v + a_ref[k]
        out_ref[...] = v

    return pl.pallas_call(
        body, name=name, grid=(1,),
        in_specs=[pl.BlockSpec((n, r, w), lambda i: (0, 0, 0))],
        out_specs=pl.BlockSpec((r, w), lambda i: (0, 0)),
        out_shape=jax.ShapeDtypeStruct((r, w), F32),
        compiler_params=_cparams(("arbitrary",), 2 * (n + 1) * r * w * 4),
    )(stack)


def _adamw(w, g, m, v, tr, name):
    def body(w_ref, g_ref, m_ref, v_ref, d_ref, nm_ref, nv_ref):
        d_ref[...], nm_ref[...], nv_ref[...] = _adam_math(w_ref[...], g_ref[...], m_ref[...], v_ref[...])

    o = jax.ShapeDtypeStruct(w.shape, F32)
    return _rows_call(body, [w, g, m, v], [o, o, o], tr, name)


def _adam_math(w, g, m, v):
    nm = ADAM_B1 * m + (1.0 - ADAM_B1) * g
    nv = ADAM_B2 * v + (1.0 - ADAM_B2) * (g * g)
    m_hat = nm / (1.0 - ADAM_B1 ** ADAM_STEP)
    v_hat = nv / (1.0 - ADAM_B2 ** ADAM_STEP)
    return -ADAM_LR * (m_hat / (jnp.sqrt(v_hat) + ADAM_EPS) + ADAM_WD * w), nm, nv


def _adamw_halves(w, g_own, g_sib, m, v, place, tr, name):
    r, c = w.shape

    def body(pl_ref, w_ref, go_ref, gs_ref, m_ref, v_ref, g_ref, d_ref, nm_ref, nv_ref):
        first = pl_ref[1] == 0
        own, sib = go_ref[...], gs_ref[...]
        g = jnp.concatenate([jnp.where(first, own, sib), jnp.where(first, sib, own)], axis=1)
        g_ref[...] = g
        d_ref[...], nm_ref[...], nv_ref[...] = _adam_math(w_ref[...], g, m_ref[...], v_ref[...])

    full = pl.BlockSpec((tr, c), lambda i, pr: (i, 0))
    half = pl.BlockSpec((tr, c // 2), lambda i, pr: (i, 0))
    o = jax.ShapeDtypeStruct((r, c), F32)
    return pl.pallas_call(
        body, name=name,
        grid_spec=pltpu.PrefetchScalarGridSpec(num_scalar_prefetch=1, grid=(r // tr,),
                                               in_specs=[full, half, half, full, full], out_specs=[full] * 4),
        out_shape=[o] * 4,
        compiler_params=_cparams(("parallel",), 2 * tr * c * 4 * 8),
    )(place, w, g_own, g_sib, m, v)


ANY = pl.BlockSpec(memory_space=pl.ANY)


def _place():
    x, y, c = lax.axis_index("x"), lax.axis_index("y"), lax.axis_index("c")
    others = [(1 - x, y), (x, 1 - y), (1 - x, 1 - y)]
    return x, y, c, 2 * x + y, others


def _remote(src, dst, send, recv, k, to):
    return pltpu.make_async_remote_copy(src_ref=src, dst_ref=dst, send_sem=send.at[k], recv_sem=recv.at[k],
                                        device_id=to, device_id_type=MESH)


def _gather_call(win_b, wout_b, cw8):
    big = [win_b, wout_b]

    def body(win, wout, cw, g_in, g_out, g_cw, send, recv):
        x, y, c, me, others = _place()
        sib = (x, y, 1 - c)
        started = []
        for a, (src, dst) in enumerate(((win, g_in), (wout, g_out))):
            half = src.shape[0] // 2
            mine = pl.ds(c * half, half)
            for j, chip in enumerate(others):
                cp = _remote(src.at[mine], dst.at[me, mine], send, recv, 6 * a + j, (*chip, c))
                cp.start()
                started.append(cp)
        for j, chip in enumerate(others):
            cp = _remote(cw, g_cw.at[me], send, recv, 12 + j, (*chip, c))
            cp.start()
            started.append(cp)
        for a, dst in enumerate((g_in, g_out)):
            half = dst.shape[1] // 2
            mine = pl.ds(c * half, half)
            for j, chip in enumerate(others):
                kj = 2 * chip[0] + chip[1]
                _remote(dst.at[kj, mine], dst.at[kj, mine], send, recv, 6 * a + j, (*chip, c)).wait_recv()
                cp = _remote(dst.at[kj, mine], dst.at[kj, mine], send, recv, 6 * a + 3 + j, sib)
                cp.start()
                started.append(cp)
        for a, dst in enumerate((g_in, g_out)):
            half = dst.shape[1] // 2
            theirs = pl.ds((1 - c) * half, half)
            for j, chip in enumerate(others):
                kj = 2 * chip[0] + chip[1]
                _remote(dst.at[kj, theirs], dst.at[kj, theirs], send, recv, 6 * a + 3 + j, sib).wait_recv()
        for j, chip in enumerate(others):
            kj = 2 * chip[0] + chip[1]
            _remote(cw, g_cw.at[kj], send, recv, 12 + j, (*chip, c)).wait_recv()
        for cp in started:
            cp.wait_send()

    outs = [jax.ShapeDtypeStruct((4,) + a.shape, a.dtype) for a in (win_b, wout_b, cw8)]
    return pl.pallas_call(
        body, name="gather_weights",
        in_specs=[ANY, ANY, ANY], out_specs=[ANY, ANY, ANY], out_shape=outs,
        scratch_shapes=[pltpu.SemaphoreType.DMA((15,)), pltpu.SemaphoreType.DMA((15,))],
    )(win_b, wout_b, cw8)


def _rs_sibling_call(p_in, p_out, vsmall):
    def body(pin, pout, vs, sib_in, sib_out, sib_v, send, recv):
        x, y, c, me, others = _place()
        sib = (x, y, 1 - c)
        cps = []
        for a, (p, sb) in enumerate(((pin, sib_in), (pout, sib_out))):
            half = p.shape[1] // 2
            cps.append(_remote(p.at[:, pl.ds((1 - c) * half, half)], sb, send, recv, a, sib))
        cps.append(_remote(vs, sib_v, send, recv, 2, sib))
        for cp in cps:
            cp.start()
        for cp in cps:
            cp.wait_recv()
        for cp in cps:
            cp.wait_send()

    def halves(p):
        return jax.ShapeDtypeStruct((4, p.shape[1] // 2, p.shape[2]), p.dtype)

    outs = [halves(p_in), halves(p_out), jax.ShapeDtypeStruct(vsmall.shape, vsmall.dtype)]
    return pl.pallas_call(
        body, name="rs_sibling",
        in_specs=[ANY] * 3, out_specs=[ANY] * 3, out_shape=outs,
        scratch_shapes=[pltpu.SemaphoreType.DMA((3,)), pltpu.SemaphoreType.DMA((3,))],
    )(p_in, p_out, vsmall)


def _rs_join_call(f_in, f_out, nw8):
    def body(fin, fout, nw, sib_in, full_out, all_nw, send, recv):
        x, y, c, me, others = _place()
        sib = (x, y, 1 - c)
        half = fout.shape[0]
        cps = [_remote(fin, sib_in, send, recv, 0, sib),
               _remote(fout, full_out.at[pl.ds(c * half, half)], send, recv, 1, sib)]
        mine = 4 * x + 2 * y + c
        peers = []
        for r in range(1, 8):
            px, py, pc = (1 - x if r & 4 else x), (1 - y if r & 2 else y), (1 - c if r & 1 else c)
            peers.append((r, (px, py, pc), 4 * px + 2 * py + pc))
            cps.append(_remote(nw, all_nw.at[mine], send, recv, 1 + r, (px, py, pc)))
        for cp in cps:
            cp.start()
        cps[0].wait_recv()
        _remote(fout, full_out.at[pl.ds((1 - c) * half, half)], send, recv, 1, sib).wait_recv()
        for r, peer, idx in peers:
            _remote(nw, all_nw.at[idx], send, recv, 1 + r, peer).wait_recv()
        for cp in cps:
            cp.wait_send()

    outs = [jax.ShapeDtypeStruct(f_in.shape, F32), jax.ShapeDtypeStruct((2 * f_out.shape[0], f_out.shape[1]), F32),
            jax.ShapeDtypeStruct((8,) + nw8.shape, F32)]
    return pl.pallas_call(
        body, name="rs_join",
        in_specs=[ANY] * 3, out_specs=[ANY] * 3, out_shape=outs,
        scratch_shapes=[pltpu.SemaphoreType.DMA((9,)), pltpu.SemaphoreType.DMA((9,))],
    )(f_in, f_out, nw8)


def _pack(arrs):
    parts = []
    for a in arrs:
        f = a.reshape(-1).astype(F32)
        pad = (-f.shape[0]) % 1024
        parts.append(jnp.pad(f, (0, pad)).reshape(-1, 128))
    return jnp.concatenate(parts, axis=0)


def _unpack(packed, shapes):
    out, row = [], 0
    for shp in shapes:
        n = 1
        for d in shp:
            n *= d
        rows = (n + 1023) // 1024 * 8
        out.append(packed[row:row + rows].reshape(-1)[:n].reshape(shp))
        row += rows
    return out


def _expand_heads(v32):
    return jnp.repeat(v32.reshape(32), HEADDIM).reshape(1, D_MODEL)


def kernel(x, norm_w, w_in, gate_b, sgu_norm_g, sgu_norm_b, sgu_w, sgu_b, conv_w, conv_b, dt_bias, A_log, D_skip, ssd_norm_w, w_out, final_norm_w, loss_target, m_norm_w, m_w_in, m_gate_b, m_sgu_norm_g, m_sgu_norm_b, m_sgu_w, m_sgu_b, m_conv_w, m_conv_b, m_dt_bias, m_A_log, m_D_skip, m_ssd_norm_w, m_w_out, m_final_norm_w, v_norm_w, v_w_in, v_gate_b, v_sgu_norm_g, v_sgu_norm_b, v_sgu_w, v_sgu_b, v_conv_w, v_conv_b, v_dt_bias, v_A_log, v_D_skip, v_ssd_norm_w, v_w_out, v_final_norm_w):
    s = x.shape[1]
    x2 = x.reshape(s, D_MODEL)
    tgt = loss_target.reshape(s, D_MODEL)
    t_ssd, t_tok, t_out, t_row = min(T_SSD, s), min(T_TOK, s), min(T_OUT, s), min(T_ROW, s)
    tm_mm, tk_dw = min(TM_MM, s), min(TK_DW, s)
    chip = 2 * lax.axis_index("x") + lax.axis_index("y")

    cw8 = jnp.pad(conv_w[0], ((0, 4), (0, 0)))
    win_b, wout_b = _c(w_in[0]), _c(w_out[0])
    g_in, g_out, g_cw = _gather_call(win_b, wout_b, cw8)
    g_in = lax.dynamic_update_index_in_dim(g_in, win_b, chip, 0)
    g_out = lax.dynamic_update_index_in_dim(g_out, wout_b, chip, 0)
    g_cw = lax.dynamic_update_index_in_dim(g_cw, cw8, chip, 0)
    wref = jnp.transpose(g_in, (1, 0, 2)).reshape(D_MODEL, IN_W)
    w_out_full = g_out.reshape(D_MODEL, D_MODEL)
    conv_w_full = jnp.transpose(g_cw[:, 0:4, :], (1, 0, 2)).reshape(4, 3072)

    w_a = jnp.concatenate([wref[:, 0:6144], wref[:, 11296:15392]], axis=1)
    fw, bw = [], []
    for g in range(SSD_GROUPS):
        xs_g = wref[:, 8192 + 512 * g:8192 + 512 * g + 512]
        b_g = wref[:, 10240 + 128 * g:10240 + 128 * g + 128]
        c_g = wref[:, 10752 + 128 * g:10752 + 128 * g + 128]
        zb_g = wref[:, 6144 + 512 * g:6144 + 512 * g + 512]
        dt_g = wref[:, 11264 + 8 * g:11264 + 8 * g + 8]
        fw += [xs_g, b_g, c_g, jnp.repeat(dt_g, HEADDIM, axis=1), zb_g]
        bw += [zb_g, xs_g, b_g, c_g, jnp.pad(dt_g, ((0, 0), (0, 120)))]
    w_b = jnp.concatenate(fw, axis=1)
    wt_a = w_a.T
    wt_b = jnp.concatenate(bw, axis=1).T

    def group_cols(full_xs, full_bc):
        parts = []
        for g in range(SSD_GROUPS):
            parts += [full_xs[:, 512 * g:512 * g + 512], full_bc[:, 128 * g:128 * g + 128], full_bc[:, 512 + 128 * g:512 + 128 * g + 128]]
        return jnp.concatenate(parts, axis=1)

    cw_g = group_cols(conv_w_full[:, 0:2048], conv_w_full[:, 2048:3072])
    cb_g = group_cols(conv_b[:, 0:2048], conv_b[:, 2048:3072])
    dtb_e, alog_e, dsk_e = _expand_heads(dt_bias), _expand_heads(A_log), _expand_heads(D_skip)

    pos_chunk = jnp.arange(SGU_BLOCK) // CHUNK
    smask = pos_chunk[None, :] <= pos_chunk[:, None]
    wm_f = jnp.where(smask[None], sgu_w[0], 0.0)
    wm = _c(wm_f)
    wmt = _c(jnp.transpose(wm_f, (0, 2, 1)))
    bias_full = jnp.repeat(sgu_b[0].T, D_MODEL // SGU_GROUPS, axis=1)
    fnw = final_norm_w.reshape(1, D_MODEL)

    xn, xnt = _norm_call(x2, norm_w, t_row)
    proj_a = _mm(xn, w_a, tm=tm_mm, tn=1024, tk=D_MODEL, name="in_proj_a", out_dtype=MXU_DTYPE)
    proj_b = _mm(xn, w_b, tm=tm_mm, tn=1024, tk=D_MODEL, name="in_proj_b")
    y_ssd, y_b, hprev, pre_all, dt_all, acs_all = _ssd_fwd_call(proj_b, dtb_e, alog_e, dsk_e, cw_g, cb_g, ssd_norm_w, t_ssd, NG_SSD)
    y_a, merged, merged_t = _tok_fwd_call(proj_a, y_b, gate_b, sgu_norm_g, sgu_norm_b, wm, bias_full, t_tok)
    dh, dh_b, dmerged, loss_t, dfw8 = _out_call(merged, x2, tgt, w_out_full, fnw, t_out)

    dproj_a, dy_b, dgb8, dgam8, dbeta8, dbfull, dws = _tok_bwd_call(
        proj_a, dmerged, y_a, y_b, gate_b, sgu_norm_g, sgu_norm_b, wm, wmt, bias_full, t_tok)
    dproj_b, a512, a768 = _ssd_bwd_call(proj_b, pre_all, dt_all, acs_all, dy_b, y_ssd, hprev, dtb_e, alog_e, dsk_e, cw_g,
                                        ssd_norm_w, t_ssd, NG_SSD)
    dw_a = _mm(xnt, dproj_a, tm=D_MODEL, tn=1024, tk=tk_dw, name="dw_in_a")
    dw_b = _mm(xnt, dproj_b, tm=D_MODEL, tn=BW_B, tk=tk_dw, name="dw_in_b")
    dw_out_p = _mm(merged_t, dh_b, tm=D_MODEL, tn=1024, tk=tk_dw, name="dw_out")

    gb = lambda a, b: jnp.concatenate([dw_b[:, BW_B * g + a:BW_B * g + b] for g in range(SSD_GROUPS)], axis=1)
    dw_ref = jnp.concatenate([dw_a[:, 0:6144], gb(0, 512), gb(512, 1024), gb(1024, 1152), gb(1152, 1280),
                              gb(1280, 1288), dw_a[:, 6144:10240]], axis=1)
    p_in = jnp.transpose(dw_ref.reshape(D_MODEL, 4, SHARD_W), (1, 0, 2))
    p_out = dw_out_p.reshape(4, D_MODEL // 4, D_MODEL)

    s512 = jnp.sum(a512, axis=2)
    heads = lambda v: jnp.sum(v.reshape(32, HEADDIM), axis=1).reshape(1, 32)
    d_ssd_nw = s512[:, 0].reshape(1, D_MODEL)
    d_dskip = heads(s512[:, 1].reshape(D_MODEL))
    d_alog = heads(s512[:, 2].reshape(D_MODEL)) * (1.0 / HEADDIM) * (-jnp.exp(A_log))
    d_dtb = heads(s512[:, 3].reshape(D_MODEL))
    s768 = jnp.sum(a768, axis=2)
    ungroup = lambda v: jnp.concatenate([v[g, :, 0:512] for g in range(4)] + [v[g, :, 512:640] for g in range(4)]
                                        + [v[g, :, 640:768] for g in range(4)], axis=1)
    d_cw = ungroup(s768[:, 0:4])
    d_cb = ungroup(s768[:, 4:5])
    d_sgu_b = jnp.sum(dbfull.reshape(128, SGU_GROUPS, 128), axis=2).T.reshape(1, SGU_GROUPS, 128)
    d_sgu_w = jnp.where(smask[None], dws, 0.0).reshape(1, SGU_GROUPS, 128, 128)
    fold = lambda a8: jnp.sum(a8, axis=0, keepdims=True)
    small_local = [fold(dgb8), fold(dgam8), fold(dbeta8), d_sgu_w, d_sgu_b, d_cw, d_cb,
                   d_dtb, d_alog, d_dskip, d_ssd_nw, fold(dfw8).reshape(D_MODEL), jnp.sum(loss_t[:, 0, 0]).reshape(1)]
    small_shapes = [a.shape for a in small_local]
    v_local = _pack(small_local)

    core = lax.axis_index("c")
    place = jnp.stack([chip, core]).astype(jnp.int32)
    hr_i, hr_o = D_MODEL // 2, D_MODEL // 8
    sib_i, sib_o, sib_v = _rs_sibling_call(p_in, p_out, v_local)
    s1b_i, o_i = _rs_add(p_in, sib_i, place, 256, "rs_add_in")
    s1b_o, o_o = _rs_add(p_out, sib_o, place, 256, "rs_add_out")
    chip_v = _add_pair(v_local, sib_v, v_local.shape[0], "ar_add_small")
    dxn, r_i, r_o, abs_v = _dx_rs_call(dproj_a, wt_a, dproj_b, wt_b, s1b_i, s1b_o, chip_v, tm=tm_mm)
    grad_x, dnw8 = _gradx_call(x2, dxn, dh, norm_w, t_row)
    abs_v = lax.dynamic_update_index_in_dim(abs_v, chip_v, chip, 0)
    f_i = _sum_own_recv(o_i, r_i, 256, "rs_sum_in")
    f_o = _sum_own_recv(o_o, r_o, 256, "rs_sum_out")
    sib_f_i, g_w_out, all_nw = _rs_join_call(f_i, f_o, dnw8)
    g_w_out = lax.dynamic_update_slice_in_dim(g_w_out, f_o, core * hr_o, axis=0)
    all_nw = lax.dynamic_update_index_in_dim(all_nw, dnw8, 2 * chip + core, 0)
    g_nw = fold(_sum_slots(all_nw, "ar_sum_norm_w"))
    total_v = _sum_slots(abs_v, "ar_sum_small")
    (g_gb, g_gam, g_beta, g_sw, g_sb, g_cw_full, g_cb, g_dtb, g_alog, g_dsk, g_snw, g_fnw, loss1) = _unpack(total_v, small_shapes)
    g_cw_shard = lax.dynamic_slice(g_cw_full, (0, chip * 768), (4, 768)).reshape(1, 4, 768)
    loss = loss1.reshape(())

    g_w_in, d_win, nm_win, nv_win = (a.T for a in _adamw_halves(w_in[0].T, f_i.T, sib_f_i.T, m_w_in[0].T, v_w_in[0].T,
                                                                place, 296, "adamw_w_in"))
    d_wout, nm_wout, nv_wout = _adamw(w_out[0], g_w_out, m_w_out[0], v_w_out[0], 128, "adamw_w_out")
    small_w = [norm_w, gate_b, sgu_norm_g, sgu_norm_b, sgu_w, sgu_b, conv_w, conv_b, dt_bias, A_log, D_skip, ssd_norm_w, final_norm_w]
    small_m = [m_norm_w, m_gate_b, m_sgu_norm_g, m_sgu_norm_b, m_sgu_w, m_sgu_b, m_conv_w, m_conv_b, m_dt_bias, m_A_log, m_D_skip, m_ssd_norm_w, m_final_norm_w]
    small_v = [v_norm_w, v_gate_b, v_sgu_norm_g, v_sgu_norm_b, v_sgu_w, v_sgu_b, v_conv_w, v_conv_b, v_dt_bias, v_A_log, v_D_skip, v_ssd_norm_w, v_final_norm_w]
    small_g = [g_nw, g_gb, g_gam, g_beta, g_sw, g_sb, g_cw_shard, g_cb, g_dtb, g_alog, g_dsk, g_snw, g_fnw]
    shapes_w = [a.shape for a in small_w]
    small_g = [a.reshape(shp) for a, shp in zip(small_g, shapes_w)]
    pw = _pack(small_w)
    pd, pm, pv = _adamw(pw, _pack(small_g), _pack(small_m), _pack(small_v), pw.shape[0], "adamw_small")
    d_small, nm_small, nv_small = _unpack(pd, shapes_w), _unpack(pm, shapes_w), _unpack(pv, shapes_w)

    def with_big(small, win, wout):
        o = list(small)
        return o[0:1] + [win.reshape(1, D_MODEL, SHARD_W)] + o[1:12] + [wout.reshape(1, D_MODEL // 4, D_MODEL)] + o[12:13]

    grads = with_big(small_g, g_w_in, g_w_out)
    deltas = with_big(d_small, d_win, d_wout)
    new_m = with_big(nm_small, nm_win, nm_wout)
    new_v = with_big(nv_small, nv_win, nv_wout)
    return (loss, grad_x.reshape(1, s, D_MODEL), *grads, *deltas, *new_m, *new_v)
```

```python
import functools

import jax
import jax.numpy as jnp
from jax import lax
from jax.experimental import pallas as pl
from jax.experimental.pallas import tpu as pltpu

F32 = jnp.float32
MXU_DTYPE = jnp.bfloat16

D_MODEL = 2048
EPS = 1e-5
CHUNK = 64
SGU_BLOCK = 128
SGU_GROUPS = 16
SSD_GROUPS = 4
SSD_GW = 512
SSD_STATE = 128
HEADDIM = 64
IN_W = 15392
SHARD_W = IN_W // 4
FW_B = 1792
BW_B = 1408
NA = 10240

ADAM_LR = 0.001
ADAM_B1 = 0.9
ADAM_B2 = 0.999
ADAM_EPS = 1e-08
ADAM_WD = 0.01
ADAM_STEP = 10

T_SSD = 256
NG_SSD = 2
T_TOK = 128
T_OUT = 256
T_ROW = 512
TM_MM = 1024
TK_DW = 1024
VMEM_CAP = 60 * 1024 * 1024
MESH = pl.DeviceIdType.MESH


def _cparams(sem, est_bytes):
    lim = int(min(VMEM_CAP, max(32 * 1024 * 1024, est_bytes + 12 * 1024 * 1024)))
    return pltpu.CompilerParams(dimension_semantics=sem, vmem_limit_bytes=lim)


def _c(x):
    return x.astype(MXU_DTYPE)


def _dot(a, b):
    return jnp.dot(a, b, preferred_element_type=F32)


def _dot_nt(a, b):
    return lax.dot_general(a, b, (((1,), (1,)), ((), ())), preferred_element_type=F32)


def _dot_tn(a, b):
    return lax.dot_general(a, b, (((0,), (0,)), ((), ())), preferred_element_type=F32)


def _split(x, n):
    parts, r = [], x
    for _ in range(n):
        p = _c(r)
        parts.append(p)
        r = r - p.astype(F32)
    return parts


def _dot01_l(m01, x, n):
    acc = None
    for p in _split(x, n):
        t = _dot(m01, p)
        acc = t if acc is None else acc + t
    return acc


def _dot01_r(x, m01, n):
    acc = None
    for p in _split(x, n):
        t = _dot(p, m01)
        acc = t if acc is None else acc + t
    return acc


def _sigmoid(x):
    return 1.0 / (1.0 + jnp.exp(-x))


def _fold8(x):
    r, w = x.shape
    return jnp.sum(x.reshape(r // 8, 8, w), axis=0)


def _iota(shape, dim):
    return lax.broadcasted_iota(jnp.int32, shape, dim)


def _ssd_masks():
    l64 = _iota((CHUNK, SSD_GW), 0)
    s64 = jnp.bitwise_and(_iota((CHUNK, SSD_GW), 1), CHUNK - 1)
    diag = l64 == s64
    causal = l64 >= s64
    row_last = l64 == CHUNK - 1
    r4 = lax.shift_right_logical(_iota((256, 256), 0), 6)
    c4 = lax.shift_right_logical(_iota((256, 256), 1), 6)
    mask4 = r4 == c4
    return diag, causal, row_last, mask4


def _cumsum_mats(t):
    r, c = _iota((t, t), 0), _iota((t, t), 1)
    same = lax.shift_right_logical(r, 6) == lax.shift_right_logical(c, 6)
    tri = _c(jnp.where(same, jnp.where(c <= r, 1.0, 0.0), 0.0))
    trit = _c(jnp.where(same, jnp.where(c >= r, 1.0, 0.0), 0.0))
    return tri, trit


def _ssd_common(xs, bm, cm, dt, acs, masks):
    diag, causal, row_last, mask4 = masks
    row_e = jnp.sum(jnp.where(diag, acs, 0.0), axis=0, keepdims=True)
    seg = acs - row_e
    lm = jnp.exp(jnp.where(causal, seg, -1e30))
    bb, cb = _c(bm), _c(cm)
    brep = jnp.concatenate([bb] * 8, axis=0)
    cbrep = _dot_nt(cb, brep)
    m = cbrep * lm
    xdt = xs * dt
    acs_last = jnp.sum(jnp.where(row_last, acs, 0.0), axis=0, keepdims=True)
    dec = jnp.exp(acs_last - acs)
    eacs = jnp.exp(acs)
    cd = jnp.exp(acs_last)
    return dict(lm=lm, bb=bb, cb=cb, brep=brep, m=m, xdt=xdt, dec=dec, eacs=eacs, cd=cd)


def _blockdiag4(xb, mask4):
    return jnp.where(mask4, jnp.concatenate([xb] * 4, axis=0), jnp.zeros((), xb.dtype))


def _ssd_chunk_fwd(xs, bm, cm, dt, acs, d_skip, ht, masks):
    q = _ssd_common(xs, bm, cm, dt, acs, masks)
    mask4 = masks[3]
    mb, xdtb = _c(q["m"]), _c(q["xdt"])
    yd = []
    for blk in range(2):
        sl = slice(256 * blk, 256 * blk + 256)
        yd.append(_dot(mb[:, sl], _blockdiag4(xdtb[:, sl], mask4)))
    y_diag = jnp.concatenate(yd, axis=1)
    p = _dot(q["cb"], _c(ht))
    y = y_diag + p * q["eacs"] + xs * d_skip
    st = _dot_tn(q["bb"], _c(q["xdt"] * q["dec"]))
    return y, ht * q["cd"] + st


def _ssd_chunk_bwd(xs, bm, cm, dt, acs, d_skip, hprev, dht, dy, masks):
    diag, causal, row_last, mask4 = masks
    q = _ssd_common(xs, bm, cm, dt, acs, masks)
    lm, bb, cb, brep, m, xdt, dec, eacs, cd = (q[k] for k in ("lm", "bb", "cb", "brep", "m", "xdt", "dec", "eacs", "cd"))
    hb = _c(hprev)
    yoff = _dot(cb, hb) * eacs
    dyb = _c(dy)
    dpb = _c(dy * eacs)
    d_c = _dot_nt(dpb, hb)
    dh_y = _dot_tn(cb, dpb)
    mb, xdtb = _c(m), _c(xdt)
    dm_parts, dxdt_parts = [], []
    for blk in range(2):
        sl = slice(256 * blk, 256 * blk + 256)
        bd = _blockdiag4(xdtb[:, sl], mask4)
        dm_parts.append(_dot_nt(dyb[:, sl], bd))
        dxf = jnp.where(mask4, _dot_tn(mb[:, sl], dyb[:, sl]), 0.0)
        dxdt_parts.append(dxf[0:64] + dxf[64:128] + dxf[128:192] + dxf[192:256])
    dm = jnp.concatenate(dm_parts, axis=1)
    dxdt = jnp.concatenate(dxdt_parts, axis=1)
    dcbb = _c(dm * lm)
    g = dm * m
    d_c = d_c + _dot(dcbb, brep)
    dbrep = _dot_tn(dcbb, cb)
    d_b = dbrep[0:64]
    for r in range(1, 8):
        d_b = d_b + dbrep[64 * r:64 * r + 64]
    dhtb = _c(dht)
    dxd = _dot(bb, dhtb)
    xd = xdt * dec
    dxdt = dxdt + dxd * dec
    tq = dxd * xd
    d_b = d_b + _dot_nt(_c(xd), dhtb)
    dcd = jnp.sum(dht * hprev, axis=0, keepdims=True)
    col_g = jnp.sum(g, axis=0, keepdims=True)
    last = jnp.sum(tq, axis=0, keepdims=True) + dcd * cd
    qq = g - jnp.where(diag, col_g, 0.0) + dy * yoff - tq + jnp.where(row_last, last, 0.0)
    dxs = dxdt * dt + dy * d_skip
    return dxs, d_b, d_c, dht * cd + dh_y, dy * xs, qq, dxdt * xs


def _ssd_finish_dt(qq, p1, dt, sig, a_neg, trit, mask4):
    bd4 = _c(jnp.where(mask4, 1.0, 0.0))
    dacs = jnp.concatenate([_dot01_r(qq[:, 256 * b:256 * b + 256], bd4, 2) for b in range(2)], axis=1)
    da = _dot01_l(trit, dacs, 2)
    ddt = p1 + da * (a_neg * (1.0 / HEADDIM))
    return ddt * sig, da * dt


def _softplus(x):
    return jnp.maximum(x, 0.0) + jnp.log1p(jnp.exp(-jnp.abs(x)))


def _conv_taps(xpad, t):
    taps = []
    for k in range(4):
        sh = 3 - k
        v = xpad if sh == 0 else pltpu.roll(xpad, sh, 0)
        taps.append(v[8:8 + t])
    return taps


def _norm_call(x, norm_w, tm):
    s = x.shape[0]

    def body(x_ref, w_ref, xn_ref, xnt_ref):
        xv = x_ref[...]
        r = lax.rsqrt(jnp.mean(xv * xv, axis=-1, keepdims=True) + EPS)
        xn = xv * r * w_ref[...]
        xn_ref[...] = _c(xn)
        xnt_ref[...] = _c(xn.T)

    return pl.pallas_call(
        body, name="rmsnorm_in",
        grid=(s // tm,),
        in_specs=[pl.BlockSpec((tm, D_MODEL), lambda i: (i, 0)), pl.BlockSpec((1, D_MODEL), lambda i: (0, 0))],
        out_specs=[pl.BlockSpec((tm, D_MODEL), lambda i: (i, 0)), pl.BlockSpec((D_MODEL, tm), lambda i: (0, i))],
        out_shape=[jax.ShapeDtypeStruct((s, D_MODEL), MXU_DTYPE), jax.ShapeDtypeStruct((D_MODEL, s), MXU_DTYPE)],
        compiler_params=_cparams(("parallel",), 2 * tm * D_MODEL * 12),
    )(x, norm_w)


def _mm(a, b, *, tm, tn, tk, name, out_dtype=F32):
    m, k = a.shape
    n = b.shape[1]
    nk = k // tk
    assert m % tm == 0 and n % tn == 0 and k % tk == 0, (a.shape, b.shape, tm, tn, tk)
    assert nk == 1 or out_dtype == F32

    def body(a_ref, b_ref, o_ref):
        if nk == 1:
            o_ref[...] = _dot(a_ref[...], b_ref[...]).astype(out_dtype)
        else:
            @pl.when(pl.program_id(2) == 0)
            def _():
                o_ref[...] = jnp.zeros_like(o_ref)

            o_ref[...] += _dot(a_ref[...], b_ref[...])

    isz = jnp.dtype(a.dtype).itemsize
    est = 2 * (tm * tk + tk * tn) * isz + 2 * tm * tn * 4
    return pl.pallas_call(
        body, name=name,
        grid=(m // tm, n // tn, nk),
        in_specs=[pl.BlockSpec((tm, tk), lambda i, j, kk: (i, kk)), pl.BlockSpec((tk, tn), lambda i, j, kk: (kk, j))],
        out_specs=pl.BlockSpec((tm, tn), lambda i, j, kk: (i, j)),
        out_shape=jax.ShapeDtypeStruct((m, n), out_dtype),
        compiler_params=_cparams(("parallel", "parallel", "arbitrary"), est),
    )(a, b)


def _dx_rs_call(dpa, wta, dpb, wtb, sb_in, sb_out, chip_v, *, tm):
    s = dpa.shape[0]
    tka, tkb = 1024, BW_B
    nka, nkb = dpa.shape[1] // tka, dpb.shape[1] // tkb
    ni, nk = s // tm, nka + nkb

    def body(a_ref, wa_ref, b_ref, wb_ref, sbin, sbout, cv, o_ref, rc_in, rc_out, abs_v, send, recv):
        i, kk = pl.program_id(0), pl.program_id(1)

        def copies():
            x, y, c, me, others = _place()
            sends, recvs = [], []
            for j, chip in enumerate(others):
                kj = 2 * chip[0] + chip[1]
                to = (*chip, c)
                sends += [_remote(sbin.at[kj], rc_in.at[j], send, recv, j, to),
                          _remote(sbout.at[kj], rc_out.at[j], send, recv, 3 + j, to),
                          _remote(cv, abs_v.at[me], send, recv, 6 + j, to)]
                recvs += [sends[-3], sends[-2], _remote(cv, abs_v.at[kj], send, recv, 6 + j, to)]
            return sends, recvs

        @pl.when((i == 0) & (kk == 0))
        def _():
            for cp in copies()[0]:
                cp.start()

        @pl.when(kk == 0)
        def _():
            o_ref[...] = jnp.zeros_like(o_ref)

        @pl.when(kk < nka)
        def _():
            o_ref[...] += _dot(a_ref[...], wa_ref[...])

        @pl.when(kk >= nka)
        def _():
            o_ref[...] += _dot(b_ref[...], wb_ref[...])

        @pl.when((i == ni - 1) & (kk == nk - 1))
        def _():
            sends, recvs = copies()
            for cp in recvs:
                cp.wait_recv()
            for cp in sends:
                cp.wait_send()

    isz = jnp.dtype(dpa.dtype).itemsize
    est = 2 * isz * (tm * tka + tka * D_MODEL + tm * tkb + tkb * D_MODEL) + 2 * tm * D_MODEL * 4
    outs = [jax.ShapeDtypeStruct((s, D_MODEL), F32),
            jax.ShapeDtypeStruct((3,) + sb_in.shape[1:], sb_in.dtype), jax.ShapeDtypeStruct((3,) + sb_out.shape[1:], sb_out.dtype),
            jax.ShapeDtypeStruct((4,) + chip_v.shape, F32)]
    return pl.pallas_call(
        body, name="dx_matmul_rs_chips",
        grid=(ni, nk),
        in_specs=[
            pl.BlockSpec((tm, tka), lambda i, kk: (i, jnp.minimum(kk, nka - 1))),
            pl.BlockSpec((tka, D_MODEL), lambda i, kk: (jnp.minimum(kk, nka - 1), 0)),
            pl.BlockSpec((tm, tkb), lambda i, kk: (i, jnp.maximum(kk - nka, 0))),
            pl.BlockSpec((tkb, D_MODEL), lambda i, kk: (jnp.maximum(kk - nka, 0), 0)),
            ANY, ANY, ANY,
        ],
        out_specs=[pl.BlockSpec((tm, D_MODEL), lambda i, kk: (i, 0)), ANY, ANY, ANY],
        out_shape=outs,
        scratch_shapes=[pltpu.SemaphoreType.DMA((9,)), pltpu.SemaphoreType.DMA((9,))],
        compiler_params=_cparams(("arbitrary", "arbitrary"), est),
    )(dpa, wta, dpb, wtb, sb_in, sb_out, chip_v)


def _gradx_call(x, dxn, dh, norm_w, tm):
    s = x.shape[0]

    def body(x_ref, g_ref, dh_ref, w_ref, gx_ref, dw_ref):
        @pl.when(pl.program_id(0) == 0)
        def _():
            dw_ref[...] = jnp.zeros_like(dw_ref)

        xv, gv = x_ref[...], g_ref[...]
        r = lax.rsqrt(jnp.mean(xv * xv, axis=-1, keepdims=True) + EPS)
        gw = gv * w_ref[...]
        gx_ref[...] = r * gw - xv * (r * r * r) * jnp.mean(xv * gw, axis=-1, keepdims=True) + dh_ref[...]
        dw_ref[...] += _fold8(gv * (xv * r))

    row = pl.BlockSpec((tm, D_MODEL), lambda i: (i, 0))
    return pl.pallas_call(
        body, name="grad_x",
        grid=(s // tm,),
        in_specs=[row, row, row, pl.BlockSpec((1, D_MODEL), lambda i: (0, 0))],
        out_specs=[row, pl.BlockSpec((8, D_MODEL), lambda i: (0, 0))],
        out_shape=[jax.ShapeDtypeStruct((s, D_MODEL), F32), jax.ShapeDtypeStruct((8, D_MODEL), F32)],
        compiler_params=_cparams(("arbitrary",), 2 * tm * D_MODEL * 16),
    )(x, dxn, dh, norm_w)


def _layernorm_stats(v):
    mu = jnp.mean(v, axis=-1, keepdims=True)
    vc = v - mu
    var = jnp.mean(vc * vc, axis=-1, keepdims=True)
    return vc * lax.rsqrt(var + EPS), lax.rsqrt(var + EPS)


def _tok_fwd_call(proj_a, y_b, gate_b, sgu_g, sgu_beta, wm, bias_full, t):
    s = proj_a.shape[0]

    def body(pa_ref, yb_ref, gb_ref, g_ref, be_ref, wm_ref, bf_ref, ya_ref, mg_ref, mgt_ref, mix_ref):
        u = pa_ref[:, 0:2048].astype(F32)
        v = pa_ref[:, 2048:4096].astype(F32)
        za = pa_ref[:, 4096:6144].astype(F32)
        xhat, _ = _layernorm_stats(v)
        vnb = _c(xhat * g_ref[...] + be_ref[...])
        for gi in range(SGU_GROUPS):
            sl = slice(128 * gi, 128 * gi + 128)
            mix_ref[:, sl] = _dot(wm_ref[gi], vnb[:, sl])
        mixed = mix_ref[...] + bf_ref[...]
        y_a = u * mixed * (za * _sigmoid(za))
        g0 = _sigmoid(pa_ref[:, 6144:8192].astype(F32) + gb_ref[:, 0:2048])
        g1 = _sigmoid(pa_ref[:, 8192:10240].astype(F32) + gb_ref[:, 2048:4096])
        merged = g0 * y_a + g1 * yb_ref[...].astype(F32)
        ya_ref[...] = _c(y_a)
        mg_ref[...] = _c(merged)
        mgt_ref[...] = _c(merged.T)

    row = pl.BlockSpec((t, D_MODEL), lambda i: (i, 0))
    vec = lambda w: pl.BlockSpec((1, w), lambda i: (0, 0))
    return pl.pallas_call(
        body, name="tok_fwd",
        grid=(s // t,),
        in_specs=[pl.BlockSpec((t, NA), lambda i: (i, 0)), row, vec(4096), vec(2048), vec(2048),
                  pl.BlockSpec((SGU_GROUPS, 128, 128), lambda i: (0, 0, 0)), pl.BlockSpec((128, D_MODEL), lambda i: (0, 0))],
        out_specs=[row, row, pl.BlockSpec((D_MODEL, t), lambda i: (0, i))],
        out_shape=[jax.ShapeDtypeStruct((s, D_MODEL), MXU_DTYPE), jax.ShapeDtypeStruct((s, D_MODEL), MXU_DTYPE),
                   jax.ShapeDtypeStruct((D_MODEL, s), MXU_DTYPE)],
        scratch_shapes=[pltpu.VMEM((t, D_MODEL), F32)],
        compiler_params=_cparams(("parallel",), 2 * t * NA * 4 + 12 * t * D_MODEL * 4),
    )(proj_a, y_b, gate_b, sgu_g, sgu_beta, wm, bias_full)


def _tok_bwd_call(proj_a, dmerged, y_a, y_b, gate_b, sgu_g, sgu_beta, wm, wmt, bias_full, t):
    s = proj_a.shape[0]

    def body(pa_ref, dm_ref, ya_ref, yb_ref, gb_ref, g_ref, be_ref, wm_ref, wmt_ref, bf_ref,
             dpa_ref, dyb_ref, dgb_ref, dgam_ref, dbeta_ref, dbf_ref, dws_ref, mix_ref, dvn_ref):
        @pl.when(pl.program_id(0) == 0)
        def _():
            dgb_ref[...] = jnp.zeros_like(dgb_ref)
            dgam_ref[...] = jnp.zeros_like(dgam_ref)
            dbeta_ref[...] = jnp.zeros_like(dbeta_ref)
            dbf_ref[...] = jnp.zeros_like(dbf_ref)
            dws_ref[...] = jnp.zeros_like(dws_ref)

        u = pa_ref[:, 0:2048].astype(F32)
        v = pa_ref[:, 2048:4096].astype(F32)
        za = pa_ref[:, 4096:6144].astype(F32)
        xhat, rstd = _layernorm_stats(v)
        vnb = _c(xhat * g_ref[...] + be_ref[...])
        for gi in range(SGU_GROUPS):
            sl = slice(128 * gi, 128 * gi + 128)
            mix_ref[:, sl] = _dot(wm_ref[gi], vnb[:, sl])
        mixed = mix_ref[...] + bf_ref[...]
        sig = _sigmoid(za)
        sz = za * sig
        dm = dm_ref[...].astype(F32)
        y_a = ya_ref[...].astype(F32)
        g0 = _sigmoid(pa_ref[:, 6144:8192].astype(F32) + gb_ref[:, 0:2048])
        g1 = _sigmoid(pa_ref[:, 8192:10240].astype(F32) + gb_ref[:, 2048:4096])
        dgl0 = dm * y_a * g0 * (1.0 - g0)
        dgl1 = dm * yb_ref[...].astype(F32) * g1 * (1.0 - g1)
        dyb_ref[...] = _c(dm * g1)
        dya = dm * g0
        dpa_ref[:, 6144:8192] = _c(dgl0)
        dpa_ref[:, 8192:10240] = _c(dgl1)
        dgb_ref[:, 0:2048] += _fold8(dgl0)
        dgb_ref[:, 2048:4096] += _fold8(dgl1)
        dpa_ref[:, 0:2048] = _c(dya * mixed * sz)
        dpa_ref[:, 4096:6144] = _c(dya * (u * mixed) * (sig * (1.0 + za * (1.0 - sig))))
        dmixed = dya * u * sz
        dbf_ref[...] += dmixed
        dmb = _c(dmixed)
        for gi in range(SGU_GROUPS):
            sl = slice(128 * gi, 128 * gi + 128)
            dvn_ref[:, sl] = _dot(wmt_ref[gi], dmb[:, sl])
            dws_ref[gi] += _dot_nt(dmb[:, sl], vnb[:, sl])
        dvn = dvn_ref[...]
        dgam_ref[...] += _fold8(dvn * xhat)
        dbeta_ref[...] += _fold8(dvn)
        dxh = dvn * g_ref[...]
        dv = rstd * (dxh - jnp.mean(dxh, axis=-1, keepdims=True) - xhat * jnp.mean(dxh * xhat, axis=-1, keepdims=True))
        dpa_ref[:, 2048:4096] = _c(dv)

    row = pl.BlockSpec((t, D_MODEL), lambda i: (i, 0))
    vec = lambda w: pl.BlockSpec((1, w), lambda i: (0, 0))
    acc = lambda w: pl.BlockSpec((8, w), lambda i: (0, 0))
    wspec = pl.BlockSpec((SGU_GROUPS, 128, 128), lambda i: (0, 0, 0))
    return pl.pallas_call(
        body, name="tok_bwd",
        grid=(s // t,),
        in_specs=[pl.BlockSpec((t, NA), lambda i: (i, 0)), row, row, row, vec(4096), vec(2048), vec(2048),
                  wspec, wspec, pl.BlockSpec((128, D_MODEL), lambda i: (0, 0))],
        out_specs=[pl.BlockSpec((t, NA), lambda i: (i, 0)), row, acc(4096), acc(2048), acc(2048),
                   pl.BlockSpec((128, D_MODEL), lambda i: (0, 0)), wspec],
        out_shape=[jax.ShapeDtypeStruct((s, NA), MXU_DTYPE), jax.ShapeDtypeStruct((s, D_MODEL), MXU_DTYPE),
                   jax.ShapeDtypeStruct((8, 4096), F32), jax.ShapeDtypeStruct((8, 2048), F32),
                   jax.ShapeDtypeStruct((8, 2048), F32), jax.ShapeDtypeStruct((128, D_MODEL), F32),
                   jax.ShapeDtypeStruct((SGU_GROUPS, 128, 128), F32)],
        scratch_shapes=[pltpu.VMEM((t, D_MODEL), F32), pltpu.VMEM((t, D_MODEL), F32)],
        compiler_params=_cparams(("arbitrary",), 2 * t * NA * 6 + 16 * t * D_MODEL * 4),
    )(proj_a, dmerged, y_a, y_b, gate_b, sgu_g, sgu_beta, wm, wmt, bias_full)


def _out_call(merged, x, target, w_out, fnw, t):
    s = x.shape[0]
    nt = s // t

    def body(mg_ref, x_ref, t_ref, w_ref, fw_ref, dh_ref, dhb_ref, dmg_ref, loss_ref, dfw_ref):
        @pl.when(pl.program_id(0) == 0)
        def _():
            dfw_ref[...] = jnp.zeros_like(dfw_ref)

        h = x_ref[...] + _dot(mg_ref[...], w_ref[...])
        r = lax.rsqrt(jnp.mean(h * h, axis=-1, keepdims=True) + EPS)
        hn = h * r
        err = hn * fw_ref[...] - t_ref[...]
        loss_ref[...] = jnp.full(loss_ref.shape, 0.5 * jnp.sum(jnp.mean(err * err, axis=-1, keepdims=True)), F32)
        dy = err * (1.0 / D_MODEL)
        dfw_ref[...] += _fold8(dy * hn)
        gw = dy * fw_ref[...]
        dh = r * gw - h * (r * r * r) * jnp.mean(h * gw, axis=-1, keepdims=True)
        dh_ref[...] = dh
        dhb = _c(dh)
        dhb_ref[...] = dhb
        dmg_ref[...] = _c(_dot_nt(dhb, w_ref[...]))

    row = pl.BlockSpec((t, D_MODEL), lambda i: (i, 0))
    return pl.pallas_call(
        body, name="out_proj_loss",
        grid=(nt,),
        in_specs=[row, row, row, pl.BlockSpec((D_MODEL, D_MODEL), lambda i: (0, 0)), pl.BlockSpec((1, D_MODEL), lambda i: (0, 0))],
        out_specs=[row, row, row, pl.BlockSpec((1, 8, 128), lambda i: (i, 0, 0)), pl.BlockSpec((8, D_MODEL), lambda i: (0, 0))],
        out_shape=[jax.ShapeDtypeStruct((s, D_MODEL), F32), jax.ShapeDtypeStruct((s, D_MODEL), MXU_DTYPE),
                   jax.ShapeDtypeStruct((s, D_MODEL), MXU_DTYPE), jax.ShapeDtypeStruct((nt, 8, 128), F32),
                   jax.ShapeDtypeStruct((8, D_MODEL), F32)],
        compiler_params=_cparams(("arbitrary",), 2 * D_MODEL * D_MODEL * 2 + 2 * t * D_MODEL * 24),
    )(merged, x, target, w_out, fnw)


def _ssd_fwd_call(proj_b, dtb, alog, dsk, cw, cb, nw, t, ng):
    s = proj_b.shape[0]
    nt, nch = s // t, t // CHUNK

    def body(pb_ref, halo_ref, dtb_ref, al_ref, ds_ref, cw_ref, cb_ref, nw_ref, y_ref, yb_ref, hp_ref, pre_ref,
             dt_ref, acs_ref, ht_ref, prev_ref):
        i = pl.program_id(1)

        @pl.when(i == 0)
        def _():
            ht_ref[...] = jnp.zeros_like(ht_ref)

        for gi in range(ng):
            prev_ref[:, 768 * gi:768 * gi + 768] = jnp.where(i == 0, 0.0, halo_ref[:, FW_B * gi:FW_B * gi + 768])
        masks = _ssd_masks()
        tri, _ = _cumsum_mats(CHUNK)
        a_neg = -jnp.exp(al_ref[...])

        def chunk(c, carry):
            rows = pl.ds(pl.multiple_of(c * CHUNK, CHUNK), CHUNK)
            for gi in range(ng):
                fo, co, go = FW_B * gi, 768 * gi, SSD_GW * gi
                xbc = pb_ref[rows, fo:fo + 768]
                taps = _conv_taps(jnp.concatenate([prev_ref[:, co:co + 768], xbc], axis=0), CHUNK)
                prev_ref[:, co:co + 768] = xbc[CHUNK - 8:CHUNK]
                pre = cb_ref[:, co:co + 768]
                for k in range(4):
                    pre = pre + taps[k] * cw_ref[k:k + 1, co:co + 768]
                pre_ref[rows, co:co + 768] = pre
                act = pre * _sigmoid(pre)
                dt = _softplus(pb_ref[rows, fo + 768:fo + 1280] + dtb_ref[:, go:go + 512])
                acs = _dot01_l(tri, dt * a_neg[:, go:go + 512], 3)
                dt_ref[rows, go:go + 512] = dt
                acs_ref[rows, go:go + 512] = acs
                ht = ht_ref[gi]
                hp_ref[c, :, go:go + 512] = ht
                y, ht_new = _ssd_chunk_fwd(act[:, 0:512], act[:, 512:640], act[:, 640:768], dt, acs,
                                           ds_ref[:, go:go + 512], ht, masks)
                y_ref[rows, go:go + 512] = y
                ht_ref[gi] = ht_new
                zb = pb_ref[rows, fo + 1280:fo + 1792]
                hh = y * (zb * _sigmoid(zb))
                rr = lax.rsqrt(jnp.mean(hh * hh, axis=-1, keepdims=True) + EPS)
                yb_ref[rows, go:go + 512] = _c(hh * rr * nw_ref[:, go:go + 512])
            return carry

        lax.fori_loop(0, nch, chunk, 0)

    gvec = lambda w: pl.BlockSpec((1, ng * w), lambda g, i: (0, g))
    return pl.pallas_call(
        body, name="ssd_fwd",
        grid=(SSD_GROUPS // ng, nt),
        in_specs=[pl.BlockSpec((t, ng * FW_B), lambda g, i: (i, g)),
                  pl.BlockSpec((8, ng * FW_B), lambda g, i: (jnp.maximum(i * (t // 8) - 1, 0), g)),
                  gvec(512), gvec(512), gvec(512),
                  pl.BlockSpec((4, ng * 768), lambda g, i: (0, g)), gvec(768), gvec(512)],
        out_specs=[pl.BlockSpec((t, ng * SSD_GW), lambda g, i: (i, g)), pl.BlockSpec((t, ng * SSD_GW), lambda g, i: (i, g)),
                   pl.BlockSpec((nch, SSD_STATE, ng * SSD_GW), lambda g, i: (i, 0, g)),
                   pl.BlockSpec((t, ng * 768), lambda g, i: (i, g)),
                   pl.BlockSpec((t, ng * SSD_GW), lambda g, i: (i, g)), pl.BlockSpec((t, ng * SSD_GW), lambda g, i: (i, g))],
        out_shape=[jax.ShapeDtypeStruct((s, D_MODEL), F32), jax.ShapeDtypeStruct((s, D_MODEL), MXU_DTYPE),
                   jax.ShapeDtypeStruct((s // CHUNK, SSD_STATE, D_MODEL), F32),
                   jax.ShapeDtypeStruct((s, SSD_GROUPS * 768), F32),
                   jax.ShapeDtypeStruct((s, D_MODEL), F32), jax.ShapeDtypeStruct((s, D_MODEL), F32)],
        scratch_shapes=[pltpu.VMEM((ng, SSD_STATE, SSD_GW), F32), pltpu.VMEM((8, ng * 768), F32)],
        compiler_params=_cparams(("parallel", "arbitrary"), ng * (2 * t * FW_B * 4 + 16 * t * SSD_GW * 4) + 16 * 1024 * 1024),
    )(proj_b, proj_b, dtb, alog, dsk, cw, cb, nw)


def _ssd_bwd_call(proj_b, pre_all, dt_all, acs_all, dyb, y, hprev, dtb, alog, dsk, cw, nw, t, ng):
    s = proj_b.shape[0]
    nt, nch = s // t, t // CHUNK

    def body(pb_ref, pre_ref, dt_ref, acs_ref, dyb_ref, y_ref, hp_ref, dtb_ref, al_ref, ds_ref, cw_ref, nw_ref,
             dpb_ref, a512_ref, a768_ref, dht_ref, nxt_ref, q_ref, p1_ref):
        i = pl.program_id(1)

        @pl.when(i == 0)
        def _():
            dht_ref[...] = jnp.zeros_like(dht_ref)
            nxt_ref[...] = jnp.zeros_like(nxt_ref)
            a512_ref[...] = jnp.zeros_like(a512_ref)
            a768_ref[...] = jnp.zeros_like(a768_ref)

        _, trit = _cumsum_mats(t)
        a_neg = -jnp.exp(al_ref[...])
        masks = _ssd_masks()

        def chunk(cc, carry):
            c = nch - 1 - cc
            rows = pl.ds(pl.multiple_of(c * CHUNK, CHUNK), CHUNK)
            for gi in range(ng):
                fo, co, go, bo = FW_B * gi, 768 * gi, SSD_GW * gi, BW_B * gi
                pre = pre_ref[rows, co:co + 768]
                sp = _sigmoid(pre)
                act = pre * sp
                zb = pb_ref[rows, fo + 1280:fo + 1792]
                yv = y_ref[rows, go:go + 512]
                sgz = _sigmoid(zb)
                sz = zb * sgz
                hh = yv * sz
                rr = lax.rsqrt(jnp.mean(hh * hh, axis=-1, keepdims=True) + EPS)
                dyb = dyb_ref[rows, go:go + 512].astype(F32)
                a512_ref[gi, 0] += _fold8(dyb * (hh * rr))
                tt = dyb * nw_ref[:, go:go + 512]
                dhh = rr * tt - hh * (rr * rr * rr) * jnp.mean(hh * tt, axis=-1, keepdims=True)
                dpb_ref[rows, bo:bo + 512] = _c(dhh * yv * (sgz * (1.0 + zb * (1.0 - sgz))))
                dxs, d_b, d_c, dht_prev, dyxs, qq, p1 = _ssd_chunk_bwd(
                    act[:, 0:512], act[:, 512:640], act[:, 640:768], dt_ref[rows, go:go + 512], acs_ref[rows, go:go + 512],
                    ds_ref[:, go:go + 512], hp_ref[c, :, go:go + 512], dht_ref[gi], dhh * sz, masks)
                dht_ref[gi] = dht_prev
                q_ref[rows, go:go + 512] = qq
                p1_ref[rows, go:go + 512] = p1
                a512_ref[gi, 1] += _fold8(dyxs)
                dpre = jnp.concatenate([dxs, d_b, d_c], axis=1) * (sp * (1.0 + pre * (1.0 - sp)))
                xbc = pb_ref[rows, fo:fo + 768]
                a768_ref[gi, 4] += _fold8(dpre)
                a768_ref[gi, 3] += _fold8(dpre * xbc)
                dpad = jnp.concatenate([dpre, nxt_ref[:, co:co + 768]], axis=0)
                dx = dpre * cw_ref[3:4, co:co + 768]
                for k in range(3):
                    d_k = pltpu.roll(dpad, CHUNK + 8 - (3 - k), 0)[0:CHUNK]
                    dx = dx + d_k * cw_ref[k:k + 1, co:co + 768]
                    a768_ref[gi, k] += _fold8(d_k * xbc)
                nxt_ref[:, co:co + 768] = dpre[0:8]
                dpb_ref[rows, bo + 512:bo + 1280] = _c(dx)
            return carry

        lax.fori_loop(0, nch, chunk, 0)

        rsel = _c(jnp.where(lax.shift_right_logical(_iota((SSD_GW, 128), 0), 6) == _iota((SSD_GW, 128), 1), 1.0, 0.0))
        for gi in range(ng):
            fo, go, bo = FW_B * gi, SSD_GW * gi, BW_B * gi
            sig = _sigmoid(pb_ref[:, fo + 768:fo + 1280] + dtb_ref[:, go:go + 512])
            ddtr, dadt = _ssd_finish_dt(q_ref[:, go:go + 512], p1_ref[:, go:go + 512], dt_ref[:, go:go + 512], sig,
                                        a_neg[:, go:go + 512], trit, masks[3])
            dpb_ref[:, bo + 1280:bo + 1408] = _c(_dot01_r(ddtr, rsel, 2))
            a512_ref[gi, 2] += _fold8(dadt)
            a512_ref[gi, 3] += _fold8(ddtr)

    gvec = lambda w: pl.BlockSpec((1, ng * w), lambda g, i: (0, g))
    rev = lambda w: pl.BlockSpec((t, ng * w), lambda g, i: (nt - 1 - i, g))
    return pl.pallas_call(
        body, name="ssd_bwd",
        grid=(SSD_GROUPS // ng, nt),
        in_specs=[rev(FW_B), rev(768), rev(SSD_GW), rev(SSD_GW), rev(SSD_GW), rev(SSD_GW),
                  pl.BlockSpec((nch, SSD_STATE, ng * SSD_GW), lambda g, i: (nt - 1 - i, 0, g)),
                  gvec(512), gvec(512), gvec(512),
                  pl.BlockSpec((4, ng * 768), lambda g, i: (0, g)), gvec(512)],
        out_specs=[rev(BW_B),
                   pl.BlockSpec((ng, 4, 8, 512), lambda g, i: (g, 0, 0, 0)),
                   pl.BlockSpec((ng, 5, 8, 768), lambda g, i: (g, 0, 0, 0))],
        out_shape=[jax.ShapeDtypeStruct((s, SSD_GROUPS * BW_B), MXU_DTYPE),
                   jax.ShapeDtypeStruct((SSD_GROUPS, 4, 8, 512), F32),
                   jax.ShapeDtypeStruct((SSD_GROUPS, 5, 8, 768), F32)],
        scratch_shapes=[pltpu.VMEM((ng, SSD_STATE, SSD_GW), F32), pltpu.VMEM((8, ng * 768), F32),
                        pltpu.VMEM((t, ng * SSD_GW), F32), pltpu.VMEM((t, ng * SSD_GW), F32)],
        compiler_params=_cparams(("parallel", "arbitrary"), ng * (2 * t * FW_B * 4 + 18 * t * SSD_GW * 4) + 16 * 1024 * 1024),
    )(proj_b, pre_all, dt_all, acs_all, dyb, y, hprev, dtb, alog, dsk, cw, nw)


def _rows_call(body, ins, outs, tr, name):
    r = ins[0].shape[0]
    spec = lambda a: pl.BlockSpec((tr, a.shape[1]), lambda i: (i, 0))
    est = 2 * tr * sum(a.shape[1] * jnp.dtype(a.dtype).itemsize for a in list(ins) + list(outs))
    return pl.pallas_call(
        body, name=name, grid=(r // tr,),
        in_specs=[spec(a) for a in ins], out_specs=[spec(o) for o in outs], out_shape=list(outs),
        compiler_params=_cparams(("parallel",), est),
    )(*ins)


def _add_pair(a, b, tr, name):
    def body(a_ref, b_ref, o_ref):
        o_ref[...] = a_ref[...] + b_ref[...]

    return _rows_call(body, [a, b], [jax.ShapeDtypeStruct(a.shape, F32)], tr, name)[0]


def _rs_add(p, sib, place, tr, name):
    _, r, c = p.shape
    half = r // 2
    nb = half // tr

    def body(pl_ref, p_ref, s_ref, b_ref, own_ref):
        v = p_ref[0] + s_ref[0]
        b_ref[0] = v.astype(jnp.bfloat16)

        @pl.when(pl.program_id(1) == pl_ref[0])
        def _():
            own_ref[...] = v

    return pl.pallas_call(
        body, name=name,
        grid_spec=pltpu.PrefetchScalarGridSpec(
            num_scalar_prefetch=1, grid=(nb, 4),
            in_specs=[pl.BlockSpec((1, tr, c), lambda i, k, pr: (k, pr[1] * nb + i, 0)),
                      pl.BlockSpec((1, tr, c), lambda i, k, pr: (k, i, 0))],
            out_specs=[pl.BlockSpec((1, tr, c), lambda i, k, pr: (k, i, 0)),
                       pl.BlockSpec((tr, c), lambda i, k, pr: (i, 0))]),
        out_shape=[jax.ShapeDtypeStruct((4, half, c), jnp.bfloat16), jax.ShapeDtypeStruct((half, c), F32)],
        compiler_params=_cparams(("parallel", "arbitrary"), 2 * tr * c * 14),
    )(place, p, sib)


def _sum_own_recv(own, recv, tr, name):
    r, c = own.shape

    def body(o_ref, r_ref, out_ref):
        v = o_ref[...]
        for j in range(3):
            v = v + r_ref[j].astype(F32)
        out_ref[...] = v

    return pl.pallas_call(
        body, name=name, grid=(r // tr,),
        in_specs=[pl.BlockSpec((tr, c), lambda i: (i, 0)), pl.BlockSpec((3, tr, c), lambda i: (0, i, 0))],
        out_specs=pl.BlockSpec((tr, c), lambda i: (i, 0)),
        out_shape=jax.ShapeDtypeStruct((r, c), F32),
        compiler_params=_cparams(("parallel",), 2 * tr * c * 14),
    )(own, recv)


def _sum_slots(stack, name):
    n, r, w = stack.shape

    def body(a_ref, out_ref):
        v = a_ref[0]
        for k in range(1, n):
            v = v + a_ref[k]
        out_ref[...] = v

    return pl.pallas_call(
        body, name=name, grid=(1,),
        in_specs=[pl.BlockSpec((n, r, w), lambda i: (0, 0, 0))],
        out_specs=pl.BlockSpec((r, w), lambda i: (0, 0)),
        out_shape=jax.ShapeDtypeStruct((r, w), F32),
        compiler_params=_cparams(("arbitrary",), 2 * (n + 1) * r * w * 4),
    )(stack)


def _adamw(w, g, m, v, tr, name):
    def body(w_ref, g_ref, m_ref, v_ref, d_ref, nm_ref, nv_ref):
        d_ref[...], nm_ref[...], nv_ref[...] = _adam_math(w_ref[...], g_ref[...], m_ref[...], v_ref[...])

    o = jax.ShapeDtypeStruct(w.shape, F32)
    return _rows_call(body, [w, g, m, v], [o, o, o], tr, name)


def _adam_math(w, g, m, v):
    nm = ADAM_B1 * m + (1.0 - ADAM_B1) * g
    nv = ADAM_B2 * v + (1.0 - ADAM_B2) * (g * g)
    m_hat = nm / (1.0 - ADAM_B1 ** ADAM_STEP)
    v_hat = nv / (1.0 - ADAM_B2 ** ADAM_STEP)
    return -ADAM_LR * (m_hat / (jnp.sqrt(v_hat) + ADAM_EPS) + ADAM_WD * w), nm, nv


def _adamw_halves(w, g_own, g_sib, m, v, place, tr, name):
    r, c = w.shape

    def body(pl_ref, w_ref, go_ref, gs_ref, m_ref, v_ref, g_ref, d_ref, nm_ref, nv_ref):
        first = pl_ref[1] == 0
        own, sib = go_ref[...], gs_ref[...]
        g = jnp.concatenate([jnp.where(first, own, sib), jnp.where(first, sib, own)], axis=1)
        g_ref[...] = g
        d_ref[...], nm_ref[...], nv_ref[...] = _adam_math(w_ref[...], g, m_ref[...], v_ref[...])

    full = pl.BlockSpec((tr, c), lambda i, pr: (i, 0))
    half = pl.BlockSpec((tr, c // 2), lambda i, pr: (i, 0))
    o = jax.ShapeDtypeStruct((r, c), F32)
    return pl.pallas_call(
        body, name=name,
        grid_spec=pltpu.PrefetchScalarGridSpec(num_scalar_prefetch=1, grid=(r // tr,),
                                               in_specs=[full, half, half, full, full], out_specs=[full] * 4),
        out_shape=[o] * 4,
        compiler_params=_cparams(("parallel",), 2 * tr * c * 4 * 8),
    )(place, w, g_own, g_sib, m, v)


ANY = pl.BlockSpec(memory_space=pl.ANY)


def _place():
    x, y, c = lax.axis_index("x"), lax.axis_index("y"), lax.axis_index("c")
    others = [(1 - x, y), (x, 1 - y), (1 - x, 1 - y)]
    return x, y, c, 2 * x + y, others


def _remote(src, dst, send, recv, k, to):
    return pltpu.make_async_remote_copy(src_ref=src, dst_ref=dst, send_sem=send.at[k], recv_sem=recv.at[k],
                                        device_id=to, device_id_type=MESH)


def _gather_call(win_b, wout_b, cw8):
    big = [win_b, wout_b]

    def body(win, wout, cw, g_in, g_out, g_cw, send, recv):
        x, y, c, me, others = _place()
        sib = (x, y, 1 - c)
        started = []
        for a, (src, dst) in enumerate(((win, g_in), (wout, g_out))):
            half = src.shape[0] // 2
            mine = pl.ds(c * half, half)
            for j, chip in enumerate(others):
                cp = _remote(src.at[mine], dst.at[me, mine], send, recv, 6 * a + j, (*chip, c))
                cp.start()
                started.append(cp)
        for j, chip in enumerate(others):
            cp = _remote(cw, g_cw.at[me], send, recv, 12 + j, (*chip, c))
            cp.start()
            started.append(cp)
        for a, dst in enumerate((g_in, g_out)):
            half = dst.shape[1] // 2
            mine = pl.ds(c * half, half)
            for j, chip in enumerate(others):
                kj = 2 * chip[0] + chip[1]
                _remote(dst.at[kj, mine], dst.at[kj, mine], send, recv, 6 * a + j, (*chip, c)).wait_recv()
                cp = _remote(dst.at[kj, mine], dst.at[kj, mine], send, recv, 6 * a + 3 + j, sib)
                cp.start()
                started.append(cp)
        for a, dst in enumerate((g_in, g_out)):
            half = dst.shape[1] // 2
            theirs = pl.ds((1 - c) * half, half)
            for j, chip in enumerate(others):
                kj = 2 * chip[0] + chip[1]
                _remote(dst.at[kj, theirs], dst.at[kj, theirs], send, recv, 6 * a + 3 + j, sib).wait_recv()
        for j, chip in enumerate(others):
            kj = 2 * chip[0] + chip[1]
            _remote(cw, g_cw.at[kj], send, recv, 12 + j, (*chip, c)).wait_recv()
        for cp in started:
            cp.wait_send()

    outs = [jax.ShapeDtypeStruct((4,) + a.shape, a.dtype) for a in (win_b, wout_b, cw8)]
    return pl.pallas_call(
        body, name="gather_weights",
        in_specs=[ANY, ANY, ANY], out_specs=[ANY, ANY, ANY], out_shape=outs,
        scratch_shapes=[pltpu.SemaphoreType.DMA((15,)), pltpu.SemaphoreType.DMA((15,))],
    )(win_b, wout_b, cw8)


def _rs_sibling_call(p_in, p_out, vsmall):
    def body(pin, pout, vs, sib_in, sib_out, sib_v, send, recv):
        x, y, c, me, others = _place()
        sib = (x, y, 1 - c)
        cps = []
        for a, (p, sb) in enumerate(((pin, sib_in), (pout, sib_out))):
            half = p.shape[1] // 2
            cps.append(_remote(p.at[:, pl.ds((1 - c) * half, half)], sb, send, recv, a, sib))
        cps.append(_remote(vs, sib_v, send, recv, 2, sib))
        for cp in cps:
            cp.start()
        for cp in cps:
            cp.wait_recv()
        for cp in cps:
            cp.wait_send()

    def halves(p):
        return jax.ShapeDtypeStruct((4, p.shape[1] // 2, p.shape[2]), p.dtype)

    outs = [halves(p_in), halves(p_out), jax.ShapeDtypeStruct(vsmall.shape, vsmall.dtype)]
    return pl.pallas_call(
        body, name="rs_sibling",
        in_specs=[ANY] * 3, out_specs=[ANY] * 3, out_shape=outs,
        scratch_shapes=[pltpu.SemaphoreType.DMA((3,)), pltpu.SemaphoreType.DMA((3,))],
    )(p_in, p_out, vsmall)


def _rs_join_call(f_in, f_out, nw8):
    def body(fin, fout, nw, sib_in, full_out, all_nw, send, recv):
        x, y, c, me, others = _place()
        sib = (x, y, 1 - c)
        half = fout.shape[0]
        cps = [_remote(fin, sib_in, send, recv, 0, sib),
               _remote(fout, full_out.at[pl.ds(c * half, half)], send, recv, 1, sib)]
        mine = 4 * x + 2 * y + c
        peers = []
        for r in range(1, 8):
            px, py, pc = (1 - x if r & 4 else x), (1 - y if r & 2 else y), (1 - c if r & 1 else c)
            peers.append((r, (px, py, pc), 4 * px + 2 * py + pc))
            cps.append(_remote(nw, all_nw.at[mine], send, recv, 1 + r, (px, py, pc)))
        for cp in cps:
            cp.start()
        cps[0].wait_recv()
        _remote(fout, full_out.at[pl.ds((1 - c) * half, half)], send, recv, 1, sib).wait_recv()
        for r, peer, idx in peers:
            _remote(nw, all_nw.at[idx], send, recv, 1 + r, peer).wait_recv()
        for cp in cps:
            cp.wait_send()

    outs = [jax.ShapeDtypeStruct(f_in.shape, F32), jax.ShapeDtypeStruct((2 * f_out.shape[0], f_out.shape[1]), F32),
            jax.ShapeDtypeStruct((8,) + nw8.shape, F32)]
    return pl.pallas_call(
        body, name="rs_join",
        in_specs=[ANY] * 3, out_specs=[ANY] * 3, out_shape=outs,
        scratch_shapes=[pltpu.SemaphoreType.DMA((9,)), pltpu.SemaphoreType.DMA((9,))],
    )(f_in, f_out, nw8)


def _pack(arrs):
    parts = []
    for a in arrs:
        f = a.reshape(-1).astype(F32)
        pad = (-f.shape[0]) % 1024
        parts.append(jnp.pad(f, (0, pad)).reshape(-1, 128))
    return jnp.concatenate(parts, axis=0)


def _unpack(packed, shapes):
    out, row = [], 0
    for shp in shapes:
        n = 1
        for d in shp:
            n *= d
        rows = (n + 1023) // 1024 * 8
        out.append(packed[row:row + rows].reshape(-1)[:n].reshape(shp))
        row += rows
    return out


def _expand_heads(v32):
    return jnp.repeat(v32.reshape(32), HEADDIM).reshape(1, D_MODEL)


def kernel(x, norm_w, w_in, gate_b, sgu_norm_g, sgu_norm_b, sgu_w, sgu_b, conv_w, conv_b, dt_bias, A_log, D_skip, ssd_norm_w, w_out, final_norm_w, loss_target, m_norm_w, m_w_in, m_gate_b, m_sgu_norm_g, m_sgu_norm_b, m_sgu_w, m_sgu_b, m_conv_w, m_conv_b, m_dt_bias, m_A_log, m_D_skip, m_ssd_norm_w, m_w_out, m_final_norm_w, v_norm_w, v_w_in, v_gate_b, v_sgu_norm_g, v_sgu_norm_b, v_sgu_w, v_sgu_b, v_conv_w, v_conv_b, v_dt_bias, v_A_log, v_D_skip, v_ssd_norm_w, v_w_out, v_final_norm_w):
    s = x.shape[1]
    x2 = x.reshape(s, D_MODEL)
    tgt = loss_target.reshape(s, D_MODEL)
    t_ssd, t_tok, t_out, t_row = min(T_SSD, s), min(T_TOK, s), min(T_OUT, s), min(T_ROW, s)
    tm_mm, tk_dw = min(TM_MM, s), min(TK_DW, s)
    chip = 2 * lax.axis_index("x") + lax.axis_index("y")

    cw8 = jnp.pad(conv_w[0], ((0, 4), (0, 0)))
    win_b, wout_b = _c(w_in[0]), _c(w_out[0])
    g_in, g_out, g_cw = _gather_call(win_b, wout_b, cw8)
    g_in = lax.dynamic_update_index_in_dim(g_in, win_b, chip, 0)
    g_out = lax.dynamic_update_index_in_dim(g_out, wout_b, chip, 0)
    g_cw = lax.dynamic_update_index_in_dim(g_cw, cw8, chip, 0)
    wref = jnp.transpose(g_in, (1, 0, 2)).reshape(D_MODEL, IN_W)
    w_out_full = g_out.reshape(D_MODEL, D_MODEL)
    conv_w_full = jnp.transpose(g_cw[:, 0:4, :], (1, 0, 2)).reshape(4, 3072)

    w_a = jnp.concatenate([wref[:, 0:6144], wref[:, 11296:15392]], axis=1)
    fw, bw = [], []
    for g in range(SSD_GROUPS):
        xs_g = wref[:, 8192 + 512 * g:8192 + 512 * g + 512]
        b_g = wref[:, 10240 + 128 * g:10240 + 128 * g + 128]
        c_g = wref[:, 10752 + 128 * g:10752 + 128 * g + 128]
        zb_g = wref[:, 6144 + 512 * g:6144 + 512 * g + 512]
        dt_g = wref[:, 11264 + 8 * g:11264 + 8 * g + 8]
        fw += [xs_g, b_g, c_g, jnp.repeat(dt_g, HEADDIM, axis=1), zb_g]
        bw += [zb_g, xs_g, b_g, c_g, jnp.pad(dt_g, ((0, 0), (0, 120)))]
    w_b = jnp.concatenate(fw, axis=1)
    wt_a = w_a.T
    wt_b = jnp.concatenate(bw, axis=1).T

    def group_cols(full_xs, full_bc):
        parts = []
        for g in range(SSD_GROUPS):
            parts += [full_xs[:, 512 * g:512 * g + 512], full_bc[:, 128 * g:128 * g + 128], full_bc[:, 512 + 128 * g:512 + 128 * g + 128]]
        return jnp.concatenate(parts, axis=1)

    cw_g = group_cols(conv_w_full[:, 0:2048], conv_w_full[:, 2048:3072])
    cb_g = group_cols(conv_b[:, 0:2048], conv_b[:, 2048:3072])
    dtb_e, alog_e, dsk_e = _expand_heads(dt_bias), _expand_heads(A_log), _expand_heads(D_skip)

    pos_chunk = jnp.arange(SGU_BLOCK) // CHUNK
    smask = pos_chunk[None, :] <= pos_chunk[:, None]
    wm_f = jnp.where(smask[None], sgu_w[0], 0.0)
    wm = _c(wm_f)
    wmt = _c(jnp.transpose(wm_f, (0, 2, 1)))
    bias_full = jnp.repeat(sgu_b[0].T, D_MODEL // SGU_GROUPS, axis=1)
    fnw = final_norm_w.reshape(1, D_MODEL)

    xn, xnt = _norm_call(x2, norm_w, t_row)
    proj_a = _mm(xn, w_a, tm=tm_mm, tn=1024, tk=D_MODEL, name="in_proj_a", out_dtype=MXU_DTYPE)
    proj_b = _mm(xn, w_b, tm=tm_mm, tn=1024, tk=D_MODEL, name="in_proj_b")
    y_ssd, y_b, hprev, pre_all, dt_all, acs_all = _ssd_fwd_call(proj_b, dtb_e, alog_e, dsk_e, cw_g, cb_g, ssd_norm_w, t_ssd, NG_SSD)
    y_a, merged, merged_t = _tok_fwd_call(proj_a, y_b, gate_b, sgu_norm_g, sgu_norm_b, wm, bias_full, t_tok)
    dh, dh_b, dmerged, loss_t, dfw8 = _out_call(merged, x2, tgt, w_out_full, fnw, t_out)

    dproj_a, dy_b, dgb8, dgam8, dbeta8, dbfull, dws = _tok_bwd_call(
        proj_a, dmerged, y_a, y_b, gate_b, sgu_norm_g, sgu_norm_b, wm, wmt, bias_full, t_tok)
    dproj_b, a512, a768 = _ssd_bwd_call(proj_b, pre_all, dt_all, acs_all, dy_b, y_ssd, hprev, dtb_e, alog_e, dsk_e, cw_g,
                                        ssd_norm_w, t_ssd, NG_SSD)
    dw_a = _mm(xnt, dproj_a, tm=D_MODEL, tn=1024, tk=tk_dw, name="dw_in_a")
    dw_b = _mm(xnt, dproj_b, tm=D_MODEL, tn=BW_B, tk=tk_dw, name="dw_in_b")
    dw_out_p = _mm(merged_t, dh_b, tm=D_MODEL, tn=1024, tk=tk_dw, name="dw_out")

    gb = lambda a, b: jnp.concatenate([dw_b[:, BW_B * g + a:BW_B * g + b] for g in range(SSD_GROUPS)], axis=1)
    dw_ref = jnp.concatenate([dw_a[:, 0:6144], gb(0, 512), gb(512, 1024), gb(1024, 1152), gb(1152, 1280),
                              gb(1280, 1288), dw_a[:, 6144:10240]], axis=1)
    p_in = jnp.transpose(dw_ref.reshape(D_MODEL, 4, SHARD_W), (1, 0, 2))
    p_out = dw_out_p.reshape(4, D_MODEL // 4, D_MODEL)

    s512 = jnp.sum(a512, axis=2)
    heads = lambda v: jnp.sum(v.reshape(32, HEADDIM), axis=1).reshape(1, 32)
    d_ssd_nw = s512[:, 0].reshape(1, D_MODEL)
    d_dskip = heads(s512[:, 1].reshape(D_MODEL))
    d_alog = heads(s512[:, 2].reshape(D_MODEL)) * (1.0 / HEADDIM) * (-jnp.exp(A_log))
    d_dtb = heads(s512[:, 3].reshape(D_MODEL))
    s768 = jnp.sum(a768, axis=2)
    ungroup = lambda v: jnp.concatenate([v[g, :, 0:512] for g in range(4)] + [v[g, :, 512:640] for g in range(4)]
                                        + [v[g, :, 640:768] for g in range(4)], axis=1)
    d_cw = ungroup(s768[:, 0:4])
    d_cb = ungroup(s768[:, 4:5])
    d_sgu_b = jnp.sum(dbfull.reshape(128, SGU_GROUPS, 128), axis=2).T.reshape(1, SGU_GROUPS, 128)
    d_sgu_w = jnp.where(smask[None], dws, 0.0).reshape(1, SGU_GROUPS, 128, 128)
    fold = lambda a8: jnp.sum(a8, axis=0, keepdims=True)
    small_local = [fold(dgb8), fold(dgam8), fold(dbeta8), d_sgu_w, d_sgu_b, d_cw, d_cb,
                   d_dtb, d_alog, d_dskip, d_ssd_nw, fold(dfw8).reshape(D_MODEL), jnp.sum(loss_t[:, 0, 0]).reshape(1)]
    small_shapes = [a.shape for a in small_local]
    v_local = _pack(small_local)

    core = lax.axis_index("c")
    place = jnp.stack([chip, core]).astype(jnp.int32)
    hr_i, hr_o = D_MODEL // 2, D_MODEL // 8
    sib_i, sib_o, sib_v = _rs_sibling_call(p_in, p_out, v_local)
    s1b_i, o_i = _rs_add(p_in, sib_i, place, 256, "rs_add_in")
    s1b_o, o_o = _rs_add(p_out, sib_o, place, 256, "rs_add_out")
    chip_v = _add_pair(v_local, sib_v, v_local.shape[0], "ar_add_small")
    dxn, r_i, r_o, abs_v = _dx_rs_call(dproj_a, wt_a, dproj_b, wt_b, s1b_i, s1b_o, chip_v, tm=tm_mm)
    grad_x, dnw8 = _gradx_call(x2, dxn, dh, norm_w, t_row)
    abs_v = lax.dynamic_update_index_in_dim(abs_v, chip_v, chip, 0)
    f_i = _sum_own_recv(o_i, r_i, 256, "rs_sum_in")
    f_o = _sum_own_recv(o_o, r_o, 256, "rs_sum_out")
    sib_f_i, g_w_out, all_nw = _rs_join_call(f_i, f_o, dnw8)
    g_w_out = lax.dynamic_update_slice_in_dim(g_w_out, f_o, core * hr_o, axis=0)
    all_nw = lax.dynamic_update_index_in_dim(all_nw, dnw8, 2 * chip + core, 0)
    g_nw = fold(_sum_slots(all_nw, "ar_sum_norm_w"))
    total_v = _sum_slots(abs_v, "ar_sum_small")
    (g_gb, g_gam, g_beta, g_sw, g_sb, g_cw_full, g_cb, g_dtb, g_alog, g_dsk, g_snw, g_fnw, loss1) = _unpack(total_v, small_shapes)
    g_cw_shard = lax.dynamic_slice(g_cw_full, (0, chip * 768), (4, 768)).reshape(1, 4, 768)
    loss = loss1.reshape(())

    g_w_in, d_win, nm_win, nv_win = (a.T for a in _adamw_halves(w_in[0].T, f_i.T, sib_f_i.T, m_w_in[0].T, v_w_in[0].T,
                                                                place, 296, "adamw_w_in"))
    d_wout, nm_wout, nv_wout = _adamw(w_out[0], g_w_out, m_w_out[0], v_w_out[0], 128, "adamw_w_out")
    small_w = [norm_w, gate_b, sgu_norm_g, sgu_norm_b, sgu_w, sgu_b, conv_w, conv_b, dt_bias, A_log, D_skip, ssd_norm_w, final_norm_w]
    small_m = [m_norm_w, m_gate_b, m_sgu_norm_g, m_sgu_norm_b, m_sgu_w, m_sgu_b, m_conv_w, m_conv_b, m_dt_bias, m_A_log, m_D_skip, m_ssd_norm_w, m_final_norm_w]
    small_v = [v_norm_w, v_gate_b, v_sgu_norm_g, v_sgu_norm_b, v_sgu_w, v_sgu_b, v_conv_w, v_conv_b, v_dt_bias, v_A_log, v_D_skip, v_ssd_norm_w, v_final_norm_w]
    small_g = [g_nw, g_gb, g_gam, g_beta, g_sw, g_sb, g_cw_shard, g_cb, g_dtb, g_alog, g_dsk, g_snw, g_fnw]
    shapes_w = [a.shape for a in small_w]
    small_g = [a.reshape(shp) for a, shp in zip(small_g, shapes_w)]
    pw = _pack(small_w)
    pd, pm, pv = _adamw(pw, _pack(small_g), _pack(small_m), _pack(small_v), pw.shape[0], "adamw_small")
    d_small, nm_small, nv_small = _unpack(pd, shapes_w), _unpack(pm, shapes_w), _unpack(pv, shapes_w)

    def with_big(small, win, wout):
        o = list(small)
        return o[0:1] + [win.reshape(1, D_MODEL, SHARD_W)] + o[1:12] + [wout.reshape(1, D_MODEL // 4, D_MODEL)] + o[12:13]

    grads = with_big(small_g, g_w_in, g_w_out)
    deltas = with_big(d_small, d_win, d_wout)
    new_m = with_big(nm_small, nm_win, nm_wout)
    new_v = with_big(nv_small, nv_win, nv_wout)
    return (loss, grad_x.reshape(1, s, D_MODEL), *grads, *deltas, *new_m, *new_v)
```

```python
import functools

import jax
import jax.numpy as jnp
from jax import lax
from jax.experimental import pallas as pl
from jax.experimental.pallas import tpu as pltpu

F32 = jnp.float32
MXU_DTYPE = jnp.bfloat16

D_MODEL = 2048
EPS = 1e-5
CHUNK = 64
SGU_BLOCK = 128
SGU_GROUPS = 16
SSD_GROUPS = 4
SSD_GW = 512
SSD_STATE = 128
HEADDIM = 64
IN_W = 15392
SHARD_W = IN_W // 4
FW_B = 1792
BW_B = 1408
NA = 10240

ADAM_LR = 0.001
ADAM_B1 = 0.9
ADAM_B2 = 0.999
ADAM_EPS = 1e-08
ADAM_WD = 0.01
ADAM_STEP = 10

T_SSD = 256
NG_SSD = 2
T_TOK = 128
T_OUT = 256
T_ROW = 512
TM_MM = 1024
TK_DW = 1024
VMEM_CAP = 60 * 1024 * 1024
MESH = pl.DeviceIdType.MESH


def _cparams(sem, est_bytes):
    lim = int(min(VMEM_CAP, max(32 * 1024 * 1024, est_bytes + 12 * 1024 * 1024)))
    return pltpu.CompilerParams(dimension_semantics=sem, vmem_limit_bytes=lim)


def _c(x):
    return x.astype(MXU_DTYPE)


def _dot(a, b):
    return jnp.dot(a, b, preferred_element_type=F32)


def _dot_nt(a, b):
    return lax.dot_general(a, b, (((1,), (1,)), ((), ())), preferred_element_type=F32)


def _dot_tn(a, b):
    return lax.dot_general(a, b, (((0,), (0,)), ((), ())), preferred_element_type=F32)


def _split(x, n):
    parts, r = [], x
    for _ in range(n):
        p = _c(r)
        parts.append(p)
        r = r - p.astype(F32)
    return parts


def _dot01_l(m01, x, n):
    acc = None
    for p in _split(x, n):
        t = _dot(m01, p)
        acc = t if acc is None else acc + t
    return acc


def _dot01_r(x, m01, n):
    acc = None
    for p in _split(x, n):
        t = _dot(p, m01)
        acc = t if acc is None else acc + t
    return acc


def _sigmoid(x):
    return 1.0 / (1.0 + jnp.exp(-x))


def _fold8(x):
    r, w = x.shape
    return jnp.sum(x.reshape(r // 8, 8, w), axis=0)


def _iota(shape, dim):
    return lax.broadcasted_iota(jnp.int32, shape, dim)


def _ssd_masks():
    l64 = _iota((CHUNK, SSD_GW), 0)
    s64 = jnp.bitwise_and(_iota((CHUNK, SSD_GW), 1), CHUNK - 1)
    diag = l64 == s64
    causal = l64 >= s64
    row_last = l64 == CHUNK - 1
    r4 = lax.shift_right_logical(_iota((256, 256), 0), 6)
    c4 = lax.shift_right_logical(_iota((256, 256), 1), 6)
    mask4 = r4 == c4
    return diag, causal, row_last, mask4


def _cumsum_mats(t):
    r, c = _iota((t, t), 0), _iota((t, t), 1)
    same = lax.shift_right_logical(r, 6) == lax.shift_right_logical(c, 6)
    tri = _c(jnp.where(same, jnp.where(c <= r, 1.0, 0.0), 0.0))
    trit = _c(jnp.where(same, jnp.where(c >= r, 1.0, 0.0), 0.0))
    return tri, trit


def _ssd_common(xs, bm, cm, dt, acs, masks):
    diag, causal, row_last, mask4 = masks
    row_e = jnp.sum(jnp.where(diag, acs, 0.0), axis=0, keepdims=True)
    seg = acs - row_e
    lm = jnp.exp(jnp.where(causal, seg, -1e30))
    bb, cb = _c(bm), _c(cm)
    brep = jnp.concatenate([bb] * 8, axis=0)
    cbrep = _dot_nt(cb, brep)
    m = cbrep * lm
    xdt = xs * dt
    acs_last = jnp.sum(jnp.where(row_last, acs, 0.0), axis=0, keepdims=True)
    dec = jnp.exp(acs_last - acs)
    eacs = jnp.exp(acs)
    cd = jnp.exp(acs_last)
    return dict(lm=lm, bb=bb, cb=cb, brep=brep, m=m, xdt=xdt, dec=dec, eacs=eacs, cd=cd)


def _blockdiag4(xb, mask4):
    return jnp.where(mask4, jnp.concatenate([xb] * 4, axis=0), jnp.zeros((), xb.dtype))


def _ssd_chunk_fwd(xs, bm, cm, dt, acs, d_skip, ht, masks):
    q = _ssd_common(xs, bm, cm, dt, acs, masks)
    mask4 = masks[3]
    mb, xdtb = _c(q["m"]), _c(q["xdt"])
    yd = []
    for blk in range(2):
        sl = slice(256 * blk, 256 * blk + 256)
        yd.append(_dot(mb[:, sl], _blockdiag4(xdtb[:, sl], mask4)))
    y_diag = jnp.concatenate(yd, axis=1)
    p = _dot(q["cb"], _c(ht))
    y = y_diag + p * q["eacs"] + xs * d_skip
    st = _dot_tn(q["bb"], _c(q["xdt"] * q["dec"]))
    return y, ht * q["cd"] + st


def _ssd_chunk_bwd(xs, bm, cm, dt, acs, d_skip, hprev, dht, dy, masks):
    diag, causal, row_last, mask4 = masks
    q = _ssd_common(xs, bm, cm, dt, acs, masks)
    lm, bb, cb, brep, m, xdt, dec, eacs, cd = (q[k] for k in ("lm", "bb", "cb", "brep", "m", "xdt", "dec", "eacs", "cd"))
    hb = _c(hprev)
    yoff = _dot(cb, hb) * eacs
    dyb = _c(dy)
    dpb = _c(dy * eacs)
    d_c = _dot_nt(dpb, hb)
    dh_y = _dot_tn(cb, dpb)
    mb, xdtb = _c(m), _c(xdt)
    dm_parts, dxdt_parts = [], []
    for blk in range(2):
        sl = slice(256 * blk, 256 * blk + 256)
        bd = _blockdiag4(xdtb[:, sl], mask4)
        dm_parts.append(_dot_nt(dyb[:, sl], bd))
        dxf = jnp.where(mask4, _dot_tn(mb[:, sl], dyb[:, sl]), 0.0)
        dxdt_parts.append(dxf[0:64] + dxf[64:128] + dxf[128:192] + dxf[192:256])
    dm = jnp.concatenate(dm_parts, axis=1)
    dxdt = jnp.concatenate(dxdt_parts, axis=1)
    dcbb = _c(dm * lm)
    g = dm * m
    d_c = d_c + _dot(dcbb, brep)
    dbrep = _dot_tn(dcbb, cb)
    d_b = dbrep[0:64]
    for r in range(1, 8):
        d_b = d_b + dbrep[64 * r:64 * r + 64]
    dhtb = _c(dht)
    dxd = _dot(bb, dhtb)
    xd = xdt * dec
    dxdt = dxdt + dxd * dec
    tq = dxd * xd
    d_b = d_b + _dot_nt(_c(xd), dhtb)
    dcd = jnp.sum(dht * hprev, axis=0, keepdims=True)
    col_g = jnp.sum(g, axis=0, keepdims=True)
    last = jnp.sum(tq, axis=0, keepdims=True) + dcd * cd
    qq = g - jnp.where(diag, col_g, 0.0) + dy * yoff - tq + jnp.where(row_last, last, 0.0)
    dxs = dxdt * dt + dy * d_skip
    return dxs, d_b, d_c, dht * cd + dh_y, dy * xs, qq, dxdt * xs


def _ssd_finish_dt(qq, p1, dt, sig, a_neg, trit, mask4):
    bd4 = _c(jnp.where(mask4, 1.0, 0.0))
    dacs = jnp.concatenate([_dot01_r(qq[:, 256 * b:256 * b + 256], bd4, 2) for b in range(2)], axis=1)
    da = _dot01_l(trit, dacs, 2)
    ddt = p1 + da * (a_neg * (1.0 / HEADDIM))
    return ddt * sig, da * dt


def _softplus(x):
    return jnp.maximum(x, 0.0) + jnp.log1p(jnp.exp(-jnp.abs(x)))


def _conv_taps(xpad, t):
    taps = []
    for k in range(4):
        sh = 3 - k
        v = xpad if sh == 0 else pltpu.roll(xpad, sh, 0)
        taps.append(v[8:8 + t])
    return taps


def _mm(a, b, *, tm, tn, tk, name, out_dtype=F32, col0=0, n=None):
    m, k = a.shape
    n = b.shape[1] if n is None else n
    nk = k // tk
    assert m % tm == 0 and n % tn == 0 and k % tk == 0, (a.shape, b.shape, tm, tn, tk)
    assert nk == 1 or out_dtype == F32

    def body(a_ref, b_ref, o_ref):
        if nk == 1:
            o_ref[...] = _dot(a_ref[...], b_ref[...]).astype(out_dtype)
        else:
            @pl.when(pl.program_id(2) == 0)
            def _():
                o_ref[...] = jnp.zeros_like(o_ref)

            o_ref[...] += _dot(a_ref[...], b_ref[...])

    isz = jnp.dtype(a.dtype).itemsize
    est = 2 * (tm * tk + tk * tn) * isz + 2 * tm * tn * 4
    return pl.pallas_call(
        body, name=name,
        grid=(m // tm, n // tn, nk),
        in_specs=[pl.BlockSpec((tm, tk), lambda i, j, kk: (i, kk)), pl.BlockSpec((tk, tn), lambda i, j, kk: (kk, j + col0))],
        out_specs=pl.BlockSpec((tm, tn), lambda i, j, kk: (i, j)),
        out_shape=jax.ShapeDtypeStruct((m, n), out_dtype),
        compiler_params=_cparams(("parallel", "parallel", "arbitrary"), est),
    )(a, b)


def _dw_groups(xnt, dpb, *, tk):
    m, k = xnt.shape
    nk = k // tk

    def body(a_ref, b_ref, zb_ref, xs_ref, bm_ref, cm_ref, dt_ref):
        outs = ((zb_ref, 0, 512), (xs_ref, 512, 1024), (bm_ref, 1024, 1152), (cm_ref, 1152, 1280), (dt_ref, 1280, 1408))

        @pl.when(pl.program_id(1) == 0)
        def _():
            for o_ref, _, _ in outs:
                o_ref[...] = jnp.zeros_like(o_ref)

        d = _dot(a_ref[...], b_ref[...])
        for o_ref, lo, hi in outs:
            o_ref[...] += d[:, lo:hi]

    isz = jnp.dtype(xnt.dtype).itemsize
    est = 2 * (m * tk + tk * BW_B) * isz + 3 * m * BW_B * 4
    piece = lambda w: pl.BlockSpec((m, w), lambda g, kk: (0, g))
    return pl.pallas_call(
        body, name="dw_in_b",
        grid=(SSD_GROUPS, nk),
        in_specs=[pl.BlockSpec((m, tk), lambda g, kk: (0, kk)), pl.BlockSpec((tk, BW_B), lambda g, kk: (kk, g))],
        out_specs=[piece(512), piece(512), piece(128), piece(128), piece(128)],
        out_shape=[jax.ShapeDtypeStruct((m, w), F32) for w in (2048, 2048, 512, 512, 512)],
        compiler_params=_cparams(("parallel", "arbitrary"), est),
    )(xnt, dpb)


def _dx_rs_call(dpa, wta, dpb, wtb, sb_in, sb_out, chip_v, *, tm):
    s = dpa.shape[0]
    tka, tkb = 1024, BW_B
    nka, nkb = dpa.shape[1] // tka, dpb.shape[1] // tkb
    ni, nk = s // tm, nka + nkb

    def body(a_ref, wa_ref, b_ref, wb_ref, sbin, sbout, cv, o_ref, rc_in, rc_out, abs_v, send, recv):
        i, kk = pl.program_id(0), pl.program_id(1)

        def copies():
            x, y, c, me, others = _place()
            sends, recvs = [], []
            for j, chip in enumerate(others):
                kj = 2 * chip[0] + chip[1]
                to = (*chip, c)
                sends += [_remote(sbin.at[kj], rc_in.at[j], send, recv, j, to),
                          _remote(sbout.at[kj], rc_out.at[j], send, recv, 3 + j, to),
                          _remote(cv, abs_v.at[me], send, recv, 6 + j, to)]
                recvs += [sends[-3], sends[-2], _remote(cv, abs_v.at[kj], send, recv, 6 + j, to)]
            return sends, recvs

        @pl.when((i == 0) & (kk == 0))
        def _():
            for cp in copies()[0]:
                cp.start()

        @pl.when(kk == 0)
        def _():
            o_ref[...] = jnp.zeros_like(o_ref)

        @pl.when(kk < nka)
        def _():
            o_ref[...] += _dot(a_ref[...], wa_ref[...])

        @pl.when(kk >= nka)
        def _():
            o_ref[...] += _dot(b_ref[...], wb_ref[...])

        @pl.when((i == ni - 1) & (kk == nk - 1))
        def _():
            sends, recvs = copies()
            for cp in recvs:
                cp.wait_recv()
            for cp in sends:
                cp.wait_send()

    isz = jnp.dtype(dpa.dtype).itemsize
    est = 2 * isz * (tm * tka + tka * D_MODEL + tm * tkb + tkb * D_MODEL) + 2 * tm * D_MODEL * 4
    outs = [jax.ShapeDtypeStruct((s, D_MODEL), F32),
            jax.ShapeDtypeStruct((3,) + sb_in.shape[1:], sb_in.dtype), jax.ShapeDtypeStruct((3,) + sb_out.shape[1:], sb_out.dtype),
            jax.ShapeDtypeStruct((4,) + chip_v.shape, F32)]
    return pl.pallas_call(
        body, name="dx_matmul_rs_chips",
        grid=(ni, nk),
        in_specs=[
            pl.BlockSpec((tm, tka), lambda i, kk: (i, jnp.minimum(kk, nka - 1))),
            pl.BlockSpec((tka, D_MODEL), lambda i, kk: (jnp.minimum(kk, nka - 1), 0)),
            pl.BlockSpec((tm, tkb), lambda i, kk: (i, jnp.maximum(kk - nka, 0))),
            pl.BlockSpec((tkb, D_MODEL), lambda i, kk: (jnp.maximum(kk - nka, 0), 0)),
            ANY, ANY, ANY,
        ],
        out_specs=[pl.BlockSpec((tm, D_MODEL), lambda i, kk: (i, 0)), ANY, ANY, ANY],
        out_shape=outs,
        scratch_shapes=[pltpu.SemaphoreType.DMA((9,)), pltpu.SemaphoreType.DMA((9,))],
        compiler_params=_cparams(("arbitrary", "arbitrary"), est),
    )(dpa, wta, dpb, wtb, sb_in, sb_out, chip_v)


def _gradx_call(x, dxn, dh, norm_w, tm):
    s = x.shape[0]

    def body(x_ref, g_ref, dh_ref, w_ref, gx_ref, dw_ref):
        @pl.when(pl.program_id(0) == 0)
        def _():
            dw_ref[...] = jnp.zeros_like(dw_ref)

        xv, gv = x_ref[...], g_ref[...]
        r = lax.rsqrt(jnp.mean(xv * xv, axis=-1, keepdims=True) + EPS)
        gw = gv * w_ref[...]
        gx_ref[...] = r * gw - xv * (r * r * r) * jnp.mean(xv * gw, axis=-1, keepdims=True) + dh_ref[...]
        dw_ref[...] += _fold8(gv * (xv * r))

    row = pl.BlockSpec((tm, D_MODEL), lambda i: (i, 0))
    return pl.pallas_call(
        body, name="grad_x",
        grid=(s // tm,),
        in_specs=[row, row, row, pl.BlockSpec((1, D_MODEL), lambda i: (0, 0))],
        out_specs=[row, pl.BlockSpec((8, D_MODEL), lambda i: (0, 0))],
        out_shape=[jax.ShapeDtypeStruct((s, D_MODEL), F32), jax.ShapeDtypeStruct((8, D_MODEL), F32)],
        compiler_params=_cparams(("arbitrary",), 2 * tm * D_MODEL * 16),
    )(x, dxn, dh, norm_w)


def _layernorm_stats(v):
    mu = jnp.mean(v, axis=-1, keepdims=True)
    vc = v - mu
    var = jnp.mean(vc * vc, axis=-1, keepdims=True)
    return vc * lax.rsqrt(var + EPS), lax.rsqrt(var + EPS)


def _tok_fwd_call(proj_a, y_b, gate_b, sgu_g, sgu_beta, wm, bias_full, t):
    s = proj_a.shape[0]

    def body(pa_ref, yb_ref, gb_ref, g_ref, be_ref, wm_ref, bf_ref, ya_ref, mg_ref, mgt_ref, mix_ref):
        u = pa_ref[:, 0:2048].astype(F32)
        v = pa_ref[:, 2048:4096].astype(F32)
        za = pa_ref[:, 4096:6144].astype(F32)
        xhat, _ = _layernorm_stats(v)
        vnb = _c(xhat * g_ref[...] + be_ref[...])
        for gi in range(SGU_GROUPS):
            sl = slice(128 * gi, 128 * gi + 128)
            mix_ref[:, sl] = _dot(wm_ref[gi], vnb[:, sl])
        mixed = mix_ref[...] + bf_ref[...]
        y_a = u * mixed * (za * _sigmoid(za))
        g0 = _sigmoid(pa_ref[:, 6144:8192].astype(F32) + gb_ref[:, 0:2048])
        g1 = _sigmoid(pa_ref[:, 8192:10240].astype(F32) + gb_ref[:, 2048:4096])
        merged = g0 * y_a + g1 * yb_ref[...].astype(F32)
        ya_ref[...] = _c(y_a)
        mg_ref[...] = _c(merged)
        mgt_ref[...] = _c(merged.T)

    row = pl.BlockSpec((t, D_MODEL), lambda i: (i, 0))
    vec = lambda w: pl.BlockSpec((1, w), lambda i: (0, 0))
    return pl.pallas_call(
        body, name="tok_fwd",
        grid=(s // t,),
        in_specs=[pl.BlockSpec((t, NA), lambda i: (i, 0)), row, vec(4096), vec(2048), vec(2048),
                  pl.BlockSpec((SGU_GROUPS, 128, 128), lambda i: (0, 0, 0)), pl.BlockSpec((128, D_MODEL), lambda i: (0, 0))],
        out_specs=[row, row, pl.BlockSpec((D_MODEL, t), lambda i: (0, i))],
        out_shape=[jax.ShapeDtypeStruct((s, D_MODEL), MXU_DTYPE), jax.ShapeDtypeStruct((s, D_MODEL), MXU_DTYPE),
                   jax.ShapeDtypeStruct((D_MODEL, s), MXU_DTYPE)],
        scratch_shapes=[pltpu.VMEM((t, D_MODEL), F32)],
        compiler_params=_cparams(("parallel",), 2 * t * NA * 4 + 12 * t * D_MODEL * 4),
    )(proj_a, y_b, gate_b, sgu_g, sgu_beta, wm, bias_full)


def _tok_bwd_call(proj_a, dmerged, y_a, y_b, gate_b, sgu_g, sgu_beta, wm, wmt, bias_full, t):
    s = proj_a.shape[0]

    def body(pa_ref, dm_ref, ya_ref, yb_ref, gb_ref, g_ref, be_ref, wm_ref, wmt_ref, bf_ref,
             dpa_ref, dyb_ref, dgb_ref, dgam_ref, dbeta_ref, dbf_ref, dws_ref, mix_ref, dvn_ref):
        @pl.when(pl.program_id(0) == 0)
        def _():
            dgb_ref[...] = jnp.zeros_like(dgb_ref)
            dgam_ref[...] = jnp.zeros_like(dgam_ref)
            dbeta_ref[...] = jnp.zeros_like(dbeta_ref)
            dbf_ref[...] = jnp.zeros_like(dbf_ref)
            dws_ref[...] = jnp.zeros_like(dws_ref)

        u = pa_ref[:, 0:2048].astype(F32)
        v = pa_ref[:, 2048:4096].astype(F32)
        za = pa_ref[:, 4096:6144].astype(F32)
        xhat, rstd = _layernorm_stats(v)
        vnb = _c(xhat * g_ref[...] + be_ref[...])
        for gi in range(SGU_GROUPS):
            sl = slice(128 * gi, 128 * gi + 128)
            mix_ref[:, sl] = _dot(wm_ref[gi], vnb[:, sl])
        mixed = mix_ref[...] + bf_ref[...]
        sig = _sigmoid(za)
        sz = za * sig
        dm = dm_ref[...].astype(F32)
        y_a = ya_ref[...].astype(F32)
        g0 = _sigmoid(pa_ref[:, 6144:8192].astype(F32) + gb_ref[:, 0:2048])
        g1 = _sigmoid(pa_ref[:, 8192:10240].astype(F32) + gb_ref[:, 2048:4096])
        dgl0 = dm * y_a * g0 * (1.0 - g0)
        dgl1 = dm * yb_ref[...].astype(F32) * g1 * (1.0 - g1)
        dyb_ref[...] = _c(dm * g1)
        dya = dm * g0
        dpa_ref[:, 6144:8192] = _c(dgl0)
        dpa_ref[:, 8192:10240] = _c(dgl1)
        dgb_ref[:, 0:2048] += _fold8(dgl0)
        dgb_ref[:, 2048:4096] += _fold8(dgl1)
        dpa_ref[:, 0:2048] = _c(dya * mixed * sz)
        dpa_ref[:, 4096:6144] = _c(dya * (u * mixed) * (sig * (1.0 + za * (1.0 - sig))))
        dmixed = dya * u * sz
        dbf_ref[...] += dmixed
        dmb = _c(dmixed)
        for gi in range(SGU_GROUPS):
            sl = slice(128 * gi, 128 * gi + 128)
            dvn_ref[:, sl] = _dot(wmt_ref[gi], dmb[:, sl])
            dws_ref[gi] += _dot_nt(dmb[:, sl], vnb[:, sl])
        dvn = dvn_ref[...]
        dgam_ref[...] += _fold8(dvn * xhat)
        dbeta_ref[...] += _fold8(dvn)
        dxh = dvn * g_ref[...]
        dv = rstd * (dxh - jnp.mean(dxh, axis=-1, keepdims=True) - xhat * jnp.mean(dxh * xhat, axis=-1, keepdims=True))
        dpa_ref[:, 2048:4096] = _c(dv)

    row = pl.BlockSpec((t, D_MODEL), lambda i: (i, 0))
    vec = lambda w: pl.BlockSpec((1, w), lambda i: (0, 0))
    acc = lambda w: pl.BlockSpec((8, w), lambda i: (0, 0))
    wspec = pl.BlockSpec((SGU_GROUPS, 128, 128), lambda i: (0, 0, 0))
    return pl.pallas_call(
        body, name="tok_bwd",
        grid=(s // t,),
        in_specs=[pl.BlockSpec((t, NA), lambda i: (i, 0)), row, row, row, vec(4096), vec(2048), vec(2048),
                  wspec, wspec, pl.BlockSpec((128, D_MODEL), lambda i: (0, 0))],
        out_specs=[pl.BlockSpec((t, NA), lambda i: (i, 0)), row, acc(4096), acc(2048), acc(2048),
                   pl.BlockSpec((128, D_MODEL), lambda i: (0, 0)), wspec],
        out_shape=[jax.ShapeDtypeStruct((s, NA), MXU_DTYPE), jax.ShapeDtypeStruct((s, D_MODEL), MXU_DTYPE),
                   jax.ShapeDtypeStruct((8, 4096), F32), jax.ShapeDtypeStruct((8, 2048), F32),
                   jax.ShapeDtypeStruct((8, 2048), F32), jax.ShapeDtypeStruct((128, D_MODEL), F32),
                   jax.ShapeDtypeStruct((SGU_GROUPS, 128, 128), F32)],
        scratch_shapes=[pltpu.VMEM((t, D_MODEL), F32), pltpu.VMEM((t, D_MODEL), F32)],
        compiler_params=_cparams(("arbitrary",), 2 * t * NA * 6 + 16 * t * D_MODEL * 4),
    )(proj_a, dmerged, y_a, y_b, gate_b, sgu_g, sgu_beta, wm, wmt, bias_full)


def _out_call(merged, x, target, w_out, fnw, t):
    s = x.shape[0]
    nt = s // t

    def body(mg_ref, x_ref, t_ref, w_ref, fw_ref, dh_ref, dhb_ref, dmg_ref, loss_ref, dfw_ref):
        @pl.when(pl.program_id(0) == 0)
        def _():
            dfw_ref[...] = jnp.zeros_like(dfw_ref)

        h = x_ref[...] + _dot(mg_ref[...], w_ref[...])
        r = lax.rsqrt(jnp.mean(h * h, axis=-1, keepdims=True) + EPS)
        hn = h * r
        err = hn * fw_ref[...] - t_ref[...]
        loss_ref[...] = jnp.full(loss_ref.shape, 0.5 * jnp.sum(jnp.mean(err * err, axis=-1, keepdims=True)), F32)
        dy = err * (1.0 / D_MODEL)
        dfw_ref[...] += _fold8(dy * hn)
        gw = dy * fw_ref[...]
        dh = r * gw - h * (r * r * r) * jnp.mean(h * gw, axis=-1, keepdims=True)
        dh_ref[...] = dh
        dhb = _c(dh)
        dhb_ref[...] = dhb
        dmg_ref[...] = _c(_dot_nt(dhb, w_ref[...]))

    row = pl.BlockSpec((t, D_MODEL), lambda i: (i, 0))
    return pl.pallas_call(
        body, name="out_proj_loss",
        grid=(nt,),
        in_specs=[row, row, row, pl.BlockSpec((D_MODEL, D_MODEL), lambda i: (0, 0)), pl.BlockSpec((1, D_MODEL), lambda i: (0, 0))],
        out_specs=[row, row, row, pl.BlockSpec((1, 8, 128), lambda i: (i, 0, 0)), pl.BlockSpec((8, D_MODEL), lambda i: (0, 0))],
        out_shape=[jax.ShapeDtypeStruct((s, D_MODEL), F32), jax.ShapeDtypeStruct((s, D_MODEL), MXU_DTYPE),
                   jax.ShapeDtypeStruct((s, D_MODEL), MXU_DTYPE), jax.ShapeDtypeStruct((nt, 8, 128), F32),
                   jax.ShapeDtypeStruct((8, D_MODEL), F32)],
        compiler_params=_cparams(("arbitrary",), 2 * D_MODEL * D_MODEL * 2 + 2 * t * D_MODEL * 24),
    )(merged, x, target, w_out, fnw)


def _ssd_fwd_call(proj_b, dtb, alog, dsk, cw, cb, nw, t, ng):
    s = proj_b.shape[0]
    nt, nch = s // t, t // CHUNK

    def body(pb_ref, halo_ref, dtb_ref, al_ref, ds_ref, cw_ref, cb_ref, nw_ref, y_ref, yb_ref, hp_ref, pre_ref,
             dt_ref, acs_ref, ht_ref, prev_ref):
        i = pl.program_id(1)

        @pl.when(i == 0)
        def _():
            ht_ref[...] = jnp.zeros_like(ht_ref)

        for gi in range(ng):
            prev_ref[:, 768 * gi:768 * gi + 768] = jnp.where(i == 0, 0.0, halo_ref[:, FW_B * gi:FW_B * gi + 768])
        masks = _ssd_masks()
        tri, _ = _cumsum_mats(CHUNK)
        a_neg = -jnp.exp(al_ref[...])

        def chunk(c, carry):
            rows = pl.ds(pl.multiple_of(c * CHUNK, CHUNK), CHUNK)
            for gi in range(ng):
                fo, co, go = FW_B * gi, 768 * gi, SSD_GW * gi
                xbc = pb_ref[rows, fo:fo + 768]
                taps = _conv_taps(jnp.concatenate([prev_ref[:, co:co + 768], xbc], axis=0), CHUNK)
                prev_ref[:, co:co + 768] = xbc[CHUNK - 8:CHUNK]
                pre = cb_ref[:, co:co + 768]
                for k in range(4):
                    pre = pre + taps[k] * cw_ref[k:k + 1, co:co + 768]
                pre_ref[rows, co:co + 768] = pre
                act = pre * _sigmoid(pre)
                dt = _softplus(pb_ref[rows, fo + 768:fo + 1280] + dtb_ref[:, go:go + 512])
                acs = _dot01_l(tri, dt * a_neg[:, go:go + 512], 3)
                dt_ref[rows, go:go + 512] = dt
                acs_ref[rows, go:go + 512] = acs
                ht = ht_ref[gi]
                hp_ref[c, :, go:go + 512] = ht
                y, ht_new = _ssd_chunk_fwd(act[:, 0:512], act[:, 512:640], act[:, 640:768], dt, acs,
                                           ds_ref[:, go:go + 512], ht, masks)
                y_ref[rows, go:go + 512] = y
                ht_ref[gi] = ht_new
                zb = pb_ref[rows, fo + 1280:fo + 1792]
                hh = y * (zb * _sigmoid(zb))
                rr = lax.rsqrt(jnp.mean(hh * hh, axis=-1, keepdims=True) + EPS)
                yb_ref[rows, go:go + 512] = _c(hh * rr * nw_ref[:, go:go + 512])
            return carry

        lax.fori_loop(0, nch, chunk, 0)

    gvec = lambda w: pl.BlockSpec((1, ng * w), lambda g, i: (0, g))
    return pl.pallas_call(
        body, name="ssd_fwd",
        grid=(SSD_GROUPS // ng, nt),
        in_specs=[pl.BlockSpec((t, ng * FW_B), lambda g, i: (i, g)),
                  pl.BlockSpec((8, ng * FW_B), lambda g, i: (jnp.maximum(i * (t // 8) - 1, 0), g)),
                  gvec(512), gvec(512), gvec(512),
                  pl.BlockSpec((4, ng * 768), lambda g, i: (0, g)), gvec(768), gvec(512)],
        out_specs=[pl.BlockSpec((t, ng * SSD_GW), lambda g, i: (i, g)), pl.BlockSpec((t, ng * SSD_GW), lambda g, i: (i, g)),
                   pl.BlockSpec((nch, SSD_STATE, ng * SSD_GW), lambda g, i: (i, 0, g)),
                   pl.BlockSpec((t, ng * 768), lambda g, i: (i, g)),
                   pl.BlockSpec((t, ng * SSD_GW), lambda g, i: (i, g)), pl.BlockSpec((t, ng * SSD_GW), lambda g, i: (i, g))],
        out_shape=[jax.ShapeDtypeStruct((s, D_MODEL), F32), jax.ShapeDtypeStruct((s, D_MODEL), MXU_DTYPE),
                   jax.ShapeDtypeStruct((s // CHUNK, SSD_STATE, D_MODEL), F32),
                   jax.ShapeDtypeStruct((s, SSD_GROUPS * 768), F32),
                   jax.ShapeDtypeStruct((s, D_MODEL), F32), jax.ShapeDtypeStruct((s, D_MODEL), F32)],
        scratch_shapes=[pltpu.VMEM((ng, SSD_STATE, SSD_GW), F32), pltpu.VMEM((8, ng * 768), F32)],
        compiler_params=_cparams(("parallel", "arbitrary"), ng * (2 * t * FW_B * 4 + 16 * t * SSD_GW * 4) + 16 * 1024 * 1024),
    )(proj_b, proj_b, dtb, alog, dsk, cw, cb, nw)


def _ssd_bwd_call(proj_b, pre_all, dt_all, acs_all, dyb, y, hprev, dtb, alog, dsk, cw, nw, t, ng):
    s = proj_b.shape[0]
    nt, nch = s // t, t // CHUNK

    def body(pb_ref, pre_ref, dt_ref, acs_ref, dyb_ref, y_ref, hp_ref, dtb_ref, al_ref, ds_ref, cw_ref, nw_ref,
             dpb_ref, a512_ref, a768_ref, dht_ref, nxt_ref, q_ref, p1_ref):
        i = pl.program_id(1)

        @pl.when(i == 0)
        def _():
            dht_ref[...] = jnp.zeros_like(dht_ref)
            nxt_ref[...] = jnp.zeros_like(nxt_ref)
            a512_ref[...] = jnp.zeros_like(a512_ref)
            a768_ref[...] = jnp.zeros_like(a768_ref)

        _, trit = _cumsum_mats(t)
        a_neg = -jnp.exp(al_ref[...])
        masks = _ssd_masks()

        def chunk(cc, carry):
            c = nch - 1 - cc
            rows = pl.ds(pl.multiple_of(c * CHUNK, CHUNK), CHUNK)
            for gi in range(ng):
                fo, co, go, bo = FW_B * gi, 768 * gi, SSD_GW * gi, BW_B * gi
                pre = pre_ref[rows, co:co + 768]
                sp = _sigmoid(pre)
                act = pre * sp
                zb = pb_ref[rows, fo + 1280:fo + 1792]
                yv = y_ref[rows, go:go + 512]
                sgz = _sigmoid(zb)
                sz = zb * sgz
                hh = yv * sz
                rr = lax.rsqrt(jnp.mean(hh * hh, axis=-1, keepdims=True) + EPS)
                dyb = dyb_ref[rows, go:go + 512].astype(F32)
                a512_ref[gi, 0] += _fold8(dyb * (hh * rr))
                tt = dyb * nw_ref[:, go:go + 512]
                dhh = rr * tt - hh * (rr * rr * rr) * jnp.mean(hh * tt, axis=-1, keepdims=True)
                dpb_ref[rows, bo:bo + 512] = _c(dhh * yv * (sgz * (1.0 + zb * (1.0 - sgz))))
                dxs, d_b, d_c, dht_prev, dyxs, qq, p1 = _ssd_chunk_bwd(
                    act[:, 0:512], act[:, 512:640], act[:, 640:768], dt_ref[rows, go:go + 512], acs_ref[rows, go:go + 512],
                    ds_ref[:, go:go + 512], hp_ref[c, :, go:go + 512], dht_ref[gi], dhh * sz, masks)
                dht_ref[gi] = dht_prev
                q_ref[rows, go:go + 512] = qq
                p1_ref[rows, go:go + 512] = p1
                a512_ref[gi, 1] += _fold8(dyxs)
                dpre = jnp.concatenate([dxs, d_b, d_c], axis=1) * (sp * (1.0 + pre * (1.0 - sp)))
                xbc = pb_ref[rows, fo:fo + 768]
                a768_ref[gi, 4] += _fold8(dpre)
                a768_ref[gi, 3] += _fold8(dpre * xbc)
                dpad = jnp.concatenate([dpre, nxt_ref[:, co:co + 768]], axis=0)
                dx = dpre * cw_ref[3:4, co:co + 768]
                for k in range(3):
                    d_k = pltpu.roll(dpad, CHUNK + 8 - (3 - k), 0)[0:CHUNK]
                    dx = dx + d_k * cw_ref[k:k + 1, co:co + 768]
                    a768_ref[gi, k] += _fold8(d_k * xbc)
                nxt_ref[:, co:co + 768] = dpre[0:8]
                dpb_ref[rows, bo + 512:bo + 1280] = _c(dx)
            return carry

        lax.fori_loop(0, nch, chunk, 0)

        rsel = _c(jnp.where(lax.shift_right_logical(_iota((SSD_GW, 128), 0), 6) == _iota((SSD_GW, 128), 1), 1.0, 0.0))
        for gi in range(ng):
            fo, go, bo = FW_B * gi, SSD_GW * gi, BW_B * gi
            sig = _sigmoid(pb_ref[:, fo + 768:fo + 1280] + dtb_ref[:, go:go + 512])
            ddtr, dadt = _ssd_finish_dt(q_ref[:, go:go + 512], p1_ref[:, go:go + 512], dt_ref[:, go:go + 512], sig,
                                        a_neg[:, go:go + 512], trit, masks[3])
            dpb_ref[:, bo + 1280:bo + 1408] = _c(_dot01_r(ddtr, rsel, 2))
            a512_ref[gi, 2] += _fold8(dadt)
            a512_ref[gi, 3] += _fold8(ddtr)

    gvec = lambda w: pl.BlockSpec((1, ng * w), lambda g, i: (0, g))
    rev = lambda w: pl.BlockSpec((t, ng * w), lambda g, i: (nt - 1 - i, g))
    return pl.pallas_call(
        body, name="ssd_bwd",
        grid=(SSD_GROUPS // ng, nt),
        in_specs=[rev(FW_B), rev(768), rev(SSD_GW), rev(SSD_GW), rev(SSD_GW), rev(SSD_GW),
                  pl.BlockSpec((nch, SSD_STATE, ng * SSD_GW), lambda g, i: (nt - 1 - i, 0, g)),
                  gvec(512), gvec(512), gvec(512),
                  pl.BlockSpec((4, ng * 768), lambda g, i: (0, g)), gvec(512)],
        out_specs=[rev(BW_B),
                   pl.BlockSpec((ng, 4, 8, 512), lambda g, i: (g, 0, 0, 0)),
                   pl.BlockSpec((ng, 5, 8, 768), lambda g, i: (g, 0, 0, 0))],
        out_shape=[jax.ShapeDtypeStruct((s, SSD_GROUPS * BW_B), MXU_DTYPE),
                   jax.ShapeDtypeStruct((SSD_GROUPS, 4, 8, 512), F32),
                   jax.ShapeDtypeStruct((SSD_GROUPS, 5, 8, 768), F32)],
        scratch_shapes=[pltpu.VMEM((ng, SSD_STATE, SSD_GW), F32), pltpu.VMEM((8, ng * 768), F32),
                        pltpu.VMEM((t, ng * SSD_GW), F32), pltpu.VMEM((t, ng * SSD_GW), F32)],
        compiler_params=_cparams(("parallel", "arbitrary"), ng * (2 * t * FW_B * 4 + 18 * t * SSD_GW * 4) + 16 * 1024 * 1024),
    )(proj_b, pre_all, dt_all, acs_all, dyb, y, hprev, dtb, alog, dsk, cw, nw)


def _rows_call(body, ins, outs, tr, name):
    r = ins[0].shape[0]
    spec = lambda a: pl.BlockSpec((tr, a.shape[1]), lambda i: (i, 0))
    est = 2 * tr * sum(a.shape[1] * jnp.dtype(a.dtype).itemsize for a in list(ins) + list(outs))
    return pl.pallas_call(
        body, name=name, grid=(r // tr,),
        in_specs=[spec(a) for a in ins], out_specs=[spec(o) for o in outs], out_shape=list(outs),
        compiler_params=_cparams(("parallel",), est),
    )(*ins)


def _add_pair(a, b, tr, name):
    def body(a_ref, b_ref, o_ref):
        o_ref[...] = a_ref[...] + b_ref[...]

    return _rows_call(body, [a, b], [jax.ShapeDtypeStruct(a.shape, F32)], tr, name)[0]


def _rs_add(p, sib, place, tr, name):
    _, r, c = p.shape
    half = r // 2
    nb = half // tr

    def body(pl_ref, p_ref, s_ref, b_ref, own_ref):
        v = p_ref[0] + s_ref[0]
        b_ref[0] = v.astype(jnp.bfloat16)

        @pl.when(pl.program_id(1) == pl_ref[0])
        def _():
            own_ref[...] = v

    return pl.pallas_call(
        body, name=name,
        grid_spec=pltpu.PrefetchScalarGridSpec(
            num_scalar_prefetch=1, grid=(nb, 4),
            in_specs=[pl.BlockSpec((1, tr, c), lambda i, k, pr: (k, pr[1] * nb + i, 0)),
                      pl.BlockSpec((1, tr, c), lambda i, k, pr: (k, i, 0))],
            out_specs=[pl.BlockSpec((1, tr, c), lambda i, k, pr: (k, i, 0)),
                       pl.BlockSpec((tr, c), lambda i, k, pr: (i, 0))]),
        out_shape=[jax.ShapeDtypeStruct((4, half, c), jnp.bfloat16), jax.ShapeDtypeStruct((half, c), F32)],
        compiler_params=_cparams(("parallel", "arbitrary"), 2 * tr * c * 14),
    )(place, p, sib)


def _sum_own_recv(own, recv, tr, name):
    r, c = own.shape

    def body(o_ref, r_ref, out_ref):
        v = o_ref[...]
        for j in range(3):
            v = v + r_ref[j].astype(F32)
        out_ref[...] = v

    return pl.pallas_call(
        body, name=name, grid=(r // tr,),
        in_specs=[pl.BlockSpec((tr, c), lambda i: (i, 0)), pl.BlockSpec((3, tr, c), lambda i: (0, i, 0))],
        out_specs=pl.BlockSpec((tr, c), lambda i: (i, 0)),
        out_shape=jax.ShapeDtypeStruct((r, c), F32),
        compiler_params=_cparams(("parallel",), 2 * tr * c * 14),
    )(own, recv)


def _sum_slots(stack, name):
    n, r, w = stack.shape

    def body(a_ref, out_ref):
        v = a_ref[0]
        for k in range(1, n):
            v = v + a_ref[k]
        out_ref[...] = v

    return pl.pallas_call(
        body, name=name, grid=(1,),
        in_specs=[pl.BlockSpec((n, r, w), lambda i: (0, 0, 0))],
        out_specs=pl.BlockSpec((r, w), lambda i: (0, 0)),
        out_shape=jax.ShapeDtypeStruct((r, w), F32),
        compiler_params=_cparams(("arbitrary",), 2 * (n + 1) * r * w * 4),
    )(stack)


def _adamw(w, g, m, v, tr, name):
    def body(w_ref, g_ref, m_ref, v_ref, d_ref, nm_ref, nv_ref):
        d_ref[...], nm_ref[...], nv_ref[...] = _adam_math(w_ref[...], g_ref[...], m_ref[...], v_ref[...])

    o = jax.ShapeDtypeStruct(w.shape, F32)
    return _rows_call(body, [w, g, m, v], [o, o, o], tr, name)


def _adam_math(w, g, m, v):
    nm = ADAM_B1 * m + (1.0 - ADAM_B1) * g
    nv = ADAM_B2 * v + (1.0 - ADAM_B2) * (g * g)
    m_hat = nm / (1.0 - ADAM_B1 ** ADAM_STEP)
    v_hat = nv / (1.0 - ADAM_B2 ** ADAM_STEP)
    return -ADAM_LR * (m_hat / (jnp.sqrt(v_hat) + ADAM_EPS) + ADAM_WD * w), nm, nv


def _adamw_halves(w, g_own, g_sib, m, v, place, tr, name):
    r, c = w.shape

    def body(pl_ref, w_ref, go_ref, gs_ref, m_ref, v_ref, g_ref, d_ref, nm_ref, nv_ref):
        first = pl_ref[1] == 0
        own, sib = go_ref[...], gs_ref[...]
        g = jnp.concatenate([jnp.where(first, own, sib), jnp.where(first, sib, own)], axis=1)
        g_ref[...] = g
        d_ref[...], nm_ref[...], nv_ref[...] = _adam_math(w_ref[...], g, m_ref[...], v_ref[...])

    full = pl.BlockSpec((tr, c), lambda i, pr: (i, 0))
    half = pl.BlockSpec((tr, c // 2), lambda i, pr: (i, 0))
    o = jax.ShapeDtypeStruct((r, c), F32)
    return pl.pallas_call(
        body, name=name,
        grid_spec=pltpu.PrefetchScalarGridSpec(num_scalar_prefetch=1, grid=(r // tr,),
                                               in_specs=[full, half, half, full, full], out_specs=[full] * 4),
        out_shape=[o] * 4,
        compiler_params=_cparams(("parallel",), 2 * tr * c * 4 * 8),
    )(place, w, g_own, g_sib, m, v)


ANY = pl.BlockSpec(memory_space=pl.ANY)


def _place():
    x, y, c = lax.axis_index("x"), lax.axis_index("y"), lax.axis_index("c")
    others = [(1 - x, y), (x, 1 - y), (1 - x, 1 - y)]
    return x, y, c, 2 * x + y, others


def _remote(src, dst, send, recv, k, to):
    return pltpu.make_async_remote_copy(src_ref=src, dst_ref=dst, send_sem=send.at[k], recv_sem=recv.at[k],
                                        device_id=to, device_id_type=MESH)


def _norm_gather_call(x, norm_w, win_b, wout_b, cw8, tm):
    s = x.shape[0]
    ni = s // tm

    def body(x_ref, w_ref, win, wout, cw, xn_ref, xnt_ref, g_in, g_out, g_cw, send, recv):
        i = pl.program_id(0)

        def direct():
            xx, yy, c, me, others = _place()
            cps = []
            for a, (src, dst) in enumerate(((win, g_in), (wout, g_out))):
                half = src.shape[0] // 2
                mine = pl.ds(c * half, half)
                cps += [_remote(src.at[mine], dst.at[me, mine], send, recv, 6 * a + j, (*chip, c)) for j, chip in enumerate(others)]
            cps += [_remote(cw, g_cw.at[me], send, recv, 12 + j, (*chip, c)) for j, chip in enumerate(others)]
            return cps

        @pl.when(i == 0)
        def _():
            for cp in direct():
                cp.start()

        xv = x_ref[...]
        r = lax.rsqrt(jnp.mean(xv * xv, axis=-1, keepdims=True) + EPS)
        xn = xv * r * w_ref[...]
        xn_ref[...] = _c(xn)
        xnt_ref[...] = _c(xn.T)

        @pl.when(i == ni - 1)
        def _():
            xx, yy, c, me, others = _place()
            sib = (xx, yy, 1 - c)
            passed = []
            for a, dst in enumerate((g_in, g_out)):
                half = dst.shape[1] // 2
                mine = pl.ds(c * half, half)
                for j, chip in enumerate(others):
                    kj = 2 * chip[0] + chip[1]
                    _remote(dst.at[kj, mine], dst.at[kj, mine], send, recv, 6 * a + j, (*chip, c)).wait_recv()
                    cp = _remote(dst.at[kj, mine], dst.at[kj, mine], send, recv, 6 * a + 3 + j, sib)
                    cp.start()
                    passed.append(cp)
            for a, dst in enumerate((g_in, g_out)):
                half = dst.shape[1] // 2
                theirs = pl.ds((1 - c) * half, half)
                for j, chip in enumerate(others):
                    kj = 2 * chip[0] + chip[1]
                    _remote(dst.at[kj, theirs], dst.at[kj, theirs], send, recv, 6 * a + 3 + j, sib).wait_recv()
            for j, chip in enumerate(others):
                kj = 2 * chip[0] + chip[1]
                _remote(cw, g_cw.at[kj], send, recv, 12 + j, (*chip, c)).wait_recv()
            for cp in direct() + passed:
                cp.wait_send()

    outs = [jax.ShapeDtypeStruct((s, D_MODEL), MXU_DTYPE), jax.ShapeDtypeStruct((D_MODEL, s), MXU_DTYPE)]
    outs += [jax.ShapeDtypeStruct((4,) + a.shape, a.dtype) for a in (win_b, wout_b, cw8)]
    return pl.pallas_call(
        body, name="rmsnorm_gather_weights",
        grid=(ni,),
        in_specs=[pl.BlockSpec((tm, D_MODEL), lambda i: (i, 0)), pl.BlockSpec((1, D_MODEL), lambda i: (0, 0)), ANY, ANY, ANY],
        out_specs=[pl.BlockSpec((tm, D_MODEL), lambda i: (i, 0)), pl.BlockSpec((D_MODEL, tm), lambda i: (0, i)), ANY, ANY, ANY],
        out_shape=outs,
        scratch_shapes=[pltpu.SemaphoreType.DMA((15,)), pltpu.SemaphoreType.DMA((15,))],
        compiler_params=_cparams(("arbitrary",), 2 * tm * D_MODEL * 12),
    )(x, norm_w, win_b, wout_b, cw8)


def _rs_sibling_call(p_in, p_out, vsmall):
    def body(pin, pout, vs, sib_in, sib_out, sib_v, send, recv):
        x, y, c, me, others = _place()
        sib = (x, y, 1 - c)
        cps = []
        for a, (p, sb) in enumerate(((pin, sib_in), (pout, sib_out))):
            half = p.shape[1] // 2
            cps.append(_remote(p.at[:, pl.ds((1 - c) * half, half)], sb, send, recv, a, sib))
        cps.append(_remote(vs, sib_v, send, recv, 2, sib))
        for cp in cps:
            cp.start()
        for cp in cps:
            cp.wait_recv()
        for cp in cps:
            cp.wait_send()

    def halves(p):
        return jax.ShapeDtypeStruct((4, p.shape[1] // 2, p.shape[2]), p.dtype)

    outs = [halves(p_in), halves(p_out), jax.ShapeDtypeStruct(vsmall.shape, vsmall.dtype)]
    return pl.pallas_call(
        body, name="rs_sibling",
        in_specs=[ANY] * 3, out_specs=[ANY] * 3, out_shape=outs,
        scratch_shapes=[pltpu.SemaphoreType.DMA((3,)), pltpu.SemaphoreType.DMA((3,))],
    )(p_in, p_out, vsmall)


def _rs_join_call(f_in, f_out, nw8):
    def body(fin, fout, nw, sib_in, full_out, all_nw, send, recv):
        x, y, c, me, others = _place()
        sib = (x, y, 1 - c)
        half = fout.shape[0]
        cps = [_remote(fin, sib_in, send, recv, 0, sib),
               _remote(fout, full_out.at[pl.ds(c * half, half)], send, recv, 1, sib)]
        mine = 4 * x + 2 * y + c
        peers = []
        for r in range(1, 8):
            px, py, pc = (1 - x if r & 4 else x), (1 - y if r & 2 else y), (1 - c if r & 1 else c)
            peers.append((r, (px, py, pc), 4 * px + 2 * py + pc))
            cps.append(_remote(nw, all_nw.at[mine], send, recv, 1 + r, (px, py, pc)))
        for cp in cps:
            cp.start()
        cps[0].wait_recv()
        _remote(fout, full_out.at[pl.ds((1 - c) * half, half)], send, recv, 1, sib).wait_recv()
        for r, peer, idx in peers:
            _remote(nw, all_nw.at[idx], send, recv, 1 + r, peer).wait_recv()
        for cp in cps:
            cp.wait_send()

    outs = [jax.ShapeDtypeStruct(f_in.shape, F32), jax.ShapeDtypeStruct((2 * f_out.shape[0], f_out.shape[1]), F32),
            jax.ShapeDtypeStruct((8,) + nw8.shape, F32)]
    return pl.pallas_call(
        body, name="rs_join",
        in_specs=[ANY] * 3, out_specs=[ANY] * 3, out_shape=outs,
        scratch_shapes=[pltpu.SemaphoreType.DMA((9,)), pltpu.SemaphoreType.DMA((9,))],
    )(f_in, f_out, nw8)


def _pack(arrs):
    parts = []
    for a in arrs:
        f = a.reshape(-1).astype(F32)
        pad = (-f.shape[0]) % 1024
        parts.append(jnp.pad(f, (0, pad)).reshape(-1, 128))
    return jnp.concatenate(parts, axis=0)


def _unpack(packed, shapes):
    out, row = [], 0
    for shp in shapes:
        n = 1
        for d in shp:
            n *= d
        rows = (n + 1023) // 1024 * 8
        out.append(packed[row:row + rows].reshape(-1)[:n].reshape(shp))
        row += rows
    return out


def _expand_heads(v32):
    return jnp.repeat(v32.reshape(32), HEADDIM).reshape(1, D_MODEL)


def kernel(x, norm_w, w_in, gate_b, sgu_norm_g, sgu_norm_b, sgu_w, sgu_b, conv_w, conv_b, dt_bias, A_log, D_skip, ssd_norm_w, w_out, final_norm_w, loss_target, m_norm_w, m_w_in, m_gate_b, m_sgu_norm_g, m_sgu_norm_b, m_sgu_w, m_sgu_b, m_conv_w, m_conv_b, m_dt_bias, m_A_log, m_D_skip, m_ssd_norm_w, m_w_out, m_final_norm_w, v_norm_w, v_w_in, v_gate_b, v_sgu_norm_g, v_sgu_norm_b, v_sgu_w, v_sgu_b, v_conv_w, v_conv_b, v_dt_bias, v_A_log, v_D_skip, v_ssd_norm_w, v_w_out, v_final_norm_w):
    s = x.shape[1]
    x2 = x.reshape(s, D_MODEL)
    tgt = loss_target.reshape(s, D_MODEL)
    t_ssd, t_tok, t_out, t_row = min(T_SSD, s), min(T_TOK, s), min(T_OUT, s), min(T_ROW, s)
    tm_mm, tk_dw = min(TM_MM, s), min(TK_DW, s)
    chip = 2 * lax.axis_index("x") + lax.axis_index("y")

    cw8 = jnp.pad(conv_w[0], ((0, 4), (0, 0)))
    win_b, wout_b = _c(w_in[0]), _c(w_out[0])
    xn, xnt, g_in, g_out, g_cw = _norm_gather_call(x2, norm_w, win_b, wout_b, cw8, t_row)
    g_in = lax.dynamic_update_index_in_dim(g_in, win_b, chip, 0)
    g_out = lax.dynamic_update_index_in_dim(g_out, wout_b, chip, 0)
    g_cw = lax.dynamic_update_index_in_dim(g_cw, cw8, chip, 0)
    wref = jnp.transpose(g_in, (1, 0, 2)).reshape(D_MODEL, IN_W)
    w_out_full = g_out.reshape(D_MODEL, D_MODEL)
    conv_w_full = jnp.transpose(g_cw[:, 0:4, :], (1, 0, 2)).reshape(4, 3072)

    w_a = jnp.concatenate([wref[:, 0:6144], wref[:, 11296:15392]], axis=1)
    fw, bw = [], []
    for g in range(SSD_GROUPS):
        xs_g = wref[:, 8192 + 512 * g:8192 + 512 * g + 512]
        b_g = wref[:, 10240 + 128 * g:10240 + 128 * g + 128]
        c_g = wref[:, 10752 + 128 * g:10752 + 128 * g + 128]
        zb_g = wref[:, 6144 + 512 * g:6144 + 512 * g + 512]
        dt_g = wref[:, 11264 + 8 * g:11264 + 8 * g + 8]
        fw += [xs_g, b_g, c_g, jnp.repeat(dt_g, HEADDIM, axis=1), zb_g]
        bw += [zb_g, xs_g, b_g, c_g, jnp.pad(dt_g, ((0, 0), (0, 120)))]
    w_b = jnp.concatenate(fw, axis=1)
    wt_a = w_a.T
    wt_b = jnp.concatenate(bw, axis=1).T

    def group_cols(full_xs, full_bc):
        parts = []
        for g in range(SSD_GROUPS):
            parts += [full_xs[:, 512 * g:512 * g + 512], full_bc[:, 128 * g:128 * g + 128], full_bc[:, 512 + 128 * g:512 + 128 * g + 128]]
        return jnp.concatenate(parts, axis=1)

    cw_g = group_cols(conv_w_full[:, 0:2048], conv_w_full[:, 2048:3072])
    cb_g = group_cols(conv_b[:, 0:2048], conv_b[:, 2048:3072])
    dtb_e, alog_e, dsk_e = _expand_heads(dt_bias), _expand_heads(A_log), _expand_heads(D_skip)

    pos_chunk = jnp.arange(SGU_BLOCK) // CHUNK
    smask = pos_chunk[None, :] <= pos_chunk[:, None]
    wm_f = jnp.where(smask[None], sgu_w[0], 0.0)
    wm = _c(wm_f)
    wmt = _c(jnp.transpose(wm_f, (0, 2, 1)))
    bias_full = jnp.repeat(sgu_b[0].T, D_MODEL // SGU_GROUPS, axis=1)
    fnw = final_norm_w.reshape(1, D_MODEL)

    proj_a = _mm(xn, w_a, tm=tm_mm, tn=1024, tk=D_MODEL, name="in_proj_a", out_dtype=MXU_DTYPE)
    proj_b = _mm(xn, w_b, tm=tm_mm, tn=1024, tk=D_MODEL, name="in_proj_b")
    y_ssd, y_b, hprev, pre_all, dt_all, acs_all = _ssd_fwd_call(proj_b, dtb_e, alog_e, dsk_e, cw_g, cb_g, ssd_norm_w, t_ssd, NG_SSD)
    y_a, merged, merged_t = _tok_fwd_call(proj_a, y_b, gate_b, sgu_norm_g, sgu_norm_b, wm, bias_full, t_tok)
    dh, dh_b, dmerged, loss_t, dfw8 = _out_call(merged, x2, tgt, w_out_full, fnw, t_out)

    dproj_a, dy_b, dgb8, dgam8, dbeta8, dbfull, dws = _tok_bwd_call(
        proj_a, dmerged, y_a, y_b, gate_b, sgu_norm_g, sgu_norm_b, wm, wmt, bias_full, t_tok)
    dproj_b, a512, a768 = _ssd_bwd_call(proj_b, pre_all, dt_all, acs_all, dy_b, y_ssd, hprev, dtb_e, alog_e, dsk_e, cw_g,
                                        ssd_norm_w, t_ssd, NG_SSD)
    dw_uvz = _mm(xnt, dproj_a, tm=D_MODEL, tn=1024, tk=tk_dw, name="dw_in_uvz", n=6144)
    dw_gate = _mm(xnt, dproj_a, tm=D_MODEL, tn=1024, tk=tk_dw, name="dw_in_gate", col0=6, n=4096)
    dw_zb, dw_xs, dw_bm, dw_cm, dw_dt = _dw_groups(xnt, dproj_b, tk=tk_dw)
    dw_out_p = _mm(merged_t, dh_b, tm=D_MODEL, tn=1024, tk=tk_dw, name="dw_out")

    dw_dt32 = jnp.concatenate([dw_dt[:, 128 * g:128 * g + 8] for g in range(SSD_GROUPS)], axis=1)
    dw_ref = jnp.concatenate([dw_uvz, dw_zb, dw_xs, dw_bm, dw_cm, dw_dt32, dw_gate], axis=1)
    p_in = jnp.transpose(dw_ref.reshape(D_MODEL, 4, SHARD_W), (1, 0, 2))
    p_out = dw_out_p.reshape(4, D_MODEL // 4, D_MODEL)

    s512 = jnp.sum(a512, axis=2)
    heads = lambda v: jnp.sum(v.reshape(32, HEADDIM), axis=1).reshape(1, 32)
    d_ssd_nw = s512[:, 0].reshape(1, D_MODEL)
    d_dskip = heads(s512[:, 1].reshape(D_MODEL))
    d_alog = heads(s512[:, 2].reshape(D_MODEL)) * (1.0 / HEADDIM) * (-jnp.exp(A_log))
    d_dtb = heads(s512[:, 3].reshape(D_MODEL))
    s768 = jnp.sum(a768, axis=2)
    ungroup = lambda v: jnp.concatenate([v[g, :, 0:512] for g in range(4)] + [v[g, :, 512:640] for g in range(4)]
                                        + [v[g, :, 640:768] for g in range(4)], axis=1)
    d_cw = ungroup(s768[:, 0:4])
    d_cb = ungroup(s768[:, 4:5])
    d_sgu_b = jnp.sum(dbfull.reshape(128, SGU_GROUPS, 128), axis=2).T.reshape(1, SGU_GROUPS, 128)
    d_sgu_w = jnp.where(smask[None], dws, 0.0).reshape(1, SGU_GROUPS, 128, 128)
    fold = lambda a8: jnp.sum(a8, axis=0, keepdims=True)
    small_local = [fold(dgb8), fold(dgam8), fold(dbeta8), d_sgu_w, d_sgu_b, d_cw, d_cb,
                   d_dtb, d_alog, d_dskip, d_ssd_nw, fold(dfw8).reshape(D_MODEL), jnp.sum(loss_t[:, 0, 0]).reshape(1)]
    small_shapes = [a.shape for a in small_local]
    v_local = _pack(small_local)

    core = lax.axis_index("c")
    place = jnp.stack([chip, core]).astype(jnp.int32)
    hr_i, hr_o = D_MODEL // 2, D_MODEL // 8
    sib_i, sib_o, sib_v = _rs_sibling_call(p_in, p_out, v_local)
    s1b_i, o_i = _rs_add(p_in, sib_i, place, 256, "rs_add_in")
    s1b_o, o_o = _rs_add(p_out, sib_o, place, 256, "rs_add_out")
    chip_v = _add_pair(v_local, sib_v, v_local.shape[0], "ar_add_small")
    dxn, r_i, r_o, abs_v = _dx_rs_call(dproj_a, wt_a, dproj_b, wt_b, s1b_i, s1b_o, chip_v, tm=tm_mm)
    grad_x, dnw8 = _gradx_call(x2, dxn, dh, norm_w, t_row)
    abs_v = lax.dynamic_update_index_in_dim(abs_v, chip_v, chip, 0)
    f_i = _sum_own_recv(o_i, r_i, 256, "rs_sum_in")
    f_o = _sum_own_recv(o_o, r_o, 256, "rs_sum_out")
    sib_f_i, g_w_out, all_nw = _rs_join_call(f_i, f_o, dnw8)
    g_w_out = lax.dynamic_update_slice_in_dim(g_w_out, f_o, core * hr_o, axis=0)
    all_nw = lax.dynamic_update_index_in_dim(all_nw, dnw8, 2 * chip + core, 0)
    g_nw = fold(_sum_slots(all_nw, "ar_sum_norm_w"))
    total_v = _sum_slots(abs_v, "ar_sum_small")
    (g_gb, g_gam, g_beta, g_sw, g_sb, g_cw_full, g_cb, g_dtb, g_alog, g_dsk, g_snw, g_fnw, loss1) = _unpack(total_v, small_shapes)
    g_cw_shard = lax.dynamic_slice(g_cw_full, (0, chip * 768), (4, 768)).reshape(1, 4, 768)
    loss = loss1.reshape(())

    g_w_in, d_win, nm_win, nv_win = (a.T for a in _adamw_halves(w_in[0].T, f_i.T, sib_f_i.T, m_w_in[0].T, v_w_in[0].T,
                                                                place, 296, "adamw_w_in"))
    d_wout, nm_wout, nv_wout = _adamw(w_out[0], g_w_out, m_w_out[0], v_w_out[0], 128, "adamw_w_out")
    small_w = [norm_w, gate_b, sgu_norm_g, sgu_norm_b, sgu_w, sgu_b, conv_w, conv_b, dt_bias, A_log, D_skip, ssd_norm_w, final_norm_w]
    small_m = [m_norm_w, m_gate_b, m_sgu_norm_g, m_sgu_norm_b, m_sgu_w, m_sgu_b, m_conv_w, m_conv_b, m_dt_bias, m_A_log, m_D_skip, m_ssd_norm_w, m_final_norm_w]
    small_v = [v_norm_w, v_gate_b, v_sgu_norm_g, v_sgu_norm_b, v_sgu_w, v_sgu_b, v_conv_w, v_conv_b, v_dt_bias, v_A_log, v_D_skip, v_ssd_norm_w, v_final_norm_w]
    small_g = [g_nw, g_gb, g_gam, g_beta, g_sw, g_sb, g_cw_shard, g_cb, g_dtb, g_alog, g_dsk, g_snw, g_fnw]
    shapes_w = [a.shape for a in small_w]
    small_g = [a.reshape(shp) for a, shp in zip(small_g, shapes_w)]
    pw = _pack(small_w)
    pd, pm, pv = _adamw(pw, _pack(small_g), _pack(small_m), _pack(small_v), pw.shape[0], "adamw_small")
    d_small, nm_small, nv_small = _unpack(pd, shapes_w), _unpack(pm, shapes_w), _unpack(pv, shapes_w)

    def with_big(small, win, wout):
        o = list(small)
        return o[0:1] + [win.reshape(1, D_MODEL, SHARD_W)] + o[1:12] + [wout.reshape(1, D_MODEL // 4, D_MODEL)] + o[12:13]

    grads = with_big(small_g, g_w_in, g_w_out)
    deltas = with_big(d_small, d_win, d_wout)
    new_m = with_big(nm_small, nm_win, nm_wout)
    new_v = with_big(nv_small, nv_win, nv_wout)
    return (loss, grad_x.reshape(1, s, D_MODEL), *grads, *deltas, *new_m, *new_v)
```

```python
import functools

import jax
import jax.numpy as jnp
from jax import lax
from jax.experimental import pallas as pl
from jax.experimental.pallas import tpu as pltpu

F32 = jnp.float32
MXU_DTYPE = jnp.bfloat16

D_MODEL = 2048
EPS = 1e-5
CHUNK = 64
SGU_BLOCK = 128
SGU_GROUPS = 16
SSD_GROUPS = 4
SSD_GW = 512
SSD_STATE = 128
HEADDIM = 64
IN_W = 15392
SHARD_W = IN_W // 4
FW_B = 1792
BW_B = 1408
NA = 10240

ADAM_LR = 0.001
ADAM_B1 = 0.9
ADAM_B2 = 0.999
ADAM_EPS = 1e-08
ADAM_WD = 0.01
ADAM_STEP = 10

T_SSD = 256
NG_SSD = 2
T_TOK = 128
T_OUT = 256
T_ROW = 512
TM_MM = 1024
TK_DW = 1024
VMEM_CAP = 60 * 1024 * 1024
MESH = pl.DeviceIdType.MESH


def _cparams(sem, est_bytes):
    lim = int(min(VMEM_CAP, max(32 * 1024 * 1024, est_bytes + 12 * 1024 * 1024)))
    return pltpu.CompilerParams(dimension_semantics=sem, vmem_limit_bytes=lim)


def _c(x):
    return x.astype(MXU_DTYPE)


def _dot(a, b):
    return jnp.dot(a, b, preferred_element_type=F32)


def _dot_nt(a, b):
    return lax.dot_general(a, b, (((1,), (1,)), ((), ())), preferred_element_type=F32)


def _dot_tn(a, b):
    return lax.dot_general(a, b, (((0,), (0,)), ((), ())), preferred_element_type=F32)


def _split(x, n):
    parts, r = [], x
    for _ in range(n):
        p = _c(r)
        parts.append(p)
        r = r - p.astype(F32)
    return parts


def _dot01_l(m01, x, n):
    acc = None
    for p in _split(x, n):
        t = _dot(m01, p)
        acc = t if acc is None else acc + t
    return acc


def _dot01_r(x, m01, n):
    acc = None
    for p in _split(x, n):
        t = _dot(p, m01)
        acc = t if acc is None else acc + t
    return acc


def _sigmoid(x):
    return 1.0 / (1.0 + jnp.exp(-x))


def _fold8(x):
    r, w = x.shape
    return jnp.sum(x.reshape(r // 8, 8, w), axis=0)


def _iota(shape, dim):
    return lax.broadcasted_iota(jnp.int32, shape, dim)


def _ssd_masks():
    l64 = _iota((CHUNK, SSD_GW), 0)
    s64 = jnp.bitwise_and(_iota((CHUNK, SSD_GW), 1), CHUNK - 1)
    diag = l64 == s64
    causal = l64 >= s64
    row_last = l64 == CHUNK - 1
    r4 = lax.shift_right_logical(_iota((256, 256), 0), 6)
    c4 = lax.shift_right_logical(_iota((256, 256), 1), 6)
    mask4 = r4 == c4
    return diag, causal, row_last, mask4


def _cumsum_mats(t):
    r, c = _iota((t, t), 0), _iota((t, t), 1)
    same = lax.shift_right_logical(r, 6) == lax.shift_right_logical(c, 6)
    tri = _c(jnp.where(same, jnp.where(c <= r, 1.0, 0.0), 0.0))
    trit = _c(jnp.where(same, jnp.where(c >= r, 1.0, 0.0), 0.0))
    return tri, trit


def _ssd_common(xs, bm, cm, dt, acs, masks):
    diag, causal, row_last, mask4 = masks
    row_e = jnp.sum(jnp.where(diag, acs, 0.0), axis=0, keepdims=True)
    seg = acs - row_e
    lm = jnp.exp(jnp.where(causal, seg, -1e30))
    bb, cb = _c(bm), _c(cm)
    brep = jnp.concatenate([bb] * 8, axis=0)
    cbrep = _dot_nt(cb, brep)
    m = cbrep * lm
    xdt = xs * dt
    acs_last = jnp.sum(jnp.where(row_last, acs, 0.0), axis=0, keepdims=True)
    dec = jnp.exp(acs_last - acs)
    eacs = jnp.exp(acs)
    cd = jnp.exp(acs_last)
    return dict(lm=lm, bb=bb, cb=cb, brep=brep, m=m, xdt=xdt, dec=dec, eacs=eacs, cd=cd)


def _blockdiag4(xb, mask4):
    return jnp.where(mask4, jnp.concatenate([xb] * 4, axis=0), jnp.zeros((), xb.dtype))


def _ssd_chunk_fwd(xs, bm, cm, dt, acs, d_skip, ht, masks):
    q = _ssd_common(xs, bm, cm, dt, acs, masks)
    mask4 = masks[3]
    mb, xdtb = _c(q["m"]), _c(q["xdt"])
    yd = []
    for blk in range(2):
        sl = slice(256 * blk, 256 * blk + 256)
        yd.append(_dot(mb[:, sl], _blockdiag4(xdtb[:, sl], mask4)))
    y_diag = jnp.concatenate(yd, axis=1)
    p = _dot(q["cb"], _c(ht))
    y = y_diag + p * q["eacs"] + xs * d_skip
    st = _dot_tn(q["bb"], _c(q["xdt"] * q["dec"]))
    return y, ht * q["cd"] + st


def _ssd_chunk_bwd(xs, bm, cm, dt, acs, d_skip, hprev, dht, dy, masks):
    diag, causal, row_last, mask4 = masks
    q = _ssd_common(xs, bm, cm, dt, acs, masks)
    lm, bb, cb, brep, m, xdt, dec, eacs, cd = (q[k] for k in ("lm", "bb", "cb", "brep", "m", "xdt", "dec", "eacs", "cd"))
    hb = _c(hprev)
    yoff = _dot(cb, hb) * eacs
    dyb = _c(dy)
    dpb = _c(dy * eacs)
    d_c = _dot_nt(dpb, hb)
    dh_y = _dot_tn(cb, dpb)
    mb, xdtb = _c(m), _c(xdt)
    dm_parts, dxdt_parts = [], []
    for blk in range(2):
        sl = slice(256 * blk, 256 * blk + 256)
        bd = _blockdiag4(xdtb[:, sl], mask4)
        dm_parts.append(_dot_nt(dyb[:, sl], bd))
        dxf = jnp.where(mask4, _dot_tn(mb[:, sl], dyb[:, sl]), 0.0)
        dxdt_parts.append(dxf[0:64] + dxf[64:128] + dxf[128:192] + dxf[192:256])
    dm = jnp.concatenate(dm_parts, axis=1)
    dxdt = jnp.concatenate(dxdt_parts, axis=1)
    dcbb = _c(dm * lm)
    g = dm * m
    d_c = d_c + _dot(dcbb, brep)
    dbrep = _dot_tn(dcbb, cb)
    d_b = dbrep[0:64]
    for r in range(1, 8):
        d_b = d_b + dbrep[64 * r:64 * r + 64]
    dhtb = _c(dht)
    dxd = _dot(bb, dhtb)
    xd = xdt * dec
    dxdt = dxdt + dxd * dec
    tq = dxd * xd
    d_b = d_b + _dot_nt(_c(xd), dhtb)
    dcd = jnp.sum(dht * hprev, axis=0, keepdims=True)
    col_g = jnp.sum(g, axis=0, keepdims=True)
    last = jnp.sum(tq, axis=0, keepdims=True) + dcd * cd
    qq = g - jnp.where(diag, col_g, 0.0) + dy * yoff - tq + jnp.where(row_last, last, 0.0)
    dxs = dxdt * dt + dy * d_skip
    return dxs, d_b, d_c, dht * cd + dh_y, dy * xs, qq, dxdt * xs


def _ssd_finish_dt(qq, p1, dt, sig, a_neg, trit, mask4):
    bd4 = _c(jnp.where(mask4, 1.0, 0.0))
    dacs = jnp.concatenate([_dot01_r(qq[:, 256 * b:256 * b + 256], bd4, 2) for b in range(2)], axis=1)
    da = _dot01_l(trit, dacs, 2)
    ddt = p1 + da * (a_neg * (1.0 / HEADDIM))
    return ddt * sig, da * dt


def _softplus(x):
    return jnp.maximum(x, 0.0) + jnp.log1p(jnp.exp(-jnp.abs(x)))


def _conv_taps(xpad, t):
    taps = []
    for k in range(4):
        sh = 3 - k
        v = xpad if sh == 0 else pltpu.roll(xpad, sh, 0)
        taps.append(v[8:8 + t])
    return taps


def _mm(a, b, *, tm, tn, tk, name, out_dtype=F32, col0=0, n=None, b_is_t=False):
    m, k = a.shape
    n = b.shape[0 if b_is_t else 1] if n is None else n
    nk = k // tk
    assert m % tm == 0 and n % tn == 0 and k % tk == 0, (a.shape, b.shape, tm, tn, tk)
    assert nk == 1 or out_dtype == F32
    dot = _dot_nt if b_is_t else _dot

    def body(a_ref, b_ref, o_ref):
        if nk == 1:
            o_ref[...] = dot(a_ref[...], b_ref[...]).astype(out_dtype)
        else:
            @pl.when(pl.program_id(2) == 0)
            def _():
                o_ref[...] = jnp.zeros_like(o_ref)

            o_ref[...] += dot(a_ref[...], b_ref[...])

    isz = jnp.dtype(a.dtype).itemsize
    est = 2 * (tm * tk + tk * tn) * isz + 2 * tm * tn * 4
    return pl.pallas_call(
        body, name=name,
        grid=(m // tm, n // tn, nk),
        in_specs=[pl.BlockSpec((tm, tk), lambda i, j, kk: (i, kk)),
                  pl.BlockSpec((tn, tk), lambda i, j, kk: (j + col0, kk)) if b_is_t
                  else pl.BlockSpec((tk, tn), lambda i, j, kk: (kk, j + col0))],
        out_specs=pl.BlockSpec((tm, tn), lambda i, j, kk: (i, j)),
        out_shape=jax.ShapeDtypeStruct((m, n), out_dtype),
        compiler_params=_cparams(("parallel", "parallel", "arbitrary"), est),
    )(a, b)


def _dw_groups(xnt, dpb, *, tk):
    m, k = xnt.shape
    nk = k // tk

    def body(a_ref, b_ref, zb_ref, xs_ref, bm_ref, cm_ref, dt_ref):
        outs = ((zb_ref, 0, 512), (xs_ref, 512, 1024), (bm_ref, 1024, 1152), (cm_ref, 1152, 1280), (dt_ref, 1280, 1408))

        @pl.when(pl.program_id(1) == 0)
        def _():
            for o_ref, _, _ in outs:
                o_ref[...] = jnp.zeros_like(o_ref)

        d = _dot(a_ref[...], b_ref[...])
        for o_ref, lo, hi in outs:
            o_ref[...] += d[:, lo:hi]

    isz = jnp.dtype(xnt.dtype).itemsize
    est = 2 * (m * tk + tk * BW_B) * isz + 3 * m * BW_B * 4
    piece = lambda w: pl.BlockSpec((m, w), lambda g, kk: (0, g))
    return pl.pallas_call(
        body, name="dw_in_b",
        grid=(SSD_GROUPS, nk),
        in_specs=[pl.BlockSpec((m, tk), lambda g, kk: (0, kk)), pl.BlockSpec((tk, BW_B), lambda g, kk: (kk, g))],
        out_specs=[piece(512), piece(512), piece(128), piece(128), piece(128)],
        out_shape=[jax.ShapeDtypeStruct((m, w), F32) for w in (2048, 2048, 512, 512, 512)],
        compiler_params=_cparams(("parallel", "arbitrary"), est),
    )(xnt, dpb)


def _dx_rs_call(dpa, wta, dpb, wtb, sb_in, sb_out, chip_v, *, tm):
    s = dpa.shape[0]
    tka, tkb = 1024, BW_B
    nka, nkb = dpa.shape[1] // tka, dpb.shape[1] // tkb
    ni, nk = s // tm, nka + nkb

    def body(a_ref, wa_ref, b_ref, wb_ref, sbin, sbout, cv, o_ref, rc_in, rc_out, abs_v, send, recv):
        i, kk = pl.program_id(0), pl.program_id(1)

        def copies():
            x, y, c, me, others = _place()
            sends, recvs = [], []
            for j, chip in enumerate(others):
                kj = 2 * chip[0] + chip[1]
                to = (*chip, c)
                sends += [_remote(sbin.at[kj], rc_in.at[j], send, recv, j, to),
                          _remote(sbout.at[kj], rc_out.at[j], send, recv, 3 + j, to),
                          _remote(cv, abs_v.at[me], send, recv, 6 + j, to)]
                recvs += [sends[-3], sends[-2], _remote(cv, abs_v.at[kj], send, recv, 6 + j, to)]
            return sends, recvs

        @pl.when((i == 0) & (kk == 0))
        def _():
            for cp in copies()[0]:
                cp.start()

        @pl.when(kk == 0)
        def _():
            o_ref[...] = jnp.zeros_like(o_ref)

        @pl.when(kk < nka)
        def _():
            o_ref[...] += _dot(a_ref[...], wa_ref[...])

        @pl.when(kk >= nka)
        def _():
            o_ref[...] += _dot(b_ref[...], wb_ref[...])

        @pl.when((i == ni - 1) & (kk == nk - 1))
        def _():
            sends, recvs = copies()
            for cp in recvs:
                cp.wait_recv()
            for cp in sends:
                cp.wait_send()

    isz = jnp.dtype(dpa.dtype).itemsize
    est = 2 * isz * (tm * tka + tka * D_MODEL + tm * tkb + tkb * D_MODEL) + 2 * tm * D_MODEL * 4
    outs = [jax.ShapeDtypeStruct((s, D_MODEL), F32),
            jax.ShapeDtypeStruct((3,) + sb_in.shape[1:], sb_in.dtype), jax.ShapeDtypeStruct((3,) + sb_out.shape[1:], sb_out.dtype),
            jax.ShapeDtypeStruct((4,) + chip_v.shape, F32)]
    return pl.pallas_call(
        body, name="dx_matmul_rs_chips",
        grid=(ni, nk),
        in_specs=[
            pl.BlockSpec((tm, tka), lambda i, kk: (i, jnp.minimum(kk, nka - 1))),
            pl.BlockSpec((tka, D_MODEL), lambda i, kk: (jnp.minimum(kk, nka - 1), 0)),
            pl.BlockSpec((tm, tkb), lambda i, kk: (i, jnp.maximum(kk - nka, 0))),
            pl.BlockSpec((tkb, D_MODEL), lambda i, kk: (jnp.maximum(kk - nka, 0), 0)),
            ANY, ANY, ANY,
        ],
        out_specs=[pl.BlockSpec((tm, D_MODEL), lambda i, kk: (i, 0)), ANY, ANY, ANY],
        out_shape=outs,
        scratch_shapes=[pltpu.SemaphoreType.DMA((9,)), pltpu.SemaphoreType.DMA((9,))],
        compiler_params=_cparams(("arbitrary", "arbitrary"), est),
    )(dpa, wta, dpb, wtb, sb_in, sb_out, chip_v)


def _gradx_call(x, dxn, dh, norm_w, tm):
    s = x.shape[0]

    def body(x_ref, g_ref, dh_ref, w_ref, gx_ref, dw_ref):
        @pl.when(pl.program_id(0) == 0)
        def _():
            dw_ref[...] = jnp.zeros_like(dw_ref)

        xv, gv = x_ref[...], g_ref[...]
        r = lax.rsqrt(jnp.mean(xv * xv, axis=-1, keepdims=True) + EPS)
        gw = gv * w_ref[...]
        gx_ref[...] = r * gw - xv * (r * r * r) * jnp.mean(xv * gw, axis=-1, keepdims=True) + dh_ref[...]
        dw_ref[...] += _fold8(gv * (xv * r))

    row = pl.BlockSpec((tm, D_MODEL), lambda i: (i, 0))
    return pl.pallas_call(
        body, name="grad_x",
        grid=(s // tm,),
        in_specs=[row, row, row, pl.BlockSpec((1, D_MODEL), lambda i: (0, 0))],
        out_specs=[row, pl.BlockSpec((8, D_MODEL), lambda i: (0, 0))],
        out_shape=[jax.ShapeDtypeStruct((s, D_MODEL), F32), jax.ShapeDtypeStruct((8, D_MODEL), F32)],
        compiler_params=_cparams(("arbitrary",), 2 * tm * D_MODEL * 16),
    )(x, dxn, dh, norm_w)


def _layernorm_stats(v):
    mu = jnp.mean(v, axis=-1, keepdims=True)
    vc = v - mu
    var = jnp.mean(vc * vc, axis=-1, keepdims=True)
    return vc * lax.rsqrt(var + EPS), lax.rsqrt(var + EPS)


def _tok_fwd_call(proj_a, y_b, gate_b, sgu_g, sgu_beta, wm, bias_full, t):
    s = proj_a.shape[0]

    def body(pa_ref, yb_ref, gb_ref, g_ref, be_ref, wm_ref, bf_ref, ya_ref, mg_ref, mgt_ref, mix_ref):
        u = pa_ref[:, 0:2048].astype(F32)
        v = pa_ref[:, 2048:4096].astype(F32)
        za = pa_ref[:, 4096:6144].astype(F32)
        xhat, _ = _layernorm_stats(v)
        vnb = _c(xhat * g_ref[...] + be_ref[...])
        for gi in range(SGU_GROUPS):
            sl = slice(128 * gi, 128 * gi + 128)
            mix_ref[:, sl] = _dot(wm_ref[gi], vnb[:, sl])
        mixed = mix_ref[...] + bf_ref[...]
        y_a = u * mixed * (za * _sigmoid(za))
        g0 = _sigmoid(pa_ref[:, 6144:8192].astype(F32) + gb_ref[:, 0:2048])
        g1 = _sigmoid(pa_ref[:, 8192:10240].astype(F32) + gb_ref[:, 2048:4096])
        merged = g0 * y_a + g1 * yb_ref[...].astype(F32)
        ya_ref[...] = _c(y_a)
        mg_ref[...] = _c(merged)
        mgt_ref[...] = _c(merged.T)

    row = pl.BlockSpec((t, D_MODEL), lambda i: (i, 0))
    vec = lambda w: pl.BlockSpec((1, w), lambda i: (0, 0))
    return pl.pallas_call(
        body, name="tok_fwd",
        grid=(s // t,),
        in_specs=[pl.BlockSpec((t, NA), lambda i: (i, 0)), row, vec(4096), vec(2048), vec(2048),
                  pl.BlockSpec((SGU_GROUPS, 128, 128), lambda i: (0, 0, 0)), pl.BlockSpec((128, D_MODEL), lambda i: (0, 0))],
        out_specs=[row, row, pl.BlockSpec((D_MODEL, t), lambda i: (0, i))],
        out_shape=[jax.ShapeDtypeStruct((s, D_MODEL), MXU_DTYPE), jax.ShapeDtypeStruct((s, D_MODEL), MXU_DTYPE),
                   jax.ShapeDtypeStruct((D_MODEL, s), MXU_DTYPE)],
        scratch_shapes=[pltpu.VMEM((t, D_MODEL), F32)],
        compiler_params=_cparams(("parallel",), 2 * t * NA * 4 + 12 * t * D_MODEL * 4),
    )(proj_a, y_b, gate_b, sgu_g, sgu_beta, wm, bias_full)


def _tok_bwd_call(proj_a, dmerged, y_a, y_b, gate_b, sgu_g, sgu_beta, wm, wmt, bias_full, t):
    s = proj_a.shape[0]

    def body(pa_ref, dm_ref, ya_ref, yb_ref, gb_ref, g_ref, be_ref, wm_ref, wmt_ref, bf_ref,
             dpa_ref, dyb_ref, dgb_ref, dgam_ref, dbeta_ref, dbf_ref, dws_ref, mix_ref, dvn_ref):
        @pl.when(pl.program_id(0) == 0)
        def _():
            dgb_ref[...] = jnp.zeros_like(dgb_ref)
            dgam_ref[...] = jnp.zeros_like(dgam_ref)
            dbeta_ref[...] = jnp.zeros_like(dbeta_ref)
            dbf_ref[...] = jnp.zeros_like(dbf_ref)
            dws_ref[...] = jnp.zeros_like(dws_ref)

        u = pa_ref[:, 0:2048].astype(F32)
        v = pa_ref[:, 2048:4096].astype(F32)
        za = pa_ref[:, 4096:6144].astype(F32)
        xhat, rstd = _layernorm_stats(v)
        vnb = _c(xhat * g_ref[...] + be_ref[...])
        for gi in range(SGU_GROUPS):
            sl = slice(128 * gi, 128 * gi + 128)
            mix_ref[:, sl] = _dot(wm_ref[gi], vnb[:, sl])
        mixed = mix_ref[...] + bf_ref[...]
        sig = _sigmoid(za)
        sz = za * sig
        dm = dm_ref[...].astype(F32)
        y_a = ya_ref[...].astype(F32)
        g0 = _sigmoid(pa_ref[:, 6144:8192].astype(F32) + gb_ref[:, 0:2048])
        g1 = _sigmoid(pa_ref[:, 8192:10240].astype(F32) + gb_ref[:, 2048:4096])
        dgl0 = dm * y_a * g0 * (1.0 - g0)
        dgl1 = dm * yb_ref[...].astype(F32) * g1 * (1.0 - g1)
        dyb_ref[...] = _c(dm * g1)
        dya = dm * g0
        dpa_ref[:, 6144:8192] = _c(dgl0)
        dpa_ref[:, 8192:10240] = _c(dgl1)
        dgb_ref[:, 0:2048] += _fold8(dgl0)
        dgb_ref[:, 2048:4096] += _fold8(dgl1)
        dpa_ref[:, 0:2048] = _c(dya * mixed * sz)
        dpa_ref[:, 4096:6144] = _c(dya * (u * mixed) * (sig * (1.0 + za * (1.0 - sig))))
        dmixed = dya * u * sz
        dbf_ref[...] += dmixed
        dmb = _c(dmixed)
        for gi in range(SGU_GROUPS):
            sl = slice(128 * gi, 128 * gi + 128)
            dvn_ref[:, sl] = _dot(wmt_ref[gi], dmb[:, sl])
            dws_ref[gi] += _dot_nt(dmb[:, sl], vnb[:, sl])
        dvn = dvn_ref[...]
        dgam_ref[...] += _fold8(dvn * xhat)
        dbeta_ref[...] += _fold8(dvn)
        dxh = dvn * g_ref[...]
        dv = rstd * (dxh - jnp.mean(dxh, axis=-1, keepdims=True) - xhat * jnp.mean(dxh * xhat, axis=-1, keepdims=True))
        dpa_ref[:, 2048:4096] = _c(dv)

    row = pl.BlockSpec((t, D_MODEL), lambda i: (i, 0))
    vec = lambda w: pl.BlockSpec((1, w), lambda i: (0, 0))
    acc = lambda w: pl.BlockSpec((8, w), lambda i: (0, 0))
    wspec = pl.BlockSpec((SGU_GROUPS, 128, 128), lambda i: (0, 0, 0))
    return pl.pallas_call(
        body, name="tok_bwd",
        grid=(s // t,),
        in_specs=[pl.BlockSpec((t, NA), lambda i: (i, 0)), row, row, row, vec(4096), vec(2048), vec(2048),
                  wspec, wspec, pl.BlockSpec((128, D_MODEL), lambda i: (0, 0))],
        out_specs=[pl.BlockSpec((t, NA), lambda i: (i, 0)), row, acc(4096), acc(2048), acc(2048),
                   pl.BlockSpec((128, D_MODEL), lambda i: (0, 0)), wspec],
        out_shape=[jax.ShapeDtypeStruct((s, NA), MXU_DTYPE), jax.ShapeDtypeStruct((s, D_MODEL), MXU_DTYPE),
                   jax.ShapeDtypeStruct((8, 4096), F32), jax.ShapeDtypeStruct((8, 2048), F32),
                   jax.ShapeDtypeStruct((8, 2048), F32), jax.ShapeDtypeStruct((128, D_MODEL), F32),
                   jax.ShapeDtypeStruct((SGU_GROUPS, 128, 128), F32)],
        scratch_shapes=[pltpu.VMEM((t, D_MODEL), F32), pltpu.VMEM((t, D_MODEL), F32)],
        compiler_params=_cparams(("arbitrary",), 2 * t * NA * 6 + 16 * t * D_MODEL * 4),
    )(proj_a, dmerged, y_a, y_b, gate_b, sgu_g, sgu_beta, wm, wmt, bias_full)


def _out_call(merged, x, target, w_out, fnw, t):
    s = x.shape[0]
    nt = s // t

    def body(mg_ref, x_ref, t_ref, w_ref, fw_ref, dh_ref, dhb_ref, dmg_ref, loss_ref, dfw_ref):
        @pl.when(pl.program_id(0) == 0)
        def _():
            dfw_ref[...] = jnp.zeros_like(dfw_ref)

        h = x_ref[...] + _dot(mg_ref[...], w_ref[...])
        r = lax.rsqrt(jnp.mean(h * h, axis=-1, keepdims=True) + EPS)
        hn = h * r
        err = hn * fw_ref[...] - t_ref[...]
        loss_ref[...] = jnp.full(loss_ref.shape, 0.5 * jnp.sum(jnp.mean(err * err, axis=-1, keepdims=True)), F32)
        dy = err * (1.0 / D_MODEL)
        dfw_ref[...] += _fold8(dy * hn)
        gw = dy * fw_ref[...]
        dh = r * gw - h * (r * r * r) * jnp.mean(h * gw, axis=-1, keepdims=True)
        dh_ref[...] = dh
        dhb = _c(dh)
        dhb_ref[...] = dhb
        dmg_ref[...] = _c(_dot_nt(dhb, w_ref[...]))

    row = pl.BlockSpec((t, D_MODEL), lambda i: (i, 0))
    return pl.pallas_call(
        body, name="out_proj_loss",
        grid=(nt,),
        in_specs=[row, row, row, pl.BlockSpec((D_MODEL, D_MODEL), lambda i: (0, 0)), pl.BlockSpec((1, D_MODEL), lambda i: (0, 0))],
        out_specs=[row, row, row, pl.BlockSpec((1, 8, 128), lambda i: (i, 0, 0)), pl.BlockSpec((8, D_MODEL), lambda i: (0, 0))],
        out_shape=[jax.ShapeDtypeStruct((s, D_MODEL), F32), jax.ShapeDtypeStruct((s, D_MODEL), MXU_DTYPE),
                   jax.ShapeDtypeStruct((s, D_MODEL), MXU_DTYPE), jax.ShapeDtypeStruct((nt, 8, 128), F32),
                   jax.ShapeDtypeStruct((8, D_MODEL), F32)],
        compiler_params=_cparams(("arbitrary",), 2 * D_MODEL * D_MODEL * 2 + 2 * t * D_MODEL * 24),
    )(merged, x, target, w_out, fnw)


def _ssd_fwd_call(proj_b, dtb, alog, dsk, cw, cb, nw, t, ng):
    s = proj_b.shape[0]
    nt, nch = s // t, t // CHUNK

    def body(pb_ref, halo_ref, dtb_ref, al_ref, ds_ref, cw_ref, cb_ref, nw_ref, y_ref, yb_ref, hp_ref, pre_ref,
             dt_ref, acs_ref, ht_ref, prev_ref):
        i = pl.program_id(1)

        @pl.when(i == 0)
        def _():
            ht_ref[...] = jnp.zeros_like(ht_ref)

        for gi in range(ng):
            prev_ref[:, 768 * gi:768 * gi + 768] = jnp.where(i == 0, 0.0, halo_ref[:, FW_B * gi:FW_B * gi + 768])
        masks = _ssd_masks()
        tri, _ = _cumsum_mats(CHUNK)
        a_neg = -jnp.exp(al_ref[...])

        def chunk(c, carry):
            rows = pl.ds(pl.multiple_of(c * CHUNK, CHUNK), CHUNK)
            for gi in range(ng):
                fo, co, go = FW_B * gi, 768 * gi, SSD_GW * gi
                xbc = pb_ref[rows, fo:fo + 768]
                taps = _conv_taps(jnp.concatenate([prev_ref[:, co:co + 768], xbc], axis=0), CHUNK)
                prev_ref[:, co:co + 768] = xbc[CHUNK - 8:CHUNK]
                pre = cb_ref[:, co:co + 768]
                for k in range(4):
                    pre = pre + taps[k] * cw_ref[k:k + 1, co:co + 768]
                pre_ref[rows, co:co + 768] = pre
                act = pre * _sigmoid(pre)
                dt = _softplus(pb_ref[rows, fo + 768:fo + 1280] + dtb_ref[:, go:go + 512])
                acs = _dot01_l(tri, dt * a_neg[:, go:go + 512], 3)
                dt_ref[rows, go:go + 512] = dt
                acs_ref[rows, go:go + 512] = acs
                ht = ht_ref[gi]
                hp_ref[c, :, go:go + 512] = ht
                y, ht_new = _ssd_chunk_fwd(act[:, 0:512], act[:, 512:640], act[:, 640:768], dt, acs,
                                           ds_ref[:, go:go + 512], ht, masks)
                y_ref[rows, go:go + 512] = y
                ht_ref[gi] = ht_new
                zb = pb_ref[rows, fo + 1280:fo + 1792]
                hh = y * (zb * _sigmoid(zb))
                rr = lax.rsqrt(jnp.mean(hh * hh, axis=-1, keepdims=True) + EPS)
                yb_ref[rows, go:go + 512] = _c(hh * rr * nw_ref[:, go:go + 512])
            return carry

        lax.fori_loop(0, nch, chunk, 0)

    gvec = lambda w: pl.BlockSpec((1, ng * w), lambda g, i: (0, g))
    return pl.pallas_call(
        body, name="ssd_fwd",
        grid=(SSD_GROUPS // ng, nt),
        in_specs=[pl.BlockSpec((t, ng * FW_B), lambda g, i: (i, g)),
                  pl.BlockSpec((8, ng * FW_B), lambda g, i: (jnp.maximum(i * (t // 8) - 1, 0), g)),
                  gvec(512), gvec(512), gvec(512),
                  pl.BlockSpec((4, ng * 768), lambda g, i: (0, g)), gvec(768), gvec(512)],
        out_specs=[pl.BlockSpec((t, ng * SSD_GW), lambda g, i: (i, g)), pl.BlockSpec((t, ng * SSD_GW), lambda g, i: (i, g)),
                   pl.BlockSpec((nch, SSD_STATE, ng * SSD_GW), lambda g, i: (i, 0, g)),
                   pl.BlockSpec((t, ng * 768), lambda g, i: (i, g)),
                   pl.BlockSpec((t, ng * SSD_GW), lambda g, i: (i, g)), pl.BlockSpec((t, ng * SSD_GW), lambda g, i: (i, g))],
        out_shape=[jax.ShapeDtypeStruct((s, D_MODEL), F32), jax.ShapeDtypeStruct((s, D_MODEL), MXU_DTYPE),
                   jax.ShapeDtypeStruct((s // CHUNK, SSD_STATE, D_MODEL), F32),
                   jax.ShapeDtypeStruct((s, SSD_GROUPS * 768), F32),
                   jax.ShapeDtypeStruct((s, D_MODEL), F32), jax.ShapeDtypeStruct((s, D_MODEL), F32)],
        scratch_shapes=[pltpu.VMEM((ng, SSD_STATE, SSD_GW), F32), pltpu.VMEM((8, ng * 768), F32)],
        compiler_params=_cparams(("parallel", "arbitrary"), ng * (2 * t * FW_B * 4 + 16 * t * SSD_GW * 4) + 16 * 1024 * 1024),
    )(proj_b, proj_b, dtb, alog, dsk, cw, cb, nw)


def _ssd_bwd_call(proj_b, pre_all, dt_all, acs_all, dyb, y, hprev, dtb, alog, dsk, cw, nw, t, ng):
    s = proj_b.shape[0]
    nt, nch = s // t, t // CHUNK

    def body(pb_ref, pre_ref, dt_ref, acs_ref, dyb_ref, y_ref, hp_ref, dtb_ref, al_ref, ds_ref, cw_ref, nw_ref,
             dpb_ref, a512_ref, a768_ref, dht_ref, nxt_ref, q_ref, p1_ref):
        i = pl.program_id(1)

        @pl.when(i == 0)
        def _():
            dht_ref[...] = jnp.zeros_like(dht_ref)
            nxt_ref[...] = jnp.zeros_like(nxt_ref)
            a512_ref[...] = jnp.zeros_like(a512_ref)
            a768_ref[...] = jnp.zeros_like(a768_ref)

        _, trit = _cumsum_mats(t)
        a_neg = -jnp.exp(al_ref[...])
        masks = _ssd_masks()

        def chunk(cc, carry):
            c = nch - 1 - cc
            rows = pl.ds(pl.multiple_of(c * CHUNK, CHUNK), CHUNK)
            for gi in range(ng):
                fo, co, go, bo = FW_B * gi, 768 * gi, SSD_GW * gi, BW_B * gi
                pre = pre_ref[rows, co:co + 768]
                sp = _sigmoid(pre)
                act = pre * sp
                zb = pb_ref[rows, fo + 1280:fo + 1792]
                yv = y_ref[rows, go:go + 512]
                sgz = _sigmoid(zb)
                sz = zb * sgz
                hh = yv * sz
                rr = lax.rsqrt(jnp.mean(hh * hh, axis=-1, keepdims=True) + EPS)
                dyb = dyb_ref[rows, go:go + 512].astype(F32)
                a512_ref[gi, 0] += _fold8(dyb * (hh * rr))
                tt = dyb * nw_ref[:, go:go + 512]
                dhh = rr * tt - hh * (rr * rr * rr) * jnp.mean(hh * tt, axis=-1, keepdims=True)
                dpb_ref[rows, bo:bo + 512] = _c(dhh * yv * (sgz * (1.0 + zb * (1.0 - sgz))))
                dxs, d_b, d_c, dht_prev, dyxs, qq, p1 = _ssd_chunk_bwd(
                    act[:, 0:512], act[:, 512:640], act[:, 640:768], dt_ref[rows, go:go + 512], acs_ref[rows, go:go + 512],
                    ds_ref[:, go:go + 512], hp_ref[c, :, go:go + 512], dht_ref[gi], dhh * sz, masks)
                dht_ref[gi] = dht_prev
                q_ref[rows, go:go + 512] = qq
                p1_ref[rows, go:go + 512] = p1
                a512_ref[gi, 1] += _fold8(dyxs)
                dpre = jnp.concatenate([dxs, d_b, d_c], axis=1) * (sp * (1.0 + pre * (1.0 - sp)))
                xbc = pb_ref[rows, fo:fo + 768]
                a768_ref[gi, 4] += _fold8(dpre)
                a768_ref[gi, 3] += _fold8(dpre * xbc)
                dpad = jnp.concatenate([dpre, nxt_ref[:, co:co + 768]], axis=0)
                dx = dpre * cw_ref[3:4, co:co + 768]
                for k in range(3):
                    d_k = pltpu.roll(dpad, CHUNK + 8 - (3 - k), 0)[0:CHUNK]
                    dx = dx + d_k * cw_ref[k:k + 1, co:co + 768]
                    a768_ref[gi, k] += _fold8(d_k * xbc)
                nxt_ref[:, co:co + 768] = dpre[0:8]
                dpb_ref[rows, bo + 512:bo + 1280] = _c(dx)
            return carry

        lax.fori_loop(0, nch, chunk, 0)

        rsel = _c(jnp.where(lax.shift_right_logical(_iota((SSD_GW, 128), 0), 6) == _iota((SSD_GW, 128), 1), 1.0, 0.0))
        for gi in range(ng):
            fo, go, bo = FW_B * gi, SSD_GW * gi, BW_B * gi
            sig = _sigmoid(pb_ref[:, fo + 768:fo + 1280] + dtb_ref[:, go:go + 512])
            ddtr, dadt = _ssd_finish_dt(q_ref[:, go:go + 512], p1_ref[:, go:go + 512], dt_ref[:, go:go + 512], sig,
                                        a_neg[:, go:go + 512], trit, masks[3])
            dpb_ref[:, bo + 1280:bo + 1408] = _c(_dot01_r(ddtr, rsel, 2))
            a512_ref[gi, 2] += _fold8(dadt)
            a512_ref[gi, 3] += _fold8(ddtr)

    gvec = lambda w: pl.BlockSpec((1, ng * w), lambda g, i: (0, g))
    rev = lambda w: pl.BlockSpec((t, ng * w), lambda g, i: (nt - 1 - i, g))
    return pl.pallas_call(
        body, name="ssd_bwd",
        grid=(SSD_GROUPS // ng, nt),
        in_specs=[rev(FW_B), rev(768), rev(SSD_GW), rev(SSD_GW), rev(SSD_GW), rev(SSD_GW),
                  pl.BlockSpec((nch, SSD_STATE, ng * SSD_GW), lambda g, i: (nt - 1 - i, 0, g)),
                  gvec(512), gvec(512), gvec(512),
                  pl.BlockSpec((4, ng * 768), lambda g, i: (0, g)), gvec(512)],
        out_specs=[rev(BW_B),
                   pl.BlockSpec((ng, 4, 8, 512), lambda g, i: (g, 0, 0, 0)),
                   pl.BlockSpec((ng, 5, 8, 768), lambda g, i: (g, 0, 0, 0))],
        out_shape=[jax.ShapeDtypeStruct((s, SSD_GROUPS * BW_B), MXU_DTYPE),
                   jax.ShapeDtypeStruct((SSD_GROUPS, 4, 8, 512), F32),
                   jax.ShapeDtypeStruct((SSD_GROUPS, 5, 8, 768), F32)],
        scratch_shapes=[pltpu.VMEM((ng, SSD_STATE, SSD_GW), F32), pltpu.VMEM((8, ng * 768), F32),
                        pltpu.VMEM((t, ng * SSD_GW), F32), pltpu.VMEM((t, ng * SSD_GW), F32)],
        compiler_params=_cparams(("parallel", "arbitrary"), ng * (2 * t * FW_B * 4 + 18 * t * SSD_GW * 4) + 16 * 1024 * 1024),
    )(proj_b, pre_all, dt_all, acs_all, dyb, y, hprev, dtb, alog, dsk, cw, nw)


def _rows_call(body, ins, outs, tr, name):
    r = ins[0].shape[0]
    spec = lambda a: pl.BlockSpec((tr, a.shape[1]), lambda i: (i, 0))
    est = 2 * tr * sum(a.shape[1] * jnp.dtype(a.dtype).itemsize for a in list(ins) + list(outs))
    return pl.pallas_call(
        body, name=name, grid=(r // tr,),
        in_specs=[spec(a) for a in ins], out_specs=[spec(o) for o in outs], out_shape=list(outs),
        compiler_params=_cparams(("parallel",), est),
    )(*ins)


def _add_pair(a, b, tr, name):
    def body(a_ref, b_ref, o_ref):
        o_ref[...] = a_ref[...] + b_ref[...]

    return _rows_call(body, [a, b], [jax.ShapeDtypeStruct(a.shape, F32)], tr, name)[0]


def _rs_add(p, sib, place, tr, name):
    _, r, c = p.shape
    half = r // 2
    nb = half // tr

    def body(pl_ref, p_ref, s_ref, b_ref, own_ref):
        v = p_ref[0] + s_ref[0]
        b_ref[0] = v.astype(jnp.bfloat16)

        @pl.when(pl.program_id(1) == pl_ref[0])
        def _():
            own_ref[...] = v

    return pl.pallas_call(
        body, name=name,
        grid_spec=pltpu.PrefetchScalarGridSpec(
            num_scalar_prefetch=1, grid=(nb, 4),
            in_specs=[pl.BlockSpec((1, tr, c), lambda i, k, pr: (k, pr[1] * nb + i, 0)),
                      pl.BlockSpec((1, tr, c), lambda i, k, pr: (k, i, 0))],
            out_specs=[pl.BlockSpec((1, tr, c), lambda i, k, pr: (k, i, 0)),
                       pl.BlockSpec((tr, c), lambda i, k, pr: (i, 0))]),
        out_shape=[jax.ShapeDtypeStruct((4, half, c), jnp.bfloat16), jax.ShapeDtypeStruct((half, c), F32)],
        compiler_params=_cparams(("parallel", "arbitrary"), 2 * tr * c * 14),
    )(place, p, sib)


def _sum_own_recv(own, recv, tr, name):
    r, c = own.shape

    def body(o_ref, r_ref, out_ref):
        v = o_ref[...]
        for j in range(3):
            v = v + r_ref[j].astype(F32)
        out_ref[...] = v

    return pl.pallas_call(
        body, name=name, grid=(r // tr,),
        in_specs=[pl.BlockSpec((tr, c), lambda i: (i, 0)), pl.BlockSpec((3, tr, c), lambda i: (0, i, 0))],
        out_specs=pl.BlockSpec((tr, c), lambda i: (i, 0)),
        out_shape=jax.ShapeDtypeStruct((r, c), F32),
        compiler_params=_cparams(("parallel",), 2 * tr * c * 14),
    )(own, recv)


def _sum_slots(stack, name):
    n, r, w = stack.shape

    def body(a_ref, out_ref):
        v = a_ref[0]
        for k in range(1, n):
            v = v + a_ref[k]
        out_ref[...] = v

    return pl.pallas_call(
        body, name=name, grid=(1,),
        in_specs=[pl.BlockSpec((n, r, w), lambda i: (0, 0, 0))],
        out_specs=pl.BlockSpec((r, w), lambda i: (0, 0)),
        out_shape=jax.ShapeDtypeStruct((r, w), F32),
        compiler_params=_cparams(("arbitrary",), 2 * (n + 1) * r * w * 4),
    )(stack)


def _adamw(w, g, m, v, tr, name):
    def body(w_ref, g_ref, m_ref, v_ref, d_ref, nm_ref, nv_ref):
        d_ref[...], nm_ref[...], nv_ref[...] = _adam_math(w_ref[...], g_ref[...], m_ref[...], v_ref[...])

    o = jax.ShapeDtypeStruct(w.shape, F32)
    return _rows_call(body, [w, g, m, v], [o, o, o], tr, name)


def _adam_math(w, g, m, v):
    nm = ADAM_B1 * m + (1.0 - ADAM_B1) * g
    nv = ADAM_B2 * v + (1.0 - ADAM_B2) * (g * g)
    m_hat = nm / (1.0 - ADAM_B1 ** ADAM_STEP)
    v_hat = nv / (1.0 - ADAM_B2 ** ADAM_STEP)
    return -ADAM_LR * (m_hat / (jnp.sqrt(v_hat) + ADAM_EPS) + ADAM_WD * w), nm, nv


def _adamw_halves(w, g_own, g_sib, m, v, place, tr, name):
    r, c = w.shape

    def body(pl_ref, w_ref, go_ref, gs_ref, m_ref, v_ref, g_ref, d_ref, nm_ref, nv_ref):
        first = pl_ref[1] == 0
        own, sib = go_ref[...], gs_ref[...]
        g = jnp.concatenate([jnp.where(first, own, sib), jnp.where(first, sib, own)], axis=1)
        g_ref[...] = g
        d_ref[...], nm_ref[...], nv_ref[...] = _adam_math(w_ref[...], g, m_ref[...], v_ref[...])

    full = pl.BlockSpec((tr, c), lambda i, pr: (i, 0))
    half = pl.BlockSpec((tr, c // 2), lambda i, pr: (i, 0))
    o = jax.ShapeDtypeStruct((r, c), F32)
    return pl.pallas_call(
        body, name=name,
        grid_spec=pltpu.PrefetchScalarGridSpec(num_scalar_prefetch=1, grid=(r // tr,),
                                               in_specs=[full, half, half, full, full], out_specs=[full] * 4),
        out_shape=[o] * 4,
        compiler_params=_cparams(("parallel",), 2 * tr * c * 4 * 8),
    )(place, w, g_own, g_sib, m, v)


ANY = pl.BlockSpec(memory_space=pl.ANY)


def _place():
    x, y, c = lax.axis_index("x"), lax.axis_index("y"), lax.axis_index("c")
    others = [(1 - x, y), (x, 1 - y), (1 - x, 1 - y)]
    return x, y, c, 2 * x + y, others


def _remote(src, dst, send, recv, k, to):
    return pltpu.make_async_remote_copy(src_ref=src, dst_ref=dst, send_sem=send.at[k], recv_sem=recv.at[k],
                                        device_id=to, device_id_type=MESH)


def _norm_gather_call(x, norm_w, win_b, wout_b, cw8, tm):
    s = x.shape[0]
    ni = s // tm

    def body(x_ref, w_ref, win, wout, cw, xn_ref, xnt_ref, g_in, g_out, g_cw, send, recv):
        i = pl.program_id(0)

        def direct():
            xx, yy, c, me, others = _place()
            cps = []
            for a, (src, dst) in enumerate(((win, g_in), (wout, g_out))):
                half = src.shape[0] // 2
                mine = pl.ds(c * half, half)
                cps += [_remote(src.at[mine], dst.at[me, mine], send, recv, 6 * a + j, (*chip, c)) for j, chip in enumerate(others)]
            cps += [_remote(cw, g_cw.at[me], send, recv, 12 + j, (*chip, c)) for j, chip in enumerate(others)]
            return cps

        @pl.when(i == 0)
        def _():
            for cp in direct():
                cp.start()

        xv = x_ref[...]
        r = lax.rsqrt(jnp.mean(xv * xv, axis=-1, keepdims=True) + EPS)
        xn = xv * r * w_ref[...]
        xn_ref[...] = _c(xn)
        xnt_ref[...] = _c(xn.T)

        @pl.when(i == ni - 1)
        def _():
            xx, yy, c, me, others = _place()
            sib = (xx, yy, 1 - c)
            passed = []
            for a, dst in enumerate((g_in, g_out)):
                half = dst.shape[1] // 2
                mine = pl.ds(c * half, half)
                for j, chip in enumerate(others):
                    kj = 2 * chip[0] + chip[1]
                    _remote(dst.at[kj, mine], dst.at[kj, mine], send, recv, 6 * a + j, (*chip, c)).wait_recv()
                    cp = _remote(dst.at[kj, mine], dst.at[kj, mine], send, recv, 6 * a + 3 + j, sib)
                    cp.start()
                    passed.append(cp)
            for a, dst in enumerate((g_in, g_out)):
                half = dst.shape[1] // 2
                theirs = pl.ds((1 - c) * half, half)
                for j, chip in enumerate(others):
                    kj = 2 * chip[0] + chip[1]
                    _remote(dst.at[kj, theirs], dst.at[kj, theirs], send, recv, 6 * a + 3 + j, sib).wait_recv()
            for j, chip in enumerate(others):
                kj = 2 * chip[0] + chip[1]
                _remote(cw, g_cw.at[kj], send, recv, 12 + j, (*chip, c)).wait_recv()
            for cp in direct() + passed:
                cp.wait_send()

    outs = [jax.ShapeDtypeStruct((s, D_MODEL), MXU_DTYPE), jax.ShapeDtypeStruct((D_MODEL, s), MXU_DTYPE)]
    outs += [jax.ShapeDtypeStruct((4,) + a.shape, a.dtype) for a in (win_b, wout_b, cw8)]
    return pl.pallas_call(
        body, name="rmsnorm_gather_weights",
        grid=(ni,),
        in_specs=[pl.BlockSpec((tm, D_MODEL), lambda i: (i, 0)), pl.BlockSpec((1, D_MODEL), lambda i: (0, 0)), ANY, ANY, ANY],
        out_specs=[pl.BlockSpec((tm, D_MODEL), lambda i: (i, 0)), pl.BlockSpec((D_MODEL, tm), lambda i: (0, i)), ANY, ANY, ANY],
        out_shape=outs,
        scratch_shapes=[pltpu.SemaphoreType.DMA((15,)), pltpu.SemaphoreType.DMA((15,))],
        compiler_params=_cparams(("arbitrary",), 2 * tm * D_MODEL * 12),
    )(x, norm_w, win_b, wout_b, cw8)


def _dw_out_rs_call(a, b, p_in, *, tn, tk):
    m, k = a.shape
    n = b.shape[1]
    nj, nk = n // tn, k // tk

    def body(a_ref, b_ref, pin, o_ref, sib_in, send, recv):
        j, kk = pl.program_id(0), pl.program_id(1)

        def copy():
            x, y, c, me, others = _place()
            half = pin.shape[1] // 2
            return _remote(pin.at[:, pl.ds((1 - c) * half, half)], sib_in, send, recv, 0, (x, y, 1 - c))

        @pl.when((j == 0) & (kk == 0))
        def _():
            copy().start()

        @pl.when(kk == 0)
        def _():
            o_ref[...] = jnp.zeros_like(o_ref)

        o_ref[...] += _dot(a_ref[...], b_ref[...])

        @pl.when((j == nj - 1) & (kk == nk - 1))
        def _():
            cp = copy()
            cp.wait_recv()
            cp.wait_send()

    isz = jnp.dtype(a.dtype).itemsize
    est = 2 * (m * tk + tk * tn) * isz + 2 * m * tn * 4
    outs = [jax.ShapeDtypeStruct((m, n), F32), jax.ShapeDtypeStruct((4, p_in.shape[1] // 2, p_in.shape[2]), p_in.dtype)]
    return pl.pallas_call(
        body, name="dw_out_rs_sibling",
        grid=(nj, nk),
        in_specs=[pl.BlockSpec((m, tk), lambda j, kk: (0, kk)), pl.BlockSpec((tk, tn), lambda j, kk: (kk, j)), ANY],
        out_specs=[pl.BlockSpec((m, tn), lambda j, kk: (0, j)), ANY],
        out_shape=outs,
        scratch_shapes=[pltpu.SemaphoreType.DMA((1,)), pltpu.SemaphoreType.DMA((1,))],
        compiler_params=_cparams(("arbitrary", "arbitrary"), est),
    )(a, b, p_in)


def _rs_sibling_call(p_out, vsmall):
    def body(pout, vs, sib_out, sib_v, send, recv):
        x, y, c, me, others = _place()
        sib = (x, y, 1 - c)
        half = pout.shape[1] // 2
        cps = [_remote(pout.at[:, pl.ds((1 - c) * half, half)], sib_out, send, recv, 0, sib),
               _remote(vs, sib_v, send, recv, 1, sib)]
        for cp in cps:
            cp.start()
        for cp in cps:
            cp.wait_recv()
        for cp in cps:
            cp.wait_send()

    outs = [jax.ShapeDtypeStruct((4, p_out.shape[1] // 2, p_out.shape[2]), p_out.dtype),
            jax.ShapeDtypeStruct(vsmall.shape, vsmall.dtype)]
    return pl.pallas_call(
        body, name="rs_sibling",
        in_specs=[ANY] * 2, out_specs=[ANY] * 2, out_shape=outs,
        scratch_shapes=[pltpu.SemaphoreType.DMA((2,)), pltpu.SemaphoreType.DMA((2,))],
    )(p_out, vsmall)


def _rs_join_call(f_in, f_out, nw8):
    def body(fin, fout, nw, sib_in, full_out, all_nw, send, recv):
        x, y, c, me, others = _place()
        sib = (x, y, 1 - c)
        half = fout.shape[0]
        cps = [_remote(fin, sib_in, send, recv, 0, sib),
               _remote(fout, full_out.at[pl.ds(c * half, half)], send, recv, 1, sib)]
        mine = 4 * x + 2 * y + c
        peers = []
        for r in range(1, 8):
            px, py, pc = (1 - x if r & 4 else x), (1 - y if r & 2 else y), (1 - c if r & 1 else c)
            peers.append((r, (px, py, pc), 4 * px + 2 * py + pc))
            cps.append(_remote(nw, all_nw.at[mine], send, recv, 1 + r, (px, py, pc)))
        for cp in cps:
            cp.start()
        cps[0].wait_recv()
        _remote(fout, full_out.at[pl.ds((1 - c) * half, half)], send, recv, 1, sib).wait_recv()
        for r, peer, idx in peers:
            _remote(nw, all_nw.at[idx], send, recv, 1 + r, peer).wait_recv()
        for cp in cps:
            cp.wait_send()

    outs = [jax.ShapeDtypeStruct(f_in.shape, F32), jax.ShapeDtypeStruct((2 * f_out.shape[0], f_out.shape[1]), F32),
            jax.ShapeDtypeStruct((8,) + nw8.shape, F32)]
    return pl.pallas_call(
        body, name="rs_join",
        in_specs=[ANY] * 3, out_specs=[ANY] * 3, out_shape=outs,
        scratch_shapes=[pltpu.SemaphoreType.DMA((9,)), pltpu.SemaphoreType.DMA((9,))],
    )(f_in, f_out, nw8)


def _pack(arrs):
    parts = []
    for a in arrs:
        f = a.reshape(-1).astype(F32)
        pad = (-f.shape[0]) % 1024
        parts.append(jnp.pad(f, (0, pad)).reshape(-1, 128))
    return jnp.concatenate(parts, axis=0)


def _unpack(packed, shapes):
    out, row = [], 0
    for shp in shapes:
        n = 1
        for d in shp:
            n *= d
        rows = (n + 1023) // 1024 * 8
        out.append(packed[row:row + rows].reshape(-1)[:n].reshape(shp))
        row += rows
    return out


def _expand_heads(v32):
    return jnp.repeat(v32.reshape(32), HEADDIM).reshape(1, D_MODEL)


def kernel(x, norm_w, w_in, gate_b, sgu_norm_g, sgu_norm_b, sgu_w, sgu_b, conv_w, conv_b, dt_bias, A_log, D_skip, ssd_norm_w, w_out, final_norm_w, loss_target, m_norm_w, m_w_in, m_gate_b, m_sgu_norm_g, m_sgu_norm_b, m_sgu_w, m_sgu_b, m_conv_w, m_conv_b, m_dt_bias, m_A_log, m_D_skip, m_ssd_norm_w, m_w_out, m_final_norm_w, v_norm_w, v_w_in, v_gate_b, v_sgu_norm_g, v_sgu_norm_b, v_sgu_w, v_sgu_b, v_conv_w, v_conv_b, v_dt_bias, v_A_log, v_D_skip, v_ssd_norm_w, v_w_out, v_final_norm_w):
    s = x.shape[1]
    x2 = x.reshape(s, D_MODEL)
    tgt = loss_target.reshape(s, D_MODEL)
    t_ssd, t_tok, t_out, t_row = min(T_SSD, s), min(T_TOK, s), min(T_OUT, s), min(T_ROW, s)
    tm_mm, tk_dw = min(TM_MM, s), min(TK_DW, s)
    chip = 2 * lax.axis_index("x") + lax.axis_index("y")

    cw8 = jnp.pad(conv_w[0], ((0, 4), (0, 0)))
    win_b, wout_b = _c(w_in[0]), _c(w_out[0])
    xn, xnt, g_in, g_out, g_cw = _norm_gather_call(x2, norm_w, win_b, wout_b, cw8, t_row)
    g_in = lax.dynamic_update_index_in_dim(g_in, win_b, chip, 0)
    g_out = lax.dynamic_update_index_in_dim(g_out, wout_b, chip, 0)
    g_cw = lax.dynamic_update_index_in_dim(g_cw, cw8, chip, 0)
    wt = jnp.transpose(g_in, (0, 2, 1)).reshape(IN_W, D_MODEL)
    w_out_full = g_out.reshape(D_MODEL, D_MODEL)
    conv_w_full = jnp.transpose(g_cw[:, 0:4, :], (1, 0, 2)).reshape(4, 3072)

    wt_a = jnp.concatenate([wt[0:6144], wt[11296:15392]], axis=0)
    fw, bw = [], []
    for g in range(SSD_GROUPS):
        xs_g = wt[8192 + 512 * g:8192 + 512 * g + 512]
        b_g = wt[10240 + 128 * g:10240 + 128 * g + 128]
        c_g = wt[10752 + 128 * g:10752 + 128 * g + 128]
        zb_g = wt[6144 + 512 * g:6144 + 512 * g + 512]
        dt_g = wt[11264 + 8 * g:11264 + 8 * g + 8]
        fw += [xs_g, b_g, c_g, jnp.repeat(dt_g, HEADDIM, axis=0), zb_g]
        bw += [zb_g, xs_g, b_g, c_g, jnp.pad(dt_g, ((0, 120), (0, 0)))]
    wt_bf = jnp.concatenate(fw, axis=0)
    wt_b = jnp.concatenate(bw, axis=0)

    def group_cols(full_xs, full_bc):
        parts = []
        for g in range(SSD_GROUPS):
            parts += [full_xs[:, 512 * g:512 * g + 512], full_bc[:, 128 * g:128 * g + 128], full_bc[:, 512 + 128 * g:512 + 128 * g + 128]]
        return jnp.concatenate(parts, axis=1)

    cw_g = group_cols(conv_w_full[:, 0:2048], conv_w_full[:, 2048:3072])
    cb_g = group_cols(conv_b[:, 0:2048], conv_b[:, 2048:3072])
    dtb_e, alog_e, dsk_e = _expand_heads(dt_bias), _expand_heads(A_log), _expand_heads(D_skip)

    pos_chunk = jnp.arange(SGU_BLOCK) // CHUNK
    smask = pos_chunk[None, :] <= pos_chunk[:, None]
    wm_f = jnp.where(smask[None], sgu_w[0], 0.0)
    wm = _c(wm_f)
    wmt = _c(jnp.transpose(wm_f, (0, 2, 1)))
    bias_full = jnp.repeat(sgu_b[0].T, D_MODEL // SGU_GROUPS, axis=1)
    fnw = final_norm_w.reshape(1, D_MODEL)

    proj_a = _mm(xn, wt_a, tm=tm_mm, tn=1024, tk=D_MODEL, name="in_proj_a", out_dtype=MXU_DTYPE, b_is_t=True)
    proj_b = _mm(xn, wt_bf, tm=tm_mm, tn=1024, tk=D_MODEL, name="in_proj_b", b_is_t=True)
    y_ssd, y_b, hprev, pre_all, dt_all, acs_all = _ssd_fwd_call(proj_b, dtb_e, alog_e, dsk_e, cw_g, cb_g, ssd_norm_w, t_ssd, NG_SSD)
    y_a, merged, merged_t = _tok_fwd_call(proj_a, y_b, gate_b, sgu_norm_g, sgu_norm_b, wm, bias_full, t_tok)
    dh, dh_b, dmerged, loss_t, dfw8 = _out_call(merged, x2, tgt, w_out_full, fnw, t_out)

    dproj_a, dy_b, dgb8, dgam8, dbeta8, dbfull, dws = _tok_bwd_call(
        proj_a, dmerged, y_a, y_b, gate_b, sgu_norm_g, sgu_norm_b, wm, wmt, bias_full, t_tok)
    dproj_b, a512, a768 = _ssd_bwd_call(proj_b, pre_all, dt_all, acs_all, dy_b, y_ssd, hprev, dtb_e, alog_e, dsk_e, cw_g,
                                        ssd_norm_w, t_ssd, NG_SSD)
    dw_uvz = _mm(xnt, dproj_a, tm=D_MODEL, tn=1024, tk=min(2 * tk_dw, s), name="dw_in_uvz", n=6144)
    dw_gate = _mm(xnt, dproj_a, tm=D_MODEL, tn=1024, tk=min(2 * tk_dw, s), name="dw_in_gate", col0=6, n=4096)
    dw_zb, dw_xs, dw_bm, dw_cm, dw_dt = _dw_groups(xnt, dproj_b, tk=tk_dw)

    dw_dt32 = jnp.concatenate([dw_dt[:, 128 * g:128 * g + 8] for g in range(SSD_GROUPS)], axis=1)
    dw_ref = jnp.concatenate([dw_uvz, dw_zb, dw_xs, dw_bm, dw_cm, dw_dt32, dw_gate], axis=1)
    p_in = jnp.transpose(dw_ref.reshape(D_MODEL, 4, SHARD_W), (1, 0, 2))
    dw_out_p, sib_i = _dw_out_rs_call(merged_t, dh_b, p_in, tn=1024, tk=tk_dw)
    p_out = dw_out_p.reshape(4, D_MODEL // 4, D_MODEL)

    s512 = jnp.sum(a512, axis=2)
    heads = lambda v: jnp.sum(v.reshape(32, HEADDIM), axis=1).reshape(1, 32)
    d_ssd_nw = s512[:, 0].reshape(1, D_MODEL)
    d_dskip = heads(s512[:, 1].reshape(D_MODEL))
    d_alog = heads(s512[:, 2].reshape(D_MODEL)) * (1.0 / HEADDIM) * (-jnp.exp(A_log))
    d_dtb = heads(s512[:, 3].reshape(D_MODEL))
    s768 = jnp.sum(a768, axis=2)
    ungroup = lambda v: jnp.concatenate([v[g, :, 0:512] for g in range(4)] + [v[g, :, 512:640] for g in range(4)]
                                        + [v[g, :, 640:768] for g in range(4)], axis=1)
    d_cw = ungroup(s768[:, 0:4])
    d_cb = ungroup(s768[:, 4:5])
    d_sgu_b = jnp.sum(dbfull.reshape(128, SGU_GROUPS, 128), axis=2).T.reshape(1, SGU_GROUPS, 128)
    d_sgu_w = jnp.where(smask[None], dws, 0.0).reshape(1, SGU_GROUPS, 128, 128)
    fold = lambda a8: jnp.sum(a8, axis=0, keepdims=True)
    small_local = [fold(dgb8), fold(dgam8), fold(dbeta8), d_sgu_w, d_sgu_b, d_cw, d_cb,
                   d_dtb, d_alog, d_dskip, d_ssd_nw, fold(dfw8).reshape(D_MODEL), jnp.sum(loss_t[:, 0, 0]).reshape(1)]
    small_shapes = [a.shape for a in small_local]
    v_local = _pack(small_local)

    core = lax.axis_index("c")
    place = jnp.stack([chip, core]).astype(jnp.int32)
    hr_i, hr_o = D_MODEL // 2, D_MODEL // 8
    sib_o, sib_v = _rs_sibling_call(p_out, v_local)
    s1b_i, o_i = _rs_add(p_in, sib_i, place, 256, "rs_add_in")
    s1b_o, o_o = _rs_add(p_out, sib_o, place, 256, "rs_add_out")
    chip_v = _add_pair(v_local, sib_v, v_local.shape[0], "ar_add_small")
    dxn, r_i, r_o, abs_v = _dx_rs_call(dproj_a, wt_a, dproj_b, wt_b, s1b_i, s1b_o, chip_v, tm=tm_mm)
    grad_x, dnw8 = _gradx_call(x2, dxn, dh, norm_w, t_row)
    abs_v = lax.dynamic_update_index_in_dim(abs_v, chip_v, chip, 0)
    f_i = _sum_own_recv(o_i, r_i, 256, "rs_sum_in")
    f_o = _sum_own_recv(o_o, r_o, 256, "rs_sum_out")
    sib_f_i, g_w_out, all_nw = _rs_join_call(f_i, f_o, dnw8)
    g_w_out = lax.dynamic_update_slice_in_dim(g_w_out, f_o, core * hr_o, axis=0)
    all_nw = lax.dynamic_update_index_in_dim(all_nw, dnw8, 2 * chip + core, 0)
    g_nw = fold(_sum_slots(all_nw, "ar_sum_norm_w"))
    total_v = _sum_slots(abs_v, "ar_sum_small")
    (g_gb, g_gam, g_beta, g_sw, g_sb, g_cw_full, g_cb, g_dtb, g_alog, g_dsk, g_snw, g_fnw, loss1) = _unpack(total_v, small_shapes)
    g_cw_shard = lax.dynamic_slice(g_cw_full, (0, chip * 768), (4, 768)).reshape(1, 4, 768)
    loss = loss1.reshape(())

    g_w_in, d_win, nm_win, nv_win = (a.T for a in _adamw_halves(w_in[0].T, f_i.T, sib_f_i.T, m_w_in[0].T, v_w_in[0].T,
                                                                place, 296, "adamw_w_in"))
    d_wout, nm_wout, nv_wout = _adamw(w_out[0], g_w_out, m_w_out[0], v_w_out[0], 128, "adamw_w_out")
    small_w = [norm_w, gate_b, sgu_norm_g, sgu_norm_b, sgu_w, sgu_b, conv_w, conv_b, dt_bias, A_log, D_skip, ssd_norm_w, final_norm_w]
    small_m = [m_norm_w, m_gate_b, m_sgu_norm_g, m_sgu_norm_b, m_sgu_w, m_sgu_b, m_conv_w, m_conv_b, m_dt_bias, m_A_log, m_D_skip, m_ssd_norm_w, m_final_norm_w]
    small_v = [v_norm_w, v_gate_b, v_sgu_norm_g, v_sgu_norm_b, v_sgu_w, v_sgu_b, v_conv_w, v_conv_b, v_dt_bias, v_A_log, v_D_skip, v_ssd_norm_w, v_final_norm_w]
    small_g = [g_nw, g_gb, g_gam, g_beta, g_sw, g_sb, g_cw_shard, g_cb, g_dtb, g_alog, g_dsk, g_snw, g_fnw]
    shapes_w = [a.shape for a in small_w]
    small_g = [a.reshape(shp) for a, shp in zip(small_g, shapes_w)]
    pw = _pack(small_w)
    pd, pm, pv = _adamw(pw, _pack(small_g), _pack(small_m), _pack(small_v), pw.shape[0], "adamw_small")
    d_small, nm_small, nv_small = _unpack(pd, shapes_w), _unpack(pm, shapes_w), _unpack(pv, shapes_w)

    def with_big(small, win, wout):
        o = list(small)
        return o[0:1] + [win.reshape(1, D_MODEL, SHARD_W)] + o[1:12] + [wout.reshape(1, D_MODEL // 4, D_MODEL)] + o[12:13]

    grads = with_big(small_g, g_w_in, g_w_out)
    deltas = with_big(d_small, d_win, d_wout)
    new_m = with_big(nm_small, nm_win, nm_wout)
    new_v = with_big(nv_small, nv_win, nv_wout)
    return (loss, grad_x.reshape(1, s, D_MODEL), *grads, *deltas, *new_m, *new_v)
```

```python
import functools

import jax
import jax.numpy as jnp
from jax import lax
from jax.experimental import pallas as pl
from jax.experimental.pallas import tpu as pltpu

F32 = jnp.float32
MXU_DTYPE = jnp.bfloat16

D_MODEL = 2048
EPS = 1e-5
CHUNK = 64
SGU_BLOCK = 128
SGU_GROUPS = 16
SSD_GROUPS = 4
SSD_GW = 512
SSD_STATE = 128
HEADDIM = 64
IN_W = 15392
SHARD_W = IN_W // 4
FW_B = 1792
BW_B = 1408
NA = 10240

ADAM_LR = 0.001
ADAM_B1 = 0.9
ADAM_B2 = 0.999
ADAM_EPS = 1e-08
ADAM_WD = 0.01
ADAM_STEP = 10

T_SSD = 256
NG_SSD = 2
T_TOK = 128
T_OUT = 256
T_ROW = 512
TM_MM = 1024
TK_DW = 1024
VMEM_CAP = 60 * 1024 * 1024
MESH = pl.DeviceIdType.MESH


def _cparams(sem, est_bytes):
    lim = int(min(VMEM_CAP, max(32 * 1024 * 1024, est_bytes + 12 * 1024 * 1024)))
    return pltpu.CompilerParams(dimension_semantics=sem, vmem_limit_bytes=lim)


def _c(x):
    return x.astype(MXU_DTYPE)


def _dot(a, b):
    return jnp.dot(a, b, preferred_element_type=F32)


def _dot_nt(a, b):
    return lax.dot_general(a, b, (((1,), (1,)), ((), ())), preferred_element_type=F32)


def _dot_tn(a, b):
    return lax.dot_general(a, b, (((0,), (0,)), ((), ())), preferred_element_type=F32)


def _split(x, n):
    parts, r = [], x
    for _ in range(n):
        p = _c(r)
        parts.append(p)
        r = r - p.astype(F32)
    return parts


def _dot01_l(m01, x, n):
    acc = None
    for p in _split(x, n):
        t = _dot(m01, p)
        acc = t if acc is None else acc + t
    return acc


def _dot01_r(x, m01, n):
    acc = None
    for p in _split(x, n):
        t = _dot(p, m01)
        acc = t if acc is None else acc + t
    return acc


def _sigmoid(x):
    return 1.0 / (1.0 + jnp.exp(-x))


def _fold8(x):
    r, w = x.shape
    return jnp.sum(x.reshape(r // 8, 8, w), axis=0)


def _iota(shape, dim):
    return lax.broadcasted_iota(jnp.int32, shape, dim)


def _ssd_masks():
    l64 = _iota((CHUNK, SSD_GW), 0)
    s64 = jnp.bitwise_and(_iota((CHUNK, SSD_GW), 1), CHUNK - 1)
    diag = l64 == s64
    causal = l64 >= s64
    row_last = l64 == CHUNK - 1
    r4 = lax.shift_right_logical(_iota((256, 256), 0), 6)
    c4 = lax.shift_right_logical(_iota((256, 256), 1), 6)
    mask4 = r4 == c4
    return diag, causal, row_last, mask4


def _cumsum_mats(t):
    r, c = _iota((t, t), 0), _iota((t, t), 1)
    same = lax.shift_right_logical(r, 6) == lax.shift_right_logical(c, 6)
    tri = _c(jnp.where(same, jnp.where(c <= r, 1.0, 0.0), 0.0))
    trit = _c(jnp.where(same, jnp.where(c >= r, 1.0, 0.0), 0.0))
    return tri, trit


def _ssd_common(xs, bm, cm, dt, acs, masks):
    diag, causal, row_last, mask4 = masks
    row_e = jnp.sum(jnp.where(diag, acs, 0.0), axis=0, keepdims=True)
    seg = acs - row_e
    lm = jnp.exp(jnp.where(causal, seg, -1e30))
    bb, cb = _c(bm), _c(cm)
    brep = jnp.concatenate([bb] * 8, axis=0)
    cbrep = _dot_nt(cb, brep)
    m = cbrep * lm
    xdt = xs * dt
    acs_last = jnp.sum(jnp.where(row_last, acs, 0.0), axis=0, keepdims=True)
    dec = jnp.exp(acs_last - acs)
    eacs = jnp.exp(acs)
    cd = jnp.exp(acs_last)
    return dict(lm=lm, bb=bb, cb=cb, brep=brep, m=m, xdt=xdt, dec=dec, eacs=eacs, cd=cd)


def _blockdiag4(xb, mask4):
    return jnp.where(mask4, jnp.concatenate([xb] * 4, axis=0), jnp.zeros((), xb.dtype))


def _ssd_chunk_fwd(xs, bm, cm, dt, acs, d_skip, ht, masks):
    q = _ssd_common(xs, bm, cm, dt, acs, masks)
    mask4 = masks[3]
    mb, xdtb = _c(q["m"]), _c(q["xdt"])
    yd = []
    for blk in range(2):
        sl = slice(256 * blk, 256 * blk + 256)
        yd.append(_dot(mb[:, sl], _blockdiag4(xdtb[:, sl], mask4)))
    y_diag = jnp.concatenate(yd, axis=1)
    p = _dot(q["cb"], _c(ht))
    y = y_diag + p * q["eacs"] + xs * d_skip
    st = _dot_tn(q["bb"], _c(q["xdt"] * q["dec"]))
    return y, ht * q["cd"] + st


def _ssd_chunk_bwd(xs, bm, cm, dt, acs, d_skip, hprev, dht, dy, masks):
    diag, causal, row_last, mask4 = masks
    q = _ssd_common(xs, bm, cm, dt, acs, masks)
    lm, bb, cb, brep, m, xdt, dec, eacs, cd = (q[k] for k in ("lm", "bb", "cb", "brep", "m", "xdt", "dec", "eacs", "cd"))
    hb = _c(hprev)
    yoff = _dot(cb, hb) * eacs
    dyb = _c(dy)
    dpb = _c(dy * eacs)
    d_c = _dot_nt(dpb, hb)
    dh_y = _dot_tn(cb, dpb)
    mb, xdtb = _c(m), _c(xdt)
    dm_parts, dxdt_parts = [], []
    for blk in range(2):
        sl = slice(256 * blk, 256 * blk + 256)
        bd = _blockdiag4(xdtb[:, sl], mask4)
        dm_parts.append(_dot_nt(dyb[:, sl], bd))
        dxf = jnp.where(mask4, _dot_tn(mb[:, sl], dyb[:, sl]), 0.0)
        dxdt_parts.append(dxf[0:64] + dxf[64:128] + dxf[128:192] + dxf[192:256])
    dm = jnp.concatenate(dm_parts, axis=1)
    dxdt = jnp.concatenate(dxdt_parts, axis=1)
    dcbb = _c(dm * lm)
    g = dm * m
    d_c = d_c + _dot(dcbb, brep)
    dbrep = _dot_tn(dcbb, cb)
    d_b = dbrep[0:64]
    for r in range(1, 8):
        d_b = d_b + dbrep[64 * r:64 * r + 64]
    dhtb = _c(dht)
    dxd = _dot(bb, dhtb)
    xd = xdt * dec
    dxdt = dxdt + dxd * dec
    tq = dxd * xd
    d_b = d_b + _dot_nt(_c(xd), dhtb)
    dcd = jnp.sum(dht * hprev, axis=0, keepdims=True)
    col_g = jnp.sum(g, axis=0, keepdims=True)
    last = jnp.sum(tq, axis=0, keepdims=True) + dcd * cd
    qq = g - jnp.where(diag, col_g, 0.0) + dy * yoff - tq + jnp.where(row_last, last, 0.0)
    dxs = dxdt * dt + dy * d_skip
    return dxs, d_b, d_c, dht * cd + dh_y, dy * xs, qq, dxdt * xs


def _ssd_finish_dt(qq, p1, dt, sig, a_neg, trit, mask4):
    bd4 = _c(jnp.where(mask4, 1.0, 0.0))
    dacs = jnp.concatenate([_dot01_r(qq[:, 256 * b:256 * b + 256], bd4, 2) for b in range(2)], axis=1)
    da = _dot01_l(trit, dacs, 2)
    ddt = p1 + da * (a_neg * (1.0 / HEADDIM))
    return ddt * sig, da * dt


def _softplus(x):
    return jnp.maximum(x, 0.0) + jnp.log1p(jnp.exp(-jnp.abs(x)))


def _conv_taps(xpad, t):
    taps = []
    for k in range(4):
        sh = 3 - k
        v = xpad if sh == 0 else pltpu.roll(xpad, sh, 0)
        taps.append(v[8:8 + t])
    return taps


def _mm(a, b, *, tm, tn, tk, name, out_dtype=F32, col0=0, n=None, b_is_t=False):
    m, k = a.shape
    n = b.shape[0 if b_is_t else 1] if n is None else n
    nk = k // tk
    assert m % tm == 0 and n % tn == 0 and k % tk == 0, (a.shape, b.shape, tm, tn, tk)
    assert nk == 1 or out_dtype == F32
    dot = _dot_nt if b_is_t else _dot

    def body(a_ref, b_ref, o_ref):
        if nk == 1:
            o_ref[...] = dot(a_ref[...], b_ref[...]).astype(out_dtype)
        else:
            @pl.when(pl.program_id(2) == 0)
            def _():
                o_ref[...] = jnp.zeros_like(o_ref)

            o_ref[...] += dot(a_ref[...], b_ref[...])

    isz = jnp.dtype(a.dtype).itemsize
    est = 2 * (tm * tk + tk * tn) * isz + 2 * tm * tn * 4
    return pl.pallas_call(
        body, name=name,
        grid=(m // tm, n // tn, nk),
        in_specs=[pl.BlockSpec((tm, tk), lambda i, j, kk: (i, kk)),
                  pl.BlockSpec((tn, tk), lambda i, j, kk: (j + col0, kk)) if b_is_t
                  else pl.BlockSpec((tk, tn), lambda i, j, kk: (kk, j + col0))],
        out_specs=pl.BlockSpec((tm, tn), lambda i, j, kk: (i, j)),
        out_shape=jax.ShapeDtypeStruct((m, n), out_dtype),
        compiler_params=_cparams(("parallel", "parallel", "arbitrary"), est),
    )(a, b)


def _dw_groups(xnt, dpb, *, tk):
    m, k = xnt.shape
    nk = k // tk

    def body(a_ref, b_ref, zb_ref, xs_ref, bm_ref, cm_ref, dt_ref):
        outs = ((zb_ref, 0, 512), (xs_ref, 512, 1024), (bm_ref, 1024, 1152), (cm_ref, 1152, 1280), (dt_ref, 1280, 1408))

        @pl.when(pl.program_id(1) == 0)
        def _():
            for o_ref, _, _ in outs:
                o_ref[...] = jnp.zeros_like(o_ref)

        d = _dot(a_ref[...], b_ref[...])
        for o_ref, lo, hi in outs:
            o_ref[...] += d[:, lo:hi]

    isz = jnp.dtype(xnt.dtype).itemsize
    est = 2 * (m * tk + tk * BW_B) * isz + 3 * m * BW_B * 4
    piece = lambda w: pl.BlockSpec((m, w), lambda g, kk: (0, g))
    return pl.pallas_call(
        body, name="dw_in_b",
        grid=(SSD_GROUPS, nk),
        in_specs=[pl.BlockSpec((m, tk), lambda g, kk: (0, kk)), pl.BlockSpec((tk, BW_B), lambda g, kk: (kk, g))],
        out_specs=[piece(512), piece(512), piece(128), piece(128), piece(128)],
        out_shape=[jax.ShapeDtypeStruct((m, w), F32) for w in (2048, 2048, 512, 512, 512)],
        compiler_params=_cparams(("parallel", "arbitrary"), est),
    )(xnt, dpb)


def _dx_rs_call(dpa, wta, dpb, wtb, sb_in, sb_out, chip_v, *, tm):
    s = dpa.shape[0]
    tka, tkb = 1024, BW_B
    nka, nkb = dpa.shape[1] // tka, dpb.shape[1] // tkb
    ni, nk = s // tm, nka + nkb

    def body(a_ref, wa_ref, b_ref, wb_ref, sbin, sbout, cv, o_ref, rc_in, rc_out, abs_v, send, recv):
        i, kk = pl.program_id(0), pl.program_id(1)

        def copies():
            x, y, c, me, others = _place()
            sends, recvs = [], []
            for j, chip in enumerate(others):
                kj = 2 * chip[0] + chip[1]
                to = (*chip, c)
                sends += [_remote(sbin.at[kj], rc_in.at[j], send, recv, j, to),
                          _remote(sbout.at[kj], rc_out.at[j], send, recv, 3 + j, to),
                          _remote(cv, abs_v.at[me], send, recv, 6 + j, to)]
                recvs += [sends[-3], sends[-2], _remote(cv, abs_v.at[kj], send, recv, 6 + j, to)]
            return sends, recvs

        @pl.when((i == 0) & (kk == 0))
        def _():
            for cp in copies()[0]:
                cp.start()

        @pl.when(kk == 0)
        def _():
            o_ref[...] = jnp.zeros_like(o_ref)

        @pl.when(kk < nka)
        def _():
            o_ref[...] += _dot(a_ref[...], wa_ref[...])

        @pl.when(kk >= nka)
        def _():
            o_ref[...] += _dot(b_ref[...], wb_ref[...])

        @pl.when((i == ni - 1) & (kk == nk - 1))
        def _():
            sends, recvs = copies()
            for cp in recvs:
                cp.wait_recv()
            for cp in sends:
                cp.wait_send()

    isz = jnp.dtype(dpa.dtype).itemsize
    est = 2 * isz * (tm * tka + tka * D_MODEL + tm * tkb + tkb * D_MODEL) + 2 * tm * D_MODEL * 4
    outs = [jax.ShapeDtypeStruct((s, D_MODEL), F32),
            jax.ShapeDtypeStruct((3,) + sb_in.shape[1:], sb_in.dtype), jax.ShapeDtypeStruct((3,) + sb_out.shape[1:], sb_out.dtype),
            jax.ShapeDtypeStruct((4,) + chip_v.shape, F32)]
    return pl.pallas_call(
        body, name="dx_matmul_rs_chips",
        grid=(ni, nk),
        in_specs=[
            pl.BlockSpec((tm, tka), lambda i, kk: (i, jnp.minimum(kk, nka - 1))),
            pl.BlockSpec((tka, D_MODEL), lambda i, kk: (jnp.minimum(kk, nka - 1), 0)),
            pl.BlockSpec((tm, tkb), lambda i, kk: (i, jnp.maximum(kk - nka, 0))),
            pl.BlockSpec((tkb, D_MODEL), lambda i, kk: (jnp.maximum(kk - nka, 0), 0)),
            ANY, ANY, ANY,
        ],
        out_specs=[pl.BlockSpec((tm, D_MODEL), lambda i, kk: (i, 0)), ANY, ANY, ANY],
        out_shape=outs,
        scratch_shapes=[pltpu.SemaphoreType.DMA((9,)), pltpu.SemaphoreType.DMA((9,))],
        compiler_params=_cparams(("arbitrary", "arbitrary"), est),
    )(dpa, wta, dpb, wtb, sb_in, sb_out, chip_v)


def _gradx_call(x, dxn, dh, norm_w, tm):
    s = x.shape[0]

    def body(x_ref, g_ref, dh_ref, w_ref, gx_ref, dw_ref):
        @pl.when(pl.program_id(0) == 0)
        def _():
            dw_ref[...] = jnp.zeros_like(dw_ref)

        xv, gv = x_ref[...], g_ref[...]
        r = lax.rsqrt(jnp.mean(xv * xv, axis=-1, keepdims=True) + EPS)
        gw = gv * w_ref[...]
        gx_ref[...] = r * gw - xv * (r * r * r) * jnp.mean(xv * gw, axis=-1, keepdims=True) + dh_ref[...]
        dw_ref[...] += _fold8(gv * (xv * r))

    row = pl.BlockSpec((tm, D_MODEL), lambda i: (i, 0))
    return pl.pallas_call(
        body, name="grad_x",
        grid=(s // tm,),
        in_specs=[row, row, row, pl.BlockSpec((1, D_MODEL), lambda i: (0, 0))],
        out_specs=[row, pl.BlockSpec((8, D_MODEL), lambda i: (0, 0))],
        out_shape=[jax.ShapeDtypeStruct((s, D_MODEL), F32), jax.ShapeDtypeStruct((8, D_MODEL), F32)],
        compiler_params=_cparams(("arbitrary",), 2 * tm * D_MODEL * 16),
    )(x, dxn, dh, norm_w)


def _layernorm_stats(v):
    mu = jnp.mean(v, axis=-1, keepdims=True)
    vc = v - mu
    var = jnp.mean(vc * vc, axis=-1, keepdims=True)
    return vc * lax.rsqrt(var + EPS), lax.rsqrt(var + EPS)


def _tok_fwd_call(proj_a, y_b, gate_b, sgu_g, sgu_beta, wm, bias_full, t):
    s = proj_a.shape[0]

    def body(pa_ref, yb_ref, gb_ref, g_ref, be_ref, wm_ref, bf_ref, ya_ref, mg_ref, mgt_ref, mix_ref):
        u = pa_ref[:, 0:2048].astype(F32)
        v = pa_ref[:, 2048:4096].astype(F32)
        za = pa_ref[:, 4096:6144].astype(F32)
        xhat, _ = _layernorm_stats(v)
        vnb = _c(xhat * g_ref[...] + be_ref[...])
        for gi in range(SGU_GROUPS):
            sl = slice(128 * gi, 128 * gi + 128)
            mix_ref[:, sl] = _dot(wm_ref[gi], vnb[:, sl])
        mixed = mix_ref[...] + bf_ref[...]
        y_a = u * mixed * (za * _sigmoid(za))
        g0 = _sigmoid(pa_ref[:, 6144:8192].astype(F32) + gb_ref[:, 0:2048])
        g1 = _sigmoid(pa_ref[:, 8192:10240].astype(F32) + gb_ref[:, 2048:4096])
        merged = g0 * y_a + g1 * yb_ref[...].astype(F32)
        ya_ref[...] = _c(y_a)
        mg_ref[...] = _c(merged)
        mgt_ref[...] = _c(merged.T)

    row = pl.BlockSpec((t, D_MODEL), lambda i: (i, 0))
    vec = lambda w: pl.BlockSpec((1, w), lambda i: (0, 0))
    return pl.pallas_call(
        body, name="tok_fwd",
        grid=(s // t,),
        in_specs=[pl.BlockSpec((t, NA), lambda i: (i, 0)), row, vec(4096), vec(2048), vec(2048),
                  pl.BlockSpec((SGU_GROUPS, 128, 128), lambda i: (0, 0, 0)), pl.BlockSpec((128, D_MODEL), lambda i: (0, 0))],
        out_specs=[row, row, pl.BlockSpec((D_MODEL, t), lambda i: (0, i))],
        out_shape=[jax.ShapeDtypeStruct((s, D_MODEL), MXU_DTYPE), jax.ShapeDtypeStruct((s, D_MODEL), MXU_DTYPE),
                   jax.ShapeDtypeStruct((D_MODEL, s), MXU_DTYPE)],
        scratch_shapes=[pltpu.VMEM((t, D_MODEL), F32)],
        compiler_params=_cparams(("parallel",), 2 * t * NA * 4 + 12 * t * D_MODEL * 4),
    )(proj_a, y_b, gate_b, sgu_g, sgu_beta, wm, bias_full)


def _tok_bwd_call(proj_a, dmerged, y_a, y_b, gate_b, sgu_g, sgu_beta, wm, wmt, bias_full, t):
    s = proj_a.shape[0]

    def body(pa_ref, dm_ref, ya_ref, yb_ref, gb_ref, g_ref, be_ref, wm_ref, wmt_ref, bf_ref,
             dpa_ref, dyb_ref, dgb_ref, dgam_ref, dbeta_ref, dbf_ref, dws_ref, mix_ref, dvn_ref):
        @pl.when(pl.program_id(0) == 0)
        def _():
            dgb_ref[...] = jnp.zeros_like(dgb_ref)
            dgam_ref[...] = jnp.zeros_like(dgam_ref)
            dbeta_ref[...] = jnp.zeros_like(dbeta_ref)
            dbf_ref[...] = jnp.zeros_like(dbf_ref)
            dws_ref[...] = jnp.zeros_like(dws_ref)

        u = pa_ref[:, 0:2048].astype(F32)
        v = pa_ref[:, 2048:4096].astype(F32)
        za = pa_ref[:, 4096:6144].astype(F32)
        xhat, rstd = _layernorm_stats(v)
        vnb = _c(xhat * g_ref[...] + be_ref[...])
        for gi in range(SGU_GROUPS):
            sl = slice(128 * gi, 128 * gi + 128)
            mix_ref[:, sl] = _dot(wm_ref[gi], vnb[:, sl])
        mixed = mix_ref[...] + bf_ref[...]
        sig = _sigmoid(za)
        sz = za * sig
        dm = dm_ref[...].astype(F32)
        y_a = ya_ref[...].astype(F32)
        g0 = _sigmoid(pa_ref[:, 6144:8192].astype(F32) + gb_ref[:, 0:2048])
        g1 = _sigmoid(pa_ref[:, 8192:10240].astype(F32) + gb_ref[:, 2048:4096])
        dgl0 = dm * y_a * g0 * (1.0 - g0)
        dgl1 = dm * yb_ref[...].astype(F32) * g1 * (1.0 - g1)
        dyb_ref[...] = _c(dm * g1)
        dya = dm * g0
        dpa_ref[:, 6144:8192] = _c(dgl0)
        dpa_ref[:, 8192:10240] = _c(dgl1)
        dgb_ref[:, 0:2048] += _fold8(dgl0)
        dgb_ref[:, 2048:4096] += _fold8(dgl1)
        dpa_ref[:, 0:2048] = _c(dya * mixed * sz)
        dpa_ref[:, 4096:6144] = _c(dya * (u * mixed) * (sig * (1.0 + za * (1.0 - sig))))
        dmixed = dya * u * sz
        dbf_ref[...] += dmixed
        dmb = _c(dmixed)
        for gi in range(SGU_GROUPS):
            sl = slice(128 * gi, 128 * gi + 128)
            dvn_ref[:, sl] = _dot(wmt_ref[gi], dmb[:, sl])
            dws_ref[gi] += _dot_nt(dmb[:, sl], vnb[:, sl])
        dvn = dvn_ref[...]
        dgam_ref[...] += _fold8(dvn * xhat)
        dbeta_ref[...] += _fold8(dvn)
        dxh = dvn * g_ref[...]
        dv = rstd * (dxh - jnp.mean(dxh, axis=-1, keepdims=True) - xhat * jnp.mean(dxh * xhat, axis=-1, keepdims=True))
        dpa_ref[:, 2048:4096] = _c(dv)

    row = pl.BlockSpec((t, D_MODEL), lambda i: (i, 0))
    vec = lambda w: pl.BlockSpec((1, w), lambda i: (0, 0))
    acc = lambda w: pl.BlockSpec((8, w), lambda i: (0, 0))
    wspec = pl.BlockSpec((SGU_GROUPS, 128, 128), lambda i: (0, 0, 0))
    return pl.pallas_call(
        body, name="tok_bwd",
        grid=(s // t,),
        in_specs=[pl.BlockSpec((t, NA), lambda i: (i, 0)), row, row, row, vec(4096), vec(2048), vec(2048),
                  wspec, wspec, pl.BlockSpec((128, D_MODEL), lambda i: (0, 0))],
        out_specs=[pl.BlockSpec((t, NA), lambda i: (i, 0)), row, acc(4096), acc(2048), acc(2048),
                   pl.BlockSpec((128, D_MODEL), lambda i: (0, 0)), wspec],
        out_shape=[jax.ShapeDtypeStruct((s, NA), MXU_DTYPE), jax.ShapeDtypeStruct((s, D_MODEL), MXU_DTYPE),
                   jax.ShapeDtypeStruct((8, 4096), F32), jax.ShapeDtypeStruct((8, 2048), F32),
                   jax.ShapeDtypeStruct((8, 2048), F32), jax.ShapeDtypeStruct((128, D_MODEL), F32),
                   jax.ShapeDtypeStruct((SGU_GROUPS, 128, 128), F32)],
        scratch_shapes=[pltpu.VMEM((t, D_MODEL), F32), pltpu.VMEM((t, D_MODEL), F32)],
        compiler_params=_cparams(("arbitrary",), 2 * t * NA * 6 + 16 * t * D_MODEL * 4),
    )(proj_a, dmerged, y_a, y_b, gate_b, sgu_g, sgu_beta, wm, wmt, bias_full)


def _out_call(merged, x, target, w_out, fnw, t):
    s = x.shape[0]
    nt = s // t

    def body(mg_ref, x_ref, t_ref, w_ref, fw_ref, dh_ref, dhb_ref, dmg_ref, loss_ref, dfw_ref):
        @pl.when(pl.program_id(0) == 0)
        def _():
            dfw_ref[...] = jnp.zeros_like(dfw_ref)

        h = x_ref[...] + _dot(mg_ref[...], w_ref[...])
        r = lax.rsqrt(jnp.mean(h * h, axis=-1, keepdims=True) + EPS)
        hn = h * r
        err = hn * fw_ref[...] - t_ref[...]
        loss_ref[...] = jnp.full(loss_ref.shape, 0.5 * jnp.sum(jnp.mean(err * err, axis=-1, keepdims=True)), F32)
        dy = err * (1.0 / D_MODEL)
        dfw_ref[...] += _fold8(dy * hn)
        gw = dy * fw_ref[...]
        dh = r * gw - h * (r * r * r) * jnp.mean(h * gw, axis=-1, keepdims=True)
        dh_ref[...] = dh
        dhb = _c(dh)
        dhb_ref[...] = dhb
        dmg_ref[...] = _c(_dot_nt(dhb, w_ref[...]))

    row = pl.BlockSpec((t, D_MODEL), lambda i: (i, 0))
    return pl.pallas_call(
        body, name="out_proj_loss",
        grid=(nt,),
        in_specs=[row, row, row, pl.BlockSpec((D_MODEL, D_MODEL), lambda i: (0, 0)), pl.BlockSpec((1, D_MODEL), lambda i: (0, 0))],
        out_specs=[row, row, row, pl.BlockSpec((1, 8, 128), lambda i: (i, 0, 0)), pl.BlockSpec((8, D_MODEL), lambda i: (0, 0))],
        out_shape=[jax.ShapeDtypeStruct((s, D_MODEL), F32), jax.ShapeDtypeStruct((s, D_MODEL), MXU_DTYPE),
                   jax.ShapeDtypeStruct((s, D_MODEL), MXU_DTYPE), jax.ShapeDtypeStruct((nt, 8, 128), F32),
                   jax.ShapeDtypeStruct((8, D_MODEL), F32)],
        compiler_params=_cparams(("arbitrary",), 2 * D_MODEL * D_MODEL * 2 + 2 * t * D_MODEL * 24),
    )(merged, x, target, w_out, fnw)


def _ssd_fwd_call(proj_b, dtb, alog, dsk, cw, cb, nw, t, ng):
    s = proj_b.shape[0]
    nt, nch = s // t, t // CHUNK

    def body(pb_ref, halo_ref, dtb_ref, al_ref, ds_ref, cw_ref, cb_ref, nw_ref, y_ref, yb_ref, hp_ref, pre_ref,
             dt_ref, acs_ref, ht_ref, prev_ref):
        i = pl.program_id(1)

        @pl.when(i == 0)
        def _():
            ht_ref[...] = jnp.zeros_like(ht_ref)

        for gi in range(ng):
            prev_ref[:, 768 * gi:768 * gi + 768] = jnp.where(i == 0, 0.0, halo_ref[:, FW_B * gi:FW_B * gi + 768])
        masks = _ssd_masks()
        tri, _ = _cumsum_mats(CHUNK)
        a_neg = -jnp.exp(al_ref[...])

        def chunk(c, carry):
            rows = pl.ds(pl.multiple_of(c * CHUNK, CHUNK), CHUNK)
            for gi in range(ng):
                fo, co, go = FW_B * gi, 768 * gi, SSD_GW * gi
                xbc = pb_ref[rows, fo:fo + 768]
                taps = _conv_taps(jnp.concatenate([prev_ref[:, co:co + 768], xbc], axis=0), CHUNK)
                prev_ref[:, co:co + 768] = xbc[CHUNK - 8:CHUNK]
                pre = cb_ref[:, co:co + 768]
                for k in range(4):
                    pre = pre + taps[k] * cw_ref[k:k + 1, co:co + 768]
                pre_ref[rows, co:co + 768] = pre
                act = pre * _sigmoid(pre)
                dt = _softplus(pb_ref[rows, fo + 768:fo + 1280] + dtb_ref[:, go:go + 512])
                acs = _dot01_l(tri, dt * a_neg[:, go:go + 512], 3)
                dt_ref[rows, go:go + 512] = dt
                acs_ref[rows, go:go + 512] = acs
                ht = ht_ref[gi]
                hp_ref[c, :, go:go + 512] = ht
                y, ht_new = _ssd_chunk_fwd(act[:, 0:512], act[:, 512:640], act[:, 640:768], dt, acs,
                                           ds_ref[:, go:go + 512], ht, masks)
                y_ref[rows, go:go + 512] = y
                ht_ref[gi] = ht_new
                zb = pb_ref[rows, fo + 1280:fo + 1792]
                hh = y * (zb * _sigmoid(zb))
                rr = lax.rsqrt(jnp.mean(hh * hh, axis=-1, keepdims=True) + EPS)
                yb_ref[rows, go:go + 512] = _c(hh * rr * nw_ref[:, go:go + 512])
            return carry

        lax.fori_loop(0, nch, chunk, 0)

    gvec = lambda w: pl.BlockSpec((1, ng * w), lambda g, i: (0, g))
    return pl.pallas_call(
        body, name="ssd_fwd",
        grid=(SSD_GROUPS // ng, nt),
        in_specs=[pl.BlockSpec((t, ng * FW_B), lambda g, i: (i, g)),
                  pl.BlockSpec((8, ng * FW_B), lambda g, i: (jnp.maximum(i * (t // 8) - 1, 0), g)),
                  gvec(512), gvec(512), gvec(512),
                  pl.BlockSpec((4, ng * 768), lambda g, i: (0, g)), gvec(768), gvec(512)],
        out_specs=[pl.BlockSpec((t, ng * SSD_GW), lambda g, i: (i, g)), pl.BlockSpec((t, ng * SSD_GW), lambda g, i: (i, g)),
                   pl.BlockSpec((nch, SSD_STATE, ng * SSD_GW), lambda g, i: (i, 0, g)),
                   pl.BlockSpec((t, ng * 768), lambda g, i: (i, g)),
                   pl.BlockSpec((t, ng * SSD_GW), lambda g, i: (i, g)), pl.BlockSpec((t, ng * SSD_GW), lambda g, i: (i, g))],
        out_shape=[jax.ShapeDtypeStruct((s, D_MODEL), F32), jax.ShapeDtypeStruct((s, D_MODEL), MXU_DTYPE),
                   jax.ShapeDtypeStruct((s // CHUNK, SSD_STATE, D_MODEL), F32),
                   jax.ShapeDtypeStruct((s, SSD_GROUPS * 768), F32),
                   jax.ShapeDtypeStruct((s, D_MODEL), F32), jax.ShapeDtypeStruct((s, D_MODEL), F32)],
        scratch_shapes=[pltpu.VMEM((ng, SSD_STATE, SSD_GW), F32), pltpu.VMEM((8, ng * 768), F32)],
        compiler_params=_cparams(("parallel", "arbitrary"), ng * (2 * t * FW_B * 4 + 16 * t * SSD_GW * 4) + 16 * 1024 * 1024),
    )(proj_b, proj_b, dtb, alog, dsk, cw, cb, nw)


def _ssd_bwd_call(proj_b, pre_all, dt_all, acs_all, dyb, y, hprev, dtb, alog, dsk, cw, nw, t, ng):
    s = proj_b.shape[0]
    nt, nch = s // t, t // CHUNK

    def body(pb_ref, pre_ref, dt_ref, acs_ref, dyb_ref, y_ref, hp_ref, dtb_ref, al_ref, ds_ref, cw_ref, nw_ref,
             dpb_ref, a512_ref, a768_ref, dht_ref, nxt_ref, q_ref, p1_ref):
        i = pl.program_id(1)

        @pl.when(i == 0)
        def _():
            dht_ref[...] = jnp.zeros_like(dht_ref)
            nxt_ref[...] = jnp.zeros_like(nxt_ref)
            a512_ref[...] = jnp.zeros_like(a512_ref)
            a768_ref[...] = jnp.zeros_like(a768_ref)

        _, trit = _cumsum_mats(t)
        a_neg = -jnp.exp(al_ref[...])
        masks = _ssd_masks()

        def chunk(cc, carry):
            c = nch - 1 - cc
            rows = pl.ds(pl.multiple_of(c * CHUNK, CHUNK), CHUNK)
            for gi in range(ng):
                fo, co, go, bo = FW_B * gi, 768 * gi, SSD_GW * gi, BW_B * gi
                pre = pre_ref[rows, co:co + 768]
                sp = _sigmoid(pre)
                act = pre * sp
                zb = pb_ref[rows, fo + 1280:fo + 1792]
                yv = y_ref[rows, go:go + 512]
                sgz = _sigmoid(zb)
                sz = zb * sgz
                hh = yv * sz
                rr = lax.rsqrt(jnp.mean(hh * hh, axis=-1, keepdims=True) + EPS)
                dyb = dyb_ref[rows, go:go + 512].astype(F32)
                a512_ref[gi, 0] += _fold8(dyb * (hh * rr))
                tt = dyb * nw_ref[:, go:go + 512]
                dhh = rr * tt - hh * (rr * rr * rr) * jnp.mean(hh * tt, axis=-1, keepdims=True)
                dpb_ref[rows, bo:bo + 512] = _c(dhh * yv * (sgz * (1.0 + zb * (1.0 - sgz))))
                dxs, d_b, d_c, dht_prev, dyxs, qq, p1 = _ssd_chunk_bwd(
                    act[:, 0:512], act[:, 512:640], act[:, 640:768], dt_ref[rows, go:go + 512], acs_ref[rows, go:go + 512],
                    ds_ref[:, go:go + 512], hp_ref[c, :, go:go + 512], dht_ref[gi], dhh * sz, masks)
                dht_ref[gi] = dht_prev
                q_ref[rows, go:go + 512] = qq
                p1_ref[rows, go:go + 512] = p1
                a512_ref[gi, 1] += _fold8(dyxs)
                dpre = jnp.concatenate([dxs, d_b, d_c], axis=1) * (sp * (1.0 + pre * (1.0 - sp)))
                xbc = pb_ref[rows, fo:fo + 768]
                a768_ref[gi, 4] += _fold8(dpre)
                a768_ref[gi, 3] += _fold8(dpre * xbc)
                dpad = jnp.concatenate([dpre, nxt_ref[:, co:co + 768]], axis=0)
                dx = dpre * cw_ref[3:4, co:co + 768]
                for k in range(3):
                    d_k = pltpu.roll(dpad, CHUNK + 8 - (3 - k), 0)[0:CHUNK]
                    dx = dx + d_k * cw_ref[k:k + 1, co:co + 768]
                    a768_ref[gi, k] += _fold8(d_k * xbc)
                nxt_ref[:, co:co + 768] = dpre[0:8]
                dpb_ref[rows, bo + 512:bo + 1280] = _c(dx)
            return carry

        lax.fori_loop(0, nch, chunk, 0)

        rsel = _c(jnp.where(lax.shift_right_logical(_iota((SSD_GW, 128), 0), 6) == _iota((SSD_GW, 128), 1), 1.0, 0.0))
        for gi in range(ng):
            fo, go, bo = FW_B * gi, SSD_GW * gi, BW_B * gi
            sig = _sigmoid(pb_ref[:, fo + 768:fo + 1280] + dtb_ref[:, go:go + 512])
            ddtr, dadt = _ssd_finish_dt(q_ref[:, go:go + 512], p1_ref[:, go:go + 512], dt_ref[:, go:go + 512], sig,
                                        a_neg[:, go:go + 512], trit, masks[3])
            dpb_ref[:, bo + 1280:bo + 1408] = _c(_dot01_r(ddtr, rsel, 2))
            a512_ref[gi, 2] += _fold8(dadt)
            a512_ref[gi, 3] += _fold8(ddtr)

    gvec = lambda w: pl.BlockSpec((1, ng * w), lambda g, i: (0, g))
    rev = lambda w: pl.BlockSpec((t, ng * w), lambda g, i: (nt - 1 - i, g))
    return pl.pallas_call(
        body, name="ssd_bwd",
        grid=(SSD_GROUPS // ng, nt),
        in_specs=[rev(FW_B), rev(768), rev(SSD_GW), rev(SSD_GW), rev(SSD_GW), rev(SSD_GW),
                  pl.BlockSpec((nch, SSD_STATE, ng * SSD_GW), lambda g, i: (nt - 1 - i, 0, g)),
                  gvec(512), gvec(512), gvec(512),
                  pl.BlockSpec((4, ng * 768), lambda g, i: (0, g)), gvec(512)],
        out_specs=[rev(BW_B),
                   pl.BlockSpec((ng, 4, 8, 512), lambda g, i: (g, 0, 0, 0)),
                   pl.BlockSpec((ng, 5, 8, 768), lambda g, i: (g, 0, 0, 0))],
        out_shape=[jax.ShapeDtypeStruct((s, SSD_GROUPS * BW_B), MXU_DTYPE),
                   jax.ShapeDtypeStruct((SSD_GROUPS, 4, 8, 512), F32),
                   jax.ShapeDtypeStruct((SSD_GROUPS, 5, 8, 768), F32)],
        scratch_shapes=[pltpu.VMEM((ng, SSD_STATE, SSD_GW), F32), pltpu.VMEM((8, ng * 768), F32),
                        pltpu.VMEM((t, ng * SSD_GW), F32), pltpu.VMEM((t, ng * SSD_GW), F32)],
        compiler_params=_cparams(("parallel", "arbitrary"), ng * (2 * t * FW_B * 4 + 18 * t * SSD_GW * 4) + 16 * 1024 * 1024),
    )(proj_b, pre_all, dt_all, acs_all, dyb, y, hprev, dtb, alog, dsk, cw, nw)


def _rows_call(body, ins, outs, tr, name):
    r = ins[0].shape[0]
    spec = lambda a: pl.BlockSpec((tr, a.shape[1]), lambda i: (i, 0))
    est = 2 * tr * sum(a.shape[1] * jnp.dtype(a.dtype).itemsize for a in list(ins) + list(outs))
    return pl.pallas_call(
        body, name=name, grid=(r // tr,),
        in_specs=[spec(a) for a in ins], out_specs=[spec(o) for o in outs], out_shape=list(outs),
        compiler_params=_cparams(("parallel",), est),
    )(*ins)


def _add_pair(a, b, tr, name):
    def body(a_ref, b_ref, o_ref):
        o_ref[...] = a_ref[...] + b_ref[...]

    return _rows_call(body, [a, b], [jax.ShapeDtypeStruct(a.shape, F32)], tr, name)[0]


def _rs_add(p, sib, place, tr, name):
    _, r, c = p.shape
    half = r // 2
    nb = half // tr

    def body(pl_ref, p_ref, s_ref, b_ref, own_ref):
        v = p_ref[0] + s_ref[0]
        b_ref[0] = v.astype(jnp.bfloat16)

        @pl.when(pl.program_id(1) == pl_ref[0])
        def _():
            own_ref[...] = v

    return pl.pallas_call(
        body, name=name,
        grid_spec=pltpu.PrefetchScalarGridSpec(
            num_scalar_prefetch=1, grid=(nb, 4),
            in_specs=[pl.BlockSpec((1, tr, c), lambda i, k, pr: (k, pr[1] * nb + i, 0)),
                      pl.BlockSpec((1, tr, c), lambda i, k, pr: (k, i, 0))],
            out_specs=[pl.BlockSpec((1, tr, c), lambda i, k, pr: (k, i, 0)),
                       pl.BlockSpec((tr, c), lambda i, k, pr: (i, 0))]),
        out_shape=[jax.ShapeDtypeStruct((4, half, c), jnp.bfloat16), jax.ShapeDtypeStruct((half, c), F32)],
        compiler_params=_cparams(("parallel", "arbitrary"), 2 * tr * c * 14),
    )(place, p, sib)


def _sum_own_recv(own, recv, tr, name):
    r, c = own.shape

    def body(o_ref, r_ref, out_ref):
        v = o_ref[...]
        for j in range(3):
            v = v + r_ref[j].astype(F32)
        out_ref[...] = v

    return pl.pallas_call(
        body, name=name, grid=(r // tr,),
        in_specs=[pl.BlockSpec((tr, c), lambda i: (i, 0)), pl.BlockSpec((3, tr, c), lambda i: (0, i, 0))],
        out_specs=pl.BlockSpec((tr, c), lambda i: (i, 0)),
        out_shape=jax.ShapeDtypeStruct((r, c), F32),
        compiler_params=_cparams(("parallel",), 2 * tr * c * 14),
    )(own, recv)


def _sum_slots(stack, name):
    n, r, w = stack.shape

    def body(a_ref, out_ref):
        v = a_ref[0]
        for k in range(1, n):
            v = v + a_ref[k]
        out_ref[...] = v

    return pl.pallas_call(
        body, name=name, grid=(1,),
        in_specs=[pl.BlockSpec((n, r, w), lambda i: (0, 0, 0))],
        out_specs=pl.BlockSpec((r, w), lambda i: (0, 0)),
        out_shape=jax.ShapeDtypeStruct((r, w), F32),
        compiler_params=_cparams(("arbitrary",), 2 * (n + 1) * r * w * 4),
    )(stack)


def _adamw(w, g, m, v, tr, name):
    def body(w_ref, g_ref, m_ref, v_ref, d_ref, nm_ref, nv_ref):
        d_ref[...], nm_ref[...], nv_ref[...] = _adam_math(w_ref[...], g_ref[...], m_ref[...], v_ref[...])

    o = jax.ShapeDtypeStruct(w.shape, F32)
    return _rows_call(body, [w, g, m, v], [o, o, o], tr, name)


def _adam_math(w, g, m, v):
    nm = ADAM_B1 * m + (1.0 - ADAM_B1) * g
    nv = ADAM_B2 * v + (1.0 - ADAM_B2) * (g * g)
    m_hat = nm / (1.0 - ADAM_B1 ** ADAM_STEP)
    v_hat = nv / (1.0 - ADAM_B2 ** ADAM_STEP)
    return -ADAM_LR * (m_hat / (jnp.sqrt(v_hat) + ADAM_EPS) + ADAM_WD * w), nm, nv


def _adamw_halves(w, g_own, g_sib, m, v, place, tr, name):
    r, c = w.shape

    def body(pl_ref, w_ref, go_ref, gs_ref, m_ref, v_ref, g_ref, d_ref, nm_ref, nv_ref):
        first = pl_ref[1] == 0
        own, sib = go_ref[...], gs_ref[...]
        g = jnp.concatenate([jnp.where(first, own, sib), jnp.where(first, sib, own)], axis=1)
        g_ref[...] = g
        d_ref[...], nm_ref[...], nv_ref[...] = _adam_math(w_ref[...], g, m_ref[...], v_ref[...])

    full = pl.BlockSpec((tr, c), lambda i, pr: (i, 0))
    half = pl.BlockSpec((tr, c // 2), lambda i, pr: (i, 0))
    o = jax.ShapeDtypeStruct((r, c), F32)
    return pl.pallas_call(
        body, name=name,
        grid_spec=pltpu.PrefetchScalarGridSpec(num_scalar_prefetch=1, grid=(r // tr,),
                                               in_specs=[full, half, half, full, full], out_specs=[full] * 4),
        out_shape=[o] * 4,
        compiler_params=_cparams(("parallel",), 2 * tr * c * 4 * 8),
    )(place, w, g_own, g_sib, m, v)


ANY = pl.BlockSpec(memory_space=pl.ANY)


def _place():
    x, y, c = lax.axis_index("x"), lax.axis_index("y"), lax.axis_index("c")
    others = [(1 - x, y), (x, 1 - y), (1 - x, 1 - y)]
    return x, y, c, 2 * x + y, others


def _remote(src, dst, send, recv, k, to):
    return pltpu.make_async_remote_copy(src_ref=src, dst_ref=dst, send_sem=send.at[k], recv_sem=recv.at[k],
                                        device_id=to, device_id_type=MESH)


def _norm_gather_call(x, norm_w, win_b, wout_b, cw8, tm):
    s = x.shape[0]
    ni = s // tm

    def body(x_ref, w_ref, win, wout, cw, xn_ref, xnt_ref, g_in, g_out, g_cw, send, recv):
        i = pl.program_id(0)

        def direct():
            xx, yy, c, me, others = _place()
            cps = []
            for a, (src, dst) in enumerate(((win, g_in), (wout, g_out))):
                half = src.shape[0] // 2
                mine = pl.ds(c * half, half)
                cps += [_remote(src.at[mine], dst.at[me, mine], send, recv, 6 * a + j, (*chip, c)) for j, chip in enumerate(others)]
            cps += [_remote(cw, g_cw.at[me], send, recv, 12 + j, (*chip, c)) for j, chip in enumerate(others)]
            return cps

        @pl.when(i == 0)
        def _():
            for cp in direct():
                cp.start()

        xv = x_ref[...]
        r = lax.rsqrt(jnp.mean(xv * xv, axis=-1, keepdims=True) + EPS)
        xn = xv * r * w_ref[...]
        xn_ref[...] = _c(xn)
        xnt_ref[...] = _c(xn.T)

        @pl.when(i == ni - 1)
        def _():
            xx, yy, c, me, others = _place()
            sib = (xx, yy, 1 - c)
            passed = []
            for a, dst in enumerate((g_in, g_out)):
                half = dst.shape[1] // 2
                mine = pl.ds(c * half, half)
                for j, chip in enumerate(others):
                    kj = 2 * chip[0] + chip[1]
                    _remote(dst.at[kj, mine], dst.at[kj, mine], send, recv, 6 * a + j, (*chip, c)).wait_recv()
                    cp = _remote(dst.at[kj, mine], dst.at[kj, mine], send, recv, 6 * a + 3 + j, sib)
                    cp.start()
                    passed.append(cp)
            for a, dst in enumerate((g_in, g_out)):
                half = dst.shape[1] // 2
                theirs = pl.ds((1 - c) * half, half)
                for j, chip in enumerate(others):
                    kj = 2 * chip[0] + chip[1]
                    _remote(dst.at[kj, theirs], dst.at[kj, theirs], send, recv, 6 * a + 3 + j, sib).wait_recv()
            for j, chip in enumerate(others):
                kj = 2 * chip[0] + chip[1]
                _remote(cw, g_cw.at[kj], send, recv, 12 + j, (*chip, c)).wait_recv()
            for cp in direct() + passed:
                cp.wait_send()

    outs = [jax.ShapeDtypeStruct((s, D_MODEL), MXU_DTYPE), jax.ShapeDtypeStruct((D_MODEL, s), MXU_DTYPE)]
    outs += [jax.ShapeDtypeStruct((4,) + a.shape, a.dtype) for a in (win_b, wout_b, cw8)]
    return pl.pallas_call(
        body, name="rmsnorm_gather_weights",
        grid=(ni,),
        in_specs=[pl.BlockSpec((tm, D_MODEL), lambda i: (i, 0)), pl.BlockSpec((1, D_MODEL), lambda i: (0, 0)), ANY, ANY, ANY],
        out_specs=[pl.BlockSpec((tm, D_MODEL), lambda i: (i, 0)), pl.BlockSpec((D_MODEL, tm), lambda i: (0, i)), ANY, ANY, ANY],
        out_shape=outs,
        scratch_shapes=[pltpu.SemaphoreType.DMA((15,)), pltpu.SemaphoreType.DMA((15,))],
        compiler_params=_cparams(("arbitrary",), 2 * tm * D_MODEL * 12),
    )(x, norm_w, win_b, wout_b, cw8)


def _dw_out_rs_call(a, b, p_in, *, tn, tk):
    m, k = a.shape
    n = b.shape[1]
    nj, nk = n // tn, k // tk

    def body(a_ref, b_ref, pin, o_ref, sib_in, send, recv):
        j, kk = pl.program_id(0), pl.program_id(1)

        def copy():
            x, y, c, me, others = _place()
            half = pin.shape[1] // 2
            return _remote(pin.at[:, pl.ds((1 - c) * half, half)], sib_in, send, recv, 0, (x, y, 1 - c))

        @pl.when((j == 0) & (kk == 0))
        def _():
            copy().start()

        @pl.when(kk == 0)
        def _():
            o_ref[...] = jnp.zeros_like(o_ref)

        o_ref[...] += _dot(a_ref[...], b_ref[...])

        @pl.when((j == nj - 1) & (kk == nk - 1))
        def _():
            cp = copy()
            cp.wait_recv()
            cp.wait_send()

    isz = jnp.dtype(a.dtype).itemsize
    est = 2 * (m * tk + tk * tn) * isz + 2 * m * tn * 4
    outs = [jax.ShapeDtypeStruct((m, n), F32), jax.ShapeDtypeStruct((4, p_in.shape[1] // 2, p_in.shape[2]), p_in.dtype)]
    return pl.pallas_call(
        body, name="dw_out_rs_sibling",
        grid=(nj, nk),
        in_specs=[pl.BlockSpec((m, tk), lambda j, kk: (0, kk)), pl.BlockSpec((tk, tn), lambda j, kk: (kk, j)), ANY],
        out_specs=[pl.BlockSpec((m, tn), lambda j, kk: (0, j)), ANY],
        out_shape=outs,
        scratch_shapes=[pltpu.SemaphoreType.DMA((1,)), pltpu.SemaphoreType.DMA((1,))],
        compiler_params=_cparams(("arbitrary", "arbitrary"), est),
    )(a, b, p_in)


def _rs_sibling_call(p_out, vsmall):
    def body(pout, vs, sib_out, sib_v, send, recv):
        x, y, c, me, others = _place()
        sib = (x, y, 1 - c)
        half = pout.shape[1] // 2
        cps = [_remote(pout.at[:, pl.ds((1 - c) * half, half)], sib_out, send, recv, 0, sib),
               _remote(vs, sib_v, send, recv, 1, sib)]
        for cp in cps:
            cp.start()
        for cp in cps:
            cp.wait_recv()
        for cp in cps:
            cp.wait_send()

    outs = [jax.ShapeDtypeStruct((4, p_out.shape[1] // 2, p_out.shape[2]), p_out.dtype),
            jax.ShapeDtypeStruct(vsmall.shape, vsmall.dtype)]
    return pl.pallas_call(
        body, name="rs_sibling",
        in_specs=[ANY] * 2, out_specs=[ANY] * 2, out_shape=outs,
        scratch_shapes=[pltpu.SemaphoreType.DMA((2,)), pltpu.SemaphoreType.DMA((2,))],
    )(p_out, vsmall)


def _rs_join_call(f_in, f_out, nw8):
    def body(fin, fout, nw, sib_in, full_out, all_nw, send, recv):
        x, y, c, me, others = _place()
        sib = (x, y, 1 - c)
        half = fout.shape[0]
        cps = [_remote(fin, sib_in, send, recv, 0, sib),
               _remote(fout, full_out.at[pl.ds(c * half, half)], send, recv, 1, sib)]
        mine = 4 * x + 2 * y + c
        peers = []
        for r in range(1, 8):
            px, py, pc = (1 - x if r & 4 else x), (1 - y if r & 2 else y), (1 - c if r & 1 else c)
            peers.append((r, (px, py, pc), 4 * px + 2 * py + pc))
            cps.append(_remote(nw, all_nw.at[mine], send, recv, 1 + r, (px, py, pc)))
        for cp in cps:
            cp.start()
        cps[0].wait_recv()
        _remote(fout, full_out.at[pl.ds((1 - c) * half, half)], send, recv, 1, sib).wait_recv()
        for r, peer, idx in peers:
            _remote(nw, all_nw.at[idx], send, recv, 1 + r, peer).wait_recv()
        for cp in cps:
            cp.wait_send()

    outs = [jax.ShapeDtypeStruct(f_in.shape, F32), jax.ShapeDtypeStruct((2 * f_out.shape[0], f_out.shape[1]), F32),
            jax.ShapeDtypeStruct((8,) + nw8.shape, F32)]
    return pl.pallas_call(
        body, name="rs_join",
        in_specs=[ANY] * 3, out_specs=[ANY] * 3, out_shape=outs,
        scratch_shapes=[pltpu.SemaphoreType.DMA((9,)), pltpu.SemaphoreType.DMA((9,))],
    )(f_in, f_out, nw8)


def _pack(arrs):
    parts = []
    for a in arrs:
        f = a.reshape(-1).astype(F32)
        pad = (-f.shape[0]) % 1024
        parts.append(jnp.pad(f, (0, pad)).reshape(-1, 128))
    return jnp.concatenate(parts, axis=0)


def _unpack(packed, shapes):
    out, row = [], 0
    for shp in shapes:
        n = 1
        for d in shp:
            n *= d
        rows = (n + 1023) // 1024 * 8
        out.append(packed[row:row + rows].reshape(-1)[:n].reshape(shp))
        row += rows
    return out


def _expand_heads(v32):
    return jnp.repeat(v32.reshape(32), HEADDIM).reshape(1, D_MODEL)


def kernel(x, norm_w, w_in, gate_b, sgu_norm_g, sgu_norm_b, sgu_w, sgu_b, conv_w, conv_b, dt_bias, A_log, D_skip, ssd_norm_w, w_out, final_norm_w, loss_target, m_norm_w, m_w_in, m_gate_b, m_sgu_norm_g, m_sgu_norm_b, m_sgu_w, m_sgu_b, m_conv_w, m_conv_b, m_dt_bias, m_A_log, m_D_skip, m_ssd_norm_w, m_w_out, m_final_norm_w, v_norm_w, v_w_in, v_gate_b, v_sgu_norm_g, v_sgu_norm_b, v_sgu_w, v_sgu_b, v_conv_w, v_conv_b, v_dt_bias, v_A_log, v_D_skip, v_ssd_norm_w, v_w_out, v_final_norm_w):
    s = x.shape[1]
    x2 = x.reshape(s, D_MODEL)
    tgt = loss_target.reshape(s, D_MODEL)
    t_ssd, t_tok, t_out, t_row = min(T_SSD, s), min(T_TOK, s), min(T_OUT, s), min(T_ROW, s)
    tm_mm, tk_dw = min(TM_MM, s), min(TK_DW, s)
    chip = 2 * lax.axis_index("x") + lax.axis_index("y")

    cw8 = jnp.pad(conv_w[0], ((0, 4), (0, 0)))
    win_b, wout_b = _c(w_in[0]), _c(w_out[0])
    xn, xnt, g_in, g_out, g_cw = _norm_gather_call(x2, norm_w, win_b, wout_b, cw8, t_row)
    g_in = lax.dynamic_update_index_in_dim(g_in, win_b, chip, 0)
    g_out = lax.dynamic_update_index_in_dim(g_out, wout_b, chip, 0)
    g_cw = lax.dynamic_update_index_in_dim(g_cw, cw8, chip, 0)
    wt = jnp.transpose(g_in, (0, 2, 1)).reshape(IN_W, D_MODEL)
    w_out_full = g_out.reshape(D_MODEL, D_MODEL)
    conv_w_full = jnp.transpose(g_cw[:, 0:4, :], (1, 0, 2)).reshape(4, 3072)

    wt_a = jnp.concatenate([wt[0:6144], wt[11296:15392]], axis=0)
    fw, bw = [], []
    for g in range(SSD_GROUPS):
        xs_g = wt[8192 + 512 * g:8192 + 512 * g + 512]
        b_g = wt[10240 + 128 * g:10240 + 128 * g + 128]
        c_g = wt[10752 + 128 * g:10752 + 128 * g + 128]
        zb_g = wt[6144 + 512 * g:6144 + 512 * g + 512]
        dt_g = wt[11264 + 8 * g:11264 + 8 * g + 8]
        fw += [xs_g, b_g, c_g, jnp.repeat(dt_g, HEADDIM, axis=0), zb_g]
        bw += [zb_g, xs_g, b_g, c_g, jnp.pad(dt_g, ((0, 120), (0, 0)))]
    wt_bf = jnp.concatenate(fw, axis=0)
    wt_b = jnp.concatenate(bw, axis=0)

    def group_cols(full_xs, full_bc):
        parts = []
        for g in range(SSD_GROUPS):
            parts += [full_xs[:, 512 * g:512 * g + 512], full_bc[:, 128 * g:128 * g + 128], full_bc[:, 512 + 128 * g:512 + 128 * g + 128]]
        return jnp.concatenate(parts, axis=1)

    cw_g = group_cols(conv_w_full[:, 0:2048], conv_w_full[:, 2048:3072])
    cb_g = group_cols(conv_b[:, 0:2048], conv_b[:, 2048:3072])
    dtb_e, alog_e, dsk_e = _expand_heads(dt_bias), _expand_heads(A_log), _expand_heads(D_skip)

    pos_chunk = jnp.arange(SGU_BLOCK) // CHUNK
    smask = pos_chunk[None, :] <= pos_chunk[:, None]
    wm_f = jnp.where(smask[None], sgu_w[0], 0.0)
    wm = _c(wm_f)
    wmt = _c(jnp.transpose(wm_f, (0, 2, 1)))
    bias_full = jnp.repeat(sgu_b[0].T, D_MODEL // SGU_GROUPS, axis=1)
    fnw = final_norm_w.reshape(1, D_MODEL)

    proj_a = _mm(xn, wt_a, tm=tm_mm, tn=1024, tk=D_MODEL, name="in_proj_a", out_dtype=MXU_DTYPE, b_is_t=True)
    proj_b = _mm(xn, wt_bf, tm=tm_mm, tn=1024, tk=D_MODEL, name="in_proj_b", b_is_t=True)
    y_ssd, y_b, hprev, pre_all, dt_all, acs_all = _ssd_fwd_call(proj_b, dtb_e, alog_e, dsk_e, cw_g, cb_g, ssd_norm_w, t_ssd, NG_SSD)
    y_a, merged, merged_t = _tok_fwd_call(proj_a, y_b, gate_b, sgu_norm_g, sgu_norm_b, wm, bias_full, t_tok)
    dh, dh_b, dmerged, loss_t, dfw8 = _out_call(merged, x2, tgt, w_out_full, fnw, t_out)

    dproj_a, dy_b, dgb8, dgam8, dbeta8, dbfull, dws = _tok_bwd_call(
        proj_a, dmerged, y_a, y_b, gate_b, sgu_norm_g, sgu_norm_b, wm, wmt, bias_full, t_tok)
    dproj_b, a512, a768 = _ssd_bwd_call(proj_b, pre_all, dt_all, acs_all, dy_b, y_ssd, hprev, dtb_e, alog_e, dsk_e, cw_g,
                                        ssd_norm_w, t_ssd, NG_SSD)
    dw_uvz = _mm(xnt, dproj_a, tm=D_MODEL, tn=1024, tk=min(2 * tk_dw, s), name="dw_in_uvz", n=6144)
    dw_gate = _mm(xnt, dproj_a, tm=D_MODEL, tn=1024, tk=min(2 * tk_dw, s), name="dw_in_gate", col0=6, n=4096)
    dw_zb, dw_xs, dw_bm, dw_cm, dw_dt = _dw_groups(xnt, dproj_b, tk=tk_dw)

    dw_dt32 = jnp.concatenate([dw_dt[:, 128 * g:128 * g + 8] for g in range(SSD_GROUPS)], axis=1)
    dw_ref = jnp.concatenate([dw_uvz, dw_zb, dw_xs, dw_bm, dw_cm, dw_dt32, dw_gate], axis=1)
    p_in = jnp.stack([dw_ref[:, SHARD_W * k:SHARD_W * (k + 1)] for k in range(4)])
    dw_out_p, sib_i = _dw_out_rs_call(merged_t, dh_b, p_in, tn=1024, tk=tk_dw)
    p_out = dw_out_p.reshape(4, D_MODEL // 4, D_MODEL)

    s512 = jnp.sum(a512, axis=2)
    heads = lambda v: jnp.sum(v.reshape(32, HEADDIM), axis=1).reshape(1, 32)
    d_ssd_nw = s512[:, 0].reshape(1, D_MODEL)
    d_dskip = heads(s512[:, 1].reshape(D_MODEL))
    d_alog = heads(s512[:, 2].reshape(D_MODEL)) * (1.0 / HEADDIM) * (-jnp.exp(A_log))
    d_dtb = heads(s512[:, 3].reshape(D_MODEL))
    s768 = jnp.sum(a768, axis=2)
    ungroup = lambda v: jnp.concatenate([v[g, :, 0:512] for g in range(4)] + [v[g, :, 512:640] for g in range(4)]
                                        + [v[g, :, 640:768] for g in range(4)], axis=1)
    d_cw = ungroup(s768[:, 0:4])
    d_cb = ungroup(s768[:, 4:5])
    d_sgu_b = jnp.sum(dbfull.reshape(128, SGU_GROUPS, 128), axis=2).T.reshape(1, SGU_GROUPS, 128)
    d_sgu_w = jnp.where(smask[None], dws, 0.0).reshape(1, SGU_GROUPS, 128, 128)
    fold = lambda a8: jnp.sum(a8, axis=0, keepdims=True)
    small_local = [fold(dgb8), fold(dgam8), fold(dbeta8), d_sgu_w, d_sgu_b, d_cw, d_cb,
                   d_dtb, d_alog, d_dskip, d_ssd_nw, fold(dfw8).reshape(D_MODEL), jnp.sum(loss_t[:, 0, 0]).reshape(1)]
    small_shapes = [a.shape for a in small_local]
    v_local = _pack(small_local)

    core = lax.axis_index("c")
    place = jnp.stack([chip, core]).astype(jnp.int32)
    hr_i, hr_o = D_MODEL // 2, D_MODEL // 8
    sib_o, sib_v = _rs_sibling_call(p_out, v_local)
    s1b_i, o_i = _rs_add(p_in, sib_i, place, 256, "rs_add_in")
    s1b_o, o_o = _rs_add(p_out, sib_o, place, 256, "rs_add_out")
    chip_v = _add_pair(v_local, sib_v, v_local.shape[0], "ar_add_small")
    dxn, r_i, r_o, abs_v = _dx_rs_call(dproj_a, wt_a, dproj_b, wt_b, s1b_i, s1b_o, chip_v, tm=tm_mm)
    grad_x, dnw8 = _gradx_call(x2, dxn, dh, norm_w, t_row)
    abs_v = lax.dynamic_update_index_in_dim(abs_v, chip_v, chip, 0)
    f_i = _sum_own_recv(o_i, r_i, 256, "rs_sum_in")
    f_o = _sum_own_recv(o_o, r_o, 256, "rs_sum_out")
    sib_f_i, g_w_out, all_nw = _rs_join_call(f_i, f_o, dnw8)
    g_w_out = lax.dynamic_update_slice_in_dim(g_w_out, f_o, core * hr_o, axis=0)
    all_nw = lax.dynamic_update_index_in_dim(all_nw, dnw8, 2 * chip + core, 0)
    g_nw = fold(_sum_slots(all_nw, "ar_sum_norm_w"))
    total_v = _sum_slots(abs_v, "ar_sum_small")
    (g_gb, g_gam, g_beta, g_sw, g_sb, g_cw_full, g_cb, g_dtb, g_alog, g_dsk, g_snw, g_fnw, loss1) = _unpack(total_v, small_shapes)
    g_cw_shard = lax.dynamic_slice(g_cw_full, (0, chip * 768), (4, 768)).reshape(1, 4, 768)
    loss = loss1.reshape(())

    g_w_in, d_win, nm_win, nv_win = (a.T for a in _adamw_halves(w_in[0].T, f_i.T, sib_f_i.T, m_w_in[0].T, v_w_in[0].T,
                                                                place, 296, "adamw_w_in"))
    d_wout, nm_wout, nv_wout = _adamw(w_out[0], g_w_out, m_w_out[0], v_w_out[0], 128, "adamw_w_out")
    small_w = [norm_w, gate_b, sgu_norm_g, sgu_norm_b, sgu_w, sgu_b, conv_w, conv_b, dt_bias, A_log, D_skip, ssd_norm_w, final_norm_w]
    small_m = [m_norm_w, m_gate_b, m_sgu_norm_g, m_sgu_norm_b, m_sgu_w, m_sgu_b, m_conv_w, m_conv_b, m_dt_bias, m_A_log, m_D_skip, m_ssd_norm_w, m_final_norm_w]
    small_v = [v_norm_w, v_gate_b, v_sgu_norm_g, v_sgu_norm_b, v_sgu_w, v_sgu_b, v_conv_w, v_conv_b, v_dt_bias, v_A_log, v_D_skip, v_ssd_norm_w, v_final_norm_w]
    small_g = [g_nw, g_gb, g_gam, g_beta, g_sw, g_sb, g_cw_shard, g_cb, g_dtb, g_alog, g_dsk, g_snw, g_fnw]
    shapes_w = [a.shape for a in small_w]
    small_g = [a.reshape(shp) for a, shp in zip(small_g, shapes_w)]
    pw = _pack(small_w)
    pd, pm, pv = _adamw(pw, _pack(small_g), _pack(small_m), _pack(small_v), pw.shape[0], "adamw_small")
    d_small, nm_small, nv_small = _unpack(pd, shapes_w), _unpack(pm, shapes_w), _unpack(pv, shapes_w)

    def with_big(small, win, wout):
        o = list(small)
        return o[0:1] + [win.reshape(1, D_MODEL, SHARD_W)] + o[1:12] + [wout.reshape(1, D_MODEL // 4, D_MODEL)] + o[12:13]

    grads = with_big(small_g, g_w_in, g_w_out)
    deltas = with_big(d_small, d_win, d_wout)
    new_m = with_big(nm_small, nm_win, nm_wout)
    new_v = with_big(nv_small, nv_win, nv_wout)
    return (loss, grad_x.reshape(1, s, D_MODEL), *grads, *deltas, *new_m, *new_v)
```

```python
import functools

import jax
import jax.numpy as jnp
from jax import lax
from jax.experimental import pallas as pl
from jax.experimental.pallas import tpu as pltpu

F32 = jnp.float32
MXU_DTYPE = jnp.bfloat16

D_MODEL = 2048
EPS = 1e-5
CHUNK = 64
SGU_BLOCK = 128
SGU_GROUPS = 16
SSD_GROUPS = 4
SSD_GW = 512
SSD_STATE = 128
HEADDIM = 64
IN_W = 15392
SHARD_W = IN_W // 4
FW_B = 1792
BW_B = 1408
NA = 10240

ADAM_LR = 0.001
ADAM_B1 = 0.9
ADAM_B2 = 0.999
ADAM_EPS = 1e-08
ADAM_WD = 0.01
ADAM_STEP = 10

T_SSD = 256
NG_SSD = 2
T_TOK = 128
T_OUT = 256
T_ROW = 512
TM_MM = 1024
TK_DW = 1024
VMEM_CAP = 60 * 1024 * 1024
MESH = pl.DeviceIdType.MESH


def _cparams(sem, est_bytes):
    lim = int(min(VMEM_CAP, max(32 * 1024 * 1024, est_bytes + 12 * 1024 * 1024)))
    return pltpu.CompilerParams(dimension_semantics=sem, vmem_limit_bytes=lim)


def _c(x):
    return x.astype(MXU_DTYPE)


def _dot(a, b):
    return jnp.dot(a, b, preferred_element_type=F32)


def _dot_nt(a, b):
    return lax.dot_general(a, b, (((1,), (1,)), ((), ())), preferred_element_type=F32)


def _dot_tn(a, b):
    return lax.dot_general(a, b, (((0,), (0,)), ((), ())), preferred_element_type=F32)


def _split(x, n):
    parts, r = [], x
    for _ in range(n):
        p = _c(r)
        parts.append(p)
        r = r - p.astype(F32)
    return parts


def _dot01_l(m01, x, n):
    acc = None
    for p in _split(x, n):
        t = _dot(m01, p)
        acc = t if acc is None else acc + t
    return acc


def _dot01_r(x, m01, n):
    acc = None
    for p in _split(x, n):
        t = _dot(p, m01)
        acc = t if acc is None else acc + t
    return acc


def _sigmoid(x):
    return 1.0 / (1.0 + jnp.exp(-x))


def _fold8(x):
    r, w = x.shape
    return jnp.sum(x.reshape(r // 8, 8, w), axis=0)


def _iota(shape, dim):
    return lax.broadcasted_iota(jnp.int32, shape, dim)


def _ssd_masks():
    l64 = _iota((CHUNK, SSD_GW), 0)
    s64 = jnp.bitwise_and(_iota((CHUNK, SSD_GW), 1), CHUNK - 1)
    diag = l64 == s64
    causal = l64 >= s64
    row_last = l64 == CHUNK - 1
    r4 = lax.shift_right_logical(_iota((256, 256), 0), 6)
    c4 = lax.shift_right_logical(_iota((256, 256), 1), 6)
    mask4 = r4 == c4
    return diag, causal, row_last, mask4


def _cumsum_mats(t):
    r, c = _iota((t, t), 0), _iota((t, t), 1)
    same = lax.shift_right_logical(r, 6) == lax.shift_right_logical(c, 6)
    tri = _c(jnp.where(same, jnp.where(c <= r, 1.0, 0.0), 0.0))
    trit = _c(jnp.where(same, jnp.where(c >= r, 1.0, 0.0), 0.0))
    return tri, trit


def _ssd_common(xs, bm, cm, dt, acs, masks):
    diag, causal, row_last, mask4 = masks
    row_e = jnp.sum(jnp.where(diag, acs, 0.0), axis=0, keepdims=True)
    seg = acs - row_e
    lm = jnp.exp(jnp.where(causal, seg, -1e30))
    bb, cb = _c(bm), _c(cm)
    brep = jnp.concatenate([bb] * 8, axis=0)
    cbrep = _dot_nt(cb, brep)
    m = cbrep * lm
    xdt = xs * dt
    acs_last = jnp.sum(jnp.where(row_last, acs, 0.0), axis=0, keepdims=True)
    dec = jnp.exp(acs_last - acs)
    eacs = jnp.exp(acs)
    cd = jnp.exp(acs_last)
    return dict(lm=lm, bb=bb, cb=cb, brep=brep, m=m, xdt=xdt, dec=dec, eacs=eacs, cd=cd)


def _blockdiag4(xb, mask4):
    return jnp.where(mask4, jnp.concatenate([xb] * 4, axis=0), jnp.zeros((), xb.dtype))


def _ssd_chunk_fwd(xs, bm, cm, dt, acs, d_skip, ht, masks):
    q = _ssd_common(xs, bm, cm, dt, acs, masks)
    mask4 = masks[3]
    mb, xdtb = _c(q["m"]), _c(q["xdt"])
    yd = []
    for blk in range(2):
        sl = slice(256 * blk, 256 * blk + 256)
        yd.append(_dot(mb[:, sl], _blockdiag4(xdtb[:, sl], mask4)))
    y_diag = jnp.concatenate(yd, axis=1)
    p = _dot(q["cb"], _c(ht))
    y = y_diag + p * q["eacs"] + xs * d_skip
    st = _dot_tn(q["bb"], _c(q["xdt"] * q["dec"]))
    return y, ht * q["cd"] + st


def _ssd_chunk_bwd(xs, bm, cm, dt, acs, d_skip, hprev, dht, dy, masks):
    diag, causal, row_last, mask4 = masks
    q = _ssd_common(xs, bm, cm, dt, acs, masks)
    lm, bb, cb, brep, m, xdt, dec, eacs, cd = (q[k] for k in ("lm", "bb", "cb", "brep", "m", "xdt", "dec", "eacs", "cd"))
    hb = _c(hprev)
    yoff = _dot(cb, hb) * eacs
    dyb = _c(dy)
    dpb = _c(dy * eacs)
    d_c = _dot_nt(dpb, hb)
    dh_y = _dot_tn(cb, dpb)
    mb, xdtb = _c(m), _c(xdt)
    dm_parts, dxdt_parts = [], []
    for blk in range(2):
        sl = slice(256 * blk, 256 * blk + 256)
        bd = _blockdiag4(xdtb[:, sl], mask4)
        dm_parts.append(_dot_nt(dyb[:, sl], bd))
        dxf = jnp.where(mask4, _dot_tn(mb[:, sl], dyb[:, sl]), 0.0)
        dxdt_parts.append(dxf[0:64] + dxf[64:128] + dxf[128:192] + dxf[192:256])
    dm = jnp.concatenate(dm_parts, axis=1)
    dxdt = jnp.concatenate(dxdt_parts, axis=1)
    dcbb = _c(dm * lm)
    g = dm * m
    d_c = d_c + _dot(dcbb, brep)
    dbrep = _dot_tn(dcbb, cb)
    d_b = dbrep[0:64]
    for r in range(1, 8):
        d_b = d_b + dbrep[64 * r:64 * r + 64]
    dhtb = _c(dht)
    dxd = _dot(bb, dhtb)
    xd = xdt * dec
    dxdt = dxdt + dxd * dec
    tq = dxd * xd
    d_b = d_b + _dot_nt(_c(xd), dhtb)
    dcd = jnp.sum(dht * hprev, axis=0, keepdims=True)
    col_g = jnp.sum(g, axis=0, keepdims=True)
    last = jnp.sum(tq, axis=0, keepdims=True) + dcd * cd
    qq = g - jnp.where(diag, col_g, 0.0) + dy * yoff - tq + jnp.where(row_last, last, 0.0)
    dxs = dxdt * dt + dy * d_skip
    return dxs, d_b, d_c, dht * cd + dh_y, dy * xs, qq, dxdt * xs


def _ssd_finish_dt(qq, p1, dt, sig, a_neg, trit, mask4):
    bd4 = _c(jnp.where(mask4, 1.0, 0.0))
    dacs = jnp.concatenate([_dot01_r(qq[:, 256 * b:256 * b + 256], bd4, 2) for b in range(2)], axis=1)
    da = _dot01_l(trit, dacs, 2)
    ddt = p1 + da * (a_neg * (1.0 / HEADDIM))
    return ddt * sig, da * dt


def _softplus(x):
    return jnp.maximum(x, 0.0) + jnp.log1p(jnp.exp(-jnp.abs(x)))


def _conv_taps(xpad, t):
    taps = []
    for k in range(4):
        sh = 3 - k
        v = xpad if sh == 0 else pltpu.roll(xpad, sh, 0)
        taps.append(v[8:8 + t])
    return taps


def _mm(a, b, *, tm, tn, tk, name, out_dtype=F32, col0=0, n=None, b_is_t=False):
    m, k = a.shape
    n = b.shape[0 if b_is_t else 1] if n is None else n
    nk = k // tk
    assert m % tm == 0 and n % tn == 0 and k % tk == 0, (a.shape, b.shape, tm, tn, tk)
    assert nk == 1 or out_dtype == F32
    dot = _dot_nt if b_is_t else _dot

    def body(a_ref, b_ref, o_ref):
        if nk == 1:
            o_ref[...] = dot(a_ref[...], b_ref[...]).astype(out_dtype)
        else:
            @pl.when(pl.program_id(2) == 0)
            def _():
                o_ref[...] = jnp.zeros_like(o_ref)

            o_ref[...] += dot(a_ref[...], b_ref[...])

    isz = jnp.dtype(a.dtype).itemsize
    est = 2 * (tm * tk + tk * tn) * isz + 2 * tm * tn * 4
    return pl.pallas_call(
        body, name=name,
        grid=(m // tm, n // tn, nk),
        in_specs=[pl.BlockSpec((tm, tk), lambda i, j, kk: (i, kk)),
                  pl.BlockSpec((tn, tk), lambda i, j, kk: (j + col0, kk)) if b_is_t
                  else pl.BlockSpec((tk, tn), lambda i, j, kk: (kk, j + col0))],
        out_specs=pl.BlockSpec((tm, tn), lambda i, j, kk: (i, j)),
        out_shape=jax.ShapeDtypeStruct((m, n), out_dtype),
        compiler_params=_cparams(("parallel", "parallel", "arbitrary"), est),
    )(a, b)


def _dw_groups(xnt, dpb, *, tk):
    m, k = xnt.shape
    nk = k // tk

    def body(a_ref, b_ref, zb_ref, xs_ref, bm_ref, cm_ref, dt_ref):
        outs = ((zb_ref, 0, 512), (xs_ref, 512, 1024), (bm_ref, 1024, 1152), (cm_ref, 1152, 1280), (dt_ref, 1280, 1408))

        @pl.when(pl.program_id(1) == 0)
        def _():
            for o_ref, _, _ in outs:
                o_ref[...] = jnp.zeros_like(o_ref)

        d = _dot(a_ref[...], b_ref[...])
        for o_ref, lo, hi in outs:
            o_ref[...] += d[:, lo:hi]

    isz = jnp.dtype(xnt.dtype).itemsize
    est = 2 * (m * tk + tk * BW_B) * isz + 3 * m * BW_B * 4
    piece = lambda w: pl.BlockSpec((m, w), lambda g, kk: (0, g))
    return pl.pallas_call(
        body, name="dw_in_b",
        grid=(SSD_GROUPS, nk),
        in_specs=[pl.BlockSpec((m, tk), lambda g, kk: (0, kk)), pl.BlockSpec((tk, BW_B), lambda g, kk: (kk, g))],
        out_specs=[piece(512), piece(512), piece(128), piece(128), piece(128)],
        out_shape=[jax.ShapeDtypeStruct((m, w), F32) for w in (2048, 2048, 512, 512, 512)],
        compiler_params=_cparams(("parallel", "arbitrary"), est),
    )(xnt, dpb)


def _dx_rs_call(dpa, wta, dpb, wtb, sb_in, sb_out, chip_v, *, tm):
    s = dpa.shape[0]
    tka, tkb = 1024, BW_B
    nka, nkb = dpa.shape[1] // tka, dpb.shape[1] // tkb
    ni, nk = s // tm, nka + nkb

    def body(a_ref, wa_ref, b_ref, wb_ref, sbin, sbout, cv, o_ref, rc_in, rc_out, abs_v, send, recv):
        i, kk = pl.program_id(0), pl.program_id(1)

        def copies():
            x, y, c, me, others = _place()
            sends, recvs = [], []
            for j, chip in enumerate(others):
                kj = 2 * chip[0] + chip[1]
                to = (*chip, c)
                sends += [_remote(sbin.at[kj], rc_in.at[j], send, recv, j, to),
                          _remote(sbout.at[kj], rc_out.at[j], send, recv, 3 + j, to),
                          _remote(cv, abs_v.at[me], send, recv, 6 + j, to)]
                recvs += [sends[-3], sends[-2], _remote(cv, abs_v.at[kj], send, recv, 6 + j, to)]
            return sends, recvs

        @pl.when((i == 0) & (kk == 0))
        def _():
            for cp in copies()[0]:
                cp.start()

        @pl.when(kk == 0)
        def _():
            o_ref[...] = jnp.zeros_like(o_ref)

        @pl.when(kk < nka)
        def _():
            o_ref[...] += _dot(a_ref[...], wa_ref[...])

        @pl.when(kk >= nka)
        def _():
            o_ref[...] += _dot(b_ref[...], wb_ref[...])

        @pl.when((i == ni - 1) & (kk == nk - 1))
        def _():
            sends, recvs = copies()
            for cp in recvs:
                cp.wait_recv()
            for cp in sends:
                cp.wait_send()

    isz = jnp.dtype(dpa.dtype).itemsize
    est = 2 * isz * (tm * tka + tka * D_MODEL + tm * tkb + tkb * D_MODEL) + 2 * tm * D_MODEL * 4
    outs = [jax.ShapeDtypeStruct((s, D_MODEL), F32),
            jax.ShapeDtypeStruct((3,) + sb_in.shape[1:], sb_in.dtype), jax.ShapeDtypeStruct((3,) + sb_out.shape[1:], sb_out.dtype),
            jax.ShapeDtypeStruct((4,) + chip_v.shape, F32)]
    return pl.pallas_call(
        body, name="dx_matmul_rs_chips",
        grid=(ni, nk),
        in_specs=[
            pl.BlockSpec((tm, tka), lambda i, kk: (i, jnp.minimum(kk, nka - 1))),
            pl.BlockSpec((tka, D_MODEL), lambda i, kk: (jnp.minimum(kk, nka - 1), 0)),
            pl.BlockSpec((tm, tkb), lambda i, kk: (i, jnp.maximum(kk - nka, 0))),
            pl.BlockSpec((tkb, D_MODEL), lambda i, kk: (jnp.maximum(kk - nka, 0), 0)),
            ANY, ANY, ANY,
        ],
        out_specs=[pl.BlockSpec((tm, D_MODEL), lambda i, kk: (i, 0)), ANY, ANY, ANY],
        out_shape=outs,
        scratch_shapes=[pltpu.SemaphoreType.DMA((9,)), pltpu.SemaphoreType.DMA((9,))],
        compiler_params=_cparams(("arbitrary", "arbitrary"), est),
    )(dpa, wta, dpb, wtb, sb_in, sb_out, chip_v)


def _gradx_call(x, dxn, dh, norm_w, tm):
    s = x.shape[0]

    def body(x_ref, g_ref, dh_ref, w_ref, gx_ref, dw_ref):
        @pl.when(pl.program_id(0) == 0)
        def _():
            dw_ref[...] = jnp.zeros_like(dw_ref)

        xv, gv = x_ref[...], g_ref[...]
        r = lax.rsqrt(jnp.mean(xv * xv, axis=-1, keepdims=True) + EPS)
        gw = gv * w_ref[...]
        gx_ref[...] = r * gw - xv * (r * r * r) * jnp.mean(xv * gw, axis=-1, keepdims=True) + dh_ref[...]
        dw_ref[...] += _fold8(gv * (xv * r))

    row = pl.BlockSpec((tm, D_MODEL), lambda i: (i, 0))
    return pl.pallas_call(
        body, name="grad_x",
        grid=(s // tm,),
        in_specs=[row, row, row, pl.BlockSpec((1, D_MODEL), lambda i: (0, 0))],
        out_specs=[row, pl.BlockSpec((8, D_MODEL), lambda i: (0, 0))],
        out_shape=[jax.ShapeDtypeStruct((s, D_MODEL), F32), jax.ShapeDtypeStruct((8, D_MODEL), F32)],
        compiler_params=_cparams(("arbitrary",), 2 * tm * D_MODEL * 16),
    )(x, dxn, dh, norm_w)


def _layernorm_stats(v):
    mu = jnp.mean(v, axis=-1, keepdims=True)
    vc = v - mu
    var = jnp.mean(vc * vc, axis=-1, keepdims=True)
    return vc * lax.rsqrt(var + EPS), lax.rsqrt(var + EPS)


def _tok_fwd_call(proj_a, y_b, gate_b, sgu_g, sgu_beta, wm, bias_full, t):
    s = proj_a.shape[0]

    def body(pa_ref, yb_ref, gb_ref, g_ref, be_ref, wm_ref, bf_ref, ya_ref, mg_ref, mgt_ref, mix_ref):
        u = pa_ref[:, 0:2048].astype(F32)
        v = pa_ref[:, 2048:4096].astype(F32)
        za = pa_ref[:, 4096:6144].astype(F32)
        xhat, _ = _layernorm_stats(v)
        vnb = _c(xhat * g_ref[...] + be_ref[...])
        for gi in range(SGU_GROUPS):
            sl = slice(128 * gi, 128 * gi + 128)
            mix_ref[:, sl] = _dot(wm_ref[gi], vnb[:, sl])
        mixed = mix_ref[...] + bf_ref[...]
        y_a = u * mixed * (za * _sigmoid(za))
        g0 = _sigmoid(pa_ref[:, 6144:8192].astype(F32) + gb_ref[:, 0:2048])
        g1 = _sigmoid(pa_ref[:, 8192:10240].astype(F32) + gb_ref[:, 2048:4096])
        merged = g0 * y_a + g1 * yb_ref[...].astype(F32)
        ya_ref[...] = _c(y_a)
        mg_ref[...] = _c(merged)
        mgt_ref[...] = _c(merged.T)

    row = pl.BlockSpec((t, D_MODEL), lambda i: (i, 0))
    vec = lambda w: pl.BlockSpec((1, w), lambda i: (0, 0))
    return pl.pallas_call(
        body, name="tok_fwd",
        grid=(s // t,),
        in_specs=[pl.BlockSpec((t, NA), lambda i: (i, 0)), row, vec(4096), vec(2048), vec(2048),
                  pl.BlockSpec((SGU_GROUPS, 128, 128), lambda i: (0, 0, 0)), pl.BlockSpec((128, D_MODEL), lambda i: (0, 0))],
        out_specs=[row, row, pl.BlockSpec((D_MODEL, t), lambda i: (0, i))],
        out_shape=[jax.ShapeDtypeStruct((s, D_MODEL), MXU_DTYPE), jax.ShapeDtypeStruct((s, D_MODEL), MXU_DTYPE),
                   jax.ShapeDtypeStruct((D_MODEL, s), MXU_DTYPE)],
        scratch_shapes=[pltpu.VMEM((t, D_MODEL), F32)],
        compiler_params=_cparams(("parallel",), 2 * t * NA * 4 + 12 * t * D_MODEL * 4),
    )(proj_a, y_b, gate_b, sgu_g, sgu_beta, wm, bias_full)


def _tok_bwd_call(proj_a, dmerged, y_a, y_b, gate_b, sgu_g, sgu_beta, wm, wmt, bias_full, t):
    s = proj_a.shape[0]

    def body(pa_ref, dm_ref, ya_ref, yb_ref, gb_ref, g_ref, be_ref, wm_ref, wmt_ref, bf_ref,
             dpa_ref, dyb_ref, dgb_ref, dgam_ref, dbeta_ref, dbf_ref, dws_ref, mix_ref, dvn_ref):
        @pl.when(pl.program_id(0) == 0)
        def _():
            dgb_ref[...] = jnp.zeros_like(dgb_ref)
            dgam_ref[...] = jnp.zeros_like(dgam_ref)
            dbeta_ref[...] = jnp.zeros_like(dbeta_ref)
            dbf_ref[...] = jnp.zeros_like(dbf_ref)
            dws_ref[...] = jnp.zeros_like(dws_ref)

        u = pa_ref[:, 0:2048].astype(F32)
        v = pa_ref[:, 2048:4096].astype(F32)
        za = pa_ref[:, 4096:6144].astype(F32)
        xhat, rstd = _layernorm_stats(v)
        vnb = _c(xhat * g_ref[...] + be_ref[...])
        for gi in range(SGU_GROUPS):
            sl = slice(128 * gi, 128 * gi + 128)
            mix_ref[:, sl] = _dot(wm_ref[gi], vnb[:, sl])
        mixed = mix_ref[...] + bf_ref[...]
        sig = _sigmoid(za)
        sz = za * sig
        dm = dm_ref[...].astype(F32)
        y_a = ya_ref[...].astype(F32)
        g0 = _sigmoid(pa_ref[:, 6144:8192].astype(F32) + gb_ref[:, 0:2048])
        g1 = _sigmoid(pa_ref[:, 8192:10240].astype(F32) + gb_ref[:, 2048:4096])
        dgl0 = dm * y_a * g0 * (1.0 - g0)
        dgl1 = dm * yb_ref[...].astype(F32) * g1 * (1.0 - g1)
        dyb_ref[...] = _c(dm * g1)
        dya = dm * g0
        dpa_ref[:, 6144:8192] = _c(dgl0)
        dpa_ref[:, 8192:10240] = _c(dgl1)
        dgb_ref[:, 0:2048] += _fold8(dgl0)
        dgb_ref[:, 2048:4096] += _fold8(dgl1)
        dpa_ref[:, 0:2048] = _c(dya * mixed * sz)
        dpa_ref[:, 4096:6144] = _c(dya * (u * mixed) * (sig * (1.0 + za * (1.0 - sig))))
        dmixed = dya * u * sz
        dbf_ref[...] += dmixed
        dmb = _c(dmixed)
        for gi in range(SGU_GROUPS):
            sl = slice(128 * gi, 128 * gi + 128)
            dvn_ref[:, sl] = _dot(wmt_ref[gi], dmb[:, sl])
            dws_ref[gi] += _dot_nt(dmb[:, sl], vnb[:, sl])
        dvn = dvn_ref[...]
        dgam_ref[...] += _fold8(dvn * xhat)
        dbeta_ref[...] += _fold8(dvn)
        dxh = dvn * g_ref[...]
        dv = rstd * (dxh - jnp.mean(dxh, axis=-1, keepdims=True) - xhat * jnp.mean(dxh * xhat, axis=-1, keepdims=True))
        dpa_ref[:, 2048:4096] = _c(dv)

    row = pl.BlockSpec((t, D_MODEL), lambda i: (i, 0))
    vec = lambda w: pl.BlockSpec((1, w), lambda i: (0, 0))
    acc = lambda w: pl.BlockSpec((8, w), lambda i: (0, 0))
    wspec = pl.BlockSpec((SGU_GROUPS, 128, 128), lambda i: (0, 0, 0))
    return pl.pallas_call(
        body, name="tok_bwd",
        grid=(s // t,),
        in_specs=[pl.BlockSpec((t, NA), lambda i: (i, 0)), row, row, row, vec(4096), vec(2048), vec(2048),
                  wspec, wspec, pl.BlockSpec((128, D_MODEL), lambda i: (0, 0))],
        out_specs=[pl.BlockSpec((t, NA), lambda i: (i, 0)), row, acc(4096), acc(2048), acc(2048),
                   pl.BlockSpec((128, D_MODEL), lambda i: (0, 0)), wspec],
        out_shape=[jax.ShapeDtypeStruct((s, NA), MXU_DTYPE), jax.ShapeDtypeStruct((s, D_MODEL), MXU_DTYPE),
                   jax.ShapeDtypeStruct((8, 4096), F32), jax.ShapeDtypeStruct((8, 2048), F32),
                   jax.ShapeDtypeStruct((8, 2048), F32), jax.ShapeDtypeStruct((128, D_MODEL), F32),
                   jax.ShapeDtypeStruct((SGU_GROUPS, 128, 128), F32)],
        scratch_shapes=[pltpu.VMEM((t, D_MODEL), F32), pltpu.VMEM((t, D_MODEL), F32)],
        compiler_params=_cparams(("arbitrary",), 2 * t * NA * 6 + 16 * t * D_MODEL * 4),
    )(proj_a, dmerged, y_a, y_b, gate_b, sgu_g, sgu_beta, wm, wmt, bias_full)


def _out_call(merged, x, target, w_out, fnw, t):
    s = x.shape[0]
    nt = s // t

    def body(mg_ref, x_ref, t_ref, w_ref, fw_ref, dh_ref, dhb_ref, dmg_ref, loss_ref, dfw_ref):
        @pl.when(pl.program_id(0) == 0)
        def _():
            dfw_ref[...] = jnp.zeros_like(dfw_ref)

        h = x_ref[...] + _dot(mg_ref[...], w_ref[...])
        r = lax.rsqrt(jnp.mean(h * h, axis=-1, keepdims=True) + EPS)
        hn = h * r
        err = hn * fw_ref[...] - t_ref[...]
        loss_ref[...] = jnp.full(loss_ref.shape, 0.5 * jnp.sum(jnp.mean(err * err, axis=-1, keepdims=True)), F32)
        dy = err * (1.0 / D_MODEL)
        dfw_ref[...] += _fold8(dy * hn)
        gw = dy * fw_ref[...]
        dh = r * gw - h * (r * r * r) * jnp.mean(h * gw, axis=-1, keepdims=True)
        dh_ref[...] = dh
        dhb = _c(dh)
        dhb_ref[...] = dhb
        dmg_ref[...] = _c(_dot_nt(dhb, w_ref[...]))

    row = pl.BlockSpec((t, D_MODEL), lambda i: (i, 0))
    return pl.pallas_call(
        body, name="out_proj_loss",
        grid=(nt,),
        in_specs=[row, row, row, pl.BlockSpec((D_MODEL, D_MODEL), lambda i: (0, 0)), pl.BlockSpec((1, D_MODEL), lambda i: (0, 0))],
        out_specs=[row, row, row, pl.BlockSpec((1, 8, 128), lambda i: (i, 0, 0)), pl.BlockSpec((8, D_MODEL), lambda i: (0, 0))],
        out_shape=[jax.ShapeDtypeStruct((s, D_MODEL), F32), jax.ShapeDtypeStruct((s, D_MODEL), MXU_DTYPE),
                   jax.ShapeDtypeStruct((s, D_MODEL), MXU_DTYPE), jax.ShapeDtypeStruct((nt, 8, 128), F32),
                   jax.ShapeDtypeStruct((8, D_MODEL), F32)],
        compiler_params=_cparams(("arbitrary",), 2 * D_MODEL * D_MODEL * 2 + 2 * t * D_MODEL * 24),
    )(merged, x, target, w_out, fnw)


def _ssd_fwd_call(proj_b, dtb, alog, dsk, cw, cb, nw, t, ng):
    s = proj_b.shape[0]
    nt, nch = s // t, t // CHUNK

    def body(pb_ref, halo_ref, dtb_ref, al_ref, ds_ref, cw_ref, cb_ref, nw_ref, y_ref, yb_ref, hp_ref, pre_ref,
             dt_ref, acs_ref, ht_ref, prev_ref):
        i = pl.program_id(1)

        @pl.when(i == 0)
        def _():
            ht_ref[...] = jnp.zeros_like(ht_ref)

        for gi in range(ng):
            prev_ref[:, 768 * gi:768 * gi + 768] = jnp.where(i == 0, 0.0, halo_ref[:, FW_B * gi:FW_B * gi + 768])
        masks = _ssd_masks()
        tri, _ = _cumsum_mats(CHUNK)
        a_neg = -jnp.exp(al_ref[...])

        def chunk(c, carry):
            rows = pl.ds(pl.multiple_of(c * CHUNK, CHUNK), CHUNK)
            for gi in range(ng):
                fo, co, go = FW_B * gi, 768 * gi, SSD_GW * gi
                xbc = pb_ref[rows, fo:fo + 768]
                taps = _conv_taps(jnp.concatenate([prev_ref[:, co:co + 768], xbc], axis=0), CHUNK)
                prev_ref[:, co:co + 768] = xbc[CHUNK - 8:CHUNK]
                pre = cb_ref[:, co:co + 768]
                for k in range(4):
                    pre = pre + taps[k] * cw_ref[k:k + 1, co:co + 768]
                pre_ref[rows, co:co + 768] = pre
                act = pre * _sigmoid(pre)
                dt = _softplus(pb_ref[rows, fo + 768:fo + 1280] + dtb_ref[:, go:go + 512])
                acs = _dot01_l(tri, dt * a_neg[:, go:go + 512], 3)
                dt_ref[rows, go:go + 512] = dt
                acs_ref[rows, go:go + 512] = acs
                ht = ht_ref[gi]
                hp_ref[c, :, go:go + 512] = ht
                y, ht_new = _ssd_chunk_fwd(act[:, 0:512], act[:, 512:640], act[:, 640:768], dt, acs,
                                           ds_ref[:, go:go + 512], ht, masks)
                y_ref[rows, go:go + 512] = y
                ht_ref[gi] = ht_new
                zb = pb_ref[rows, fo + 1280:fo + 1792]
                hh = y * (zb * _sigmoid(zb))
                rr = lax.rsqrt(jnp.mean(hh * hh, axis=-1, keepdims=True) + EPS)
                yb_ref[rows, go:go + 512] = _c(hh * rr * nw_ref[:, go:go + 512])
            return carry

        lax.fori_loop(0, nch, chunk, 0)

    gvec = lambda w: pl.BlockSpec((1, ng * w), lambda g, i: (0, g))
    return pl.pallas_call(
        body, name="ssd_fwd",
        grid=(SSD_GROUPS // ng, nt),
        in_specs=[pl.BlockSpec((t, ng * FW_B), lambda g, i: (i, g)),
                  pl.BlockSpec((8, ng * FW_B), lambda g, i: (jnp.maximum(i * (t // 8) - 1, 0), g)),
                  gvec(512), gvec(512), gvec(512),
                  pl.BlockSpec((4, ng * 768), lambda g, i: (0, g)), gvec(768), gvec(512)],
        out_specs=[pl.BlockSpec((t, ng * SSD_GW), lambda g, i: (i, g)), pl.BlockSpec((t, ng * SSD_GW), lambda g, i: (i, g)),
                   pl.BlockSpec((nch, SSD_STATE, ng * SSD_GW), lambda g, i: (i, 0, g)),
                   pl.BlockSpec((t, ng * 768), lambda g, i: (i, g)),
                   pl.BlockSpec((t, ng * SSD_GW), lambda g, i: (i, g)), pl.BlockSpec((t, ng * SSD_GW), lambda g, i: (i, g))],
        out_shape=[jax.ShapeDtypeStruct((s, D_MODEL), F32), jax.ShapeDtypeStruct((s, D_MODEL), MXU_DTYPE),
                   jax.ShapeDtypeStruct((s // CHUNK, SSD_STATE, D_MODEL), F32),
                   jax.ShapeDtypeStruct((s, SSD_GROUPS * 768), F32),
                   jax.ShapeDtypeStruct((s, D_MODEL), F32), jax.ShapeDtypeStruct((s, D_MODEL), F32)],
        scratch_shapes=[pltpu.VMEM((ng, SSD_STATE, SSD_GW), F32), pltpu.VMEM((8, ng * 768), F32)],
        compiler_params=_cparams(("parallel", "arbitrary"), ng * (2 * t * FW_B * 4 + 16 * t * SSD_GW * 4) + 16 * 1024 * 1024),
    )(proj_b, proj_b, dtb, alog, dsk, cw, cb, nw)


def _ssd_bwd_call(proj_b, pre_all, dt_all, acs_all, dyb, y, hprev, dtb, alog, dsk, cw, nw, t, ng):
    s = proj_b.shape[0]
    nt, nch = s // t, t // CHUNK

    def body(pb_ref, pre_ref, dt_ref, acs_ref, dyb_ref, y_ref, hp_ref, dtb_ref, al_ref, ds_ref, cw_ref, nw_ref,
             dpb_ref, a512_ref, a768_ref, dht_ref, nxt_ref, q_ref, p1_ref):
        i = pl.program_id(1)

        @pl.when(i == 0)
        def _():
            dht_ref[...] = jnp.zeros_like(dht_ref)
            nxt_ref[...] = jnp.zeros_like(nxt_ref)
            a512_ref[...] = jnp.zeros_like(a512_ref)
            a768_ref[...] = jnp.zeros_like(a768_ref)

        _, trit = _cumsum_mats(t)
        a_neg = -jnp.exp(al_ref[...])
        masks = _ssd_masks()

        def chunk(cc, carry):
            c = nch - 1 - cc
            rows = pl.ds(pl.multiple_of(c * CHUNK, CHUNK), CHUNK)
            for gi in range(ng):
                fo, co, go, bo = FW_B * gi, 768 * gi, SSD_GW * gi, BW_B * gi
                pre = pre_ref[rows, co:co + 768]
                sp = _sigmoid(pre)
                act = pre * sp
                zb = pb_ref[rows, fo + 1280:fo + 1792]
                yv = y_ref[rows, go:go + 512]
                sgz = _sigmoid(zb)
                sz = zb * sgz
                hh = yv * sz
                rr = lax.rsqrt(jnp.mean(hh * hh, axis=-1, keepdims=True) + EPS)
                dyb = dyb_ref[rows, go:go + 512].astype(F32)
                a512_ref[gi, 0] += _fold8(dyb * (hh * rr))
                tt = dyb * nw_ref[:, go:go + 512]
                dhh = rr * tt - hh * (rr * rr * rr) * jnp.mean(hh * tt, axis=-1, keepdims=True)
                dpb_ref[rows, bo:bo + 512] = _c(dhh * yv * (sgz * (1.0 + zb * (1.0 - sgz))))
                dxs, d_b, d_c, dht_prev, dyxs, qq, p1 = _ssd_chunk_bwd(
                    act[:, 0:512], act[:, 512:640], act[:, 640:768], dt_ref[rows, go:go + 512], acs_ref[rows, go:go + 512],
                    ds_ref[:, go:go + 512], hp_ref[c, :, go:go + 512], dht_ref[gi], dhh * sz, masks)
                dht_ref[gi] = dht_prev
                q_ref[rows, go:go + 512] = qq
                p1_ref[rows, go:go + 512] = p1
                a512_ref[gi, 1] += _fold8(dyxs)
                dpre = jnp.concatenate([dxs, d_b, d_c], axis=1) * (sp * (1.0 + pre * (1.0 - sp)))
                xbc = pb_ref[rows, fo:fo + 768]
                a768_ref[gi, 4] += _fold8(dpre)
                a768_ref[gi, 3] += _fold8(dpre * xbc)
                dpad = jnp.concatenate([dpre, nxt_ref[:, co:co + 768]], axis=0)
                dx = dpre * cw_ref[3:4, co:co + 768]
                for k in range(3):
                    d_k = pltpu.roll(dpad, CHUNK + 8 - (3 - k), 0)[0:CHUNK]
                    dx = dx + d_k * cw_ref[k:k + 1, co:co + 768]
                    a768_ref[gi, k] += _fold8(d_k * xbc)
                nxt_ref[:, co:co + 768] = dpre[0:8]
                dpb_ref[rows, bo + 512:bo + 1280] = _c(dx)
            return carry

        lax.fori_loop(0, nch, chunk, 0)

        rsel = _c(jnp.where(lax.shift_right_logical(_iota((SSD_GW, 128), 0), 6) == _iota((SSD_GW, 128), 1), 1.0, 0.0))
        for gi in range(ng):
            fo, go, bo = FW_B * gi, SSD_GW * gi, BW_B * gi
            sig = _sigmoid(pb_ref[:, fo + 768:fo + 1280] + dtb_ref[:, go:go + 512])
            ddtr, dadt = _ssd_finish_dt(q_ref[:, go:go + 512], p1_ref[:, go:go + 512], dt_ref[:, go:go + 512], sig,
                                        a_neg[:, go:go + 512], trit, masks[3])
            dpb_ref[:, bo + 1280:bo + 1408] = _c(_dot01_r(ddtr, rsel, 2))
            a512_ref[gi, 2] += _fold8(dadt)
            a512_ref[gi, 3] += _fold8(ddtr)

    gvec = lambda w: pl.BlockSpec((1, ng * w), lambda g, i: (0, g))
    rev = lambda w: pl.BlockSpec((t, ng * w), lambda g, i: (nt - 1 - i, g))
    return pl.pallas_call(
        body, name="ssd_bwd",
        grid=(SSD_GROUPS // ng, nt),
        in_specs=[rev(FW_B), rev(768), rev(SSD_GW), rev(SSD_GW), rev(SSD_GW), rev(SSD_GW),
                  pl.BlockSpec((nch, SSD_STATE, ng * SSD_GW), lambda g, i: (nt - 1 - i, 0, g)),
                  gvec(512), gvec(512), gvec(512),
                  pl.BlockSpec((4, ng * 768), lambda g, i: (0, g)), gvec(512)],
        out_specs=[rev(BW_B),
                   pl.BlockSpec((ng, 4, 8, 512), lambda g, i: (g, 0, 0, 0)),
                   pl.BlockSpec((ng, 5, 8, 768), lambda g, i: (g, 0, 0, 0))],
        out_shape=[jax.ShapeDtypeStruct((s, SSD_GROUPS * BW_B), MXU_DTYPE),
                   jax.ShapeDtypeStruct((SSD_GROUPS, 4, 8, 512), F32),
                   jax.ShapeDtypeStruct((SSD_GROUPS, 5, 8, 768), F32)],
        scratch_shapes=[pltpu.VMEM((ng, SSD_STATE, SSD_GW), F32), pltpu.VMEM((8, ng * 768), F32),
                        pltpu.VMEM((t, ng * SSD_GW), F32), pltpu.VMEM((t, ng * SSD_GW), F32)],
        compiler_params=_cparams(("parallel", "arbitrary"), ng * (2 * t * FW_B * 4 + 18 * t * SSD_GW * 4) + 16 * 1024 * 1024),
    )(proj_b, pre_all, dt_all, acs_all, dyb, y, hprev, dtb, alog, dsk, cw, nw)


def _rows_call(body, ins, outs, tr, name):
    r = ins[0].shape[0]
    spec = lambda a: pl.BlockSpec((tr, a.shape[1]), lambda i: (i, 0))
    est = 2 * tr * sum(a.shape[1] * jnp.dtype(a.dtype).itemsize for a in list(ins) + list(outs))
    return pl.pallas_call(
        body, name=name, grid=(r // tr,),
        in_specs=[spec(a) for a in ins], out_specs=[spec(o) for o in outs], out_shape=list(outs),
        compiler_params=_cparams(("parallel",), est),
    )(*ins)


def _add_pair(a, b, tr, name):
    def body(a_ref, b_ref, o_ref):
        o_ref[...] = a_ref[...] + b_ref[...]

    return _rows_call(body, [a, b], [jax.ShapeDtypeStruct(a.shape, F32)], tr, name)[0]


def _rs_add(p, sib, place, tr, name):
    _, r, c = p.shape
    half = r // 2
    nb = half // tr

    def body(pl_ref, p_ref, s_ref, b_ref, own_ref):
        v = p_ref[0] + s_ref[0]
        b_ref[0] = v.astype(jnp.bfloat16)

        @pl.when(pl.program_id(1) == pl_ref[0])
        def _():
            own_ref[...] = v

    return pl.pallas_call(
        body, name=name,
        grid_spec=pltpu.PrefetchScalarGridSpec(
            num_scalar_prefetch=1, grid=(nb, 4),
            in_specs=[pl.BlockSpec((1, tr, c), lambda i, k, pr: (k, pr[1] * nb + i, 0)),
                      pl.BlockSpec((1, tr, c), lambda i, k, pr: (k, i, 0))],
            out_specs=[pl.BlockSpec((1, tr, c), lambda i, k, pr: (k, i, 0)),
                       pl.BlockSpec((tr, c), lambda i, k, pr: (i, 0))]),
        out_shape=[jax.ShapeDtypeStruct((4, half, c), jnp.bfloat16), jax.ShapeDtypeStruct((half, c), F32)],
        compiler_params=_cparams(("parallel", "arbitrary"), 2 * tr * c * 14),
    )(place, p, sib)


WIN_STEP = 3840
WIN_W = 3968


def _rs_add_windows(dw, sib, place, tr, name):
    r = dw.shape[0]
    half = r // 2
    nb = half // tr
    tail = WIN_W - WIN_STEP

    def body(pl_ref, pm_ref, pt_ref, s_ref, b_ref, own_ref):
        vm = pm_ref[...] + s_ref[0, :, 0:WIN_STEP]
        vt = pt_ref[...] + s_ref[0, :, WIN_STEP:WIN_W]
        b_ref[0, :, 0:WIN_STEP] = vm.astype(jnp.bfloat16)
        b_ref[0, :, WIN_STEP:WIN_W] = vt.astype(jnp.bfloat16)

        @pl.when(pl.program_id(1) == pl_ref[0])
        def _():
            own_ref[:, 0:WIN_STEP] = vm
            own_ref[:, WIN_STEP:WIN_W] = vt

    return pl.pallas_call(
        body, name=name,
        grid_spec=pltpu.PrefetchScalarGridSpec(
            num_scalar_prefetch=1, grid=(nb, 4),
            in_specs=[pl.BlockSpec((tr, WIN_STEP), lambda i, k, pr: (pr[1] * nb + i, k)),
                      pl.BlockSpec((tr, tail), lambda i, k, pr: (pr[1] * nb + i, (WIN_STEP // tail) * (k + 1))),
                      pl.BlockSpec((1, tr, WIN_W), lambda i, k, pr: (k, i, 0))],
            out_specs=[pl.BlockSpec((1, tr, WIN_W), lambda i, k, pr: (k, i, 0)),
                       pl.BlockSpec((tr, WIN_W), lambda i, k, pr: (i, 0))]),
        out_shape=[jax.ShapeDtypeStruct((4, half, WIN_W), jnp.bfloat16), jax.ShapeDtypeStruct((half, WIN_W), F32)],
        compiler_params=_cparams(("parallel", "arbitrary"), 2 * tr * WIN_W * 14),
    )(place, dw, dw, sib)


def _sum_own_recv(own, recv, tr, name):
    r, c = own.shape

    def body(o_ref, r_ref, out_ref):
        v = o_ref[...]
        for j in range(3):
            v = v + r_ref[j].astype(F32)
        out_ref[...] = v

    return pl.pallas_call(
        body, name=name, grid=(r // tr,),
        in_specs=[pl.BlockSpec((tr, c), lambda i: (i, 0)), pl.BlockSpec((3, tr, c), lambda i: (0, i, 0))],
        out_specs=pl.BlockSpec((tr, c), lambda i: (i, 0)),
        out_shape=jax.ShapeDtypeStruct((r, c), F32),
        compiler_params=_cparams(("parallel",), 2 * tr * c * 14),
    )(own, recv)


def _sum_slots(stack, name):
    n, r, w = stack.shape

    def body(a_ref, out_ref):
        v = a_ref[0]
        for k in range(1, n):
            v = v + a_ref[k]
        out_ref[...] = v

    return pl.pallas_call(
        body, name=name, grid=(1,),
        in_specs=[pl.BlockSpec((n, r, w), lambda i: (0, 0, 0))],
        out_specs=pl.BlockSpec((r, w), lambda i: (0, 0)),
        out_shape=jax.ShapeDtypeStruct((r, w), F32),
        compiler_params=_cparams(("arbitrary",), 2 * (n + 1) * r * w * 4),
    )(stack)


def _adamw(w, g, m, v, tr, name):
    def body(w_ref, g_ref, m_ref, v_ref, d_ref, nm_ref, nv_ref):
        d_ref[...], nm_ref[...], nv_ref[...] = _adam_math(w_ref[...], g_ref[...], m_ref[...], v_ref[...])

    o = jax.ShapeDtypeStruct(w.shape, F32)
    return _rows_call(body, [w, g, m, v], [o, o, o], tr, name)


def _adam_math(w, g, m, v):
    nm = ADAM_B1 * m + (1.0 - ADAM_B1) * g
    nv = ADAM_B2 * v + (1.0 - ADAM_B2) * (g * g)
    m_hat = nm / (1.0 - ADAM_B1 ** ADAM_STEP)
    v_hat = nv / (1.0 - ADAM_B2 ** ADAM_STEP)
    return -ADAM_LR * (m_hat / (jnp.sqrt(v_hat) + ADAM_EPS) + ADAM_WD * w), nm, nv


def _adamw_halves(w, g_own, g_sib, m, v, place, tr, name):
    r, c = w.shape

    def body(pl_ref, w_ref, go_ref, gs_ref, m_ref, v_ref, g_ref, d_ref, nm_ref, nv_ref):
        first = pl_ref[1] == 0
        own, sib = go_ref[...], gs_ref[...]
        g = jnp.concatenate([jnp.where(first, own, sib), jnp.where(first, sib, own)], axis=1)
        g_ref[...] = g
        d_ref[...], nm_ref[...], nv_ref[...] = _adam_math(w_ref[...], g, m_ref[...], v_ref[...])

    full = pl.BlockSpec((tr, c), lambda i, pr: (i, 0))
    half = pl.BlockSpec((tr, c // 2), lambda i, pr: (i, 0))
    o = jax.ShapeDtypeStruct((r, c), F32)
    return pl.pallas_call(
        body, name=name,
        grid_spec=pltpu.PrefetchScalarGridSpec(num_scalar_prefetch=1, grid=(r // tr,),
                                               in_specs=[full, half, half, full, full], out_specs=[full] * 4),
        out_shape=[o] * 4,
        compiler_params=_cparams(("parallel",), 2 * tr * c * 4 * 8),
    )(place, w, g_own, g_sib, m, v)


ANY = pl.BlockSpec(memory_space=pl.ANY)


def _place():
    x, y, c = lax.axis_index("x"), lax.axis_index("y"), lax.axis_index("c")
    others = [(1 - x, y), (x, 1 - y), (1 - x, 1 - y)]
    return x, y, c, 2 * x + y, others


def _remote(src, dst, send, recv, k, to):
    return pltpu.make_async_remote_copy(src_ref=src, dst_ref=dst, send_sem=send.at[k], recv_sem=recv.at[k],
                                        device_id=to, device_id_type=MESH)


def _norm_gather_call(x, norm_w, win_b, wout_b, cw8, tm):
    s = x.shape[0]
    ni = s // tm

    def body(x_ref, w_ref, win, wout, cw, xn_ref, xnt_ref, g_in, g_out, g_cw, send, recv):
        i = pl.program_id(0)

        def direct():
            xx, yy, c, me, others = _place()
            cps = []
            for a, (src, dst) in enumerate(((win, g_in), (wout, g_out))):
                half = src.shape[0] // 2
                mine = pl.ds(c * half, half)
                cps += [_remote(src.at[mine], dst.at[me, mine], send, recv, 6 * a + j, (*chip, c)) for j, chip in enumerate(others)]
            cps += [_remote(cw, g_cw.at[me], send, recv, 12 + j, (*chip, c)) for j, chip in enumerate(others)]
            return cps

        @pl.when(i == 0)
        def _():
            for cp in direct():
                cp.start()

        xv = x_ref[...]
        r = lax.rsqrt(jnp.mean(xv * xv, axis=-1, keepdims=True) + EPS)
        xn = xv * r * w_ref[...]
        xn_ref[...] = _c(xn)
        xnt_ref[...] = _c(xn.T)

        @pl.when(i == ni - 1)
        def _():
            xx, yy, c, me, others = _place()
            sib = (xx, yy, 1 - c)
            passed = []
            for a, dst in enumerate((g_in, g_out)):
                half = dst.shape[1] // 2
                mine = pl.ds(c * half, half)
                for j, chip in enumerate(others):
                    kj = 2 * chip[0] + chip[1]
                    _remote(dst.at[kj, mine], dst.at[kj, mine], send, recv, 6 * a + j, (*chip, c)).wait_recv()
                    cp = _remote(dst.at[kj, mine], dst.at[kj, mine], send, recv, 6 * a + 3 + j, sib)
                    cp.start()
                    passed.append(cp)
            for a, dst in enumerate((g_in, g_out)):
                half = dst.shape[1] // 2
                theirs = pl.ds((1 - c) * half, half)
                for j, chip in enumerate(others):
                    kj = 2 * chip[0] + chip[1]
                    _remote(dst.at[kj, theirs], dst.at[kj, theirs], send, recv, 6 * a + 3 + j, sib).wait_recv()
            for j, chip in enumerate(others):
                kj = 2 * chip[0] + chip[1]
                _remote(cw, g_cw.at[kj], send, recv, 12 + j, (*chip, c)).wait_recv()
            for cp in direct() + passed:
                cp.wait_send()

    outs = [jax.ShapeDtypeStruct((s, D_MODEL), MXU_DTYPE), jax.ShapeDtypeStruct((D_MODEL, s), MXU_DTYPE)]
    outs += [jax.ShapeDtypeStruct((4,) + a.shape, a.dtype) for a in (win_b, wout_b, cw8)]
    return pl.pallas_call(
        body, name="rmsnorm_gather_weights",
        grid=(ni,),
        in_specs=[pl.BlockSpec((tm, D_MODEL), lambda i: (i, 0)), pl.BlockSpec((1, D_MODEL), lambda i: (0, 0)), ANY, ANY, ANY],
        out_specs=[pl.BlockSpec((tm, D_MODEL), lambda i: (i, 0)), pl.BlockSpec((D_MODEL, tm), lambda i: (0, i)), ANY, ANY, ANY],
        out_shape=outs,
        scratch_shapes=[pltpu.SemaphoreType.DMA((15,)), pltpu.SemaphoreType.DMA((15,))],
        compiler_params=_cparams(("arbitrary",), 2 * tm * D_MODEL * 12),
    )(x, norm_w, win_b, wout_b, cw8)


def _dw_out_rs_call(a, b, dw, *, tn, tk):
    m, k = a.shape
    n = b.shape[1]
    nj, nk = n // tn, k // tk
    half = dw.shape[0] // 2

    def body(a_ref, b_ref, pin, o_ref, sib_in, send, recv):
        j, kk = pl.program_id(0), pl.program_id(1)

        def copies():
            x, y, c, me, others = _place()
            rows = pl.ds((1 - c) * half, half)
            return [_remote(pin.at[rows, pl.ds(WIN_STEP * w, WIN_W)], sib_in.at[w], send, recv, w, (x, y, 1 - c)) for w in range(4)]

        @pl.when((j == 0) & (kk == 0))
        def _():
            for cp in copies():
                cp.start()

        @pl.when(kk == 0)
        def _():
            o_ref[...] = jnp.zeros_like(o_ref)

        o_ref[...] += _dot(a_ref[...], b_ref[...])

        @pl.when((j == nj - 1) & (kk == nk - 1))
        def _():
            cps = copies()
            for cp in cps:
                cp.wait_recv()
            for cp in cps:
                cp.wait_send()

    isz = jnp.dtype(a.dtype).itemsize
    est = 2 * (m * tk + tk * tn) * isz + 2 * m * tn * 4
    outs = [jax.ShapeDtypeStruct((m, n), F32), jax.ShapeDtypeStruct((4, half, WIN_W), dw.dtype)]
    return pl.pallas_call(
        body, name="dw_out_rs_sibling",
        grid=(nj, nk),
        in_specs=[pl.BlockSpec((m, tk), lambda j, kk: (0, kk)), pl.BlockSpec((tk, tn), lambda j, kk: (kk, j)), ANY],
        out_specs=[pl.BlockSpec((m, tn), lambda j, kk: (0, j)), ANY],
        out_shape=outs,
        scratch_shapes=[pltpu.SemaphoreType.DMA((4,)), pltpu.SemaphoreType.DMA((4,))],
        compiler_params=_cparams(("arbitrary", "arbitrary"), est),
    )(a, b, dw)


def _rs_sibling_call(p_out, vsmall):
    def body(pout, vs, sib_out, sib_v, send, recv):
        x, y, c, me, others = _place()
        sib = (x, y, 1 - c)
        half = pout.shape[1] // 2
        cps = [_remote(pout.at[:, pl.ds((1 - c) * half, half)], sib_out, send, recv, 0, sib),
               _remote(vs, sib_v, send, recv, 1, sib)]
        for cp in cps:
            cp.start()
        for cp in cps:
            cp.wait_recv()
        for cp in cps:
            cp.wait_send()

    outs = [jax.ShapeDtypeStruct((4, p_out.shape[1] // 2, p_out.shape[2]), p_out.dtype),
            jax.ShapeDtypeStruct(vsmall.shape, vsmall.dtype)]
    return pl.pallas_call(
        body, name="rs_sibling",
        in_specs=[ANY] * 2, out_specs=[ANY] * 2, out_shape=outs,
        scratch_shapes=[pltpu.SemaphoreType.DMA((2,)), pltpu.SemaphoreType.DMA((2,))],
    )(p_out, vsmall)


def _rs_join_call(f_in, f_out, nw8):
    def body(fin, fout, nw, sib_in, full_out, all_nw, send, recv):
        x, y, c, me, others = _place()
        sib = (x, y, 1 - c)
        half = fout.shape[0]
        cps = [_remote(fin, sib_in, send, recv, 0, sib),
               _remote(fout, full_out.at[pl.ds(c * half, half)], send, recv, 1, sib)]
        mine = 4 * x + 2 * y + c
        peers = []
        for r in range(1, 8):
            px, py, pc = (1 - x if r & 4 else x), (1 - y if r & 2 else y), (1 - c if r & 1 else c)
            peers.append((r, (px, py, pc), 4 * px + 2 * py + pc))
            cps.append(_remote(nw, all_nw.at[mine], send, recv, 1 + r, (px, py, pc)))
        for cp in cps:
            cp.start()
        cps[0].wait_recv()
        _remote(fout, full_out.at[pl.ds((1 - c) * half, half)], send, recv, 1, sib).wait_recv()
        for r, peer, idx in peers:
            _remote(nw, all_nw.at[idx], send, recv, 1 + r, peer).wait_recv()
        for cp in cps:
            cp.wait_send()

    outs = [jax.ShapeDtypeStruct(f_in.shape, F32), jax.ShapeDtypeStruct((2 * f_out.shape[0], f_out.shape[1]), F32),
            jax.ShapeDtypeStruct((8,) + nw8.shape, F32)]
    return pl.pallas_call(
        body, name="rs_join",
        in_specs=[ANY] * 3, out_specs=[ANY] * 3, out_shape=outs,
        scratch_shapes=[pltpu.SemaphoreType.DMA((9,)), pltpu.SemaphoreType.DMA((9,))],
    )(f_in, f_out, nw8)


def _pack(arrs):
    parts = []
    for a in arrs:
        f = a.reshape(-1).astype(F32)
        pad = (-f.shape[0]) % 1024
        parts.append(jnp.pad(f, (0, pad)).reshape(-1, 128))
    return jnp.concatenate(parts, axis=0)


def _unpack(packed, shapes):
    out, row = [], 0
    for shp in shapes:
        n = 1
        for d in shp:
            n *= d
        rows = (n + 1023) // 1024 * 8
        out.append(packed[row:row + rows].reshape(-1)[:n].reshape(shp))
        row += rows
    return out


def _expand_heads(v32):
    return jnp.repeat(v32.reshape(32), HEADDIM).reshape(1, D_MODEL)


def kernel(x, norm_w, w_in, gate_b, sgu_norm_g, sgu_norm_b, sgu_w, sgu_b, conv_w, conv_b, dt_bias, A_log, D_skip, ssd_norm_w, w_out, final_norm_w, loss_target, m_norm_w, m_w_in, m_gate_b, m_sgu_norm_g, m_sgu_norm_b, m_sgu_w, m_sgu_b, m_conv_w, m_conv_b, m_dt_bias, m_A_log, m_D_skip, m_ssd_norm_w, m_w_out, m_final_norm_w, v_norm_w, v_w_in, v_gate_b, v_sgu_norm_g, v_sgu_norm_b, v_sgu_w, v_sgu_b, v_conv_w, v_conv_b, v_dt_bias, v_A_log, v_D_skip, v_ssd_norm_w, v_w_out, v_final_norm_w):
    s = x.shape[1]
    x2 = x.reshape(s, D_MODEL)
    tgt = loss_target.reshape(s, D_MODEL)
    t_ssd, t_tok, t_out, t_row = min(T_SSD, s), min(T_TOK, s), min(T_OUT, s), min(T_ROW, s)
    tm_mm, tk_dw = min(TM_MM, s), min(TK_DW, s)
    chip = 2 * lax.axis_index("x") + lax.axis_index("y")

    cw8 = jnp.pad(conv_w[0], ((0, 4), (0, 0)))
    win_b, wout_b = _c(w_in[0]), _c(w_out[0])
    xn, xnt, g_in, g_out, g_cw = _norm_gather_call(x2, norm_w, win_b, wout_b, cw8, t_row)
    g_in = lax.dynamic_update_index_in_dim(g_in, win_b, chip, 0)
    g_out = lax.dynamic_update_index_in_dim(g_out, wout_b, chip, 0)
    g_cw = lax.dynamic_update_index_in_dim(g_cw, cw8, chip, 0)
    wt = jnp.transpose(g_in, (0, 2, 1)).reshape(IN_W, D_MODEL)
    w_out_full = g_out.reshape(D_MODEL, D_MODEL)
    conv_w_full = jnp.transpose(g_cw[:, 0:4, :], (1, 0, 2)).reshape(4, 3072)

    wt_a = jnp.concatenate([wt[0:6144], wt[11296:15392]], axis=0)
    fw, bw = [], []
    for g in range(SSD_GROUPS):
        xs_g = wt[8192 + 512 * g:8192 + 512 * g + 512]
        b_g = wt[10240 + 128 * g:10240 + 128 * g + 128]
        c_g = wt[10752 + 128 * g:10752 + 128 * g + 128]
        zb_g = wt[6144 + 512 * g:6144 + 512 * g + 512]
        dt_g = wt[11264 + 8 * g:11264 + 8 * g + 8]
        fw += [xs_g, b_g, c_g, jnp.repeat(dt_g, HEADDIM, axis=0), zb_g]
        bw += [zb_g, xs_g, b_g, c_g, jnp.pad(dt_g, ((0, 120), (0, 0)))]
    wt_bf = jnp.concatenate(fw, axis=0)
    wt_b = jnp.concatenate(bw, axis=0)

    def group_cols(full_xs, full_bc):
        parts = []
        for g in range(SSD_GROUPS):
            parts += [full_xs[:, 512 * g:512 * g + 512], full_bc[:, 128 * g:128 * g + 128], full_bc[:, 512 + 128 * g:512 + 128 * g + 128]]
        return jnp.concatenate(parts, axis=1)

    cw_g = group_cols(conv_w_full[:, 0:2048], conv_w_full[:, 2048:3072])
    cb_g = group_cols(conv_b[:, 0:2048], conv_b[:, 2048:3072])
    dtb_e, alog_e, dsk_e = _expand_heads(dt_bias), _expand_heads(A_log), _expand_heads(D_skip)

    pos_chunk = jnp.arange(SGU_BLOCK) // CHUNK
    smask = pos_chunk[None, :] <= pos_chunk[:, None]
    wm_f = jnp.where(smask[None], sgu_w[0], 0.0)
    wm = _c(wm_f)
    wmt = _c(jnp.transpose(wm_f, (0, 2, 1)))
    bias_full = jnp.repeat(sgu_b[0].T, D_MODEL // SGU_GROUPS, axis=1)
    fnw = final_norm_w.reshape(1, D_MODEL)

    proj_a = _mm(xn, wt_a, tm=tm_mm, tn=1024, tk=D_MODEL, name="in_proj_a", out_dtype=MXU_DTYPE, b_is_t=True)
    proj_b = _mm(xn, wt_bf, tm=tm_mm, tn=1024, tk=D_MODEL, name="in_proj_b", b_is_t=True)
    y_ssd, y_b, hprev, pre_all, dt_all, acs_all = _ssd_fwd_call(proj_b, dtb_e, alog_e, dsk_e, cw_g, cb_g, ssd_norm_w, t_ssd, NG_SSD)
    y_a, merged, merged_t = _tok_fwd_call(proj_a, y_b, gate_b, sgu_norm_g, sgu_norm_b, wm, bias_full, t_tok)
    dh, dh_b, dmerged, loss_t, dfw8 = _out_call(merged, x2, tgt, w_out_full, fnw, t_out)

    dproj_a, dy_b, dgb8, dgam8, dbeta8, dbfull, dws = _tok_bwd_call(
        proj_a, dmerged, y_a, y_b, gate_b, sgu_norm_g, sgu_norm_b, wm, wmt, bias_full, t_tok)
    dproj_b, a512, a768 = _ssd_bwd_call(proj_b, pre_all, dt_all, acs_all, dy_b, y_ssd, hprev, dtb_e, alog_e, dsk_e, cw_g,
                                        ssd_norm_w, t_ssd, NG_SSD)
    dw_uvz = _mm(xnt, dproj_a, tm=D_MODEL, tn=1024, tk=min(2 * tk_dw, s), name="dw_in_uvz", n=6144)
    dw_gate = _mm(xnt, dproj_a, tm=D_MODEL, tn=1024, tk=min(2 * tk_dw, s), name="dw_in_gate", col0=6, n=4096)
    dw_zb, dw_xs, dw_bm, dw_cm, dw_dt = _dw_groups(xnt, dproj_b, tk=tk_dw)

    dw_dt32 = jnp.concatenate([dw_dt[:, 128 * g:128 * g + 8] for g in range(SSD_GROUPS)], axis=1)
    dw_ref = jnp.concatenate([dw_uvz, dw_zb, dw_xs, dw_bm, dw_cm, dw_dt32, dw_gate,
                              jnp.zeros((D_MODEL, 3 * WIN_STEP + WIN_W - IN_W), F32)], axis=1)
    dw_out_p, sib_i = _dw_out_rs_call(merged_t, dh_b, dw_ref, tn=1024, tk=tk_dw)
    p_out = dw_out_p.reshape(4, D_MODEL // 4, D_MODEL)

    s512 = jnp.sum(a512, axis=2)
    heads = lambda v: jnp.sum(v.reshape(32, HEADDIM), axis=1).reshape(1, 32)
    d_ssd_nw = s512[:, 0].reshape(1, D_MODEL)
    d_dskip = heads(s512[:, 1].reshape(D_MODEL))
    d_alog = heads(s512[:, 2].reshape(D_MODEL)) * (1.0 / HEADDIM) * (-jnp.exp(A_log))
    d_dtb = heads(s512[:, 3].reshape(D_MODEL))
    s768 = jnp.sum(a768, axis=2)
    ungroup = lambda v: jnp.concatenate([v[g, :, 0:512] for g in range(4)] + [v[g, :, 512:640] for g in range(4)]
                                        + [v[g, :, 640:768] for g in range(4)], axis=1)
    d_cw = ungroup(s768[:, 0:4])
    d_cb = ungroup(s768[:, 4:5])
    d_sgu_b = jnp.sum(dbfull.reshape(128, SGU_GROUPS, 128), axis=2).T.reshape(1, SGU_GROUPS, 128)
    d_sgu_w = jnp.where(smask[None], dws, 0.0).reshape(1, SGU_GROUPS, 128, 128)
    fold = lambda a8: jnp.sum(a8, axis=0, keepdims=True)
    small_local = [fold(dgb8), fold(dgam8), fold(dbeta8), d_sgu_w, d_sgu_b, d_cw, d_cb,
                   d_dtb, d_alog, d_dskip, d_ssd_nw, fold(dfw8).reshape(D_MODEL), jnp.sum(loss_t[:, 0, 0]).reshape(1)]
    small_shapes = [a.shape for a in small_local]
    v_local = _pack(small_local)

    core = lax.axis_index("c")
    place = jnp.stack([chip, core]).astype(jnp.int32)
    hr_i, hr_o = D_MODEL // 2, D_MODEL // 8
    sib_o, sib_v = _rs_sibling_call(p_out, v_local)
    s1b_i, o_i = _rs_add_windows(dw_ref, sib_i, place, 256, "rs_add_in")
    s1b_o, o_o = _rs_add(p_out, sib_o, place, 256, "rs_add_out")
    chip_v = _add_pair(v_local, sib_v, v_local.shape[0], "ar_add_small")
    dxn, r_i, r_o, abs_v = _dx_rs_call(dproj_a, wt_a, dproj_b, wt_b, s1b_i, s1b_o, chip_v, tm=tm_mm)
    grad_x, dnw8 = _gradx_call(x2, dxn, dh, norm_w, t_row)
    abs_v = lax.dynamic_update_index_in_dim(abs_v, chip_v, chip, 0)
    f_i = _sum_own_recv(o_i, r_i, 256, "rs_sum_in")
    f_o = _sum_own_recv(o_o, r_o, 256, "rs_sum_out")
    sib_f_i, g_w_out, all_nw = _rs_join_call(f_i, f_o, dnw8)
    g_w_out = lax.dynamic_update_slice_in_dim(g_w_out, f_o, core * hr_o, axis=0)
    all_nw = lax.dynamic_update_index_in_dim(all_nw, dnw8, 2 * chip + core, 0)
    g_nw = fold(_sum_slots(all_nw, "ar_sum_norm_w"))
    total_v = _sum_slots(abs_v, "ar_sum_small")
    (g_gb, g_gam, g_beta, g_sw, g_sb, g_cw_full, g_cb, g_dtb, g_alog, g_dsk, g_snw, g_fnw, loss1) = _unpack(total_v, small_shapes)
    g_cw_shard = lax.dynamic_slice(g_cw_full, (0, chip * 768), (4, 768)).reshape(1, 4, 768)
    loss = loss1.reshape(())

    shard_t = lambda win: lax.dynamic_slice_in_dim(win, 8 * chip, SHARD_W, axis=1).T
    g_w_in, d_win, nm_win, nv_win = (a.T for a in _adamw_halves(w_in[0].T, shard_t(f_i), shard_t(sib_f_i), m_w_in[0].T,
                                                                v_w_in[0].T, place, 296, "adamw_w_in"))
    d_wout, nm_wout, nv_wout = _adamw(w_out[0], g_w_out, m_w_out[0], v_w_out[0], 128, "adamw_w_out")
    small_w = [norm_w, gate_b, sgu_norm_g, sgu_norm_b, sgu_w, sgu_b, conv_w, conv_b, dt_bias, A_log, D_skip, ssd_norm_w, final_norm_w]
    small_m = [m_norm_w, m_gate_b, m_sgu_norm_g, m_sgu_norm_b, m_sgu_w, m_sgu_b, m_conv_w, m_conv_b, m_dt_bias, m_A_log, m_D_skip, m_ssd_norm_w, m_final_norm_w]
    small_v = [v_norm_w, v_gate_b, v_sgu_norm_g, v_sgu_norm_b, v_sgu_w, v_sgu_b, v_conv_w, v_conv_b, v_dt_bias, v_A_log, v_D_skip, v_ssd_norm_w, v_final_norm_w]
    small_g = [g_nw, g_gb, g_gam, g_beta, g_sw, g_sb, g_cw_shard, g_cb, g_dtb, g_alog, g_dsk, g_snw, g_fnw]
    shapes_w = [a.shape for a in small_w]
    small_g = [a.reshape(shp) for a, shp in zip(small_g, shapes_w)]
    pw = _pack(small_w)
    pd, pm, pv = _adamw(pw, _pack(small_g), _pack(small_m), _pack(small_v), pw.shape[0], "adamw_small")
    d_small, nm_small, nv_small = _unpack(pd, shapes_w), _unpack(pm, shapes_w), _unpack(pv, shapes_w)

    def with_big(small, win, wout):
        o = list(small)
        return o[0:1] + [win.reshape(1, D_MODEL, SHARD_W)] + o[1:12] + [wout.reshape(1, D_MODEL // 4, D_MODEL)] + o[12:13]

    grads = with_big(small_g, g_w_in, g_w_out)
    deltas = with_big(d_small, d_win, d_wout)
    new_m = with_big(nm_small, nm_win, nm_wout)
    new_v = with_big(nv_small, nv_win, nv_wout)
    return (loss, grad_x.reshape(1, s, D_MODEL), *grads, *deltas, *new_m, *new_v)
```

```python
import functools

import jax
import jax.numpy as jnp
from jax import lax
from jax.experimental import pallas as pl
from jax.experimental.pallas import tpu as pltpu

F32 = jnp.float32
MXU_DTYPE = jnp.bfloat16

D_MODEL = 2048
EPS = 1e-5
CHUNK = 64
SGU_BLOCK = 128
SGU_GROUPS = 16
SSD_GROUPS = 4
SSD_GW = 512
SSD_STATE = 128
HEADDIM = 64
IN_W = 15392
SHARD_W = IN_W // 4
BW_B = 1408
FW_B = BW_B
XBC_O, DT_O = 512, 1280
NA = 10240

ADAM_LR = 0.001
ADAM_B1 = 0.9
ADAM_B2 = 0.999
ADAM_EPS = 1e-08
ADAM_WD = 0.01
ADAM_STEP = 10

T_SSD = 256
NG_SSD = 2
T_TOK = 128
T_OUT = 256
T_ROW = 512
TM_MM = 1024
TK_DW = 1024
VMEM_CAP = 60 * 1024 * 1024
MESH = pl.DeviceIdType.MESH


def _cparams(sem, est_bytes):
    lim = int(min(VMEM_CAP, max(32 * 1024 * 1024, est_bytes + 12 * 1024 * 1024)))
    return pltpu.CompilerParams(dimension_semantics=sem, vmem_limit_bytes=lim)


def _c(x):
    return x.astype(MXU_DTYPE)


def _dot(a, b):
    return jnp.dot(a, b, preferred_element_type=F32)


def _dot_nt(a, b):
    return lax.dot_general(a, b, (((1,), (1,)), ((), ())), preferred_element_type=F32)


def _dot_tn(a, b):
    return lax.dot_general(a, b, (((0,), (0,)), ((), ())), preferred_element_type=F32)


def _split(x, n):
    parts, r = [], x
    for _ in range(n):
        p = _c(r)
        parts.append(p)
        r = r - p.astype(F32)
    return parts


def _dot01_l(m01, x, n):
    acc = None
    for p in _split(x, n):
        t = _dot(m01, p)
        acc = t if acc is None else acc + t
    return acc


def _dot01_r(x, m01, n):
    acc = None
    for p in _split(x, n):
        t = _dot(p, m01)
        acc = t if acc is None else acc + t
    return acc


def _sigmoid(x):
    return 1.0 / (1.0 + jnp.exp(-x))


def _fold8(x):
    r, w = x.shape
    return jnp.sum(x.reshape(r // 8, 8, w), axis=0)


def _iota(shape, dim):
    return lax.broadcasted_iota(jnp.int32, shape, dim)


def _ssd_masks():
    l64 = _iota((CHUNK, SSD_GW), 0)
    s64 = jnp.bitwise_and(_iota((CHUNK, SSD_GW), 1), CHUNK - 1)
    diag = l64 == s64
    causal = l64 >= s64
    row_last = l64 == CHUNK - 1
    r4 = lax.shift_right_logical(_iota((256, 256), 0), 6)
    c4 = lax.shift_right_logical(_iota((256, 256), 1), 6)
    mask4 = r4 == c4
    return diag, causal, row_last, mask4


def _cumsum_mats(t):
    r, c = _iota((t, t), 0), _iota((t, t), 1)
    same = lax.shift_right_logical(r, 6) == lax.shift_right_logical(c, 6)
    tri = _c(jnp.where(same, jnp.where(c <= r, 1.0, 0.0), 0.0))
    trit = _c(jnp.where(same, jnp.where(c >= r, 1.0, 0.0), 0.0))
    return tri, trit


def _head_expand_mat():
    return _c(jnp.where(_iota((128, SSD_GW), 0) == lax.shift_right_logical(_iota((128, SSD_GW), 1), 6), 1.0, 0.0))


def _ssd_common(xs, bm, cm, dt, acs, masks):
    diag, causal, row_last, mask4 = masks
    row_e = jnp.sum(jnp.where(diag, acs, 0.0), axis=0, keepdims=True)
    seg = acs - row_e
    lm = jnp.exp(jnp.where(causal, seg, -1e30))
    bb, cb = _c(bm), _c(cm)
    brep = jnp.concatenate([bb] * 8, axis=0)
    cbrep = _dot_nt(cb, brep)
    m = cbrep * lm
    xdt = xs * dt
    acs_last = jnp.sum(jnp.where(row_last, acs, 0.0), axis=0, keepdims=True)
    dec = jnp.exp(acs_last - acs)
    eacs = jnp.exp(acs)
    cd = jnp.exp(acs_last)
    return dict(lm=lm, bb=bb, cb=cb, brep=brep, m=m, xdt=xdt, dec=dec, eacs=eacs, cd=cd)


def _blockdiag4(xb, mask4):
    return jnp.where(mask4, jnp.concatenate([xb] * 4, axis=0), jnp.zeros((), xb.dtype))


def _ssd_chunk_fwd(xs, bm, cm, dt, acs, d_skip, ht, masks):
    q = _ssd_common(xs, bm, cm, dt, acs, masks)
    mask4 = masks[3]
    mb, xdtb = _c(q["m"]), _c(q["xdt"])
    yd = []
    for blk in range(2):
        sl = slice(256 * blk, 256 * blk + 256)
        yd.append(_dot(mb[:, sl], _blockdiag4(xdtb[:, sl], mask4)))
    y_diag = jnp.concatenate(yd, axis=1)
    p = _dot(q["cb"], _c(ht))
    y = y_diag + p * q["eacs"] + xs * d_skip
    st = _dot_tn(q["bb"], _c(q["xdt"] * q["dec"]))
    return y, ht * q["cd"] + st


def _ssd_chunk_bwd(xs, bm, cm, dt, acs, d_skip, hprev, dht, dy, masks):
    diag, causal, row_last, mask4 = masks
    q = _ssd_common(xs, bm, cm, dt, acs, masks)
    lm, bb, cb, brep, m, xdt, dec, eacs, cd = (q[k] for k in ("lm", "bb", "cb", "brep", "m", "xdt", "dec", "eacs", "cd"))
    hb = _c(hprev)
    yoff = _dot(cb, hb) * eacs
    dyb = _c(dy)
    dpb = _c(dy * eacs)
    d_c = _dot_nt(dpb, hb)
    dh_y = _dot_tn(cb, dpb)
    mb, xdtb = _c(m), _c(xdt)
    dm_parts, dxdt_parts = [], []
    for blk in range(2):
        sl = slice(256 * blk, 256 * blk + 256)
        bd = _blockdiag4(xdtb[:, sl], mask4)
        dm_parts.append(_dot_nt(dyb[:, sl], bd))
        dxf = jnp.where(mask4, _dot_tn(mb[:, sl], dyb[:, sl]), 0.0)
        dxdt_parts.append(dxf[0:64] + dxf[64:128] + dxf[128:192] + dxf[192:256])
    dm = jnp.concatenate(dm_parts, axis=1)
    dxdt = jnp.concatenate(dxdt_parts, axis=1)
    dcbb = _c(dm * lm)
    g = dm * m
    d_c = d_c + _dot(dcbb, brep)
    dbrep = _dot_tn(dcbb, cb)
    d_b = dbrep[0:64]
    for r in range(1, 8):
        d_b = d_b + dbrep[64 * r:64 * r + 64]
    dhtb = _c(dht)
    dxd = _dot(bb, dhtb)
    xd = xdt * dec
    dxdt = dxdt + dxd * dec
    tq = dxd * xd
    d_b = d_b + _dot_nt(_c(xd), dhtb)
    dcd = jnp.sum(dht * hprev, axis=0, keepdims=True)
    col_g = jnp.sum(g, axis=0, keepdims=True)
    last = jnp.sum(tq, axis=0, keepdims=True) + dcd * cd
    qq = g - jnp.where(diag, col_g, 0.0) + dy * yoff - tq + jnp.where(row_last, last, 0.0)
    dxs = dxdt * dt + dy * d_skip
    return dxs, d_b, d_c, dht * cd + dh_y, dy * xs, qq, dxdt * xs


def _ssd_finish_dt(qq, p1, dt, a_neg, trit, mask4):
    bd4 = _c(jnp.where(mask4, 1.0, 0.0))
    dacs = jnp.concatenate([_dot01_r(qq[:, 256 * b:256 * b + 256], bd4, 2) for b in range(2)], axis=1)
    da = _dot01_l(trit, dacs, 2)
    return p1 + da * (a_neg * (1.0 / HEADDIM)), da * dt


def _softplus(x):
    return jnp.maximum(x, 0.0) + jnp.log(1.0 + jnp.exp(-jnp.abs(x)))


def _conv_taps(xpad, t):
    taps = []
    for k in range(4):
        sh = 3 - k
        v = xpad if sh == 0 else pltpu.roll(xpad, sh, 0)
        taps.append(v[8:8 + t])
    return taps


def _mm(a, b, *, tm, tn, tk, name, out_dtype=F32, col0=0, n=None, b_is_t=False):
    m, k = a.shape
    n = b.shape[0 if b_is_t else 1] if n is None else n
    nk = k // tk
    assert m % tm == 0 and n % tn == 0 and k % tk == 0, (a.shape, b.shape, tm, tn, tk)
    assert nk == 1 or out_dtype == F32
    dot = _dot_nt if b_is_t else _dot

    def body(a_ref, b_ref, o_ref):
        if nk == 1:
            o_ref[...] = dot(a_ref[...], b_ref[...]).astype(out_dtype)
        else:
            @pl.when(pl.program_id(2) == 0)
            def _():
                o_ref[...] = jnp.zeros_like(o_ref)

            o_ref[...] += dot(a_ref[...], b_ref[...])

    isz = jnp.dtype(a.dtype).itemsize
    est = 2 * (tm * tk + tk * tn) * isz + 2 * tm * tn * 4
    return pl.pallas_call(
        body, name=name,
        grid=(m // tm, n // tn, nk),
        in_specs=[pl.BlockSpec((tm, tk), lambda i, j, kk: (i, kk)),
                  pl.BlockSpec((tn, tk), lambda i, j, kk: (j + col0, kk)) if b_is_t
                  else pl.BlockSpec((tk, tn), lambda i, j, kk: (kk, j + col0))],
        out_specs=pl.BlockSpec((tm, tn), lambda i, j, kk: (i, j)),
        out_shape=jax.ShapeDtypeStruct((m, n), out_dtype),
        compiler_params=_cparams(("parallel", "parallel", "arbitrary"), est),
    )(a, b)


def _dw_groups(xnt, dpb, *, tk):
    m, k = xnt.shape
    nk = k // tk

    def body(a_ref, b_ref, zb_ref, xs_ref, bm_ref, cm_ref, dt_ref):
        outs = ((zb_ref, 0, 512), (xs_ref, 512, 1024), (bm_ref, 1024, 1152), (cm_ref, 1152, 1280), (dt_ref, 1280, 1408))

        @pl.when(pl.program_id(1) == 0)
        def _():
            for o_ref, _, _ in outs:
                o_ref[...] = jnp.zeros_like(o_ref)

        d = _dot(a_ref[...], b_ref[...])
        for o_ref, lo, hi in outs:
            o_ref[...] += d[:, lo:hi]

    isz = jnp.dtype(xnt.dtype).itemsize
    est = 2 * (m * tk + tk * BW_B) * isz + 3 * m * BW_B * 4
    piece = lambda w: pl.BlockSpec((m, w), lambda g, kk: (0, g))
    return pl.pallas_call(
        body, name="dw_in_b",
        grid=(SSD_GROUPS, nk),
        in_specs=[pl.BlockSpec((m, tk), lambda g, kk: (0, kk)), pl.BlockSpec((tk, BW_B), lambda g, kk: (kk, g))],
        out_specs=[piece(512), piece(512), piece(128), piece(128), piece(128)],
        out_shape=[jax.ShapeDtypeStruct((m, w), F32) for w in (2048, 2048, 512, 512, 512)],
        compiler_params=_cparams(("parallel", "arbitrary"), est),
    )(xnt, dpb)


def _dx_rs_call(dpa, wta, dpb, wtb, sb_in, sb_out, chip_v, *, tm):
    s = dpa.shape[0]
    tka, tkb = 1024, BW_B
    nka, nkb = dpa.shape[1] // tka, dpb.shape[1] // tkb
    ni, nk = s // tm, nka + nkb

    def body(a_ref, wa_ref, b_ref, wb_ref, sbin, sbout, cv, o_ref, rc_in, rc_out, abs_v, send, recv):
        i, kk = pl.program_id(0), pl.program_id(1)

        def copies():
            x, y, c, me, others = _place()
            sends, recvs = [], []
            for j, chip in enumerate(others):
                kj = 2 * chip[0] + chip[1]
                to = (*chip, c)
                sends += [_remote(sbin.at[kj], rc_in.at[j], send, recv, j, to),
                          _remote(sbout.at[kj], rc_out.at[j], send, recv, 3 + j, to),
                          _remote(cv, abs_v.at[me], send, recv, 6 + j, to)]
                recvs += [sends[-3], sends[-2], _remote(cv, abs_v.at[kj], send, recv, 6 + j, to)]
            return sends, recvs

        @pl.when((i == 0) & (kk == 0))
        def _():
            for cp in copies()[0]:
                cp.start()

        @pl.when(kk == 0)
        def _():
            o_ref[...] = jnp.zeros_like(o_ref)

        @pl.when(kk < nka)
        def _():
            o_ref[...] += _dot(a_ref[...], wa_ref[...])

        @pl.when(kk >= nka)
        def _():
            o_ref[...] += _dot(b_ref[...], wb_ref[...])

        @pl.when((i == ni - 1) & (kk == nk - 1))
        def _():
            sends, recvs = copies()
            for cp in recvs:
                cp.wait_recv()
            for cp in sends:
                cp.wait_send()

    isz = jnp.dtype(dpa.dtype).itemsize
    est = 2 * isz * (tm * tka + tka * D_MODEL + tm * tkb + tkb * D_MODEL) + 2 * tm * D_MODEL * 4
    outs = [jax.ShapeDtypeStruct((s, D_MODEL), F32),
            jax.ShapeDtypeStruct((3,) + sb_in.shape[1:], sb_in.dtype), jax.ShapeDtypeStruct((3,) + sb_out.shape[1:], sb_out.dtype),
            jax.ShapeDtypeStruct((4,) + chip_v.shape, F32)]
    return pl.pallas_call(
        body, name="dx_matmul_rs_chips",
        grid=(ni, nk),
        in_specs=[
            pl.BlockSpec((tm, tka), lambda i, kk: (i, jnp.minimum(kk, nka - 1))),
            pl.BlockSpec((tka, D_MODEL), lambda i, kk: (jnp.minimum(kk, nka - 1), 0)),
            pl.BlockSpec((tm, tkb), lambda i, kk: (i, jnp.maximum(kk - nka, 0))),
            pl.BlockSpec((tkb, D_MODEL), lambda i, kk: (jnp.maximum(kk - nka, 0), 0)),
            ANY, ANY, ANY,
        ],
        out_specs=[pl.BlockSpec((tm, D_MODEL), lambda i, kk: (i, 0)), ANY, ANY, ANY],
        out_shape=outs,
        scratch_shapes=[pltpu.SemaphoreType.DMA((9,)), pltpu.SemaphoreType.DMA((9,))],
        compiler_params=_cparams(("arbitrary", "arbitrary"), est),
    )(dpa, wta, dpb, wtb, sb_in, sb_out, chip_v)


def _gradx_call(x, dxn, dh, norm_w, tm):
    s = x.shape[0]

    def body(x_ref, g_ref, dh_ref, w_ref, gx_ref, dw_ref):
        @pl.when(pl.program_id(0) == 0)
        def _():
            dw_ref[...] = jnp.zeros_like(dw_ref)

        xv, gv = x_ref[...], g_ref[...]
        r = lax.rsqrt(jnp.mean(xv * xv, axis=-1, keepdims=True) + EPS)
        gw = gv * w_ref[...]
        gx_ref[...] = r * gw - xv * (r * r * r) * jnp.mean(xv * gw, axis=-1, keepdims=True) + dh_ref[...]
        dw_ref[...] += _fold8(gv * (xv * r))

    row = pl.BlockSpec((tm, D_MODEL), lambda i: (i, 0))
    return pl.pallas_call(
        body, name="grad_x",
        grid=(s // tm,),
        in_specs=[row, row, row, pl.BlockSpec((1, D_MODEL), lambda i: (0, 0))],
        out_specs=[row, pl.BlockSpec((8, D_MODEL), lambda i: (0, 0))],
        out_shape=[jax.ShapeDtypeStruct((s, D_MODEL), F32), jax.ShapeDtypeStruct((8, D_MODEL), F32)],
        compiler_params=_cparams(("arbitrary",), 2 * tm * D_MODEL * 16),
    )(x, dxn, dh, norm_w)


def _layernorm_stats(v):
    mu = jnp.mean(v, axis=-1, keepdims=True)
    vc = v - mu
    var = jnp.mean(vc * vc, axis=-1, keepdims=True)
    return vc * lax.rsqrt(var + EPS), lax.rsqrt(var + EPS)


def _tok_fwd_call(proj_a, y_b, gate_b, sgu_g, sgu_beta, wm, bias_full, t):
    s = proj_a.shape[0]

    def body(pa_ref, yb_ref, gb_ref, g_ref, be_ref, wm_ref, bf_ref, ya_ref, mg_ref, mgt_ref, mix_ref):
        u = pa_ref[:, 0:2048].astype(F32)
        v = pa_ref[:, 2048:4096].astype(F32)
        za = pa_ref[:, 4096:6144].astype(F32)
        xhat, _ = _layernorm_stats(v)
        vnb = _c(xhat * g_ref[...] + be_ref[...])
        for gi in range(SGU_GROUPS):
            sl = slice(128 * gi, 128 * gi + 128)
            mix_ref[:, sl] = _dot(wm_ref[gi], vnb[:, sl])
        mixed = mix_ref[...] + bf_ref[...]
        y_a = u * mixed * (za * _sigmoid(za))
        g0 = _sigmoid(pa_ref[:, 6144:8192].astype(F32) + gb_ref[:, 0:2048])
        g1 = _sigmoid(pa_ref[:, 8192:10240].astype(F32) + gb_ref[:, 2048:4096])
        merged = g0 * y_a + g1 * yb_ref[...].astype(F32)
        ya_ref[...] = _c(y_a)
        mg_ref[...] = _c(merged)
        mgt_ref[...] = _c(merged.T)

    row = pl.BlockSpec((t, D_MODEL), lambda i: (i, 0))
    vec = lambda w: pl.BlockSpec((1, w), lambda i: (0, 0))
    return pl.pallas_call(
        body, name="tok_fwd",
        grid=(s // t,),
        in_specs=[pl.BlockSpec((t, NA), lambda i: (i, 0)), row, vec(4096), vec(2048), vec(2048),
                  pl.BlockSpec((SGU_GROUPS, 128, 128), lambda i: (0, 0, 0)), pl.BlockSpec((128, D_MODEL), lambda i: (0, 0))],
        out_specs=[row, row, pl.BlockSpec((D_MODEL, t), lambda i: (0, i))],
        out_shape=[jax.ShapeDtypeStruct((s, D_MODEL), MXU_DTYPE), jax.ShapeDtypeStruct((s, D_MODEL), MXU_DTYPE),
                   jax.ShapeDtypeStruct((D_MODEL, s), MXU_DTYPE)],
        scratch_shapes=[pltpu.VMEM((t, D_MODEL), F32)],
        compiler_params=_cparams(("parallel",), 2 * t * NA * 4 + 12 * t * D_MODEL * 4),
    )(proj_a, y_b, gate_b, sgu_g, sgu_beta, wm, bias_full)


def _tok_bwd_call(proj_a, dmerged, y_a, y_b, gate_b, sgu_g, sgu_beta, wm, wmt, bias_full, t):
    s = proj_a.shape[0]

    def body(pa_ref, dm_ref, ya_ref, yb_ref, gb_ref, g_ref, be_ref, wm_ref, wmt_ref, bf_ref,
             dpa_ref, dyb_ref, dgb_ref, dgam_ref, dbeta_ref, dbf_ref, dws_ref, mix_ref, dvn_ref):
        @pl.when(pl.program_id(0) == 0)
        def _():
            dgb_ref[...] = jnp.zeros_like(dgb_ref)
            dgam_ref[...] = jnp.zeros_like(dgam_ref)
            dbeta_ref[...] = jnp.zeros_like(dbeta_ref)
            dbf_ref[...] = jnp.zeros_like(dbf_ref)
            dws_ref[...] = jnp.zeros_like(dws_ref)

        u = pa_ref[:, 0:2048].astype(F32)
        v = pa_ref[:, 2048:4096].astype(F32)
        za = pa_ref[:, 4096:6144].astype(F32)
        xhat, rstd = _layernorm_stats(v)
        vnb = _c(xhat * g_ref[...] + be_ref[...])
        for gi in range(SGU_GROUPS):
            sl = slice(128 * gi, 128 * gi + 128)
            mix_ref[:, sl] = _dot(wm_ref[gi], vnb[:, sl])
        mixed = mix_ref[...] + bf_ref[...]
        sig = _sigmoid(za)
        sz = za * sig
        dm = dm_ref[...].astype(F32)
        y_a = ya_ref[...].astype(F32)
        g0 = _sigmoid(pa_ref[:, 6144:8192].astype(F32) + gb_ref[:, 0:2048])
        g1 = _sigmoid(pa_ref[:, 8192:10240].astype(F32) + gb_ref[:, 2048:4096])
        dgl0 = dm * y_a * g0 * (1.0 - g0)
        dgl1 = dm * yb_ref[...].astype(F32) * g1 * (1.0 - g1)
        dyb_ref[...] = _c(dm * g1)
        dya = dm * g0
        dpa_ref[:, 6144:8192] = _c(dgl0)
        dpa_ref[:, 8192:10240] = _c(dgl1)
        dgb_ref[:, 0:2048] += _fold8(dgl0)
        dgb_ref[:, 2048:4096] += _fold8(dgl1)
        dpa_ref[:, 0:2048] = _c(dya * mixed * sz)
        dpa_ref[:, 4096:6144] = _c(dya * (u * mixed) * (sig * (1.0 + za * (1.0 - sig))))
        dmixed = dya * u * sz
        dbf_ref[...] += dmixed
        dmb = _c(dmixed)
        for gi in range(SGU_GROUPS):
            sl = slice(128 * gi, 128 * gi + 128)
            dvn_ref[:, sl] = _dot(wmt_ref[gi], dmb[:, sl])
            dws_ref[gi] += _dot_nt(dmb[:, sl], vnb[:, sl])
        dvn = dvn_ref[...]
        dgam_ref[...] += _fold8(dvn * xhat)
        dbeta_ref[...] += _fold8(dvn)
        dxh = dvn * g_ref[...]
        dv = rstd * (dxh - jnp.mean(dxh, axis=-1, keepdims=True) - xhat * jnp.mean(dxh * xhat, axis=-1, keepdims=True))
        dpa_ref[:, 2048:4096] = _c(dv)

    row = pl.BlockSpec((t, D_MODEL), lambda i: (i, 0))
    vec = lambda w: pl.BlockSpec((1, w), lambda i: (0, 0))
    acc = lambda w: pl.BlockSpec((8, w), lambda i: (0, 0))
    wspec = pl.BlockSpec((SGU_GROUPS, 128, 128), lambda i: (0, 0, 0))
    return pl.pallas_call(
        body, name="tok_bwd",
        grid=(s // t,),
        in_specs=[pl.BlockSpec((t, NA), lambda i: (i, 0)), row, row, row, vec(4096), vec(2048), vec(2048),
                  wspec, wspec, pl.BlockSpec((128, D_MODEL), lambda i: (0, 0))],
        out_specs=[pl.BlockSpec((t, NA), lambda i: (i, 0)), row, acc(4096), acc(2048), acc(2048),
                   pl.BlockSpec((128, D_MODEL), lambda i: (0, 0)), wspec],
        out_shape=[jax.ShapeDtypeStruct((s, NA), MXU_DTYPE), jax.ShapeDtypeStruct((s, D_MODEL), MXU_DTYPE),
                   jax.ShapeDtypeStruct((8, 4096), F32), jax.ShapeDtypeStruct((8, 2048), F32),
                   jax.ShapeDtypeStruct((8, 2048), F32), jax.ShapeDtypeStruct((128, D_MODEL), F32),
                   jax.ShapeDtypeStruct((SGU_GROUPS, 128, 128), F32)],
        scratch_shapes=[pltpu.VMEM((t, D_MODEL), F32), pltpu.VMEM((t, D_MODEL), F32)],
        compiler_params=_cparams(("arbitrary",), 2 * t * NA * 6 + 16 * t * D_MODEL * 4),
    )(proj_a, dmerged, y_a, y_b, gate_b, sgu_g, sgu_beta, wm, wmt, bias_full)


def _out_call(merged, x, target, w_out, fnw, t):
    s = x.shape[0]
    nt = s // t

    def body(mg_ref, x_ref, t_ref, w_ref, fw_ref, dh_ref, dhb_ref, dmg_ref, loss_ref, dfw_ref):
        @pl.when(pl.program_id(0) == 0)
        def _():
            dfw_ref[...] = jnp.zeros_like(dfw_ref)

        h = x_ref[...] + _dot(mg_ref[...], w_ref[...])
        r = lax.rsqrt(jnp.mean(h * h, axis=-1, keepdims=True) + EPS)
        hn = h * r
        err = hn * fw_ref[...] - t_ref[...]
        loss_ref[...] = jnp.full(loss_ref.shape, 0.5 * jnp.sum(jnp.mean(err * err, axis=-1, keepdims=True)), F32)
        dy = err * (1.0 / D_MODEL)
        dfw_ref[...] += _fold8(dy * hn)
        gw = dy * fw_ref[...]
        dh = r * gw - h * (r * r * r) * jnp.mean(h * gw, axis=-1, keepdims=True)
        dh_ref[...] = dh
        dhb = _c(dh)
        dhb_ref[...] = dhb
        dmg_ref[...] = _c(_dot_nt(dhb, w_ref[...]))

    row = pl.BlockSpec((t, D_MODEL), lambda i: (i, 0))
    return pl.pallas_call(
        body, name="out_proj_loss",
        grid=(nt,),
        in_specs=[row, row, row, pl.BlockSpec((D_MODEL, D_MODEL), lambda i: (0, 0)), pl.BlockSpec((1, D_MODEL), lambda i: (0, 0))],
        out_specs=[row, row, row, pl.BlockSpec((1, 8, 128), lambda i: (i, 0, 0)), pl.BlockSpec((8, D_MODEL), lambda i: (0, 0))],
        out_shape=[jax.ShapeDtypeStruct((s, D_MODEL), F32), jax.ShapeDtypeStruct((s, D_MODEL), MXU_DTYPE),
                   jax.ShapeDtypeStruct((s, D_MODEL), MXU_DTYPE), jax.ShapeDtypeStruct((nt, 8, 128), F32),
                   jax.ShapeDtypeStruct((8, D_MODEL), F32)],
        compiler_params=_cparams(("arbitrary",), 2 * D_MODEL * D_MODEL * 2 + 2 * t * D_MODEL * 24),
    )(merged, x, target, w_out, fnw)


def _ssd_fwd_call(proj_b, dtb, alog, dsk, cw, cb, nw, t, ng):
    s = proj_b.shape[0]
    nt, nch = s // t, t // CHUNK

    def body(pb_ref, halo_ref, dtb_ref, al_ref, ds_ref, cw_ref, cb_ref, nw_ref, y_ref, yb_ref, hp_ref, pre_ref,
             dt_ref, acs_ref, ht_ref, prev_ref):
        i = pl.program_id(1)

        @pl.when(i == 0)
        def _():
            ht_ref[...] = jnp.zeros_like(ht_ref)

        for gi in range(ng):
            prev_ref[:, 768 * gi:768 * gi + 768] = jnp.where(i == 0, 0.0, halo_ref[:, FW_B * gi + XBC_O:FW_B * gi + DT_O])
        masks = _ssd_masks()
        tri, _ = _cumsum_mats(CHUNK)
        a_neg = -jnp.exp(al_ref[...])
        expand = _head_expand_mat()

        def chunk(c, carry):
            rows = pl.ds(pl.multiple_of(c * CHUNK, CHUNK), CHUNK)
            for gi in range(ng):
                fo, co, go, no = FW_B * gi, 768 * gi, SSD_GW * gi, 128 * gi
                xbc = pb_ref[rows, fo + XBC_O:fo + DT_O]
                taps = _conv_taps(jnp.concatenate([prev_ref[:, co:co + 768], xbc], axis=0), CHUNK)
                prev_ref[:, co:co + 768] = xbc[CHUNK - 8:CHUNK]
                pre = cb_ref[:, co:co + 768]
                for k in range(4):
                    pre = pre + taps[k] * cw_ref[k:k + 1, co:co + 768]
                pre_ref[rows, co:co + 768] = pre
                act = pre * _sigmoid(pre)
                dt_n = _softplus(pb_ref[rows, fo + DT_O:fo + DT_O + 128] + dtb_ref[:, no:no + 128])
                acs_n = _dot01_l(tri, dt_n * a_neg[:, no:no + 128], 3)
                dt = _dot01_r(dt_n, expand, 3)
                acs = _dot01_r(acs_n, expand, 3)
                dt_ref[rows, go:go + 512] = dt
                acs_ref[rows, go:go + 512] = acs
                ht = ht_ref[gi]
                hp_ref[c, :, go:go + 512] = ht
                y, ht_new = _ssd_chunk_fwd(act[:, 0:512], act[:, 512:640], act[:, 640:768], dt, acs,
                                           ds_ref[:, go:go + 512], ht, masks)
                y_ref[rows, go:go + 512] = y
                ht_ref[gi] = ht_new
                zb = pb_ref[rows, fo:fo + 512]
                hh = y * (zb * _sigmoid(zb))
                rr = lax.rsqrt(jnp.mean(hh * hh, axis=-1, keepdims=True) + EPS)
                yb_ref[rows, go:go + 512] = _c(hh * rr * nw_ref[:, go:go + 512])
            return carry

        lax.fori_loop(0, nch, chunk, 0)

    gvec = lambda w: pl.BlockSpec((1, ng * w), lambda g, i: (0, g))
    return pl.pallas_call(
        body, name="ssd_fwd",
        grid=(SSD_GROUPS // ng, nt),
        in_specs=[pl.BlockSpec((t, ng * FW_B), lambda g, i: (i, g)),
                  pl.BlockSpec((8, ng * FW_B), lambda g, i: (jnp.maximum(i * (t // 8) - 1, 0), g)),
                  gvec(128), gvec(128), gvec(512),
                  pl.BlockSpec((4, ng * 768), lambda g, i: (0, g)), gvec(768), gvec(512)],
        out_specs=[pl.BlockSpec((t, ng * SSD_GW), lambda g, i: (i, g)), pl.BlockSpec((t, ng * SSD_GW), lambda g, i: (i, g)),
                   pl.BlockSpec((nch, SSD_STATE, ng * SSD_GW), lambda g, i: (i, 0, g)),
                   pl.BlockSpec((t, ng * 768), lambda g, i: (i, g)),
                   pl.BlockSpec((t, ng * SSD_GW), lambda g, i: (i, g)), pl.BlockSpec((t, ng * SSD_GW), lambda g, i: (i, g))],
        out_shape=[jax.ShapeDtypeStruct((s, D_MODEL), F32), jax.ShapeDtypeStruct((s, D_MODEL), MXU_DTYPE),
                   jax.ShapeDtypeStruct((s // CHUNK, SSD_STATE, D_MODEL), F32),
                   jax.ShapeDtypeStruct((s, SSD_GROUPS * 768), F32),
                   jax.ShapeDtypeStruct((s, D_MODEL), F32), jax.ShapeDtypeStruct((s, D_MODEL), F32)],
        scratch_shapes=[pltpu.VMEM((ng, SSD_STATE, SSD_GW), F32), pltpu.VMEM((8, ng * 768), F32)],
        compiler_params=_cparams(("parallel", "arbitrary"), ng * (2 * t * FW_B * 4 + 16 * t * SSD_GW * 4) + 16 * 1024 * 1024),
    )(proj_b, proj_b, dtb, alog, dsk, cw, cb, nw)


def _ssd_bwd_call(proj_b, pre_all, dt_all, acs_all, dyb, y, hprev, dtb, alog, dsk, cw, nw, t, ng):
    s = proj_b.shape[0]
    nt, nch = s // t, t // CHUNK

    def body(pb_ref, pre_ref, dt_ref, acs_ref, dyb_ref, y_ref, hp_ref, dtb_ref, al_ref, ds_ref, cw_ref, nw_ref,
             dpb_ref, a512_ref, a768_ref, dht_ref, nxt_ref, q_ref, p1_ref):
        i = pl.program_id(1)

        @pl.when(i == 0)
        def _():
            dht_ref[...] = jnp.zeros_like(dht_ref)
            nxt_ref[...] = jnp.zeros_like(nxt_ref)
            a512_ref[...] = jnp.zeros_like(a512_ref)
            a768_ref[...] = jnp.zeros_like(a768_ref)

        _, trit = _cumsum_mats(t)
        a_neg = -jnp.exp(al_ref[...])
        masks = _ssd_masks()

        def chunk(cc, carry):
            c = nch - 1 - cc
            rows = pl.ds(pl.multiple_of(c * CHUNK, CHUNK), CHUNK)
            for gi in range(ng):
                fo, co, go, bo = FW_B * gi, 768 * gi, SSD_GW * gi, BW_B * gi
                pre = pre_ref[rows, co:co + 768]
                sp = _sigmoid(pre)
                act = pre * sp
                zb = pb_ref[rows, fo:fo + 512]
                yv = y_ref[rows, go:go + 512]
                sgz = _sigmoid(zb)
                sz = zb * sgz
                hh = yv * sz
                rr = lax.rsqrt(jnp.mean(hh * hh, axis=-1, keepdims=True) + EPS)
                dyb = dyb_ref[rows, go:go + 512].astype(F32)
                a512_ref[gi, 0] += _fold8(dyb * (hh * rr))
                tt = dyb * nw_ref[:, go:go + 512]
                dhh = rr * tt - hh * (rr * rr * rr) * jnp.mean(hh * tt, axis=-1, keepdims=True)
                dpb_ref[rows, bo:bo + 512] = _c(dhh * yv * (sgz * (1.0 + zb * (1.0 - sgz))))
                dxs, d_b, d_c, dht_prev, dyxs, qq, p1 = _ssd_chunk_bwd(
                    act[:, 0:512], act[:, 512:640], act[:, 640:768], dt_ref[rows, go:go + 512], acs_ref[rows, go:go + 512],
                    ds_ref[:, go:go + 512], hp_ref[c, :, go:go + 512], dht_ref[gi], dhh * sz, masks)
                dht_ref[gi] = dht_prev
                q_ref[rows, go:go + 512] = qq
                p1_ref[rows, go:go + 512] = p1
                a512_ref[gi, 1] += _fold8(dyxs)
                dpre = jnp.concatenate([dxs, d_b, d_c], axis=1) * (sp * (1.0 + pre * (1.0 - sp)))
                xbc = pb_ref[rows, fo + XBC_O:fo + DT_O]
                a768_ref[gi, 4] += _fold8(dpre)
                a768_ref[gi, 3] += _fold8(dpre * xbc)
                dpad = jnp.concatenate([dpre, nxt_ref[:, co:co + 768]], axis=0)
                dx = dpre * cw_ref[3:4, co:co + 768]
                for k in range(3):
                    d_k = pltpu.roll(dpad, CHUNK + 8 - (3 - k), 0)[0:CHUNK]
                    dx = dx + d_k * cw_ref[k:k + 1, co:co + 768]
                    a768_ref[gi, k] += _fold8(d_k * xbc)
                nxt_ref[:, co:co + 768] = dpre[0:8]
                dpb_ref[rows, bo + 512:bo + 1280] = _c(dx)
            return carry

        lax.fori_loop(0, nch, chunk, 0)

        rsel = _c(jnp.where(lax.shift_right_logical(_iota((SSD_GW, 128), 0), 6) == _iota((SSD_GW, 128), 1), 1.0, 0.0))
        for gi in range(ng):
            fo, go, bo, no = FW_B * gi, SSD_GW * gi, BW_B * gi, 128 * gi
            ddt, dadt = _ssd_finish_dt(q_ref[:, go:go + 512], p1_ref[:, go:go + 512], dt_ref[:, go:go + 512],
                                       a_neg[:, go:go + 512], trit, masks[3])
            sig_n = _sigmoid(pb_ref[:, fo + DT_O:fo + DT_O + 128] + dtb_ref[:, no:no + 128])
            ddtr_n = _dot01_r(ddt, rsel, 2) * sig_n
            dpb_ref[:, bo + DT_O:bo + DT_O + 128] = _c(ddtr_n)
            a512_ref[gi, 2] += _fold8(dadt)
            a512_ref[gi, 3, :, 0:128] += _fold8(ddtr_n)

    gvec = lambda w: pl.BlockSpec((1, ng * w), lambda g, i: (0, g))
    rev = lambda w: pl.BlockSpec((t, ng * w), lambda g, i: (nt - 1 - i, g))
    return pl.pallas_call(
        body, name="ssd_bwd",
        grid=(SSD_GROUPS // ng, nt),
        in_specs=[rev(FW_B), rev(768), rev(SSD_GW), rev(SSD_GW), rev(SSD_GW), rev(SSD_GW),
                  pl.BlockSpec((nch, SSD_STATE, ng * SSD_GW), lambda g, i: (nt - 1 - i, 0, g)),
                  gvec(128), gvec(512), gvec(512),
                  pl.BlockSpec((4, ng * 768), lambda g, i: (0, g)), gvec(512)],
        out_specs=[rev(BW_B),
                   pl.BlockSpec((ng, 4, 8, 512), lambda g, i: (g, 0, 0, 0)),
                   pl.BlockSpec((ng, 5, 8, 768), lambda g, i: (g, 0, 0, 0))],
        out_shape=[jax.ShapeDtypeStruct((s, SSD_GROUPS * BW_B), MXU_DTYPE),
                   jax.ShapeDtypeStruct((SSD_GROUPS, 4, 8, 512), F32),
                   jax.ShapeDtypeStruct((SSD_GROUPS, 5, 8, 768), F32)],
        scratch_shapes=[pltpu.VMEM((ng, SSD_STATE, SSD_GW), F32), pltpu.VMEM((8, ng * 768), F32),
                        pltpu.VMEM((t, ng * SSD_GW), F32), pltpu.VMEM((t, ng * SSD_GW), F32)],
        compiler_params=_cparams(("parallel", "arbitrary"), ng * (2 * t * FW_B * 4 + 18 * t * SSD_GW * 4) + 16 * 1024 * 1024),
    )(proj_b, pre_all, dt_all, acs_all, dyb, y, hprev, dtb, alog, dsk, cw, nw)


def _rows_call(body, ins, outs, tr, name):
    r = ins[0].shape[0]
    spec = lambda a: pl.BlockSpec((tr, a.shape[1]), lambda i: (i, 0))
    est = 2 * tr * sum(a.shape[1] * jnp.dtype(a.dtype).itemsize for a in list(ins) + list(outs))
    return pl.pallas_call(
        body, name=name, grid=(r // tr,),
        in_specs=[spec(a) for a in ins], out_specs=[spec(o) for o in outs], out_shape=list(outs),
        compiler_params=_cparams(("parallel",), est),
    )(*ins)


def _add_pair(a, b, tr, name):
    def body(a_ref, b_ref, o_ref):
        o_ref[...] = a_ref[...] + b_ref[...]

    return _rows_call(body, [a, b], [jax.ShapeDtypeStruct(a.shape, F32)], tr, name)[0]


def _rs_add(p, sib, place, tr, name):
    _, r, c = p.shape
    half = r // 2
    nb = half // tr

    def body(pl_ref, p_ref, s_ref, b_ref, own_ref):
        v = p_ref[0] + s_ref[0]
        b_ref[0] = v.astype(jnp.bfloat16)

        @pl.when(pl.program_id(1) == pl_ref[0])
        def _():
            own_ref[...] = v

    return pl.pallas_call(
        body, name=name,
        grid_spec=pltpu.PrefetchScalarGridSpec(
            num_scalar_prefetch=1, grid=(nb, 4),
            in_specs=[pl.BlockSpec((1, tr, c), lambda i, k, pr: (k, pr[1] * nb + i, 0)),
                      pl.BlockSpec((1, tr, c), lambda i, k, pr: (k, i, 0))],
            out_specs=[pl.BlockSpec((1, tr, c), lambda i, k, pr: (k, i, 0)),
                       pl.BlockSpec((tr, c), lambda i, k, pr: (i, 0))]),
        out_shape=[jax.ShapeDtypeStruct((4, half, c), jnp.bfloat16), jax.ShapeDtypeStruct((half, c), F32)],
        compiler_params=_cparams(("parallel", "arbitrary"), 2 * tr * c * 14),
    )(place, p, sib)


WIN_STEP = 3840
WIN_W = 3968


def _rs_add_windows(dw, sib, place, tr, name):
    r = dw.shape[0]
    half = r // 2
    nb = half // tr
    tail = WIN_W - WIN_STEP

    def body(pl_ref, pm_ref, pt_ref, s_ref, b_ref, own_ref):
        vm = pm_ref[...] + s_ref[0, :, 0:WIN_STEP]
        vt = pt_ref[...] + s_ref[0, :, WIN_STEP:WIN_W]
        b_ref[0, :, 0:WIN_STEP] = vm.astype(jnp.bfloat16)
        b_ref[0, :, WIN_STEP:WIN_W] = vt.astype(jnp.bfloat16)

        @pl.when(pl.program_id(1) == pl_ref[0])
        def _():
            own_ref[:, 0:WIN_STEP] = vm
            own_ref[:, WIN_STEP:WIN_W] = vt

    return pl.pallas_call(
        body, name=name,
        grid_spec=pltpu.PrefetchScalarGridSpec(
            num_scalar_prefetch=1, grid=(nb, 4),
            in_specs=[pl.BlockSpec((tr, WIN_STEP), lambda i, k, pr: (pr[1] * nb + i, k)),
                      pl.BlockSpec((tr, tail), lambda i, k, pr: (pr[1] * nb + i, (WIN_STEP // tail) * (k + 1))),
                      pl.BlockSpec((1, tr, WIN_W), lambda i, k, pr: (k, i, 0))],
            out_specs=[pl.BlockSpec((1, tr, WIN_W), lambda i, k, pr: (k, i, 0)),
                       pl.BlockSpec((tr, WIN_W), lambda i, k, pr: (i, 0))]),
        out_shape=[jax.ShapeDtypeStruct((4, half, WIN_W), jnp.bfloat16), jax.ShapeDtypeStruct((half, WIN_W), F32)],
        compiler_params=_cparams(("parallel", "arbitrary"), 2 * tr * WIN_W * 14),
    )(place, dw, dw, sib)


def _sum_own_recv(own, recv, tr, name):
    r, c = own.shape

    def body(o_ref, r_ref, out_ref):
        v = o_ref[...]
        for j in range(3):
            v = v + r_ref[j].astype(F32)
        out_ref[...] = v

    return pl.pallas_call(
        body, name=name, grid=(r // tr,),
        in_specs=[pl.BlockSpec((tr, c), lambda i: (i, 0)), pl.BlockSpec((3, tr, c), lambda i: (0, i, 0))],
        out_specs=pl.BlockSpec((tr, c), lambda i: (i, 0)),
        out_shape=jax.ShapeDtypeStruct((r, c), F32),
        compiler_params=_cparams(("parallel",), 2 * tr * c * 14),
    )(own, recv)


def _sum_slots(stack, name):
    n, r, w = stack.shape

    def body(a_ref, out_ref):
        v = a_ref[0]
        for k in range(1, n):
            v = v + a_ref[k]
        out_ref[...] = v

    return pl.pallas_call(
        body, name=name, grid=(1,),
        in_specs=[pl.BlockSpec((n, r, w), lambda i: (0, 0, 0))],
        out_specs=pl.BlockSpec((r, w), lambda i: (0, 0)),
        out_shape=jax.ShapeDtypeStruct((r, w), F32),
        compiler_params=_cparams(("arbitrary",), 2 * (n + 1) * r * w * 4),
    )(stack)


def _adamw(w, g, m, v, tr, name):
    def body(w_ref, g_ref, m_ref, v_ref, d_ref, nm_ref, nv_ref):
        d_ref[...], nm_ref[...], nv_ref[...] = _adam_math(w_ref[...], g_ref[...], m_ref[...], v_ref[...])

    o = jax.ShapeDtypeStruct(w.shape, F32)
    return _rows_call(body, [w, g, m, v], [o, o, o], tr, name)


def _adam_math(w, g, m, v):
    nm = ADAM_B1 * m + (1.0 - ADAM_B1) * g
    nv = ADAM_B2 * v + (1.0 - ADAM_B2) * (g * g)
    m_hat = nm / (1.0 - ADAM_B1 ** ADAM_STEP)
    v_hat = nv / (1.0 - ADAM_B2 ** ADAM_STEP)
    return -ADAM_LR * (m_hat / (jnp.sqrt(v_hat) + ADAM_EPS) + ADAM_WD * w), nm, nv


def _adamw_halves(w, g_own, g_sib, m, v, place, tr, name):
    r, c = w.shape

    def body(pl_ref, w_ref, go_ref, gs_ref, m_ref, v_ref, g_ref, d_ref, nm_ref, nv_ref):
        first = pl_ref[1] == 0
        own, sib = go_ref[...], gs_ref[...]
        g = jnp.concatenate([jnp.where(first, own, sib), jnp.where(first, sib, own)], axis=1)
        g_ref[...] = g
        d_ref[...], nm_ref[...], nv_ref[...] = _adam_math(w_ref[...], g, m_ref[...], v_ref[...])

    full = pl.BlockSpec((tr, c), lambda i, pr: (i, 0))
    half = pl.BlockSpec((tr, c // 2), lambda i, pr: (i, 0))
    o = jax.ShapeDtypeStruct((r, c), F32)
    return pl.pallas_call(
        body, name=name,
        grid_spec=pltpu.PrefetchScalarGridSpec(num_scalar_prefetch=1, grid=(r // tr,),
                                               in_specs=[full, half, half, full, full], out_specs=[full] * 4),
        out_shape=[o] * 4,
        compiler_params=_cparams(("parallel",), 2 * tr * c * 4 * 8),
    )(place, w, g_own, g_sib, m, v)


ANY = pl.BlockSpec(memory_space=pl.ANY)


def _place():
    x, y, c = lax.axis_index("x"), lax.axis_index("y"), lax.axis_index("c")
    others = [(1 - x, y), (x, 1 - y), (1 - x, 1 - y)]
    return x, y, c, 2 * x + y, others


def _remote(src, dst, send, recv, k, to):
    return pltpu.make_async_remote_copy(src_ref=src, dst_ref=dst, send_sem=send.at[k], recv_sem=recv.at[k],
                                        device_id=to, device_id_type=MESH)


def _norm_gather_call(x, norm_w, win_b, wout_b, cw8, tm):
    s = x.shape[0]
    ni = s // tm

    def body(x_ref, w_ref, win, wout, cw, xn_ref, xnt_ref, g_in, g_out, g_cw, send, recv):
        i = pl.program_id(0)

        def direct():
            xx, yy, c, me, others = _place()
            cps = []
            for a, (src, dst) in enumerate(((win, g_in), (wout, g_out))):
                half = src.shape[0] // 2
                mine = pl.ds(c * half, half)
                cps += [_remote(src.at[mine], dst.at[me, mine], send, recv, 6 * a + j, (*chip, c)) for j, chip in enumerate(others)]
            cps += [_remote(cw, g_cw.at[me], send, recv, 12 + j, (*chip, c)) for j, chip in enumerate(others)]
            return cps

        @pl.when(i == 0)
        def _():
            for cp in direct():
                cp.start()

        xv = x_ref[...]
        r = lax.rsqrt(jnp.mean(xv * xv, axis=-1, keepdims=True) + EPS)
        xn = xv * r * w_ref[...]
        xn_ref[...] = _c(xn)
        xnt_ref[...] = _c(xn.T)

        @pl.when(i == ni - 1)
        def _():
            xx, yy, c, me, others = _place()
            sib = (xx, yy, 1 - c)
            passed = []
            for a, dst in enumerate((g_in, g_out)):
                half = dst.shape[1] // 2
                mine = pl.ds(c * half, half)
                for j, chip in enumerate(others):
                    kj = 2 * chip[0] + chip[1]
                    _remote(dst.at[kj, mine], dst.at[kj, mine], send, recv, 6 * a + j, (*chip, c)).wait_recv()
                    cp = _remote(dst.at[kj, mine], dst.at[kj, mine], send, recv, 6 * a + 3 + j, sib)
                    cp.start()
                    passed.append(cp)
            for a, dst in enumerate((g_in, g_out)):
                half = dst.shape[1] // 2
                theirs = pl.ds((1 - c) * half, half)
                for j, chip in enumerate(others):
                    kj = 2 * chip[0] + chip[1]
                    _remote(dst.at[kj, theirs], dst.at[kj, theirs], send, recv, 6 * a + 3 + j, sib).wait_recv()
            for j, chip in enumerate(others):
                kj = 2 * chip[0] + chip[1]
                _remote(cw, g_cw.at[kj], send, recv, 12 + j, (*chip, c)).wait_recv()
            for cp in direct() + passed:
                cp.wait_send()

    outs = [jax.ShapeDtypeStruct((s, D_MODEL), MXU_DTYPE), jax.ShapeDtypeStruct((D_MODEL, s), MXU_DTYPE)]
    outs += [jax.ShapeDtypeStruct((4,) + a.shape, a.dtype) for a in (win_b, wout_b, cw8)]
    return pl.pallas_call(
        body, name="rmsnorm_gather_weights",
        grid=(ni,),
        in_specs=[pl.BlockSpec((tm, D_MODEL), lambda i: (i, 0)), pl.BlockSpec((1, D_MODEL), lambda i: (0, 0)), ANY, ANY, ANY],
        out_specs=[pl.BlockSpec((tm, D_MODEL), lambda i: (i, 0)), pl.BlockSpec((D_MODEL, tm), lambda i: (0, i)), ANY, ANY, ANY],
        out_shape=outs,
        scratch_shapes=[pltpu.SemaphoreType.DMA((15,)), pltpu.SemaphoreType.DMA((15,))],
        compiler_params=_cparams(("arbitrary",), 2 * tm * D_MODEL * 12),
    )(x, norm_w, win_b, wout_b, cw8)


def _dw_out_rs_call(a, b, dw, *, tn, tk):
    m, k = a.shape
    n = b.shape[1]
    nj, nk = n // tn, k // tk
    half = dw.shape[0] // 2

    def body(a_ref, b_ref, pin, o_ref, sib_in, send, recv):
        j, kk = pl.program_id(0), pl.program_id(1)

        def copies():
            x, y, c, me, others = _place()
            rows = pl.ds((1 - c) * half, half)
            return [_remote(pin.at[rows, pl.ds(WIN_STEP * w, WIN_W)], sib_in.at[w], send, recv, w, (x, y, 1 - c)) for w in range(4)]

        @pl.when((j == 0) & (kk == 0))
        def _():
            for cp in copies():
                cp.start()

        @pl.when(kk == 0)
        def _():
            o_ref[...] = jnp.zeros_like(o_ref)

        o_ref[...] += _dot(a_ref[...], b_ref[...])

        @pl.when((j == nj - 1) & (kk == nk - 1))
        def _():
            cps = copies()
            for cp in cps:
                cp.wait_recv()
            for cp in cps:
                cp.wait_send()

    isz = jnp.dtype(a.dtype).itemsize
    est = 2 * (m * tk + tk * tn) * isz + 2 * m * tn * 4
    outs = [jax.ShapeDtypeStruct((m, n), F32), jax.ShapeDtypeStruct((4, half, WIN_W), dw.dtype)]
    return pl.pallas_call(
        body, name="dw_out_rs_sibling",
        grid=(nj, nk),
        in_specs=[pl.BlockSpec((m, tk), lambda j, kk: (0, kk)), pl.BlockSpec((tk, tn), lambda j, kk: (kk, j)), ANY],
        out_specs=[pl.BlockSpec((m, tn), lambda j, kk: (0, j)), ANY],
        out_shape=outs,
        scratch_shapes=[pltpu.SemaphoreType.DMA((4,)), pltpu.SemaphoreType.DMA((4,))],
        compiler_params=_cparams(("arbitrary", "arbitrary"), est),
    )(a, b, dw)


def _rs_sibling_call(p_out, vsmall):
    def body(pout, vs, sib_out, sib_v, send, recv):
        x, y, c, me, others = _place()
        sib = (x, y, 1 - c)
        half = pout.shape[1] // 2
        cps = [_remote(pout.at[:, pl.ds((1 - c) * half, half)], sib_out, send, recv, 0, sib),
               _remote(vs, sib_v, send, recv, 1, sib)]
        for cp in cps:
            cp.start()
        for cp in cps:
            cp.wait_recv()
        for cp in cps:
            cp.wait_send()

    outs = [jax.ShapeDtypeStruct((4, p_out.shape[1] // 2, p_out.shape[2]), p_out.dtype),
            jax.ShapeDtypeStruct(vsmall.shape, vsmall.dtype)]
    return pl.pallas_call(
        body, name="rs_sibling",
        in_specs=[ANY] * 2, out_specs=[ANY] * 2, out_shape=outs,
        scratch_shapes=[pltpu.SemaphoreType.DMA((2,)), pltpu.SemaphoreType.DMA((2,))],
    )(p_out, vsmall)


def _rs_join_call(f_in, f_out, nw8):
    def body(fin, fout, nw, sib_in, full_out, all_nw, send, recv):
        x, y, c, me, others = _place()
        sib = (x, y, 1 - c)
        half = fout.shape[0]
        cps = [_remote(fin, sib_in, send, recv, 0, sib),
               _remote(fout, full_out.at[pl.ds(c * half, half)], send, recv, 1, sib)]
        mine = 4 * x + 2 * y + c
        peers = []
        for r in range(1, 8):
            px, py, pc = (1 - x if r & 4 else x), (1 - y if r & 2 else y), (1 - c if r & 1 else c)
            peers.append((r, (px, py, pc), 4 * px + 2 * py + pc))
            cps.append(_remote(nw, all_nw.at[mine], send, recv, 1 + r, (px, py, pc)))
        for cp in cps:
            cp.start()
        cps[0].wait_recv()
        _remote(fout, full_out.at[pl.ds((1 - c) * half, half)], send, recv, 1, sib).wait_recv()
        for r, peer, idx in peers:
            _remote(nw, all_nw.at[idx], send, recv, 1 + r, peer).wait_recv()
        for cp in cps:
            cp.wait_send()

    outs = [jax.ShapeDtypeStruct(f_in.shape, F32), jax.ShapeDtypeStruct((2 * f_out.shape[0], f_out.shape[1]), F32),
            jax.ShapeDtypeStruct((8,) + nw8.shape, F32)]
    return pl.pallas_call(
        body, name="rs_join",
        in_specs=[ANY] * 3, out_specs=[ANY] * 3, out_shape=outs,
        scratch_shapes=[pltpu.SemaphoreType.DMA((9,)), pltpu.SemaphoreType.DMA((9,))],
    )(f_in, f_out, nw8)


def _pack(arrs):
    parts = []
    for a in arrs:
        f = a.reshape(-1).astype(F32)
        pad = (-f.shape[0]) % 1024
        parts.append(jnp.pad(f, (0, pad)).reshape(-1, 128))
    return jnp.concatenate(parts, axis=0)


def _unpack(packed, shapes):
    out, row = [], 0
    for shp in shapes:
        n = 1
        for d in shp:
            n *= d
        rows = (n + 1023) // 1024 * 8
        out.append(packed[row:row + rows].reshape(-1)[:n].reshape(shp))
        row += rows
    return out


def _expand_heads(v32):
    return jnp.repeat(v32.reshape(32), HEADDIM).reshape(1, D_MODEL)


def kernel(x, norm_w, w_in, gate_b, sgu_norm_g, sgu_norm_b, sgu_w, sgu_b, conv_w, conv_b, dt_bias, A_log, D_skip, ssd_norm_w, w_out, final_norm_w, loss_target, m_norm_w, m_w_in, m_gate_b, m_sgu_norm_g, m_sgu_norm_b, m_sgu_w, m_sgu_b, m_conv_w, m_conv_b, m_dt_bias, m_A_log, m_D_skip, m_ssd_norm_w, m_w_out, m_final_norm_w, v_norm_w, v_w_in, v_gate_b, v_sgu_norm_g, v_sgu_norm_b, v_sgu_w, v_sgu_b, v_conv_w, v_conv_b, v_dt_bias, v_A_log, v_D_skip, v_ssd_norm_w, v_w_out, v_final_norm_w):
    s = x.shape[1]
    x2 = x.reshape(s, D_MODEL)
    tgt = loss_target.reshape(s, D_MODEL)
    t_ssd, t_tok, t_out, t_row = min(T_SSD, s), min(T_TOK, s), min(T_OUT, s), min(T_ROW, s)
    tm_mm, tk_dw = min(TM_MM, s), min(TK_DW, s)
    chip = 2 * lax.axis_index("x") + lax.axis_index("y")

    cw8 = jnp.pad(conv_w[0], ((0, 4), (0, 0)))
    win_b, wout_b = _c(w_in[0]), _c(w_out[0])
    xn, xnt, g_in, g_out, g_cw = _norm_gather_call(x2, norm_w, win_b, wout_b, cw8, t_row)
    g_in = lax.dynamic_update_index_in_dim(g_in, win_b, chip, 0)
    g_out = lax.dynamic_update_index_in_dim(g_out, wout_b, chip, 0)
    g_cw = lax.dynamic_update_index_in_dim(g_cw, cw8, chip, 0)
    wt = jnp.transpose(g_in, (0, 2, 1)).reshape(IN_W, D_MODEL)
    w_out_full = g_out.reshape(D_MODEL, D_MODEL)
    conv_w_full = jnp.transpose(g_cw[:, 0:4, :], (1, 0, 2)).reshape(4, 3072)

    wt_a = jnp.concatenate([wt[0:6144], wt[11296:15392]], axis=0)
    bw = []
    for g in range(SSD_GROUPS):
        xs_g = wt[8192 + 512 * g:8192 + 512 * g + 512]
        b_g = wt[10240 + 128 * g:10240 + 128 * g + 128]
        c_g = wt[10752 + 128 * g:10752 + 128 * g + 128]
        zb_g = wt[6144 + 512 * g:6144 + 512 * g + 512]
        dt_g = wt[11264 + 8 * g:11264 + 8 * g + 8]
        bw += [zb_g, xs_g, b_g, c_g, jnp.pad(dt_g, ((0, 120), (0, 0)))]
    wt_b = jnp.concatenate(bw, axis=0)

    def group_cols(full_xs, full_bc):
        parts = []
        for g in range(SSD_GROUPS):
            parts += [full_xs[:, 512 * g:512 * g + 512], full_bc[:, 128 * g:128 * g + 128], full_bc[:, 512 + 128 * g:512 + 128 * g + 128]]
        return jnp.concatenate(parts, axis=1)

    cw_g = group_cols(conv_w_full[:, 0:2048], conv_w_full[:, 2048:3072])
    cb_g = group_cols(conv_b[:, 0:2048], conv_b[:, 2048:3072])
    alog_e, dsk_e = _expand_heads(A_log), _expand_heads(D_skip)
    narrow = lambda v32: jnp.pad(v32.reshape(SSD_GROUPS, 8), ((0, 0), (0, 120))).reshape(1, SSD_GROUPS * 128)
    dtb_n, alog_n = narrow(dt_bias), narrow(A_log)

    pos_chunk = jnp.arange(SGU_BLOCK) // CHUNK
    smask = pos_chunk[None, :] <= pos_chunk[:, None]
    wm_f = jnp.where(smask[None], sgu_w[0], 0.0)
    wm = _c(wm_f)
    wmt = _c(jnp.transpose(wm_f, (0, 2, 1)))
    bias_full = jnp.repeat(sgu_b[0].T, D_MODEL // SGU_GROUPS, axis=1)
    fnw = final_norm_w.reshape(1, D_MODEL)

    proj_a = _mm(xn, wt_a, tm=tm_mm, tn=1024, tk=D_MODEL, name="in_proj_a", out_dtype=MXU_DTYPE, b_is_t=True)
    proj_b = _mm(xn, wt_b, tm=tm_mm, tn=BW_B, tk=D_MODEL, name="in_proj_b", b_is_t=True)
    y_ssd, y_b, hprev, pre_all, dt_all, acs_all = _ssd_fwd_call(proj_b, dtb_n, alog_n, dsk_e, cw_g, cb_g, ssd_norm_w, t_ssd, NG_SSD)
    y_a, merged, merged_t = _tok_fwd_call(proj_a, y_b, gate_b, sgu_norm_g, sgu_norm_b, wm, bias_full, t_tok)
    dh, dh_b, dmerged, loss_t, dfw8 = _out_call(merged, x2, tgt, w_out_full, fnw, t_out)

    dproj_a, dy_b, dgb8, dgam8, dbeta8, dbfull, dws = _tok_bwd_call(
        proj_a, dmerged, y_a, y_b, gate_b, sgu_norm_g, sgu_norm_b, wm, wmt, bias_full, t_tok)
    dproj_b, a512, a768 = _ssd_bwd_call(proj_b, pre_all, dt_all, acs_all, dy_b, y_ssd, hprev, dtb_n, alog_e, dsk_e, cw_g,
                                        ssd_norm_w, t_ssd, NG_SSD)
    dw_uvz = _mm(xnt, dproj_a, tm=D_MODEL, tn=1024, tk=min(2 * tk_dw, s), name="dw_in_uvz", n=6144)
    dw_gate = _mm(xnt, dproj_a, tm=D_MODEL, tn=1024, tk=min(2 * tk_dw, s), name="dw_in_gate", col0=6, n=4096)
    dw_zb, dw_xs, dw_bm, dw_cm, dw_dt = _dw_groups(xnt, dproj_b, tk=tk_dw)

    dw_dt32 = jnp.concatenate([dw_dt[:, 128 * g:128 * g + 8] for g in range(SSD_GROUPS)], axis=1)
    dw_ref = jnp.concatenate([dw_uvz, dw_zb, dw_xs, dw_bm, dw_cm, dw_dt32, dw_gate,
                              jnp.zeros((D_MODEL, 3 * WIN_STEP + WIN_W - IN_W), F32)], axis=1)
    dw_out_p, sib_i = _dw_out_rs_call(merged_t, dh_b, dw_ref, tn=1024, tk=tk_dw)
    p_out = dw_out_p.reshape(4, D_MODEL // 4, D_MODEL)

    s512 = jnp.sum(a512, axis=2)
    heads = lambda v: jnp.sum(v.reshape(32, HEADDIM), axis=1).reshape(1, 32)
    d_ssd_nw = s512[:, 0].reshape(1, D_MODEL)
    d_dskip = heads(s512[:, 1].reshape(D_MODEL))
    d_alog = heads(s512[:, 2].reshape(D_MODEL)) * (1.0 / HEADDIM) * (-jnp.exp(A_log))
    d_dtb = s512[:, 3, 0:8].reshape(1, 32)
    s768 = jnp.sum(a768, axis=2)
    ungroup = lambda v: jnp.concatenate([v[g, :, 0:512] for g in range(4)] + [v[g, :, 512:640] for g in range(4)]
                                        + [v[g, :, 640:768] for g in range(4)], axis=1)
    d_cw = ungroup(s768[:, 0:4])
    d_cb = ungroup(s768[:, 4:5])
    d_sgu_b = jnp.sum(dbfull.reshape(128, SGU_GROUPS, 128), axis=2).T.reshape(1, SGU_GROUPS, 128)
    d_sgu_w = jnp.where(smask[None], dws, 0.0).reshape(1, SGU_GROUPS, 128, 128)
    fold = lambda a8: jnp.sum(a8, axis=0, keepdims=True)
    small_local = [fold(dgb8), fold(dgam8), fold(dbeta8), d_sgu_w, d_sgu_b, d_cw, d_cb,
                   d_dtb, d_alog, d_dskip, d_ssd_nw, fold(dfw8).reshape(D_MODEL), jnp.sum(loss_t[:, 0, 0]).reshape(1)]
    small_shapes = [a.shape for a in small_local]
    v_local = _pack(small_local)

    core = lax.axis_index("c")
    place = jnp.stack([chip, core]).astype(jnp.int32)
    hr_i, hr_o = D_MODEL // 2, D_MODEL // 8
    sib_o, sib_v = _rs_sibling_call(p_out, v_local)
    s1b_i, o_i = _rs_add_windows(dw_ref, sib_i, place, 256, "rs_add_in")
    s1b_o, o_o = _rs_add(p_out, sib_o, place, 256, "rs_add_out")
    chip_v = _add_pair(v_local, sib_v, v_local.shape[0], "ar_add_small")
    dxn, r_i, r_o, abs_v = _dx_rs_call(dproj_a, wt_a, dproj_b, wt_b, s1b_i, s1b_o, chip_v, tm=tm_mm)
    grad_x, dnw8 = _gradx_call(x2, dxn, dh, norm_w, t_row)
    abs_v = lax.dynamic_update_index_in_dim(abs_v, chip_v, chip, 0)
    f_i = _sum_own_recv(o_i, r_i, 256, "rs_sum_in")
    f_o = _sum_own_recv(o_o, r_o, 256, "rs_sum_out")
    sib_f_i, g_w_out, all_nw = _rs_join_call(f_i, f_o, dnw8)
    g_w_out = lax.dynamic_update_slice_in_dim(g_w_out, f_o, core * hr_o, axis=0)
    all_nw = lax.dynamic_update_index_in_dim(all_nw, dnw8, 2 * chip + core, 0)
    g_nw = fold(_sum_slots(all_nw, "ar_sum_norm_w"))
    total_v = _sum_slots(abs_v, "ar_sum_small")
    (g_gb, g_gam, g_beta, g_sw, g_sb, g_cw_full, g_cb, g_dtb, g_alog, g_dsk, g_snw, g_fnw, loss1) = _unpack(total_v, small_shapes)
    g_cw_shard = lax.dynamic_slice(g_cw_full, (0, chip * 768), (4, 768)).reshape(1, 4, 768)
    loss = loss1.reshape(())

    shard_t = lambda win: lax.dynamic_slice_in_dim(win, 8 * chip, SHARD_W, axis=1).T
    g_w_in, d_win, nm_win, nv_win = (a.T for a in _adamw_halves(w_in[0].T, shard_t(f_i), shard_t(sib_f_i), m_w_in[0].T,
                                                                v_w_in[0].T, place, 296, "adamw_w_in"))
    d_wout, nm_wout, nv_wout = _adamw(w_out[0], g_w_out, m_w_out[0], v_w_out[0], 128, "adamw_w_out")
    small_w = [norm_w, gate_b, sgu_norm_g, sgu_norm_b, sgu_w, sgu_b, conv_w, conv_b, dt_bias, A_log, D_skip, ssd_norm_w, final_norm_w]
    small_m = [m_norm_w, m_gate_b, m_sgu_norm_g, m_sgu_norm_b, m_sgu_w, m_sgu_b, m_conv_w, m_conv_b, m_dt_bias, m_A_log, m_D_skip, m_ssd_norm_w, m_final_norm_w]
    small_v = [v_norm_w, v_gate_b, v_sgu_norm_g, v_sgu_norm_b, v_sgu_w, v_sgu_b, v_conv_w, v_conv_b, v_dt_bias, v_A_log, v_D_skip, v_ssd_norm_w, v_final_norm_w]
    small_g = [g_nw, g_gb, g_gam, g_beta, g_sw, g_sb, g_cw_shard, g_cb, g_dtb, g_alog, g_dsk, g_snw, g_fnw]
    shapes_w = [a.shape for a in small_w]
    small_g = [a.reshape(shp) for a, shp in zip(small_g, shapes_w)]
    pw = _pack(small_w)
    pd, pm, pv = _adamw(pw, _pack(small_g), _pack(small_m), _pack(small_v), pw.shape[0], "adamw_small")
    d_small, nm_small, nv_small = _unpack(pd, shapes_w), _unpack(pm, shapes_w), _unpack(pv, shapes_w)

    def with_big(small, win, wout):
        o = list(small)
        return o[0:1] + [win.reshape(1, D_MODEL, SHARD_W)] + o[1:12] + [wout.reshape(1, D_MODEL // 4, D_MODEL)] + o[12:13]

    grads = with_big(small_g, g_w_in, g_w_out)
    deltas = with_big(d_small, d_win, d_wout)
    new_m = with_big(nm_small, nm_win, nm_wout)
    new_v = with_big(nv_small, nv_win, nv_wout)
    return (loss, grad_x.reshape(1, s, D_MODEL), *grads, *deltas, *new_m, *new_v)
```

```python
import functools

import jax
import jax.numpy as jnp
from jax import lax
from jax.experimental import pallas as pl
from jax.experimental.pallas import tpu as pltpu

F32 = jnp.float32
MXU_DTYPE = jnp.bfloat16

D_MODEL = 2048
EPS = 1e-5
CHUNK = 64
SGU_BLOCK = 128
SGU_GROUPS = 16
SSD_GROUPS = 4
SSD_GW = 512
SSD_STATE = 128
HEADDIM = 64
IN_W = 15392
SHARD_W = IN_W // 4
BW_B = 1408
FW_B = BW_B
XBC_O, DT_O = 512, 1280
NA = 10240

ADAM_LR = 0.001
ADAM_B1 = 0.9
ADAM_B2 = 0.999
ADAM_EPS = 1e-08
ADAM_WD = 0.01
ADAM_STEP = 10

T_SSD = 256
NG_SSD = 2
T_TOK = 128
T_OUT = 256
T_ROW = 512
TM_MM = 1024
TK_DW = 1024
VMEM_CAP = 60 * 1024 * 1024
MESH = pl.DeviceIdType.MESH


def _cparams(sem, est_bytes):
    lim = int(min(VMEM_CAP, max(32 * 1024 * 1024, est_bytes + 12 * 1024 * 1024)))
    return pltpu.CompilerParams(dimension_semantics=sem, vmem_limit_bytes=lim)


def _c(x):
    return x.astype(MXU_DTYPE)


def _dot(a, b):
    return jnp.dot(a, b, preferred_element_type=F32)


def _dot_nt(a, b):
    return lax.dot_general(a, b, (((1,), (1,)), ((), ())), preferred_element_type=F32)


def _dot_tn(a, b):
    return lax.dot_general(a, b, (((0,), (0,)), ((), ())), preferred_element_type=F32)


def _split(x, n):
    parts, r = [], x
    for _ in range(n):
        p = _c(r)
        parts.append(p)
        r = r - p.astype(F32)
    return parts


def _dot01_l(m01, x, n):
    acc = None
    for p in _split(x, n):
        t = _dot(m01, p)
        acc = t if acc is None else acc + t
    return acc


def _dot01_r(x, m01, n):
    acc = None
    for p in _split(x, n):
        t = _dot(p, m01)
        acc = t if acc is None else acc + t
    return acc


def _sigmoid(x):
    return 1.0 / (1.0 + jnp.exp(-x))


def _fold8(x):
    r, w = x.shape
    return jnp.sum(x.reshape(r // 8, 8, w), axis=0)


def _iota(shape, dim):
    return lax.broadcasted_iota(jnp.int32, shape, dim)


def _ssd_masks():
    l64 = _iota((CHUNK, SSD_GW), 0)
    s64 = jnp.bitwise_and(_iota((CHUNK, SSD_GW), 1), CHUNK - 1)
    diag = l64 == s64
    causal = l64 >= s64
    row_last = l64 == CHUNK - 1
    r4 = lax.shift_right_logical(_iota((256, 256), 0), 6)
    c4 = lax.shift_right_logical(_iota((256, 256), 1), 6)
    mask4 = r4 == c4
    return diag, causal, row_last, mask4


def _cumsum_mats(t):
    r, c = _iota((t, t), 0), _iota((t, t), 1)
    same = lax.shift_right_logical(r, 6) == lax.shift_right_logical(c, 6)
    tri = _c(jnp.where(same, jnp.where(c <= r, 1.0, 0.0), 0.0))
    trit = _c(jnp.where(same, jnp.where(c >= r, 1.0, 0.0), 0.0))
    return tri, trit


def _head_expand_mat():
    return _c(jnp.where(_iota((128, SSD_GW), 0) == lax.shift_right_logical(_iota((128, SSD_GW), 1), 6), 1.0, 0.0))


def _ssd_common(xs, bm, cm, dt, acs, masks):
    diag, causal, row_last, mask4 = masks
    row_e = jnp.sum(jnp.where(diag, acs, 0.0), axis=0, keepdims=True)
    seg = acs - row_e
    lm = jnp.exp(jnp.where(causal, seg, -1e30))
    bb, cb = _c(bm), _c(cm)
    brep = jnp.concatenate([bb] * 8, axis=0)
    cbrep = _dot_nt(cb, brep)
    m = cbrep * lm
    xdt = xs * dt
    acs_last = jnp.sum(jnp.where(row_last, acs, 0.0), axis=0, keepdims=True)
    dec = jnp.exp(acs_last - acs)
    eacs = jnp.exp(acs)
    cd = jnp.exp(acs_last)
    return dict(lm=lm, bb=bb, cb=cb, brep=brep, m=m, xdt=xdt, dec=dec, eacs=eacs, cd=cd)


def _blockdiag4(xb, mask4):
    return jnp.where(mask4, jnp.concatenate([xb] * 4, axis=0), jnp.zeros((), xb.dtype))


def _ssd_chunk_fwd(xs, bm, cm, dt, acs, d_skip, ht, masks):
    q = _ssd_common(xs, bm, cm, dt, acs, masks)
    mask4 = masks[3]
    mb, xdtb = _c(q["m"]), _c(q["xdt"])
    yd = []
    for blk in range(2):
        sl = slice(256 * blk, 256 * blk + 256)
        yd.append(_dot(mb[:, sl], _blockdiag4(xdtb[:, sl], mask4)))
    y_diag = jnp.concatenate(yd, axis=1)
    p = _dot(q["cb"], _c(ht))
    y = y_diag + p * q["eacs"] + xs * d_skip
    st = _dot_tn(q["bb"], _c(q["xdt"] * q["dec"]))
    return y, ht * q["cd"] + st


def _ssd_chunk_bwd(xs, bm, cm, dt, acs, d_skip, hprev, dht, dy, masks):
    diag, causal, row_last, mask4 = masks
    q = _ssd_common(xs, bm, cm, dt, acs, masks)
    lm, bb, cb, brep, m, xdt, dec, eacs, cd = (q[k] for k in ("lm", "bb", "cb", "brep", "m", "xdt", "dec", "eacs", "cd"))
    hb = _c(hprev)
    yoff = _dot(cb, hb) * eacs
    dyb = _c(dy)
    dpb = _c(dy * eacs)
    d_c = _dot_nt(dpb, hb)
    dh_y = _dot_tn(cb, dpb)
    mb, xdtb = _c(m), _c(xdt)
    dm_parts, dxdt_parts = [], []
    for blk in range(2):
        sl = slice(256 * blk, 256 * blk + 256)
        bd = _blockdiag4(xdtb[:, sl], mask4)
        dm_parts.append(_dot_nt(dyb[:, sl], bd))
        dxf = jnp.where(mask4, _dot_tn(mb[:, sl], dyb[:, sl]), 0.0)
        dxdt_parts.append(dxf[0:64] + dxf[64:128] + dxf[128:192] + dxf[192:256])
    dm = jnp.concatenate(dm_parts, axis=1)
    dxdt = jnp.concatenate(dxdt_parts, axis=1)
    dcbb = _c(dm * lm)
    g = dm * m
    d_c = d_c + _dot(dcbb, brep)
    dbrep = _dot_tn(dcbb, cb)
    d_b = dbrep[0:64]
    for r in range(1, 8):
        d_b = d_b + dbrep[64 * r:64 * r + 64]
    dhtb = _c(dht)
    dxd = _dot(bb, dhtb)
    xd = xdt * dec
    dxdt = dxdt + dxd * dec
    tq = dxd * xd
    d_b = d_b + _dot_nt(_c(xd), dhtb)
    dcd = jnp.sum(dht * hprev, axis=0, keepdims=True)
    col_g = jnp.sum(g, axis=0, keepdims=True)
    last = jnp.sum(tq, axis=0, keepdims=True) + dcd * cd
    qq = g - jnp.where(diag, col_g, 0.0) + dy * yoff - tq + jnp.where(row_last, last, 0.0)
    dxs = dxdt * dt + dy * d_skip
    return dxs, d_b, d_c, dht * cd + dh_y, dy * xs, qq, dxdt * xs


def _ssd_finish_dt(qq, p1, dt, a_neg, trit, mask4):
    bd4 = _c(jnp.where(mask4, 1.0, 0.0))
    dacs = jnp.concatenate([_dot01_r(qq[:, 256 * b:256 * b + 256], bd4, 2) for b in range(2)], axis=1)
    da = _dot01_l(trit, dacs, 2)
    return p1 + da * (a_neg * (1.0 / HEADDIM)), da * dt


def _softplus(x):
    return jnp.maximum(x, 0.0) + jnp.log(1.0 + jnp.exp(-jnp.abs(x)))


def _conv_taps(xpad, t):
    taps = []
    for k in range(4):
        sh = 3 - k
        v = xpad if sh == 0 else pltpu.roll(xpad, sh, 0)
        taps.append(v[8:8 + t])
    return taps


def _mm(a, b, *, tm, tn, tk, name, out_dtype=F32, col0=0, n=None, b_is_t=False):
    m, k = a.shape
    n = b.shape[0 if b_is_t else 1] if n is None else n
    nk = k // tk
    assert m % tm == 0 and n % tn == 0 and k % tk == 0, (a.shape, b.shape, tm, tn, tk)
    assert nk == 1 or out_dtype == F32
    dot = _dot_nt if b_is_t else _dot

    def body(a_ref, b_ref, o_ref):
        if nk == 1:
            o_ref[...] = dot(a_ref[...], b_ref[...]).astype(out_dtype)
        else:
            @pl.when(pl.program_id(2) == 0)
            def _():
                o_ref[...] = jnp.zeros_like(o_ref)

            o_ref[...] += dot(a_ref[...], b_ref[...])

    isz = jnp.dtype(a.dtype).itemsize
    est = 2 * (tm * tk + tk * tn) * isz + 2 * tm * tn * 4
    return pl.pallas_call(
        body, name=name,
        grid=(m // tm, n // tn, nk),
        in_specs=[pl.BlockSpec((tm, tk), lambda i, j, kk: (i, kk)),
                  pl.BlockSpec((tn, tk), lambda i, j, kk: (j + col0, kk)) if b_is_t
                  else pl.BlockSpec((tk, tn), lambda i, j, kk: (kk, j + col0))],
        out_specs=pl.BlockSpec((tm, tn), lambda i, j, kk: (i, j)),
        out_shape=jax.ShapeDtypeStruct((m, n), out_dtype),
        compiler_params=_cparams(("parallel", "parallel", "arbitrary"), est),
    )(a, b)


def _dw_groups(xnt, dpb, *, tk):
    m, k = xnt.shape
    nk = k // tk

    def body(a_ref, b_ref, zb_ref, xs_ref, bm_ref, cm_ref, dt_ref):
        outs = ((zb_ref, 0, 512), (xs_ref, 512, 1024), (bm_ref, 1024, 1152), (cm_ref, 1152, 1280), (dt_ref, 1280, 1408))

        @pl.when(pl.program_id(1) == 0)
        def _():
            for o_ref, _, _ in outs:
                o_ref[...] = jnp.zeros_like(o_ref)

        d = _dot(a_ref[...], b_ref[...])
        for o_ref, lo, hi in outs:
            o_ref[...] += d[:, lo:hi]

    isz = jnp.dtype(xnt.dtype).itemsize
    est = 2 * (m * tk + tk * BW_B) * isz + 3 * m * BW_B * 4
    piece = lambda w: pl.BlockSpec((m, w), lambda g, kk: (0, g))
    return pl.pallas_call(
        body, name="dw_in_b",
        grid=(SSD_GROUPS, nk),
        in_specs=[pl.BlockSpec((m, tk), lambda g, kk: (0, kk)), pl.BlockSpec((tk, BW_B), lambda g, kk: (kk, g))],
        out_specs=[piece(512), piece(512), piece(128), piece(128), piece(128)],
        out_shape=[jax.ShapeDtypeStruct((m, w), F32) for w in (2048, 2048, 512, 512, 512)],
        compiler_params=_cparams(("parallel", "arbitrary"), est),
    )(xnt, dpb)


def _dx_rs_call(dpa, wta, dpb, wtb, sb_in, sb_out, chip_v, *, tm):
    s = dpa.shape[0]
    tka, tkb = 1024, BW_B
    nka, nkb = dpa.shape[1] // tka, dpb.shape[1] // tkb
    ni, nk = s // tm, nka + nkb

    def body(a_ref, wa_ref, b_ref, wb_ref, sbin, sbout, cv, o_ref, rc_in, rc_out, abs_v, send, recv):
        i, kk = pl.program_id(0), pl.program_id(1)

        def copies():
            x, y, c, me, others = _place()
            sends, recvs = [], []
            for j, chip in enumerate(others):
                kj = 2 * chip[0] + chip[1]
                to = (*chip, c)
                sends += [_remote(sbin.at[kj], rc_in.at[j], send, recv, j, to),
                          _remote(sbout.at[kj], rc_out.at[j], send, recv, 3 + j, to),
                          _remote(cv, abs_v.at[me], send, recv, 6 + j, to)]
                recvs += [sends[-3], sends[-2], _remote(cv, abs_v.at[kj], send, recv, 6 + j, to)]
            return sends, recvs

        @pl.when((i == 0) & (kk == 0))
        def _():
            for cp in copies()[0]:
                cp.start()

        @pl.when(kk == 0)
        def _():
            o_ref[...] = jnp.zeros_like(o_ref)

        @pl.when(kk < nka)
        def _():
            o_ref[...] += _dot(a_ref[...], wa_ref[...])

        @pl.when(kk >= nka)
        def _():
            o_ref[...] += _dot(b_ref[...], wb_ref[...])

        @pl.when((i == ni - 1) & (kk == nk - 1))
        def _():
            sends, recvs = copies()
            for cp in recvs:
                cp.wait_recv()
            for cp in sends:
                cp.wait_send()

    isz = jnp.dtype(dpa.dtype).itemsize
    est = 2 * isz * (tm * tka + tka * D_MODEL + tm * tkb + tkb * D_MODEL) + 2 * tm * D_MODEL * 4
    outs = [jax.ShapeDtypeStruct((s, D_MODEL), F32),
            jax.ShapeDtypeStruct((3,) + sb_in.shape[1:], sb_in.dtype), jax.ShapeDtypeStruct((3,) + sb_out.shape[1:], sb_out.dtype),
            jax.ShapeDtypeStruct((4,) + chip_v.shape, F32)]
    return pl.pallas_call(
        body, name="dx_matmul_rs_chips",
        grid=(ni, nk),
        in_specs=[
            pl.BlockSpec((tm, tka), lambda i, kk: (i, jnp.minimum(kk, nka - 1))),
            pl.BlockSpec((tka, D_MODEL), lambda i, kk: (jnp.minimum(kk, nka - 1), 0)),
            pl.BlockSpec((tm, tkb), lambda i, kk: (i, jnp.maximum(kk - nka, 0))),
            pl.BlockSpec((tkb, D_MODEL), lambda i, kk: (jnp.maximum(kk - nka, 0), 0)),
            ANY, ANY, ANY,
        ],
        out_specs=[pl.BlockSpec((tm, D_MODEL), lambda i, kk: (i, 0)), ANY, ANY, ANY],
        out_shape=outs,
        scratch_shapes=[pltpu.SemaphoreType.DMA((9,)), pltpu.SemaphoreType.DMA((9,))],
        compiler_params=_cparams(("arbitrary", "arbitrary"), est),
    )(dpa, wta, dpb, wtb, sb_in, sb_out, chip_v)


def _gradx_call(x, dxn, dh, norm_w, tm):
    s = x.shape[0]

    def body(x_ref, g_ref, dh_ref, w_ref, gx_ref, dw_ref):
        @pl.when(pl.program_id(0) == 0)
        def _():
            dw_ref[...] = jnp.zeros_like(dw_ref)

        xv, gv = x_ref[...], g_ref[...]
        r = lax.rsqrt(jnp.mean(xv * xv, axis=-1, keepdims=True) + EPS)
        gw = gv * w_ref[...]
        gx_ref[...] = r * gw - xv * (r * r * r) * jnp.mean(xv * gw, axis=-1, keepdims=True) + dh_ref[...]
        dw_ref[...] += _fold8(gv * (xv * r))

    row = pl.BlockSpec((tm, D_MODEL), lambda i: (i, 0))
    return pl.pallas_call(
        body, name="grad_x",
        grid=(s // tm,),
        in_specs=[row, row, row, pl.BlockSpec((1, D_MODEL), lambda i: (0, 0))],
        out_specs=[row, pl.BlockSpec((8, D_MODEL), lambda i: (0, 0))],
        out_shape=[jax.ShapeDtypeStruct((s, D_MODEL), F32), jax.ShapeDtypeStruct((8, D_MODEL), F32)],
        compiler_params=_cparams(("arbitrary",), 2 * tm * D_MODEL * 16),
    )(x, dxn, dh, norm_w)


def _layernorm_stats(v):
    mu = jnp.mean(v, axis=-1, keepdims=True)
    vc = v - mu
    var = jnp.mean(vc * vc, axis=-1, keepdims=True)
    return vc * lax.rsqrt(var + EPS), lax.rsqrt(var + EPS)


def _tok_fwd_call(proj_a, y_b, gate_b, sgu_g, sgu_beta, wm, bias_full, t):
    s = proj_a.shape[0]

    def body(pa_ref, yb_ref, gb_ref, g_ref, be_ref, wm_ref, bf_ref, ya_ref, mg_ref, mgt_ref, mix_ref):
        u = pa_ref[:, 0:2048].astype(F32)
        v = pa_ref[:, 2048:4096].astype(F32)
        za = pa_ref[:, 4096:6144].astype(F32)
        xhat, _ = _layernorm_stats(v)
        vnb = _c(xhat * g_ref[...] + be_ref[...])
        for gi in range(SGU_GROUPS):
            sl = slice(128 * gi, 128 * gi + 128)
            mix_ref[:, sl] = _dot(wm_ref[gi], vnb[:, sl])
        mixed = mix_ref[...] + bf_ref[...]
        y_a = u * mixed * (za * _sigmoid(za))
        g0 = _sigmoid(pa_ref[:, 6144:8192].astype(F32) + gb_ref[:, 0:2048])
        g1 = _sigmoid(pa_ref[:, 8192:10240].astype(F32) + gb_ref[:, 2048:4096])
        merged = g0 * y_a + g1 * yb_ref[...].astype(F32)
        ya_ref[...] = _c(y_a)
        mg_ref[...] = _c(merged)
        mgt_ref[...] = _c(merged.T)

    row = pl.BlockSpec((t, D_MODEL), lambda i: (i, 0))
    vec = lambda w: pl.BlockSpec((1, w), lambda i: (0, 0))
    return pl.pallas_call(
        body, name="tok_fwd",
        grid=(s // t,),
        in_specs=[pl.BlockSpec((t, NA), lambda i: (i, 0)), row, vec(4096), vec(2048), vec(2048),
                  pl.BlockSpec((SGU_GROUPS, 128, 128), lambda i: (0, 0, 0)), pl.BlockSpec((128, D_MODEL), lambda i: (0, 0))],
        out_specs=[row, row, pl.BlockSpec((D_MODEL, t), lambda i: (0, i))],
        out_shape=[jax.ShapeDtypeStruct((s, D_MODEL), MXU_DTYPE), jax.ShapeDtypeStruct((s, D_MODEL), MXU_DTYPE),
                   jax.ShapeDtypeStruct((D_MODEL, s), MXU_DTYPE)],
        scratch_shapes=[pltpu.VMEM((t, D_MODEL), F32)],
        compiler_params=_cparams(("parallel",), 2 * t * NA * 4 + 12 * t * D_MODEL * 4),
    )(proj_a, y_b, gate_b, sgu_g, sgu_beta, wm, bias_full)


def _tok_bwd_call(proj_a, dmerged, y_a, y_b, gate_b, sgu_g, sgu_beta, wm, wmt, bias_full, t):
    s = proj_a.shape[0]

    def body(pa_ref, dm_ref, ya_ref, yb_ref, gb_ref, g_ref, be_ref, wm_ref, wmt_ref, bf_ref,
             dpa_ref, dyb_ref, dgb_ref, dgam_ref, dbeta_ref, dbf_ref, dws_ref, mix_ref, dvn_ref):
        @pl.when(pl.program_id(0) == 0)
        def _():
            dgb_ref[...] = jnp.zeros_like(dgb_ref)
            dgam_ref[...] = jnp.zeros_like(dgam_ref)
            dbeta_ref[...] = jnp.zeros_like(dbeta_ref)
            dbf_ref[...] = jnp.zeros_like(dbf_ref)
            dws_ref[...] = jnp.zeros_like(dws_ref)

        u = pa_ref[:, 0:2048].astype(F32)
        v = pa_ref[:, 2048:4096].astype(F32)
        za = pa_ref[:, 4096:6144].astype(F32)
        xhat, rstd = _layernorm_stats(v)
        vnb = _c(xhat * g_ref[...] + be_ref[...])
        for gi in range(SGU_GROUPS):
            sl = slice(128 * gi, 128 * gi + 128)
            mix_ref[:, sl] = _dot(wm_ref[gi], vnb[:, sl])
        mixed = mix_ref[...] + bf_ref[...]
        sig = _sigmoid(za)
        sz = za * sig
        dm = dm_ref[...].astype(F32)
        y_a = ya_ref[...].astype(F32)
        g0 = _sigmoid(pa_ref[:, 6144:8192].astype(F32) + gb_ref[:, 0:2048])
        g1 = _sigmoid(pa_ref[:, 8192:10240].astype(F32) + gb_ref[:, 2048:4096])
        dgl0 = dm * y_a * g0 * (1.0 - g0)
        dgl1 = dm * yb_ref[...].astype(F32) * g1 * (1.0 - g1)
        dyb_ref[...] = _c(dm * g1)
        dya = dm * g0
        dpa_ref[:, 6144:8192] = _c(dgl0)
        dpa_ref[:, 8192:10240] = _c(dgl1)
        dgb_ref[:, 0:2048] += _fold8(dgl0)
        dgb_ref[:, 2048:4096] += _fold8(dgl1)
        dpa_ref[:, 0:2048] = _c(dya * mixed * sz)
        dpa_ref[:, 4096:6144] = _c(dya * (u * mixed) * (sig * (1.0 + za * (1.0 - sig))))
        dmixed = dya * u * sz
        dbf_ref[...] += dmixed
        dmb = _c(dmixed)
        for gi in range(SGU_GROUPS):
            sl = slice(128 * gi, 128 * gi + 128)
            dvn_ref[:, sl] = _dot(wmt_ref[gi], dmb[:, sl])
            dws_ref[gi] += _dot_nt(dmb[:, sl], vnb[:, sl])
        dvn = dvn_ref[...]
        dgam_ref[...] += _fold8(dvn * xhat)
        dbeta_ref[...] += _fold8(dvn)
        dxh = dvn * g_ref[...]
        dv = rstd * (dxh - jnp.mean(dxh, axis=-1, keepdims=True) - xhat * jnp.mean(dxh * xhat, axis=-1, keepdims=True))
        dpa_ref[:, 2048:4096] = _c(dv)

    row = pl.BlockSpec((t, D_MODEL), lambda i: (i, 0))
    vec = lambda w: pl.BlockSpec((1, w), lambda i: (0, 0))
    acc = lambda w: pl.BlockSpec((8, w), lambda i: (0, 0))
    wspec = pl.BlockSpec((SGU_GROUPS, 128, 128), lambda i: (0, 0, 0))
    return pl.pallas_call(
        body, name="tok_bwd",
        grid=(s // t,),
        in_specs=[pl.BlockSpec((t, NA), lambda i: (i, 0)), row, row, row, vec(4096), vec(2048), vec(2048),
                  wspec, wspec, pl.BlockSpec((128, D_MODEL), lambda i: (0, 0))],
        out_specs=[pl.BlockSpec((t, NA), lambda i: (i, 0)), row, acc(4096), acc(2048), acc(2048),
                   pl.BlockSpec((128, D_MODEL), lambda i: (0, 0)), wspec],
        out_shape=[jax.ShapeDtypeStruct((s, NA), MXU_DTYPE), jax.ShapeDtypeStruct((s, D_MODEL), MXU_DTYPE),
                   jax.ShapeDtypeStruct((8, 4096), F32), jax.ShapeDtypeStruct((8, 2048), F32),
                   jax.ShapeDtypeStruct((8, 2048), F32), jax.ShapeDtypeStruct((128, D_MODEL), F32),
                   jax.ShapeDtypeStruct((SGU_GROUPS, 128, 128), F32)],
        scratch_shapes=[pltpu.VMEM((t, D_MODEL), F32), pltpu.VMEM((t, D_MODEL), F32)],
        compiler_params=_cparams(("arbitrary",), 2 * t * NA * 6 + 16 * t * D_MODEL * 4),
    )(proj_a, dmerged, y_a, y_b, gate_b, sgu_g, sgu_beta, wm, wmt, bias_full)


def _out_call(merged, x, target, w_out, fnw, t):
    s = x.shape[0]
    nt = s // t

    def body(mg_ref, x_ref, t_ref, w_ref, fw_ref, dh_ref, dhb_ref, dmg_ref, loss_ref, dfw_ref):
        @pl.when(pl.program_id(0) == 0)
        def _():
            dfw_ref[...] = jnp.zeros_like(dfw_ref)

        h = x_ref[...] + _dot(mg_ref[...], w_ref[...])
        r = lax.rsqrt(jnp.mean(h * h, axis=-1, keepdims=True) + EPS)
        hn = h * r
        err = hn * fw_ref[...] - t_ref[...]
        loss_ref[...] = jnp.full(loss_ref.shape, 0.5 * jnp.sum(jnp.mean(err * err, axis=-1, keepdims=True)), F32)
        dy = err * (1.0 / D_MODEL)
        dfw_ref[...] += _fold8(dy * hn)
        gw = dy * fw_ref[...]
        dh = r * gw - h * (r * r * r) * jnp.mean(h * gw, axis=-1, keepdims=True)
        dh_ref[...] = dh
        dhb = _c(dh)
        dhb_ref[...] = dhb
        dmg_ref[...] = _c(_dot_nt(dhb, w_ref[...]))

    row = pl.BlockSpec((t, D_MODEL), lambda i: (i, 0))
    return pl.pallas_call(
        body, name="out_proj_loss",
        grid=(nt,),
        in_specs=[row, row, row, pl.BlockSpec((D_MODEL, D_MODEL), lambda i: (0, 0)), pl.BlockSpec((1, D_MODEL), lambda i: (0, 0))],
        out_specs=[row, row, row, pl.BlockSpec((1, 8, 128), lambda i: (i, 0, 0)), pl.BlockSpec((8, D_MODEL), lambda i: (0, 0))],
        out_shape=[jax.ShapeDtypeStruct((s, D_MODEL), F32), jax.ShapeDtypeStruct((s, D_MODEL), MXU_DTYPE),
                   jax.ShapeDtypeStruct((s, D_MODEL), MXU_DTYPE), jax.ShapeDtypeStruct((nt, 8, 128), F32),
                   jax.ShapeDtypeStruct((8, D_MODEL), F32)],
        compiler_params=_cparams(("arbitrary",), 2 * D_MODEL * D_MODEL * 2 + 2 * t * D_MODEL * 24),
    )(merged, x, target, w_out, fnw)


def _ssd_fwd_call(proj_b, dtb, alog, dsk, cw, cb, nw, t, ng):
    s = proj_b.shape[0]
    nt, nch = s // t, t // CHUNK

    def body(pb_ref, halo_ref, dtb_ref, al_ref, ds_ref, cw_ref, cb_ref, nw_ref, y_ref, yb_ref, hp_ref, pre_ref,
             dt_ref, acs_ref, ht_ref, prev_ref):
        i = pl.program_id(1)

        @pl.when(i == 0)
        def _():
            ht_ref[...] = jnp.zeros_like(ht_ref)

        for gi in range(ng):
            prev_ref[:, 768 * gi:768 * gi + 768] = jnp.where(i == 0, 0.0, halo_ref[:, FW_B * gi + XBC_O:FW_B * gi + DT_O])
        masks = _ssd_masks()
        tri, _ = _cumsum_mats(t)
        a_neg = -jnp.exp(al_ref[...])
        expand = _head_expand_mat()
        for gi in range(ng):
            fo, go, no = FW_B * gi, SSD_GW * gi, 128 * gi
            dt_n = _softplus(pb_ref[:, fo + DT_O:fo + DT_O + 128] + dtb_ref[:, no:no + 128])
            acs_n = _dot01_l(tri, dt_n * a_neg[:, no:no + 128], 3)
            dt_ref[:, go:go + 512] = _dot01_r(dt_n, expand, 3)
            acs_ref[:, go:go + 512] = _dot01_r(acs_n, expand, 3)

        def chunk(c, carry):
            rows = pl.ds(pl.multiple_of(c * CHUNK, CHUNK), CHUNK)
            for gi in range(ng):
                fo, co, go, no = FW_B * gi, 768 * gi, SSD_GW * gi, 128 * gi
                xbc = pb_ref[rows, fo + XBC_O:fo + DT_O]
                taps = _conv_taps(jnp.concatenate([prev_ref[:, co:co + 768], xbc], axis=0), CHUNK)
                prev_ref[:, co:co + 768] = xbc[CHUNK - 8:CHUNK]
                pre = cb_ref[:, co:co + 768]
                for k in range(4):
                    pre = pre + taps[k] * cw_ref[k:k + 1, co:co + 768]
                pre_ref[rows, co:co + 768] = pre
                act = pre * _sigmoid(pre)
                dt = dt_ref[rows, go:go + 512]
                acs = acs_ref[rows, go:go + 512]
                ht = ht_ref[gi]
                hp_ref[c, :, go:go + 512] = ht
                y, ht_new = _ssd_chunk_fwd(act[:, 0:512], act[:, 512:640], act[:, 640:768], dt, acs,
                                           ds_ref[:, go:go + 512], ht, masks)
                y_ref[rows, go:go + 512] = y
                ht_ref[gi] = ht_new
                zb = pb_ref[rows, fo:fo + 512]
                hh = y * (zb * _sigmoid(zb))
                rr = lax.rsqrt(jnp.mean(hh * hh, axis=-1, keepdims=True) + EPS)
                yb_ref[rows, go:go + 512] = _c(hh * rr * nw_ref[:, go:go + 512])
            return carry

        lax.fori_loop(0, nch, chunk, 0)

    gvec = lambda w: pl.BlockSpec((1, ng * w), lambda g, i: (0, g))
    return pl.pallas_call(
        body, name="ssd_fwd",
        grid=(SSD_GROUPS // ng, nt),
        in_specs=[pl.BlockSpec((t, ng * FW_B), lambda g, i: (i, g)),
                  pl.BlockSpec((8, ng * FW_B), lambda g, i: (jnp.maximum(i * (t // 8) - 1, 0), g)),
                  gvec(128), gvec(128), gvec(512),
                  pl.BlockSpec((4, ng * 768), lambda g, i: (0, g)), gvec(768), gvec(512)],
        out_specs=[pl.BlockSpec((t, ng * SSD_GW), lambda g, i: (i, g)), pl.BlockSpec((t, ng * SSD_GW), lambda g, i: (i, g)),
                   pl.BlockSpec((nch, SSD_STATE, ng * SSD_GW), lambda g, i: (i, 0, g)),
                   pl.BlockSpec((t, ng * 768), lambda g, i: (i, g)),
                   pl.BlockSpec((t, ng * SSD_GW), lambda g, i: (i, g)), pl.BlockSpec((t, ng * SSD_GW), lambda g, i: (i, g))],
        out_shape=[jax.ShapeDtypeStruct((s, D_MODEL), F32), jax.ShapeDtypeStruct((s, D_MODEL), MXU_DTYPE),
                   jax.ShapeDtypeStruct((s // CHUNK, SSD_STATE, D_MODEL), F32),
                   jax.ShapeDtypeStruct((s, SSD_GROUPS * 768), F32),
                   jax.ShapeDtypeStruct((s, D_MODEL), F32), jax.ShapeDtypeStruct((s, D_MODEL), F32)],
        scratch_shapes=[pltpu.VMEM((ng, SSD_STATE, SSD_GW), F32), pltpu.VMEM((8, ng * 768), F32)],
        compiler_params=_cparams(("parallel", "arbitrary"), ng * (2 * t * FW_B * 4 + 16 * t * SSD_GW * 4) + 16 * 1024 * 1024),
    )(proj_b, proj_b, dtb, alog, dsk, cw, cb, nw)


def _ssd_bwd_call(proj_b, pre_all, dt_all, acs_all, dyb, y, hprev, dtb, alog, dsk, cw, nw, t, ng):
    s = proj_b.shape[0]
    nt, nch = s // t, t // CHUNK

    def body(pb_ref, pre_ref, dt_ref, acs_ref, dyb_ref, y_ref, hp_ref, dtb_ref, al_ref, ds_ref, cw_ref, nw_ref,
             dpb_ref, a512_ref, a768_ref, dht_ref, nxt_ref, q_ref, p1_ref):
        i = pl.program_id(1)

        @pl.when(i == 0)
        def _():
            dht_ref[...] = jnp.zeros_like(dht_ref)
            nxt_ref[...] = jnp.zeros_like(nxt_ref)
            a512_ref[...] = jnp.zeros_like(a512_ref)
            a768_ref[...] = jnp.zeros_like(a768_ref)

        _, trit = _cumsum_mats(t)
        a_neg = -jnp.exp(al_ref[...])
        masks = _ssd_masks()

        def chunk(cc, carry):
            c = nch - 1 - cc
            rows = pl.ds(pl.multiple_of(c * CHUNK, CHUNK), CHUNK)
            for gi in range(ng):
                fo, co, go, bo = FW_B * gi, 768 * gi, SSD_GW * gi, BW_B * gi
                pre = pre_ref[rows, co:co + 768]
                sp = _sigmoid(pre)
                act = pre * sp
                zb = pb_ref[rows, fo:fo + 512]
                yv = y_ref[rows, go:go + 512]
                sgz = _sigmoid(zb)
                sz = zb * sgz
                hh = yv * sz
                rr = lax.rsqrt(jnp.mean(hh * hh, axis=-1, keepdims=True) + EPS)
                dyb = dyb_ref[rows, go:go + 512].astype(F32)
                a512_ref[gi, 0] += _fold8(dyb * (hh * rr))
                tt = dyb * nw_ref[:, go:go + 512]
                dhh = rr * tt - hh * (rr * rr * rr) * jnp.mean(hh * tt, axis=-1, keepdims=True)
                dpb_ref[rows, bo:bo + 512] = _c(dhh * yv * (sgz * (1.0 + zb * (1.0 - sgz))))
                dxs, d_b, d_c, dht_prev, dyxs, qq, p1 = _ssd_chunk_bwd(
                    act[:, 0:512], act[:, 512:640], act[:, 640:768], dt_ref[rows, go:go + 512], acs_ref[rows, go:go + 512],
                    ds_ref[:, go:go + 512], hp_ref[c, :, go:go + 512], dht_ref[gi], dhh * sz, masks)
                dht_ref[gi] = dht_prev
                q_ref[rows, go:go + 512] = qq
                p1_ref[rows, go:go + 512] = p1
                a512_ref[gi, 1] += _fold8(dyxs)
                dpre = jnp.concatenate([dxs, d_b, d_c], axis=1) * (sp * (1.0 + pre * (1.0 - sp)))
                xbc = pb_ref[rows, fo + XBC_O:fo + DT_O]
                a768_ref[gi, 4] += _fold8(dpre)
                a768_ref[gi, 3] += _fold8(dpre * xbc)
                dpad = jnp.concatenate([dpre, nxt_ref[:, co:co + 768]], axis=0)
                dx = dpre * cw_ref[3:4, co:co + 768]
                for k in range(3):
                    d_k = pltpu.roll(dpad, CHUNK + 8 - (3 - k), 0)[0:CHUNK]
                    dx = dx + d_k * cw_ref[k:k + 1, co:co + 768]
                    a768_ref[gi, k] += _fold8(d_k * xbc)
                nxt_ref[:, co:co + 768] = dpre[0:8]
                dpb_ref[rows, bo + 512:bo + 1280] = _c(dx)
            return carry

        lax.fori_loop(0, nch, chunk, 0)

        rsel = _c(jnp.where(lax.shift_right_logical(_iota((SSD_GW, 128), 0), 6) == _iota((SSD_GW, 128), 1), 1.0, 0.0))
        for gi in range(ng):
            fo, go, bo, no = FW_B * gi, SSD_GW * gi, BW_B * gi, 128 * gi
            ddt, dadt = _ssd_finish_dt(q_ref[:, go:go + 512], p1_ref[:, go:go + 512], dt_ref[:, go:go + 512],
                                       a_neg[:, go:go + 512], trit, masks[3])
            sig_n = _sigmoid(pb_ref[:, fo + DT_O:fo + DT_O + 128] + dtb_ref[:, no:no + 128])
            ddtr_n = _dot01_r(ddt, rsel, 2) * sig_n
            dpb_ref[:, bo + DT_O:bo + DT_O + 128] = _c(ddtr_n)
            a512_ref[gi, 2] += _fold8(dadt)
            a512_ref[gi, 3, :, 0:128] += _fold8(ddtr_n)

    gvec = lambda w: pl.BlockSpec((1, ng * w), lambda g, i: (0, g))
    rev = lambda w: pl.BlockSpec((t, ng * w), lambda g, i: (nt - 1 - i, g))
    return pl.pallas_call(
        body, name="ssd_bwd",
        grid=(SSD_GROUPS // ng, nt),
        in_specs=[rev(FW_B), rev(768), rev(SSD_GW), rev(SSD_GW), rev(SSD_GW), rev(SSD_GW),
                  pl.BlockSpec((nch, SSD_STATE, ng * SSD_GW), lambda g, i: (nt - 1 - i, 0, g)),
                  gvec(128), gvec(512), gvec(512),
                  pl.BlockSpec((4, ng * 768), lambda g, i: (0, g)), gvec(512)],
        out_specs=[rev(BW_B),
                   pl.BlockSpec((ng, 4, 8, 512), lambda g, i: (g, 0, 0, 0)),
                   pl.BlockSpec((ng, 5, 8, 768), lambda g, i: (g, 0, 0, 0))],
        out_shape=[jax.ShapeDtypeStruct((s, SSD_GROUPS * BW_B), MXU_DTYPE),
                   jax.ShapeDtypeStruct((SSD_GROUPS, 4, 8, 512), F32),
                   jax.ShapeDtypeStruct((SSD_GROUPS, 5, 8, 768), F32)],
        scratch_shapes=[pltpu.VMEM((ng, SSD_STATE, SSD_GW), F32), pltpu.VMEM((8, ng * 768), F32),
                        pltpu.VMEM((t, ng * SSD_GW), F32), pltpu.VMEM((t, ng * SSD_GW), F32)],
        compiler_params=_cparams(("parallel", "arbitrary"), ng * (2 * t * FW_B * 4 + 18 * t * SSD_GW * 4) + 16 * 1024 * 1024),
    )(proj_b, pre_all, dt_all, acs_all, dyb, y, hprev, dtb, alog, dsk, cw, nw)


def _rows_call(body, ins, outs, tr, name):
    r = ins[0].shape[0]
    spec = lambda a: pl.BlockSpec((tr, a.shape[1]), lambda i: (i, 0))
    est = 2 * tr * sum(a.shape[1] * jnp.dtype(a.dtype).itemsize for a in list(ins) + list(outs))
    return pl.pallas_call(
        body, name=name, grid=(r // tr,),
        in_specs=[spec(a) for a in ins], out_specs=[spec(o) for o in outs], out_shape=list(outs),
        compiler_params=_cparams(("parallel",), est),
    )(*ins)


def _add_pair(a, b, tr, name):
    def body(a_ref, b_ref, o_ref):
        o_ref[...] = a_ref[...] + b_ref[...]

    return _rows_call(body, [a, b], [jax.ShapeDtypeStruct(a.shape, F32)], tr, name)[0]


def _rs_add(p, sib, place, tr, name):
    _, r, c = p.shape
    half = r // 2
    nb = half // tr

    def body(pl_ref, p_ref, s_ref, b_ref, own_ref):
        v = p_ref[0] + s_ref[0]
        b_ref[0] = v.astype(jnp.bfloat16)

        @pl.when(pl.program_id(1) == pl_ref[0])
        def _():
            own_ref[...] = v

    return pl.pallas_call(
        body, name=name,
        grid_spec=pltpu.PrefetchScalarGridSpec(
            num_scalar_prefetch=1, grid=(nb, 4),
            in_specs=[pl.BlockSpec((1, tr, c), lambda i, k, pr: (k, pr[1] * nb + i, 0)),
                      pl.BlockSpec((1, tr, c), lambda i, k, pr: (k, i, 0))],
            out_specs=[pl.BlockSpec((1, tr, c), lambda i, k, pr: (k, i, 0)),
                       pl.BlockSpec((tr, c), lambda i, k, pr: (i, 0))]),
        out_shape=[jax.ShapeDtypeStruct((4, half, c), jnp.bfloat16), jax.ShapeDtypeStruct((half, c), F32)],
        compiler_params=_cparams(("parallel", "arbitrary"), 2 * tr * c * 14),
    )(place, p, sib)


WIN_STEP = 3840
WIN_W = 3968


def _rs_add_windows(dw, sib, place, tr, name):
    r = dw.shape[0]
    half = r // 2
    nb = half // tr
    tail = WIN_W - WIN_STEP

    def body(pl_ref, pm_ref, pt_ref, s_ref, b_ref, own_ref):
        vm = pm_ref[...] + s_ref[0, :, 0:WIN_STEP]
        vt = pt_ref[...] + s_ref[0, :, WIN_STEP:WIN_W]
        b_ref[0, :, 0:WIN_STEP] = vm.astype(jnp.bfloat16)
        b_ref[0, :, WIN_STEP:WIN_W] = vt.astype(jnp.bfloat16)

        @pl.when(pl.program_id(1) == pl_ref[0])
        def _():
            own_ref[:, 0:WIN_STEP] = vm
            own_ref[:, WIN_STEP:WIN_W] = vt

    return pl.pallas_call(
        body, name=name,
        grid_spec=pltpu.PrefetchScalarGridSpec(
            num_scalar_prefetch=1, grid=(nb, 4),
            in_specs=[pl.BlockSpec((tr, WIN_STEP), lambda i, k, pr: (pr[1] * nb + i, k)),
                      pl.BlockSpec((tr, tail), lambda i, k, pr: (pr[1] * nb + i, (WIN_STEP // tail) * (k + 1))),
                      pl.BlockSpec((1, tr, WIN_W), lambda i, k, pr: (k, i, 0))],
            out_specs=[pl.BlockSpec((1, tr, WIN_W), lambda i, k, pr: (k, i, 0)),
                       pl.BlockSpec((tr, WIN_W), lambda i, k, pr: (i, 0))]),
        out_shape=[jax.ShapeDtypeStruct((4, half, WIN_W), jnp.bfloat16), jax.ShapeDtypeStruct((half, WIN_W), F32)],
        compiler_params=_cparams(("parallel", "arbitrary"), 2 * tr * WIN_W * 14),
    )(place, dw, dw, sib)


def _sum_own_recv(own, recv, tr, name):
    r, c = own.shape

    def body(o_ref, r_ref, out_ref):
        v = o_ref[...]
        for j in range(3):
            v = v + r_ref[j].astype(F32)
        out_ref[...] = v

    return pl.pallas_call(
        body, name=name, grid=(r // tr,),
        in_specs=[pl.BlockSpec((tr, c), lambda i: (i, 0)), pl.BlockSpec((3, tr, c), lambda i: (0, i, 0))],
        out_specs=pl.BlockSpec((tr, c), lambda i: (i, 0)),
        out_shape=jax.ShapeDtypeStruct((r, c), F32),
        compiler_params=_cparams(("parallel",), 2 * tr * c * 14),
    )(own, recv)


def _sum_slots(stack, name):
    n, r, w = stack.shape

    def body(a_ref, out_ref):
        v = a_ref[0]
        for k in range(1, n):
            v = v + a_ref[k]
        out_ref[...] = v

    return pl.pallas_call(
        body, name=name, grid=(1,),
        in_specs=[pl.BlockSpec((n, r, w), lambda i: (0, 0, 0))],
        out_specs=pl.BlockSpec((r, w), lambda i: (0, 0)),
        out_shape=jax.ShapeDtypeStruct((r, w), F32),
        compiler_params=_cparams(("arbitrary",), 2 * (n + 1) * r * w * 4),
    )(stack)


def _adamw(w, g, m, v, tr, name):
    def body(w_ref, g_ref, m_ref, v_ref, d_ref, nm_ref, nv_ref):
        d_ref[...], nm_ref[...], nv_ref[...] = _adam_math(w_ref[...], g_ref[...], m_ref[...], v_ref[...])

    o = jax.ShapeDtypeStruct(w.shape, F32)
    return _rows_call(body, [w, g, m, v], [o, o, o], tr, name)


def _adam_math(w, g, m, v):
    nm = ADAM_B1 * m + (1.0 - ADAM_B1) * g
    nv = ADAM_B2 * v + (1.0 - ADAM_B2) * (g * g)
    m_hat = nm / (1.0 - ADAM_B1 ** ADAM_STEP)
    v_hat = nv / (1.0 - ADAM_B2 ** ADAM_STEP)
    return -ADAM_LR * (m_hat / (jnp.sqrt(v_hat) + ADAM_EPS) + ADAM_WD * w), nm, nv


def _adamw_halves(w, g_own, g_sib, m, v, place, tr, name):
    r, c = w.shape

    def body(pl_ref, w_ref, go_ref, gs_ref, m_ref, v_ref, g_ref, d_ref, nm_ref, nv_ref):
        first = pl_ref[1] == 0
        own, sib = go_ref[...], gs_ref[...]
        g = jnp.concatenate([jnp.where(first, own, sib), jnp.where(first, sib, own)], axis=1)
        g_ref[...] = g
        d_ref[...], nm_ref[...], nv_ref[...] = _adam_math(w_ref[...], g, m_ref[...], v_ref[...])

    full = pl.BlockSpec((tr, c), lambda i, pr: (i, 0))
    half = pl.BlockSpec((tr, c // 2), lambda i, pr: (i, 0))
    o = jax.ShapeDtypeStruct((r, c), F32)
    return pl.pallas_call(
        body, name=name,
        grid_spec=pltpu.PrefetchScalarGridSpec(num_scalar_prefetch=1, grid=(r // tr,),
                                               in_specs=[full, half, half, full, full], out_specs=[full] * 4),
        out_shape=[o] * 4,
        compiler_params=_cparams(("parallel",), 2 * tr * c * 4 * 8),
    )(place, w, g_own, g_sib, m, v)


ANY = pl.BlockSpec(memory_space=pl.ANY)


def _place():
    x, y, c = lax.axis_index("x"), lax.axis_index("y"), lax.axis_index("c")
    others = [(1 - x, y), (x, 1 - y), (1 - x, 1 - y)]
    return x, y, c, 2 * x + y, others


def _remote(src, dst, send, recv, k, to):
    return pltpu.make_async_remote_copy(src_ref=src, dst_ref=dst, send_sem=send.at[k], recv_sem=recv.at[k],
                                        device_id=to, device_id_type=MESH)


def _norm_gather_call(x, norm_w, win_b, wout_b, cw8, tm):
    s = x.shape[0]
    ni = s // tm

    def body(x_ref, w_ref, win, wout, cw, xn_ref, xnt_ref, g_in, g_out, g_cw, send, recv):
        i = pl.program_id(0)

        def direct():
            xx, yy, c, me, others = _place()
            cps = []
            for a, (src, dst) in enumerate(((win, g_in), (wout, g_out))):
                half = src.shape[0] // 2
                mine = pl.ds(c * half, half)
                cps += [_remote(src.at[mine], dst.at[me, mine], send, recv, 6 * a + j, (*chip, c)) for j, chip in enumerate(others)]
            cps += [_remote(cw, g_cw.at[me], send, recv, 12 + j, (*chip, c)) for j, chip in enumerate(others)]
            return cps

        @pl.when(i == 0)
        def _():
            for cp in direct():
                cp.start()

        xv = x_ref[...]
        r = lax.rsqrt(jnp.mean(xv * xv, axis=-1, keepdims=True) + EPS)
        xn = xv * r * w_ref[...]
        xn_ref[...] = _c(xn)
        xnt_ref[...] = _c(xn.T)

        @pl.when(i == ni - 1)
        def _():
            xx, yy, c, me, others = _place()
            sib = (xx, yy, 1 - c)
            passed = []
            for a, dst in enumerate((g_in, g_out)):
                half = dst.shape[1] // 2
                mine = pl.ds(c * half, half)
                for j, chip in enumerate(others):
                    kj = 2 * chip[0] + chip[1]
                    _remote(dst.at[kj, mine], dst.at[kj, mine], send, recv, 6 * a + j, (*chip, c)).wait_recv()
                    cp = _remote(dst.at[kj, mine], dst.at[kj, mine], send, recv, 6 * a + 3 + j, sib)
                    cp.start()
                    passed.append(cp)
            for a, dst in enumerate((g_in, g_out)):
                half = dst.shape[1] // 2
                theirs = pl.ds((1 - c) * half, half)
                for j, chip in enumerate(others):
                    kj = 2 * chip[0] + chip[1]
                    _remote(dst.at[kj, theirs], dst.at[kj, theirs], send, recv, 6 * a + 3 + j, sib).wait_recv()
            for j, chip in enumerate(others):
                kj = 2 * chip[0] + chip[1]
                _remote(cw, g_cw.at[kj], send, recv, 12 + j, (*chip, c)).wait_recv()
            for cp in direct() + passed:
                cp.wait_send()

    outs = [jax.ShapeDtypeStruct((s, D_MODEL), MXU_DTYPE), jax.ShapeDtypeStruct((D_MODEL, s), MXU_DTYPE)]
    outs += [jax.ShapeDtypeStruct((4,) + a.shape, a.dtype) for a in (win_b, wout_b, cw8)]
    return pl.pallas_call(
        body, name="rmsnorm_gather_weights",
        grid=(ni,),
        in_specs=[pl.BlockSpec((tm, D_MODEL), lambda i: (i, 0)), pl.BlockSpec((1, D_MODEL), lambda i: (0, 0)), ANY, ANY, ANY],
        out_specs=[pl.BlockSpec((tm, D_MODEL), lambda i: (i, 0)), pl.BlockSpec((D_MODEL, tm), lambda i: (0, i)), ANY, ANY, ANY],
        out_shape=outs,
        scratch_shapes=[pltpu.SemaphoreType.DMA((15,)), pltpu.SemaphoreType.DMA((15,))],
        compiler_params=_cparams(("arbitrary",), 2 * tm * D_MODEL * 12),
    )(x, norm_w, win_b, wout_b, cw8)


def _dw_out_rs_call(a, b, dw, *, tn, tk):
    m, k = a.shape
    n = b.shape[1]
    nj, nk = n // tn, k // tk
    half = dw.shape[0] // 2

    def body(a_ref, b_ref, pin, o_ref, sib_in, send, recv):
        j, kk = pl.program_id(0), pl.program_id(1)

        def copies():
            x, y, c, me, others = _place()
            rows = pl.ds((1 - c) * half, half)
            return [_remote(pin.at[rows, pl.ds(WIN_STEP * w, WIN_W)], sib_in.at[w], send, recv, w, (x, y, 1 - c)) for w in range(4)]

        @pl.when((j == 0) & (kk == 0))
        def _():
            for cp in copies():
                cp.start()

        @pl.when(kk == 0)
        def _():
            o_ref[...] = jnp.zeros_like(o_ref)

        o_ref[...] += _dot(a_ref[...], b_ref[...])

        @pl.when((j == nj - 1) & (kk == nk - 1))
        def _():
            cps = copies()
            for cp in cps:
                cp.wait_recv()
            for cp in cps:
                cp.wait_send()

    isz = jnp.dtype(a.dtype).itemsize
    est = 2 * (m * tk + tk * tn) * isz + 2 * m * tn * 4
    outs = [jax.ShapeDtypeStruct((m, n), F32), jax.ShapeDtypeStruct((4, half, WIN_W), dw.dtype)]
    return pl.pallas_call(
        body, name="dw_out_rs_sibling",
        grid=(nj, nk),
        in_specs=[pl.BlockSpec((m, tk), lambda j, kk: (0, kk)), pl.BlockSpec((tk, tn), lambda j, kk: (kk, j)), ANY],
        out_specs=[pl.BlockSpec((m, tn), lambda j, kk: (0, j)), ANY],
        out_shape=outs,
        scratch_shapes=[pltpu.SemaphoreType.DMA((4,)), pltpu.SemaphoreType.DMA((4,))],
        compiler_params=_cparams(("arbitrary", "arbitrary"), est),
    )(a, b, dw)


def _rs_sibling_call(p_out, vsmall):
    def body(pout, vs, sib_out, sib_v, send, recv):
        x, y, c, me, others = _place()
        sib = (x, y, 1 - c)
        half = pout.shape[1] // 2
        cps = [_remote(pout.at[:, pl.ds((1 - c) * half, half)], sib_out, send, recv, 0, sib),
               _remote(vs, sib_v, send, recv, 1, sib)]
        for cp in cps:
            cp.start()
        for cp in cps:
            cp.wait_recv()
        for cp in cps:
            cp.wait_send()

    outs = [jax.ShapeDtypeStruct((4, p_out.shape[1] // 2, p_out.shape[2]), p_out.dtype),
            jax.ShapeDtypeStruct(vsmall.shape, vsmall.dtype)]
    return pl.pallas_call(
        body, name="rs_sibling",
        in_specs=[ANY] * 2, out_specs=[ANY] * 2, out_shape=outs,
        scratch_shapes=[pltpu.SemaphoreType.DMA((2,)), pltpu.SemaphoreType.DMA((2,))],
    )(p_out, vsmall)


def _rs_join_call(f_in, f_out, nw8):
    def body(fin, fout, nw, sib_in, full_out, all_nw, send, recv):
        x, y, c, me, others = _place()
        sib = (x, y, 1 - c)
        half = fout.shape[0]
        cps = [_remote(fin, sib_in, send, recv, 0, sib),
               _remote(fout, full_out.at[pl.ds(c * half, half)], send, recv, 1, sib)]
        mine = 4 * x + 2 * y + c
        peers = []
        for r in range(1, 8):
            px, py, pc = (1 - x if r & 4 else x), (1 - y if r & 2 else y), (1 - c if r & 1 else c)
            peers.append((r, (px, py, pc), 4 * px + 2 * py + pc))
            cps.append(_remote(nw, all_nw.at[mine], send, recv, 1 + r, (px, py, pc)))
        for cp in cps:
            cp.start()
        cps[0].wait_recv()
        _remote(fout, full_out.at[pl.ds((1 - c) * half, half)], send, recv, 1, sib).wait_recv()
        for r, peer, idx in peers:
            _remote(nw, all_nw.at[idx], send, recv, 1 + r, peer).wait_recv()
        for cp in cps:
            cp.wait_send()

    outs = [jax.ShapeDtypeStruct(f_in.shape, F32), jax.ShapeDtypeStruct((2 * f_out.shape[0], f_out.shape[1]), F32),
            jax.ShapeDtypeStruct((8,) + nw8.shape, F32)]
    return pl.pallas_call(
        body, name="rs_join",
        in_specs=[ANY] * 3, out_specs=[ANY] * 3, out_shape=outs,
        scratch_shapes=[pltpu.SemaphoreType.DMA((9,)), pltpu.SemaphoreType.DMA((9,))],
    )(f_in, f_out, nw8)


def _pack(arrs):
    parts = []
    for a in arrs:
        f = a.reshape(-1).astype(F32)
        pad = (-f.shape[0]) % 1024
        parts.append(jnp.pad(f, (0, pad)).reshape(-1, 128))
    return jnp.concatenate(parts, axis=0)


def _unpack(packed, shapes):
    out, row = [], 0
    for shp in shapes:
        n = 1
        for d in shp:
            n *= d
        rows = (n + 1023) // 1024 * 8
        out.append(packed[row:row + rows].reshape(-1)[:n].reshape(shp))
        row += rows
    return out


def _expand_heads(v32):
    return jnp.repeat(v32.reshape(32), HEADDIM).reshape(1, D_MODEL)


def kernel(x, norm_w, w_in, gate_b, sgu_norm_g, sgu_norm_b, sgu_w, sgu_b, conv_w, conv_b, dt_bias, A_log, D_skip, ssd_norm_w, w_out, final_norm_w, loss_target, m_norm_w, m_w_in, m_gate_b, m_sgu_norm_g, m_sgu_norm_b, m_sgu_w, m_sgu_b, m_conv_w, m_conv_b, m_dt_bias, m_A_log, m_D_skip, m_ssd_norm_w, m_w_out, m_final_norm_w, v_norm_w, v_w_in, v_gate_b, v_sgu_norm_g, v_sgu_norm_b, v_sgu_w, v_sgu_b, v_conv_w, v_conv_b, v_dt_bias, v_A_log, v_D_skip, v_ssd_norm_w, v_w_out, v_final_norm_w):
    s = x.shape[1]
    x2 = x.reshape(s, D_MODEL)
    tgt = loss_target.reshape(s, D_MODEL)
    t_ssd, t_tok, t_out, t_row = min(T_SSD, s), min(T_TOK, s), min(T_OUT, s), min(T_ROW, s)
    tm_mm, tk_dw = min(TM_MM, s), min(TK_DW, s)
    chip = 2 * lax.axis_index("x") + lax.axis_index("y")

    cw8 = jnp.pad(conv_w[0], ((0, 4), (0, 0)))
    win_b, wout_b = _c(w_in[0]), _c(w_out[0])
    xn, xnt, g_in, g_out, g_cw = _norm_gather_call(x2, norm_w, win_b, wout_b, cw8, t_row)
    g_in = lax.dynamic_update_index_in_dim(g_in, win_b, chip, 0)
    g_out = lax.dynamic_update_index_in_dim(g_out, wout_b, chip, 0)
    g_cw = lax.dynamic_update_index_in_dim(g_cw, cw8, chip, 0)
    wt = jnp.transpose(g_in, (0, 2, 1)).reshape(IN_W, D_MODEL)
    w_out_full = g_out.reshape(D_MODEL, D_MODEL)
    conv_w_full = jnp.transpose(g_cw[:, 0:4, :], (1, 0, 2)).reshape(4, 3072)

    wt_a = jnp.concatenate([wt[0:6144], wt[11296:15392]], axis=0)
    bw = []
    for g in range(SSD_GROUPS):
        xs_g = wt[8192 + 512 * g:8192 + 512 * g + 512]
        b_g = wt[10240 + 128 * g:10240 + 128 * g + 128]
        c_g = wt[10752 + 128 * g:10752 + 128 * g + 128]
        zb_g = wt[6144 + 512 * g:6144 + 512 * g + 512]
        dt_g = wt[11264 + 8 * g:11264 + 8 * g + 8]
        bw += [zb_g, xs_g, b_g, c_g, jnp.pad(dt_g, ((0, 120), (0, 0)))]
    wt_b = jnp.concatenate(bw, axis=0)

    def group_cols(full_xs, full_bc):
        parts = []
        for g in range(SSD_GROUPS):
            parts += [full_xs[:, 512 * g:512 * g + 512], full_bc[:, 128 * g:128 * g + 128], full_bc[:, 512 + 128 * g:512 + 128 * g + 128]]
        return jnp.concatenate(parts, axis=1)

    cw_g = group_cols(conv_w_full[:, 0:2048], conv_w_full[:, 2048:3072])
    cb_g = group_cols(conv_b[:, 0:2048], conv_b[:, 2048:3072])
    alog_e, dsk_e = _expand_heads(A_log), _expand_heads(D_skip)
    narrow = lambda v32: jnp.pad(v32.reshape(SSD_GROUPS, 8), ((0, 0), (0, 120))).reshape(1, SSD_GROUPS * 128)
    dtb_n, alog_n = narrow(dt_bias), narrow(A_log)

    pos_chunk = jnp.arange(SGU_BLOCK) // CHUNK
    smask = pos_chunk[None, :] <= pos_chunk[:, None]
    wm_f = jnp.where(smask[None], sgu_w[0], 0.0)
    wm = _c(wm_f)
    wmt = _c(jnp.transpose(wm_f, (0, 2, 1)))
    bias_full = jnp.repeat(sgu_b[0].T, D_MODEL // SGU_GROUPS, axis=1)
    fnw = final_norm_w.reshape(1, D_MODEL)

    proj_a = _mm(xn, wt_a, tm=tm_mm, tn=1024, tk=D_MODEL, name="in_proj_a", out_dtype=MXU_DTYPE, b_is_t=True)
    proj_b = _mm(xn, wt_b, tm=tm_mm, tn=BW_B, tk=D_MODEL, name="in_proj_b", b_is_t=True)
    y_ssd, y_b, hprev, pre_all, dt_all, acs_all = _ssd_fwd_call(proj_b, dtb_n, alog_n, dsk_e, cw_g, cb_g, ssd_norm_w, t_ssd, NG_SSD)
    y_a, merged, merged_t = _tok_fwd_call(proj_a, y_b, gate_b, sgu_norm_g, sgu_norm_b, wm, bias_full, t_tok)
    dh, dh_b, dmerged, loss_t, dfw8 = _out_call(merged, x2, tgt, w_out_full, fnw, t_out)

    dproj_a, dy_b, dgb8, dgam8, dbeta8, dbfull, dws = _tok_bwd_call(
        proj_a, dmerged, y_a, y_b, gate_b, sgu_norm_g, sgu_norm_b, wm, wmt, bias_full, t_tok)
    dproj_b, a512, a768 = _ssd_bwd_call(proj_b, pre_all, dt_all, acs_all, dy_b, y_ssd, hprev, dtb_n, alog_e, dsk_e, cw_g,
                                        ssd_norm_w, t_ssd, NG_SSD)
    dw_uvz = _mm(xnt, dproj_a, tm=D_MODEL, tn=1024, tk=min(2 * tk_dw, s), name="dw_in_uvz", n=6144)
    dw_gate = _mm(xnt, dproj_a, tm=D_MODEL, tn=1024, tk=min(2 * tk_dw, s), name="dw_in_gate", col0=6, n=4096)
    dw_zb, dw_xs, dw_bm, dw_cm, dw_dt = _dw_groups(xnt, dproj_b, tk=tk_dw)

    dw_dt32 = jnp.concatenate([dw_dt[:, 128 * g:128 * g + 8] for g in range(SSD_GROUPS)], axis=1)
    dw_ref = jnp.concatenate([dw_uvz, dw_zb, dw_xs, dw_bm, dw_cm, dw_dt32, dw_gate,
                              jnp.zeros((D_MODEL, 3 * WIN_STEP + WIN_W - IN_W), F32)], axis=1)
    dw_out_p, sib_i = _dw_out_rs_call(merged_t, dh_b, dw_ref, tn=1024, tk=tk_dw)
    p_out = dw_out_p.reshape(4, D_MODEL // 4, D_MODEL)

    s512 = jnp.sum(a512, axis=2)
    heads = lambda v: jnp.sum(v.reshape(32, HEADDIM), axis=1).reshape(1, 32)
    d_ssd_nw = s512[:, 0].reshape(1, D_MODEL)
    d_dskip = heads(s512[:, 1].reshape(D_MODEL))
    d_alog = heads(s512[:, 2].reshape(D_MODEL)) * (1.0 / HEADDIM) * (-jnp.exp(A_log))
    d_dtb = s512[:, 3, 0:8].reshape(1, 32)
    s768 = jnp.sum(a768, axis=2)
    ungroup = lambda v: jnp.concatenate([v[g, :, 0:512] for g in range(4)] + [v[g, :, 512:640] for g in range(4)]
                                        + [v[g, :, 640:768] for g in range(4)], axis=1)
    d_cw = ungroup(s768[:, 0:4])
    d_cb = ungroup(s768[:, 4:5])
    d_sgu_b = jnp.sum(dbfull.reshape(128, SGU_GROUPS, 128), axis=2).T.reshape(1, SGU_GROUPS, 128)
    d_sgu_w = jnp.where(smask[None], dws, 0.0).reshape(1, SGU_GROUPS, 128, 128)
    fold = lambda a8: jnp.sum(a8, axis=0, keepdims=True)
    small_local = [fold(dgb8), fold(dgam8), fold(dbeta8), d_sgu_w, d_sgu_b, d_cw, d_cb,
                   d_dtb, d_alog, d_dskip, d_ssd_nw, fold(dfw8).reshape(D_MODEL), jnp.sum(loss_t[:, 0, 0]).reshape(1)]
    small_shapes = [a.shape for a in small_local]
    v_local = _pack(small_local)

    core = lax.axis_index("c")
    place = jnp.stack([chip, core]).astype(jnp.int32)
    hr_i, hr_o = D_MODEL // 2, D_MODEL // 8
    sib_o, sib_v = _rs_sibling_call(p_out, v_local)
    s1b_i, o_i = _rs_add_windows(dw_ref, sib_i, place, 256, "rs_add_in")
    s1b_o, o_o = _rs_add(p_out, sib_o, place, 256, "rs_add_out")
    chip_v = _add_pair(v_local, sib_v, v_local.shape[0], "ar_add_small")
    dxn, r_i, r_o, abs_v = _dx_rs_call(dproj_a, wt_a, dproj_b, wt_b, s1b_i, s1b_o, chip_v, tm=tm_mm)
    grad_x, dnw8 = _gradx_call(x2, dxn, dh, norm_w, t_row)
    abs_v = lax.dynamic_update_index_in_dim(abs_v, chip_v, chip, 0)
    f_i = _sum_own_recv(o_i, r_i, 256, "rs_sum_in")
    f_o = _sum_own_recv(o_o, r_o, 256, "rs_sum_out")
    sib_f_i, g_w_out, all_nw = _rs_join_call(f_i, f_o, dnw8)
    g_w_out = lax.dynamic_update_slice_in_dim(g_w_out, f_o, core * hr_o, axis=0)
    all_nw = lax.dynamic_update_index_in_dim(all_nw, dnw8, 2 * chip + core, 0)
    g_nw = fold(_sum_slots(all_nw, "ar_sum_norm_w"))
    total_v = _sum_slots(abs_v, "ar_sum_small")
    (g_gb, g_gam, g_beta, g_sw, g_sb, g_cw_full, g_cb, g_dtb, g_alog, g_dsk, g_snw, g_fnw, loss1) = _unpack(total_v, small_shapes)
    g_cw_shard = lax.dynamic_slice(g_cw_full, (0, chip * 768), (4, 768)).reshape(1, 4, 768)
    loss = loss1.reshape(())

    shard_t = lambda win: lax.dynamic_slice_in_dim(win, 8 * chip, SHARD_W, axis=1).T
    g_w_in, d_win, nm_win, nv_win = (a.T for a in _adamw_halves(w_in[0].T, shard_t(f_i), shard_t(sib_f_i), m_w_in[0].T,
                                                                v_w_in[0].T, place, 296, "adamw_w_in"))
    d_wout, nm_wout, nv_wout = _adamw(w_out[0], g_w_out, m_w_out[0], v_w_out[0], 128, "adamw_w_out")
    small_w = [norm_w, gate_b, sgu_norm_g, sgu_norm_b, sgu_w, sgu_b, conv_w, conv_b, dt_bias, A_log, D_skip, ssd_norm_w, final_norm_w]
    small_m = [m_norm_w, m_gate_b, m_sgu_norm_g, m_sgu_norm_b, m_sgu_w, m_sgu_b, m_conv_w, m_conv_b, m_dt_bias, m_A_log, m_D_skip, m_ssd_norm_w, m_final_norm_w]
    small_v = [v_norm_w, v_gate_b, v_sgu_norm_g, v_sgu_norm_b, v_sgu_w, v_sgu_b, v_conv_w, v_conv_b, v_dt_bias, v_A_log, v_D_skip, v_ssd_norm_w, v_final_norm_w]
    small_g = [g_nw, g_gb, g_gam, g_beta, g_sw, g_sb, g_cw_shard, g_cb, g_dtb, g_alog, g_dsk, g_snw, g_fnw]
    shapes_w = [a.shape for a in small_w]
    small_g = [a.reshape(shp) for a, shp in zip(small_g, shapes_w)]
    pw = _pack(small_w)
    pd, pm, pv = _adamw(pw, _pack(small_g), _pack(small_m), _pack(small_v), pw.shape[0], "adamw_small")
    d_small, nm_small, nv_small = _unpack(pd, shapes_w), _unpack(pm, shapes_w), _unpack(pv, shapes_w)

    def with_big(small, win, wout):
        o = list(small)
        return o[0:1] + [win.reshape(1, D_MODEL, SHARD_W)] + o[1:12] + [wout.reshape(1, D_MODEL // 4, D_MODEL)] + o[12:13]

    grads = with_big(small_g, g_w_in, g_w_out)
    deltas = with_big(d_small, d_win, d_wout)
    new_m = with_big(nm_small, nm_win, nm_wout)
    new_v = with_big(nv_small, nv_win, nv_wout)
    return (loss, grad_x.reshape(1, s, D_MODEL), *grads, *deltas, *new_m, *new_v)
```

```python
import functools

import jax
import jax.numpy as jnp
from jax import lax
from jax.experimental import pallas as pl
from jax.experimental.pallas import tpu as pltpu

F32 = jnp.float32
MXU_DTYPE = jnp.bfloat16

D_MODEL = 2048
EPS = 1e-5
CHUNK = 64
SGU_BLOCK = 128
SGU_GROUPS = 16
SSD_GROUPS = 4
SSD_GW = 512
SSD_STATE = 128
HEADDIM = 64
IN_W = 15392
SHARD_W = IN_W // 4
BW_B = 1408
FW_B = BW_B
XBC_O, DT_O = 512, 1280
NA = 10240

ADAM_LR = 0.001
ADAM_B1 = 0.9
ADAM_B2 = 0.999
ADAM_EPS = 1e-08
ADAM_WD = 0.01
ADAM_STEP = 10

T_SSD = 256
NG_SSD = 2
T_TOK = 128
T_OUT = 256
T_ROW = 512
TM_MM = 1024
TK_DW = 1024
VMEM_CAP = 60 * 1024 * 1024
MESH = pl.DeviceIdType.MESH


def _cparams(sem, est_bytes):
    lim = int(min(VMEM_CAP, max(32 * 1024 * 1024, est_bytes + 12 * 1024 * 1024)))
    return pltpu.CompilerParams(dimension_semantics=sem, vmem_limit_bytes=lim)


def _c(x):
    return x.astype(MXU_DTYPE)


def _dot(a, b):
    return jnp.dot(a, b, preferred_element_type=F32)


def _dot_nt(a, b):
    return lax.dot_general(a, b, (((1,), (1,)), ((), ())), preferred_element_type=F32)


def _dot_tn(a, b):
    return lax.dot_general(a, b, (((0,), (0,)), ((), ())), preferred_element_type=F32)


def _split(x, n):
    parts, r = [], x
    for _ in range(n):
        p = _c(r)
        parts.append(p)
        r = r - p.astype(F32)
    return parts


def _dot01_l(m01, x, n):
    acc = None
    for p in _split(x, n):
        t = _dot(m01, p)
        acc = t if acc is None else acc + t
    return acc


def _dot01_r(x, m01, n):
    acc = None
    for p in _split(x, n):
        t = _dot(p, m01)
        acc = t if acc is None else acc + t
    return acc


def _sigmoid(x):
    return 1.0 / (1.0 + jnp.exp(-x))


def _fold8(x):
    r, w = x.shape
    return jnp.sum(x.reshape(r // 8, 8, w), axis=0)


def _iota(shape, dim):
    return lax.broadcasted_iota(jnp.int32, shape, dim)


def _ssd_masks():
    l64 = _iota((CHUNK, SSD_GW), 0)
    s64 = jnp.bitwise_and(_iota((CHUNK, SSD_GW), 1), CHUNK - 1)
    diag = l64 == s64
    causal = l64 >= s64
    row_last = l64 == CHUNK - 1
    r4 = lax.shift_right_logical(_iota((256, 256), 0), 6)
    c4 = lax.shift_right_logical(_iota((256, 256), 1), 6)
    mask4 = r4 == c4
    return diag, causal, row_last, mask4


def _cumsum_mats(t):
    r, c = _iota((t, t), 0), _iota((t, t), 1)
    same = lax.shift_right_logical(r, 6) == lax.shift_right_logical(c, 6)
    tri = _c(jnp.where(same, jnp.where(c <= r, 1.0, 0.0), 0.0))
    trit = _c(jnp.where(same, jnp.where(c >= r, 1.0, 0.0), 0.0))
    return tri, trit


def _head_expand_mat():
    return _c(jnp.where(_iota((128, SSD_GW), 0) == lax.shift_right_logical(_iota((128, SSD_GW), 1), 6), 1.0, 0.0))


def _ssd_common(xs, bm, cm, dt, acs, masks):
    diag, causal, row_last, mask4 = masks
    row_e = jnp.sum(jnp.where(diag, acs, 0.0), axis=0, keepdims=True)
    seg = acs - row_e
    lm = jnp.exp(jnp.where(causal, seg, -1e30))
    bb, cb = _c(bm), _c(cm)
    brep = jnp.concatenate([bb] * 8, axis=0)
    cbrep = _dot_nt(cb, brep)
    m = cbrep * lm
    xdt = xs * dt
    acs_last = jnp.sum(jnp.where(row_last, acs, 0.0), axis=0, keepdims=True)
    dec = jnp.exp(acs_last - acs)
    eacs = jnp.exp(acs)
    cd = jnp.exp(acs_last)
    return dict(lm=lm, bb=bb, cb=cb, brep=brep, m=m, xdt=xdt, dec=dec, eacs=eacs, cd=cd)


def _blockdiag4(xb, mask4):
    return jnp.where(mask4, jnp.concatenate([xb] * 4, axis=0), jnp.zeros((), xb.dtype))


def _ssd_chunk_fwd(xs, bm, cm, dt, acs, d_skip, ht, masks):
    q = _ssd_common(xs, bm, cm, dt, acs, masks)
    mask4 = masks[3]
    mb, xdtb = _c(q["m"]), _c(q["xdt"])
    yd = []
    for blk in range(2):
        sl = slice(256 * blk, 256 * blk + 256)
        yd.append(_dot(mb[:, sl], _blockdiag4(xdtb[:, sl], mask4)))
    y_diag = jnp.concatenate(yd, axis=1)
    p = _dot(q["cb"], _c(ht))
    y = y_diag + p * q["eacs"] + xs * d_skip
    st = _dot_tn(q["bb"], _c(q["xdt"] * q["dec"]))
    return y, ht * q["cd"] + st


def _ssd_chunk_bwd(xs, bm, cm, dt, acs, d_skip, hprev, dht, dy, masks):
    diag, causal, row_last, mask4 = masks
    q = _ssd_common(xs, bm, cm, dt, acs, masks)
    lm, bb, cb, brep, m, xdt, dec, eacs, cd = (q[k] for k in ("lm", "bb", "cb", "brep", "m", "xdt", "dec", "eacs", "cd"))
    hb = _c(hprev)
    yoff = _dot(cb, hb) * eacs
    dyb = _c(dy)
    dpb = _c(dy * eacs)
    d_c = _dot_nt(dpb, hb)
    dh_y = _dot_tn(cb, dpb)
    mb, xdtb = _c(m), _c(xdt)
    dm_parts, dxdt_parts = [], []
    for blk in range(2):
        sl = slice(256 * blk, 256 * blk + 256)
        bd = _blockdiag4(xdtb[:, sl], mask4)
        dm_parts.append(_dot_nt(dyb[:, sl], bd))
        dxf = jnp.where(mask4, _dot_tn(mb[:, sl], dyb[:, sl]), 0.0)
        dxdt_parts.append(dxf[0:64] + dxf[64:128] + dxf[128:192] + dxf[192:256])
    dm = jnp.concatenate(dm_parts, axis=1)
    dxdt = jnp.concatenate(dxdt_parts, axis=1)
    dcbb = _c(dm * lm)
    g = dm * m
    d_c = d_c + _dot(dcbb, brep)
    dbrep = _dot_tn(dcbb, cb)
    d_b = dbrep[0:64]
    for r in range(1, 8):
        d_b = d_b + dbrep[64 * r:64 * r + 64]
    dhtb = _c(dht)
    dxd = _dot(bb, dhtb)
    xd = xdt * dec
    dxdt = dxdt + dxd * dec
    tq = dxd * xd
    d_b = d_b + _dot_nt(_c(xd), dhtb)
    dcd = jnp.sum(dht * hprev, axis=0, keepdims=True)
    col_g = jnp.sum(g, axis=0, keepdims=True)
    last = jnp.sum(tq, axis=0, keepdims=True) + dcd * cd
    qq = g - jnp.where(diag, col_g, 0.0) + dy * yoff - tq + jnp.where(row_last, last, 0.0)
    dxs = dxdt * dt + dy * d_skip
    return dxs, d_b, d_c, dht * cd + dh_y, dy * xs, qq, dxdt * xs


def _ssd_finish_dt(qq, p1, dt, a_neg, trit, mask4):
    bd4 = _c(jnp.where(mask4, 1.0, 0.0))
    dacs = jnp.concatenate([_dot01_r(qq[:, 256 * b:256 * b + 256], bd4, 2) for b in range(2)], axis=1)
    da = _dot01_l(trit, dacs, 2)
    return p1 + da * (a_neg * (1.0 / HEADDIM)), da * dt


def _softplus(x):
    return jnp.maximum(x, 0.0) + jnp.log(1.0 + jnp.exp(-jnp.abs(x)))


def _conv_taps(xpad, t):
    taps = []
    for k in range(4):
        sh = 3 - k
        v = xpad if sh == 0 else pltpu.roll(xpad, sh, 0)
        taps.append(v[8:8 + t])
    return taps


def _mm(a, b, *, tm, tn, tk, name, out_dtype=F32, col0=0, n=None, b_is_t=False):
    m, k = a.shape
    n = b.shape[0 if b_is_t else 1] if n is None else n
    nk = k // tk
    assert m % tm == 0 and n % tn == 0 and k % tk == 0, (a.shape, b.shape, tm, tn, tk)
    assert nk == 1 or out_dtype == F32
    dot = _dot_nt if b_is_t else _dot

    def body(a_ref, b_ref, o_ref):
        if nk == 1:
            o_ref[...] = dot(a_ref[...], b_ref[...]).astype(out_dtype)
        else:
            @pl.when(pl.program_id(2) == 0)
            def _():
                o_ref[...] = jnp.zeros_like(o_ref)

            o_ref[...] += dot(a_ref[...], b_ref[...])

    isz = jnp.dtype(a.dtype).itemsize
    est = 2 * (tm * tk + tk * tn) * isz + 2 * tm * tn * 4
    return pl.pallas_call(
        body, name=name,
        grid=(m // tm, n // tn, nk),
        in_specs=[pl.BlockSpec((tm, tk), lambda i, j, kk: (i, kk)),
                  pl.BlockSpec((tn, tk), lambda i, j, kk: (j + col0, kk)) if b_is_t
                  else pl.BlockSpec((tk, tn), lambda i, j, kk: (kk, j + col0))],
        out_specs=pl.BlockSpec((tm, tn), lambda i, j, kk: (i, j)),
        out_shape=jax.ShapeDtypeStruct((m, n), out_dtype),
        compiler_params=_cparams(("parallel", "parallel", "arbitrary"), est),
    )(a, b)


def _dw_groups(xnt, dpb, *, tk):
    m, k = xnt.shape
    nk = k // tk

    def body(a_ref, b_ref, zb_ref, xs_ref, bm_ref, cm_ref, dt_ref):
        outs = ((zb_ref, 0, 512), (xs_ref, 512, 1024), (bm_ref, 1024, 1152), (cm_ref, 1152, 1280), (dt_ref, 1280, 1408))

        @pl.when(pl.program_id(1) == 0)
        def _():
            for o_ref, _, _ in outs:
                o_ref[...] = jnp.zeros_like(o_ref)

        d = _dot(a_ref[...], b_ref[...])
        for o_ref, lo, hi in outs:
            o_ref[...] += d[:, lo:hi]

    isz = jnp.dtype(xnt.dtype).itemsize
    est = 2 * (m * tk + tk * BW_B) * isz + 3 * m * BW_B * 4
    piece = lambda w: pl.BlockSpec((m, w), lambda g, kk: (0, g))
    return pl.pallas_call(
        body, name="dw_in_b",
        grid=(SSD_GROUPS, nk),
        in_specs=[pl.BlockSpec((m, tk), lambda g, kk: (0, kk)), pl.BlockSpec((tk, BW_B), lambda g, kk: (kk, g))],
        out_specs=[piece(512), piece(512), piece(128), piece(128), piece(128)],
        out_shape=[jax.ShapeDtypeStruct((m, w), F32) for w in (2048, 2048, 512, 512, 512)],
        compiler_params=_cparams(("parallel", "arbitrary"), est),
    )(xnt, dpb)


def _dx_rs_call(dpa, wta, dpb, wtb, sb_in, sb_out, chip_v, *, tm):
    s = dpa.shape[0]
    tka, tkb = 1024, BW_B
    nka, nkb = dpa.shape[1] // tka, dpb.shape[1] // tkb
    ni, nk = s // tm, nka + nkb

    def body(a_ref, wa_ref, b_ref, wb_ref, sbin, sbout, cv, o_ref, rc_in, rc_out, abs_v, send, recv):
        i, kk = pl.program_id(0), pl.program_id(1)

        def copies():
            x, y, c, me, others = _place()
            sends, recvs = [], []
            for j, chip in enumerate(others):
                kj = 2 * chip[0] + chip[1]
                to = (*chip, c)
                sends += [_remote(sbin.at[kj], rc_in.at[j], send, recv, j, to),
                          _remote(sbout.at[kj], rc_out.at[j], send, recv, 3 + j, to),
                          _remote(cv, abs_v.at[me], send, recv, 6 + j, to)]
                recvs += [sends[-3], sends[-2], _remote(cv, abs_v.at[kj], send, recv, 6 + j, to)]
            return sends, recvs

        @pl.when((i == 0) & (kk == 0))
        def _():
            for cp in copies()[0]:
                cp.start()

        @pl.when(kk == 0)
        def _():
            o_ref[...] = jnp.zeros_like(o_ref)

        @pl.when(kk < nka)
        def _():
            o_ref[...] += _dot(a_ref[...], wa_ref[...])

        @pl.when(kk >= nka)
        def _():
            o_ref[...] += _dot(b_ref[...], wb_ref[...])

        @pl.when((i == ni - 1) & (kk == nk - 1))
        def _():
            sends, recvs = copies()
            for cp in recvs:
                cp.wait_recv()
            for cp in sends:
                cp.wait_send()

    isz = jnp.dtype(dpa.dtype).itemsize
    est = 2 * isz * (tm * tka + tka * D_MODEL + tm * tkb + tkb * D_MODEL) + 2 * tm * D_MODEL * 4
    outs = [jax.ShapeDtypeStruct((s, D_MODEL), F32),
            jax.ShapeDtypeStruct((3,) + sb_in.shape[1:], sb_in.dtype), jax.ShapeDtypeStruct((3,) + sb_out.shape[1:], sb_out.dtype),
            jax.ShapeDtypeStruct((4,) + chip_v.shape, F32)]
    return pl.pallas_call(
        body, name="dx_matmul_rs_chips",
        grid=(ni, nk),
        in_specs=[
            pl.BlockSpec((tm, tka), lambda i, kk: (i, jnp.minimum(kk, nka - 1))),
            pl.BlockSpec((tka, D_MODEL), lambda i, kk: (jnp.minimum(kk, nka - 1), 0)),
            pl.BlockSpec((tm, tkb), lambda i, kk: (i, jnp.maximum(kk - nka, 0))),
            pl.BlockSpec((tkb, D_MODEL), lambda i, kk: (jnp.maximum(kk - nka, 0), 0)),
            ANY, ANY, ANY,
        ],
        out_specs=[pl.BlockSpec((tm, D_MODEL), lambda i, kk: (i, 0)), ANY, ANY, ANY],
        out_shape=outs,
        scratch_shapes=[pltpu.SemaphoreType.DMA((9,)), pltpu.SemaphoreType.DMA((9,))],
        compiler_params=_cparams(("arbitrary", "arbitrary"), est),
    )(dpa, wta, dpb, wtb, sb_in, sb_out, chip_v)


def _gradx_call(x, dxn, dh, norm_w, tm):
    s = x.shape[0]

    def body(x_ref, g_ref, dh_ref, w_ref, gx_ref, dw_ref):
        @pl.when(pl.program_id(0) == 0)
        def _():
            dw_ref[...] = jnp.zeros_like(dw_ref)

        xv, gv = x_ref[...], g_ref[...]
        r = lax.rsqrt(jnp.mean(xv * xv, axis=-1, keepdims=True) + EPS)
        gw = gv * w_ref[...]
        gx_ref[...] = r * gw - xv * (r * r * r) * jnp.mean(xv * gw, axis=-1, keepdims=True) + dh_ref[...]
        dw_ref[...] += _fold8(gv * (xv * r))

    row = pl.BlockSpec((tm, D_MODEL), lambda i: (i, 0))
    return pl.pallas_call(
        body, name="grad_x",
        grid=(s // tm,),
        in_specs=[row, row, row, pl.BlockSpec((1, D_MODEL), lambda i: (0, 0))],
        out_specs=[row, pl.BlockSpec((8, D_MODEL), lambda i: (0, 0))],
        out_shape=[jax.ShapeDtypeStruct((s, D_MODEL), F32), jax.ShapeDtypeStruct((8, D_MODEL), F32)],
        compiler_params=_cparams(("arbitrary",), 2 * tm * D_MODEL * 16),
    )(x, dxn, dh, norm_w)


def _layernorm_stats(v):
    mu = jnp.mean(v, axis=-1, keepdims=True)
    vc = v - mu
    var = jnp.mean(vc * vc, axis=-1, keepdims=True)
    return vc * lax.rsqrt(var + EPS), lax.rsqrt(var + EPS)


def _tok_fwd_call(proj_a, y_b, gate_b, sgu_g, sgu_beta, wm, bias_full, t):
    s = proj_a.shape[0]

    def body(pa_ref, yb_ref, gb_ref, g_ref, be_ref, wm_ref, bf_ref, ya_ref, mg_ref, mgt_ref, mix_ref):
        u = pa_ref[:, 0:2048].astype(F32)
        v = pa_ref[:, 2048:4096].astype(F32)
        za = pa_ref[:, 4096:6144].astype(F32)
        xhat, _ = _layernorm_stats(v)
        vnb = _c(xhat * g_ref[...] + be_ref[...])
        for gi in range(SGU_GROUPS):
            sl = slice(128 * gi, 128 * gi + 128)
            mix_ref[:, sl] = _dot(wm_ref[gi], vnb[:, sl])
        mixed = mix_ref[...] + bf_ref[...]
        y_a = u * mixed * (za * _sigmoid(za))
        g0 = _sigmoid(pa_ref[:, 6144:8192].astype(F32) + gb_ref[:, 0:2048])
        g1 = _sigmoid(pa_ref[:, 8192:10240].astype(F32) + gb_ref[:, 2048:4096])
        merged = g0 * y_a + g1 * yb_ref[...].astype(F32)
        ya_ref[...] = _c(y_a)
        mg_ref[...] = _c(merged)
        mgt_ref[...] = _c(merged.T)

    row = pl.BlockSpec((t, D_MODEL), lambda i: (i, 0))
    vec = lambda w: pl.BlockSpec((1, w), lambda i: (0, 0))
    return pl.pallas_call(
        body, name="tok_fwd",
        grid=(s // t,),
        in_specs=[pl.BlockSpec((t, NA), lambda i: (i, 0)), row, vec(4096), vec(2048), vec(2048),
                  pl.BlockSpec((SGU_GROUPS, 128, 128), lambda i: (0, 0, 0)), pl.BlockSpec((128, D_MODEL), lambda i: (0, 0))],
        out_specs=[row, row, pl.BlockSpec((D_MODEL, t), lambda i: (0, i))],
        out_shape=[jax.ShapeDtypeStruct((s, D_MODEL), MXU_DTYPE), jax.ShapeDtypeStruct((s, D_MODEL), MXU_DTYPE),
                   jax.ShapeDtypeStruct((D_MODEL, s), MXU_DTYPE)],
        scratch_shapes=[pltpu.VMEM((t, D_MODEL), F32)],
        compiler_params=_cparams(("parallel",), 2 * t * NA * 4 + 12 * t * D_MODEL * 4),
    )(proj_a, y_b, gate_b, sgu_g, sgu_beta, wm, bias_full)


def _tok_bwd_call(proj_a, dmerged, y_a, y_b, gate_b, sgu_g, sgu_beta, wm, wmt, bias_full, t):
    s = proj_a.shape[0]

    def body(pa_ref, dm_ref, ya_ref, yb_ref, gb_ref, g_ref, be_ref, wm_ref, wmt_ref, bf_ref,
             dpa_ref, dyb_ref, dgb_ref, dgam_ref, dbeta_ref, dbf_ref, dws_ref, mix_ref, dvn_ref):
        @pl.when(pl.program_id(0) == 0)
        def _():
            dgb_ref[...] = jnp.zeros_like(dgb_ref)
            dgam_ref[...] = jnp.zeros_like(dgam_ref)
            dbeta_ref[...] = jnp.zeros_like(dbeta_ref)
            dbf_ref[...] = jnp.zeros_like(dbf_ref)
            dws_ref[...] = jnp.zeros_like(dws_ref)

        u = pa_ref[:, 0:2048].astype(F32)
        v = pa_ref[:, 2048:4096].astype(F32)
        za = pa_ref[:, 4096:6144].astype(F32)
        xhat, rstd = _layernorm_stats(v)
        vnb = _c(xhat * g_ref[...] + be_ref[...])
        for gi in range(SGU_GROUPS):
            sl = slice(128 * gi, 128 * gi + 128)
            mix_ref[:, sl] = _dot(wm_ref[gi], vnb[:, sl])
        mixed = mix_ref[...] + bf_ref[...]
        sig = _sigmoid(za)
        sz = za * sig
        dm = dm_ref[...].astype(F32)
        y_a = ya_ref[...].astype(F32)
        g0 = _sigmoid(pa_ref[:, 6144:8192].astype(F32) + gb_ref[:, 0:2048])
        g1 = _sigmoid(pa_ref[:, 8192:10240].astype(F32) + gb_ref[:, 2048:4096])
        dgl0 = dm * y_a * g0 * (1.0 - g0)
        dgl1 = dm * yb_ref[...].astype(F32) * g1 * (1.0 - g1)
        dyb_ref[...] = _c(dm * g1)
        dya = dm * g0
        dpa_ref[:, 6144:8192] = _c(dgl0)
        dpa_ref[:, 8192:10240] = _c(dgl1)
        dgb_ref[:, 0:2048] += _fold8(dgl0)
        dgb_ref[:, 2048:4096] += _fold8(dgl1)
        dpa_ref[:, 0:2048] = _c(dya * mixed * sz)
        dpa_ref[:, 4096:6144] = _c(dya * (u * mixed) * (sig * (1.0 + za * (1.0 - sig))))
        dmixed = dya * u * sz
        dbf_ref[...] += dmixed
        dmb = _c(dmixed)
        for gi in range(SGU_GROUPS):
            sl = slice(128 * gi, 128 * gi + 128)
            dvn_ref[:, sl] = _dot(wmt_ref[gi], dmb[:, sl])
            dws_ref[gi] += _dot_nt(dmb[:, sl], vnb[:, sl])
        dvn = dvn_ref[...]
        dgam_ref[...] += _fold8(dvn * xhat)
        dbeta_ref[...] += _fold8(dvn)
        dxh = dvn * g_ref[...]
        dv = rstd * (dxh - jnp.mean(dxh, axis=-1, keepdims=True) - xhat * jnp.mean(dxh * xhat, axis=-1, keepdims=True))
        dpa_ref[:, 2048:4096] = _c(dv)

    row = pl.BlockSpec((t, D_MODEL), lambda i: (i, 0))
    vec = lambda w: pl.BlockSpec((1, w), lambda i: (0, 0))
    acc = lambda w: pl.BlockSpec((8, w), lambda i: (0, 0))
    wspec = pl.BlockSpec((SGU_GROUPS, 128, 128), lambda i: (0, 0, 0))
    return pl.pallas_call(
        body, name="tok_bwd",
        grid=(s // t,),
        in_specs=[pl.BlockSpec((t, NA), lambda i: (i, 0)), row, row, row, vec(4096), vec(2048), vec(2048),
                  wspec, wspec, pl.BlockSpec((128, D_MODEL), lambda i: (0, 0))],
        out_specs=[pl.BlockSpec((t, NA), lambda i: (i, 0)), row, acc(4096), acc(2048), acc(2048),
                   pl.BlockSpec((128, D_MODEL), lambda i: (0, 0)), wspec],
        out_shape=[jax.ShapeDtypeStruct((s, NA), MXU_DTYPE), jax.ShapeDtypeStruct((s, D_MODEL), MXU_DTYPE),
                   jax.ShapeDtypeStruct((8, 4096), F32), jax.ShapeDtypeStruct((8, 2048), F32),
                   jax.ShapeDtypeStruct((8, 2048), F32), jax.ShapeDtypeStruct((128, D_MODEL), F32),
                   jax.ShapeDtypeStruct((SGU_GROUPS, 128, 128), F32)],
        scratch_shapes=[pltpu.VMEM((t, D_MODEL), F32), pltpu.VMEM((t, D_MODEL), F32)],
        compiler_params=_cparams(("arbitrary",), 2 * t * NA * 6 + 16 * t * D_MODEL * 4),
    )(proj_a, dmerged, y_a, y_b, gate_b, sgu_g, sgu_beta, wm, wmt, bias_full)


def _out_call(merged, x, target, w_out, fnw, t):
    s = x.shape[0]
    nt = s // t

    def body(mg_ref, x_ref, t_ref, w_ref, fw_ref, dh_ref, dhb_ref, dmg_ref, loss_ref, dfw_ref):
        @pl.when(pl.program_id(0) == 0)
        def _():
            dfw_ref[...] = jnp.zeros_like(dfw_ref)

        h = x_ref[...] + _dot(mg_ref[...], w_ref[...])
        r = lax.rsqrt(jnp.mean(h * h, axis=-1, keepdims=True) + EPS)
        hn = h * r
        err = hn * fw_ref[...] - t_ref[...]
        loss_ref[...] = jnp.full(loss_ref.shape, 0.5 * jnp.sum(jnp.mean(err * err, axis=-1, keepdims=True)), F32)
        dy = err * (1.0 / D_MODEL)
        dfw_ref[...] += _fold8(dy * hn)
        gw = dy * fw_ref[...]
        dh = r * gw - h * (r * r * r) * jnp.mean(h * gw, axis=-1, keepdims=True)
        dh_ref[...] = dh
        dhb = _c(dh)
        dhb_ref[...] = dhb
        dmg_ref[...] = _c(_dot_nt(dhb, w_ref[...]))

    row = pl.BlockSpec((t, D_MODEL), lambda i: (i, 0))
    return pl.pallas_call(
        body, name="out_proj_loss",
        grid=(nt,),
        in_specs=[row, row, row, pl.BlockSpec((D_MODEL, D_MODEL), lambda i: (0, 0)), pl.BlockSpec((1, D_MODEL), lambda i: (0, 0))],
        out_specs=[row, row, row, pl.BlockSpec((1, 8, 128), lambda i: (i, 0, 0)), pl.BlockSpec((8, D_MODEL), lambda i: (0, 0))],
        out_shape=[jax.ShapeDtypeStruct((s, D_MODEL), F32), jax.ShapeDtypeStruct((s, D_MODEL), MXU_DTYPE),
                   jax.ShapeDtypeStruct((s, D_MODEL), MXU_DTYPE), jax.ShapeDtypeStruct((nt, 8, 128), F32),
                   jax.ShapeDtypeStruct((8, D_MODEL), F32)],
        compiler_params=_cparams(("arbitrary",), 2 * D_MODEL * D_MODEL * 2 + 2 * t * D_MODEL * 24),
    )(merged, x, target, w_out, fnw)


def _ssd_fwd_call(proj_b, dtb, alog, dsk, cw, cb, nw, t, ng):
    s = proj_b.shape[0]
    nt, nch = s // t, t // CHUNK

    def body(pb_ref, halo_ref, dtb_ref, al_ref, ds_ref, cw_ref, cb_ref, nw_ref, y_ref, yb_ref, hp_ref, pre_ref,
             dt_ref, acs_ref, ht_ref, prev_ref):
        i = pl.program_id(1)

        @pl.when(i == 0)
        def _():
            ht_ref[...] = jnp.zeros_like(ht_ref)

        for gi in range(ng):
            prev_ref[:, 768 * gi:768 * gi + 768] = jnp.where(i == 0, 0.0, halo_ref[:, FW_B * gi + XBC_O:FW_B * gi + DT_O])
        masks = _ssd_masks()
        tri, _ = _cumsum_mats(t)
        a_neg = -jnp.exp(al_ref[...])
        expand = _head_expand_mat()
        for gi in range(ng):
            fo, go, no = FW_B * gi, SSD_GW * gi, 128 * gi
            dt_n = _softplus(pb_ref[:, fo + DT_O:fo + DT_O + 128] + dtb_ref[:, no:no + 128])
            acs_n = _dot01_l(tri, dt_n * a_neg[:, no:no + 128], 3)
            dt_ref[:, go:go + 512] = _dot01_r(dt_n, expand, 2)
            acs_ref[:, go:go + 512] = _dot01_r(acs_n, expand, 2)

        def chunk(c, carry):
            rows = pl.ds(pl.multiple_of(c * CHUNK, CHUNK), CHUNK)
            for gi in range(ng):
                fo, co, go, no = FW_B * gi, 768 * gi, SSD_GW * gi, 128 * gi
                xbc = pb_ref[rows, fo + XBC_O:fo + DT_O]
                taps = _conv_taps(jnp.concatenate([prev_ref[:, co:co + 768], xbc], axis=0), CHUNK)
                prev_ref[:, co:co + 768] = xbc[CHUNK - 8:CHUNK]
                pre = cb_ref[:, co:co + 768]
                for k in range(4):
                    pre = pre + taps[k] * cw_ref[k:k + 1, co:co + 768]
                pre_ref[rows, co:co + 768] = pre
                act = pre * _sigmoid(pre)
                dt = dt_ref[rows, go:go + 512]
                acs = acs_ref[rows, go:go + 512]
                ht = ht_ref[gi]
                hp_ref[c, :, go:go + 512] = ht
                y, ht_new = _ssd_chunk_fwd(act[:, 0:512], act[:, 512:640], act[:, 640:768], dt, acs,
                                           ds_ref[:, go:go + 512], ht, masks)
                y_ref[rows, go:go + 512] = y
                ht_ref[gi] = ht_new
                zb = pb_ref[rows, fo:fo + 512]
                hh = y * (zb * _sigmoid(zb))
                rr = lax.rsqrt(jnp.mean(hh * hh, axis=-1, keepdims=True) + EPS)
                yb_ref[rows, go:go + 512] = _c(hh * rr * nw_ref[:, go:go + 512])
            return carry

        lax.fori_loop(0, nch, chunk, 0)

    gvec = lambda w: pl.BlockSpec((1, ng * w), lambda g, i: (0, g))
    return pl.pallas_call(
        body, name="ssd_fwd",
        grid=(SSD_GROUPS // ng, nt),
        in_specs=[pl.BlockSpec((t, ng * FW_B), lambda g, i: (i, g)),
                  pl.BlockSpec((8, ng * FW_B), lambda g, i: (jnp.maximum(i * (t // 8) - 1, 0), g)),
                  gvec(128), gvec(128), gvec(512),
                  pl.BlockSpec((4, ng * 768), lambda g, i: (0, g)), gvec(768), gvec(512)],
        out_specs=[pl.BlockSpec((t, ng * SSD_GW), lambda g, i: (i, g)), pl.BlockSpec((t, ng * SSD_GW), lambda g, i: (i, g)),
                   pl.BlockSpec((nch, SSD_STATE, ng * SSD_GW), lambda g, i: (i, 0, g)),
                   pl.BlockSpec((t, ng * 768), lambda g, i: (i, g)),
                   pl.BlockSpec((t, ng * SSD_GW), lambda g, i: (i, g)), pl.BlockSpec((t, ng * SSD_GW), lambda g, i: (i, g))],
        out_shape=[jax.ShapeDtypeStruct((s, D_MODEL), F32), jax.ShapeDtypeStruct((s, D_MODEL), MXU_DTYPE),
                   jax.ShapeDtypeStruct((s // CHUNK, SSD_STATE, D_MODEL), F32),
                   jax.ShapeDtypeStruct((s, SSD_GROUPS * 768), F32),
                   jax.ShapeDtypeStruct((s, D_MODEL), F32), jax.ShapeDtypeStruct((s, D_MODEL), F32)],
        scratch_shapes=[pltpu.VMEM((ng, SSD_STATE, SSD_GW), F32), pltpu.VMEM((8, ng * 768), F32)],
        compiler_params=_cparams(("parallel", "arbitrary"), ng * (2 * t * FW_B * 4 + 16 * t * SSD_GW * 4) + 16 * 1024 * 1024),
    )(proj_b, proj_b, dtb, alog, dsk, cw, cb, nw)


def _ssd_bwd_call(proj_b, pre_all, dt_all, acs_all, dyb, y, hprev, dtb, alog, dsk, cw, nw, t, ng):
    s = proj_b.shape[0]
    nt, nch = s // t, t // CHUNK

    def body(pb_ref, pre_ref, dt_ref, acs_ref, dyb_ref, y_ref, hp_ref, dtb_ref, al_ref, ds_ref, cw_ref, nw_ref,
             dpb_ref, a512_ref, a768_ref, dht_ref, nxt_ref, q_ref, p1_ref):
        i = pl.program_id(1)

        @pl.when(i == 0)
        def _():
            dht_ref[...] = jnp.zeros_like(dht_ref)
            nxt_ref[...] = jnp.zeros_like(nxt_ref)
            a512_ref[...] = jnp.zeros_like(a512_ref)
            a768_ref[...] = jnp.zeros_like(a768_ref)

        _, trit = _cumsum_mats(t)
        a_neg = -jnp.exp(al_ref[...])
        masks = _ssd_masks()

        def chunk(cc, carry):
            c = nch - 1 - cc
            rows = pl.ds(pl.multiple_of(c * CHUNK, CHUNK), CHUNK)
            for gi in range(ng):
                fo, co, go, bo = FW_B * gi, 768 * gi, SSD_GW * gi, BW_B * gi
                pre = pre_ref[rows, co:co + 768]
                sp = _sigmoid(pre)
                act = pre * sp
                zb = pb_ref[rows, fo:fo + 512]
                yv = y_ref[rows, go:go + 512]
                sgz = _sigmoid(zb)
                sz = zb * sgz
                hh = yv * sz
                rr = lax.rsqrt(jnp.mean(hh * hh, axis=-1, keepdims=True) + EPS)
                dyb = dyb_ref[rows, go:go + 512].astype(F32)
                a512_ref[gi, 0] += _fold8(dyb * (hh * rr))
                tt = dyb * nw_ref[:, go:go + 512]
                dhh = rr * tt - hh * (rr * rr * rr) * jnp.mean(hh * tt, axis=-1, keepdims=True)
                dpb_ref[rows, bo:bo + 512] = _c(dhh * yv * (sgz * (1.0 + zb * (1.0 - sgz))))
                dxs, d_b, d_c, dht_prev, dyxs, qq, p1 = _ssd_chunk_bwd(
                    act[:, 0:512], act[:, 512:640], act[:, 640:768], dt_ref[rows, go:go + 512], acs_ref[rows, go:go + 512],
                    ds_ref[:, go:go + 512], hp_ref[c, :, go:go + 512], dht_ref[gi], dhh * sz, masks)
                dht_ref[gi] = dht_prev
                q_ref[rows, go:go + 512] = qq
                p1_ref[rows, go:go + 512] = p1
                a512_ref[gi, 1] += _fold8(dyxs)
                dpre = jnp.concatenate([dxs, d_b, d_c], axis=1) * (sp * (1.0 + pre * (1.0 - sp)))
                xbc = pb_ref[rows, fo + XBC_O:fo + DT_O]
                a768_ref[gi, 4] += _fold8(dpre)
                a768_ref[gi, 3] += _fold8(dpre * xbc)
                dpad = jnp.concatenate([dpre, nxt_ref[:, co:co + 768]], axis=0)
                dx = dpre * cw_ref[3:4, co:co + 768]
                for k in range(3):
                    d_k = pltpu.roll(dpad, CHUNK + 8 - (3 - k), 0)[0:CHUNK]
                    dx = dx + d_k * cw_ref[k:k + 1, co:co + 768]
                    a768_ref[gi, k] += _fold8(d_k * xbc)
                nxt_ref[:, co:co + 768] = dpre[0:8]
                dpb_ref[rows, bo + 512:bo + 1280] = _c(dx)
            return carry

        lax.fori_loop(0, nch, chunk, 0)

        rsel = _c(jnp.where(lax.shift_right_logical(_iota((SSD_GW, 128), 0), 6) == _iota((SSD_GW, 128), 1), 1.0, 0.0))
        for gi in range(ng):
            fo, go, bo, no = FW_B * gi, SSD_GW * gi, BW_B * gi, 128 * gi
            ddt, dadt = _ssd_finish_dt(q_ref[:, go:go + 512], p1_ref[:, go:go + 512], dt_ref[:, go:go + 512],
                                       a_neg[:, go:go + 512], trit, masks[3])
            sig_n = _sigmoid(pb_ref[:, fo + DT_O:fo + DT_O + 128] + dtb_ref[:, no:no + 128])
            ddtr_n = _dot01_r(ddt, rsel, 2) * sig_n
            dpb_ref[:, bo + DT_O:bo + DT_O + 128] = _c(ddtr_n)
            a512_ref[gi, 2] += _fold8(dadt)
            a512_ref[gi, 3, :, 0:128] += _fold8(ddtr_n)

    gvec = lambda w: pl.BlockSpec((1, ng * w), lambda g, i: (0, g))
    rev = lambda w: pl.BlockSpec((t, ng * w), lambda g, i: (nt - 1 - i, g))
    return pl.pallas_call(
        body, name="ssd_bwd",
        grid=(SSD_GROUPS // ng, nt),
        in_specs=[rev(FW_B), rev(768), rev(SSD_GW), rev(SSD_GW), rev(SSD_GW), rev(SSD_GW),
                  pl.BlockSpec((nch, SSD_STATE, ng * SSD_GW), lambda g, i: (nt - 1 - i, 0, g)),
                  gvec(128), gvec(512), gvec(512),
                  pl.BlockSpec((4, ng * 768), lambda g, i: (0, g)), gvec(512)],
        out_specs=[rev(BW_B),
                   pl.BlockSpec((ng, 4, 8, 512), lambda g, i: (g, 0, 0, 0)),
                   pl.BlockSpec((ng, 5, 8, 768), lambda g, i: (g, 0, 0, 0))],
        out_shape=[jax.ShapeDtypeStruct((s, SSD_GROUPS * BW_B), MXU_DTYPE),
                   jax.ShapeDtypeStruct((SSD_GROUPS, 4, 8, 512), F32),
                   jax.ShapeDtypeStruct((SSD_GROUPS, 5, 8, 768), F32)],
        scratch_shapes=[pltpu.VMEM((ng, SSD_STATE, SSD_GW), F32), pltpu.VMEM((8, ng * 768), F32),
                        pltpu.VMEM((t, ng * SSD_GW), F32), pltpu.VMEM((t, ng * SSD_GW), F32)],
        compiler_params=_cparams(("parallel", "arbitrary"), ng * (2 * t * FW_B * 4 + 18 * t * SSD_GW * 4) + 16 * 1024 * 1024),
    )(proj_b, pre_all, dt_all, acs_all, dyb, y, hprev, dtb, alog, dsk, cw, nw)


def _rows_call(body, ins, outs, tr, name):
    r = ins[0].shape[0]
    spec = lambda a: pl.BlockSpec((tr, a.shape[1]), lambda i: (i, 0))
    est = 2 * tr * sum(a.shape[1] * jnp.dtype(a.dtype).itemsize for a in list(ins) + list(outs))
    return pl.pallas_call(
        body, name=name, grid=(r // tr,),
        in_specs=[spec(a) for a in ins], out_specs=[spec(o) for o in outs], out_shape=list(outs),
        compiler_params=_cparams(("parallel",), est),
    )(*ins)


def _add_pair(a, b, tr, name):
    def body(a_ref, b_ref, o_ref):
        o_ref[...] = a_ref[...] + b_ref[...]

    return _rows_call(body, [a, b], [jax.ShapeDtypeStruct(a.shape, F32)], tr, name)[0]


def _rs_add(p, sib, place, tr, name):
    _, r, c = p.shape
    half = r // 2
    nb = half // tr

    def body(pl_ref, p_ref, s_ref, b_ref, own_ref):
        v = p_ref[0] + s_ref[0]
        b_ref[0] = v.astype(jnp.bfloat16)

        @pl.when(pl.program_id(1) == pl_ref[0])
        def _():
            own_ref[...] = v

    return pl.pallas_call(
        body, name=name,
        grid_spec=pltpu.PrefetchScalarGridSpec(
            num_scalar_prefetch=1, grid=(nb, 4),
            in_specs=[pl.BlockSpec((1, tr, c), lambda i, k, pr: (k, pr[1] * nb + i, 0)),
                      pl.BlockSpec((1, tr, c), lambda i, k, pr: (k, i, 0))],
            out_specs=[pl.BlockSpec((1, tr, c), lambda i, k, pr: (k, i, 0)),
                       pl.BlockSpec((tr, c), lambda i, k, pr: (i, 0))]),
        out_shape=[jax.ShapeDtypeStruct((4, half, c), jnp.bfloat16), jax.ShapeDtypeStruct((half, c), F32)],
        compiler_params=_cparams(("parallel", "arbitrary"), 2 * tr * c * 14),
    )(place, p, sib)


WIN_STEP = 3840
WIN_W = 3968


def _rs_add_windows(dw, sib, place, tr, name):
    r = dw.shape[0]
    half = r // 2
    nb = half // tr
    tail = WIN_W - WIN_STEP

    def body(pl_ref, pm_ref, pt_ref, s_ref, b_ref, own_ref):
        vm = pm_ref[...] + s_ref[0, :, 0:WIN_STEP]
        vt = pt_ref[...] + s_ref[0, :, WIN_STEP:WIN_W]
        b_ref[0, :, 0:WIN_STEP] = vm.astype(jnp.bfloat16)
        b_ref[0, :, WIN_STEP:WIN_W] = vt.astype(jnp.bfloat16)

        @pl.when(pl.program_id(1) == pl_ref[0])
        def _():
            own_ref[:, 0:WIN_STEP] = vm
            own_ref[:, WIN_STEP:WIN_W] = vt

    return pl.pallas_call(
        body, name=name,
        grid_spec=pltpu.PrefetchScalarGridSpec(
            num_scalar_prefetch=1, grid=(nb, 4),
            in_specs=[pl.BlockSpec((tr, WIN_STEP), lambda i, k, pr: (pr[1] * nb + i, k)),
                      pl.BlockSpec((tr, tail), lambda i, k, pr: (pr[1] * nb + i, (WIN_STEP // tail) * (k + 1))),
                      pl.BlockSpec((1, tr, WIN_W), lambda i, k, pr: (k, i, 0))],
            out_specs=[pl.BlockSpec((1, tr, WIN_W), lambda i, k, pr: (k, i, 0)),
                       pl.BlockSpec((tr, WIN_W), lambda i, k, pr: (i, 0))]),
        out_shape=[jax.ShapeDtypeStruct((4, half, WIN_W), jnp.bfloat16), jax.ShapeDtypeStruct((half, WIN_W), F32)],
        compiler_params=_cparams(("parallel", "arbitrary"), 2 * tr * WIN_W * 14),
    )(place, dw, dw, sib)


def _sum_own_recv(own, recv, tr, name):
    r, c = own.shape

    def body(o_ref, r_ref, out_ref):
        v = o_ref[...]
        for j in range(3):
            v = v + r_ref[j].astype(F32)
        out_ref[...] = v

    return pl.pallas_call(
        body, name=name, grid=(r // tr,),
        in_specs=[pl.BlockSpec((tr, c), lambda i: (i, 0)), pl.BlockSpec((3, tr, c), lambda i: (0, i, 0))],
        out_specs=pl.BlockSpec((tr, c), lambda i: (i, 0)),
        out_shape=jax.ShapeDtypeStruct((r, c), F32),
        compiler_params=_cparams(("parallel",), 2 * tr * c * 14),
    )(own, recv)


def _sum_slots(stack, name):
    n, r, w = stack.shape

    def body(a_ref, out_ref):
        v = a_ref[0]
        for k in range(1, n):
            v = v + a_ref[k]
        out_ref[...] = v

    return pl.pallas_call(
        body, name=name, grid=(1,),
        in_specs=[pl.BlockSpec((n, r, w), lambda i: (0, 0, 0))],
        out_specs=pl.BlockSpec((r, w), lambda i: (0, 0)),
        out_shape=jax.ShapeDtypeStruct((r, w), F32),
        compiler_params=_cparams(("arbitrary",), 2 * (n + 1) * r * w * 4),
    )(stack)


def _adamw(w, g, m, v, tr, name):
    def body(w_ref, g_ref, m_ref, v_ref, d_ref, nm_ref, nv_ref):
        d_ref[...], nm_ref[...], nv_ref[...] = _adam_math(w_ref[...], g_ref[...], m_ref[...], v_ref[...])

    o = jax.ShapeDtypeStruct(w.shape, F32)
    return _rows_call(body, [w, g, m, v], [o, o, o], tr, name)


def _adam_math(w, g, m, v):
    nm = ADAM_B1 * m + (1.0 - ADAM_B1) * g
    nv = ADAM_B2 * v + (1.0 - ADAM_B2) * (g * g)
    m_hat = nm / (1.0 - ADAM_B1 ** ADAM_STEP)
    v_hat = nv / (1.0 - ADAM_B2 ** ADAM_STEP)
    return -ADAM_LR * (m_hat / (jnp.sqrt(v_hat) + ADAM_EPS) + ADAM_WD * w), nm, nv


def _adamw_halves(w, g_own, g_sib, m, v, place, tr, name):
    r, c = w.shape

    def body(pl_ref, w_ref, go_ref, gs_ref, m_ref, v_ref, g_ref, d_ref, nm_ref, nv_ref):
        first = pl_ref[1] == 0
        own, sib = go_ref[...], gs_ref[...]
        g = jnp.concatenate([jnp.where(first, own, sib), jnp.where(first, sib, own)], axis=1)
        g_ref[...] = g
        d_ref[...], nm_ref[...], nv_ref[...] = _adam_math(w_ref[...], g, m_ref[...], v_ref[...])

    full = pl.BlockSpec((tr, c), lambda i, pr: (i, 0))
    half = pl.BlockSpec((tr, c // 2), lambda i, pr: (i, 0))
    o = jax.ShapeDtypeStruct((r, c), F32)
    return pl.pallas_call(
        body, name=name,
        grid_spec=pltpu.PrefetchScalarGridSpec(num_scalar_prefetch=1, grid=(r // tr,),
                                               in_specs=[full, half, half, full, full], out_specs=[full] * 4),
        out_shape=[o] * 4,
        compiler_params=_cparams(("parallel",), 2 * tr * c * 4 * 8),
    )(place, w, g_own, g_sib, m, v)


ANY = pl.BlockSpec(memory_space=pl.ANY)


def _place():
    x, y, c = lax.axis_index("x"), lax.axis_index("y"), lax.axis_index("c")
    others = [(1 - x, y), (x, 1 - y), (1 - x, 1 - y)]
    return x, y, c, 2 * x + y, others


def _remote(src, dst, send, recv, k, to):
    return pltpu.make_async_remote_copy(src_ref=src, dst_ref=dst, send_sem=send.at[k], recv_sem=recv.at[k],
                                        device_id=to, device_id_type=MESH)


def _norm_gather_call(x, norm_w, win_b, wout_b, cw8, tm):
    s = x.shape[0]
    ni = s // tm
    h_in, h_out = win_b.shape[0] // 2, wout_b.shape[0] // 2
    q_in = h_in // 2

    def body(x_ref, w_ref, win, wout, cw, xn_ref, xnt_ref, g_in, g_out, g_cw, send, recv):
        i = pl.program_id(0)

        def direct():
            xx, yy, c, me, others = _place()
            mi, mo = pl.ds(c * h_in, h_in), pl.ds(c * h_out, h_out)
            cps = [_remote(win.at[mi], g_in.at[me, mi], send, recv, j, (*others[j], c)) for j in range(2)]
            cps += [_remote(wout.at[mo], g_out.at[me, mo], send, recv, 7 + j, (*chip, c)) for j, chip in enumerate(others)]
            cps += [_remote(cw, g_cw.at[me], send, recv, 13 + j, (*chip, c)) for j, chip in enumerate(others)]
            return cps

        @pl.when(i == 0)
        def _():
            for cp in direct():
                cp.start()

        xv = x_ref[...]
        r = lax.rsqrt(jnp.mean(xv * xv, axis=-1, keepdims=True) + EPS)
        xn = xv * r * w_ref[...]
        xn_ref[...] = _c(xn)
        xnt_ref[...] = _c(xn.T)

        @pl.when(i == ni - 1)
        def _():
            xx, yy, c, me, others = _place()
            sib = (xx, yy, 1 - c)
            kx, ky, kd = (2 * chip[0] + chip[1] for chip in others)
            mi, ti = pl.ds(c * h_in, h_in), pl.ds((1 - c) * h_in, h_in)
            quarter = [pl.ds(c * h_in, q_in), pl.ds(c * h_in + q_in, q_in)]
            started = []

            def go(cp):
                cp.start()
                started.append(cp)

            _remote(g_in.at[kx, mi], g_in.at[kx, mi], send, recv, 0, (*others[0], c)).wait_recv()
            go(_remote(g_in.at[kx, quarter[0]], g_in.at[kx, quarter[0]], send, recv, 2, (*others[1], c)))
            go(_remote(g_in.at[kx, mi], g_in.at[kx, mi], send, recv, 4, sib))
            _remote(g_in.at[ky, mi], g_in.at[ky, mi], send, recv, 1, (*others[1], c)).wait_recv()
            go(_remote(g_in.at[ky, quarter[1]], g_in.at[ky, quarter[1]], send, recv, 3, (*others[0], c)))
            go(_remote(g_in.at[ky, mi], g_in.at[ky, mi], send, recv, 5, sib))
            mo, to = pl.ds(c * h_out, h_out), pl.ds((1 - c) * h_out, h_out)
            for j, chip in enumerate(others):
                kj = 2 * chip[0] + chip[1]
                _remote(g_out.at[kj, mo], g_out.at[kj, mo], send, recv, 7 + j, (*chip, c)).wait_recv()
                go(_remote(g_out.at[kj, mo], g_out.at[kj, mo], send, recv, 10 + j, sib))
            _remote(g_in.at[kd, quarter[0]], g_in.at[kd, quarter[0]], send, recv, 2, (*others[1], c)).wait_recv()
            _remote(g_in.at[kd, quarter[1]], g_in.at[kd, quarter[1]], send, recv, 3, (*others[0], c)).wait_recv()
            go(_remote(g_in.at[kd, mi], g_in.at[kd, mi], send, recv, 6, sib))
            for k_src, sem in ((kx, 4), (ky, 5), (kd, 6)):
                _remote(g_in.at[k_src, ti], g_in.at[k_src, ti], send, recv, sem, sib).wait_recv()
            for j, chip in enumerate(others):
                kj = 2 * chip[0] + chip[1]
                _remote(g_out.at[kj, to], g_out.at[kj, to], send, recv, 10 + j, sib).wait_recv()
                _remote(cw, g_cw.at[kj], send, recv, 13 + j, (*chip, c)).wait_recv()
            for cp in direct() + started:
                cp.wait_send()

    outs = [jax.ShapeDtypeStruct((s, D_MODEL), MXU_DTYPE), jax.ShapeDtypeStruct((D_MODEL, s), MXU_DTYPE)]
    outs += [jax.ShapeDtypeStruct((4,) + a.shape, a.dtype) for a in (win_b, wout_b, cw8)]
    return pl.pallas_call(
        body, name="rmsnorm_gather_weights",
        grid=(ni,),
        in_specs=[pl.BlockSpec((tm, D_MODEL), lambda i: (i, 0)), pl.BlockSpec((1, D_MODEL), lambda i: (0, 0)), ANY, ANY, ANY],
        out_specs=[pl.BlockSpec((tm, D_MODEL), lambda i: (i, 0)), pl.BlockSpec((D_MODEL, tm), lambda i: (0, i)), ANY, ANY, ANY],
        out_shape=outs,
        scratch_shapes=[pltpu.SemaphoreType.DMA((16,)), pltpu.SemaphoreType.DMA((16,))],
        compiler_params=_cparams(("arbitrary",), 2 * tm * D_MODEL * 12),
    )(x, norm_w, win_b, wout_b, cw8)


def _dw_out_rs_call(a, b, dw, *, tn, tk):
    m, k = a.shape
    n = b.shape[1]
    nj, nk = n // tn, k // tk
    half = dw.shape[0] // 2

    def body(a_ref, b_ref, pin, o_ref, sib_in, send, recv):
        j, kk = pl.program_id(0), pl.program_id(1)

        def copies():
            x, y, c, me, others = _place()
            rows = pl.ds((1 - c) * half, half)
            return [_remote(pin.at[rows, pl.ds(WIN_STEP * w, WIN_W)], sib_in.at[w], send, recv, w, (x, y, 1 - c)) for w in range(4)]

        @pl.when((j == 0) & (kk == 0))
        def _():
            for cp in copies():
                cp.start()

        @pl.when(kk == 0)
        def _():
            o_ref[...] = jnp.zeros_like(o_ref)

        o_ref[...] += _dot(a_ref[...], b_ref[...])

        @pl.when((j == nj - 1) & (kk == nk - 1))
        def _():
            cps = copies()
            for cp in cps:
                cp.wait_recv()
            for cp in cps:
                cp.wait_send()

    isz = jnp.dtype(a.dtype).itemsize
    est = 2 * (m * tk + tk * tn) * isz + 2 * m * tn * 4
    outs = [jax.ShapeDtypeStruct((m, n), F32), jax.ShapeDtypeStruct((4, half, WIN_W), dw.dtype)]
    return pl.pallas_call(
        body, name="dw_out_rs_sibling",
        grid=(nj, nk),
        in_specs=[pl.BlockSpec((m, tk), lambda j, kk: (0, kk)), pl.BlockSpec((tk, tn), lambda j, kk: (kk, j)), ANY],
        out_specs=[pl.BlockSpec((m, tn), lambda j, kk: (0, j)), ANY],
        out_shape=outs,
        scratch_shapes=[pltpu.SemaphoreType.DMA((4,)), pltpu.SemaphoreType.DMA((4,))],
        compiler_params=_cparams(("arbitrary", "arbitrary"), est),
    )(a, b, dw)


def _rs_sibling_call(p_out, vsmall):
    def body(pout, vs, sib_out, sib_v, send, recv):
        x, y, c, me, others = _place()
        sib = (x, y, 1 - c)
        half = pout.shape[1] // 2
        cps = [_remote(pout.at[:, pl.ds((1 - c) * half, half)], sib_out, send, recv, 0, sib),
               _remote(vs, sib_v, send, recv, 1, sib)]
        for cp in cps:
            cp.start()
        for cp in cps:
            cp.wait_recv()
        for cp in cps:
            cp.wait_send()

    outs = [jax.ShapeDtypeStruct((4, p_out.shape[1] // 2, p_out.shape[2]), p_out.dtype),
            jax.ShapeDtypeStruct(vsmall.shape, vsmall.dtype)]
    return pl.pallas_call(
        body, name="rs_sibling",
        in_specs=[ANY] * 2, out_specs=[ANY] * 2, out_shape=outs,
        scratch_shapes=[pltpu.SemaphoreType.DMA((2,)), pltpu.SemaphoreType.DMA((2,))],
    )(p_out, vsmall)


def _rs_join_call(f_in, f_out, nw8):
    def body(fin, fout, nw, sib_in, full_out, all_nw, send, recv):
        x, y, c, me, others = _place()
        sib = (x, y, 1 - c)
        half = fout.shape[0]
        cps = [_remote(fin, sib_in, send, recv, 0, sib),
               _remote(fout, full_out.at[pl.ds(c * half, half)], send, recv, 1, sib)]
        mine = 4 * x + 2 * y + c
        peers = []
        for r in range(1, 8):
            px, py, pc = (1 - x if r & 4 else x), (1 - y if r & 2 else y), (1 - c if r & 1 else c)
            peers.append((r, (px, py, pc), 4 * px + 2 * py + pc))
            cps.append(_remote(nw, all_nw.at[mine], send, recv, 1 + r, (px, py, pc)))
        for cp in cps:
            cp.start()
        cps[0].wait_recv()
        _remote(fout, full_out.at[pl.ds((1 - c) * half, half)], send, recv, 1, sib).wait_recv()
        for r, peer, idx in peers:
            _remote(nw, all_nw.at[idx], send, recv, 1 + r, peer).wait_recv()
        for cp in cps:
            cp.wait_send()

    outs = [jax.ShapeDtypeStruct(f_in.shape, F32), jax.ShapeDtypeStruct((2 * f_out.shape[0], f_out.shape[1]), F32),
            jax.ShapeDtypeStruct((8,) + nw8.shape, F32)]
    return pl.pallas_call(
        body, name="rs_join",
        in_specs=[ANY] * 3, out_specs=[ANY] * 3, out_shape=outs,
        scratch_shapes=[pltpu.SemaphoreType.DMA((9,)), pltpu.SemaphoreType.DMA((9,))],
    )(f_in, f_out, nw8)


def _pack(arrs):
    parts = []
    for a in arrs:
        f = a.reshape(-1).astype(F32)
        pad = (-f.shape[0]) % 1024
        parts.append(jnp.pad(f, (0, pad)).reshape(-1, 128))
    return jnp.concatenate(parts, axis=0)


def _unpack(packed, shapes):
    out, row = [], 0
    for shp in shapes:
        n = 1
        for d in shp:
            n *= d
        rows = (n + 1023) // 1024 * 8
        out.append(packed[row:row + rows].reshape(-1)[:n].reshape(shp))
        row += rows
    return out


def _expand_heads(v32):
    return jnp.repeat(v32.reshape(32), HEADDIM).reshape(1, D_MODEL)


def kernel(x, norm_w, w_in, gate_b, sgu_norm_g, sgu_norm_b, sgu_w, sgu_b, conv_w, conv_b, dt_bias, A_log, D_skip, ssd_norm_w, w_out, final_norm_w, loss_target, m_norm_w, m_w_in, m_gate_b, m_sgu_norm_g, m_sgu_norm_b, m_sgu_w, m_sgu_b, m_conv_w, m_conv_b, m_dt_bias, m_A_log, m_D_skip, m_ssd_norm_w, m_w_out, m_final_norm_w, v_norm_w, v_w_in, v_gate_b, v_sgu_norm_g, v_sgu_norm_b, v_sgu_w, v_sgu_b, v_conv_w, v_conv_b, v_dt_bias, v_A_log, v_D_skip, v_ssd_norm_w, v_w_out, v_final_norm_w):
    s = x.shape[1]
    x2 = x.reshape(s, D_MODEL)
    tgt = loss_target.reshape(s, D_MODEL)
    t_ssd, t_tok, t_out, t_row = min(T_SSD, s), min(T_TOK, s), min(T_OUT, s), min(T_ROW, s)
    tm_mm, tk_dw = min(TM_MM, s), min(TK_DW, s)
    chip = 2 * lax.axis_index("x") + lax.axis_index("y")

    cw8 = jnp.pad(conv_w[0], ((0, 4), (0, 0)))
    win_b, wout_b = _c(w_in[0]), _c(w_out[0])
    xn, xnt, g_in, g_out, g_cw = _norm_gather_call(x2, norm_w, win_b, wout_b, cw8, t_row)
    g_in = lax.dynamic_update_index_in_dim(g_in, win_b, chip, 0)
    g_out = lax.dynamic_update_index_in_dim(g_out, wout_b, chip, 0)
    g_cw = lax.dynamic_update_index_in_dim(g_cw, cw8, chip, 0)
    wt = jnp.transpose(g_in, (0, 2, 1)).reshape(IN_W, D_MODEL)
    w_out_full = g_out.reshape(D_MODEL, D_MODEL)
    conv_w_full = jnp.transpose(g_cw[:, 0:4, :], (1, 0, 2)).reshape(4, 3072)

    wt_a = jnp.concatenate([wt[0:6144], wt[11296:15392]], axis=0)
    bw = []
    for g in range(SSD_GROUPS):
        xs_g = wt[8192 + 512 * g:8192 + 512 * g + 512]
        b_g = wt[10240 + 128 * g:10240 + 128 * g + 128]
        c_g = wt[10752 + 128 * g:10752 + 128 * g + 128]
        zb_g = wt[6144 + 512 * g:6144 + 512 * g + 512]
        dt_g = wt[11264 + 8 * g:11264 + 8 * g + 8]
        bw += [zb_g, xs_g, b_g, c_g, jnp.pad(dt_g, ((0, 120), (0, 0)))]
    wt_b = jnp.concatenate(bw, axis=0)

    def group_cols(full_xs, full_bc):
        parts = []
        for g in range(SSD_GROUPS):
            parts += [full_xs[:, 512 * g:512 * g + 512], full_bc[:, 128 * g:128 * g + 128], full_bc[:, 512 + 128 * g:512 + 128 * g + 128]]
        return jnp.concatenate(parts, axis=1)

    cw_g = group_cols(conv_w_full[:, 0:2048], conv_w_full[:, 2048:3072])
    cb_g = group_cols(conv_b[:, 0:2048], conv_b[:, 2048:3072])
    alog_e, dsk_e = _expand_heads(A_log), _expand_heads(D_skip)
    narrow = lambda v32: jnp.pad(v32.reshape(SSD_GROUPS, 8), ((0, 0), (0, 120))).reshape(1, SSD_GROUPS * 128)
    dtb_n, alog_n = narrow(dt_bias), narrow(A_log)

    pos_chunk = jnp.arange(SGU_BLOCK) // CHUNK
    smask = pos_chunk[None, :] <= pos_chunk[:, None]
    wm_f = jnp.where(smask[None], sgu_w[0], 0.0)
    wm = _c(wm_f)
    wmt = _c(jnp.transpose(wm_f, (0, 2, 1)))
    bias_full = jnp.repeat(sgu_b[0].T, D_MODEL // SGU_GROUPS, axis=1)
    fnw = final_norm_w.reshape(1, D_MODEL)

    proj_a = _mm(xn, wt_a, tm=tm_mm, tn=1024, tk=D_MODEL, name="in_proj_a", out_dtype=MXU_DTYPE, b_is_t=True)
    proj_b = _mm(xn, wt_b, tm=tm_mm, tn=BW_B, tk=D_MODEL, name="in_proj_b", b_is_t=True)
    y_ssd, y_b, hprev, pre_all, dt_all, acs_all = _ssd_fwd_call(proj_b, dtb_n, alog_n, dsk_e, cw_g, cb_g, ssd_norm_w, t_ssd, NG_SSD)
    y_a, merged, merged_t = _tok_fwd_call(proj_a, y_b, gate_b, sgu_norm_g, sgu_norm_b, wm, bias_full, t_tok)
    dh, dh_b, dmerged, loss_t, dfw8 = _out_call(merged, x2, tgt, w_out_full, fnw, t_out)

    dproj_a, dy_b, dgb8, dgam8, dbeta8, dbfull, dws = _tok_bwd_call(
        proj_a, dmerged, y_a, y_b, gate_b, sgu_norm_g, sgu_norm_b, wm, wmt, bias_full, t_tok)
    dproj_b, a512, a768 = _ssd_bwd_call(proj_b, pre_all, dt_all, acs_all, dy_b, y_ssd, hprev, dtb_n, alog_e, dsk_e, cw_g,
                                        ssd_norm_w, t_ssd, NG_SSD)
    dw_uvz = _mm(xnt, dproj_a, tm=D_MODEL, tn=1024, tk=min(2 * tk_dw, s), name="dw_in_uvz", n=6144)
    dw_gate = _mm(xnt, dproj_a, tm=D_MODEL, tn=1024, tk=min(2 * tk_dw, s), name="dw_in_gate", col0=6, n=4096)
    dw_zb, dw_xs, dw_bm, dw_cm, dw_dt = _dw_groups(xnt, dproj_b, tk=tk_dw)

    dw_dt32 = jnp.concatenate([dw_dt[:, 128 * g:128 * g + 8] for g in range(SSD_GROUPS)], axis=1)
    dw_ref = jnp.concatenate([dw_uvz, dw_zb, dw_xs, dw_bm, dw_cm, dw_dt32, dw_gate,
                              jnp.zeros((D_MODEL, 3 * WIN_STEP + WIN_W - IN_W), F32)], axis=1)
    dw_out_p, sib_i = _dw_out_rs_call(merged_t, dh_b, dw_ref, tn=1024, tk=tk_dw)
    p_out = dw_out_p.reshape(4, D_MODEL // 4, D_MODEL)

    s512 = jnp.sum(a512, axis=2)
    heads = lambda v: jnp.sum(v.reshape(32, HEADDIM), axis=1).reshape(1, 32)
    d_ssd_nw = s512[:, 0].reshape(1, D_MODEL)
    d_dskip = heads(s512[:, 1].reshape(D_MODEL))
    d_alog = heads(s512[:, 2].reshape(D_MODEL)) * (1.0 / HEADDIM) * (-jnp.exp(A_log))
    d_dtb = s512[:, 3, 0:8].reshape(1, 32)
    s768 = jnp.sum(a768, axis=2)
    ungroup = lambda v: jnp.concatenate([v[g, :, 0:512] for g in range(4)] + [v[g, :, 512:640] for g in range(4)]
                                        + [v[g, :, 640:768] for g in range(4)], axis=1)
    d_cw = ungroup(s768[:, 0:4])
    d_cb = ungroup(s768[:, 4:5])
    d_sgu_b = jnp.sum(dbfull.reshape(128, SGU_GROUPS, 128), axis=2).T.reshape(1, SGU_GROUPS, 128)
    d_sgu_w = jnp.where(smask[None], dws, 0.0).reshape(1, SGU_GROUPS, 128, 128)
    fold = lambda a8: jnp.sum(a8, axis=0, keepdims=True)
    small_local = [fold(dgb8), fold(dgam8), fold(dbeta8), d_sgu_w, d_sgu_b, d_cw, d_cb,
                   d_dtb, d_alog, d_dskip, d_ssd_nw, fold(dfw8).reshape(D_MODEL), jnp.sum(loss_t[:, 0, 0]).reshape(1)]
    small_shapes = [a.shape for a in small_local]
    v_local = _pack(small_local)

    core = lax.axis_index("c")
    place = jnp.stack([chip, core]).astype(jnp.int32)
    hr_i, hr_o = D_MODEL // 2, D_MODEL // 8
    sib_o, sib_v = _rs_sibling_call(p_out, v_local)
    s1b_i, o_i = _rs_add_windows(dw_ref, sib_i, place, 256, "rs_add_in")
    s1b_o, o_o = _rs_add(p_out, sib_o, place, 256, "rs_add_out")
    chip_v = _add_pair(v_local, sib_v, v_local.shape[0], "ar_add_small")
    dxn, r_i, r_o, abs_v = _dx_rs_call(dproj_a, wt_a, dproj_b, wt_b, s1b_i, s1b_o, chip_v, tm=tm_mm)
    grad_x, dnw8 = _gradx_call(x2, dxn, dh, norm_w, t_row)
    abs_v = lax.dynamic_update_index_in_dim(abs_v, chip_v, chip, 0)
    f_i = _sum_own_recv(o_i, r_i, 256, "rs_sum_in")
    f_o = _sum_own_recv(o_o, r_o, 256, "rs_sum_out")
    sib_f_i, g_w_out, all_nw = _rs_join_call(f_i, f_o, dnw8)
    g_w_out = lax.dynamic_update_slice_in_dim(g_w_out, f_o, core * hr_o, axis=0)
    all_nw = lax.dynamic_update_index_in_dim(all_nw, dnw8, 2 * chip + core, 0)
    g_nw = fold(_sum_slots(all_nw, "ar_sum_norm_w"))
    total_v = _sum_slots(abs_v, "ar_sum_small")
    (g_gb, g_gam, g_beta, g_sw, g_sb, g_cw_full, g_cb, g_dtb, g_alog, g_dsk, g_snw, g_fnw, loss1) = _unpack(total_v, small_shapes)
    g_cw_shard = lax.dynamic_slice(g_cw_full, (0, chip * 768), (4, 768)).reshape(1, 4, 768)
    loss = loss1.reshape(())

    shard_t = lambda win: lax.dynamic_slice_in_dim(win, 8 * chip, SHARD_W, axis=1).T
    g_w_in, d_win, nm_win, nv_win = (a.T for a in _adamw_halves(w_in[0].T, shard_t(f_i), shard_t(sib_f_i), m_w_in[0].T,
                                                                v_w_in[0].T, place, 296, "adamw_w_in"))
    d_wout, nm_wout, nv_wout = _adamw(w_out[0], g_w_out, m_w_out[0], v_w_out[0], 128, "adamw_w_out")
    small_w = [norm_w, gate_b, sgu_norm_g, sgu_norm_b, sgu_w, sgu_b, conv_w, conv_b, dt_bias, A_log, D_skip, ssd_norm_w, final_norm_w]
    small_m = [m_norm_w, m_gate_b, m_sgu_norm_g, m_sgu_norm_b, m_sgu_w, m_sgu_b, m_conv_w, m_conv_b, m_dt_bias, m_A_log, m_D_skip, m_ssd_norm_w, m_final_norm_w]
    small_v = [v_norm_w, v_gate_b, v_sgu_norm_g, v_sgu_norm_b, v_sgu_w, v_sgu_b, v_conv_w, v_conv_b, v_dt_bias, v_A_log, v_D_skip, v_ssd_norm_w, v_final_norm_w]
    small_g = [g_nw, g_gb, g_gam, g_beta, g_sw, g_sb, g_cw_shard, g_cb, g_dtb, g_alog, g_dsk, g_snw, g_fnw]
    shapes_w = [a.shape for a in small_w]
    small_g = [a.reshape(shp) for a, shp in zip(small_g, shapes_w)]
    pw = _pack(small_w)
    pd, pm, pv = _adamw(pw, _pack(small_g), _pack(small_m), _pack(small_v), pw.shape[0], "adamw_small")
    d_small, nm_small, nv_small = _unpack(pd, shapes_w), _unpack(pm, shapes_w), _unpack(pv, shapes_w)

    def with_big(small, win, wout):
        o = list(small)
        return o[0:1] + [win.reshape(1, D_MODEL, SHARD_W)] + o[1:12] + [wout.reshape(1, D_MODEL // 4, D_MODEL)] + o[12:13]

    grads = with_big(small_g, g_w_in, g_w_out)
    deltas = with_big(d_small, d_win, d_wout)
    new_m = with_big(nm_small, nm_win, nm_wout)
    new_v = with_big(nv_small, nv_win, nv_wout)
    return (loss, grad_x.reshape(1, s, D_MODEL), *grads, *deltas, *new_m, *new_v)
```

```python
import functools

import jax
import jax.numpy as jnp
from jax import lax
from jax.experimental import pallas as pl
from jax.experimental.pallas import tpu as pltpu

F32 = jnp.float32
MXU_DTYPE = jnp.bfloat16

D_MODEL = 2048
EPS = 1e-5
CHUNK = 64
SGU_BLOCK = 128
SGU_GROUPS = 16
SSD_GROUPS = 4
SSD_GW = 512
SSD_STATE = 128
HEADDIM = 64
IN_W = 15392
SHARD_W = IN_W // 4
BW_B = 1408
FW_B = BW_B
XBC_O, DT_O = 512, 1280
NA = 10240

ADAM_LR = 0.001
ADAM_B1 = 0.9
ADAM_B2 = 0.999
ADAM_EPS = 1e-08
ADAM_WD = 0.01
ADAM_STEP = 10

T_SSD = 256
NG_SSD = 2
T_TOK = 128
T_OUT = 256
T_ROW = 512
TM_MM = 1024
TK_DW = 1024
VMEM_CAP = 60 * 1024 * 1024
MESH = pl.DeviceIdType.MESH


def _cparams(sem, est_bytes):
    lim = int(min(VMEM_CAP, max(32 * 1024 * 1024, est_bytes + 12 * 1024 * 1024)))
    return pltpu.CompilerParams(dimension_semantics=sem, vmem_limit_bytes=lim)


def _c(x):
    return x.astype(MXU_DTYPE)


def _dot(a, b):
    return jnp.dot(a, b, preferred_element_type=F32)


def _dot_nt(a, b):
    return lax.dot_general(a, b, (((1,), (1,)), ((), ())), preferred_element_type=F32)


def _dot_tn(a, b):
    return lax.dot_general(a, b, (((0,), (0,)), ((), ())), preferred_element_type=F32)


def _split(x, n):
    parts, r = [], x
    for _ in range(n):
        p = _c(r)
        parts.append(p)
        r = r - p.astype(F32)
    return parts


def _dot01_l(m01, x, n):
    acc = None
    for p in _split(x, n):
        t = _dot(m01, p)
        acc = t if acc is None else acc + t
    return acc


def _dot01_r(x, m01, n):
    acc = None
    for p in _split(x, n):
        t = _dot(p, m01)
        acc = t if acc is None else acc + t
    return acc


def _sigmoid(x):
    return 1.0 / (1.0 + jnp.exp(-x))


def _fold8(x):
    r, w = x.shape
    return jnp.sum(x.reshape(r // 8, 8, w), axis=0)


def _iota(shape, dim):
    return lax.broadcasted_iota(jnp.int32, shape, dim)


def _ssd_masks():
    l64 = _iota((CHUNK, SSD_GW), 0)
    s64 = jnp.bitwise_and(_iota((CHUNK, SSD_GW), 1), CHUNK - 1)
    diag = l64 == s64
    causal = l64 >= s64
    row_last = l64 == CHUNK - 1
    r4 = lax.shift_right_logical(_iota((256, 256), 0), 6)
    c4 = lax.shift_right_logical(_iota((256, 256), 1), 6)
    mask4 = r4 == c4
    return diag, causal, row_last, mask4


def _cumsum_mats(t):
    r, c = _iota((t, t), 0), _iota((t, t), 1)
    same = lax.shift_right_logical(r, 6) == lax.shift_right_logical(c, 6)
    tri = _c(jnp.where(same, jnp.where(c <= r, 1.0, 0.0), 0.0))
    trit = _c(jnp.where(same, jnp.where(c >= r, 1.0, 0.0), 0.0))
    return tri, trit


def _head_expand_mat():
    return _c(jnp.where(_iota((128, SSD_GW), 0) == lax.shift_right_logical(_iota((128, SSD_GW), 1), 6), 1.0, 0.0))


def _ssd_common(xs, bm, cm, dt, acs, masks):
    diag, causal, row_last, mask4 = masks
    row_e = jnp.sum(jnp.where(diag, acs, 0.0), axis=0, keepdims=True)
    seg = acs - row_e
    lm = jnp.exp(jnp.where(causal, seg, -1e30))
    bb, cb = _c(bm), _c(cm)
    brep = jnp.concatenate([bb] * 8, axis=0)
    cbrep = _dot_nt(cb, brep)
    m = cbrep * lm
    xdt = xs * dt
    acs_last = jnp.sum(jnp.where(row_last, acs, 0.0), axis=0, keepdims=True)
    dec = jnp.exp(acs_last - acs)
    eacs = jnp.exp(acs)
    cd = jnp.exp(acs_last)
    return dict(lm=lm, bb=bb, cb=cb, brep=brep, m=m, xdt=xdt, dec=dec, eacs=eacs, cd=cd)


def _blockdiag4(xb, mask4):
    return jnp.where(mask4, jnp.concatenate([xb] * 4, axis=0), jnp.zeros((), xb.dtype))


def _ssd_chunk_fwd(xs, bm, cm, dt, acs, d_skip, ht, masks):
    q = _ssd_common(xs, bm, cm, dt, acs, masks)
    mask4 = masks[3]
    mb, xdtb = _c(q["m"]), _c(q["xdt"])
    yd = []
    for blk in range(2):
        sl = slice(256 * blk, 256 * blk + 256)
        yd.append(_dot(mb[:, sl], _blockdiag4(xdtb[:, sl], mask4)))
    y_diag = jnp.concatenate(yd, axis=1)
    p = _dot(q["cb"], _c(ht))
    y = y_diag + p * q["eacs"] + xs * d_skip
    st = _dot_tn(q["bb"], _c(q["xdt"] * q["dec"]))
    return y, ht * q["cd"] + st


def _ssd_chunk_bwd(xs, bm, cm, dt, acs, d_skip, hprev, dht, dy, masks):
    diag, causal, row_last, mask4 = masks
    q = _ssd_common(xs, bm, cm, dt, acs, masks)
    lm, bb, cb, brep, m, xdt, dec, eacs, cd = (q[k] for k in ("lm", "bb", "cb", "brep", "m", "xdt", "dec", "eacs", "cd"))
    hb = _c(hprev)
    yoff = _dot(cb, hb) * eacs
    dyb = _c(dy)
    dpb = _c(dy * eacs)
    d_c = _dot_nt(dpb, hb)
    dh_y = _dot_tn(cb, dpb)
    mb, xdtb = _c(m), _c(xdt)
    dm_parts, dxdt_parts = [], []
    for blk in range(2):
        sl = slice(256 * blk, 256 * blk + 256)
        bd = _blockdiag4(xdtb[:, sl], mask4)
        dm_parts.append(_dot_nt(dyb[:, sl], bd))
        dxf = jnp.where(mask4, _dot_tn(mb[:, sl], dyb[:, sl]), 0.0)
        dxdt_parts.append(dxf[0:64] + dxf[64:128] + dxf[128:192] + dxf[192:256])
    dm = jnp.concatenate(dm_parts, axis=1)
    dxdt = jnp.concatenate(dxdt_parts, axis=1)
    dcbb = _c(dm * lm)
    g = dm * m
    d_c = d_c + _dot(dcbb, brep)
    dbrep = _dot_tn(dcbb, cb)
    d_b = dbrep[0:64]
    for r in range(1, 8):
        d_b = d_b + dbrep[64 * r:64 * r + 64]
    dhtb = _c(dht)
    dxd = _dot(bb, dhtb)
    xd = xdt * dec
    dxdt = dxdt + dxd * dec
    tq = dxd * xd
    d_b = d_b + _dot_nt(_c(xd), dhtb)
    dcd = jnp.sum(dht * hprev, axis=0, keepdims=True)
    col_g = jnp.sum(g, axis=0, keepdims=True)
    last = jnp.sum(tq, axis=0, keepdims=True) + dcd * cd
    qq = g - jnp.where(diag, col_g, 0.0) + dy * yoff - tq + jnp.where(row_last, last, 0.0)
    dxs = dxdt * dt + dy * d_skip
    return dxs, d_b, d_c, dht * cd + dh_y, dy * xs, qq, dxdt * xs


def _ssd_finish_dt(qq, p1, dt, a_neg, trit, mask4):
    bd4 = _c(jnp.where(mask4, 1.0, 0.0))
    dacs = jnp.concatenate([_dot01_r(qq[:, 256 * b:256 * b + 256], bd4, 2) for b in range(2)], axis=1)
    da = _dot01_l(trit, dacs, 2)
    return p1 + da * (a_neg * (1.0 / HEADDIM)), da * dt


def _softplus(x):
    return jnp.maximum(x, 0.0) + jnp.log(1.0 + jnp.exp(-jnp.abs(x)))


def _conv_taps(xpad, t):
    taps = []
    for k in range(4):
        sh = 3 - k
        v = xpad if sh == 0 else pltpu.roll(xpad, sh, 0)
        taps.append(v[8:8 + t])
    return taps


def _mm(a, b, *, tm, tn, tk, name, out_dtype=F32, col0=0, n=None, b_is_t=False):
    m, k = a.shape
    n = b.shape[0 if b_is_t else 1] if n is None else n
    nk = k // tk
    assert m % tm == 0 and n % tn == 0 and k % tk == 0, (a.shape, b.shape, tm, tn, tk)
    dot = _dot_nt if b_is_t else _dot
    via_acc = nk > 1 and out_dtype != F32

    def body(a_ref, b_ref, o_ref, *acc):
        if nk == 1:
            o_ref[...] = dot(a_ref[...], b_ref[...]).astype(out_dtype)
            return
        acc_ref = acc[0] if via_acc else o_ref

        @pl.when(pl.program_id(2) == 0)
        def _():
            acc_ref[...] = jnp.zeros_like(acc_ref)

        acc_ref[...] += dot(a_ref[...], b_ref[...])
        if via_acc:
            @pl.when(pl.program_id(2) == nk - 1)
            def _():
                o_ref[...] = acc_ref[...].astype(out_dtype)

    isz = jnp.dtype(a.dtype).itemsize
    est = 2 * (tm * tk + tk * tn) * isz + 3 * tm * tn * 4
    return pl.pallas_call(
        body, name=name,
        grid=(m // tm, n // tn, nk),
        in_specs=[pl.BlockSpec((tm, tk), lambda i, j, kk: (i, kk)),
                  pl.BlockSpec((tn, tk), lambda i, j, kk: (j + col0, kk)) if b_is_t
                  else pl.BlockSpec((tk, tn), lambda i, j, kk: (kk, j + col0))],
        out_specs=pl.BlockSpec((tm, tn), lambda i, j, kk: (i, j)),
        out_shape=jax.ShapeDtypeStruct((m, n), out_dtype),
        scratch_shapes=[pltpu.VMEM((tm, tn), F32)] if via_acc else [],
        compiler_params=_cparams(("parallel", "parallel", "arbitrary"), est),
    )(a, b)


def _dw_groups(xnt, dpb, *, tk, out_dtype):
    m, k = xnt.shape
    nk = k // tk

    def body(a_ref, b_ref, zb_ref, xs_ref, bm_ref, cm_ref, dt_ref, acc_ref):
        @pl.when(pl.program_id(1) == 0)
        def _():
            acc_ref[...] = jnp.zeros_like(acc_ref)

        acc_ref[...] += _dot(a_ref[...], b_ref[...])

        @pl.when(pl.program_id(1) == nk - 1)
        def _():
            for o_ref, lo, hi in ((zb_ref, 0, 512), (xs_ref, 512, 1024), (bm_ref, 1024, 1152), (cm_ref, 1152, 1280),
                                  (dt_ref, 1280, 1408)):
                o_ref[...] = acc_ref[:, lo:hi].astype(out_dtype)

    isz = jnp.dtype(xnt.dtype).itemsize
    est = 2 * (m * tk + tk * BW_B) * isz + 3 * m * BW_B * 4
    piece = lambda w: pl.BlockSpec((m, w), lambda g, kk: (0, g))
    return pl.pallas_call(
        body, name="dw_in_b",
        grid=(SSD_GROUPS, nk),
        in_specs=[pl.BlockSpec((m, tk), lambda g, kk: (0, kk)), pl.BlockSpec((tk, BW_B), lambda g, kk: (kk, g))],
        out_specs=[piece(512), piece(512), piece(128), piece(128), piece(128)],
        out_shape=[jax.ShapeDtypeStruct((m, w), out_dtype) for w in (2048, 2048, 512, 512, 512)],
        scratch_shapes=[pltpu.VMEM((m, BW_B), F32)],
        compiler_params=_cparams(("parallel", "arbitrary"), est),
    )(xnt, dpb)


def _dx_rs_call(dpa, wta, dpb, wtb, sb_in, sb_out, chip_v, *, tm):
    s = dpa.shape[0]
    tka, tkb = 1024, BW_B
    nka, nkb = dpa.shape[1] // tka, dpb.shape[1] // tkb
    ni, nk = s // tm, nka + nkb

    def body(a_ref, wa_ref, b_ref, wb_ref, sbin, sbout, cv, o_ref, rc_in, rc_out, abs_v, send, recv):
        i, kk = pl.program_id(0), pl.program_id(1)

        def copies():
            x, y, c, me, others = _place()
            sends, recvs = [], []
            for j, chip in enumerate(others):
                kj = 2 * chip[0] + chip[1]
                to = (*chip, c)
                sends += [_remote(sbin.at[kj], rc_in.at[j], send, recv, j, to),
                          _remote(sbout.at[kj], rc_out.at[j], send, recv, 3 + j, to),
                          _remote(cv, abs_v.at[me], send, recv, 6 + j, to)]
                recvs += [sends[-3], sends[-2], _remote(cv, abs_v.at[kj], send, recv, 6 + j, to)]
            return sends, recvs

        @pl.when((i == 0) & (kk == 0))
        def _():
            for cp in copies()[0]:
                cp.start()

        @pl.when(kk == 0)
        def _():
            o_ref[...] = jnp.zeros_like(o_ref)

        @pl.when(kk < nka)
        def _():
            o_ref[...] += _dot(a_ref[...], wa_ref[...])

        @pl.when(kk >= nka)
        def _():
            o_ref[...] += _dot(b_ref[...], wb_ref[...])

        @pl.when((i == ni - 1) & (kk == nk - 1))
        def _():
            sends, recvs = copies()
            for cp in recvs:
                cp.wait_recv()
            for cp in sends:
                cp.wait_send()

    isz = jnp.dtype(dpa.dtype).itemsize
    est = 2 * isz * (tm * tka + tka * D_MODEL + tm * tkb + tkb * D_MODEL) + 2 * tm * D_MODEL * 4
    outs = [jax.ShapeDtypeStruct((s, D_MODEL), F32),
            jax.ShapeDtypeStruct((3,) + sb_in.shape[1:], sb_in.dtype), jax.ShapeDtypeStruct((3,) + sb_out.shape[1:], sb_out.dtype),
            jax.ShapeDtypeStruct((4,) + chip_v.shape, F32)]
    return pl.pallas_call(
        body, name="dx_matmul_rs_chips",
        grid=(ni, nk),
        in_specs=[
            pl.BlockSpec((tm, tka), lambda i, kk: (i, jnp.minimum(kk, nka - 1))),
            pl.BlockSpec((tka, D_MODEL), lambda i, kk: (jnp.minimum(kk, nka - 1), 0)),
            pl.BlockSpec((tm, tkb), lambda i, kk: (i, jnp.maximum(kk - nka, 0))),
            pl.BlockSpec((tkb, D_MODEL), lambda i, kk: (jnp.maximum(kk - nka, 0), 0)),
            ANY, ANY, ANY,
        ],
        out_specs=[pl.BlockSpec((tm, D_MODEL), lambda i, kk: (i, 0)), ANY, ANY, ANY],
        out_shape=outs,
        scratch_shapes=[pltpu.SemaphoreType.DMA((9,)), pltpu.SemaphoreType.DMA((9,))],
        compiler_params=_cparams(("arbitrary", "arbitrary"), est),
    )(dpa, wta, dpb, wtb, sb_in, sb_out, chip_v)


def _gradx_call(x, dxn, dh, norm_w, tm):
    s = x.shape[0]

    def body(x_ref, g_ref, dh_ref, w_ref, gx_ref, dw_ref):
        @pl.when(pl.program_id(0) == 0)
        def _():
            dw_ref[...] = jnp.zeros_like(dw_ref)

        xv, gv = x_ref[...], g_ref[...]
        r = lax.rsqrt(jnp.mean(xv * xv, axis=-1, keepdims=True) + EPS)
        gw = gv * w_ref[...]
        gx_ref[...] = r * gw - xv * (r * r * r) * jnp.mean(xv * gw, axis=-1, keepdims=True) + dh_ref[...]
        dw_ref[...] += _fold8(gv * (xv * r))

    row = pl.BlockSpec((tm, D_MODEL), lambda i: (i, 0))
    return pl.pallas_call(
        body, name="grad_x",
        grid=(s // tm,),
        in_specs=[row, row, row, pl.BlockSpec((1, D_MODEL), lambda i: (0, 0))],
        out_specs=[row, pl.BlockSpec((8, D_MODEL), lambda i: (0, 0))],
        out_shape=[jax.ShapeDtypeStruct((s, D_MODEL), F32), jax.ShapeDtypeStruct((8, D_MODEL), F32)],
        compiler_params=_cparams(("arbitrary",), 2 * tm * D_MODEL * 16),
    )(x, dxn, dh, norm_w)


def _layernorm_stats(v):
    mu = jnp.mean(v, axis=-1, keepdims=True)
    vc = v - mu
    var = jnp.mean(vc * vc, axis=-1, keepdims=True)
    return vc * lax.rsqrt(var + EPS), lax.rsqrt(var + EPS)


def _tok_fwd_call(proj_a, y_b, gate_b, sgu_g, sgu_beta, wm, bias_full, t):
    s = proj_a.shape[0]

    def body(pa_ref, yb_ref, gb_ref, g_ref, be_ref, wm_ref, bf_ref, ya_ref, mg_ref, mgt_ref, mix_ref):
        u = pa_ref[:, 0:2048].astype(F32)
        v = pa_ref[:, 2048:4096].astype(F32)
        za = pa_ref[:, 4096:6144].astype(F32)
        xhat, _ = _layernorm_stats(v)
        vnb = _c(xhat * g_ref[...] + be_ref[...])
        for gi in range(SGU_GROUPS):
            sl = slice(128 * gi, 128 * gi + 128)
            mix_ref[:, sl] = _dot(wm_ref[gi], vnb[:, sl])
        mixed = mix_ref[...] + bf_ref[...]
        y_a = u * mixed * (za * _sigmoid(za))
        g0 = _sigmoid(pa_ref[:, 6144:8192].astype(F32) + gb_ref[:, 0:2048])
        g1 = _sigmoid(pa_ref[:, 8192:10240].astype(F32) + gb_ref[:, 2048:4096])
        merged = g0 * y_a + g1 * yb_ref[...].astype(F32)
        ya_ref[...] = _c(y_a)
        mg_ref[...] = _c(merged)
        mgt_ref[...] = _c(merged.T)

    row = pl.BlockSpec((t, D_MODEL), lambda i: (i, 0))
    vec = lambda w: pl.BlockSpec((1, w), lambda i: (0, 0))
    return pl.pallas_call(
        body, name="tok_fwd",
        grid=(s // t,),
        in_specs=[pl.BlockSpec((t, NA), lambda i: (i, 0)), row, vec(4096), vec(2048), vec(2048),
                  pl.BlockSpec((SGU_GROUPS, 128, 128), lambda i: (0, 0, 0)), pl.BlockSpec((128, D_MODEL), lambda i: (0, 0))],
        out_specs=[row, row, pl.BlockSpec((D_MODEL, t), lambda i: (0, i))],
        out_shape=[jax.ShapeDtypeStruct((s, D_MODEL), MXU_DTYPE), jax.ShapeDtypeStruct((s, D_MODEL), MXU_DTYPE),
                   jax.ShapeDtypeStruct((D_MODEL, s), MXU_DTYPE)],
        scratch_shapes=[pltpu.VMEM((t, D_MODEL), F32)],
        compiler_params=_cparams(("parallel",), 2 * t * NA * 4 + 12 * t * D_MODEL * 4),
    )(proj_a, y_b, gate_b, sgu_g, sgu_beta, wm, bias_full)


def _tok_bwd_call(proj_a, dmerged, y_a, y_b, gate_b, sgu_g, sgu_beta, wm, wmt, bias_full, t):
    s = proj_a.shape[0]

    def body(pa_ref, dm_ref, ya_ref, yb_ref, gb_ref, g_ref, be_ref, wm_ref, wmt_ref, bf_ref,
             dpa_ref, dyb_ref, dgb_ref, dgam_ref, dbeta_ref, dbf_ref, dws_ref, mix_ref, dvn_ref):
        @pl.when(pl.program_id(0) == 0)
        def _():
            dgb_ref[...] = jnp.zeros_like(dgb_ref)
            dgam_ref[...] = jnp.zeros_like(dgam_ref)
            dbeta_ref[...] = jnp.zeros_like(dbeta_ref)
            dbf_ref[...] = jnp.zeros_like(dbf_ref)
            dws_ref[...] = jnp.zeros_like(dws_ref)

        u = pa_ref[:, 0:2048].astype(F32)
        v = pa_ref[:, 2048:4096].astype(F32)
        za = pa_ref[:, 4096:6144].astype(F32)
        xhat, rstd = _layernorm_stats(v)
        vnb = _c(xhat * g_ref[...] + be_ref[...])
        for gi in range(SGU_GROUPS):
            sl = slice(128 * gi, 128 * gi + 128)
            mix_ref[:, sl] = _dot(wm_ref[gi], vnb[:, sl])
        mixed = mix_ref[...] + bf_ref[...]
        sig = _sigmoid(za)
        sz = za * sig
        dm = dm_ref[...].astype(F32)
        y_a = ya_ref[...].astype(F32)
        g0 = _sigmoid(pa_ref[:, 6144:8192].astype(F32) + gb_ref[:, 0:2048])
        g1 = _sigmoid(pa_ref[:, 8192:10240].astype(F32) + gb_ref[:, 2048:4096])
        dgl0 = dm * y_a * g0 * (1.0 - g0)
        dgl1 = dm * yb_ref[...].astype(F32) * g1 * (1.0 - g1)
        dyb_ref[...] = _c(dm * g1)
        dya = dm * g0
        dpa_ref[:, 6144:8192] = _c(dgl0)
        dpa_ref[:, 8192:10240] = _c(dgl1)
        dgb_ref[:, 0:2048] += _fold8(dgl0)
        dgb_ref[:, 2048:4096] += _fold8(dgl1)
        dpa_ref[:, 0:2048] = _c(dya * mixed * sz)
        dpa_ref[:, 4096:6144] = _c(dya * (u * mixed) * (sig * (1.0 + za * (1.0 - sig))))
        dmixed = dya * u * sz
        dbf_ref[...] += dmixed
        dmb = _c(dmixed)
        for gi in range(SGU_GROUPS):
            sl = slice(128 * gi, 128 * gi + 128)
            dvn_ref[:, sl] = _dot(wmt_ref[gi], dmb[:, sl])
            dws_ref[gi] += _dot_nt(dmb[:, sl], vnb[:, sl])
        dvn = dvn_ref[...]
        dgam_ref[...] += _fold8(dvn * xhat)
        dbeta_ref[...] += _fold8(dvn)
        dxh = dvn * g_ref[...]
        dv = rstd * (dxh - jnp.mean(dxh, axis=-1, keepdims=True) - xhat * jnp.mean(dxh * xhat, axis=-1, keepdims=True))
        dpa_ref[:, 2048:4096] = _c(dv)

    row = pl.BlockSpec((t, D_MODEL), lambda i: (i, 0))
    vec = lambda w: pl.BlockSpec((1, w), lambda i: (0, 0))
    acc = lambda w: pl.BlockSpec((8, w), lambda i: (0, 0))
    wspec = pl.BlockSpec((SGU_GROUPS, 128, 128), lambda i: (0, 0, 0))
    return pl.pallas_call(
        body, name="tok_bwd",
        grid=(s // t,),
        in_specs=[pl.BlockSpec((t, NA), lambda i: (i, 0)), row, row, row, vec(4096), vec(2048), vec(2048),
                  wspec, wspec, pl.BlockSpec((128, D_MODEL), lambda i: (0, 0))],
        out_specs=[pl.BlockSpec((t, NA), lambda i: (i, 0)), row, acc(4096), acc(2048), acc(2048),
                   pl.BlockSpec((128, D_MODEL), lambda i: (0, 0)), wspec],
        out_shape=[jax.ShapeDtypeStruct((s, NA), MXU_DTYPE), jax.ShapeDtypeStruct((s, D_MODEL), MXU_DTYPE),
                   jax.ShapeDtypeStruct((8, 4096), F32), jax.ShapeDtypeStruct((8, 2048), F32),
                   jax.ShapeDtypeStruct((8, 2048), F32), jax.ShapeDtypeStruct((128, D_MODEL), F32),
                   jax.ShapeDtypeStruct((SGU_GROUPS, 128, 128), F32)],
        scratch_shapes=[pltpu.VMEM((t, D_MODEL), F32), pltpu.VMEM((t, D_MODEL), F32)],
        compiler_params=_cparams(("arbitrary",), 2 * t * NA * 6 + 16 * t * D_MODEL * 4),
    )(proj_a, dmerged, y_a, y_b, gate_b, sgu_g, sgu_beta, wm, wmt, bias_full)


def _out_call(merged, x, target, w_out, fnw, t):
    s = x.shape[0]
    nt = s // t

    def body(mg_ref, x_ref, t_ref, w_ref, fw_ref, dh_ref, dhb_ref, dmg_ref, loss_ref, dfw_ref):
        @pl.when(pl.program_id(0) == 0)
        def _():
            dfw_ref[...] = jnp.zeros_like(dfw_ref)

        h = x_ref[...] + _dot(mg_ref[...], w_ref[...])
        r = lax.rsqrt(jnp.mean(h * h, axis=-1, keepdims=True) + EPS)
        hn = h * r
        err = hn * fw_ref[...] - t_ref[...]
        loss_ref[...] = jnp.full(loss_ref.shape, 0.5 * jnp.sum(jnp.mean(err * err, axis=-1, keepdims=True)), F32)
        dy = err * (1.0 / D_MODEL)
        dfw_ref[...] += _fold8(dy * hn)
        gw = dy * fw_ref[...]
        dh = r * gw - h * (r * r * r) * jnp.mean(h * gw, axis=-1, keepdims=True)
        dh_ref[...] = dh
        dhb = _c(dh)
        dhb_ref[...] = dhb
        dmg_ref[...] = _c(_dot_nt(dhb, w_ref[...]))

    row = pl.BlockSpec((t, D_MODEL), lambda i: (i, 0))
    return pl.pallas_call(
        body, name="out_proj_loss",
        grid=(nt,),
        in_specs=[row, row, row, pl.BlockSpec((D_MODEL, D_MODEL), lambda i: (0, 0)), pl.BlockSpec((1, D_MODEL), lambda i: (0, 0))],
        out_specs=[row, row, row, pl.BlockSpec((1, 8, 128), lambda i: (i, 0, 0)), pl.BlockSpec((8, D_MODEL), lambda i: (0, 0))],
        out_shape=[jax.ShapeDtypeStruct((s, D_MODEL), F32), jax.ShapeDtypeStruct((s, D_MODEL), MXU_DTYPE),
                   jax.ShapeDtypeStruct((s, D_MODEL), MXU_DTYPE), jax.ShapeDtypeStruct((nt, 8, 128), F32),
                   jax.ShapeDtypeStruct((8, D_MODEL), F32)],
        compiler_params=_cparams(("arbitrary",), 2 * D_MODEL * D_MODEL * 2 + 2 * t * D_MODEL * 24),
    )(merged, x, target, w_out, fnw)


def _ssd_fwd_call(proj_b, dtb, alog, dsk, cw, cb, nw, t, ng):
    s = proj_b.shape[0]
    nt, nch = s // t, t // CHUNK

    def body(pb_ref, halo_ref, dtb_ref, al_ref, ds_ref, cw_ref, cb_ref, nw_ref, y_ref, yb_ref, hp_ref, pre_ref,
             dt_ref, acs_ref, ht_ref, prev_ref):
        i = pl.program_id(1)

        @pl.when(i == 0)
        def _():
            ht_ref[...] = jnp.zeros_like(ht_ref)

        for gi in range(ng):
            prev_ref[:, 768 * gi:768 * gi + 768] = jnp.where(i == 0, 0.0, halo_ref[:, FW_B * gi + XBC_O:FW_B * gi + DT_O])
        masks = _ssd_masks()
        tri, _ = _cumsum_mats(t)
        a_neg = -jnp.exp(al_ref[...])
        expand = _head_expand_mat()
        for gi in range(ng):
            fo, go, no = FW_B * gi, SSD_GW * gi, 128 * gi
            dt_n = _softplus(pb_ref[:, fo + DT_O:fo + DT_O + 128] + dtb_ref[:, no:no + 128])
            acs_n = _dot01_l(tri, dt_n * a_neg[:, no:no + 128], 3)
            dt_ref[:, go:go + 512] = _dot01_r(dt_n, expand, 2)
            acs_ref[:, go:go + 512] = _dot01_r(acs_n, expand, 2)

        def chunk(c, carry):
            rows = pl.ds(pl.multiple_of(c * CHUNK, CHUNK), CHUNK)
            for gi in range(ng):
                fo, co, go, no = FW_B * gi, 768 * gi, SSD_GW * gi, 128 * gi
                xbc = pb_ref[rows, fo + XBC_O:fo + DT_O]
                taps = _conv_taps(jnp.concatenate([prev_ref[:, co:co + 768], xbc], axis=0), CHUNK)
                prev_ref[:, co:co + 768] = xbc[CHUNK - 8:CHUNK]
                pre = cb_ref[:, co:co + 768]
                for k in range(4):
                    pre = pre + taps[k] * cw_ref[k:k + 1, co:co + 768]
                pre_ref[rows, co:co + 768] = pre
                act = pre * _sigmoid(pre)
                dt = dt_ref[rows, go:go + 512]
                acs = acs_ref[rows, go:go + 512]
                ht = ht_ref[gi]
                hp_ref[c, :, go:go + 512] = ht
                y, ht_new = _ssd_chunk_fwd(act[:, 0:512], act[:, 512:640], act[:, 640:768], dt, acs,
                                           ds_ref[:, go:go + 512], ht, masks)
                y_ref[rows, go:go + 512] = y
                ht_ref[gi] = ht_new
                zb = pb_ref[rows, fo:fo + 512]
                hh = y * (zb * _sigmoid(zb))
                rr = lax.rsqrt(jnp.mean(hh * hh, axis=-1, keepdims=True) + EPS)
                yb_ref[rows, go:go + 512] = _c(hh * rr * nw_ref[:, go:go + 512])
            return carry

        lax.fori_loop(0, nch, chunk, 0)

    gvec = lambda w: pl.BlockSpec((1, ng * w), lambda g, i: (0, g))
    return pl.pallas_call(
        body, name="ssd_fwd",
        grid=(SSD_GROUPS // ng, nt),
        in_specs=[pl.BlockSpec((t, ng * FW_B), lambda g, i: (i, g)),
                  pl.BlockSpec((8, ng * FW_B), lambda g, i: (jnp.maximum(i * (t // 8) - 1, 0), g)),
                  gvec(128), gvec(128), gvec(512),
                  pl.BlockSpec((4, ng * 768), lambda g, i: (0, g)), gvec(768), gvec(512)],
        out_specs=[pl.BlockSpec((t, ng * SSD_GW), lambda g, i: (i, g)), pl.BlockSpec((t, ng * SSD_GW), lambda g, i: (i, g)),
                   pl.BlockSpec((nch, SSD_STATE, ng * SSD_GW), lambda g, i: (i, 0, g)),
                   pl.BlockSpec((t, ng * 768), lambda g, i: (i, g)),
                   pl.BlockSpec((t, ng * SSD_GW), lambda g, i: (i, g)), pl.BlockSpec((t, ng * SSD_GW), lambda g, i: (i, g))],
        out_shape=[jax.ShapeDtypeStruct((s, D_MODEL), F32), jax.ShapeDtypeStruct((s, D_MODEL), MXU_DTYPE),
                   jax.ShapeDtypeStruct((s // CHUNK, SSD_STATE, D_MODEL), F32),
                   jax.ShapeDtypeStruct((s, SSD_GROUPS * 768), F32),
                   jax.ShapeDtypeStruct((s, D_MODEL), F32), jax.ShapeDtypeStruct((s, D_MODEL), F32)],
        scratch_shapes=[pltpu.VMEM((ng, SSD_STATE, SSD_GW), F32), pltpu.VMEM((8, ng * 768), F32)],
        compiler_params=_cparams(("parallel", "arbitrary"), ng * (2 * t * FW_B * 4 + 16 * t * SSD_GW * 4) + 16 * 1024 * 1024),
    )(proj_b, proj_b, dtb, alog, dsk, cw, cb, nw)


def _ssd_bwd_call(proj_b, pre_all, dt_all, acs_all, dyb, y, hprev, dtb, alog, dsk, cw, nw, t, ng):
    s = proj_b.shape[0]
    nt, nch = s // t, t // CHUNK

    def body(pb_ref, pre_ref, dt_ref, acs_ref, dyb_ref, y_ref, hp_ref, dtb_ref, al_ref, ds_ref, cw_ref, nw_ref,
             dpb_ref, a512_ref, a768_ref, dht_ref, nxt_ref, q_ref, p1_ref):
        i = pl.program_id(1)

        @pl.when(i == 0)
        def _():
            dht_ref[...] = jnp.zeros_like(dht_ref)
            nxt_ref[...] = jnp.zeros_like(nxt_ref)
            a512_ref[...] = jnp.zeros_like(a512_ref)
            a768_ref[...] = jnp.zeros_like(a768_ref)

        _, trit = _cumsum_mats(t)
        a_neg = -jnp.exp(al_ref[...])
        masks = _ssd_masks()

        def chunk(cc, carry):
            c = nch - 1 - cc
            rows = pl.ds(pl.multiple_of(c * CHUNK, CHUNK), CHUNK)
            for gi in range(ng):
                fo, co, go, bo = FW_B * gi, 768 * gi, SSD_GW * gi, BW_B * gi
                pre = pre_ref[rows, co:co + 768]
                sp = _sigmoid(pre)
                act = pre * sp
                zb = pb_ref[rows, fo:fo + 512]
                yv = y_ref[rows, go:go + 512]
                sgz = _sigmoid(zb)
                sz = zb * sgz
                hh = yv * sz
                rr = lax.rsqrt(jnp.mean(hh * hh, axis=-1, keepdims=True) + EPS)
                dyb = dyb_ref[rows, go:go + 512].astype(F32)
                a512_ref[gi, 0] += _fold8(dyb * (hh * rr))
                tt = dyb * nw_ref[:, go:go + 512]
                dhh = rr * tt - hh * (rr * rr * rr) * jnp.mean(hh * tt, axis=-1, keepdims=True)
                dpb_ref[rows, bo:bo + 512] = _c(dhh * yv * (sgz * (1.0 + zb * (1.0 - sgz))))
                dxs, d_b, d_c, dht_prev, dyxs, qq, p1 = _ssd_chunk_bwd(
                    act[:, 0:512], act[:, 512:640], act[:, 640:768], dt_ref[rows, go:go + 512], acs_ref[rows, go:go + 512],
                    ds_ref[:, go:go + 512], hp_ref[c, :, go:go + 512], dht_ref[gi], dhh * sz, masks)
                dht_ref[gi] = dht_prev
                q_ref[rows, go:go + 512] = qq
                p1_ref[rows, go:go + 512] = p1
                a512_ref[gi, 1] += _fold8(dyxs)
                dpre = jnp.concatenate([dxs, d_b, d_c], axis=1) * (sp * (1.0 + pre * (1.0 - sp)))
                xbc = pb_ref[rows, fo + XBC_O:fo + DT_O]
                a768_ref[gi, 4] += _fold8(dpre)
                a768_ref[gi, 3] += _fold8(dpre * xbc)
                dpad = jnp.concatenate([dpre, nxt_ref[:, co:co + 768]], axis=0)
                dx = dpre * cw_ref[3:4, co:co + 768]
                for k in range(3):
                    d_k = pltpu.roll(dpad, CHUNK + 8 - (3 - k), 0)[0:CHUNK]
                    dx = dx + d_k * cw_ref[k:k + 1, co:co + 768]
                    a768_ref[gi, k] += _fold8(d_k * xbc)
                nxt_ref[:, co:co + 768] = dpre[0:8]
                dpb_ref[rows, bo + 512:bo + 1280] = _c(dx)
            return carry

        lax.fori_loop(0, nch, chunk, 0)

        rsel = _c(jnp.where(lax.shift_right_logical(_iota((SSD_GW, 128), 0), 6) == _iota((SSD_GW, 128), 1), 1.0, 0.0))
        for gi in range(ng):
            fo, go, bo, no = FW_B * gi, SSD_GW * gi, BW_B * gi, 128 * gi
            ddt, dadt = _ssd_finish_dt(q_ref[:, go:go + 512], p1_ref[:, go:go + 512], dt_ref[:, go:go + 512],
                                       a_neg[:, go:go + 512], trit, masks[3])
            sig_n = _sigmoid(pb_ref[:, fo + DT_O:fo + DT_O + 128] + dtb_ref[:, no:no + 128])
            ddtr_n = _dot01_r(ddt, rsel, 2) * sig_n
            dpb_ref[:, bo + DT_O:bo + DT_O + 128] = _c(ddtr_n)
            a512_ref[gi, 2] += _fold8(dadt)
            a512_ref[gi, 3, :, 0:128] += _fold8(ddtr_n)

    gvec = lambda w: pl.BlockSpec((1, ng * w), lambda g, i: (0, g))
    rev = lambda w: pl.BlockSpec((t, ng * w), lambda g, i: (nt - 1 - i, g))
    return pl.pallas_call(
        body, name="ssd_bwd",
        grid=(SSD_GROUPS // ng, nt),
        in_specs=[rev(FW_B), rev(768), rev(SSD_GW), rev(SSD_GW), rev(SSD_GW), rev(SSD_GW),
                  pl.BlockSpec((nch, SSD_STATE, ng * SSD_GW), lambda g, i: (nt - 1 - i, 0, g)),
                  gvec(128), gvec(512), gvec(512),
                  pl.BlockSpec((4, ng * 768), lambda g, i: (0, g)), gvec(512)],
        out_specs=[rev(BW_B),
                   pl.BlockSpec((ng, 4, 8, 512), lambda g, i: (g, 0, 0, 0)),
                   pl.BlockSpec((ng, 5, 8, 768), lambda g, i: (g, 0, 0, 0))],
        out_shape=[jax.ShapeDtypeStruct((s, SSD_GROUPS * BW_B), MXU_DTYPE),
                   jax.ShapeDtypeStruct((SSD_GROUPS, 4, 8, 512), F32),
                   jax.ShapeDtypeStruct((SSD_GROUPS, 5, 8, 768), F32)],
        scratch_shapes=[pltpu.VMEM((ng, SSD_STATE, SSD_GW), F32), pltpu.VMEM((8, ng * 768), F32),
                        pltpu.VMEM((t, ng * SSD_GW), F32), pltpu.VMEM((t, ng * SSD_GW), F32)],
        compiler_params=_cparams(("parallel", "arbitrary"), ng * (2 * t * FW_B * 4 + 18 * t * SSD_GW * 4) + 16 * 1024 * 1024),
    )(proj_b, pre_all, dt_all, acs_all, dyb, y, hprev, dtb, alog, dsk, cw, nw)


def _rows_call(body, ins, outs, tr, name):
    r = ins[0].shape[0]
    spec = lambda a: pl.BlockSpec((tr, a.shape[1]), lambda i: (i, 0))
    est = 2 * tr * sum(a.shape[1] * jnp.dtype(a.dtype).itemsize for a in list(ins) + list(outs))
    return pl.pallas_call(
        body, name=name, grid=(r // tr,),
        in_specs=[spec(a) for a in ins], out_specs=[spec(o) for o in outs], out_shape=list(outs),
        compiler_params=_cparams(("parallel",), est),
    )(*ins)


def _add_pair(a, b, tr, name):
    def body(a_ref, b_ref, o_ref):
        o_ref[...] = a_ref[...] + b_ref[...]

    return _rows_call(body, [a, b], [jax.ShapeDtypeStruct(a.shape, F32)], tr, name)[0]


def _rs_add(p, sib, place, tr, name):
    _, r, c = p.shape
    half = r // 2
    nb = half // tr

    def body(pl_ref, p_ref, s_ref, b_ref, own_ref):
        v = p_ref[0] + s_ref[0]
        b_ref[0] = v.astype(jnp.bfloat16)

        @pl.when(pl.program_id(1) == pl_ref[0])
        def _():
            own_ref[...] = v

    return pl.pallas_call(
        body, name=name,
        grid_spec=pltpu.PrefetchScalarGridSpec(
            num_scalar_prefetch=1, grid=(nb, 4),
            in_specs=[pl.BlockSpec((1, tr, c), lambda i, k, pr: (k, pr[1] * nb + i, 0)),
                      pl.BlockSpec((1, tr, c), lambda i, k, pr: (k, i, 0))],
            out_specs=[pl.BlockSpec((1, tr, c), lambda i, k, pr: (k, i, 0)),
                       pl.BlockSpec((tr, c), lambda i, k, pr: (i, 0))]),
        out_shape=[jax.ShapeDtypeStruct((4, half, c), jnp.bfloat16), jax.ShapeDtypeStruct((half, c), F32)],
        compiler_params=_cparams(("parallel", "arbitrary"), 2 * tr * c * 14),
    )(place, p, sib)


WIN_STEP = 3840
WIN_W = 3968


def _rs_add_windows(dw, sib, place, tr, name):
    r = dw.shape[0]
    half = r // 2
    nb = half // tr
    tail = WIN_W - WIN_STEP

    def body(pl_ref, pm_ref, pt_ref, s_ref, b_ref, own_ref):
        vm = pm_ref[...].astype(F32) + s_ref[0, :, 0:WIN_STEP].astype(F32)
        vt = pt_ref[...].astype(F32) + s_ref[0, :, WIN_STEP:WIN_W].astype(F32)
        b_ref[0, :, 0:WIN_STEP] = vm.astype(jnp.bfloat16)
        b_ref[0, :, WIN_STEP:WIN_W] = vt.astype(jnp.bfloat16)

        @pl.when(pl.program_id(1) == pl_ref[0])
        def _():
            own_ref[:, 0:WIN_STEP] = vm
            own_ref[:, WIN_STEP:WIN_W] = vt

    return pl.pallas_call(
        body, name=name,
        grid_spec=pltpu.PrefetchScalarGridSpec(
            num_scalar_prefetch=1, grid=(nb, 4),
            in_specs=[pl.BlockSpec((tr, WIN_STEP), lambda i, k, pr: (pr[1] * nb + i, k)),
                      pl.BlockSpec((tr, tail), lambda i, k, pr: (pr[1] * nb + i, (WIN_STEP // tail) * (k + 1))),
                      pl.BlockSpec((1, tr, WIN_W), lambda i, k, pr: (k, i, 0))],
            out_specs=[pl.BlockSpec((1, tr, WIN_W), lambda i, k, pr: (k, i, 0)),
                       pl.BlockSpec((tr, WIN_W), lambda i, k, pr: (i, 0))]),
        out_shape=[jax.ShapeDtypeStruct((4, half, WIN_W), jnp.bfloat16), jax.ShapeDtypeStruct((half, WIN_W), F32)],
        compiler_params=_cparams(("parallel", "arbitrary"), 2 * tr * WIN_W * 14),
    )(place, dw, dw, sib)


def _sum_own_recv(own, recv, tr, name):
    r, c = own.shape

    def body(o_ref, r_ref, out_ref):
        v = o_ref[...]
        for j in range(3):
            v = v + r_ref[j].astype(F32)
        out_ref[...] = v

    return pl.pallas_call(
        body, name=name, grid=(r // tr,),
        in_specs=[pl.BlockSpec((tr, c), lambda i: (i, 0)), pl.BlockSpec((3, tr, c), lambda i: (0, i, 0))],
        out_specs=pl.BlockSpec((tr, c), lambda i: (i, 0)),
        out_shape=jax.ShapeDtypeStruct((r, c), F32),
        compiler_params=_cparams(("parallel",), 2 * tr * c * 14),
    )(own, recv)


def _sum_slots(stack, name):
    n, r, w = stack.shape

    def body(a_ref, out_ref):
        v = a_ref[0]
        for k in range(1, n):
            v = v + a_ref[k]
        out_ref[...] = v

    return pl.pallas_call(
        body, name=name, grid=(1,),
        in_specs=[pl.BlockSpec((n, r, w), lambda i: (0, 0, 0))],
        out_specs=pl.BlockSpec((r, w), lambda i: (0, 0)),
        out_shape=jax.ShapeDtypeStruct((r, w), F32),
        compiler_params=_cparams(("arbitrary",), 2 * (n + 1) * r * w * 4),
    )(stack)


def _adamw(w, g, m, v, tr, name):
    def body(w_ref, g_ref, m_ref, v_ref, d_ref, nm_ref, nv_ref):
        d_ref[...], nm_ref[...], nv_ref[...] = _adam_math(w_ref[...], g_ref[...], m_ref[...], v_ref[...])

    o = jax.ShapeDtypeStruct(w.shape, F32)
    return _rows_call(body, [w, g, m, v], [o, o, o], tr, name)


def _adam_math(w, g, m, v):
    nm = ADAM_B1 * m + (1.0 - ADAM_B1) * g
    nv = ADAM_B2 * v + (1.0 - ADAM_B2) * (g * g)
    m_hat = nm / (1.0 - ADAM_B1 ** ADAM_STEP)
    v_hat = nv / (1.0 - ADAM_B2 ** ADAM_STEP)
    return -ADAM_LR * (m_hat / (jnp.sqrt(v_hat) + ADAM_EPS) + ADAM_WD * w), nm, nv


def _adamw_halves(w, g_own, g_sib, m, v, place, tr, name):
    r, c = w.shape

    def body(pl_ref, w_ref, go_ref, gs_ref, m_ref, v_ref, g_ref, d_ref, nm_ref, nv_ref):
        first = pl_ref[1] == 0
        own, sib = go_ref[...], gs_ref[...]
        g = jnp.concatenate([jnp.where(first, own, sib), jnp.where(first, sib, own)], axis=1)
        g_ref[...] = g
        d_ref[...], nm_ref[...], nv_ref[...] = _adam_math(w_ref[...], g, m_ref[...], v_ref[...])

    full = pl.BlockSpec((tr, c), lambda i, pr: (i, 0))
    half = pl.BlockSpec((tr, c // 2), lambda i, pr: (i, 0))
    o = jax.ShapeDtypeStruct((r, c), F32)
    return pl.pallas_call(
        body, name=name,
        grid_spec=pltpu.PrefetchScalarGridSpec(num_scalar_prefetch=1, grid=(r // tr,),
                                               in_specs=[full, half, half, full, full], out_specs=[full] * 4),
        out_shape=[o] * 4,
        compiler_params=_cparams(("parallel",), 2 * tr * c * 4 * 8),
    )(place, w, g_own, g_sib, m, v)


ANY = pl.BlockSpec(memory_space=pl.ANY)


def _place():
    x, y, c = lax.axis_index("x"), lax.axis_index("y"), lax.axis_index("c")
    others = [(1 - x, y), (x, 1 - y), (1 - x, 1 - y)]
    return x, y, c, 2 * x + y, others


def _remote(src, dst, send, recv, k, to):
    return pltpu.make_async_remote_copy(src_ref=src, dst_ref=dst, send_sem=send.at[k], recv_sem=recv.at[k],
                                        device_id=to, device_id_type=MESH)


def _norm_gather_call(x, norm_w, win_b, wout_b, cw8, tm):
    s = x.shape[0]
    ni = s // tm
    h_in, h_out = win_b.shape[0] // 2, wout_b.shape[0] // 2
    q_in = h_in // 2

    def body(x_ref, w_ref, win, wout, cw, xn_ref, xnt_ref, g_in, g_out, g_cw, send, recv):
        i = pl.program_id(0)

        def direct():
            xx, yy, c, me, others = _place()
            mi, mo = pl.ds(c * h_in, h_in), pl.ds(c * h_out, h_out)
            cps = [_remote(win.at[mi], g_in.at[me, mi], send, recv, j, (*others[j], c)) for j in range(2)]
            cps += [_remote(wout.at[mo], g_out.at[me, mo], send, recv, 7 + j, (*chip, c)) for j, chip in enumerate(others)]
            cps += [_remote(cw, g_cw.at[me], send, recv, 13 + j, (*chip, c)) for j, chip in enumerate(others)]
            return cps

        @pl.when(i == 0)
        def _():
            for cp in direct():
                cp.start()

        xv = x_ref[...]
        r = lax.rsqrt(jnp.mean(xv * xv, axis=-1, keepdims=True) + EPS)
        xn = xv * r * w_ref[...]
        xn_ref[...] = _c(xn)
        xnt_ref[...] = _c(xn.T)

        @pl.when(i == ni - 1)
        def _():
            xx, yy, c, me, others = _place()
            sib = (xx, yy, 1 - c)
            kx, ky, kd = (2 * chip[0] + chip[1] for chip in others)
            mi, ti = pl.ds(c * h_in, h_in), pl.ds((1 - c) * h_in, h_in)
            quarter = [pl.ds(c * h_in, q_in), pl.ds(c * h_in + q_in, q_in)]
            started = []

            def go(cp):
                cp.start()
                started.append(cp)

            _remote(g_in.at[kx, mi], g_in.at[kx, mi], send, recv, 0, (*others[0], c)).wait_recv()
            go(_remote(g_in.at[kx, quarter[0]], g_in.at[kx, quarter[0]], send, recv, 2, (*others[1], c)))
            go(_remote(g_in.at[kx, mi], g_in.at[kx, mi], send, recv, 4, sib))
            _remote(g_in.at[ky, mi], g_in.at[ky, mi], send, recv, 1, (*others[1], c)).wait_recv()
            go(_remote(g_in.at[ky, quarter[1]], g_in.at[ky, quarter[1]], send, recv, 3, (*others[0], c)))
            go(_remote(g_in.at[ky, mi], g_in.at[ky, mi], send, recv, 5, sib))
            mo, to = pl.ds(c * h_out, h_out), pl.ds((1 - c) * h_out, h_out)
            for j, chip in enumerate(others):
                kj = 2 * chip[0] + chip[1]
                _remote(g_out.at[kj, mo], g_out.at[kj, mo], send, recv, 7 + j, (*chip, c)).wait_recv()
                go(_remote(g_out.at[kj, mo], g_out.at[kj, mo], send, recv, 10 + j, sib))
            _remote(g_in.at[kd, quarter[0]], g_in.at[kd, quarter[0]], send, recv, 2, (*others[1], c)).wait_recv()
            _remote(g_in.at[kd, quarter[1]], g_in.at[kd, quarter[1]], send, recv, 3, (*others[0], c)).wait_recv()
            go(_remote(g_in.at[kd, mi], g_in.at[kd, mi], send, recv, 6, sib))
            for k_src, sem in ((kx, 4), (ky, 5), (kd, 6)):
                _remote(g_in.at[k_src, ti], g_in.at[k_src, ti], send, recv, sem, sib).wait_recv()
            for j, chip in enumerate(others):
                kj = 2 * chip[0] + chip[1]
                _remote(g_out.at[kj, to], g_out.at[kj, to], send, recv, 10 + j, sib).wait_recv()
                _remote(cw, g_cw.at[kj], send, recv, 13 + j, (*chip, c)).wait_recv()
            for cp in direct() + started:
                cp.wait_send()

    outs = [jax.ShapeDtypeStruct((s, D_MODEL), MXU_DTYPE), jax.ShapeDtypeStruct((D_MODEL, s), MXU_DTYPE)]
    outs += [jax.ShapeDtypeStruct((4,) + a.shape, a.dtype) for a in (win_b, wout_b, cw8)]
    return pl.pallas_call(
        body, name="rmsnorm_gather_weights",
        grid=(ni,),
        in_specs=[pl.BlockSpec((tm, D_MODEL), lambda i: (i, 0)), pl.BlockSpec((1, D_MODEL), lambda i: (0, 0)), ANY, ANY, ANY],
        out_specs=[pl.BlockSpec((tm, D_MODEL), lambda i: (i, 0)), pl.BlockSpec((D_MODEL, tm), lambda i: (0, i)), ANY, ANY, ANY],
        out_shape=outs,
        scratch_shapes=[pltpu.SemaphoreType.DMA((16,)), pltpu.SemaphoreType.DMA((16,))],
        compiler_params=_cparams(("arbitrary",), 2 * tm * D_MODEL * 12),
    )(x, norm_w, win_b, wout_b, cw8)


def _dw_out_rs_call(a, b, dw, *, tn, tk):
    m, k = a.shape
    n = b.shape[1]
    nj, nk = n // tn, k // tk
    half = dw.shape[0] // 2

    def body(a_ref, b_ref, pin, o_ref, sib_in, send, recv):
        j, kk = pl.program_id(0), pl.program_id(1)

        def copies():
            x, y, c, me, others = _place()
            rows = pl.ds((1 - c) * half, half)
            return [_remote(pin.at[rows, pl.ds(WIN_STEP * w, WIN_W)], sib_in.at[w], send, recv, w, (x, y, 1 - c)) for w in range(4)]

        @pl.when((j == 0) & (kk == 0))
        def _():
            for cp in copies():
                cp.start()

        @pl.when(kk == 0)
        def _():
            o_ref[...] = jnp.zeros_like(o_ref)

        o_ref[...] += _dot(a_ref[...], b_ref[...])

        @pl.when((j == nj - 1) & (kk == nk - 1))
        def _():
            cps = copies()
            for cp in cps:
                cp.wait_recv()
            for cp in cps:
                cp.wait_send()

    isz = jnp.dtype(a.dtype).itemsize
    est = 2 * (m * tk + tk * tn) * isz + 2 * m * tn * 4
    outs = [jax.ShapeDtypeStruct((m, n), F32), jax.ShapeDtypeStruct((4, half, WIN_W), dw.dtype)]
    return pl.pallas_call(
        body, name="dw_out_rs_sibling",
        grid=(nj, nk),
        in_specs=[pl.BlockSpec((m, tk), lambda j, kk: (0, kk)), pl.BlockSpec((tk, tn), lambda j, kk: (kk, j)), ANY],
        out_specs=[pl.BlockSpec((m, tn), lambda j, kk: (0, j)), ANY],
        out_shape=outs,
        scratch_shapes=[pltpu.SemaphoreType.DMA((4,)), pltpu.SemaphoreType.DMA((4,))],
        compiler_params=_cparams(("arbitrary", "arbitrary"), est),
    )(a, b, dw)


def _rs_sibling_call(p_out, vsmall):
    def body(pout, vs, sib_out, sib_v, send, recv):
        x, y, c, me, others = _place()
        sib = (x, y, 1 - c)
        half = pout.shape[1] // 2
        cps = [_remote(pout.at[:, pl.ds((1 - c) * half, half)], sib_out, send, recv, 0, sib),
               _remote(vs, sib_v, send, recv, 1, sib)]
        for cp in cps:
            cp.start()
        for cp in cps:
            cp.wait_recv()
        for cp in cps:
            cp.wait_send()

    outs = [jax.ShapeDtypeStruct((4, p_out.shape[1] // 2, p_out.shape[2]), p_out.dtype),
            jax.ShapeDtypeStruct(vsmall.shape, vsmall.dtype)]
    return pl.pallas_call(
        body, name="rs_sibling",
        in_specs=[ANY] * 2, out_specs=[ANY] * 2, out_shape=outs,
        scratch_shapes=[pltpu.SemaphoreType.DMA((2,)), pltpu.SemaphoreType.DMA((2,))],
    )(p_out, vsmall)


def _rs_join_call(f_in, f_out, nw8):
    def body(fin, fout, nw, sib_in, full_out, all_nw, send, recv):
        x, y, c, me, others = _place()
        sib = (x, y, 1 - c)
        half = fout.shape[0]
        cps = [_remote(fin, sib_in, send, recv, 0, sib),
               _remote(fout, full_out.at[pl.ds(c * half, half)], send, recv, 1, sib)]
        mine = 4 * x + 2 * y + c
        peers = []
        for r in range(1, 8):
            px, py, pc = (1 - x if r & 4 else x), (1 - y if r & 2 else y), (1 - c if r & 1 else c)
            peers.append((r, (px, py, pc), 4 * px + 2 * py + pc))
            cps.append(_remote(nw, all_nw.at[mine], send, recv, 1 + r, (px, py, pc)))
        for cp in cps:
            cp.start()
        cps[0].wait_recv()
        _remote(fout, full_out.at[pl.ds((1 - c) * half, half)], send, recv, 1, sib).wait_recv()
        for r, peer, idx in peers:
            _remote(nw, all_nw.at[idx], send, recv, 1 + r, peer).wait_recv()
        for cp in cps:
            cp.wait_send()

    outs = [jax.ShapeDtypeStruct(f_in.shape, F32), jax.ShapeDtypeStruct((2 * f_out.shape[0], f_out.shape[1]), F32),
            jax.ShapeDtypeStruct((8,) + nw8.shape, F32)]
    return pl.pallas_call(
        body, name="rs_join",
        in_specs=[ANY] * 3, out_specs=[ANY] * 3, out_shape=outs,
        scratch_shapes=[pltpu.SemaphoreType.DMA((9,)), pltpu.SemaphoreType.DMA((9,))],
    )(f_in, f_out, nw8)


def _pack(arrs):
    parts = []
    for a in arrs:
        f = a.reshape(-1).astype(F32)
        pad = (-f.shape[0]) % 1024
        parts.append(jnp.pad(f, (0, pad)).reshape(-1, 128))
    return jnp.concatenate(parts, axis=0)


def _unpack(packed, shapes):
    out, row = [], 0
    for shp in shapes:
        n = 1
        for d in shp:
            n *= d
        rows = (n + 1023) // 1024 * 8
        out.append(packed[row:row + rows].reshape(-1)[:n].reshape(shp))
        row += rows
    return out


def _expand_heads(v32):
    return jnp.repeat(v32.reshape(32), HEADDIM).reshape(1, D_MODEL)


def kernel(x, norm_w, w_in, gate_b, sgu_norm_g, sgu_norm_b, sgu_w, sgu_b, conv_w, conv_b, dt_bias, A_log, D_skip, ssd_norm_w, w_out, final_norm_w, loss_target, m_norm_w, m_w_in, m_gate_b, m_sgu_norm_g, m_sgu_norm_b, m_sgu_w, m_sgu_b, m_conv_w, m_conv_b, m_dt_bias, m_A_log, m_D_skip, m_ssd_norm_w, m_w_out, m_final_norm_w, v_norm_w, v_w_in, v_gate_b, v_sgu_norm_g, v_sgu_norm_b, v_sgu_w, v_sgu_b, v_conv_w, v_conv_b, v_dt_bias, v_A_log, v_D_skip, v_ssd_norm_w, v_w_out, v_final_norm_w):
    s = x.shape[1]
    x2 = x.reshape(s, D_MODEL)
    tgt = loss_target.reshape(s, D_MODEL)
    t_ssd, t_tok, t_out, t_row = min(T_SSD, s), min(T_TOK, s), min(T_OUT, s), min(T_ROW, s)
    tm_mm, tk_dw = min(TM_MM, s), min(TK_DW, s)
    chip = 2 * lax.axis_index("x") + lax.axis_index("y")

    cw8 = jnp.pad(conv_w[0], ((0, 4), (0, 0)))
    win_b, wout_b = _c(w_in[0]), _c(w_out[0])
    xn, xnt, g_in, g_out, g_cw = _norm_gather_call(x2, norm_w, win_b, wout_b, cw8, t_row)
    g_in = lax.dynamic_update_index_in_dim(g_in, win_b, chip, 0)
    g_out = lax.dynamic_update_index_in_dim(g_out, wout_b, chip, 0)
    g_cw = lax.dynamic_update_index_in_dim(g_cw, cw8, chip, 0)
    wt = jnp.transpose(g_in, (0, 2, 1)).reshape(IN_W, D_MODEL)
    w_out_full = g_out.reshape(D_MODEL, D_MODEL)
    conv_w_full = jnp.transpose(g_cw[:, 0:4, :], (1, 0, 2)).reshape(4, 3072)

    wt_a = jnp.concatenate([wt[0:6144], wt[11296:15392]], axis=0)
    per_group = lambda lo, n: wt[lo:lo + SSD_GROUPS * n].reshape(SSD_GROUPS, n, D_MODEL)
    wt_b = jnp.concatenate([per_group(6144, 512), per_group(8192, 512), per_group(10240, 128), per_group(10752, 128),
                            jnp.pad(per_group(11264, 8), ((0, 0), (0, 120), (0, 0)))], axis=1)
    wt_b = wt_b.reshape(SSD_GROUPS * BW_B, D_MODEL)

    def group_cols(full_xs, full_bc):
        parts = []
        for g in range(SSD_GROUPS):
            parts += [full_xs[:, 512 * g:512 * g + 512], full_bc[:, 128 * g:128 * g + 128], full_bc[:, 512 + 128 * g:512 + 128 * g + 128]]
        return jnp.concatenate(parts, axis=1)

    cw_g = group_cols(conv_w_full[:, 0:2048], conv_w_full[:, 2048:3072])
    cb_g = group_cols(conv_b[:, 0:2048], conv_b[:, 2048:3072])
    alog_e, dsk_e = _expand_heads(A_log), _expand_heads(D_skip)
    narrow = lambda v32: jnp.pad(v32.reshape(SSD_GROUPS, 8), ((0, 0), (0, 120))).reshape(1, SSD_GROUPS * 128)
    dtb_n, alog_n = narrow(dt_bias), narrow(A_log)

    pos_chunk = jnp.arange(SGU_BLOCK) // CHUNK
    smask = pos_chunk[None, :] <= pos_chunk[:, None]
    wm_f = jnp.where(smask[None], sgu_w[0], 0.0)
    wm = _c(wm_f)
    wmt = _c(jnp.transpose(wm_f, (0, 2, 1)))
    bias_full = jnp.repeat(sgu_b[0].T, D_MODEL // SGU_GROUPS, axis=1)
    fnw = final_norm_w.reshape(1, D_MODEL)

    proj_a = _mm(xn, wt_a, tm=tm_mm, tn=1024, tk=D_MODEL, name="in_proj_a", out_dtype=MXU_DTYPE, b_is_t=True)
    proj_b = _mm(xn, wt_b, tm=tm_mm, tn=BW_B, tk=D_MODEL, name="in_proj_b", b_is_t=True)
    y_ssd, y_b, hprev, pre_all, dt_all, acs_all = _ssd_fwd_call(proj_b, dtb_n, alog_n, dsk_e, cw_g, cb_g, ssd_norm_w, t_ssd, NG_SSD)
    y_a, merged, merged_t = _tok_fwd_call(proj_a, y_b, gate_b, sgu_norm_g, sgu_norm_b, wm, bias_full, t_tok)
    dh, dh_b, dmerged, loss_t, dfw8 = _out_call(merged, x2, tgt, w_out_full, fnw, t_out)

    dproj_a, dy_b, dgb8, dgam8, dbeta8, dbfull, dws = _tok_bwd_call(
        proj_a, dmerged, y_a, y_b, gate_b, sgu_norm_g, sgu_norm_b, wm, wmt, bias_full, t_tok)
    dproj_b, a512, a768 = _ssd_bwd_call(proj_b, pre_all, dt_all, acs_all, dy_b, y_ssd, hprev, dtb_n, alog_e, dsk_e, cw_g,
                                        ssd_norm_w, t_ssd, NG_SSD)
    dw_uvz = _mm(xnt, dproj_a, tm=D_MODEL, tn=1024, tk=min(2 * tk_dw, s), name="dw_in_uvz", n=6144, out_dtype=MXU_DTYPE)
    dw_gate = _mm(xnt, dproj_a, tm=D_MODEL, tn=1024, tk=min(2 * tk_dw, s), name="dw_in_gate", col0=6, n=4096,
                  out_dtype=MXU_DTYPE)
    dw_zb, dw_xs, dw_bm, dw_cm, dw_dt = _dw_groups(xnt, dproj_b, tk=tk_dw, out_dtype=MXU_DTYPE)

    dw_dt32 = jnp.concatenate([dw_dt[:, 128 * g:128 * g + 8] for g in range(SSD_GROUPS)], axis=1)
    dw_ref = jnp.concatenate([dw_uvz, dw_zb, dw_xs, dw_bm, dw_cm, dw_dt32, dw_gate,
                              jnp.zeros((D_MODEL, 3 * WIN_STEP + WIN_W - IN_W), MXU_DTYPE)], axis=1)
    dw_out_p, sib_i = _dw_out_rs_call(merged_t, dh_b, dw_ref, tn=1024, tk=tk_dw)
    p_out = dw_out_p.reshape(4, D_MODEL // 4, D_MODEL)

    s512 = jnp.sum(a512, axis=2)
    heads = lambda v: jnp.sum(v.reshape(32, HEADDIM), axis=1).reshape(1, 32)
    d_ssd_nw = s512[:, 0].reshape(1, D_MODEL)
    d_dskip = heads(s512[:, 1].reshape(D_MODEL))
    d_alog = heads(s512[:, 2].reshape(D_MODEL)) * (1.0 / HEADDIM) * (-jnp.exp(A_log))
    d_dtb = s512[:, 3, 0:8].reshape(1, 32)
    s768 = jnp.sum(a768, axis=2)
    ungroup = lambda v: jnp.concatenate([v[g, :, 0:512] for g in range(4)] + [v[g, :, 512:640] for g in range(4)]
                                        + [v[g, :, 640:768] for g in range(4)], axis=1)
    d_cw = ungroup(s768[:, 0:4])
    d_cb = ungroup(s768[:, 4:5])
    d_sgu_b = jnp.sum(dbfull.reshape(128, SGU_GROUPS, 128), axis=2).T.reshape(1, SGU_GROUPS, 128)
    d_sgu_w = jnp.where(smask[None], dws, 0.0).reshape(1, SGU_GROUPS, 128, 128)
    fold = lambda a8: jnp.sum(a8, axis=0, keepdims=True)
    small_local = [fold(dgb8), fold(dgam8), fold(dbeta8), d_sgu_w, d_sgu_b, d_cw, d_cb,
                   d_dtb, d_alog, d_dskip, d_ssd_nw, fold(dfw8).reshape(D_MODEL), jnp.sum(loss_t[:, 0, 0]).reshape(1)]
    small_shapes = [a.shape for a in small_local]
    v_local = _pack(small_local)

    core = lax.axis_index("c")
    place = jnp.stack([chip, core]).astype(jnp.int32)
    hr_i, hr_o = D_MODEL // 2, D_MODEL // 8
    sib_o, sib_v = _rs_sibling_call(p_out, v_local)
    s1b_i, o_i = _rs_add_windows(dw_ref, sib_i, place, 256, "rs_add_in")
    s1b_o, o_o = _rs_add(p_out, sib_o, place, 256, "rs_add_out")
    chip_v = _add_pair(v_local, sib_v, v_local.shape[0], "ar_add_small")
    dxn, r_i, r_o, abs_v = _dx_rs_call(dproj_a, wt_a, dproj_b, wt_b, s1b_i, s1b_o, chip_v, tm=tm_mm)
    grad_x, dnw8 = _gradx_call(x2, dxn, dh, norm_w, t_row)
    abs_v = lax.dynamic_update_index_in_dim(abs_v, chip_v, chip, 0)
    f_i = _sum_own_recv(o_i, r_i, 256, "rs_sum_in")
    f_o = _sum_own_recv(o_o, r_o, 256, "rs_sum_out")
    sib_f_i, g_w_out, all_nw = _rs_join_call(f_i, f_o, dnw8)
    g_w_out = lax.dynamic_update_slice_in_dim(g_w_out, f_o, core * hr_o, axis=0)
    all_nw = lax.dynamic_update_index_in_dim(all_nw, dnw8, 2 * chip + core, 0)
    g_nw = fold(_sum_slots(all_nw, "ar_sum_norm_w"))
    total_v = _sum_slots(abs_v, "ar_sum_small")
    (g_gb, g_gam, g_beta, g_sw, g_sb, g_cw_full, g_cb, g_dtb, g_alog, g_dsk, g_snw, g_fnw, loss1) = _unpack(total_v, small_shapes)
    g_cw_shard = lax.dynamic_slice(g_cw_full, (0, chip * 768), (4, 768)).reshape(1, 4, 768)
    loss = loss1.reshape(())

    shard_t = lambda win: lax.dynamic_slice_in_dim(win, 8 * chip, SHARD_W, axis=1).T
    g_w_in, d_win, nm_win, nv_win = (a.T for a in _adamw_halves(w_in[0].T, shard_t(f_i), shard_t(sib_f_i), m_w_in[0].T,
                                                                v_w_in[0].T, place, 296, "adamw_w_in"))
    d_wout, nm_wout, nv_wout = _adamw(w_out[0], g_w_out, m_w_out[0], v_w_out[0], 128, "adamw_w_out")
    small_w = [norm_w, gate_b, sgu_norm_g, sgu_norm_b, sgu_w, sgu_b, conv_w, conv_b, dt_bias, A_log, D_skip, ssd_norm_w, final_norm_w]
    small_m = [m_norm_w, m_gate_b, m_sgu_norm_g, m_sgu_norm_b, m_sgu_w, m_sgu_b, m_conv_w, m_conv_b, m_dt_bias, m_A_log, m_D_skip, m_ssd_norm_w, m_final_norm_w]
    small_v = [v_norm_w, v_gate_b, v_sgu_norm_g, v_sgu_norm_b, v_sgu_w, v_sgu_b, v_conv_w, v_conv_b, v_dt_bias, v_A_log, v_D_skip, v_ssd_norm_w, v_final_norm_w]
    small_g = [g_nw, g_gb, g_gam, g_beta, g_sw, g_sb, g_cw_shard, g_cb, g_dtb, g_alog, g_dsk, g_snw, g_fnw]
    shapes_w = [a.shape for a in small_w]
    small_g = [a.reshape(shp) for a, shp in zip(small_g, shapes_w)]
    pw = _pack(small_w)
    pd, pm, pv = _adamw(pw, _pack(small_g), _pack(small_m), _pack(small_v), pw.shape[0], "adamw_small")
    d_small, nm_small, nv_small = _unpack(pd, shapes_w), _unpack(pm, shapes_w), _unpack(pv, shapes_w)

    def with_big(small, win, wout):
        o = list(small)
        return o[0:1] + [win.reshape(1, D_MODEL, SHARD_W)] + o[1:12] + [wout.reshape(1, D_MODEL // 4, D_MODEL)] + o[12:13]

    grads = with_big(small_g, g_w_in, g_w_out)
    deltas = with_big(d_small, d_win, d_wout)
    new_m = with_big(nm_small, nm_win, nm_wout)
    new_v = with_big(nv_small, nv_win, nv_wout)
    return (loss, grad_x.reshape(1, s, D_MODEL), *grads, *deltas, *new_m, *new_v)
```

```python
import functools

import jax
import jax.numpy as jnp
from jax import lax
from jax.experimental import pallas as pl
from jax.experimental.pallas import tpu as pltpu

F32 = jnp.float32
MXU_DTYPE = jnp.bfloat16

D_MODEL = 2048
EPS = 1e-5
CHUNK = 64
SGU_BLOCK = 128
SGU_GROUPS = 16
SSD_GROUPS = 4
SSD_GW = 512
SSD_STATE = 128
HEADDIM = 64
IN_W = 15392
SHARD_W = IN_W // 4
BW_B = 1408
FW_B = BW_B
XBC_O, DT_O = 512, 1280
NA = 10240

ADAM_LR = 0.001
ADAM_B1 = 0.9
ADAM_B2 = 0.999
ADAM_EPS = 1e-08
ADAM_WD = 0.01
ADAM_STEP = 10

T_SSD = 256
NG_SSD = 2
T_TOK = 128
T_OUT = 256
T_ROW = 512
TM_MM = 1024
TK_DW = 1024
VMEM_CAP = 60 * 1024 * 1024
MESH = pl.DeviceIdType.MESH


def _cparams(sem, est_bytes):
    lim = int(min(VMEM_CAP, max(32 * 1024 * 1024, est_bytes + 12 * 1024 * 1024)))
    return pltpu.CompilerParams(dimension_semantics=sem, vmem_limit_bytes=lim)


def _c(x):
    return x.astype(MXU_DTYPE)


def _dot(a, b):
    return jnp.dot(a, b, preferred_element_type=F32)


def _dot_nt(a, b):
    return lax.dot_general(a, b, (((1,), (1,)), ((), ())), preferred_element_type=F32)


def _dot_tn(a, b):
    return lax.dot_general(a, b, (((0,), (0,)), ((), ())), preferred_element_type=F32)


def _split(x, n):
    parts, r = [], x
    for _ in range(n):
        p = _c(r)
        parts.append(p)
        r = r - p.astype(F32)
    return parts


def _dot01_l(m01, x, n):
    acc = None
    for p in _split(x, n):
        t = _dot(m01, p)
        acc = t if acc is None else acc + t
    return acc


def _dot01_r(x, m01, n):
    acc = None
    for p in _split(x, n):
        t = _dot(p, m01)
        acc = t if acc is None else acc + t
    return acc


def _sigmoid(x):
    return 1.0 / (1.0 + jnp.exp(-x))


def _fold8(x):
    r, w = x.shape
    return jnp.sum(x.reshape(r // 8, 8, w), axis=0)


def _iota(shape, dim):
    return lax.broadcasted_iota(jnp.int32, shape, dim)


def _ssd_masks():
    l64 = _iota((CHUNK, SSD_GW), 0)
    s64 = jnp.bitwise_and(_iota((CHUNK, SSD_GW), 1), CHUNK - 1)
    diag = l64 == s64
    causal = l64 >= s64
    row_last = l64 == CHUNK - 1
    r4 = lax.shift_right_logical(_iota((256, 256), 0), 6)
    c4 = lax.shift_right_logical(_iota((256, 256), 1), 6)
    mask4 = r4 == c4
    return diag, causal, row_last, mask4


def _cumsum_mats(t):
    r, c = _iota((t, t), 0), _iota((t, t), 1)
    same = lax.shift_right_logical(r, 6) == lax.shift_right_logical(c, 6)
    tri = _c(jnp.where(same, jnp.where(c <= r, 1.0, 0.0), 0.0))
    trit = _c(jnp.where(same, jnp.where(c >= r, 1.0, 0.0), 0.0))
    return tri, trit


def _head_expand_mat():
    return _c(jnp.where(_iota((128, SSD_GW), 0) == lax.shift_right_logical(_iota((128, SSD_GW), 1), 6), 1.0, 0.0))


def _ssd_common(xs, bm, cm, dt, acs, masks):
    diag, causal, row_last, mask4 = masks
    row_e = jnp.sum(jnp.where(diag, acs, 0.0), axis=0, keepdims=True)
    seg = acs - row_e
    lm = jnp.exp(jnp.where(causal, seg, -1e30))
    bb, cb = _c(bm), _c(cm)
    brep = jnp.concatenate([bb] * 8, axis=0)
    cbrep = _dot_nt(cb, brep)
    m = cbrep * lm
    xdt = xs * dt
    acs_last = jnp.sum(jnp.where(row_last, acs, 0.0), axis=0, keepdims=True)
    dec = jnp.exp(acs_last - acs)
    eacs = jnp.exp(acs)
    cd = jnp.exp(acs_last)
    return dict(lm=lm, bb=bb, cb=cb, brep=brep, m=m, xdt=xdt, dec=dec, eacs=eacs, cd=cd)


def _blockdiag4(xb, mask4):
    return jnp.where(mask4, jnp.concatenate([xb] * 4, axis=0), jnp.zeros((), xb.dtype))


def _ssd_chunk_fwd(xs, bm, cm, dt, acs, d_skip, ht, masks):
    q = _ssd_common(xs, bm, cm, dt, acs, masks)
    mask4 = masks[3]
    mb, xdtb = _c(q["m"]), _c(q["xdt"])
    yd = []
    for blk in range(2):
        sl = slice(256 * blk, 256 * blk + 256)
        yd.append(_dot(mb[:, sl], _blockdiag4(xdtb[:, sl], mask4)))
    y_diag = jnp.concatenate(yd, axis=1)
    p = _dot(q["cb"], _c(ht))
    y = y_diag + p * q["eacs"] + xs * d_skip
    st = _dot_tn(q["bb"], _c(q["xdt"] * q["dec"]))
    return y, ht * q["cd"] + st


def _ssd_chunk_bwd(xs, bm, cm, dt, acs, d_skip, hprev, dht, dy, masks):
    diag, causal, row_last, mask4 = masks
    q = _ssd_common(xs, bm, cm, dt, acs, masks)
    lm, bb, cb, brep, m, xdt, dec, eacs, cd = (q[k] for k in ("lm", "bb", "cb", "brep", "m", "xdt", "dec", "eacs", "cd"))
    hb = _c(hprev)
    yoff = _dot(cb, hb) * eacs
    dyb = _c(dy)
    dpb = _c(dy * eacs)
    d_c = _dot_nt(dpb, hb)
    dh_y = _dot_tn(cb, dpb)
    mb, xdtb = _c(m), _c(xdt)
    dm_parts, dxdt_parts = [], []
    for blk in range(2):
        sl = slice(256 * blk, 256 * blk + 256)
        bd = _blockdiag4(xdtb[:, sl], mask4)
        dm_parts.append(_dot_nt(dyb[:, sl], bd))
        dxf = jnp.where(mask4, _dot_tn(mb[:, sl], dyb[:, sl]), 0.0)
        dxdt_parts.append(dxf[0:64] + dxf[64:128] + dxf[128:192] + dxf[192:256])
    dm = jnp.concatenate(dm_parts, axis=1)
    dxdt = jnp.concatenate(dxdt_parts, axis=1)
    dcbb = _c(dm * lm)
    g = dm * m
    d_c = d_c + _dot(dcbb, brep)
    dbrep = _dot_tn(dcbb, cb)
    d_b = dbrep[0:64]
    for r in range(1, 8):
        d_b = d_b + dbrep[64 * r:64 * r + 64]
    dhtb = _c(dht)
    dxd = _dot(bb, dhtb)
    xd = xdt * dec
    dxdt = dxdt + dxd * dec
    tq = dxd * xd
    d_b = d_b + _dot_nt(_c(xd), dhtb)
    dcd = jnp.sum(dht * hprev, axis=0, keepdims=True)
    col_g = jnp.sum(g, axis=0, keepdims=True)
    last = jnp.sum(tq, axis=0, keepdims=True) + dcd * cd
    qq = g - jnp.where(diag, col_g, 0.0) + dy * yoff - tq + jnp.where(row_last, last, 0.0)
    dxs = dxdt * dt + dy * d_skip
    return dxs, d_b, d_c, dht * cd + dh_y, dy * xs, qq, dxdt * xs


def _ssd_finish_dt(qq, p1, dt, a_neg, trit, mask4):
    bd4 = _c(jnp.where(mask4, 1.0, 0.0))
    dacs = jnp.concatenate([_dot01_r(qq[:, 256 * b:256 * b + 256], bd4, 2) for b in range(2)], axis=1)
    da = _dot01_l(trit, dacs, 2)
    return p1 + da * (a_neg * (1.0 / HEADDIM)), da * dt


def _softplus(x):
    return jnp.maximum(x, 0.0) + jnp.log(1.0 + jnp.exp(-jnp.abs(x)))


def _conv_taps(xpad, t):
    taps = []
    for k in range(4):
        sh = 3 - k
        v = xpad if sh == 0 else pltpu.roll(xpad, sh, 0)
        taps.append(v[8:8 + t])
    return taps


def _mm(a, b, *, tm, tn, tk, name, out_dtype=F32, col0=0, n=None, b_is_t=False):
    m, k = a.shape
    n = b.shape[0 if b_is_t else 1] if n is None else n
    nk = k // tk
    assert m % tm == 0 and n % tn == 0 and k % tk == 0, (a.shape, b.shape, tm, tn, tk)
    dot = _dot_nt if b_is_t else _dot
    via_acc = nk > 1 and out_dtype != F32

    def body(a_ref, b_ref, o_ref, *acc):
        if nk == 1:
            o_ref[...] = dot(a_ref[...], b_ref[...]).astype(out_dtype)
            return
        acc_ref = acc[0] if via_acc else o_ref

        @pl.when(pl.program_id(2) == 0)
        def _():
            acc_ref[...] = jnp.zeros_like(acc_ref)

        acc_ref[...] += dot(a_ref[...], b_ref[...])
        if via_acc:
            @pl.when(pl.program_id(2) == nk - 1)
            def _():
                o_ref[...] = acc_ref[...].astype(out_dtype)

    isz = jnp.dtype(a.dtype).itemsize
    est = 2 * (tm * tk + tk * tn) * isz + 3 * tm * tn * 4
    return pl.pallas_call(
        body, name=name,
        grid=(m // tm, n // tn, nk),
        in_specs=[pl.BlockSpec((tm, tk), lambda i, j, kk: (i, kk)),
                  pl.BlockSpec((tn, tk), lambda i, j, kk: (j + col0, kk)) if b_is_t
                  else pl.BlockSpec((tk, tn), lambda i, j, kk: (kk, j + col0))],
        out_specs=pl.BlockSpec((tm, tn), lambda i, j, kk: (i, j)),
        out_shape=jax.ShapeDtypeStruct((m, n), out_dtype),
        scratch_shapes=[pltpu.VMEM((tm, tn), F32)] if via_acc else [],
        compiler_params=_cparams(("parallel", "parallel", "arbitrary"), est),
    )(a, b)


def _dw_groups(xnt, dpb, *, tk, out_dtype):
    m, k = xnt.shape
    nk = k // tk

    def body(a_ref, b_ref, zb_ref, xs_ref, bm_ref, cm_ref, dt_ref, acc_ref):
        @pl.when(pl.program_id(1) == 0)
        def _():
            acc_ref[...] = jnp.zeros_like(acc_ref)

        acc_ref[...] += _dot(a_ref[...], b_ref[...])

        @pl.when(pl.program_id(1) == nk - 1)
        def _():
            for o_ref, lo, hi in ((zb_ref, 0, 512), (xs_ref, 512, 1024), (bm_ref, 1024, 1152), (cm_ref, 1152, 1280),
                                  (dt_ref, 1280, 1408)):
                o_ref[...] = acc_ref[:, lo:hi].astype(out_dtype)

    isz = jnp.dtype(xnt.dtype).itemsize
    est = 2 * (m * tk + tk * BW_B) * isz + 3 * m * BW_B * 4
    piece = lambda w: pl.BlockSpec((m, w), lambda g, kk: (0, g))
    return pl.pallas_call(
        body, name="dw_in_b",
        grid=(SSD_GROUPS, nk),
        in_specs=[pl.BlockSpec((m, tk), lambda g, kk: (0, kk)), pl.BlockSpec((tk, BW_B), lambda g, kk: (kk, g))],
        out_specs=[piece(512), piece(512), piece(128), piece(128), piece(128)],
        out_shape=[jax.ShapeDtypeStruct((m, w), out_dtype) for w in (2048, 2048, 512, 512, 512)],
        scratch_shapes=[pltpu.VMEM((m, BW_B), F32)],
        compiler_params=_cparams(("parallel", "arbitrary"), est),
    )(xnt, dpb)


def _dx_rs_call(dpa, wta, dpb, wtb, sb_in, sb_out, chip_v, *, tm):
    s = dpa.shape[0]
    tka, tkb = 1024, BW_B
    nka, nkb = dpa.shape[1] // tka, dpb.shape[1] // tkb
    ni, nk = s // tm, nka + nkb

    def body(a_ref, wa_ref, b_ref, wb_ref, sbin, sbout, cv, o_ref, rc_in, rc_out, abs_v, send, recv):
        i, kk = pl.program_id(0), pl.program_id(1)

        def copies():
            x, y, c, me, others = _place()
            sends, recvs = [], []
            for j, chip in enumerate(others):
                kj = 2 * chip[0] + chip[1]
                to = (*chip, c)
                sends += [_remote(sbin.at[kj], rc_in.at[j], send, recv, j, to),
                          _remote(sbout.at[kj], rc_out.at[j], send, recv, 3 + j, to),
                          _remote(cv, abs_v.at[me], send, recv, 6 + j, to)]
                recvs += [sends[-3], sends[-2], _remote(cv, abs_v.at[kj], send, recv, 6 + j, to)]
            return sends, recvs

        @pl.when((i == 0) & (kk == 0))
        def _():
            for cp in copies()[0]:
                cp.start()

        @pl.when(kk == 0)
        def _():
            o_ref[...] = jnp.zeros_like(o_ref)

        @pl.when(kk < nka)
        def _():
            o_ref[...] += _dot(a_ref[...], wa_ref[...])

        @pl.when(kk >= nka)
        def _():
            o_ref[...] += _dot(b_ref[...], wb_ref[...])

        @pl.when((i == ni - 1) & (kk == nk - 1))
        def _():
            sends, recvs = copies()
            for cp in recvs:
                cp.wait_recv()
            for cp in sends:
                cp.wait_send()

    isz = jnp.dtype(dpa.dtype).itemsize
    est = 2 * isz * (tm * tka + tka * D_MODEL + tm * tkb + tkb * D_MODEL) + 2 * tm * D_MODEL * 4
    outs = [jax.ShapeDtypeStruct((s, D_MODEL), F32),
            jax.ShapeDtypeStruct((3,) + sb_in.shape[1:], sb_in.dtype), jax.ShapeDtypeStruct((3,) + sb_out.shape[1:], sb_out.dtype),
            jax.ShapeDtypeStruct((4,) + chip_v.shape, F32)]
    return pl.pallas_call(
        body, name="dx_matmul_rs_chips",
        grid=(ni, nk),
        in_specs=[
            pl.BlockSpec((tm, tka), lambda i, kk: (i, jnp.minimum(kk, nka - 1))),
            pl.BlockSpec((tka, D_MODEL), lambda i, kk: (jnp.minimum(kk, nka - 1), 0)),
            pl.BlockSpec((tm, tkb), lambda i, kk: (i, jnp.maximum(kk - nka, 0))),
            pl.BlockSpec((tkb, D_MODEL), lambda i, kk: (jnp.maximum(kk - nka, 0), 0)),
            ANY, ANY, ANY,
        ],
        out_specs=[pl.BlockSpec((tm, D_MODEL), lambda i, kk: (i, 0)), ANY, ANY, ANY],
        out_shape=outs,
        scratch_shapes=[pltpu.SemaphoreType.DMA((9,)), pltpu.SemaphoreType.DMA((9,))],
        compiler_params=_cparams(("arbitrary", "arbitrary"), est),
    )(dpa, wta, dpb, wtb, sb_in, sb_out, chip_v)


def _gradx_call(x, dxn, dh, norm_w, tm):
    s = x.shape[0]

    def body(x_ref, g_ref, dh_ref, w_ref, gx_ref, dw_ref):
        @pl.when(pl.program_id(0) == 0)
        def _():
            dw_ref[...] = jnp.zeros_like(dw_ref)

        xv, gv = x_ref[...], g_ref[...]
        r = lax.rsqrt(jnp.mean(xv * xv, axis=-1, keepdims=True) + EPS)
        gw = gv * w_ref[...]
        gx_ref[...] = r * gw - xv * (r * r * r) * jnp.mean(xv * gw, axis=-1, keepdims=True) + dh_ref[...].astype(F32)
        dw_ref[...] += _fold8(gv * (xv * r))

    row = pl.BlockSpec((tm, D_MODEL), lambda i: (i, 0))
    return pl.pallas_call(
        body, name="grad_x",
        grid=(s // tm,),
        in_specs=[row, row, row, pl.BlockSpec((1, D_MODEL), lambda i: (0, 0))],
        out_specs=[row, pl.BlockSpec((8, D_MODEL), lambda i: (0, 0))],
        out_shape=[jax.ShapeDtypeStruct((s, D_MODEL), F32), jax.ShapeDtypeStruct((8, D_MODEL), F32)],
        compiler_params=_cparams(("arbitrary",), 2 * tm * D_MODEL * 16),
    )(x, dxn, dh, norm_w)


def _layernorm_stats(v):
    mu = jnp.mean(v, axis=-1, keepdims=True)
    vc = v - mu
    var = jnp.mean(vc * vc, axis=-1, keepdims=True)
    return vc * lax.rsqrt(var + EPS), lax.rsqrt(var + EPS)


def _tok_fwd_call(proj_a, y_b, gate_b, sgu_g, sgu_beta, wm, bias_full, t):
    s = proj_a.shape[0]

    def body(pa_ref, yb_ref, gb_ref, g_ref, be_ref, wm_ref, bf_ref, ya_ref, mg_ref, mgt_ref, mix_ref):
        u = pa_ref[:, 0:2048].astype(F32)
        v = pa_ref[:, 2048:4096].astype(F32)
        za = pa_ref[:, 4096:6144].astype(F32)
        xhat, _ = _layernorm_stats(v)
        vnb = _c(xhat * g_ref[...] + be_ref[...])
        for gi in range(SGU_GROUPS):
            sl = slice(128 * gi, 128 * gi + 128)
            mix_ref[:, sl] = _dot(wm_ref[gi], vnb[:, sl])
        mixed = mix_ref[...] + bf_ref[...]
        y_a = u * mixed * (za * _sigmoid(za))
        g0 = _sigmoid(pa_ref[:, 6144:8192].astype(F32) + gb_ref[:, 0:2048])
        g1 = _sigmoid(pa_ref[:, 8192:10240].astype(F32) + gb_ref[:, 2048:4096])
        merged = g0 * y_a + g1 * yb_ref[...].astype(F32)
        ya_ref[...] = _c(y_a)
        mg_ref[...] = _c(merged)
        mgt_ref[...] = _c(merged.T)

    row = pl.BlockSpec((t, D_MODEL), lambda i: (i, 0))
    vec = lambda w: pl.BlockSpec((1, w), lambda i: (0, 0))
    return pl.pallas_call(
        body, name="tok_fwd",
        grid=(s // t,),
        in_specs=[pl.BlockSpec((t, NA), lambda i: (i, 0)), row, vec(4096), vec(2048), vec(2048),
                  pl.BlockSpec((SGU_GROUPS, 128, 128), lambda i: (0, 0, 0)), pl.BlockSpec((128, D_MODEL), lambda i: (0, 0))],
        out_specs=[row, row, pl.BlockSpec((D_MODEL, t), lambda i: (0, i))],
        out_shape=[jax.ShapeDtypeStruct((s, D_MODEL), MXU_DTYPE), jax.ShapeDtypeStruct((s, D_MODEL), MXU_DTYPE),
                   jax.ShapeDtypeStruct((D_MODEL, s), MXU_DTYPE)],
        scratch_shapes=[pltpu.VMEM((t, D_MODEL), F32)],
        compiler_params=_cparams(("parallel",), 2 * t * NA * 4 + 12 * t * D_MODEL * 4),
    )(proj_a, y_b, gate_b, sgu_g, sgu_beta, wm, bias_full)


def _tok_bwd_call(proj_a, dmerged, y_a, y_b, gate_b, sgu_g, sgu_beta, wm, wmt, bias_full, t):
    s = proj_a.shape[0]

    def body(pa_ref, dm_ref, ya_ref, yb_ref, gb_ref, g_ref, be_ref, wm_ref, wmt_ref, bf_ref,
             dpa_ref, dyb_ref, dgb_ref, dgam_ref, dbeta_ref, dbf_ref, dws_ref, mix_ref, dvn_ref):
        @pl.when(pl.program_id(0) == 0)
        def _():
            dgb_ref[...] = jnp.zeros_like(dgb_ref)
            dgam_ref[...] = jnp.zeros_like(dgam_ref)
            dbeta_ref[...] = jnp.zeros_like(dbeta_ref)
            dbf_ref[...] = jnp.zeros_like(dbf_ref)
            dws_ref[...] = jnp.zeros_like(dws_ref)

        u = pa_ref[:, 0:2048].astype(F32)
        v = pa_ref[:, 2048:4096].astype(F32)
        za = pa_ref[:, 4096:6144].astype(F32)
        xhat, rstd = _layernorm_stats(v)
        vnb = _c(xhat * g_ref[...] + be_ref[...])
        for gi in range(SGU_GROUPS):
            sl = slice(128 * gi, 128 * gi + 128)
            mix_ref[:, sl] = _dot(wm_ref[gi], vnb[:, sl])
        mixed = mix_ref[...] + bf_ref[...]
        sig = _sigmoid(za)
        sz = za * sig
        dm = dm_ref[...].astype(F32)
        y_a = ya_ref[...].astype(F32)
        g0 = _sigmoid(pa_ref[:, 6144:8192].astype(F32) + gb_ref[:, 0:2048])
        g1 = _sigmoid(pa_ref[:, 8192:10240].astype(F32) + gb_ref[:, 2048:4096])
        dgl0 = dm * y_a * g0 * (1.0 - g0)
        dgl1 = dm * yb_ref[...].astype(F32) * g1 * (1.0 - g1)
        dyb_ref[...] = _c(dm * g1)
        dya = dm * g0
        dpa_ref[:, 6144:8192] = _c(dgl0)
        dpa_ref[:, 8192:10240] = _c(dgl1)
        dgb_ref[:, 0:2048] += _fold8(dgl0)
        dgb_ref[:, 2048:4096] += _fold8(dgl1)
        dpa_ref[:, 0:2048] = _c(dya * mixed * sz)
        dpa_ref[:, 4096:6144] = _c(dya * (u * mixed) * (sig * (1.0 + za * (1.0 - sig))))
        dmixed = dya * u * sz
        dbf_ref[...] += dmixed
        dmb = _c(dmixed)
        for gi in range(SGU_GROUPS):
            sl = slice(128 * gi, 128 * gi + 128)
            dvn_ref[:, sl] = _dot(wmt_ref[gi], dmb[:, sl])
            dws_ref[gi] += _dot_nt(dmb[:, sl], vnb[:, sl])
        dvn = dvn_ref[...]
        dgam_ref[...] += _fold8(dvn * xhat)
        dbeta_ref[...] += _fold8(dvn)
        dxh = dvn * g_ref[...]
        dv = rstd * (dxh - jnp.mean(dxh, axis=-1, keepdims=True) - xhat * jnp.mean(dxh * xhat, axis=-1, keepdims=True))
        dpa_ref[:, 2048:4096] = _c(dv)

    row = pl.BlockSpec((t, D_MODEL), lambda i: (i, 0))
    vec = lambda w: pl.BlockSpec((1, w), lambda i: (0, 0))
    acc = lambda w: pl.BlockSpec((8, w), lambda i: (0, 0))
    wspec = pl.BlockSpec((SGU_GROUPS, 128, 128), lambda i: (0, 0, 0))
    return pl.pallas_call(
        body, name="tok_bwd",
        grid=(s // t,),
        in_specs=[pl.BlockSpec((t, NA), lambda i: (i, 0)), row, row, row, vec(4096), vec(2048), vec(2048),
                  wspec, wspec, pl.BlockSpec((128, D_MODEL), lambda i: (0, 0))],
        out_specs=[pl.BlockSpec((t, NA), lambda i: (i, 0)), row, acc(4096), acc(2048), acc(2048),
                   pl.BlockSpec((128, D_MODEL), lambda i: (0, 0)), wspec],
        out_shape=[jax.ShapeDtypeStruct((s, NA), MXU_DTYPE), jax.ShapeDtypeStruct((s, D_MODEL), MXU_DTYPE),
                   jax.ShapeDtypeStruct((8, 4096), F32), jax.ShapeDtypeStruct((8, 2048), F32),
                   jax.ShapeDtypeStruct((8, 2048), F32), jax.ShapeDtypeStruct((128, D_MODEL), F32),
                   jax.ShapeDtypeStruct((SGU_GROUPS, 128, 128), F32)],
        scratch_shapes=[pltpu.VMEM((t, D_MODEL), F32), pltpu.VMEM((t, D_MODEL), F32)],
        compiler_params=_cparams(("arbitrary",), 2 * t * NA * 6 + 16 * t * D_MODEL * 4),
    )(proj_a, dmerged, y_a, y_b, gate_b, sgu_g, sgu_beta, wm, wmt, bias_full)


def _out_call(merged, x, target, w_out, fnw, t):
    s = x.shape[0]
    nt = s // t

    def body(mg_ref, x_ref, t_ref, w_ref, fw_ref, dhb_ref, dmg_ref, loss_ref, dfw_ref):
        @pl.when(pl.program_id(0) == 0)
        def _():
            dfw_ref[...] = jnp.zeros_like(dfw_ref)

        h = x_ref[...] + _dot(mg_ref[...], w_ref[...])
        r = lax.rsqrt(jnp.mean(h * h, axis=-1, keepdims=True) + EPS)
        hn = h * r
        err = hn * fw_ref[...] - t_ref[...]
        loss_ref[...] = jnp.full(loss_ref.shape, 0.5 * jnp.sum(jnp.mean(err * err, axis=-1, keepdims=True)), F32)
        dy = err * (1.0 / D_MODEL)
        dfw_ref[...] += _fold8(dy * hn)
        gw = dy * fw_ref[...]
        dh = r * gw - h * (r * r * r) * jnp.mean(h * gw, axis=-1, keepdims=True)
        dhb = _c(dh)
        dhb_ref[...] = dhb
        dmg_ref[...] = _c(_dot_nt(dhb, w_ref[...]))

    row = pl.BlockSpec((t, D_MODEL), lambda i: (i, 0))
    return pl.pallas_call(
        body, name="out_proj_loss",
        grid=(nt,),
        in_specs=[row, row, row, pl.BlockSpec((D_MODEL, D_MODEL), lambda i: (0, 0)), pl.BlockSpec((1, D_MODEL), lambda i: (0, 0))],
        out_specs=[row, row, pl.BlockSpec((1, 8, 128), lambda i: (i, 0, 0)), pl.BlockSpec((8, D_MODEL), lambda i: (0, 0))],
        out_shape=[jax.ShapeDtypeStruct((s, D_MODEL), MXU_DTYPE),
                   jax.ShapeDtypeStruct((s, D_MODEL), MXU_DTYPE), jax.ShapeDtypeStruct((nt, 8, 128), F32),
                   jax.ShapeDtypeStruct((8, D_MODEL), F32)],
        compiler_params=_cparams(("arbitrary",), 2 * D_MODEL * D_MODEL * 2 + 2 * t * D_MODEL * 20),
    )(merged, x, target, w_out, fnw)


def _ssd_fwd_call(proj_b, dtb, alog, dsk, cw, cb, nw, t, ng):
    s = proj_b.shape[0]
    nt, nch = s // t, t // CHUNK

    def body(pb_ref, halo_ref, dtb_ref, al_ref, ds_ref, cw_ref, cb_ref, nw_ref, y_ref, yb_ref, hp_ref, pre_ref,
             dt_ref, acs_ref, ht_ref, prev_ref):
        i = pl.program_id(1)

        @pl.when(i == 0)
        def _():
            ht_ref[...] = jnp.zeros_like(ht_ref)

        for gi in range(ng):
            prev_ref[:, 768 * gi:768 * gi + 768] = jnp.where(i == 0, 0.0, halo_ref[:, FW_B * gi + XBC_O:FW_B * gi + DT_O])
        masks = _ssd_masks()
        tri, _ = _cumsum_mats(t)
        a_neg = -jnp.exp(al_ref[...])
        expand = _head_expand_mat()
        for gi in range(ng):
            fo, go, no = FW_B * gi, SSD_GW * gi, 128 * gi
            dt_n = _softplus(pb_ref[:, fo + DT_O:fo + DT_O + 128] + dtb_ref[:, no:no + 128])
            acs_n = _dot01_l(tri, dt_n * a_neg[:, no:no + 128], 3)
            dt_ref[:, go:go + 512] = _dot01_r(dt_n, expand, 2)
            acs_ref[:, go:go + 512] = _dot01_r(acs_n, expand, 2)

        def chunk(c, carry):
            rows = pl.ds(pl.multiple_of(c * CHUNK, CHUNK), CHUNK)
            for gi in range(ng):
                fo, co, go, no = FW_B * gi, 768 * gi, SSD_GW * gi, 128 * gi
                xbc = pb_ref[rows, fo + XBC_O:fo + DT_O]
                taps = _conv_taps(jnp.concatenate([prev_ref[:, co:co + 768], xbc], axis=0), CHUNK)
                prev_ref[:, co:co + 768] = xbc[CHUNK - 8:CHUNK]
                pre = cb_ref[:, co:co + 768]
                for k in range(4):
                    pre = pre + taps[k] * cw_ref[k:k + 1, co:co + 768]
                pre_ref[rows, co:co + 768] = pre
                act = pre * _sigmoid(pre)
                dt = dt_ref[rows, go:go + 512]
                acs = acs_ref[rows, go:go + 512]
                ht = ht_ref[gi]
                hp_ref[c, :, go:go + 512] = ht
                y, ht_new = _ssd_chunk_fwd(act[:, 0:512], act[:, 512:640], act[:, 640:768], dt, acs,
                                           ds_ref[:, go:go + 512], ht, masks)
                y_ref[rows, go:go + 512] = y
                ht_ref[gi] = ht_new
                zb = pb_ref[rows, fo:fo + 512]
                hh = y * (zb * _sigmoid(zb))
                rr = lax.rsqrt(jnp.mean(hh * hh, axis=-1, keepdims=True) + EPS)
                yb_ref[rows, go:go + 512] = _c(hh * rr * nw_ref[:, go:go + 512])
            return carry

        lax.fori_loop(0, nch, chunk, 0)

    gvec = lambda w: pl.BlockSpec((1, ng * w), lambda g, i: (0, g))
    return pl.pallas_call(
        body, name="ssd_fwd",
        grid=(SSD_GROUPS // ng, nt),
        in_specs=[pl.BlockSpec((t, ng * FW_B), lambda g, i: (i, g)),
                  pl.BlockSpec((8, ng * FW_B), lambda g, i: (jnp.maximum(i * (t // 8) - 1, 0), g)),
                  gvec(128), gvec(128), gvec(512),
                  pl.BlockSpec((4, ng * 768), lambda g, i: (0, g)), gvec(768), gvec(512)],
        out_specs=[pl.BlockSpec((t, ng * SSD_GW), lambda g, i: (i, g)), pl.BlockSpec((t, ng * SSD_GW), lambda g, i: (i, g)),
                   pl.BlockSpec((nch, SSD_STATE, ng * SSD_GW), lambda g, i: (i, 0, g)),
                   pl.BlockSpec((t, ng * 768), lambda g, i: (i, g)),
                   pl.BlockSpec((t, ng * SSD_GW), lambda g, i: (i, g)), pl.BlockSpec((t, ng * SSD_GW), lambda g, i: (i, g))],
        out_shape=[jax.ShapeDtypeStruct((s, D_MODEL), F32), jax.ShapeDtypeStruct((s, D_MODEL), MXU_DTYPE),
                   jax.ShapeDtypeStruct((s // CHUNK, SSD_STATE, D_MODEL), F32),
                   jax.ShapeDtypeStruct((s, SSD_GROUPS * 768), F32),
                   jax.ShapeDtypeStruct((s, D_MODEL), F32), jax.ShapeDtypeStruct((s, D_MODEL), F32)],
        scratch_shapes=[pltpu.VMEM((ng, SSD_STATE, SSD_GW), F32), pltpu.VMEM((8, ng * 768), F32)],
        compiler_params=_cparams(("parallel", "arbitrary"), ng * (2 * t * FW_B * 4 + 16 * t * SSD_GW * 4) + 16 * 1024 * 1024),
    )(proj_b, proj_b, dtb, alog, dsk, cw, cb, nw)


def _ssd_bwd_call(proj_b, pre_all, dt_all, acs_all, dyb, y, hprev, dtb, alog, dsk, cw, nw, t, ng):
    s = proj_b.shape[0]
    nt, nch = s // t, t // CHUNK

    def body(pb_ref, pre_ref, dt_ref, acs_ref, dyb_ref, y_ref, hp_ref, dtb_ref, al_ref, ds_ref, cw_ref, nw_ref,
             dpb_ref, a512_ref, a768_ref, dht_ref, nxt_ref, q_ref, p1_ref):
        i = pl.program_id(1)

        @pl.when(i == 0)
        def _():
            dht_ref[...] = jnp.zeros_like(dht_ref)
            nxt_ref[...] = jnp.zeros_like(nxt_ref)
            a512_ref[...] = jnp.zeros_like(a512_ref)
            a768_ref[...] = jnp.zeros_like(a768_ref)

        _, trit = _cumsum_mats(t)
        a_neg = -jnp.exp(al_ref[...])
        masks = _ssd_masks()

        def chunk(cc, carry):
            c = nch - 1 - cc
            rows = pl.ds(pl.multiple_of(c * CHUNK, CHUNK), CHUNK)
            for gi in range(ng):
                fo, co, go, bo = FW_B * gi, 768 * gi, SSD_GW * gi, BW_B * gi
                pre = pre_ref[rows, co:co + 768]
                sp = _sigmoid(pre)
                act = pre * sp
                zb = pb_ref[rows, fo:fo + 512]
                yv = y_ref[rows, go:go + 512]
                sgz = _sigmoid(zb)
                sz = zb * sgz
                hh = yv * sz
                rr = lax.rsqrt(jnp.mean(hh * hh, axis=-1, keepdims=True) + EPS)
                dyb = dyb_ref[rows, go:go + 512].astype(F32)
                a512_ref[gi, 0] += _fold8(dyb * (hh * rr))
                tt = dyb * nw_ref[:, go:go + 512]
                dhh = rr * tt - hh * (rr * rr * rr) * jnp.mean(hh * tt, axis=-1, keepdims=True)
                dpb_ref[rows, bo:bo + 512] = _c(dhh * yv * (sgz * (1.0 + zb * (1.0 - sgz))))
                dxs, d_b, d_c, dht_prev, dyxs, qq, p1 = _ssd_chunk_bwd(
                    act[:, 0:512], act[:, 512:640], act[:, 640:768], dt_ref[rows, go:go + 512], acs_ref[rows, go:go + 512],
                    ds_ref[:, go:go + 512], hp_ref[c, :, go:go + 512], dht_ref[gi], dhh * sz, masks)
                dht_ref[gi] = dht_prev
                q_ref[rows, go:go + 512] = qq
                p1_ref[rows, go:go + 512] = p1
                a512_ref[gi, 1] += _fold8(dyxs)
                dpre = jnp.concatenate([dxs, d_b, d_c], axis=1) * (sp * (1.0 + pre * (1.0 - sp)))
                xbc = pb_ref[rows, fo + XBC_O:fo + DT_O]
                a768_ref[gi, 4] += _fold8(dpre)
                a768_ref[gi, 3] += _fold8(dpre * xbc)
                dpad = jnp.concatenate([dpre, nxt_ref[:, co:co + 768]], axis=0)
                dx = dpre * cw_ref[3:4, co:co + 768]
                for k in range(3):
                    d_k = pltpu.roll(dpad, CHUNK + 8 - (3 - k), 0)[0:CHUNK]
                    dx = dx + d_k * cw_ref[k:k + 1, co:co + 768]
                    a768_ref[gi, k] += _fold8(d_k * xbc)
                nxt_ref[:, co:co + 768] = dpre[0:8]
                dpb_ref[rows, bo + 512:bo + 1280] = _c(dx)
            return carry

        lax.fori_loop(0, nch, chunk, 0)

        rsel = _c(jnp.where(lax.shift_right_logical(_iota((SSD_GW, 128), 0), 6) == _iota((SSD_GW, 128), 1), 1.0, 0.0))
        for gi in range(ng):
            fo, go, bo, no = FW_B * gi, SSD_GW * gi, BW_B * gi, 128 * gi
            ddt, dadt = _ssd_finish_dt(q_ref[:, go:go + 512], p1_ref[:, go:go + 512], dt_ref[:, go:go + 512],
                                       a_neg[:, go:go + 512], trit, masks[3])
            sig_n = _sigmoid(pb_ref[:, fo + DT_O:fo + DT_O + 128] + dtb_ref[:, no:no + 128])
            ddtr_n = _dot01_r(ddt, rsel, 2) * sig_n
            dpb_ref[:, bo + DT_O:bo + DT_O + 128] = _c(ddtr_n)
            a512_ref[gi, 2] += _fold8(dadt)
            a512_ref[gi, 3, :, 0:128] += _fold8(ddtr_n)

    gvec = lambda w: pl.BlockSpec((1, ng * w), lambda g, i: (0, g))
    rev = lambda w: pl.BlockSpec((t, ng * w), lambda g, i: (nt - 1 - i, g))
    return pl.pallas_call(
        body, name="ssd_bwd",
        grid=(SSD_GROUPS // ng, nt),
        in_specs=[rev(FW_B), rev(768), rev(SSD_GW), rev(SSD_GW), rev(SSD_GW), rev(SSD_GW),
                  pl.BlockSpec((nch, SSD_STATE, ng * SSD_GW), lambda g, i: (nt - 1 - i, 0, g)),
                  gvec(128), gvec(512), gvec(512),
                  pl.BlockSpec((4, ng * 768), lambda g, i: (0, g)), gvec(512)],
        out_specs=[rev(BW_B),
                   pl.BlockSpec((ng, 4, 8, 512), lambda g, i: (g, 0, 0, 0)),
                   pl.BlockSpec((ng, 5, 8, 768), lambda g, i: (g, 0, 0, 0))],
        out_shape=[jax.ShapeDtypeStruct((s, SSD_GROUPS * BW_B), MXU_DTYPE),
                   jax.ShapeDtypeStruct((SSD_GROUPS, 4, 8, 512), F32),
                   jax.ShapeDtypeStruct((SSD_GROUPS, 5, 8, 768), F32)],
        scratch_shapes=[pltpu.VMEM((ng, SSD_STATE, SSD_GW), F32), pltpu.VMEM((8, ng * 768), F32),
                        pltpu.VMEM((t, ng * SSD_GW), F32), pltpu.VMEM((t, ng * SSD_GW), F32)],
        compiler_params=_cparams(("parallel", "arbitrary"), ng * (2 * t * FW_B * 4 + 18 * t * SSD_GW * 4) + 16 * 1024 * 1024),
    )(proj_b, pre_all, dt_all, acs_all, dyb, y, hprev, dtb, alog, dsk, cw, nw)


def _rows_call(body, ins, outs, tr, name):
    r = ins[0].shape[0]
    spec = lambda a: pl.BlockSpec((tr, a.shape[1]), lambda i: (i, 0))
    est = 2 * tr * sum(a.shape[1] * jnp.dtype(a.dtype).itemsize for a in list(ins) + list(outs))
    return pl.pallas_call(
        body, name=name, grid=(r // tr,),
        in_specs=[spec(a) for a in ins], out_specs=[spec(o) for o in outs], out_shape=list(outs),
        compiler_params=_cparams(("parallel",), est),
    )(*ins)


def _add_pair(a, b, tr, name):
    def body(a_ref, b_ref, o_ref):
        o_ref[...] = a_ref[...] + b_ref[...]

    return _rows_call(body, [a, b], [jax.ShapeDtypeStruct(a.shape, F32)], tr, name)[0]


def _rs_add(p, sib, place, tr, name):
    _, r, c = p.shape
    half = r // 2
    nb = half // tr

    def body(pl_ref, p_ref, s_ref, b_ref, own_ref):
        v = p_ref[0] + s_ref[0]
        b_ref[0] = v.astype(jnp.bfloat16)

        @pl.when(pl.program_id(1) == pl_ref[0])
        def _():
            own_ref[...] = v

    return pl.pallas_call(
        body, name=name,
        grid_spec=pltpu.PrefetchScalarGridSpec(
            num_scalar_prefetch=1, grid=(nb, 4),
            in_specs=[pl.BlockSpec((1, tr, c), lambda i, k, pr: (k, pr[1] * nb + i, 0)),
                      pl.BlockSpec((1, tr, c), lambda i, k, pr: (k, i, 0))],
            out_specs=[pl.BlockSpec((1, tr, c), lambda i, k, pr: (k, i, 0)),
                       pl.BlockSpec((tr, c), lambda i, k, pr: (i, 0))]),
        out_shape=[jax.ShapeDtypeStruct((4, half, c), jnp.bfloat16), jax.ShapeDtypeStruct((half, c), F32)],
        compiler_params=_cparams(("parallel", "arbitrary"), 2 * tr * c * 14),
    )(place, p, sib)


WIN_STEP = 3840
WIN_W = 3968


def _rs_add_windows(dw, sib, place, tr, name):
    r = dw.shape[0]
    half = r // 2
    nb = half // tr
    tail = WIN_W - WIN_STEP

    def body(pl_ref, pm_ref, pt_ref, s_ref, b_ref, own_ref):
        vm = pm_ref[...].astype(F32) + s_ref[0, :, 0:WIN_STEP].astype(F32)
        vt = pt_ref[...].astype(F32) + s_ref[0, :, WIN_STEP:WIN_W].astype(F32)
        b_ref[0, :, 0:WIN_STEP] = vm.astype(jnp.bfloat16)
        b_ref[0, :, WIN_STEP:WIN_W] = vt.astype(jnp.bfloat16)

        @pl.when(pl.program_id(1) == pl_ref[0])
        def _():
            own_ref[:, 0:WIN_STEP] = vm
            own_ref[:, WIN_STEP:WIN_W] = vt

    return pl.pallas_call(
        body, name=name,
        grid_spec=pltpu.PrefetchScalarGridSpec(
            num_scalar_prefetch=1, grid=(nb, 4),
            in_specs=[pl.BlockSpec((tr, WIN_STEP), lambda i, k, pr: (pr[1] * nb + i, k)),
                      pl.BlockSpec((tr, tail), lambda i, k, pr: (pr[1] * nb + i, (WIN_STEP // tail) * (k + 1))),
                      pl.BlockSpec((1, tr, WIN_W), lambda i, k, pr: (k, i, 0))],
            out_specs=[pl.BlockSpec((1, tr, WIN_W), lambda i, k, pr: (k, i, 0)),
                       pl.BlockSpec((tr, WIN_W), lambda i, k, pr: (i, 0))]),
        out_shape=[jax.ShapeDtypeStruct((4, half, WIN_W), jnp.bfloat16), jax.ShapeDtypeStruct((half, WIN_W), F32)],
        compiler_params=_cparams(("parallel", "arbitrary"), 2 * tr * WIN_W * 14),
    )(place, dw, dw, sib)


def _sum_own_recv(own, recv, tr, name):
    r, c = own.shape

    def body(o_ref, r_ref, out_ref):
        v = o_ref[...]
        for j in range(3):
            v = v + r_ref[j].astype(F32)
        out_ref[...] = v

    return pl.pallas_call(
        body, name=name, grid=(r // tr,),
        in_specs=[pl.BlockSpec((tr, c), lambda i: (i, 0)), pl.BlockSpec((3, tr, c), lambda i: (0, i, 0))],
        out_specs=pl.BlockSpec((tr, c), lambda i: (i, 0)),
        out_shape=jax.ShapeDtypeStruct((r, c), F32),
        compiler_params=_cparams(("parallel",), 2 * tr * c * 14),
    )(own, recv)


def _sum_slots(stack, name):
    n, r, w = stack.shape

    def body(a_ref, out_ref):
        v = a_ref[0]
        for k in range(1, n):
            v = v + a_ref[k]
        out_ref[...] = v

    return pl.pallas_call(
        body, name=name, grid=(1,),
        in_specs=[pl.BlockSpec((n, r, w), lambda i: (0, 0, 0))],
        out_specs=pl.BlockSpec((r, w), lambda i: (0, 0)),
        out_shape=jax.ShapeDtypeStruct((r, w), F32),
        compiler_params=_cparams(("arbitrary",), 2 * (n + 1) * r * w * 4),
    )(stack)


def _adamw(w, g, m, v, tr, name):
    def body(w_ref, g_ref, m_ref, v_ref, d_ref, nm_ref, nv_ref):
        d_ref[...], nm_ref[...], nv_ref[...] = _adam_math(w_ref[...], g_ref[...], m_ref[...], v_ref[...])

    o = jax.ShapeDtypeStruct(w.shape, F32)
    return _rows_call(body, [w, g, m, v], [o, o, o], tr, name)


def _adam_math(w, g, m, v):
    nm = ADAM_B1 * m + (1.0 - ADAM_B1) * g
    nv = ADAM_B2 * v + (1.0 - ADAM_B2) * (g * g)
    m_hat = nm / (1.0 - ADAM_B1 ** ADAM_STEP)
    v_hat = nv / (1.0 - ADAM_B2 ** ADAM_STEP)
    return -ADAM_LR * (m_hat / (jnp.sqrt(v_hat) + ADAM_EPS) + ADAM_WD * w), nm, nv


def _adamw_halves(w, g_own, g_sib, m, v, place, tr, name):
    r, c = w.shape

    def body(pl_ref, w_ref, go_ref, gs_ref, m_ref, v_ref, g_ref, d_ref, nm_ref, nv_ref):
        first = pl_ref[1] == 0
        own, sib = go_ref[...], gs_ref[...]
        g = jnp.concatenate([jnp.where(first, own, sib), jnp.where(first, sib, own)], axis=1)
        g_ref[...] = g
        d_ref[...], nm_ref[...], nv_ref[...] = _adam_math(w_ref[...], g, m_ref[...], v_ref[...])

    full = pl.BlockSpec((tr, c), lambda i, pr: (i, 0))
    half = pl.BlockSpec((tr, c // 2), lambda i, pr: (i, 0))
    o = jax.ShapeDtypeStruct((r, c), F32)
    return pl.pallas_call(
        body, name=name,
        grid_spec=pltpu.PrefetchScalarGridSpec(num_scalar_prefetch=1, grid=(r // tr,),
                                               in_specs=[full, half, half, full, full], out_specs=[full] * 4),
        out_shape=[o] * 4,
        compiler_params=_cparams(("parallel",), 2 * tr * c * 4 * 8),
    )(place, w, g_own, g_sib, m, v)


ANY = pl.BlockSpec(memory_space=pl.ANY)


def _place():
    x, y, c = lax.axis_index("x"), lax.axis_index("y"), lax.axis_index("c")
    others = [(1 - x, y), (x, 1 - y), (1 - x, 1 - y)]
    return x, y, c, 2 * x + y, others


def _remote(src, dst, send, recv, k, to):
    return pltpu.make_async_remote_copy(src_ref=src, dst_ref=dst, send_sem=send.at[k], recv_sem=recv.at[k],
                                        device_id=to, device_id_type=MESH)


def _norm_gather_call(x, norm_w, win_b, wout_b, cw8, tm):
    s = x.shape[0]
    ni = s // tm
    h_in, h_out = win_b.shape[0] // 2, wout_b.shape[0] // 2
    q_in = h_in // 2

    def body(x_ref, w_ref, win, wout, cw, xn_ref, xnt_ref, g_in, g_out, g_cw, send, recv):
        i = pl.program_id(0)

        def direct():
            xx, yy, c, me, others = _place()
            mi, mo = pl.ds(c * h_in, h_in), pl.ds(c * h_out, h_out)
            cps = [_remote(win.at[mi], g_in.at[me, mi], send, recv, j, (*others[j], c)) for j in range(2)]
            cps += [_remote(wout.at[mo], g_out.at[me, mo], send, recv, 7 + j, (*chip, c)) for j, chip in enumerate(others)]
            cps += [_remote(cw, g_cw.at[me], send, recv, 13 + j, (*chip, c)) for j, chip in enumerate(others)]
            return cps

        @pl.when(i == 0)
        def _():
            for cp in direct():
                cp.start()

        xv = x_ref[...]
        r = lax.rsqrt(jnp.mean(xv * xv, axis=-1, keepdims=True) + EPS)
        xn = xv * r * w_ref[...]
        xn_ref[...] = _c(xn)
        xnt_ref[...] = _c(xn.T)

        @pl.when(i == ni - 1)
        def _():
            xx, yy, c, me, others = _place()
            sib = (xx, yy, 1 - c)
            kx, ky, kd = (2 * chip[0] + chip[1] for chip in others)
            mi, ti = pl.ds(c * h_in, h_in), pl.ds((1 - c) * h_in, h_in)
            quarter = [pl.ds(c * h_in, q_in), pl.ds(c * h_in + q_in, q_in)]
            started = []

            def go(cp):
                cp.start()
                started.append(cp)

            _remote(g_in.at[kx, mi], g_in.at[kx, mi], send, recv, 0, (*others[0], c)).wait_recv()
            go(_remote(g_in.at[kx, quarter[0]], g_in.at[kx, quarter[0]], send, recv, 2, (*others[1], c)))
            go(_remote(g_in.at[kx, mi], g_in.at[kx, mi], send, recv, 4, sib))
            _remote(g_in.at[ky, mi], g_in.at[ky, mi], send, recv, 1, (*others[1], c)).wait_recv()
            go(_remote(g_in.at[ky, quarter[1]], g_in.at[ky, quarter[1]], send, recv, 3, (*others[0], c)))
            go(_remote(g_in.at[ky, mi], g_in.at[ky, mi], send, recv, 5, sib))
            mo, to = pl.ds(c * h_out, h_out), pl.ds((1 - c) * h_out, h_out)
            for j, chip in enumerate(others):
                kj = 2 * chip[0] + chip[1]
                _remote(g_out.at[kj, mo], g_out.at[kj, mo], send, recv, 7 + j, (*chip, c)).wait_recv()
                go(_remote(g_out.at[kj, mo], g_out.at[kj, mo], send, recv, 10 + j, sib))
            _remote(g_in.at[kd, quarter[0]], g_in.at[kd, quarter[0]], send, recv, 2, (*others[1], c)).wait_recv()
            _remote(g_in.at[kd, quarter[1]], g_in.at[kd, quarter[1]], send, recv, 3, (*others[0], c)).wait_recv()
            go(_remote(g_in.at[kd, mi], g_in.at[kd, mi], send, recv, 6, sib))
            for k_src, sem in ((kx, 4), (ky, 5), (kd, 6)):
                _remote(g_in.at[k_src, ti], g_in.at[k_src, ti], send, recv, sem, sib).wait_recv()
            for j, chip in enumerate(others):
                kj = 2 * chip[0] + chip[1]
                _remote(g_out.at[kj, to], g_out.at[kj, to], send, recv, 10 + j, sib).wait_recv()
                _remote(cw, g_cw.at[kj], send, recv, 13 + j, (*chip, c)).wait_recv()
            for cp in direct() + started:
                cp.wait_send()

    outs = [jax.ShapeDtypeStruct((s, D_MODEL), MXU_DTYPE), jax.ShapeDtypeStruct((D_MODEL, s), MXU_DTYPE)]
    outs += [jax.ShapeDtypeStruct((4,) + a.shape, a.dtype) for a in (win_b, wout_b, cw8)]
    return pl.pallas_call(
        body, name="rmsnorm_gather_weights",
        grid=(ni,),
        in_specs=[pl.BlockSpec((tm, D_MODEL), lambda i: (i, 0)), pl.BlockSpec((1, D_MODEL), lambda i: (0, 0)), ANY, ANY, ANY],
        out_specs=[pl.BlockSpec((tm, D_MODEL), lambda i: (i, 0)), pl.BlockSpec((D_MODEL, tm), lambda i: (0, i)), ANY, ANY, ANY],
        out_shape=outs,
        scratch_shapes=[pltpu.SemaphoreType.DMA((16,)), pltpu.SemaphoreType.DMA((16,))],
        compiler_params=_cparams(("arbitrary",), 2 * tm * D_MODEL * 12),
    )(x, norm_w, win_b, wout_b, cw8)


def _dw_out_rs_call(a, b, dw, *, tn, tk):
    m, k = a.shape
    n = b.shape[1]
    nj, nk = n // tn, k // tk
    half = dw.shape[0] // 2

    def body(a_ref, b_ref, pin, o_ref, sib_in, send, recv):
        j, kk = pl.program_id(0), pl.program_id(1)

        def copies():
            x, y, c, me, others = _place()
            rows = pl.ds((1 - c) * half, half)
            return [_remote(pin.at[rows, pl.ds(WIN_STEP * w, WIN_W)], sib_in.at[w], send, recv, w, (x, y, 1 - c)) for w in range(4)]

        @pl.when((j == 0) & (kk == 0))
        def _():
            for cp in copies():
                cp.start()

        @pl.when(kk == 0)
        def _():
            o_ref[...] = jnp.zeros_like(o_ref)

        o_ref[...] += _dot(a_ref[...], b_ref[...])

        @pl.when((j == nj - 1) & (kk == nk - 1))
        def _():
            cps = copies()
            for cp in cps:
                cp.wait_recv()
            for cp in cps:
                cp.wait_send()

    isz = jnp.dtype(a.dtype).itemsize
    est = 2 * (m * tk + tk * tn) * isz + 2 * m * tn * 4
    outs = [jax.ShapeDtypeStruct((m, n), F32), jax.ShapeDtypeStruct((4, half, WIN_W), dw.dtype)]
    return pl.pallas_call(
        body, name="dw_out_rs_sibling",
        grid=(nj, nk),
        in_specs=[pl.BlockSpec((m, tk), lambda j, kk: (0, kk)), pl.BlockSpec((tk, tn), lambda j, kk: (kk, j)), ANY],
        out_specs=[pl.BlockSpec((m, tn), lambda j, kk: (0, j)), ANY],
        out_shape=outs,
        scratch_shapes=[pltpu.SemaphoreType.DMA((4,)), pltpu.SemaphoreType.DMA((4,))],
        compiler_params=_cparams(("arbitrary", "arbitrary"), est),
    )(a, b, dw)


def _rs_sibling_call(p_out, vsmall):
    def body(pout, vs, sib_out, sib_v, send, recv):
        x, y, c, me, others = _place()
        sib = (x, y, 1 - c)
        half = pout.shape[1] // 2
        cps = [_remote(pout.at[:, pl.ds((1 - c) * half, half)], sib_out, send, recv, 0, sib),
               _remote(vs, sib_v, send, recv, 1, sib)]
        for cp in cps:
            cp.start()
        for cp in cps:
            cp.wait_recv()
        for cp in cps:
            cp.wait_send()

    outs = [jax.ShapeDtypeStruct((4, p_out.shape[1] // 2, p_out.shape[2]), p_out.dtype),
            jax.ShapeDtypeStruct(vsmall.shape, vsmall.dtype)]
    return pl.pallas_call(
        body, name="rs_sibling",
        in_specs=[ANY] * 2, out_specs=[ANY] * 2, out_shape=outs,
        scratch_shapes=[pltpu.SemaphoreType.DMA((2,)), pltpu.SemaphoreType.DMA((2,))],
    )(p_out, vsmall)


def _rs_join_call(f_in, f_out, nw8):
    def body(fin, fout, nw, sib_in, full_out, all_nw, send, recv):
        x, y, c, me, others = _place()
        sib = (x, y, 1 - c)
        half = fout.shape[0]
        cps = [_remote(fin, sib_in, send, recv, 0, sib),
               _remote(fout, full_out.at[pl.ds(c * half, half)], send, recv, 1, sib)]
        mine = 4 * x + 2 * y + c
        peers = []
        for r in range(1, 8):
            px, py, pc = (1 - x if r & 4 else x), (1 - y if r & 2 else y), (1 - c if r & 1 else c)
            peers.append((r, (px, py, pc), 4 * px + 2 * py + pc))
            cps.append(_remote(nw, all_nw.at[mine], send, recv, 1 + r, (px, py, pc)))
        for cp in cps:
            cp.start()
        cps[0].wait_recv()
        _remote(fout, full_out.at[pl.ds((1 - c) * half, half)], send, recv, 1, sib).wait_recv()
        for r, peer, idx in peers:
            _remote(nw, all_nw.at[idx], send, recv, 1 + r, peer).wait_recv()
        for cp in cps:
            cp.wait_send()

    outs = [jax.ShapeDtypeStruct(f_in.shape, F32), jax.ShapeDtypeStruct((2 * f_out.shape[0], f_out.shape[1]), F32),
            jax.ShapeDtypeStruct((8,) + nw8.shape, F32)]
    return pl.pallas_call(
        body, name="rs_join",
        in_specs=[ANY] * 3, out_specs=[ANY] * 3, out_shape=outs,
        scratch_shapes=[pltpu.SemaphoreType.DMA((9,)), pltpu.SemaphoreType.DMA((9,))],
    )(f_in, f_out, nw8)


def _pack(arrs):
    parts = []
    for a in arrs:
        f = a.reshape(-1).astype(F32)
        pad = (-f.shape[0]) % 1024
        parts.append(jnp.pad(f, (0, pad)).reshape(-1, 128))
    return jnp.concatenate(parts, axis=0)


def _unpack(packed, shapes):
    out, row = [], 0
    for shp in shapes:
        n = 1
        for d in shp:
            n *= d
        rows = (n + 1023) // 1024 * 8
        out.append(packed[row:row + rows].reshape(-1)[:n].reshape(shp))
        row += rows
    return out


def _expand_heads(v32):
    return jnp.repeat(v32.reshape(32), HEADDIM).reshape(1, D_MODEL)


def kernel(x, norm_w, w_in, gate_b, sgu_norm_g, sgu_norm_b, sgu_w, sgu_b, conv_w, conv_b, dt_bias, A_log, D_skip, ssd_norm_w, w_out, final_norm_w, loss_target, m_norm_w, m_w_in, m_gate_b, m_sgu_norm_g, m_sgu_norm_b, m_sgu_w, m_sgu_b, m_conv_w, m_conv_b, m_dt_bias, m_A_log, m_D_skip, m_ssd_norm_w, m_w_out, m_final_norm_w, v_norm_w, v_w_in, v_gate_b, v_sgu_norm_g, v_sgu_norm_b, v_sgu_w, v_sgu_b, v_conv_w, v_conv_b, v_dt_bias, v_A_log, v_D_skip, v_ssd_norm_w, v_w_out, v_final_norm_w):
    s = x.shape[1]
    x2 = x.reshape(s, D_MODEL)
    tgt = loss_target.reshape(s, D_MODEL)
    t_ssd, t_tok, t_out, t_row = min(T_SSD, s), min(T_TOK, s), min(T_OUT, s), min(T_ROW, s)
    tm_mm, tk_dw = min(TM_MM, s), min(TK_DW, s)
    chip = 2 * lax.axis_index("x") + lax.axis_index("y")

    cw8 = jnp.pad(conv_w[0], ((0, 4), (0, 0)))
    win_b, wout_b = _c(w_in[0]), _c(w_out[0])
    xn, xnt, g_in, g_out, g_cw = _norm_gather_call(x2, norm_w, win_b, wout_b, cw8, t_row)
    g_in = lax.dynamic_update_index_in_dim(g_in, win_b, chip, 0)
    g_out = lax.dynamic_update_index_in_dim(g_out, wout_b, chip, 0)
    g_cw = lax.dynamic_update_index_in_dim(g_cw, cw8, chip, 0)
    wt = jnp.transpose(g_in, (0, 2, 1)).reshape(IN_W, D_MODEL)
    w_out_full = g_out.reshape(D_MODEL, D_MODEL)
    conv_w_full = jnp.transpose(g_cw[:, 0:4, :], (1, 0, 2)).reshape(4, 3072)

    wt_a = jnp.concatenate([wt[0:6144], wt[11296:15392]], axis=0)
    per_group = lambda lo, n: wt[lo:lo + SSD_GROUPS * n].reshape(SSD_GROUPS, n, D_MODEL)
    wt_b = jnp.concatenate([per_group(6144, 512), per_group(8192, 512), per_group(10240, 128), per_group(10752, 128),
                            jnp.pad(per_group(11264, 8), ((0, 0), (0, 120), (0, 0)))], axis=1)
    wt_b = wt_b.reshape(SSD_GROUPS * BW_B, D_MODEL)

    def group_cols(full_xs, full_bc):
        parts = []
        for g in range(SSD_GROUPS):
            parts += [full_xs[:, 512 * g:512 * g + 512], full_bc[:, 128 * g:128 * g + 128], full_bc[:, 512 + 128 * g:512 + 128 * g + 128]]
        return jnp.concatenate(parts, axis=1)

    cw_g = group_cols(conv_w_full[:, 0:2048], conv_w_full[:, 2048:3072])
    cb_g = group_cols(conv_b[:, 0:2048], conv_b[:, 2048:3072])
    alog_e, dsk_e = _expand_heads(A_log), _expand_heads(D_skip)
    narrow = lambda v32: jnp.pad(v32.reshape(SSD_GROUPS, 8), ((0, 0), (0, 120))).reshape(1, SSD_GROUPS * 128)
    dtb_n, alog_n = narrow(dt_bias), narrow(A_log)

    pos_chunk = jnp.arange(SGU_BLOCK) // CHUNK
    smask = pos_chunk[None, :] <= pos_chunk[:, None]
    wm_f = jnp.where(smask[None], sgu_w[0], 0.0)
    wm = _c(wm_f)
    wmt = _c(jnp.transpose(wm_f, (0, 2, 1)))
    bias_full = jnp.repeat(sgu_b[0].T, D_MODEL // SGU_GROUPS, axis=1)
    fnw = final_norm_w.reshape(1, D_MODEL)

    proj_a = _mm(xn, wt_a, tm=tm_mm, tn=1024, tk=D_MODEL, name="in_proj_a", out_dtype=MXU_DTYPE, b_is_t=True)
    proj_b = _mm(xn, wt_b, tm=tm_mm, tn=BW_B, tk=D_MODEL, name="in_proj_b", b_is_t=True)
    y_ssd, y_b, hprev, pre_all, dt_all, acs_all = _ssd_fwd_call(proj_b, dtb_n, alog_n, dsk_e, cw_g, cb_g, ssd_norm_w, t_ssd, NG_SSD)
    y_a, merged, merged_t = _tok_fwd_call(proj_a, y_b, gate_b, sgu_norm_g, sgu_norm_b, wm, bias_full, t_tok)
    dh_b, dmerged, loss_t, dfw8 = _out_call(merged, x2, tgt, w_out_full, fnw, t_out)

    dproj_a, dy_b, dgb8, dgam8, dbeta8, dbfull, dws = _tok_bwd_call(
        proj_a, dmerged, y_a, y_b, gate_b, sgu_norm_g, sgu_norm_b, wm, wmt, bias_full, t_tok)
    dproj_b, a512, a768 = _ssd_bwd_call(proj_b, pre_all, dt_all, acs_all, dy_b, y_ssd, hprev, dtb_n, alog_e, dsk_e, cw_g,
                                        ssd_norm_w, t_ssd, NG_SSD)
    dw_uvz = _mm(xnt, dproj_a, tm=D_MODEL, tn=1024, tk=min(2 * tk_dw, s), name="dw_in_uvz", n=6144, out_dtype=MXU_DTYPE)
    dw_gate = _mm(xnt, dproj_a, tm=D_MODEL, tn=1024, tk=min(2 * tk_dw, s), name="dw_in_gate", col0=6, n=4096,
                  out_dtype=MXU_DTYPE)
    dw_zb, dw_xs, dw_bm, dw_cm, dw_dt = _dw_groups(xnt, dproj_b, tk=tk_dw, out_dtype=MXU_DTYPE)

    dw_dt32 = jnp.concatenate([dw_dt[:, 128 * g:128 * g + 8] for g in range(SSD_GROUPS)], axis=1)
    dw_ref = jnp.concatenate([dw_uvz, dw_zb, dw_xs, dw_bm, dw_cm, dw_dt32, dw_gate,
                              jnp.zeros((D_MODEL, 3 * WIN_STEP + WIN_W - IN_W), MXU_DTYPE)], axis=1)
    dw_out_p, sib_i = _dw_out_rs_call(merged_t, dh_b, dw_ref, tn=1024, tk=tk_dw)
    p_out = dw_out_p.reshape(4, D_MODEL // 4, D_MODEL)

    s512 = jnp.sum(a512, axis=2)
    heads = lambda v: jnp.sum(v.reshape(32, HEADDIM), axis=1).reshape(1, 32)
    d_ssd_nw = s512[:, 0].reshape(1, D_MODEL)
    d_dskip = heads(s512[:, 1].reshape(D_MODEL))
    d_alog = heads(s512[:, 2].reshape(D_MODEL)) * (1.0 / HEADDIM) * (-jnp.exp(A_log))
    d_dtb = s512[:, 3, 0:8].reshape(1, 32)
    s768 = jnp.sum(a768, axis=2)
    ungroup = lambda v: jnp.concatenate([v[g, :, 0:512] for g in range(4)] + [v[g, :, 512:640] for g in range(4)]
                                        + [v[g, :, 640:768] for g in range(4)], axis=1)
    d_cw = ungroup(s768[:, 0:4])
    d_cb = ungroup(s768[:, 4:5])
    d_sgu_b = jnp.sum(dbfull.reshape(128, SGU_GROUPS, 128), axis=2).T.reshape(1, SGU_GROUPS, 128)
    d_sgu_w = jnp.where(smask[None], dws, 0.0).reshape(1, SGU_GROUPS, 128, 128)
    fold = lambda a8: jnp.sum(a8, axis=0, keepdims=True)
    small_local = [fold(dgb8), fold(dgam8), fold(dbeta8), d_sgu_w, d_sgu_b, d_cw, d_cb,
                   d_dtb, d_alog, d_dskip, d_ssd_nw, fold(dfw8).reshape(D_MODEL), jnp.sum(loss_t[:, 0, 0]).reshape(1)]
    small_shapes = [a.shape for a in small_local]
    v_local = _pack(small_local)

    core = lax.axis_index("c")
    place = jnp.stack([chip, core]).astype(jnp.int32)
    hr_i, hr_o = D_MODEL // 2, D_MODEL // 8
    sib_o, sib_v = _rs_sibling_call(p_out, v_local)
    s1b_i, o_i = _rs_add_windows(dw_ref, sib_i, place, 256, "rs_add_in")
    s1b_o, o_o = _rs_add(p_out, sib_o, place, 256, "rs_add_out")
    chip_v = _add_pair(v_local, sib_v, v_local.shape[0], "ar_add_small")
    dxn, r_i, r_o, abs_v = _dx_rs_call(dproj_a, wt_a, dproj_b, wt_b, s1b_i, s1b_o, chip_v, tm=tm_mm)
    grad_x, dnw8 = _gradx_call(x2, dxn, dh_b, norm_w, t_row)
    abs_v = lax.dynamic_update_index_in_dim(abs_v, chip_v, chip, 0)
    f_i = _sum_own_recv(o_i, r_i, 256, "rs_sum_in")
    f_o = _sum_own_recv(o_o, r_o, 256, "rs_sum_out")
    sib_f_i, g_w_out, all_nw = _rs_join_call(f_i, f_o, dnw8)
    g_w_out = lax.dynamic_update_slice_in_dim(g_w_out, f_o, core * hr_o, axis=0)
    all_nw = lax.dynamic_update_index_in_dim(all_nw, dnw8, 2 * chip + core, 0)
    g_nw = fold(_sum_slots(all_nw, "ar_sum_norm_w"))
    total_v = _sum_slots(abs_v, "ar_sum_small")
    (g_gb, g_gam, g_beta, g_sw, g_sb, g_cw_full, g_cb, g_dtb, g_alog, g_dsk, g_snw, g_fnw, loss1) = _unpack(total_v, small_shapes)
    g_cw_shard = lax.dynamic_slice(g_cw_full, (0, chip * 768), (4, 768)).reshape(1, 4, 768)
    loss = loss1.reshape(())

    shard_t = lambda win: lax.dynamic_slice_in_dim(win, 8 * chip, SHARD_W, axis=1).T
    g_w_in, d_win, nm_win, nv_win = (a.T for a in _adamw_halves(w_in[0].T, shard_t(f_i), shard_t(sib_f_i), m_w_in[0].T,
                                                                v_w_in[0].T, place, 296, "adamw_w_in"))
    d_wout, nm_wout, nv_wout = _adamw(w_out[0], g_w_out, m_w_out[0], v_w_out[0], 128, "adamw_w_out")
    small_w = [norm_w, gate_b, sgu_norm_g, sgu_norm_b, sgu_w, sgu_b, conv_w, conv_b, dt_bias, A_log, D_skip, ssd_norm_w, final_norm_w]
    small_m = [m_norm_w, m_gate_b, m_sgu_norm_g, m_sgu_norm_b, m_sgu_w, m_sgu_b, m_conv_w, m_conv_b, m_dt_bias, m_A_log, m_D_skip, m_ssd_norm_w, m_final_norm_w]
    small_v = [v_norm_w, v_gate_b, v_sgu_norm_g, v_sgu_norm_b, v_sgu_w, v_sgu_b, v_conv_w, v_conv_b, v_dt_bias, v_A_log, v_D_skip, v_ssd_norm_w, v_final_norm_w]
    small_g = [g_nw, g_gb, g_gam, g_beta, g_sw, g_sb, g_cw_shard, g_cb, g_dtb, g_alog, g_dsk, g_snw, g_fnw]
    shapes_w = [a.shape for a in small_w]
    small_g = [a.reshape(shp) for a, shp in zip(small_g, shapes_w)]
    pw = _pack(small_w)
    pd, pm, pv = _adamw(pw, _pack(small_g), _pack(small_m), _pack(small_v), pw.shape[0], "adamw_small")
    d_small, nm_small, nv_small = _unpack(pd, shapes_w), _unpack(pm, shapes_w), _unpack(pv, shapes_w)

    def with_big(small, win, wout):
        o = list(small)
        return o[0:1] + [win.reshape(1, D_MODEL, SHARD_W)] + o[1:12] + [wout.reshape(1, D_MODEL // 4, D_MODEL)] + o[12:13]

    grads = with_big(small_g, g_w_in, g_w_out)
    deltas = with_big(d_small, d_win, d_wout)
    new_m = with_big(nm_small, nm_win, nm_wout)
    new_v = with_big(nv_small, nv_win, nv_wout)
    return (loss, grad_x.reshape(1, s, D_MODEL), *grads, *deltas, *new_m, *new_v)
```

```python
import functools

import jax
import jax.numpy as jnp
from jax import lax
from jax.experimental import pallas as pl
from jax.experimental.pallas import tpu as pltpu

F32 = jnp.float32
MXU_DTYPE = jnp.bfloat16

D_MODEL = 2048
EPS = 1e-5
CHUNK = 64
SGU_BLOCK = 128
SGU_GROUPS = 16
SSD_GROUPS = 4
SSD_GW = 512
SSD_STATE = 128
HEADDIM = 64
IN_W = 15392
SHARD_W = IN_W // 4
BW_B = 1408
FW_B = BW_B
XBC_O, DT_O = 512, 1280
NA = 10240

ADAM_LR = 0.001
ADAM_B1 = 0.9
ADAM_B2 = 0.999
ADAM_EPS = 1e-08
ADAM_WD = 0.01
ADAM_STEP = 10

T_SSD = 256
NG_SSD = 4
T_TOK = 128
T_OUT = 256
T_ROW = 512
TM_MM = 1024
TK_DW = 1024
VMEM_CAP = 60 * 1024 * 1024
MESH = pl.DeviceIdType.MESH


def _cparams(sem, est_bytes):
    lim = int(min(VMEM_CAP, max(32 * 1024 * 1024, est_bytes + 12 * 1024 * 1024)))
    return pltpu.CompilerParams(dimension_semantics=sem, vmem_limit_bytes=lim)


def _c(x):
    return x.astype(MXU_DTYPE)


def _dot(a, b):
    return jnp.dot(a, b, preferred_element_type=F32)


def _dot_nt(a, b):
    return lax.dot_general(a, b, (((1,), (1,)), ((), ())), preferred_element_type=F32)


def _dot_tn(a, b):
    return lax.dot_general(a, b, (((0,), (0,)), ((), ())), preferred_element_type=F32)


def _split(x, n):
    parts, r = [], x
    for _ in range(n):
        p = _c(r)
        parts.append(p)
        r = r - p.astype(F32)
    return parts


def _dot01_l(m01, x, n):
    acc = None
    for p in _split(x, n):
        t = _dot(m01, p)
        acc = t if acc is None else acc + t
    return acc


def _dot01_r(x, m01, n):
    acc = None
    for p in _split(x, n):
        t = _dot(p, m01)
        acc = t if acc is None else acc + t
    return acc


def _sigmoid(x):
    return 1.0 / (1.0 + jnp.exp(-x))


def _fold8(x):
    r, w = x.shape
    return jnp.sum(x.reshape(r // 8, 8, w), axis=0)


def _iota(shape, dim):
    return lax.broadcasted_iota(jnp.int32, shape, dim)


def _ssd_masks():
    l64 = _iota((CHUNK, SSD_GW), 0)
    s64 = jnp.bitwise_and(_iota((CHUNK, SSD_GW), 1), CHUNK - 1)
    diag = l64 == s64
    causal = l64 >= s64
    row_last = l64 == CHUNK - 1
    r4 = lax.shift_right_logical(_iota((256, 256), 0), 6)
    c4 = lax.shift_right_logical(_iota((256, 256), 1), 6)
    mask4 = r4 == c4
    return diag, causal, row_last, mask4


def _cumsum_mats(t):
    r, c = _iota((t, t), 0), _iota((t, t), 1)
    same = lax.shift_right_logical(r, 6) == lax.shift_right_logical(c, 6)
    tri = _c(jnp.where(same, jnp.where(c <= r, 1.0, 0.0), 0.0))
    trit = _c(jnp.where(same, jnp.where(c >= r, 1.0, 0.0), 0.0))
    return tri, trit


def _head_expand_mat():
    return _c(jnp.where(_iota((128, SSD_GW), 0) == lax.shift_right_logical(_iota((128, SSD_GW), 1), 6), 1.0, 0.0))


def _ssd_common(xs, bm, cm, dt, acs, masks):
    diag, causal, row_last, mask4 = masks
    row_e = jnp.sum(jnp.where(diag, acs, 0.0), axis=0, keepdims=True)
    seg = acs - row_e
    lm = jnp.exp(jnp.where(causal, seg, -1e30))
    bb, cb = _c(bm), _c(cm)
    brep = jnp.concatenate([bb] * 8, axis=0)
    cbrep = _dot_nt(cb, brep)
    m = cbrep * lm
    xdt = xs * dt
    acs_last = jnp.sum(jnp.where(row_last, acs, 0.0), axis=0, keepdims=True)
    dec = jnp.exp(acs_last - acs)
    eacs = jnp.exp(acs)
    cd = jnp.exp(acs_last)
    return dict(lm=lm, bb=bb, cb=cb, brep=brep, m=m, xdt=xdt, dec=dec, eacs=eacs, cd=cd)


def _blockdiag4(xb, mask4):
    return jnp.where(mask4, jnp.concatenate([xb] * 4, axis=0), jnp.zeros((), xb.dtype))


def _ssd_chunk_fwd(xs, bm, cm, dt, acs, d_skip, ht, masks):
    q = _ssd_common(xs, bm, cm, dt, acs, masks)
    mask4 = masks[3]
    mb, xdtb = _c(q["m"]), _c(q["xdt"])
    yd = []
    for blk in range(2):
        sl = slice(256 * blk, 256 * blk + 256)
        yd.append(_dot(mb[:, sl], _blockdiag4(xdtb[:, sl], mask4)))
    y_diag = jnp.concatenate(yd, axis=1)
    p = _dot(q["cb"], _c(ht))
    y = y_diag + p * q["eacs"] + xs * d_skip
    st = _dot_tn(q["bb"], _c(q["xdt"] * q["dec"]))
    return y, ht * q["cd"] + st


def _ssd_chunk_bwd(xs, bm, cm, dt, acs, d_skip, hprev, dht, dy, masks):
    diag, causal, row_last, mask4 = masks
    q = _ssd_common(xs, bm, cm, dt, acs, masks)
    lm, bb, cb, brep, m, xdt, dec, eacs, cd = (q[k] for k in ("lm", "bb", "cb", "brep", "m", "xdt", "dec", "eacs", "cd"))
    hb = _c(hprev)
    yoff = _dot(cb, hb) * eacs
    dyb = _c(dy)
    dpb = _c(dy * eacs)
    d_c = _dot_nt(dpb, hb)
    dh_y = _dot_tn(cb, dpb)
    mb, xdtb = _c(m), _c(xdt)
    dm_parts, dxdt_parts = [], []
    for blk in range(2):
        sl = slice(256 * blk, 256 * blk + 256)
        bd = _blockdiag4(xdtb[:, sl], mask4)
        dm_parts.append(_dot_nt(dyb[:, sl], bd))
        dxf = jnp.where(mask4, _dot_tn(mb[:, sl], dyb[:, sl]), 0.0)
        dxdt_parts.append(dxf[0:64] + dxf[64:128] + dxf[128:192] + dxf[192:256])
    dm = jnp.concatenate(dm_parts, axis=1)
    dxdt = jnp.concatenate(dxdt_parts, axis=1)
    dcbb = _c(dm * lm)
    g = dm * m
    d_c = d_c + _dot(dcbb, brep)
    dbrep = _dot_tn(dcbb, cb)
    d_b = dbrep[0:64]
    for r in range(1, 8):
        d_b = d_b + dbrep[64 * r:64 * r + 64]
    dhtb = _c(dht)
    dxd = _dot(bb, dhtb)
    xd = xdt * dec
    dxdt = dxdt + dxd * dec
    tq = dxd * xd
    d_b = d_b + _dot_nt(_c(xd), dhtb)
    dcd = jnp.sum(dht * hprev, axis=0, keepdims=True)
    col_g = jnp.sum(g, axis=0, keepdims=True)
    last = jnp.sum(tq, axis=0, keepdims=True) + dcd * cd
    qq = g - jnp.where(diag, col_g, 0.0) + dy * yoff - tq + jnp.where(row_last, last, 0.0)
    dxs = dxdt * dt + dy * d_skip
    return dxs, d_b, d_c, dht * cd + dh_y, dy * xs, qq, dxdt * xs


def _ssd_finish_dt(qq, p1, dt, a_neg, trit, mask4):
    bd4 = _c(jnp.where(mask4, 1.0, 0.0))
    dacs = jnp.concatenate([_dot01_r(qq[:, 256 * b:256 * b + 256], bd4, 2) for b in range(2)], axis=1)
    da = _dot01_l(trit, dacs, 2)
    return p1 + da * (a_neg * (1.0 / HEADDIM)), da * dt


def _softplus(x):
    return jnp.maximum(x, 0.0) + jnp.log(1.0 + jnp.exp(-jnp.abs(x)))


def _conv_taps(xpad, t):
    taps = []
    for k in range(4):
        sh = 3 - k
        v = xpad if sh == 0 else pltpu.roll(xpad, sh, 0)
        taps.append(v[8:8 + t])
    return taps


def _mm(a, b, *, tm, tn, tk, name, out_dtype=F32, col0=0, n=None, b_is_t=False):
    m, k = a.shape
    n = b.shape[0 if b_is_t else 1] if n is None else n
    nk = k // tk
    assert m % tm == 0 and n % tn == 0 and k % tk == 0, (a.shape, b.shape, tm, tn, tk)
    dot = _dot_nt if b_is_t else _dot
    via_acc = nk > 1 and out_dtype != F32

    def body(a_ref, b_ref, o_ref, *acc):
        if nk == 1:
            o_ref[...] = dot(a_ref[...], b_ref[...]).astype(out_dtype)
            return
        acc_ref = acc[0] if via_acc else o_ref

        @pl.when(pl.program_id(2) == 0)
        def _():
            acc_ref[...] = jnp.zeros_like(acc_ref)

        acc_ref[...] += dot(a_ref[...], b_ref[...])
        if via_acc:
            @pl.when(pl.program_id(2) == nk - 1)
            def _():
                o_ref[...] = acc_ref[...].astype(out_dtype)

    isz = jnp.dtype(a.dtype).itemsize
    est = 2 * (tm * tk + tk * tn) * isz + 3 * tm * tn * 4
    return pl.pallas_call(
        body, name=name,
        grid=(m // tm, n // tn, nk),
        in_specs=[pl.BlockSpec((tm, tk), lambda i, j, kk: (i, kk)),
                  pl.BlockSpec((tn, tk), lambda i, j, kk: (j + col0, kk)) if b_is_t
                  else pl.BlockSpec((tk, tn), lambda i, j, kk: (kk, j + col0))],
        out_specs=pl.BlockSpec((tm, tn), lambda i, j, kk: (i, j)),
        out_shape=jax.ShapeDtypeStruct((m, n), out_dtype),
        scratch_shapes=[pltpu.VMEM((tm, tn), F32)] if via_acc else [],
        compiler_params=_cparams(("parallel", "parallel", "arbitrary"), est),
    )(a, b)


def _dw_groups(xnt, dpb, *, tk, out_dtype):
    m, k = xnt.shape
    nk = k // tk

    def body(a_ref, b_ref, zb_ref, xs_ref, bm_ref, cm_ref, dt_ref, acc_ref):
        @pl.when(pl.program_id(1) == 0)
        def _():
            acc_ref[...] = jnp.zeros_like(acc_ref)

        acc_ref[...] += _dot(a_ref[...], b_ref[...])

        @pl.when(pl.program_id(1) == nk - 1)
        def _():
            for o_ref, lo, hi in ((zb_ref, 0, 512), (xs_ref, 512, 1024), (bm_ref, 1024, 1152), (cm_ref, 1152, 1280),
                                  (dt_ref, 1280, 1408)):
                o_ref[...] = acc_ref[:, lo:hi].astype(out_dtype)

    isz = jnp.dtype(xnt.dtype).itemsize
    est = 2 * (m * tk + tk * BW_B) * isz + 3 * m * BW_B * 4
    piece = lambda w: pl.BlockSpec((m, w), lambda g, kk: (0, g))
    return pl.pallas_call(
        body, name="dw_in_b",
        grid=(SSD_GROUPS, nk),
        in_specs=[pl.BlockSpec((m, tk), lambda g, kk: (0, kk)), pl.BlockSpec((tk, BW_B), lambda g, kk: (kk, g))],
        out_specs=[piece(512), piece(512), piece(128), piece(128), piece(128)],
        out_shape=[jax.ShapeDtypeStruct((m, w), out_dtype) for w in (2048, 2048, 512, 512, 512)],
        scratch_shapes=[pltpu.VMEM((m, BW_B), F32)],
        compiler_params=_cparams(("parallel", "arbitrary"), est),
    )(xnt, dpb)


def _dx_rs_call(dpa, wta, dpb, wtb, sb_in, sb_out, chip_v, *, tm):
    s = dpa.shape[0]
    tka, tkb = 1024, BW_B
    nka, nkb = dpa.shape[1] // tka, dpb.shape[1] // tkb
    ni, nk = s // tm, nka + nkb

    def body(a_ref, wa_ref, b_ref, wb_ref, sbin, sbout, cv, o_ref, rc_in, rc_out, abs_v, send, recv):
        i, kk = pl.program_id(0), pl.program_id(1)

        def copies():
            x, y, c, me, others = _place()
            sends, recvs = [], []
            for j, chip in enumerate(others):
                kj = 2 * chip[0] + chip[1]
                to = (*chip, c)
                sends += [_remote(sbin.at[kj], rc_in.at[j], send, recv, j, to),
                          _remote(sbout.at[kj], rc_out.at[j], send, recv, 3 + j, to),
                          _remote(cv, abs_v.at[me], send, recv, 6 + j, to)]
                recvs += [sends[-3], sends[-2], _remote(cv, abs_v.at[kj], send, recv, 6 + j, to)]
            return sends, recvs

        @pl.when((i == 0) & (kk == 0))
        def _():
            for cp in copies()[0]:
                cp.start()

        @pl.when(kk == 0)
        def _():
            o_ref[...] = jnp.zeros_like(o_ref)

        @pl.when(kk < nka)
        def _():
            o_ref[...] += _dot(a_ref[...], wa_ref[...])

        @pl.when(kk >= nka)
        def _():
            o_ref[...] += _dot(b_ref[...], wb_ref[...])

        @pl.when((i == ni - 1) & (kk == nk - 1))
        def _():
            sends, recvs = copies()
            for cp in recvs:
                cp.wait_recv()
            for cp in sends:
                cp.wait_send()

    isz = jnp.dtype(dpa.dtype).itemsize
    est = 2 * isz * (tm * tka + tka * D_MODEL + tm * tkb + tkb * D_MODEL) + 2 * tm * D_MODEL * 4
    outs = [jax.ShapeDtypeStruct((s, D_MODEL), F32),
            jax.ShapeDtypeStruct((3,) + sb_in.shape[1:], sb_in.dtype), jax.ShapeDtypeStruct((3,) + sb_out.shape[1:], sb_out.dtype),
            jax.ShapeDtypeStruct((4,) + chip_v.shape, F32)]
    return pl.pallas_call(
        body, name="dx_matmul_rs_chips",
        grid=(ni, nk),
        in_specs=[
            pl.BlockSpec((tm, tka), lambda i, kk: (i, jnp.minimum(kk, nka - 1))),
            pl.BlockSpec((tka, D_MODEL), lambda i, kk: (jnp.minimum(kk, nka - 1), 0)),
            pl.BlockSpec((tm, tkb), lambda i, kk: (i, jnp.maximum(kk - nka, 0))),
            pl.BlockSpec((tkb, D_MODEL), lambda i, kk: (jnp.maximum(kk - nka, 0), 0)),
            ANY, ANY, ANY,
        ],
        out_specs=[pl.BlockSpec((tm, D_MODEL), lambda i, kk: (i, 0)), ANY, ANY, ANY],
        out_shape=outs,
        scratch_shapes=[pltpu.SemaphoreType.DMA((9,)), pltpu.SemaphoreType.DMA((9,))],
        compiler_params=_cparams(("arbitrary", "arbitrary"), est),
    )(dpa, wta, dpb, wtb, sb_in, sb_out, chip_v)


def _gradx_call(x, dxn, dh, norm_w, tm):
    s = x.shape[0]

    def body(x_ref, g_ref, dh_ref, w_ref, gx_ref, dw_ref):
        @pl.when(pl.program_id(0) == 0)
        def _():
            dw_ref[...] = jnp.zeros_like(dw_ref)

        xv, gv = x_ref[...], g_ref[...]
        r = lax.rsqrt(jnp.mean(xv * xv, axis=-1, keepdims=True) + EPS)
        gw = gv * w_ref[...]
        gx_ref[...] = r * gw - xv * (r * r * r) * jnp.mean(xv * gw, axis=-1, keepdims=True) + dh_ref[...].astype(F32)
        dw_ref[...] += _fold8(gv * (xv * r))

    row = pl.BlockSpec((tm, D_MODEL), lambda i: (i, 0))
    return pl.pallas_call(
        body, name="grad_x",
        grid=(s // tm,),
        in_specs=[row, row, row, pl.BlockSpec((1, D_MODEL), lambda i: (0, 0))],
        out_specs=[row, pl.BlockSpec((8, D_MODEL), lambda i: (0, 0))],
        out_shape=[jax.ShapeDtypeStruct((s, D_MODEL), F32), jax.ShapeDtypeStruct((8, D_MODEL), F32)],
        compiler_params=_cparams(("arbitrary",), 2 * tm * D_MODEL * 16),
    )(x, dxn, dh, norm_w)


def _layernorm_stats(v):
    mu = jnp.mean(v, axis=-1, keepdims=True)
    vc = v - mu
    var = jnp.mean(vc * vc, axis=-1, keepdims=True)
    return vc * lax.rsqrt(var + EPS), lax.rsqrt(var + EPS)


def _tok_fwd_call(proj_a, y_b, gate_b, sgu_g, sgu_beta, wm, bias_full, t):
    s = proj_a.shape[0]

    def body(pa_ref, yb_ref, gb_ref, g_ref, be_ref, wm_ref, bf_ref, ya_ref, mg_ref, mgt_ref, mix_ref):
        u = pa_ref[:, 0:2048].astype(F32)
        v = pa_ref[:, 2048:4096].astype(F32)
        za = pa_ref[:, 4096:6144].astype(F32)
        xhat, _ = _layernorm_stats(v)
        vnb = _c(xhat * g_ref[...] + be_ref[...])
        for gi in range(SGU_GROUPS):
            sl = slice(128 * gi, 128 * gi + 128)
            mix_ref[:, sl] = _dot(wm_ref[gi], vnb[:, sl])
        mixed = mix_ref[...] + bf_ref[...]
        y_a = u * mixed * (za * _sigmoid(za))
        g0 = _sigmoid(pa_ref[:, 6144:8192].astype(F32) + gb_ref[:, 0:2048])
        g1 = _sigmoid(pa_ref[:, 8192:10240].astype(F32) + gb_ref[:, 2048:4096])
        merged = g0 * y_a + g1 * yb_ref[...].astype(F32)
        ya_ref[...] = _c(y_a)
        mg_ref[...] = _c(merged)
        mgt_ref[...] = _c(merged.T)

    row = pl.BlockSpec((t, D_MODEL), lambda i: (i, 0))
    vec = lambda w: pl.BlockSpec((1, w), lambda i: (0, 0))
    return pl.pallas_call(
        body, name="tok_fwd",
        grid=(s // t,),
        in_specs=[pl.BlockSpec((t, NA), lambda i: (i, 0)), row, vec(4096), vec(2048), vec(2048),
                  pl.BlockSpec((SGU_GROUPS, 128, 128), lambda i: (0, 0, 0)), pl.BlockSpec((128, D_MODEL), lambda i: (0, 0))],
        out_specs=[row, row, pl.BlockSpec((D_MODEL, t), lambda i: (0, i))],
        out_shape=[jax.ShapeDtypeStruct((s, D_MODEL), MXU_DTYPE), jax.ShapeDtypeStruct((s, D_MODEL), MXU_DTYPE),
                   jax.ShapeDtypeStruct((D_MODEL, s), MXU_DTYPE)],
        scratch_shapes=[pltpu.VMEM((t, D_MODEL), F32)],
        compiler_params=_cparams(("parallel",), 2 * t * NA * 4 + 12 * t * D_MODEL * 4),
    )(proj_a, y_b, gate_b, sgu_g, sgu_beta, wm, bias_full)


def _tok_bwd_call(proj_a, dmerged, y_a, y_b, gate_b, sgu_g, sgu_beta, wm, wmt, bias_full, t):
    s = proj_a.shape[0]

    def body(pa_ref, dm_ref, ya_ref, yb_ref, gb_ref, g_ref, be_ref, wm_ref, wmt_ref, bf_ref,
             dpa_ref, dyb_ref, dgb_ref, dgam_ref, dbeta_ref, dbf_ref, dws_ref, mix_ref, dvn_ref):
        @pl.when(pl.program_id(0) == 0)
        def _():
            dgb_ref[...] = jnp.zeros_like(dgb_ref)
            dgam_ref[...] = jnp.zeros_like(dgam_ref)
            dbeta_ref[...] = jnp.zeros_like(dbeta_ref)
            dbf_ref[...] = jnp.zeros_like(dbf_ref)
            dws_ref[...] = jnp.zeros_like(dws_ref)

        u = pa_ref[:, 0:2048].astype(F32)
        v = pa_ref[:, 2048:4096].astype(F32)
        za = pa_ref[:, 4096:6144].astype(F32)
        xhat, rstd = _layernorm_stats(v)
        vnb = _c(xhat * g_ref[...] + be_ref[...])
        for gi in range(SGU_GROUPS):
            sl = slice(128 * gi, 128 * gi + 128)
            mix_ref[:, sl] = _dot(wm_ref[gi], vnb[:, sl])
        mixed = mix_ref[...] + bf_ref[...]
        sig = _sigmoid(za)
        sz = za * sig
        dm = dm_ref[...].astype(F32)
        y_a = ya_ref[...].astype(F32)
        g0 = _sigmoid(pa_ref[:, 6144:8192].astype(F32) + gb_ref[:, 0:2048])
        g1 = _sigmoid(pa_ref[:, 8192:10240].astype(F32) + gb_ref[:, 2048:4096])
        dgl0 = dm * y_a * g0 * (1.0 - g0)
        dgl1 = dm * yb_ref[...].astype(F32) * g1 * (1.0 - g1)
        dyb_ref[...] = _c(dm * g1)
        dya = dm * g0
        dpa_ref[:, 6144:8192] = _c(dgl0)
        dpa_ref[:, 8192:10240] = _c(dgl1)
        dgb_ref[:, 0:2048] += _fold8(dgl0)
        dgb_ref[:, 2048:4096] += _fold8(dgl1)
        dpa_ref[:, 0:2048] = _c(dya * mixed * sz)
        dpa_ref[:, 4096:6144] = _c(dya * (u * mixed) * (sig * (1.0 + za * (1.0 - sig))))
        dmixed = dya * u * sz
        dbf_ref[...] += dmixed
        dmb = _c(dmixed)
        for gi in range(SGU_GROUPS):
            sl = slice(128 * gi, 128 * gi + 128)
            dvn_ref[:, sl] = _dot(wmt_ref[gi], dmb[:, sl])
            dws_ref[gi] += _dot_nt(dmb[:, sl], vnb[:, sl])
        dvn = dvn_ref[...]
        dgam_ref[...] += _fold8(dvn * xhat)
        dbeta_ref[...] += _fold8(dvn)
        dxh = dvn * g_ref[...]
        dv = rstd * (dxh - jnp.mean(dxh, axis=-1, keepdims=True) - xhat * jnp.mean(dxh * xhat, axis=-1, keepdims=True))
        dpa_ref[:, 2048:4096] = _c(dv)

    row = pl.BlockSpec((t, D_MODEL), lambda i: (i, 0))
    vec = lambda w: pl.BlockSpec((1, w), lambda i: (0, 0))
    acc = lambda w: pl.BlockSpec((8, w), lambda i: (0, 0))
    wspec = pl.BlockSpec((SGU_GROUPS, 128, 128), lambda i: (0, 0, 0))
    return pl.pallas_call(
        body, name="tok_bwd",
        grid=(s // t,),
        in_specs=[pl.BlockSpec((t, NA), lambda i: (i, 0)), row, row, row, vec(4096), vec(2048), vec(2048),
                  wspec, wspec, pl.BlockSpec((128, D_MODEL), lambda i: (0, 0))],
        out_specs=[pl.BlockSpec((t, NA), lambda i: (i, 0)), row, acc(4096), acc(2048), acc(2048),
                   pl.BlockSpec((128, D_MODEL), lambda i: (0, 0)), wspec],
        out_shape=[jax.ShapeDtypeStruct((s, NA), MXU_DTYPE), jax.ShapeDtypeStruct((s, D_MODEL), MXU_DTYPE),
                   jax.ShapeDtypeStruct((8, 4096), F32), jax.ShapeDtypeStruct((8, 2048), F32),
                   jax.ShapeDtypeStruct((8, 2048), F32), jax.ShapeDtypeStruct((128, D_MODEL), F32),
                   jax.ShapeDtypeStruct((SGU_GROUPS, 128, 128), F32)],
        scratch_shapes=[pltpu.VMEM((t, D_MODEL), F32), pltpu.VMEM((t, D_MODEL), F32)],
        compiler_params=_cparams(("arbitrary",), 2 * t * NA * 6 + 16 * t * D_MODEL * 4),
    )(proj_a, dmerged, y_a, y_b, gate_b, sgu_g, sgu_beta, wm, wmt, bias_full)


def _out_call(merged, x, target, w_out, fnw, t):
    s = x.shape[0]
    nt = s // t

    def body(mg_ref, x_ref, t_ref, w_ref, fw_ref, dhb_ref, dmg_ref, loss_ref, dfw_ref):
        @pl.when(pl.program_id(0) == 0)
        def _():
            dfw_ref[...] = jnp.zeros_like(dfw_ref)

        h = x_ref[...] + _dot(mg_ref[...], w_ref[...])
        r = lax.rsqrt(jnp.mean(h * h, axis=-1, keepdims=True) + EPS)
        hn = h * r
        err = hn * fw_ref[...] - t_ref[...]
        loss_ref[...] = jnp.full(loss_ref.shape, 0.5 * jnp.sum(jnp.mean(err * err, axis=-1, keepdims=True)), F32)
        dy = err * (1.0 / D_MODEL)
        dfw_ref[...] += _fold8(dy * hn)
        gw = dy * fw_ref[...]
        dh = r * gw - h * (r * r * r) * jnp.mean(h * gw, axis=-1, keepdims=True)
        dhb = _c(dh)
        dhb_ref[...] = dhb
        dmg_ref[...] = _c(_dot_nt(dhb, w_ref[...]))

    row = pl.BlockSpec((t, D_MODEL), lambda i: (i, 0))
    return pl.pallas_call(
        body, name="out_proj_loss",
        grid=(nt,),
        in_specs=[row, row, row, pl.BlockSpec((D_MODEL, D_MODEL), lambda i: (0, 0)), pl.BlockSpec((1, D_MODEL), lambda i: (0, 0))],
        out_specs=[row, row, pl.BlockSpec((1, 8, 128), lambda i: (i, 0, 0)), pl.BlockSpec((8, D_MODEL), lambda i: (0, 0))],
        out_shape=[jax.ShapeDtypeStruct((s, D_MODEL), MXU_DTYPE),
                   jax.ShapeDtypeStruct((s, D_MODEL), MXU_DTYPE), jax.ShapeDtypeStruct((nt, 8, 128), F32),
                   jax.ShapeDtypeStruct((8, D_MODEL), F32)],
        compiler_params=_cparams(("arbitrary",), 2 * D_MODEL * D_MODEL * 2 + 2 * t * D_MODEL * 20),
    )(merged, x, target, w_out, fnw)


def _ssd_fwd_call(proj_b, dtb, alog, dsk, cw, cb, nw, t, ng):
    s = proj_b.shape[0]
    nt, nch = s // t, t // CHUNK

    def body(pb_ref, halo_ref, dtb_ref, al_ref, ds_ref, cw_ref, cb_ref, nw_ref, y_ref, yb_ref, hp_ref, pre_ref,
             dt_ref, acs_ref, ht_ref, prev_ref):
        i = pl.program_id(1)

        @pl.when(i == 0)
        def _():
            ht_ref[...] = jnp.zeros_like(ht_ref)

        for gi in range(ng):
            prev_ref[:, 768 * gi:768 * gi + 768] = jnp.where(i == 0, 0.0, halo_ref[:, FW_B * gi + XBC_O:FW_B * gi + DT_O])
        masks = _ssd_masks()
        tri, _ = _cumsum_mats(t)
        a_neg = -jnp.exp(al_ref[...])
        expand = _head_expand_mat()
        for gi in range(ng):
            fo, go, no = FW_B * gi, SSD_GW * gi, 128 * gi
            dt_n = _softplus(pb_ref[:, fo + DT_O:fo + DT_O + 128] + dtb_ref[:, no:no + 128])
            acs_n = _dot01_l(tri, dt_n * a_neg[:, no:no + 128], 3)
            dt_ref[:, go:go + 512] = _dot01_r(dt_n, expand, 2)
            acs_ref[:, go:go + 512] = _dot01_r(acs_n, expand, 2)

        def chunk(c, carry):
            rows = pl.ds(pl.multiple_of(c * CHUNK, CHUNK), CHUNK)
            for gi in range(ng):
                fo, co, go, no = FW_B * gi, 768 * gi, SSD_GW * gi, 128 * gi
                xbc = pb_ref[rows, fo + XBC_O:fo + DT_O]
                taps = _conv_taps(jnp.concatenate([prev_ref[:, co:co + 768], xbc], axis=0), CHUNK)
                prev_ref[:, co:co + 768] = xbc[CHUNK - 8:CHUNK]
                pre = cb_ref[:, co:co + 768]
                for k in range(4):
                    pre = pre + taps[k] * cw_ref[k:k + 1, co:co + 768]
                pre_ref[rows, co:co + 768] = pre
                act = pre * _sigmoid(pre)
                dt = dt_ref[rows, go:go + 512]
                acs = acs_ref[rows, go:go + 512]
                ht = ht_ref[gi]
                hp_ref[c, :, go:go + 512] = ht
                y, ht_new = _ssd_chunk_fwd(act[:, 0:512], act[:, 512:640], act[:, 640:768], dt, acs,
                                           ds_ref[:, go:go + 512], ht, masks)
                y_ref[rows, go:go + 512] = y
                ht_ref[gi] = ht_new
                zb = pb_ref[rows, fo:fo + 512]
                hh = y * (zb * _sigmoid(zb))
                rr = lax.rsqrt(jnp.mean(hh * hh, axis=-1, keepdims=True) + EPS)
                yb_ref[rows, go:go + 512] = _c(hh * rr * nw_ref[:, go:go + 512])
            return carry

        lax.fori_loop(0, nch, chunk, 0)

    gvec = lambda w: pl.BlockSpec((1, ng * w), lambda g, i: (0, g))
    return pl.pallas_call(
        body, name="ssd_fwd",
        grid=(SSD_GROUPS // ng, nt),
        in_specs=[pl.BlockSpec((t, ng * FW_B), lambda g, i: (i, g)),
                  pl.BlockSpec((8, ng * FW_B), lambda g, i: (jnp.maximum(i * (t // 8) - 1, 0), g)),
                  gvec(128), gvec(128), gvec(512),
                  pl.BlockSpec((4, ng * 768), lambda g, i: (0, g)), gvec(768), gvec(512)],
        out_specs=[pl.BlockSpec((t, ng * SSD_GW), lambda g, i: (i, g)), pl.BlockSpec((t, ng * SSD_GW), lambda g, i: (i, g)),
                   pl.BlockSpec((nch, SSD_STATE, ng * SSD_GW), lambda g, i: (i, 0, g)),
                   pl.BlockSpec((t, ng * 768), lambda g, i: (i, g)),
                   pl.BlockSpec((t, ng * SSD_GW), lambda g, i: (i, g)), pl.BlockSpec((t, ng * SSD_GW), lambda g, i: (i, g))],
        out_shape=[jax.ShapeDtypeStruct((s, D_MODEL), F32), jax.ShapeDtypeStruct((s, D_MODEL), MXU_DTYPE),
                   jax.ShapeDtypeStruct((s // CHUNK, SSD_STATE, D_MODEL), F32),
                   jax.ShapeDtypeStruct((s, SSD_GROUPS * 768), F32),
                   jax.ShapeDtypeStruct((s, D_MODEL), F32), jax.ShapeDtypeStruct((s, D_MODEL), F32)],
        scratch_shapes=[pltpu.VMEM((ng, SSD_STATE, SSD_GW), F32), pltpu.VMEM((8, ng * 768), F32)],
        compiler_params=_cparams(("parallel", "arbitrary"), ng * (2 * t * FW_B * 4 + 16 * t * SSD_GW * 4) + 16 * 1024 * 1024),
    )(proj_b, proj_b, dtb, alog, dsk, cw, cb, nw)


def _ssd_bwd_call(proj_b, pre_all, dt_all, acs_all, dyb, y, hprev, dtb, alog, dsk, cw, nw, t, ng):
    s = proj_b.shape[0]
    nt, nch = s // t, t // CHUNK

    def body(pb_ref, pre_ref, dt_ref, acs_ref, dyb_ref, y_ref, hp_ref, dtb_ref, al_ref, ds_ref, cw_ref, nw_ref,
             dpb_ref, a512_ref, a768_ref, dht_ref, nxt_ref, q_ref, p1_ref):
        i = pl.program_id(1)

        @pl.when(i == 0)
        def _():
            dht_ref[...] = jnp.zeros_like(dht_ref)
            nxt_ref[...] = jnp.zeros_like(nxt_ref)
            a512_ref[...] = jnp.zeros_like(a512_ref)
            a768_ref[...] = jnp.zeros_like(a768_ref)

        _, trit = _cumsum_mats(t)
        a_neg = -jnp.exp(al_ref[...])
        masks = _ssd_masks()

        def chunk(cc, carry):
            c = nch - 1 - cc
            rows = pl.ds(pl.multiple_of(c * CHUNK, CHUNK), CHUNK)
            for gi in range(ng):
                fo, co, go, bo = FW_B * gi, 768 * gi, SSD_GW * gi, BW_B * gi
                pre = pre_ref[rows, co:co + 768]
                sp = _sigmoid(pre)
                act = pre * sp
                zb = pb_ref[rows, fo:fo + 512]
                yv = y_ref[rows, go:go + 512]
                sgz = _sigmoid(zb)
                sz = zb * sgz
                hh = yv * sz
                rr = lax.rsqrt(jnp.mean(hh * hh, axis=-1, keepdims=True) + EPS)
                dyb = dyb_ref[rows, go:go + 512].astype(F32)
                a512_ref[gi, 0] += _fold8(dyb * (hh * rr))
                tt = dyb * nw_ref[:, go:go + 512]
                dhh = rr * tt - hh * (rr * rr * rr) * jnp.mean(hh * tt, axis=-1, keepdims=True)
                dpb_ref[rows, bo:bo + 512] = _c(dhh * yv * (sgz * (1.0 + zb * (1.0 - sgz))))
                dxs, d_b, d_c, dht_prev, dyxs, qq, p1 = _ssd_chunk_bwd(
                    act[:, 0:512], act[:, 512:640], act[:, 640:768], dt_ref[rows, go:go + 512], acs_ref[rows, go:go + 512],
                    ds_ref[:, go:go + 512], hp_ref[c, :, go:go + 512], dht_ref[gi], dhh * sz, masks)
                dht_ref[gi] = dht_prev
                q_ref[rows, go:go + 512] = qq
                p1_ref[rows, go:go + 512] = p1
                a512_ref[gi, 1] += _fold8(dyxs)
                dpre = jnp.concatenate([dxs, d_b, d_c], axis=1) * (sp * (1.0 + pre * (1.0 - sp)))
                xbc = pb_ref[rows, fo + XBC_O:fo + DT_O]
                a768_ref[gi, 4] += _fold8(dpre)
                a768_ref[gi, 3] += _fold8(dpre * xbc)
                dpad = jnp.concatenate([dpre, nxt_ref[:, co:co + 768]], axis=0)
                dx = dpre * cw_ref[3:4, co:co + 768]
                for k in range(3):
                    d_k = pltpu.roll(dpad, CHUNK + 8 - (3 - k), 0)[0:CHUNK]
                    dx = dx + d_k * cw_ref[k:k + 1, co:co + 768]
                    a768_ref[gi, k] += _fold8(d_k * xbc)
                nxt_ref[:, co:co + 768] = dpre[0:8]
                dpb_ref[rows, bo + 512:bo + 1280] = _c(dx)
            return carry

        lax.fori_loop(0, nch, chunk, 0)

        rsel = _c(jnp.where(lax.shift_right_logical(_iota((SSD_GW, 128), 0), 6) == _iota((SSD_GW, 128), 1), 1.0, 0.0))
        for gi in range(ng):
            fo, go, bo, no = FW_B * gi, SSD_GW * gi, BW_B * gi, 128 * gi
            ddt, dadt = _ssd_finish_dt(q_ref[:, go:go + 512], p1_ref[:, go:go + 512], dt_ref[:, go:go + 512],
                                       a_neg[:, go:go + 512], trit, masks[3])
            sig_n = _sigmoid(pb_ref[:, fo + DT_O:fo + DT_O + 128] + dtb_ref[:, no:no + 128])
            ddtr_n = _dot01_r(ddt, rsel, 2) * sig_n
            dpb_ref[:, bo + DT_O:bo + DT_O + 128] = _c(ddtr_n)
            a512_ref[gi, 2] += _fold8(dadt)
            a512_ref[gi, 3, :, 0:128] += _fold8(ddtr_n)

    gvec = lambda w: pl.BlockSpec((1, ng * w), lambda g, i: (0, g))
    rev = lambda w: pl.BlockSpec((t, ng * w), lambda g, i: (nt - 1 - i, g))
    return pl.pallas_call(
        body, name="ssd_bwd",
        grid=(SSD_GROUPS // ng, nt),
        in_specs=[rev(FW_B), rev(768), rev(SSD_GW), rev(SSD_GW), rev(SSD_GW), rev(SSD_GW),
                  pl.BlockSpec((nch, SSD_STATE, ng * SSD_GW), lambda g, i: (nt - 1 - i, 0, g)),
                  gvec(128), gvec(512), gvec(512),
                  pl.BlockSpec((4, ng * 768), lambda g, i: (0, g)), gvec(512)],
        out_specs=[rev(BW_B),
                   pl.BlockSpec((ng, 4, 8, 512), lambda g, i: (g, 0, 0, 0)),
                   pl.BlockSpec((ng, 5, 8, 768), lambda g, i: (g, 0, 0, 0))],
        out_shape=[jax.ShapeDtypeStruct((s, SSD_GROUPS * BW_B), MXU_DTYPE),
                   jax.ShapeDtypeStruct((SSD_GROUPS, 4, 8, 512), F32),
                   jax.ShapeDtypeStruct((SSD_GROUPS, 5, 8, 768), F32)],
        scratch_shapes=[pltpu.VMEM((ng, SSD_STATE, SSD_GW), F32), pltpu.VMEM((8, ng * 768), F32),
                        pltpu.VMEM((t, ng * SSD_GW), F32), pltpu.VMEM((t, ng * SSD_GW), F32)],
        compiler_params=_cparams(("parallel", "arbitrary"), ng * (2 * t * FW_B * 4 + 18 * t * SSD_GW * 4) + 16 * 1024 * 1024),
    )(proj_b, pre_all, dt_all, acs_all, dyb, y, hprev, dtb, alog, dsk, cw, nw)


def _rows_call(body, ins, outs, tr, name):
    r = ins[0].shape[0]
    spec = lambda a: pl.BlockSpec((tr, a.shape[1]), lambda i: (i, 0))
    est = 2 * tr * sum(a.shape[1] * jnp.dtype(a.dtype).itemsize for a in list(ins) + list(outs))
    return pl.pallas_call(
        body, name=name, grid=(r // tr,),
        in_specs=[spec(a) for a in ins], out_specs=[spec(o) for o in outs], out_shape=list(outs),
        compiler_params=_cparams(("parallel",), est),
    )(*ins)


def _add_pair(a, b, tr, name):
    def body(a_ref, b_ref, o_ref):
        o_ref[...] = a_ref[...] + b_ref[...]

    return _rows_call(body, [a, b], [jax.ShapeDtypeStruct(a.shape, F32)], tr, name)[0]


def _rs_add(p, sib, place, tr, name):
    _, r, c = p.shape
    half = r // 2
    nb = half // tr

    def body(pl_ref, p_ref, s_ref, b_ref, own_ref):
        v = p_ref[0] + s_ref[0]
        b_ref[0] = v.astype(jnp.bfloat16)

        @pl.when(pl.program_id(1) == pl_ref[0])
        def _():
            own_ref[...] = v

    return pl.pallas_call(
        body, name=name,
        grid_spec=pltpu.PrefetchScalarGridSpec(
            num_scalar_prefetch=1, grid=(nb, 4),
            in_specs=[pl.BlockSpec((1, tr, c), lambda i, k, pr: (k, pr[1] * nb + i, 0)),
                      pl.BlockSpec((1, tr, c), lambda i, k, pr: (k, i, 0))],
            out_specs=[pl.BlockSpec((1, tr, c), lambda i, k, pr: (k, i, 0)),
                       pl.BlockSpec((tr, c), lambda i, k, pr: (i, 0))]),
        out_shape=[jax.ShapeDtypeStruct((4, half, c), jnp.bfloat16), jax.ShapeDtypeStruct((half, c), F32)],
        compiler_params=_cparams(("parallel", "arbitrary"), 2 * tr * c * 14),
    )(place, p, sib)


WIN_STEP = 3840
WIN_W = 3968


def _rs_add_windows(dw, sib, place, tr, name):
    r = dw.shape[0]
    half = r // 2
    nb = half // tr
    tail = WIN_W - WIN_STEP

    def body(pl_ref, pm_ref, pt_ref, s_ref, b_ref, own_ref):
        vm = pm_ref[...].astype(F32) + s_ref[0, :, 0:WIN_STEP].astype(F32)
        vt = pt_ref[...].astype(F32) + s_ref[0, :, WIN_STEP:WIN_W].astype(F32)
        b_ref[0, :, 0:WIN_STEP] = vm.astype(jnp.bfloat16)
        b_ref[0, :, WIN_STEP:WIN_W] = vt.astype(jnp.bfloat16)

        @pl.when(pl.program_id(1) == pl_ref[0])
        def _():
            own_ref[:, 0:WIN_STEP] = vm
            own_ref[:, WIN_STEP:WIN_W] = vt

    return pl.pallas_call(
        body, name=name,
        grid_spec=pltpu.PrefetchScalarGridSpec(
            num_scalar_prefetch=1, grid=(nb, 4),
            in_specs=[pl.BlockSpec((tr, WIN_STEP), lambda i, k, pr: (pr[1] * nb + i, k)),
                      pl.BlockSpec((tr, tail), lambda i, k, pr: (pr[1] * nb + i, (WIN_STEP // tail) * (k + 1))),
                      pl.BlockSpec((1, tr, WIN_W), lambda i, k, pr: (k, i, 0))],
            out_specs=[pl.BlockSpec((1, tr, WIN_W), lambda i, k, pr: (k, i, 0)),
                       pl.BlockSpec((tr, WIN_W), lambda i, k, pr: (i, 0))]),
        out_shape=[jax.ShapeDtypeStruct((4, half, WIN_W), jnp.bfloat16), jax.ShapeDtypeStruct((half, WIN_W), F32)],
        compiler_params=_cparams(("parallel", "arbitrary"), 2 * tr * WIN_W * 14),
    )(place, dw, dw, sib)


def _sum_own_recv(own, recv, tr, name):
    r, c = own.shape

    def body(o_ref, r_ref, out_ref):
        v = o_ref[...]
        for j in range(3):
            v = v + r_ref[j].astype(F32)
        out_ref[...] = v

    return pl.pallas_call(
        body, name=name, grid=(r // tr,),
        in_specs=[pl.BlockSpec((tr, c), lambda i: (i, 0)), pl.BlockSpec((3, tr, c), lambda i: (0, i, 0))],
        out_specs=pl.BlockSpec((tr, c), lambda i: (i, 0)),
        out_shape=jax.ShapeDtypeStruct((r, c), F32),
        compiler_params=_cparams(("parallel",), 2 * tr * c * 14),
    )(own, recv)


def _sum_slots(stack, name):
    n, r, w = stack.shape

    def body(a_ref, out_ref):
        v = a_ref[0]
        for k in range(1, n):
            v = v + a_ref[k]
        out_ref[...] = v

    return pl.pallas_call(
        body, name=name, grid=(1,),
        in_specs=[pl.BlockSpec((n, r, w), lambda i: (0, 0, 0))],
        out_specs=pl.BlockSpec((r, w), lambda i: (0, 0)),
        out_shape=jax.ShapeDtypeStruct((r, w), F32),
        compiler_params=_cparams(("arbitrary",), 2 * (n + 1) * r * w * 4),
    )(stack)


def _adamw(w, g, m, v, tr, name):
    def body(w_ref, g_ref, m_ref, v_ref, d_ref, nm_ref, nv_ref):
        d_ref[...], nm_ref[...], nv_ref[...] = _adam_math(w_ref[...], g_ref[...], m_ref[...], v_ref[...])

    o = jax.ShapeDtypeStruct(w.shape, F32)
    return _rows_call(body, [w, g, m, v], [o, o, o], tr, name)


def _adam_math(w, g, m, v):
    nm = ADAM_B1 * m + (1.0 - ADAM_B1) * g
    nv = ADAM_B2 * v + (1.0 - ADAM_B2) * (g * g)
    m_hat = nm / (1.0 - ADAM_B1 ** ADAM_STEP)
    v_hat = nv / (1.0 - ADAM_B2 ** ADAM_STEP)
    return -ADAM_LR * (m_hat / (jnp.sqrt(v_hat) + ADAM_EPS) + ADAM_WD * w), nm, nv


def _adamw_halves(w, g_own, g_sib, m, v, place, tr, name):
    r, c = w.shape

    def body(pl_ref, w_ref, go_ref, gs_ref, m_ref, v_ref, g_ref, d_ref, nm_ref, nv_ref):
        first = pl_ref[1] == 0
        own, sib = go_ref[...], gs_ref[...]
        g = jnp.concatenate([jnp.where(first, own, sib), jnp.where(first, sib, own)], axis=1)
        g_ref[...] = g
        d_ref[...], nm_ref[...], nv_ref[...] = _adam_math(w_ref[...], g, m_ref[...], v_ref[...])

    full = pl.BlockSpec((tr, c), lambda i, pr: (i, 0))
    half = pl.BlockSpec((tr, c // 2), lambda i, pr: (i, 0))
    o = jax.ShapeDtypeStruct((r, c), F32)
    return pl.pallas_call(
        body, name=name,
        grid_spec=pltpu.PrefetchScalarGridSpec(num_scalar_prefetch=1, grid=(r // tr,),
                                               in_specs=[full, half, half, full, full], out_specs=[full] * 4),
        out_shape=[o] * 4,
        compiler_params=_cparams(("parallel",), 2 * tr * c * 4 * 8),
    )(place, w, g_own, g_sib, m, v)


ANY = pl.BlockSpec(memory_space=pl.ANY)


def _place():
    x, y, c = lax.axis_index("x"), lax.axis_index("y"), lax.axis_index("c")
    others = [(1 - x, y), (x, 1 - y), (1 - x, 1 - y)]
    return x, y, c, 2 * x + y, others


def _remote(src, dst, send, recv, k, to):
    return pltpu.make_async_remote_copy(src_ref=src, dst_ref=dst, send_sem=send.at[k], recv_sem=recv.at[k],
                                        device_id=to, device_id_type=MESH)


def _norm_gather_call(x, norm_w, win_b, wout_b, cw8, tm):
    s = x.shape[0]
    ni = s // tm
    h_in, h_out = win_b.shape[0] // 2, wout_b.shape[0] // 2
    q_in = h_in // 2

    def body(x_ref, w_ref, win, wout, cw, xn_ref, xnt_ref, g_in, g_out, g_cw, send, recv):
        i = pl.program_id(0)

        def direct():
            xx, yy, c, me, others = _place()
            mi, mo = pl.ds(c * h_in, h_in), pl.ds(c * h_out, h_out)
            cps = [_remote(win.at[mi], g_in.at[me, mi], send, recv, j, (*others[j], c)) for j in range(2)]
            cps += [_remote(wout.at[mo], g_out.at[me, mo], send, recv, 7 + j, (*chip, c)) for j, chip in enumerate(others)]
            cps += [_remote(cw, g_cw.at[me], send, recv, 13 + j, (*chip, c)) for j, chip in enumerate(others)]
            return cps

        @pl.when(i == 0)
        def _():
            for cp in direct():
                cp.start()

        xv = x_ref[...]
        r = lax.rsqrt(jnp.mean(xv * xv, axis=-1, keepdims=True) + EPS)
        xn = xv * r * w_ref[...]
        xn_ref[...] = _c(xn)
        xnt_ref[...] = _c(xn.T)

        @pl.when(i == ni - 1)
        def _():
            xx, yy, c, me, others = _place()
            sib = (xx, yy, 1 - c)
            kx, ky, kd = (2 * chip[0] + chip[1] for chip in others)
            mi, ti = pl.ds(c * h_in, h_in), pl.ds((1 - c) * h_in, h_in)
            quarter = [pl.ds(c * h_in, q_in), pl.ds(c * h_in + q_in, q_in)]
            started = []

            def go(cp):
                cp.start()
                started.append(cp)

            _remote(g_in.at[kx, mi], g_in.at[kx, mi], send, recv, 0, (*others[0], c)).wait_recv()
            go(_remote(g_in.at[kx, quarter[0]], g_in.at[kx, quarter[0]], send, recv, 2, (*others[1], c)))
            go(_remote(g_in.at[kx, mi], g_in.at[kx, mi], send, recv, 4, sib))
            _remote(g_in.at[ky, mi], g_in.at[ky, mi], send, recv, 1, (*others[1], c)).wait_recv()
            go(_remote(g_in.at[ky, quarter[1]], g_in.at[ky, quarter[1]], send, recv, 3, (*others[0], c)))
            go(_remote(g_in.at[ky, mi], g_in.at[ky, mi], send, recv, 5, sib))
            mo, to = pl.ds(c * h_out, h_out), pl.ds((1 - c) * h_out, h_out)
            for j, chip in enumerate(others):
                kj = 2 * chip[0] + chip[1]
                _remote(g_out.at[kj, mo], g_out.at[kj, mo], send, recv, 7 + j, (*chip, c)).wait_recv()
                go(_remote(g_out.at[kj, mo], g_out.at[kj, mo], send, recv, 10 + j, sib))
            _remote(g_in.at[kd, quarter[0]], g_in.at[kd, quarter[0]], send, recv, 2, (*others[1], c)).wait_recv()
            _remote(g_in.at[kd, quarter[1]], g_in.at[kd, quarter[1]], send, recv, 3, (*others[0], c)).wait_recv()
            go(_remote(g_in.at[kd, mi], g_in.at[kd, mi], send, recv, 6, sib))
            for k_src, sem in ((kx, 4), (ky, 5), (kd, 6)):
                _remote(g_in.at[k_src, ti], g_in.at[k_src, ti], send, recv, sem, sib).wait_recv()
            for j, chip in enumerate(others):
                kj = 2 * chip[0] + chip[1]
                _remote(g_out.at[kj, to], g_out.at[kj, to], send, recv, 10 + j, sib).wait_recv()
                _remote(cw, g_cw.at[kj], send, recv, 13 + j, (*chip, c)).wait_recv()
            for cp in direct() + started:
                cp.wait_send()

    outs = [jax.ShapeDtypeStruct((s, D_MODEL), MXU_DTYPE), jax.ShapeDtypeStruct((D_MODEL, s), MXU_DTYPE)]
    outs += [jax.ShapeDtypeStruct((4,) + a.shape, a.dtype) for a in (win_b, wout_b, cw8)]
    return pl.pallas_call(
        body, name="rmsnorm_gather_weights",
        grid=(ni,),
        in_specs=[pl.BlockSpec((tm, D_MODEL), lambda i: (i, 0)), pl.BlockSpec((1, D_MODEL), lambda i: (0, 0)), ANY, ANY, ANY],
        out_specs=[pl.BlockSpec((tm, D_MODEL), lambda i: (i, 0)), pl.BlockSpec((D_MODEL, tm), lambda i: (0, i)), ANY, ANY, ANY],
        out_shape=outs,
        scratch_shapes=[pltpu.SemaphoreType.DMA((16,)), pltpu.SemaphoreType.DMA((16,))],
        compiler_params=_cparams(("arbitrary",), 2 * tm * D_MODEL * 12),
    )(x, norm_w, win_b, wout_b, cw8)


def _dw_out_rs_call(a, b, dw, *, tn, tk):
    m, k = a.shape
    n = b.shape[1]
    nj, nk = n // tn, k // tk
    half = dw.shape[0] // 2

    def body(a_ref, b_ref, pin, o_ref, sib_in, send, recv):
        j, kk = pl.program_id(0), pl.program_id(1)

        def copies():
            x, y, c, me, others = _place()
            rows = pl.ds((1 - c) * half, half)
            return [_remote(pin.at[rows, pl.ds(WIN_STEP * w, WIN_W)], sib_in.at[w], send, recv, w, (x, y, 1 - c)) for w in range(4)]

        @pl.when((j == 0) & (kk == 0))
        def _():
            for cp in copies():
                cp.start()

        @pl.when(kk == 0)
        def _():
            o_ref[...] = jnp.zeros_like(o_ref)

        o_ref[...] += _dot(a_ref[...], b_ref[...])

        @pl.when((j == nj - 1) & (kk == nk - 1))
        def _():
            cps = copies()
            for cp in cps:
                cp.wait_recv()
            for cp in cps:
                cp.wait_send()

    isz = jnp.dtype(a.dtype).itemsize
    est = 2 * (m * tk + tk * tn) * isz + 2 * m * tn * 4
    outs = [jax.ShapeDtypeStruct((m, n), F32), jax.ShapeDtypeStruct((4, half, WIN_W), dw.dtype)]
    return pl.pallas_call(
        body, name="dw_out_rs_sibling",
        grid=(nj, nk),
        in_specs=[pl.BlockSpec((m, tk), lambda j, kk: (0, kk)), pl.BlockSpec((tk, tn), lambda j, kk: (kk, j)), ANY],
        out_specs=[pl.BlockSpec((m, tn), lambda j, kk: (0, j)), ANY],
        out_shape=outs,
        scratch_shapes=[pltpu.SemaphoreType.DMA((4,)), pltpu.SemaphoreType.DMA((4,))],
        compiler_params=_cparams(("arbitrary", "arbitrary"), est),
    )(a, b, dw)


def _rs_sibling_call(p_out, vsmall):
    def body(pout, vs, sib_out, sib_v, send, recv):
        x, y, c, me, others = _place()
        sib = (x, y, 1 - c)
        half = pout.shape[1] // 2
        cps = [_remote(pout.at[:, pl.ds((1 - c) * half, half)], sib_out, send, recv, 0, sib),
               _remote(vs, sib_v, send, recv, 1, sib)]
        for cp in cps:
            cp.start()
        for cp in cps:
            cp.wait_recv()
        for cp in cps:
            cp.wait_send()

    outs = [jax.ShapeDtypeStruct((4, p_out.shape[1] // 2, p_out.shape[2]), p_out.dtype),
            jax.ShapeDtypeStruct(vsmall.shape, vsmall.dtype)]
    return pl.pallas_call(
        body, name="rs_sibling",
        in_specs=[ANY] * 2, out_specs=[ANY] * 2, out_shape=outs,
        scratch_shapes=[pltpu.SemaphoreType.DMA((2,)), pltpu.SemaphoreType.DMA((2,))],
    )(p_out, vsmall)


def _rs_join_call(f_in, f_out, nw8):
    def body(fin, fout, nw, sib_in, full_out, all_nw, send, recv):
        x, y, c, me, others = _place()
        sib = (x, y, 1 - c)
        half = fout.shape[0]
        cps = [_remote(fin, sib_in, send, recv, 0, sib),
               _remote(fout, full_out.at[pl.ds(c * half, half)], send, recv, 1, sib)]
        mine = 4 * x + 2 * y + c
        peers = []
        for r in range(1, 8):
            px, py, pc = (1 - x if r & 4 else x), (1 - y if r & 2 else y), (1 - c if r & 1 else c)
            peers.append((r, (px, py, pc), 4 * px + 2 * py + pc))
            cps.append(_remote(nw, all_nw.at[mine], send, recv, 1 + r, (px, py, pc)))
        for cp in cps:
            cp.start()
        cps[0].wait_recv()
        _remote(fout, full_out.at[pl.ds((1 - c) * half, half)], send, recv, 1, sib).wait_recv()
        for r, peer, idx in peers:
            _remote(nw, all_nw.at[idx], send, recv, 1 + r, peer).wait_recv()
        for cp in cps:
            cp.wait_send()

    outs = [jax.ShapeDtypeStruct(f_in.shape, F32), jax.ShapeDtypeStruct((2 * f_out.shape[0], f_out.shape[1]), F32),
            jax.ShapeDtypeStruct((8,) + nw8.shape, F32)]
    return pl.pallas_call(
        body, name="rs_join",
        in_specs=[ANY] * 3, out_specs=[ANY] * 3, out_shape=outs,
        scratch_shapes=[pltpu.SemaphoreType.DMA((9,)), pltpu.SemaphoreType.DMA((9,))],
    )(f_in, f_out, nw8)


def _pack(arrs):
    parts = []
    for a in arrs:
        f = a.reshape(-1).astype(F32)
        pad = (-f.shape[0]) % 1024
        parts.append(jnp.pad(f, (0, pad)).reshape(-1, 128))
    return jnp.concatenate(parts, axis=0)


def _unpack(packed, shapes):
    out, row = [], 0
    for shp in shapes:
        n = 1
        for d in shp:
            n *= d
        rows = (n + 1023) // 1024 * 8
        out.append(packed[row:row + rows].reshape(-1)[:n].reshape(shp))
        row += rows
    return out


def _expand_heads(v32):
    return jnp.repeat(v32.reshape(32), HEADDIM).reshape(1, D_MODEL)


def kernel(x, norm_w, w_in, gate_b, sgu_norm_g, sgu_norm_b, sgu_w, sgu_b, conv_w, conv_b, dt_bias, A_log, D_skip, ssd_norm_w, w_out, final_norm_w, loss_target, m_norm_w, m_w_in, m_gate_b, m_sgu_norm_g, m_sgu_norm_b, m_sgu_w, m_sgu_b, m_conv_w, m_conv_b, m_dt_bias, m_A_log, m_D_skip, m_ssd_norm_w, m_w_out, m_final_norm_w, v_norm_w, v_w_in, v_gate_b, v_sgu_norm_g, v_sgu_norm_b, v_sgu_w, v_sgu_b, v_conv_w, v_conv_b, v_dt_bias, v_A_log, v_D_skip, v_ssd_norm_w, v_w_out, v_final_norm_w):
    s = x.shape[1]
    x2 = x.reshape(s, D_MODEL)
    tgt = loss_target.reshape(s, D_MODEL)
    t_ssd, t_tok, t_out, t_row = min(T_SSD, s), min(T_TOK, s), min(T_OUT, s), min(T_ROW, s)
    tm_mm, tk_dw = min(TM_MM, s), min(TK_DW, s)
    chip = 2 * lax.axis_index("x") + lax.axis_index("y")

    cw8 = jnp.pad(conv_w[0], ((0, 4), (0, 0)))
    win_b, wout_b = _c(w_in[0]), _c(w_out[0])
    xn, xnt, g_in, g_out, g_cw = _norm_gather_call(x2, norm_w, win_b, wout_b, cw8, t_row)
    g_in = lax.dynamic_update_index_in_dim(g_in, win_b, chip, 0)
    g_out = lax.dynamic_update_index_in_dim(g_out, wout_b, chip, 0)
    g_cw = lax.dynamic_update_index_in_dim(g_cw, cw8, chip, 0)
    wt = jnp.transpose(g_in, (0, 2, 1)).reshape(IN_W, D_MODEL)
    w_out_full = g_out.reshape(D_MODEL, D_MODEL)
    conv_w_full = jnp.transpose(g_cw[:, 0:4, :], (1, 0, 2)).reshape(4, 3072)

    wt_a = jnp.concatenate([wt[0:6144], wt[11296:15392]], axis=0)
    per_group = lambda lo, n: wt[lo:lo + SSD_GROUPS * n].reshape(SSD_GROUPS, n, D_MODEL)
    wt_b = jnp.concatenate([per_group(6144, 512), per_group(8192, 512), per_group(10240, 128), per_group(10752, 128),
                            jnp.pad(per_group(11264, 8), ((0, 0), (0, 120), (0, 0)))], axis=1)
    wt_b = wt_b.reshape(SSD_GROUPS * BW_B, D_MODEL)

    def group_cols(full_xs, full_bc):
        parts = []
        for g in range(SSD_GROUPS):
            parts += [full_xs[:, 512 * g:512 * g + 512], full_bc[:, 128 * g:128 * g + 128], full_bc[:, 512 + 128 * g:512 + 128 * g + 128]]
        return jnp.concatenate(parts, axis=1)

    cw_g = group_cols(conv_w_full[:, 0:2048], conv_w_full[:, 2048:3072])
    cb_g = group_cols(conv_b[:, 0:2048], conv_b[:, 2048:3072])
    alog_e, dsk_e = _expand_heads(A_log), _expand_heads(D_skip)
    narrow = lambda v32: jnp.pad(v32.reshape(SSD_GROUPS, 8), ((0, 0), (0, 120))).reshape(1, SSD_GROUPS * 128)
    dtb_n, alog_n = narrow(dt_bias), narrow(A_log)

    pos_chunk = jnp.arange(SGU_BLOCK) // CHUNK
    smask = pos_chunk[None, :] <= pos_chunk[:, None]
    wm_f = jnp.where(smask[None], sgu_w[0], 0.0)
    wm = _c(wm_f)
    wmt = _c(jnp.transpose(wm_f, (0, 2, 1)))
    bias_full = jnp.repeat(sgu_b[0].T, D_MODEL // SGU_GROUPS, axis=1)
    fnw = final_norm_w.reshape(1, D_MODEL)

    proj_a = _mm(xn, wt_a, tm=tm_mm, tn=1024, tk=D_MODEL, name="in_proj_a", out_dtype=MXU_DTYPE, b_is_t=True)
    proj_b = _mm(xn, wt_b, tm=tm_mm, tn=BW_B, tk=D_MODEL, name="in_proj_b", b_is_t=True)
    y_ssd, y_b, hprev, pre_all, dt_all, acs_all = _ssd_fwd_call(proj_b, dtb_n, alog_n, dsk_e, cw_g, cb_g, ssd_norm_w, t_ssd, NG_SSD)
    y_a, merged, merged_t = _tok_fwd_call(proj_a, y_b, gate_b, sgu_norm_g, sgu_norm_b, wm, bias_full, t_tok)
    dh_b, dmerged, loss_t, dfw8 = _out_call(merged, x2, tgt, w_out_full, fnw, t_out)

    dproj_a, dy_b, dgb8, dgam8, dbeta8, dbfull, dws = _tok_bwd_call(
        proj_a, dmerged, y_a, y_b, gate_b, sgu_norm_g, sgu_norm_b, wm, wmt, bias_full, t_tok)
    dproj_b, a512, a768 = _ssd_bwd_call(proj_b, pre_all, dt_all, acs_all, dy_b, y_ssd, hprev, dtb_n, alog_e, dsk_e, cw_g,
                                        ssd_norm_w, t_ssd, NG_SSD)
    dw_uvz = _mm(xnt, dproj_a, tm=D_MODEL, tn=1024, tk=min(2 * tk_dw, s), name="dw_in_uvz", n=6144, out_dtype=MXU_DTYPE)
    dw_gate = _mm(xnt, dproj_a, tm=D_MODEL, tn=1024, tk=min(2 * tk_dw, s), name="dw_in_gate", col0=6, n=4096,
                  out_dtype=MXU_DTYPE)
    dw_zb, dw_xs, dw_bm, dw_cm, dw_dt = _dw_groups(xnt, dproj_b, tk=tk_dw, out_dtype=MXU_DTYPE)

    dw_dt32 = jnp.concatenate([dw_dt[:, 128 * g:128 * g + 8] for g in range(SSD_GROUPS)], axis=1)
    dw_ref = jnp.concatenate([dw_uvz, dw_zb, dw_xs, dw_bm, dw_cm, dw_dt32, dw_gate,
                              jnp.zeros((D_MODEL, 3 * WIN_STEP + WIN_W - IN_W), MXU_DTYPE)], axis=1)
    dw_out_p, sib_i = _dw_out_rs_call(merged_t, dh_b, dw_ref, tn=1024, tk=tk_dw)
    p_out = dw_out_p.reshape(4, D_MODEL // 4, D_MODEL)

    s512 = jnp.sum(a512, axis=2)
    heads = lambda v: jnp.sum(v.reshape(32, HEADDIM), axis=1).reshape(1, 32)
    d_ssd_nw = s512[:, 0].reshape(1, D_MODEL)
    d_dskip = heads(s512[:, 1].reshape(D_MODEL))
    d_alog = heads(s512[:, 2].reshape(D_MODEL)) * (1.0 / HEADDIM) * (-jnp.exp(A_log))
    d_dtb = s512[:, 3, 0:8].reshape(1, 32)
    s768 = jnp.sum(a768, axis=2)
    ungroup = lambda v: jnp.concatenate([v[g, :, 0:512] for g in range(4)] + [v[g, :, 512:640] for g in range(4)]
                                        + [v[g, :, 640:768] for g in range(4)], axis=1)
    d_cw = ungroup(s768[:, 0:4])
    d_cb = ungroup(s768[:, 4:5])
    d_sgu_b = jnp.sum(dbfull.reshape(128, SGU_GROUPS, 128), axis=2).T.reshape(1, SGU_GROUPS, 128)
    d_sgu_w = jnp.where(smask[None], dws, 0.0).reshape(1, SGU_GROUPS, 128, 128)
    fold = lambda a8: jnp.sum(a8, axis=0, keepdims=True)
    small_local = [fold(dgb8), fold(dgam8), fold(dbeta8), d_sgu_w, d_sgu_b, d_cw, d_cb,
                   d_dtb, d_alog, d_dskip, d_ssd_nw, fold(dfw8).reshape(D_MODEL), jnp.sum(loss_t[:, 0, 0]).reshape(1)]
    small_shapes = [a.shape for a in small_local]
    v_local = _pack(small_local)

    core = lax.axis_index("c")
    place = jnp.stack([chip, core]).astype(jnp.int32)
    hr_i, hr_o = D_MODEL // 2, D_MODEL // 8
    sib_o, sib_v = _rs_sibling_call(p_out, v_local)
    s1b_i, o_i = _rs_add_windows(dw_ref, sib_i, place, 256, "rs_add_in")
    s1b_o, o_o = _rs_add(p_out, sib_o, place, 256, "rs_add_out")
    chip_v = _add_pair(v_local, sib_v, v_local.shape[0], "ar_add_small")
    dxn, r_i, r_o, abs_v = _dx_rs_call(dproj_a, wt_a, dproj_b, wt_b, s1b_i, s1b_o, chip_v, tm=tm_mm)
    grad_x, dnw8 = _gradx_call(x2, dxn, dh_b, norm_w, t_row)
    abs_v = lax.dynamic_update_index_in_dim(abs_v, chip_v, chip, 0)
    f_i = _sum_own_recv(o_i, r_i, 256, "rs_sum_in")
    f_o = _sum_own_recv(o_o, r_o, 256, "rs_sum_out")
    sib_f_i, g_w_out, all_nw = _rs_join_call(f_i, f_o, dnw8)
    g_w_out = lax.dynamic_update_slice_in_dim(g_w_out, f_o, core * hr_o, axis=0)
    all_nw = lax.dynamic_update_index_in_dim(all_nw, dnw8, 2 * chip + core, 0)
    g_nw = fold(_sum_slots(all_nw, "ar_sum_norm_w"))
    total_v = _sum_slots(abs_v, "ar_sum_small")
    (g_gb, g_gam, g_beta, g_sw, g_sb, g_cw_full, g_cb, g_dtb, g_alog, g_dsk, g_snw, g_fnw, loss1) = _unpack(total_v, small_shapes)
    g_cw_shard = lax.dynamic_slice(g_cw_full, (0, chip * 768), (4, 768)).reshape(1, 4, 768)
    loss = loss1.reshape(())

    shard_t = lambda win: lax.dynamic_slice_in_dim(win, 8 * chip, SHARD_W, axis=1).T
    g_w_in, d_win, nm_win, nv_win = (a.T for a in _adamw_halves(w_in[0].T, shard_t(f_i), shard_t(sib_f_i), m_w_in[0].T,
                                                                v_w_in[0].T, place, 296, "adamw_w_in"))
    d_wout, nm_wout, nv_wout = _adamw(w_out[0], g_w_out, m_w_out[0], v_w_out[0], 128, "adamw_w_out")
    small_w = [norm_w, gate_b, sgu_norm_g, sgu_norm_b, sgu_w, sgu_b, conv_w, conv_b, dt_bias, A_log, D_skip, ssd_norm_w, final_norm_w]
    small_m = [m_norm_w, m_gate_b, m_sgu_norm_g, m_sgu_norm_b, m_sgu_w, m_sgu_b, m_conv_w, m_conv_b, m_dt_bias, m_A_log, m_D_skip, m_ssd_norm_w, m_final_norm_w]
    small_v = [v_norm_w, v_gate_b, v_sgu_norm_g, v_sgu_norm_b, v_sgu_w, v_sgu_b, v_conv_w, v_conv_b, v_dt_bias, v_A_log, v_D_skip, v_ssd_norm_w, v_final_norm_w]
    small_g = [g_nw, g_gb, g_gam, g_beta, g_sw, g_sb, g_cw_shard, g_cb, g_dtb, g_alog, g_dsk, g_snw, g_fnw]
    shapes_w = [a.shape for a in small_w]
    small_g = [a.reshape(shp) for a, shp in zip(small_g, shapes_w)]
    pw = _pack(small_w)
    pd, pm, pv = _adamw(pw, _pack(small_g), _pack(small_m), _pack(small_v), pw.shape[0], "adamw_small")
    d_small, nm_small, nv_small = _unpack(pd, shapes_w), _unpack(pm, shapes_w), _unpack(pv, shapes_w)

    def with_big(small, win, wout):
        o = list(small)
        return o[0:1] + [win.reshape(1, D_MODEL, SHARD_W)] + o[1:12] + [wout.reshape(1, D_MODEL // 4, D_MODEL)] + o[12:13]

    grads = with_big(small_g, g_w_in, g_w_out)
    deltas = with_big(d_small, d_win, d_wout)
    new_m = with_big(nm_small, nm_win, nm_wout)
    new_v = with_big(nv_small, nv_win, nv_wout)
    return (loss, grad_x.reshape(1, s, D_MODEL), *grads, *deltas, *new_m, *new_v)
```

```python
import functools

import jax
import jax.numpy as jnp
from jax import lax
from jax.experimental import pallas as pl
from jax.experimental.pallas import tpu as pltpu

F32 = jnp.float32
MXU_DTYPE = jnp.bfloat16

D_MODEL = 2048
EPS = 1e-5
CHUNK = 64
SGU_BLOCK = 128
SGU_GROUPS = 16
SSD_GROUPS = 4
SSD_GW = 512
SSD_STATE = 128
HEADDIM = 64
IN_W = 15392
SHARD_W = IN_W // 4
BW_B = 1408
FW_B = BW_B
XBC_O, DT_O = 512, 1280
NA = 10240

ADAM_LR = 0.001
ADAM_B1 = 0.9
ADAM_B2 = 0.999
ADAM_EPS = 1e-08
ADAM_WD = 0.01
ADAM_STEP = 10

T_SSD = 256
NG_SSD = 4
T_TOK = 128
T_OUT = 256
T_ROW = 512
TM_MM = 1024
TK_DW = 1024
VMEM_CAP = 60 * 1024 * 1024
MESH = pl.DeviceIdType.MESH


def _cparams(sem, est_bytes):
    lim = int(min(VMEM_CAP, max(32 * 1024 * 1024, est_bytes + 12 * 1024 * 1024)))
    return pltpu.CompilerParams(dimension_semantics=sem, vmem_limit_bytes=lim)


def _c(x):
    return x.astype(MXU_DTYPE)


def _dot(a, b):
    return jnp.dot(a, b, preferred_element_type=F32)


def _dot_nt(a, b):
    return lax.dot_general(a, b, (((1,), (1,)), ((), ())), preferred_element_type=F32)


def _dot_tn(a, b):
    return lax.dot_general(a, b, (((0,), (0,)), ((), ())), preferred_element_type=F32)


def _split(x, n):
    parts, r = [], x
    for _ in range(n):
        p = _c(r)
        parts.append(p)
        r = r - p.astype(F32)
    return parts


def _dot01_l(m01, x, n):
    acc = None
    for p in _split(x, n):
        t = _dot(m01, p)
        acc = t if acc is None else acc + t
    return acc


def _dot01_r(x, m01, n):
    acc = None
    for p in _split(x, n):
        t = _dot(p, m01)
        acc = t if acc is None else acc + t
    return acc


def _sigmoid(x):
    return 0.5 * jnp.tanh(0.5 * x) + 0.5


def _fold8(x):
    r, w = x.shape
    return jnp.sum(x.reshape(r // 8, 8, w), axis=0)


def _iota(shape, dim):
    return lax.broadcasted_iota(jnp.int32, shape, dim)


def _ssd_masks():
    l64 = _iota((CHUNK, SSD_GW), 0)
    s64 = jnp.bitwise_and(_iota((CHUNK, SSD_GW), 1), CHUNK - 1)
    diag = l64 == s64
    causal = l64 >= s64
    row_last = l64 == CHUNK - 1
    r4 = lax.shift_right_logical(_iota((256, 256), 0), 6)
    c4 = lax.shift_right_logical(_iota((256, 256), 1), 6)
    mask4 = r4 == c4
    return diag, causal, row_last, mask4


def _cumsum_mats(t):
    r, c = _iota((t, t), 0), _iota((t, t), 1)
    same = lax.shift_right_logical(r, 6) == lax.shift_right_logical(c, 6)
    tri = _c(jnp.where(same, jnp.where(c <= r, 1.0, 0.0), 0.0))
    trit = _c(jnp.where(same, jnp.where(c >= r, 1.0, 0.0), 0.0))
    return tri, trit


def _head_expand_mat():
    return _c(jnp.where(_iota((128, SSD_GW), 0) == lax.shift_right_logical(_iota((128, SSD_GW), 1), 6), 1.0, 0.0))


def _ssd_common(xs, bm, cm, dt, acs, masks):
    diag, causal, row_last, mask4 = masks
    row_e = jnp.sum(jnp.where(diag, acs, 0.0), axis=0, keepdims=True)
    seg = acs - row_e
    lm = jnp.exp(jnp.where(causal, seg, -1e30))
    bb, cb = _c(bm), _c(cm)
    brep = jnp.concatenate([bb] * 8, axis=0)
    cbrep = _dot_nt(cb, brep)
    m = cbrep * lm
    xdt = xs * dt
    acs_last = jnp.sum(jnp.where(row_last, acs, 0.0), axis=0, keepdims=True)
    dec = jnp.exp(acs_last - acs)
    eacs = jnp.exp(acs)
    cd = jnp.exp(acs_last)
    return dict(lm=lm, bb=bb, cb=cb, brep=brep, m=m, xdt=xdt, dec=dec, eacs=eacs, cd=cd)


def _blockdiag4(xb, mask4):
    return jnp.where(mask4, jnp.concatenate([xb] * 4, axis=0), jnp.zeros((), xb.dtype))


def _ssd_chunk_fwd(xs, bm, cm, dt, acs, d_skip, ht, masks):
    q = _ssd_common(xs, bm, cm, dt, acs, masks)
    mask4 = masks[3]
    mb, xdtb = _c(q["m"]), _c(q["xdt"])
    yd = []
    for blk in range(2):
        sl = slice(256 * blk, 256 * blk + 256)
        yd.append(_dot(mb[:, sl], _blockdiag4(xdtb[:, sl], mask4)))
    y_diag = jnp.concatenate(yd, axis=1)
    p = _dot(q["cb"], _c(ht))
    y = y_diag + p * q["eacs"] + xs * d_skip
    st = _dot_tn(q["bb"], _c(q["xdt"] * q["dec"]))
    return y, ht * q["cd"] + st


def _ssd_chunk_bwd(xs, bm, cm, dt, acs, d_skip, hprev, dht, dy, masks):
    diag, causal, row_last, mask4 = masks
    q = _ssd_common(xs, bm, cm, dt, acs, masks)
    lm, bb, cb, brep, m, xdt, dec, eacs, cd = (q[k] for k in ("lm", "bb", "cb", "brep", "m", "xdt", "dec", "eacs", "cd"))
    hb = _c(hprev)
    yoff = _dot(cb, hb) * eacs
    dyb = _c(dy)
    dpb = _c(dy * eacs)
    d_c = _dot_nt(dpb, hb)
    dh_y = _dot_tn(cb, dpb)
    mb, xdtb = _c(m), _c(xdt)
    dm_parts, dxdt_parts = [], []
    for blk in range(2):
        sl = slice(256 * blk, 256 * blk + 256)
        bd = _blockdiag4(xdtb[:, sl], mask4)
        dm_parts.append(_dot_nt(dyb[:, sl], bd))
        dxf = jnp.where(mask4, _dot_tn(mb[:, sl], dyb[:, sl]), 0.0)
        dxdt_parts.append(dxf[0:64] + dxf[64:128] + dxf[128:192] + dxf[192:256])
    dm = jnp.concatenate(dm_parts, axis=1)
    dxdt = jnp.concatenate(dxdt_parts, axis=1)
    dcbb = _c(dm * lm)
    g = dm * m
    d_c = d_c + _dot(dcbb, brep)
    dbrep = _dot_tn(dcbb, cb)
    d_b = dbrep[0:64]
    for r in range(1, 8):
        d_b = d_b + dbrep[64 * r:64 * r + 64]
    dhtb = _c(dht)
    dxd = _dot(bb, dhtb)
    xd = xdt * dec
    dxdt = dxdt + dxd * dec
    tq = dxd * xd
    d_b = d_b + _dot_nt(_c(xd), dhtb)
    dcd = jnp.sum(dht * hprev, axis=0, keepdims=True)
    col_g = jnp.sum(g, axis=0, keepdims=True)
    last = jnp.sum(tq, axis=0, keepdims=True) + dcd * cd
    qq = g - jnp.where(diag, col_g, 0.0) + dy * yoff - tq + jnp.where(row_last, last, 0.0)
    dxs = dxdt * dt + dy * d_skip
    return dxs, d_b, d_c, dht * cd + dh_y, dy * xs, qq, dxdt * xs


def _ssd_finish_dt(qq, p1, dt, a_neg, trit, mask4):
    bd4 = _c(jnp.where(mask4, 1.0, 0.0))
    dacs = jnp.concatenate([_dot01_r(qq[:, 256 * b:256 * b + 256], bd4, 2) for b in range(2)], axis=1)
    da = _dot01_l(trit, dacs, 2)
    return p1 + da * (a_neg * (1.0 / HEADDIM)), da * dt


def _softplus(x):
    return jnp.maximum(x, 0.0) + jnp.log(1.0 + jnp.exp(-jnp.abs(x)))


def _conv_taps(xpad, t):
    taps = []
    for k in range(4):
        sh = 3 - k
        v = xpad if sh == 0 else pltpu.roll(xpad, sh, 0)
        taps.append(v[8:8 + t])
    return taps


def _mm(a, b, *, tm, tn, tk, name, out_dtype=F32, col0=0, n=None, b_is_t=False):
    m, k = a.shape
    n = b.shape[0 if b_is_t else 1] if n is None else n
    nk = k // tk
    assert m % tm == 0 and n % tn == 0 and k % tk == 0, (a.shape, b.shape, tm, tn, tk)
    dot = _dot_nt if b_is_t else _dot
    via_acc = nk > 1 and out_dtype != F32

    def body(a_ref, b_ref, o_ref, *acc):
        if nk == 1:
            o_ref[...] = dot(a_ref[...], b_ref[...]).astype(out_dtype)
            return
        acc_ref = acc[0] if via_acc else o_ref

        @pl.when(pl.program_id(2) == 0)
        def _():
            acc_ref[...] = jnp.zeros_like(acc_ref)

        acc_ref[...] += dot(a_ref[...], b_ref[...])
        if via_acc:
            @pl.when(pl.program_id(2) == nk - 1)
            def _():
                o_ref[...] = acc_ref[...].astype(out_dtype)

    isz = jnp.dtype(a.dtype).itemsize
    est = 2 * (tm * tk + tk * tn) * isz + 3 * tm * tn * 4
    return pl.pallas_call(
        body, name=name,
        grid=(m // tm, n // tn, nk),
        in_specs=[pl.BlockSpec((tm, tk), lambda i, j, kk: (i, kk)),
                  pl.BlockSpec((tn, tk), lambda i, j, kk: (j + col0, kk)) if b_is_t
                  else pl.BlockSpec((tk, tn), lambda i, j, kk: (kk, j + col0))],
        out_specs=pl.BlockSpec((tm, tn), lambda i, j, kk: (i, j)),
        out_shape=jax.ShapeDtypeStruct((m, n), out_dtype),
        scratch_shapes=[pltpu.VMEM((tm, tn), F32)] if via_acc else [],
        compiler_params=_cparams(("parallel", "parallel", "arbitrary"), est),
    )(a, b)


def _dw_groups(xnt, dpb, *, tk, out_dtype):
    m, k = xnt.shape
    nk = k // tk

    def body(a_ref, b_ref, zb_ref, xs_ref, bm_ref, cm_ref, dt_ref, acc_ref):
        @pl.when(pl.program_id(1) == 0)
        def _():
            acc_ref[...] = jnp.zeros_like(acc_ref)

        acc_ref[...] += _dot(a_ref[...], b_ref[...])

        @pl.when(pl.program_id(1) == nk - 1)
        def _():
            for o_ref, lo, hi in ((zb_ref, 0, 512), (xs_ref, 512, 1024), (bm_ref, 1024, 1152), (cm_ref, 1152, 1280),
                                  (dt_ref, 1280, 1408)):
                o_ref[...] = acc_ref[:, lo:hi].astype(out_dtype)

    isz = jnp.dtype(xnt.dtype).itemsize
    est = 2 * (m * tk + tk * BW_B) * isz + 3 * m * BW_B * 4
    piece = lambda w: pl.BlockSpec((m, w), lambda g, kk: (0, g))
    return pl.pallas_call(
        body, name="dw_in_b",
        grid=(SSD_GROUPS, nk),
        in_specs=[pl.BlockSpec((m, tk), lambda g, kk: (0, kk)), pl.BlockSpec((tk, BW_B), lambda g, kk: (kk, g))],
        out_specs=[piece(512), piece(512), piece(128), piece(128), piece(128)],
        out_shape=[jax.ShapeDtypeStruct((m, w), out_dtype) for w in (2048, 2048, 512, 512, 512)],
        scratch_shapes=[pltpu.VMEM((m, BW_B), F32)],
        compiler_params=_cparams(("parallel", "arbitrary"), est),
    )(xnt, dpb)


def _dx_rs_call(dpa, wta, dpb, wtb, sb_in, sb_out, chip_v, *, tm):
    s = dpa.shape[0]
    tka, tkb = 1024, BW_B
    nka, nkb = dpa.shape[1] // tka, dpb.shape[1] // tkb
    ni, nk = s // tm, nka + nkb

    def body(a_ref, wa_ref, b_ref, wb_ref, sbin, sbout, cv, o_ref, rc_in, rc_out, abs_v, send, recv):
        i, kk = pl.program_id(0), pl.program_id(1)

        def copies():
            x, y, c, me, others = _place()
            sends, recvs = [], []
            for j, chip in enumerate(others):
                kj = 2 * chip[0] + chip[1]
                to = (*chip, c)
                sends += [_remote(sbin.at[kj], rc_in.at[j], send, recv, j, to),
                          _remote(sbout.at[kj], rc_out.at[j], send, recv, 3 + j, to),
                          _remote(cv, abs_v.at[me], send, recv, 6 + j, to)]
                recvs += [sends[-3], sends[-2], _remote(cv, abs_v.at[kj], send, recv, 6 + j, to)]
            return sends, recvs

        @pl.when((i == 0) & (kk == 0))
        def _():
            for cp in copies()[0]:
                cp.start()

        @pl.when(kk == 0)
        def _():
            o_ref[...] = jnp.zeros_like(o_ref)

        @pl.when(kk < nka)
        def _():
            o_ref[...] += _dot(a_ref[...], wa_ref[...])

        @pl.when(kk >= nka)
        def _():
            o_ref[...] += _dot(b_ref[...], wb_ref[...])

        @pl.when((i == ni - 1) & (kk == nk - 1))
        def _():
            sends, recvs = copies()
            for cp in recvs:
                cp.wait_recv()
            for cp in sends:
                cp.wait_send()

    isz = jnp.dtype(dpa.dtype).itemsize
    est = 2 * isz * (tm * tka + tka * D_MODEL + tm * tkb + tkb * D_MODEL) + 2 * tm * D_MODEL * 4
    outs = [jax.ShapeDtypeStruct((s, D_MODEL), F32),
            jax.ShapeDtypeStruct((3,) + sb_in.shape[1:], sb_in.dtype), jax.ShapeDtypeStruct((3,) + sb_out.shape[1:], sb_out.dtype),
            jax.ShapeDtypeStruct((4,) + chip_v.shape, F32)]
    return pl.pallas_call(
        body, name="dx_matmul_rs_chips",
        grid=(ni, nk),
        in_specs=[
            pl.BlockSpec((tm, tka), lambda i, kk: (i, jnp.minimum(kk, nka - 1))),
            pl.BlockSpec((tka, D_MODEL), lambda i, kk: (jnp.minimum(kk, nka - 1), 0)),
            pl.BlockSpec((tm, tkb), lambda i, kk: (i, jnp.maximum(kk - nka, 0))),
            pl.BlockSpec((tkb, D_MODEL), lambda i, kk: (jnp.maximum(kk - nka, 0), 0)),
            ANY, ANY, ANY,
        ],
        out_specs=[pl.BlockSpec((tm, D_MODEL), lambda i, kk: (i, 0)), ANY, ANY, ANY],
        out_shape=outs,
        scratch_shapes=[pltpu.SemaphoreType.DMA((9,)), pltpu.SemaphoreType.DMA((9,))],
        compiler_params=_cparams(("arbitrary", "arbitrary"), est),
    )(dpa, wta, dpb, wtb, sb_in, sb_out, chip_v)


def _gradx_call(x, dxn, dh, norm_w, tm):
    s = x.shape[0]

    def body(x_ref, g_ref, dh_ref, w_ref, gx_ref, dw_ref):
        @pl.when(pl.program_id(0) == 0)
        def _():
            dw_ref[...] = jnp.zeros_like(dw_ref)

        xv, gv = x_ref[...], g_ref[...]
        r = lax.rsqrt(jnp.mean(xv * xv, axis=-1, keepdims=True) + EPS)
        gw = gv * w_ref[...]
        gx_ref[...] = r * gw - xv * (r * r * r) * jnp.mean(xv * gw, axis=-1, keepdims=True) + dh_ref[...].astype(F32)
        dw_ref[...] += _fold8(gv * (xv * r))

    row = pl.BlockSpec((tm, D_MODEL), lambda i: (i, 0))
    return pl.pallas_call(
        body, name="grad_x",
        grid=(s // tm,),
        in_specs=[row, row, row, pl.BlockSpec((1, D_MODEL), lambda i: (0, 0))],
        out_specs=[row, pl.BlockSpec((8, D_MODEL), lambda i: (0, 0))],
        out_shape=[jax.ShapeDtypeStruct((s, D_MODEL), F32), jax.ShapeDtypeStruct((8, D_MODEL), F32)],
        compiler_params=_cparams(("arbitrary",), 2 * tm * D_MODEL * 16),
    )(x, dxn, dh, norm_w)


def _layernorm_stats(v):
    mu = jnp.mean(v, axis=-1, keepdims=True)
    vc = v - mu
    var = jnp.mean(vc * vc, axis=-1, keepdims=True)
    return vc * lax.rsqrt(var + EPS), lax.rsqrt(var + EPS)


def _tok_fwd_call(proj_a, y_b, gate_b, sgu_g, sgu_beta, wm, bias_full, t):
    s = proj_a.shape[0]

    def body(pa_ref, yb_ref, gb_ref, g_ref, be_ref, wm_ref, bf_ref, ya_ref, mg_ref, mgt_ref, mix_ref):
        u = pa_ref[:, 0:2048].astype(F32)
        v = pa_ref[:, 2048:4096].astype(F32)
        za = pa_ref[:, 4096:6144].astype(F32)
        xhat, _ = _layernorm_stats(v)
        vnb = _c(xhat * g_ref[...] + be_ref[...])
        for gi in range(SGU_GROUPS):
            sl = slice(128 * gi, 128 * gi + 128)
            mix_ref[:, sl] = _dot(wm_ref[gi], vnb[:, sl])
        mixed = mix_ref[...] + bf_ref[...]
        y_a = u * mixed * (za * _sigmoid(za))
        g0 = _sigmoid(pa_ref[:, 6144:8192].astype(F32) + gb_ref[:, 0:2048])
        g1 = _sigmoid(pa_ref[:, 8192:10240].astype(F32) + gb_ref[:, 2048:4096])
        merged = g0 * y_a + g1 * yb_ref[...].astype(F32)
        ya_ref[...] = _c(y_a)
        mg_ref[...] = _c(merged)
        mgt_ref[...] = _c(merged.T)

    row = pl.BlockSpec((t, D_MODEL), lambda i: (i, 0))
    vec = lambda w: pl.BlockSpec((1, w), lambda i: (0, 0))
    return pl.pallas_call(
        body, name="tok_fwd",
        grid=(s // t,),
        in_specs=[pl.BlockSpec((t, NA), lambda i: (i, 0)), row, vec(4096), vec(2048), vec(2048),
                  pl.BlockSpec((SGU_GROUPS, 128, 128), lambda i: (0, 0, 0)), pl.BlockSpec((128, D_MODEL), lambda i: (0, 0))],
        out_specs=[row, row, pl.BlockSpec((D_MODEL, t), lambda i: (0, i))],
        out_shape=[jax.ShapeDtypeStruct((s, D_MODEL), MXU_DTYPE), jax.ShapeDtypeStruct((s, D_MODEL), MXU_DTYPE),
                   jax.ShapeDtypeStruct((D_MODEL, s), MXU_DTYPE)],
        scratch_shapes=[pltpu.VMEM((t, D_MODEL), F32)],
        compiler_params=_cparams(("parallel",), 2 * t * NA * 4 + 12 * t * D_MODEL * 4),
    )(proj_a, y_b, gate_b, sgu_g, sgu_beta, wm, bias_full)


def _tok_bwd_call(proj_a, dmerged, y_a, y_b, gate_b, sgu_g, sgu_beta, wm, wmt, bias_full, t):
    s = proj_a.shape[0]

    def body(pa_ref, dm_ref, ya_ref, yb_ref, gb_ref, g_ref, be_ref, wm_ref, wmt_ref, bf_ref,
             dpa_ref, dyb_ref, dgb_ref, dgam_ref, dbeta_ref, dbf_ref, dws_ref, mix_ref, dvn_ref):
        @pl.when(pl.program_id(0) == 0)
        def _():
            dgb_ref[...] = jnp.zeros_like(dgb_ref)
            dgam_ref[...] = jnp.zeros_like(dgam_ref)
            dbeta_ref[...] = jnp.zeros_like(dbeta_ref)
            dbf_ref[...] = jnp.zeros_like(dbf_ref)
            dws_ref[...] = jnp.zeros_like(dws_ref)

        u = pa_ref[:, 0:2048].astype(F32)
        v = pa_ref[:, 2048:4096].astype(F32)
        za = pa_ref[:, 4096:6144].astype(F32)
        xhat, rstd = _layernorm_stats(v)
        vnb = _c(xhat * g_ref[...] + be_ref[...])
        for gi in range(SGU_GROUPS):
            sl = slice(128 * gi, 128 * gi + 128)
            mix_ref[:, sl] = _dot(wm_ref[gi], vnb[:, sl])
        mixed = mix_ref[...] + bf_ref[...]
        sig = _sigmoid(za)
        sz = za * sig
        dm = dm_ref[...].astype(F32)
        y_a = ya_ref[...].astype(F32)
        g0 = _sigmoid(pa_ref[:, 6144:8192].astype(F32) + gb_ref[:, 0:2048])
        g1 = _sigmoid(pa_ref[:, 8192:10240].astype(F32) + gb_ref[:, 2048:4096])
        dgl0 = dm * y_a * g0 * (1.0 - g0)
        dgl1 = dm * yb_ref[...].astype(F32) * g1 * (1.0 - g1)
        dyb_ref[...] = _c(dm * g1)
        dya = dm * g0
        dpa_ref[:, 6144:8192] = _c(dgl0)
        dpa_ref[:, 8192:10240] = _c(dgl1)
        dgb_ref[:, 0:2048] += _fold8(dgl0)
        dgb_ref[:, 2048:4096] += _fold8(dgl1)
        dpa_ref[:, 0:2048] = _c(dya * mixed * sz)
        dpa_ref[:, 4096:6144] = _c(dya * (u * mixed) * (sig * (1.0 + za * (1.0 - sig))))
        dmixed = dya * u * sz
        dbf_ref[...] += dmixed
        dmb = _c(dmixed)
        for gi in range(SGU_GROUPS):
            sl = slice(128 * gi, 128 * gi + 128)
            dvn_ref[:, sl] = _dot(wmt_ref[gi], dmb[:, sl])
            dws_ref[gi] += _dot_nt(dmb[:, sl], vnb[:, sl])
        dvn = dvn_ref[...]
        dgam_ref[...] += _fold8(dvn * xhat)
        dbeta_ref[...] += _fold8(dvn)
        dxh = dvn * g_ref[...]
        dv = rstd * (dxh - jnp.mean(dxh, axis=-1, keepdims=True) - xhat * jnp.mean(dxh * xhat, axis=-1, keepdims=True))
        dpa_ref[:, 2048:4096] = _c(dv)

    row = pl.BlockSpec((t, D_MODEL), lambda i: (i, 0))
    vec = lambda w: pl.BlockSpec((1, w), lambda i: (0, 0))
    acc = lambda w: pl.BlockSpec((8, w), lambda i: (0, 0))
    wspec = pl.BlockSpec((SGU_GROUPS, 128, 128), lambda i: (0, 0, 0))
    return pl.pallas_call(
        body, name="tok_bwd",
        grid=(s // t,),
        in_specs=[pl.BlockSpec((t, NA), lambda i: (i, 0)), row, row, row, vec(4096), vec(2048), vec(2048),
                  wspec, wspec, pl.BlockSpec((128, D_MODEL), lambda i: (0, 0))],
        out_specs=[pl.BlockSpec((t, NA), lambda i: (i, 0)), row, acc(4096), acc(2048), acc(2048),
                   pl.BlockSpec((128, D_MODEL), lambda i: (0, 0)), wspec],
        out_shape=[jax.ShapeDtypeStruct((s, NA), MXU_DTYPE), jax.ShapeDtypeStruct((s, D_MODEL), MXU_DTYPE),
                   jax.ShapeDtypeStruct((8, 4096), F32), jax.ShapeDtypeStruct((8, 2048), F32),
                   jax.ShapeDtypeStruct((8, 2048), F32), jax.ShapeDtypeStruct((128, D_MODEL), F32),
                   jax.ShapeDtypeStruct((SGU_GROUPS, 128, 128), F32)],
        scratch_shapes=[pltpu.VMEM((t, D_MODEL), F32), pltpu.VMEM((t, D_MODEL), F32)],
        compiler_params=_cparams(("arbitrary",), 2 * t * NA * 6 + 16 * t * D_MODEL * 4),
    )(proj_a, dmerged, y_a, y_b, gate_b, sgu_g, sgu_beta, wm, wmt, bias_full)


def _out_call(merged, x, target, w_out, fnw, t):
    s = x.shape[0]
    nt = s // t

    def body(mg_ref, x_ref, t_ref, w_ref, fw_ref, dhb_ref, dmg_ref, loss_ref, dfw_ref):
        @pl.when(pl.program_id(0) == 0)
        def _():
            dfw_ref[...] = jnp.zeros_like(dfw_ref)

        h = x_ref[...] + _dot(mg_ref[...], w_ref[...])
        r = lax.rsqrt(jnp.mean(h * h, axis=-1, keepdims=True) + EPS)
        hn = h * r
        err = hn * fw_ref[...] - t_ref[...]
        loss_ref[...] = jnp.full(loss_ref.shape, 0.5 * jnp.sum(jnp.mean(err * err, axis=-1, keepdims=True)), F32)
        dy = err * (1.0 / D_MODEL)
        dfw_ref[...] += _fold8(dy * hn)
        gw = dy * fw_ref[...]
        dh = r * gw - h * (r * r * r) * jnp.mean(h * gw, axis=-1, keepdims=True)
        dhb = _c(dh)
        dhb_ref[...] = dhb
        dmg_ref[...] = _c(_dot_nt(dhb, w_ref[...]))

    row = pl.BlockSpec((t, D_MODEL), lambda i: (i, 0))
    return pl.pallas_call(
        body, name="out_proj_loss",
        grid=(nt,),
        in_specs=[row, row, row, pl.BlockSpec((D_MODEL, D_MODEL), lambda i: (0, 0)), pl.BlockSpec((1, D_MODEL), lambda i: (0, 0))],
        out_specs=[row, row, pl.BlockSpec((1, 8, 128), lambda i: (i, 0, 0)), pl.BlockSpec((8, D_MODEL), lambda i: (0, 0))],
        out_shape=[jax.ShapeDtypeStruct((s, D_MODEL), MXU_DTYPE),
                   jax.ShapeDtypeStruct((s, D_MODEL), MXU_DTYPE), jax.ShapeDtypeStruct((nt, 8, 128), F32),
                   jax.ShapeDtypeStruct((8, D_MODEL), F32)],
        compiler_params=_cparams(("arbitrary",), 2 * D_MODEL * D_MODEL * 2 + 2 * t * D_MODEL * 20),
    )(merged, x, target, w_out, fnw)


def _ssd_fwd_call(proj_b, dtb, alog, dsk, cw, cb, nw, t, ng):
    s = proj_b.shape[0]
    nt, nch = s // t, t // CHUNK

    def body(pb_ref, halo_ref, dtb_ref, al_ref, ds_ref, cw_ref, cb_ref, nw_ref, y_ref, yb_ref, hp_ref, pre_ref,
             dt_ref, acs_ref, ht_ref, prev_ref):
        i = pl.program_id(1)

        @pl.when(i == 0)
        def _():
            ht_ref[...] = jnp.zeros_like(ht_ref)

        for gi in range(ng):
            prev_ref[:, 768 * gi:768 * gi + 768] = jnp.where(i == 0, 0.0, halo_ref[:, FW_B * gi + XBC_O:FW_B * gi + DT_O])
        masks = _ssd_masks()
        tri, _ = _cumsum_mats(t)
        a_neg = -jnp.exp(al_ref[...])
        expand = _head_expand_mat()
        for gi in range(ng):
            fo, go, no = FW_B * gi, SSD_GW * gi, 128 * gi
            dt_n = _softplus(pb_ref[:, fo + DT_O:fo + DT_O + 128] + dtb_ref[:, no:no + 128])
            acs_n = _dot01_l(tri, dt_n * a_neg[:, no:no + 128], 3)
            dt_ref[:, go:go + 512] = _dot01_r(dt_n, expand, 2)
            acs_ref[:, go:go + 512] = _dot01_r(acs_n, expand, 2)

        def chunk(c, carry):
            rows = pl.ds(pl.multiple_of(c * CHUNK, CHUNK), CHUNK)
            for gi in range(ng):
                fo, co, go, no = FW_B * gi, 768 * gi, SSD_GW * gi, 128 * gi
                xbc = pb_ref[rows, fo + XBC_O:fo + DT_O]
                taps = _conv_taps(jnp.concatenate([prev_ref[:, co:co + 768], xbc], axis=0), CHUNK)
                prev_ref[:, co:co + 768] = xbc[CHUNK - 8:CHUNK]
                pre = cb_ref[:, co:co + 768]
                for k in range(4):
                    pre = pre + taps[k] * cw_ref[k:k + 1, co:co + 768]
                pre_ref[rows, co:co + 768] = pre
                act = pre * _sigmoid(pre)
                dt = dt_ref[rows, go:go + 512]
                acs = acs_ref[rows, go:go + 512]
                ht = ht_ref[gi]
                hp_ref[c, :, go:go + 512] = ht
                y, ht_new = _ssd_chunk_fwd(act[:, 0:512], act[:, 512:640], act[:, 640:768], dt, acs,
                                           ds_ref[:, go:go + 512], ht, masks)
                y_ref[rows, go:go + 512] = y
                ht_ref[gi] = ht_new
                zb = pb_ref[rows, fo:fo + 512]
                hh = y * (zb * _sigmoid(zb))
                rr = lax.rsqrt(jnp.mean(hh * hh, axis=-1, keepdims=True) + EPS)
                yb_ref[rows, go:go + 512] = _c(hh * rr * nw_ref[:, go:go + 512])
            return carry

        lax.fori_loop(0, nch, chunk, 0)

    gvec = lambda w: pl.BlockSpec((1, ng * w), lambda g, i: (0, g))
    return pl.pallas_call(
        body, name="ssd_fwd",
        grid=(SSD_GROUPS // ng, nt),
        in_specs=[pl.BlockSpec((t, ng * FW_B), lambda g, i: (i, g)),
                  pl.BlockSpec((8, ng * FW_B), lambda g, i: (jnp.maximum(i * (t // 8) - 1, 0), g)),
                  gvec(128), gvec(128), gvec(512),
                  pl.BlockSpec((4, ng * 768), lambda g, i: (0, g)), gvec(768), gvec(512)],
        out_specs=[pl.BlockSpec((t, ng * SSD_GW), lambda g, i: (i, g)), pl.BlockSpec((t, ng * SSD_GW), lambda g, i: (i, g)),
                   pl.BlockSpec((nch, SSD_STATE, ng * SSD_GW), lambda g, i: (i, 0, g)),
                   pl.BlockSpec((t, ng * 768), lambda g, i: (i, g)),
                   pl.BlockSpec((t, ng * SSD_GW), lambda g, i: (i, g)), pl.BlockSpec((t, ng * SSD_GW), lambda g, i: (i, g))],
        out_shape=[jax.ShapeDtypeStruct((s, D_MODEL), F32), jax.ShapeDtypeStruct((s, D_MODEL), MXU_DTYPE),
                   jax.ShapeDtypeStruct((s // CHUNK, SSD_STATE, D_MODEL), F32),
                   jax.ShapeDtypeStruct((s, SSD_GROUPS * 768), F32),
                   jax.ShapeDtypeStruct((s, D_MODEL), F32), jax.ShapeDtypeStruct((s, D_MODEL), F32)],
        scratch_shapes=[pltpu.VMEM((ng, SSD_STATE, SSD_GW), F32), pltpu.VMEM((8, ng * 768), F32)],
        compiler_params=_cparams(("parallel", "arbitrary"), ng * (2 * t * FW_B * 4 + 16 * t * SSD_GW * 4) + 16 * 1024 * 1024),
    )(proj_b, proj_b, dtb, alog, dsk, cw, cb, nw)


def _ssd_bwd_call(proj_b, pre_all, dt_all, acs_all, dyb, y, hprev, dtb, alog, dsk, cw, nw, t, ng):
    s = proj_b.shape[0]
    nt, nch = s // t, t // CHUNK

    def body(pb_ref, pre_ref, dt_ref, acs_ref, dyb_ref, y_ref, hp_ref, dtb_ref, al_ref, ds_ref, cw_ref, nw_ref,
             dpb_ref, a512_ref, a768_ref, dht_ref, nxt_ref, q_ref, p1_ref):
        i = pl.program_id(1)

        @pl.when(i == 0)
        def _():
            dht_ref[...] = jnp.zeros_like(dht_ref)
            nxt_ref[...] = jnp.zeros_like(nxt_ref)
            a512_ref[...] = jnp.zeros_like(a512_ref)
            a768_ref[...] = jnp.zeros_like(a768_ref)

        _, trit = _cumsum_mats(t)
        a_neg = -jnp.exp(al_ref[...])
        masks = _ssd_masks()

        def chunk(cc, carry):
            c = nch - 1 - cc
            rows = pl.ds(pl.multiple_of(c * CHUNK, CHUNK), CHUNK)
            for gi in range(ng):
                fo, co, go, bo = FW_B * gi, 768 * gi, SSD_GW * gi, BW_B * gi
                pre = pre_ref[rows, co:co + 768]
                sp = _sigmoid(pre)
                act = pre * sp
                zb = pb_ref[rows, fo:fo + 512]
                yv = y_ref[rows, go:go + 512]
                sgz = _sigmoid(zb)
                sz = zb * sgz
                hh = yv * sz
                rr = lax.rsqrt(jnp.mean(hh * hh, axis=-1, keepdims=True) + EPS)
                dyb = dyb_ref[rows, go:go + 512].astype(F32)
                a512_ref[gi, 0] += _fold8(dyb * (hh * rr))
                tt = dyb * nw_ref[:, go:go + 512]
                dhh = rr * tt - hh * (rr * rr * rr) * jnp.mean(hh * tt, axis=-1, keepdims=True)
                dpb_ref[rows, bo:bo + 512] = _c(dhh * yv * (sgz * (1.0 + zb * (1.0 - sgz))))
                dxs, d_b, d_c, dht_prev, dyxs, qq, p1 = _ssd_chunk_bwd(
                    act[:, 0:512], act[:, 512:640], act[:, 640:768], dt_ref[rows, go:go + 512], acs_ref[rows, go:go + 512],
                    ds_ref[:, go:go + 512], hp_ref[c, :, go:go + 512], dht_ref[gi], dhh * sz, masks)
                dht_ref[gi] = dht_prev
                q_ref[rows, go:go + 512] = qq
                p1_ref[rows, go:go + 512] = p1
                a512_ref[gi, 1] += _fold8(dyxs)
                dpre = jnp.concatenate([dxs, d_b, d_c], axis=1) * (sp * (1.0 + pre * (1.0 - sp)))
                xbc = pb_ref[rows, fo + XBC_O:fo + DT_O]
                a768_ref[gi, 4] += _fold8(dpre)
                a768_ref[gi, 3] += _fold8(dpre * xbc)
                dpad = jnp.concatenate([dpre, nxt_ref[:, co:co + 768]], axis=0)
                dx = dpre * cw_ref[3:4, co:co + 768]
                for k in range(3):
                    d_k = pltpu.roll(dpad, CHUNK + 8 - (3 - k), 0)[0:CHUNK]
                    dx = dx + d_k * cw_ref[k:k + 1, co:co + 768]
                    a768_ref[gi, k] += _fold8(d_k * xbc)
                nxt_ref[:, co:co + 768] = dpre[0:8]
                dpb_ref[rows, bo + 512:bo + 1280] = _c(dx)
            return carry

        lax.fori_loop(0, nch, chunk, 0)

        rsel = _c(jnp.where(lax.shift_right_logical(_iota((SSD_GW, 128), 0), 6) == _iota((SSD_GW, 128), 1), 1.0, 0.0))
        for gi in range(ng):
            fo, go, bo, no = FW_B * gi, SSD_GW * gi, BW_B * gi, 128 * gi
            ddt, dadt = _ssd_finish_dt(q_ref[:, go:go + 512], p1_ref[:, go:go + 512], dt_ref[:, go:go + 512],
                                       a_neg[:, go:go + 512], trit, masks[3])
            sig_n = _sigmoid(pb_ref[:, fo + DT_O:fo + DT_O + 128] + dtb_ref[:, no:no + 128])
            ddtr_n = _dot01_r(ddt, rsel, 2) * sig_n
            dpb_ref[:, bo + DT_O:bo + DT_O + 128] = _c(ddtr_n)
            a512_ref[gi, 2] += _fold8(dadt)
            a512_ref[gi, 3, :, 0:128] += _fold8(ddtr_n)

    gvec = lambda w: pl.BlockSpec((1, ng * w), lambda g, i: (0, g))
    rev = lambda w: pl.BlockSpec((t, ng * w), lambda g, i: (nt - 1 - i, g))
    return pl.pallas_call(
        body, name="ssd_bwd",
        grid=(SSD_GROUPS // ng, nt),
        in_specs=[rev(FW_B), rev(768), rev(SSD_GW), rev(SSD_GW), rev(SSD_GW), rev(SSD_GW),
                  pl.BlockSpec((nch, SSD_STATE, ng * SSD_GW), lambda g, i: (nt - 1 - i, 0, g)),
                  gvec(128), gvec(512), gvec(512),
                  pl.BlockSpec((4, ng * 768), lambda g, i: (0, g)), gvec(512)],
        out_specs=[rev(BW_B),
                   pl.BlockSpec((ng, 4, 8, 512), lambda g, i: (g, 0, 0, 0)),
                   pl.BlockSpec((ng, 5, 8, 768), lambda g, i: (g, 0, 0, 0))],
        out_shape=[jax.ShapeDtypeStruct((s, SSD_GROUPS * BW_B), MXU_DTYPE),
                   jax.ShapeDtypeStruct((SSD_GROUPS, 4, 8, 512), F32),
                   jax.ShapeDtypeStruct((SSD_GROUPS, 5, 8, 768), F32)],
        scratch_shapes=[pltpu.VMEM((ng, SSD_STATE, SSD_GW), F32), pltpu.VMEM((8, ng * 768), F32),
                        pltpu.VMEM((t, ng * SSD_GW), F32), pltpu.VMEM((t, ng * SSD_GW), F32)],
        compiler_params=_cparams(("parallel", "arbitrary"), ng * (2 * t * FW_B * 4 + 18 * t * SSD_GW * 4) + 16 * 1024 * 1024),
    )(proj_b, pre_all, dt_all, acs_all, dyb, y, hprev, dtb, alog, dsk, cw, nw)


def _rows_call(body, ins, outs, tr, name):
    r = ins[0].shape[0]
    spec = lambda a: pl.BlockSpec((tr, a.shape[1]), lambda i: (i, 0))
    est = 2 * tr * sum(a.shape[1] * jnp.dtype(a.dtype).itemsize for a in list(ins) + list(outs))
    return pl.pallas_call(
        body, name=name, grid=(r // tr,),
        in_specs=[spec(a) for a in ins], out_specs=[spec(o) for o in outs], out_shape=list(outs),
        compiler_params=_cparams(("parallel",), est),
    )(*ins)


def _add_pair(a, b, tr, name):
    def body(a_ref, b_ref, o_ref):
        o_ref[...] = a_ref[...] + b_ref[...]

    return _rows_call(body, [a, b], [jax.ShapeDtypeStruct(a.shape, F32)], tr, name)[0]


def _rs_add(p, sib, place, tr, name):
    _, r, c = p.shape
    half = r // 2
    nb = half // tr

    def body(pl_ref, p_ref, s_ref, b_ref, own_ref):
        v = p_ref[0] + s_ref[0]
        b_ref[0] = v.astype(jnp.bfloat16)

        @pl.when(pl.program_id(1) == pl_ref[0])
        def _():
            own_ref[...] = v

    return pl.pallas_call(
        body, name=name,
        grid_spec=pltpu.PrefetchScalarGridSpec(
            num_scalar_prefetch=1, grid=(nb, 4),
            in_specs=[pl.BlockSpec((1, tr, c), lambda i, k, pr: (k, pr[1] * nb + i, 0)),
                      pl.BlockSpec((1, tr, c), lambda i, k, pr: (k, i, 0))],
            out_specs=[pl.BlockSpec((1, tr, c), lambda i, k, pr: (k, i, 0)),
                       pl.BlockSpec((tr, c), lambda i, k, pr: (i, 0))]),
        out_shape=[jax.ShapeDtypeStruct((4, half, c), jnp.bfloat16), jax.ShapeDtypeStruct((half, c), F32)],
        compiler_params=_cparams(("parallel", "arbitrary"), 2 * tr * c * 14),
    )(place, p, sib)


WIN_STEP = 3840
WIN_W = 3968


def _rs_add_windows(dw, sib, place, tr, name):
    r = dw.shape[0]
    half = r // 2
    nb = half // tr
    tail = WIN_W - WIN_STEP

    def body(pl_ref, pm_ref, pt_ref, s_ref, b_ref, own_ref):
        vm = pm_ref[...].astype(F32) + s_ref[0, :, 0:WIN_STEP].astype(F32)
        vt = pt_ref[...].astype(F32) + s_ref[0, :, WIN_STEP:WIN_W].astype(F32)
        b_ref[0, :, 0:WIN_STEP] = vm.astype(jnp.bfloat16)
        b_ref[0, :, WIN_STEP:WIN_W] = vt.astype(jnp.bfloat16)

        @pl.when(pl.program_id(1) == pl_ref[0])
        def _():
            own_ref[:, 0:WIN_STEP] = vm
            own_ref[:, WIN_STEP:WIN_W] = vt

    return pl.pallas_call(
        body, name=name,
        grid_spec=pltpu.PrefetchScalarGridSpec(
            num_scalar_prefetch=1, grid=(nb, 4),
            in_specs=[pl.BlockSpec((tr, WIN_STEP), lambda i, k, pr: (pr[1] * nb + i, k)),
                      pl.BlockSpec((tr, tail), lambda i, k, pr: (pr[1] * nb + i, (WIN_STEP // tail) * (k + 1))),
                      pl.BlockSpec((1, tr, WIN_W), lambda i, k, pr: (k, i, 0))],
            out_specs=[pl.BlockSpec((1, tr, WIN_W), lambda i, k, pr: (k, i, 0)),
                       pl.BlockSpec((tr, WIN_W), lambda i, k, pr: (i, 0))]),
        out_shape=[jax.ShapeDtypeStruct((4, half, WIN_W), jnp.bfloat16), jax.ShapeDtypeStruct((half, WIN_W), F32)],
        compiler_params=_cparams(("parallel", "arbitrary"), 2 * tr * WIN_W * 14),
    )(place, dw, dw, sib)


def _sum_own_recv(own, recv, tr, name):
    r, c = own.shape

    def body(o_ref, r_ref, out_ref):
        v = o_ref[...]
        for j in range(3):
            v = v + r_ref[j].astype(F32)
        out_ref[...] = v

    return pl.pallas_call(
        body, name=name, grid=(r // tr,),
        in_specs=[pl.BlockSpec((tr, c), lambda i: (i, 0)), pl.BlockSpec((3, tr, c), lambda i: (0, i, 0))],
        out_specs=pl.BlockSpec((tr, c), lambda i: (i, 0)),
        out_shape=jax.ShapeDtypeStruct((r, c), F32),
        compiler_params=_cparams(("parallel",), 2 * tr * c * 14),
    )(own, recv)


def _sum_slots(stack, name):
    n, r, w = stack.shape

    def body(a_ref, out_ref):
        v = a_ref[0]
        for k in range(1, n):
            v = v + a_ref[k]
        out_ref[...] = v

    return pl.pallas_call(
        body, name=name, grid=(1,),
        in_specs=[pl.BlockSpec((n, r, w), lambda i: (0, 0, 0))],
        out_specs=pl.BlockSpec((r, w), lambda i: (0, 0)),
        out_shape=jax.ShapeDtypeStruct((r, w), F32),
        compiler_params=_cparams(("arbitrary",), 2 * (n + 1) * r * w * 4),
    )(stack)


def _adamw(w, g, m, v, tr, name):
    def body(w_ref, g_ref, m_ref, v_ref, d_ref, nm_ref, nv_ref):
        d_ref[...], nm_ref[...], nv_ref[...] = _adam_math(w_ref[...], g_ref[...], m_ref[...], v_ref[...])

    o = jax.ShapeDtypeStruct(w.shape, F32)
    return _rows_call(body, [w, g, m, v], [o, o, o], tr, name)


def _adam_math(w, g, m, v):
    nm = ADAM_B1 * m + (1.0 - ADAM_B1) * g
    nv = ADAM_B2 * v + (1.0 - ADAM_B2) * (g * g)
    m_hat = nm / (1.0 - ADAM_B1 ** ADAM_STEP)
    v_hat = nv / (1.0 - ADAM_B2 ** ADAM_STEP)
    return -ADAM_LR * (m_hat / (jnp.sqrt(v_hat) + ADAM_EPS) + ADAM_WD * w), nm, nv


def _adamw_halves(w, g_own, g_sib, m, v, place, tr, name):
    r, c = w.shape

    def body(pl_ref, w_ref, go_ref, gs_ref, m_ref, v_ref, g_ref, d_ref, nm_ref, nv_ref):
        first = pl_ref[1] == 0
        own, sib = go_ref[...], gs_ref[...]
        g = jnp.concatenate([jnp.where(first, own, sib), jnp.where(first, sib, own)], axis=1)
        g_ref[...] = g
        d_ref[...], nm_ref[...], nv_ref[...] = _adam_math(w_ref[...], g, m_ref[...], v_ref[...])

    full = pl.BlockSpec((tr, c), lambda i, pr: (i, 0))
    half = pl.BlockSpec((tr, c // 2), lambda i, pr: (i, 0))
    o = jax.ShapeDtypeStruct((r, c), F32)
    return pl.pallas_call(
        body, name=name,
        grid_spec=pltpu.PrefetchScalarGridSpec(num_scalar_prefetch=1, grid=(r // tr,),
                                               in_specs=[full, half, half, full, full], out_specs=[full] * 4),
        out_shape=[o] * 4,
        compiler_params=_cparams(("parallel",), 2 * tr * c * 4 * 8),
    )(place, w, g_own, g_sib, m, v)


ANY = pl.BlockSpec(memory_space=pl.ANY)


def _place():
    x, y, c = lax.axis_index("x"), lax.axis_index("y"), lax.axis_index("c")
    others = [(1 - x, y), (x, 1 - y), (1 - x, 1 - y)]
    return x, y, c, 2 * x + y, others


def _remote(src, dst, send, recv, k, to):
    return pltpu.make_async_remote_copy(src_ref=src, dst_ref=dst, send_sem=send.at[k], recv_sem=recv.at[k],
                                        device_id=to, device_id_type=MESH)


def _norm_gather_call(x, norm_w, win_b, wout_b, cw8, tm):
    s = x.shape[0]
    ni = s // tm
    h_in, h_out = win_b.shape[0] // 2, wout_b.shape[0] // 2
    q_in = h_in // 2

    def body(x_ref, w_ref, win, wout, cw, xn_ref, xnt_ref, g_in, g_out, g_cw, send, recv):
        i = pl.program_id(0)

        def direct():
            xx, yy, c, me, others = _place()
            mi, mo = pl.ds(c * h_in, h_in), pl.ds(c * h_out, h_out)
            cps = [_remote(win.at[mi], g_in.at[me, mi], send, recv, j, (*others[j], c)) for j in range(2)]
            cps += [_remote(wout.at[mo], g_out.at[me, mo], send, recv, 7 + j, (*chip, c)) for j, chip in enumerate(others)]
            cps += [_remote(cw, g_cw.at[me], send, recv, 13 + j, (*chip, c)) for j, chip in enumerate(others)]
            return cps

        @pl.when(i == 0)
        def _():
            for cp in direct():
                cp.start()

        xv = x_ref[...]
        r = lax.rsqrt(jnp.mean(xv * xv, axis=-1, keepdims=True) + EPS)
        xn = xv * r * w_ref[...]
        xn_ref[...] = _c(xn)
        xnt_ref[...] = _c(xn.T)

        @pl.when(i == ni - 1)
        def _():
            xx, yy, c, me, others = _place()
            sib = (xx, yy, 1 - c)
            kx, ky, kd = (2 * chip[0] + chip[1] for chip in others)
            mi, ti = pl.ds(c * h_in, h_in), pl.ds((1 - c) * h_in, h_in)
            quarter = [pl.ds(c * h_in, q_in), pl.ds(c * h_in + q_in, q_in)]
            started = []

            def go(cp):
                cp.start()
                started.append(cp)

            _remote(g_in.at[kx, mi], g_in.at[kx, mi], send, recv, 0, (*others[0], c)).wait_recv()
            go(_remote(g_in.at[kx, quarter[0]], g_in.at[kx, quarter[0]], send, recv, 2, (*others[1], c)))
            go(_remote(g_in.at[kx, mi], g_in.at[kx, mi], send, recv, 4, sib))
            _remote(g_in.at[ky, mi], g_in.at[ky, mi], send, recv, 1, (*others[1], c)).wait_recv()
            go(_remote(g_in.at[ky, quarter[1]], g_in.at[ky, quarter[1]], send, recv, 3, (*others[0], c)))
            go(_remote(g_in.at[ky, mi], g_in.at[ky, mi], send, recv, 5, sib))
            mo, to = pl.ds(c * h_out, h_out), pl.ds((1 - c) * h_out, h_out)
            for j, chip in enumerate(others):
                kj = 2 * chip[0] + chip[1]
                _remote(g_out.at[kj, mo], g_out.at[kj, mo], send, recv, 7 + j, (*chip, c)).wait_recv()
                go(_remote(g_out.at[kj, mo], g_out.at[kj, mo], send, recv, 10 + j, sib))
            _remote(g_in.at[kd, quarter[0]], g_in.at[kd, quarter[0]], send, recv, 2, (*others[1], c)).wait_recv()
            _remote(g_in.at[kd, quarter[1]], g_in.at[kd, quarter[1]], send, recv, 3, (*others[0], c)).wait_recv()
            go(_remote(g_in.at[kd, mi], g_in.at[kd, mi], send, recv, 6, sib))
            for k_src, sem in ((kx, 4), (ky, 5), (kd, 6)):
                _remote(g_in.at[k_src, ti], g_in.at[k_src, ti], send, recv, sem, sib).wait_recv()
            for j, chip in enumerate(others):
                kj = 2 * chip[0] + chip[1]
                _remote(g_out.at[kj, to], g_out.at[kj, to], send, recv, 10 + j, sib).wait_recv()
                _remote(cw, g_cw.at[kj], send, recv, 13 + j, (*chip, c)).wait_recv()
            for cp in direct() + started:
                cp.wait_send()

    outs = [jax.ShapeDtypeStruct((s, D_MODEL), MXU_DTYPE), jax.ShapeDtypeStruct((D_MODEL, s), MXU_DTYPE)]
    outs += [jax.ShapeDtypeStruct((4,) + a.shape, a.dtype) for a in (win_b, wout_b, cw8)]
    return pl.pallas_call(
        body, name="rmsnorm_gather_weights",
        grid=(ni,),
        in_specs=[pl.BlockSpec((tm, D_MODEL), lambda i: (i, 0)), pl.BlockSpec((1, D_MODEL), lambda i: (0, 0)), ANY, ANY, ANY],
        out_specs=[pl.BlockSpec((tm, D_MODEL), lambda i: (i, 0)), pl.BlockSpec((D_MODEL, tm), lambda i: (0, i)), ANY, ANY, ANY],
        out_shape=outs,
        scratch_shapes=[pltpu.SemaphoreType.DMA((16,)), pltpu.SemaphoreType.DMA((16,))],
        compiler_params=_cparams(("arbitrary",), 2 * tm * D_MODEL * 12),
    )(x, norm_w, win_b, wout_b, cw8)


def _dw_out_rs_call(a, b, dw, *, tn, tk):
    m, k = a.shape
    n = b.shape[1]
    nj, nk = n // tn, k // tk
    half = dw.shape[0] // 2

    def body(a_ref, b_ref, pin, o_ref, sib_in, send, recv):
        j, kk = pl.program_id(0), pl.program_id(1)

        def copies():
            x, y, c, me, others = _place()
            rows = pl.ds((1 - c) * half, half)
            return [_remote(pin.at[rows, pl.ds(WIN_STEP * w, WIN_W)], sib_in.at[w], send, recv, w, (x, y, 1 - c)) for w in range(4)]

        @pl.when((j == 0) & (kk == 0))
        def _():
            for cp in copies():
                cp.start()

        @pl.when(kk == 0)
        def _():
            o_ref[...] = jnp.zeros_like(o_ref)

        o_ref[...] += _dot(a_ref[...], b_ref[...])

        @pl.when((j == nj - 1) & (kk == nk - 1))
        def _():
            cps = copies()
            for cp in cps:
                cp.wait_recv()
            for cp in cps:
                cp.wait_send()

    isz = jnp.dtype(a.dtype).itemsize
    est = 2 * (m * tk + tk * tn) * isz + 2 * m * tn * 4
    outs = [jax.ShapeDtypeStruct((m, n), F32), jax.ShapeDtypeStruct((4, half, WIN_W), dw.dtype)]
    return pl.pallas_call(
        body, name="dw_out_rs_sibling",
        grid=(nj, nk),
        in_specs=[pl.BlockSpec((m, tk), lambda j, kk: (0, kk)), pl.BlockSpec((tk, tn), lambda j, kk: (kk, j)), ANY],
        out_specs=[pl.BlockSpec((m, tn), lambda j, kk: (0, j)), ANY],
        out_shape=outs,
        scratch_shapes=[pltpu.SemaphoreType.DMA((4,)), pltpu.SemaphoreType.DMA((4,))],
        compiler_params=_cparams(("arbitrary", "arbitrary"), est),
    )(a, b, dw)


def _rs_sibling_call(p_out, vsmall):
    def body(pout, vs, sib_out, sib_v, send, recv):
        x, y, c, me, others = _place()
        sib = (x, y, 1 - c)
        half = pout.shape[1] // 2
        cps = [_remote(pout.at[:, pl.ds((1 - c) * half, half)], sib_out, send, recv, 0, sib),
               _remote(vs, sib_v, send, recv, 1, sib)]
        for cp in cps:
            cp.start()
        for cp in cps:
            cp.wait_recv()
        for cp in cps:
            cp.wait_send()

    outs = [jax.ShapeDtypeStruct((4, p_out.shape[1] // 2, p_out.shape[2]), p_out.dtype),
            jax.ShapeDtypeStruct(vsmall.shape, vsmall.dtype)]
    return pl.pallas_call(
        body, name="rs_sibling",
        in_specs=[ANY] * 2, out_specs=[ANY] * 2, out_shape=outs,
        scratch_shapes=[pltpu.SemaphoreType.DMA((2,)), pltpu.SemaphoreType.DMA((2,))],
    )(p_out, vsmall)


def _rs_join_call(f_in, f_out, nw8):
    def body(fin, fout, nw, sib_in, full_out, all_nw, send, recv):
        x, y, c, me, others = _place()
        sib = (x, y, 1 - c)
        half = fout.shape[0]
        cps = [_remote(fin, sib_in, send, recv, 0, sib),
               _remote(fout, full_out.at[pl.ds(c * half, half)], send, recv, 1, sib)]
        mine = 4 * x + 2 * y + c
        peers = []
        for r in range(1, 8):
            px, py, pc = (1 - x if r & 4 else x), (1 - y if r & 2 else y), (1 - c if r & 1 else c)
            peers.append((r, (px, py, pc), 4 * px + 2 * py + pc))
            cps.append(_remote(nw, all_nw.at[mine], send, recv, 1 + r, (px, py, pc)))
        for cp in cps:
            cp.start()
        cps[0].wait_recv()
        _remote(fout, full_out.at[pl.ds((1 - c) * half, half)], send, recv, 1, sib).wait_recv()
        for r, peer, idx in peers:
            _remote(nw, all_nw.at[idx], send, recv, 1 + r, peer).wait_recv()
        for cp in cps:
            cp.wait_send()

    outs = [jax.ShapeDtypeStruct(f_in.shape, F32), jax.ShapeDtypeStruct((2 * f_out.shape[0], f_out.shape[1]), F32),
            jax.ShapeDtypeStruct((8,) + nw8.shape, F32)]
    return pl.pallas_call(
        body, name="rs_join",
        in_specs=[ANY] * 3, out_specs=[ANY] * 3, out_shape=outs,
        scratch_shapes=[pltpu.SemaphoreType.DMA((9,)), pltpu.SemaphoreType.DMA((9,))],
    )(f_in, f_out, nw8)


def _pack(arrs):
    parts = []
    for a in arrs:
        f = a.reshape(-1).astype(F32)
        pad = (-f.shape[0]) % 1024
        parts.append(jnp.pad(f, (0, pad)).reshape(-1, 128))
    return jnp.concatenate(parts, axis=0)


def _unpack(packed, shapes):
    out, row = [], 0
    for shp in shapes:
        n = 1
        for d in shp:
            n *= d
        rows = (n + 1023) // 1024 * 8
        out.append(packed[row:row + rows].reshape(-1)[:n].reshape(shp))
        row += rows
    return out


def _expand_heads(v32):
    return jnp.repeat(v32.reshape(32), HEADDIM).reshape(1, D_MODEL)


def kernel(x, norm_w, w_in, gate_b, sgu_norm_g, sgu_norm_b, sgu_w, sgu_b, conv_w, conv_b, dt_bias, A_log, D_skip, ssd_norm_w, w_out, final_norm_w, loss_target, m_norm_w, m_w_in, m_gate_b, m_sgu_norm_g, m_sgu_norm_b, m_sgu_w, m_sgu_b, m_conv_w, m_conv_b, m_dt_bias, m_A_log, m_D_skip, m_ssd_norm_w, m_w_out, m_final_norm_w, v_norm_w, v_w_in, v_gate_b, v_sgu_norm_g, v_sgu_norm_b, v_sgu_w, v_sgu_b, v_conv_w, v_conv_b, v_dt_bias, v_A_log, v_D_skip, v_ssd_norm_w, v_w_out, v_final_norm_w):
    s = x.shape[1]
    x2 = x.reshape(s, D_MODEL)
    tgt = loss_target.reshape(s, D_MODEL)
    t_ssd, t_tok, t_out, t_row = min(T_SSD, s), min(T_TOK, s), min(T_OUT, s), min(T_ROW, s)
    tm_mm, tk_dw = min(TM_MM, s), min(TK_DW, s)
    chip = 2 * lax.axis_index("x") + lax.axis_index("y")

    cw8 = jnp.pad(conv_w[0], ((0, 4), (0, 0)))
    win_b, wout_b = _c(w_in[0]), _c(w_out[0])
    xn, xnt, g_in, g_out, g_cw = _norm_gather_call(x2, norm_w, win_b, wout_b, cw8, t_row)
    g_in = lax.dynamic_update_index_in_dim(g_in, win_b, chip, 0)
    g_out = lax.dynamic_update_index_in_dim(g_out, wout_b, chip, 0)
    g_cw = lax.dynamic_update_index_in_dim(g_cw, cw8, chip, 0)
    wt = jnp.transpose(g_in, (0, 2, 1)).reshape(IN_W, D_MODEL)
    w_out_full = g_out.reshape(D_MODEL, D_MODEL)
    conv_w_full = jnp.transpose(g_cw[:, 0:4, :], (1, 0, 2)).reshape(4, 3072)

    wt_a = jnp.concatenate([wt[0:6144], wt[11296:15392]], axis=0)
    per_group = lambda lo, n: wt[lo:lo + SSD_GROUPS * n].reshape(SSD_GROUPS, n, D_MODEL)
    wt_b = jnp.concatenate([per_group(6144, 512), per_group(8192, 512), per_group(10240, 128), per_group(10752, 128),
                            jnp.pad(per_group(11264, 8), ((0, 0), (0, 120), (0, 0)))], axis=1)
    wt_b = wt_b.reshape(SSD_GROUPS * BW_B, D_MODEL)

    def group_cols(full_xs, full_bc):
        parts = []
        for g in range(SSD_GROUPS):
            parts += [full_xs[:, 512 * g:512 * g + 512], full_bc[:, 128 * g:128 * g + 128], full_bc[:, 512 + 128 * g:512 + 128 * g + 128]]
        return jnp.concatenate(parts, axis=1)

    cw_g = group_cols(conv_w_full[:, 0:2048], conv_w_full[:, 2048:3072])
    cb_g = group_cols(conv_b[:, 0:2048], conv_b[:, 2048:3072])
    alog_e, dsk_e = _expand_heads(A_log), _expand_heads(D_skip)
    narrow = lambda v32: jnp.pad(v32.reshape(SSD_GROUPS, 8), ((0, 0), (0, 120))).reshape(1, SSD_GROUPS * 128)
    dtb_n, alog_n = narrow(dt_bias), narrow(A_log)

    pos_chunk = jnp.arange(SGU_BLOCK) // CHUNK
    smask = pos_chunk[None, :] <= pos_chunk[:, None]
    wm_f = jnp.where(smask[None], sgu_w[0], 0.0)
    wm = _c(wm_f)
    wmt = _c(jnp.transpose(wm_f, (0, 2, 1)))
    bias_full = jnp.repeat(sgu_b[0].T, D_MODEL // SGU_GROUPS, axis=1)
    fnw = final_norm_w.reshape(1, D_MODEL)

    proj_a = _mm(xn, wt_a, tm=tm_mm, tn=2048, tk=D_MODEL, name="in_proj_a", out_dtype=MXU_DTYPE, b_is_t=True)
    proj_b = _mm(xn, wt_b, tm=tm_mm, tn=BW_B, tk=D_MODEL, name="in_proj_b", b_is_t=True)
    y_ssd, y_b, hprev, pre_all, dt_all, acs_all = _ssd_fwd_call(proj_b, dtb_n, alog_n, dsk_e, cw_g, cb_g, ssd_norm_w, t_ssd, NG_SSD)
    y_a, merged, merged_t = _tok_fwd_call(proj_a, y_b, gate_b, sgu_norm_g, sgu_norm_b, wm, bias_full, t_tok)
    dh_b, dmerged, loss_t, dfw8 = _out_call(merged, x2, tgt, w_out_full, fnw, t_out)

    dproj_a, dy_b, dgb8, dgam8, dbeta8, dbfull, dws = _tok_bwd_call(
        proj_a, dmerged, y_a, y_b, gate_b, sgu_norm_g, sgu_norm_b, wm, wmt, bias_full, t_tok)
    dproj_b, a512, a768 = _ssd_bwd_call(proj_b, pre_all, dt_all, acs_all, dy_b, y_ssd, hprev, dtb_n, alog_e, dsk_e, cw_g,
                                        ssd_norm_w, t_ssd, NG_SSD)
    dw_uvz = _mm(xnt, dproj_a, tm=D_MODEL, tn=1024, tk=min(2 * tk_dw, s), name="dw_in_uvz", n=6144, out_dtype=MXU_DTYPE)
    dw_gate = _mm(xnt, dproj_a, tm=D_MODEL, tn=1024, tk=min(2 * tk_dw, s), name="dw_in_gate", col0=6, n=4096,
                  out_dtype=MXU_DTYPE)
    dw_zb, dw_xs, dw_bm, dw_cm, dw_dt = _dw_groups(xnt, dproj_b, tk=tk_dw, out_dtype=MXU_DTYPE)

    dw_dt32 = jnp.concatenate([dw_dt[:, 128 * g:128 * g + 8] for g in range(SSD_GROUPS)], axis=1)
    dw_ref = jnp.concatenate([dw_uvz, dw_zb, dw_xs, dw_bm, dw_cm, dw_dt32, dw_gate,
                              jnp.zeros((D_MODEL, 3 * WIN_STEP + WIN_W - IN_W), MXU_DTYPE)], axis=1)
    dw_out_p, sib_i = _dw_out_rs_call(merged_t, dh_b, dw_ref, tn=1024, tk=tk_dw)
    p_out = dw_out_p.reshape(4, D_MODEL // 4, D_MODEL)

    s512 = jnp.sum(a512, axis=2)
    heads = lambda v: jnp.sum(v.reshape(32, HEADDIM), axis=1).reshape(1, 32)
    d_ssd_nw = s512[:, 0].reshape(1, D_MODEL)
    d_dskip = heads(s512[:, 1].reshape(D_MODEL))
    d_alog = heads(s512[:, 2].reshape(D_MODEL)) * (1.0 / HEADDIM) * (-jnp.exp(A_log))
    d_dtb = s512[:, 3, 0:8].reshape(1, 32)
    s768 = jnp.sum(a768, axis=2)
    ungroup = lambda v: jnp.concatenate([v[g, :, 0:512] for g in range(4)] + [v[g, :, 512:640] for g in range(4)]
                                        + [v[g, :, 640:768] for g in range(4)], axis=1)
    d_cw = ungroup(s768[:, 0:4])
    d_cb = ungroup(s768[:, 4:5])
    d_sgu_b = jnp.sum(dbfull.reshape(128, SGU_GROUPS, 128), axis=2).T.reshape(1, SGU_GROUPS, 128)
    d_sgu_w = jnp.where(smask[None], dws, 0.0).reshape(1, SGU_GROUPS, 128, 128)
    fold = lambda a8: jnp.sum(a8, axis=0, keepdims=True)
    small_local = [fold(dgb8), fold(dgam8), fold(dbeta8), d_sgu_w, d_sgu_b, d_cw, d_cb,
                   d_dtb, d_alog, d_dskip, d_ssd_nw, fold(dfw8).reshape(D_MODEL), jnp.sum(loss_t[:, 0, 0]).reshape(1)]
    small_shapes = [a.shape for a in small_local]
    v_local = _pack(small_local)

    core = lax.axis_index("c")
    place = jnp.stack([chip, core]).astype(jnp.int32)
    hr_i, hr_o = D_MODEL // 2, D_MODEL // 8
    sib_o, sib_v = _rs_sibling_call(p_out, v_local)
    s1b_i, o_i = _rs_add_windows(dw_ref, sib_i, place, 256, "rs_add_in")
    s1b_o, o_o = _rs_add(p_out, sib_o, place, 256, "rs_add_out")
    chip_v = _add_pair(v_local, sib_v, v_local.shape[0], "ar_add_small")
    dxn, r_i, r_o, abs_v = _dx_rs_call(dproj_a, wt_a, dproj_b, wt_b, s1b_i, s1b_o, chip_v, tm=tm_mm)
    grad_x, dnw8 = _gradx_call(x2, dxn, dh_b, norm_w, t_row)
    abs_v = lax.dynamic_update_index_in_dim(abs_v, chip_v, chip, 0)
    f_i = _sum_own_recv(o_i, r_i, 256, "rs_sum_in")
    f_o = _sum_own_recv(o_o, r_o, 256, "rs_sum_out")
    sib_f_i, g_w_out, all_nw = _rs_join_call(f_i, f_o, dnw8)
    g_w_out = lax.dynamic_update_slice_in_dim(g_w_out, f_o, core * hr_o, axis=0)
    all_nw = lax.dynamic_update_index_in_dim(all_nw, dnw8, 2 * chip + core, 0)
    g_nw = fold(_sum_slots(all_nw, "ar_sum_norm_w"))
    total_v = _sum_slots(abs_v, "ar_sum_small")
    (g_gb, g_gam, g_beta, g_sw, g_sb, g_cw_full, g_cb, g_dtb, g_alog, g_dsk, g_snw, g_fnw, loss1) = _unpack(total_v, small_shapes)
    g_cw_shard = lax.dynamic_slice(g_cw_full, (0, chip * 768), (4, 768)).reshape(1, 4, 768)
    loss = loss1.reshape(())

    shard_t = lambda win: lax.dynamic_slice_in_dim(win, 8 * chip, SHARD_W, axis=1).T
    g_w_in, d_win, nm_win, nv_win = (a.T for a in _adamw_halves(w_in[0].T, shard_t(f_i), shard_t(sib_f_i), m_w_in[0].T,
                                                                v_w_in[0].T, place, 296, "adamw_w_in"))
    d_wout, nm_wout, nv_wout = _adamw(w_out[0], g_w_out, m_w_out[0], v_w_out[0], 128, "adamw_w_out")
    small_w = [norm_w, gate_b, sgu_norm_g, sgu_norm_b, sgu_w, sgu_b, conv_w, conv_b, dt_bias, A_log, D_skip, ssd_norm_w, final_norm_w]
    small_m = [m_norm_w, m_gate_b, m_sgu_norm_g, m_sgu_norm_b, m_sgu_w, m_sgu_b, m_conv_w, m_conv_b, m_dt_bias, m_A_log, m_D_skip, m_ssd_norm_w, m_final_norm_w]
    small_v = [v_norm_w, v_gate_b, v_sgu_norm_g, v_sgu_norm_b, v_sgu_w, v_sgu_b, v_conv_w, v_conv_b, v_dt_bias, v_A_log, v_D_skip, v_ssd_norm_w, v_final_norm_w]
    small_g = [g_nw, g_gb, g_gam, g_beta, g_sw, g_sb, g_cw_shard, g_cb, g_dtb, g_alog, g_dsk, g_snw, g_fnw]
    shapes_w = [a.shape for a in small_w]
    small_g = [a.reshape(shp) for a, shp in zip(small_g, shapes_w)]
    pw = _pack(small_w)
    pd, pm, pv = _adamw(pw, _pack(small_g), _pack(small_m), _pack(small_v), pw.shape[0], "adamw_small")
    d_small, nm_small, nv_small = _unpack(pd, shapes_w), _unpack(pm, shapes_w), _unpack(pv, shapes_w)

    def with_big(small, win, wout):
        o = list(small)
        return o[0:1] + [win.reshape(1, D_MODEL, SHARD_W)] + o[1:12] + [wout.reshape(1, D_MODEL // 4, D_MODEL)] + o[12:13]

    grads = with_big(small_g, g_w_in, g_w_out)
    deltas = with_big(d_small, d_win, d_wout)
    new_m = with_big(nm_small, nm_win, nm_wout)
    new_v = with_big(nv_small, nv_win, nv_wout)
    return (loss, grad_x.reshape(1, s, D_MODEL), *grads, *deltas, *new_m, *new_v)
```

```python
import functools

import jax
import jax.numpy as jnp
from jax import lax
from jax.experimental import pallas as pl
from jax.experimental.pallas import tpu as pltpu

F32 = jnp.float32
MXU_DTYPE = jnp.bfloat16

D_MODEL = 2048
EPS = 1e-5
CHUNK = 64
SGU_BLOCK = 128
SGU_GROUPS = 16
SSD_GROUPS = 4
SSD_GW = 512
SSD_STATE = 128
HEADDIM = 64
IN_W = 15392
SHARD_W = IN_W // 4
BW_B = 1408
FW_B = BW_B
XBC_O, DT_O = 512, 1280
NA = 10240

ADAM_LR = 0.001
ADAM_B1 = 0.9
ADAM_B2 = 0.999
ADAM_EPS = 1e-08
ADAM_WD = 0.01
ADAM_STEP = 10

T_SSD = 256
NG_SSD = 4
T_TOK = 128
T_OUT = 256
T_ROW = 512
TM_MM = 1024
TK_DW = 1024
VMEM_CAP = 60 * 1024 * 1024
MESH = pl.DeviceIdType.MESH


def _cparams(sem, est_bytes):
    lim = int(min(VMEM_CAP, max(32 * 1024 * 1024, est_bytes + 12 * 1024 * 1024)))
    return pltpu.CompilerParams(dimension_semantics=sem, vmem_limit_bytes=lim)


def _c(x):
    return x.astype(MXU_DTYPE)


def _dot(a, b):
    return jnp.dot(a, b, preferred_element_type=F32)


def _dot_nt(a, b):
    return lax.dot_general(a, b, (((1,), (1,)), ((), ())), preferred_element_type=F32)


def _dot_tn(a, b):
    return lax.dot_general(a, b, (((0,), (0,)), ((), ())), preferred_element_type=F32)


def _split(x, n):
    parts, r = [], x
    for _ in range(n):
        p = _c(r)
        parts.append(p)
        r = r - p.astype(F32)
    return parts


def _dot01_l(m01, x, n):
    acc = None
    for p in _split(x, n):
        t = _dot(m01, p)
        acc = t if acc is None else acc + t
    return acc


def _dot01_r(x, m01, n):
    acc = None
    for p in _split(x, n):
        t = _dot(p, m01)
        acc = t if acc is None else acc + t
    return acc


def _sigmoid(x):
    return 0.5 * jnp.tanh(0.5 * x) + 0.5


def _fold8(x):
    r, w = x.shape
    return jnp.sum(x.reshape(r // 8, 8, w), axis=0)


def _iota(shape, dim):
    return lax.broadcasted_iota(jnp.int32, shape, dim)


def _ssd_masks():
    l64 = _iota((CHUNK, SSD_GW), 0)
    s64 = jnp.bitwise_and(_iota((CHUNK, SSD_GW), 1), CHUNK - 1)
    diag = l64 == s64
    causal = l64 >= s64
    row_last = l64 == CHUNK - 1
    r4 = lax.shift_right_logical(_iota((256, 256), 0), 6)
    c4 = lax.shift_right_logical(_iota((256, 256), 1), 6)
    mask4 = r4 == c4
    return diag, causal, row_last, mask4


def _cumsum_mats(t):
    r, c = _iota((t, t), 0), _iota((t, t), 1)
    same = lax.shift_right_logical(r, 6) == lax.shift_right_logical(c, 6)
    tri = _c(jnp.where(same, jnp.where(c <= r, 1.0, 0.0), 0.0))
    trit = _c(jnp.where(same, jnp.where(c >= r, 1.0, 0.0), 0.0))
    return tri, trit


def _head_expand_mat():
    return _c(jnp.where(_iota((128, SSD_GW), 0) == lax.shift_right_logical(_iota((128, SSD_GW), 1), 6), 1.0, 0.0))


def _ssd_common(xs, bm, cm, dt, acs, masks):
    diag, causal, row_last, mask4 = masks
    row_e = jnp.sum(jnp.where(diag, acs, 0.0), axis=0, keepdims=True)
    seg = acs - row_e
    lm = jnp.exp(jnp.where(causal, seg, -1e30))
    bb, cb = _c(bm), _c(cm)
    brep = jnp.concatenate([bb] * 8, axis=0)
    cbrep = _dot_nt(cb, brep)
    m = cbrep * lm
    xdt = xs * dt
    acs_last = jnp.sum(jnp.where(row_last, acs, 0.0), axis=0, keepdims=True)
    dec = jnp.exp(acs_last - acs)
    eacs = jnp.exp(acs)
    cd = jnp.exp(acs_last)
    return dict(lm=lm, bb=bb, cb=cb, brep=brep, m=m, xdt=xdt, dec=dec, eacs=eacs, cd=cd)


def _blockdiag4(xb, mask4):
    return jnp.where(mask4, jnp.concatenate([xb] * 4, axis=0), jnp.zeros((), xb.dtype))


def _ssd_chunk_fwd(xs, bm, cm, dt, acs, d_skip, ht, masks):
    q = _ssd_common(xs, bm, cm, dt, acs, masks)
    mask4 = masks[3]
    mb, xdtb = _c(q["m"]), _c(q["xdt"])
    yd = []
    for blk in range(2):
        sl = slice(256 * blk, 256 * blk + 256)
        yd.append(_dot(mb[:, sl], _blockdiag4(xdtb[:, sl], mask4)))
    y_diag = jnp.concatenate(yd, axis=1)
    p = _dot(q["cb"], _c(ht))
    y = y_diag + p * q["eacs"] + xs * d_skip
    st = _dot_tn(q["bb"], _c(q["xdt"] * q["dec"]))
    return y, ht * q["cd"] + st


def _ssd_chunk_bwd(xs, bm, cm, dt, acs, d_skip, hprev, dht, dy, masks):
    diag, causal, row_last, mask4 = masks
    q = _ssd_common(xs, bm, cm, dt, acs, masks)
    lm, bb, cb, brep, m, xdt, dec, eacs, cd = (q[k] for k in ("lm", "bb", "cb", "brep", "m", "xdt", "dec", "eacs", "cd"))
    hb = _c(hprev)
    yoff = _dot(cb, hb) * eacs
    dyb = _c(dy)
    dpb = _c(dy * eacs)
    d_c = _dot_nt(dpb, hb)
    dh_y = _dot_tn(cb, dpb)
    mb, xdtb = _c(m), _c(xdt)
    dm_parts, dxdt_parts = [], []
    for blk in range(2):
        sl = slice(256 * blk, 256 * blk + 256)
        bd = _blockdiag4(xdtb[:, sl], mask4)
        dm_parts.append(_dot_nt(dyb[:, sl], bd))
        dxf = jnp.where(mask4, _dot_tn(mb[:, sl], dyb[:, sl]), 0.0)
        dxdt_parts.append(dxf[0:64] + dxf[64:128] + dxf[128:192] + dxf[192:256])
    dm = jnp.concatenate(dm_parts, axis=1)
    dxdt = jnp.concatenate(dxdt_parts, axis=1)
    dcbb = _c(dm * lm)
    g = dm * m
    d_c = d_c + _dot(dcbb, brep)
    dbrep = _dot_tn(dcbb, cb)
    d_b = dbrep[0:64]
    for r in range(1, 8):
        d_b = d_b + dbrep[64 * r:64 * r + 64]
    dhtb = _c(dht)
    dxd = _dot(bb, dhtb)
    xd = xdt * dec
    dxdt = dxdt + dxd * dec
    tq = dxd * xd
    d_b = d_b + _dot_nt(_c(xd), dhtb)
    dcd = jnp.sum(dht * hprev, axis=0, keepdims=True)
    col_g = jnp.sum(g, axis=0, keepdims=True)
    last = jnp.sum(tq, axis=0, keepdims=True) + dcd * cd
    qq = g - jnp.where(diag, col_g, 0.0) + dy * yoff - tq + jnp.where(row_last, last, 0.0)
    dxs = dxdt * dt + dy * d_skip
    return dxs, d_b, d_c, dht * cd + dh_y, dy * xs, qq, dxdt * xs


def _ssd_finish_dt(qq, p1, dt, a_neg, trit, mask4):
    bd4 = _c(jnp.where(mask4, 1.0, 0.0))
    dacs = jnp.concatenate([_dot01_r(qq[:, 256 * b:256 * b + 256], bd4, 2) for b in range(2)], axis=1)
    da = _dot01_l(trit, dacs, 2)
    return p1 + da * (a_neg * (1.0 / HEADDIM)), da * dt


def _softplus(x):
    return jnp.maximum(x, 0.0) + jnp.log(1.0 + jnp.exp(-jnp.abs(x)))


def _conv_taps(xpad, t):
    taps = []
    for k in range(4):
        sh = 3 - k
        v = xpad if sh == 0 else pltpu.roll(xpad, sh, 0)
        taps.append(v[8:8 + t])
    return taps


def _mm(a, b, *, tm, tn, tk, name, out_dtype=F32, col0=0, n=None, b_is_t=False):
    m, k = a.shape
    n = b.shape[0 if b_is_t else 1] if n is None else n
    nk = k // tk
    assert m % tm == 0 and n % tn == 0 and k % tk == 0, (a.shape, b.shape, tm, tn, tk)
    dot = _dot_nt if b_is_t else _dot
    via_acc = nk > 1 and out_dtype != F32

    def body(a_ref, b_ref, o_ref, *acc):
        if nk == 1:
            o_ref[...] = dot(a_ref[...], b_ref[...]).astype(out_dtype)
            return
        acc_ref = acc[0] if via_acc else o_ref

        @pl.when(pl.program_id(2) == 0)
        def _():
            acc_ref[...] = jnp.zeros_like(acc_ref)

        acc_ref[...] += dot(a_ref[...], b_ref[...])
        if via_acc:
            @pl.when(pl.program_id(2) == nk - 1)
            def _():
                o_ref[...] = acc_ref[...].astype(out_dtype)

    isz = jnp.dtype(a.dtype).itemsize
    est = 2 * (tm * tk + tk * tn) * isz + 3 * tm * tn * 4
    return pl.pallas_call(
        body, name=name,
        grid=(m // tm, n // tn, nk),
        in_specs=[pl.BlockSpec((tm, tk), lambda i, j, kk: (i, kk)),
                  pl.BlockSpec((tn, tk), lambda i, j, kk: (j + col0, kk)) if b_is_t
                  else pl.BlockSpec((tk, tn), lambda i, j, kk: (kk, j + col0))],
        out_specs=pl.BlockSpec((tm, tn), lambda i, j, kk: (i, j)),
        out_shape=jax.ShapeDtypeStruct((m, n), out_dtype),
        scratch_shapes=[pltpu.VMEM((tm, tn), F32)] if via_acc else [],
        compiler_params=_cparams(("parallel", "parallel", "arbitrary"), est),
    )(a, b)


def _dw_groups(xnt, dpb, *, tk, out_dtype):
    m, k = xnt.shape
    nk = k // tk

    def body(a_ref, b_ref, zb_ref, xs_ref, bm_ref, cm_ref, dt_ref, acc_ref):
        @pl.when(pl.program_id(1) == 0)
        def _():
            acc_ref[...] = jnp.zeros_like(acc_ref)

        acc_ref[...] += _dot(a_ref[...], b_ref[...])

        @pl.when(pl.program_id(1) == nk - 1)
        def _():
            for o_ref, lo, hi in ((zb_ref, 0, 512), (xs_ref, 512, 1024), (bm_ref, 1024, 1152), (cm_ref, 1152, 1280),
                                  (dt_ref, 1280, 1408)):
                o_ref[...] = acc_ref[:, lo:hi].astype(out_dtype)

    isz, osz = jnp.dtype(xnt.dtype).itemsize, jnp.dtype(out_dtype).itemsize
    est = 2 * (m * tk + tk * BW_B) * isz + m * BW_B * (4 + 2 * osz)
    piece = lambda w: pl.BlockSpec((m, w), lambda g, kk: (0, g))
    return pl.pallas_call(
        body, name="dw_in_b",
        grid=(SSD_GROUPS, nk),
        in_specs=[pl.BlockSpec((m, tk), lambda g, kk: (0, kk)), pl.BlockSpec((tk, BW_B), lambda g, kk: (kk, g))],
        out_specs=[piece(512), piece(512), piece(128), piece(128), piece(128)],
        out_shape=[jax.ShapeDtypeStruct((m, w), out_dtype) for w in (2048, 2048, 512, 512, 512)],
        scratch_shapes=[pltpu.VMEM((m, BW_B), F32)],
        compiler_params=_cparams(("parallel", "arbitrary"), est),
    )(xnt, dpb)


def _dx_rs_call(dpa, wta, dpb, wtb, sb_in, sb_out, chip_v, *, tm):
    s = dpa.shape[0]
    tka, tkb = 1024, BW_B
    nka, nkb = dpa.shape[1] // tka, dpb.shape[1] // tkb
    ni, nk = s // tm, nka + nkb

    def body(a_ref, wa_ref, b_ref, wb_ref, sbin, sbout, cv, o_ref, rc_in, rc_out, abs_v, send, recv):
        i, kk = pl.program_id(0), pl.program_id(1)

        def copies():
            x, y, c, me, others = _place()
            sends, recvs = [], []
            for j, chip in enumerate(others):
                kj = 2 * chip[0] + chip[1]
                to = (*chip, c)
                sends += [_remote(sbin.at[kj], rc_in.at[j], send, recv, j, to),
                          _remote(sbout.at[kj], rc_out.at[j], send, recv, 3 + j, to),
                          _remote(cv, abs_v.at[me], send, recv, 6 + j, to)]
                recvs += [sends[-3], sends[-2], _remote(cv, abs_v.at[kj], send, recv, 6 + j, to)]
            return sends, recvs

        @pl.when((i == 0) & (kk == 0))
        def _():
            for cp in copies()[0]:
                cp.start()

        @pl.when(kk == 0)
        def _():
            o_ref[...] = jnp.zeros_like(o_ref)

        @pl.when(kk < nka)
        def _():
            o_ref[...] += _dot(a_ref[...], wa_ref[...])

        @pl.when(kk >= nka)
        def _():
            o_ref[...] += _dot(b_ref[...], wb_ref[...])

        @pl.when((i == ni - 1) & (kk == nk - 1))
        def _():
            sends, recvs = copies()
            for cp in recvs:
                cp.wait_recv()
            for cp in sends:
                cp.wait_send()

    isz = jnp.dtype(dpa.dtype).itemsize
    est = 2 * isz * (tm * tka + tka * D_MODEL + tm * tkb + tkb * D_MODEL) + 2 * tm * D_MODEL * 4
    outs = [jax.ShapeDtypeStruct((s, D_MODEL), F32),
            jax.ShapeDtypeStruct((3,) + sb_in.shape[1:], sb_in.dtype), jax.ShapeDtypeStruct((3,) + sb_out.shape[1:], sb_out.dtype),
            jax.ShapeDtypeStruct((4,) + chip_v.shape, F32)]
    return pl.pallas_call(
        body, name="dx_matmul_rs_chips",
        grid=(ni, nk),
        in_specs=[
            pl.BlockSpec((tm, tka), lambda i, kk: (i, jnp.minimum(kk, nka - 1))),
            pl.BlockSpec((tka, D_MODEL), lambda i, kk: (jnp.minimum(kk, nka - 1), 0)),
            pl.BlockSpec((tm, tkb), lambda i, kk: (i, jnp.maximum(kk - nka, 0))),
            pl.BlockSpec((tkb, D_MODEL), lambda i, kk: (jnp.maximum(kk - nka, 0), 0)),
            ANY, ANY, ANY,
        ],
        out_specs=[pl.BlockSpec((tm, D_MODEL), lambda i, kk: (i, 0)), ANY, ANY, ANY],
        out_shape=outs,
        scratch_shapes=[pltpu.SemaphoreType.DMA((9,)), pltpu.SemaphoreType.DMA((9,))],
        compiler_params=_cparams(("arbitrary", "arbitrary"), est),
    )(dpa, wta, dpb, wtb, sb_in, sb_out, chip_v)


def _gradx_call(x, dxn, dh, norm_w, tm):
    s = x.shape[0]

    def body(x_ref, g_ref, dh_ref, w_ref, gx_ref, dw_ref):
        @pl.when(pl.program_id(0) == 0)
        def _():
            dw_ref[...] = jnp.zeros_like(dw_ref)

        xv, gv = x_ref[...], g_ref[...]
        r = lax.rsqrt(jnp.mean(xv * xv, axis=-1, keepdims=True) + EPS)
        gw = gv * w_ref[...]
        gx_ref[...] = r * gw - xv * (r * r * r) * jnp.mean(xv * gw, axis=-1, keepdims=True) + dh_ref[...].astype(F32)
        dw_ref[...] += _fold8(gv * (xv * r))

    row = pl.BlockSpec((tm, D_MODEL), lambda i: (i, 0))
    return pl.pallas_call(
        body, name="grad_x",
        grid=(s // tm,),
        in_specs=[row, row, row, pl.BlockSpec((1, D_MODEL), lambda i: (0, 0))],
        out_specs=[row, pl.BlockSpec((8, D_MODEL), lambda i: (0, 0))],
        out_shape=[jax.ShapeDtypeStruct((s, D_MODEL), F32), jax.ShapeDtypeStruct((8, D_MODEL), F32)],
        compiler_params=_cparams(("arbitrary",), 2 * tm * D_MODEL * 16),
    )(x, dxn, dh, norm_w)


def _layernorm_stats(v):
    mu = jnp.mean(v, axis=-1, keepdims=True)
    vc = v - mu
    var = jnp.mean(vc * vc, axis=-1, keepdims=True)
    return vc * lax.rsqrt(var + EPS), lax.rsqrt(var + EPS)


def _tok_fwd_call(proj_a, y_b, gate_b, sgu_g, sgu_beta, wm, bias_full, t):
    s = proj_a.shape[0]

    def body(pa_ref, yb_ref, gb_ref, g_ref, be_ref, wm_ref, bf_ref, ya_ref, mg_ref, mgt_ref, mix_ref):
        u = pa_ref[:, 0:2048].astype(F32)
        v = pa_ref[:, 2048:4096].astype(F32)
        za = pa_ref[:, 4096:6144].astype(F32)
        xhat, _ = _layernorm_stats(v)
        vnb = _c(xhat * g_ref[...] + be_ref[...])
        for gi in range(SGU_GROUPS):
            sl = slice(128 * gi, 128 * gi + 128)
            mix_ref[:, sl] = _dot(wm_ref[gi], vnb[:, sl])
        mixed = mix_ref[...] + bf_ref[...]
        y_a = u * mixed * (za * _sigmoid(za))
        g0 = _sigmoid(pa_ref[:, 6144:8192].astype(F32) + gb_ref[:, 0:2048])
        g1 = _sigmoid(pa_ref[:, 8192:10240].astype(F32) + gb_ref[:, 2048:4096])
        merged = g0 * y_a + g1 * yb_ref[...].astype(F32)
        ya_ref[...] = _c(y_a)
        mg_ref[...] = _c(merged)
        mgt_ref[...] = _c(merged.T)

    row = pl.BlockSpec((t, D_MODEL), lambda i: (i, 0))
    vec = lambda w: pl.BlockSpec((1, w), lambda i: (0, 0))
    return pl.pallas_call(
        body, name="tok_fwd",
        grid=(s // t,),
        in_specs=[pl.BlockSpec((t, NA), lambda i: (i, 0)), row, vec(4096), vec(2048), vec(2048),
                  pl.BlockSpec((SGU_GROUPS, 128, 128), lambda i: (0, 0, 0)), pl.BlockSpec((128, D_MODEL), lambda i: (0, 0))],
        out_specs=[row, row, pl.BlockSpec((D_MODEL, t), lambda i: (0, i))],
        out_shape=[jax.ShapeDtypeStruct((s, D_MODEL), MXU_DTYPE), jax.ShapeDtypeStruct((s, D_MODEL), MXU_DTYPE),
                   jax.ShapeDtypeStruct((D_MODEL, s), MXU_DTYPE)],
        scratch_shapes=[pltpu.VMEM((t, D_MODEL), F32)],
        compiler_params=_cparams(("parallel",), 2 * t * NA * 4 + 12 * t * D_MODEL * 4),
    )(proj_a, y_b, gate_b, sgu_g, sgu_beta, wm, bias_full)


def _tok_bwd_call(proj_a, dmerged, y_a, y_b, gate_b, sgu_g, sgu_beta, wm, wmt, bias_full, t):
    s = proj_a.shape[0]

    def body(pa_ref, dm_ref, ya_ref, yb_ref, gb_ref, g_ref, be_ref, wm_ref, wmt_ref, bf_ref,
             dpa_ref, dyb_ref, dgb_ref, dgam_ref, dbeta_ref, dbf_ref, dws_ref, mix_ref, dvn_ref):
        @pl.when(pl.program_id(0) == 0)
        def _():
            dgb_ref[...] = jnp.zeros_like(dgb_ref)
            dgam_ref[...] = jnp.zeros_like(dgam_ref)
            dbeta_ref[...] = jnp.zeros_like(dbeta_ref)
            dbf_ref[...] = jnp.zeros_like(dbf_ref)
            dws_ref[...] = jnp.zeros_like(dws_ref)

        u = pa_ref[:, 0:2048].astype(F32)
        v = pa_ref[:, 2048:4096].astype(F32)
        za = pa_ref[:, 4096:6144].astype(F32)
        xhat, rstd = _layernorm_stats(v)
        vnb = _c(xhat * g_ref[...] + be_ref[...])
        for gi in range(SGU_GROUPS):
            sl = slice(128 * gi, 128 * gi + 128)
            mix_ref[:, sl] = _dot(wm_ref[gi], vnb[:, sl])
        mixed = mix_ref[...] + bf_ref[...]
        sig = _sigmoid(za)
        sz = za * sig
        dm = dm_ref[...].astype(F32)
        y_a = ya_ref[...].astype(F32)
        g0 = _sigmoid(pa_ref[:, 6144:8192].astype(F32) + gb_ref[:, 0:2048])
        g1 = _sigmoid(pa_ref[:, 8192:10240].astype(F32) + gb_ref[:, 2048:4096])
        dgl0 = dm * y_a * g0 * (1.0 - g0)
        dgl1 = dm * yb_ref[...].astype(F32) * g1 * (1.0 - g1)
        dyb_ref[...] = _c(dm * g1)
        dya = dm * g0
        dpa_ref[:, 6144:8192] = _c(dgl0)
        dpa_ref[:, 8192:10240] = _c(dgl1)
        dgb_ref[:, 0:2048] += _fold8(dgl0)
        dgb_ref[:, 2048:4096] += _fold8(dgl1)
        dpa_ref[:, 0:2048] = _c(dya * mixed * sz)
        dpa_ref[:, 4096:6144] = _c(dya * (u * mixed) * (sig * (1.0 + za * (1.0 - sig))))
        dmixed = dya * u * sz
        dbf_ref[...] += dmixed
        dmb = _c(dmixed)
        for gi in range(SGU_GROUPS):
            sl = slice(128 * gi, 128 * gi + 128)
            dvn_ref[:, sl] = _dot(wmt_ref[gi], dmb[:, sl])
            dws_ref[gi] += _dot_nt(dmb[:, sl], vnb[:, sl])
        dvn = dvn_ref[...]
        dgam_ref[...] += _fold8(dvn * xhat)
        dbeta_ref[...] += _fold8(dvn)
        dxh = dvn * g_ref[...]
        dv = rstd * (dxh - jnp.mean(dxh, axis=-1, keepdims=True) - xhat * jnp.mean(dxh * xhat, axis=-1, keepdims=True))
        dpa_ref[:, 2048:4096] = _c(dv)

    row = pl.BlockSpec((t, D_MODEL), lambda i: (i, 0))
    vec = lambda w: pl.BlockSpec((1, w), lambda i: (0, 0))
    acc = lambda w: pl.BlockSpec((8, w), lambda i: (0, 0))
    wspec = pl.BlockSpec((SGU_GROUPS, 128, 128), lambda i: (0, 0, 0))
    return pl.pallas_call(
        body, name="tok_bwd",
        grid=(s // t,),
        in_specs=[pl.BlockSpec((t, NA), lambda i: (i, 0)), row, row, row, vec(4096), vec(2048), vec(2048),
                  wspec, wspec, pl.BlockSpec((128, D_MODEL), lambda i: (0, 0))],
        out_specs=[pl.BlockSpec((t, NA), lambda i: (i, 0)), row, acc(4096), acc(2048), acc(2048),
                   pl.BlockSpec((128, D_MODEL), lambda i: (0, 0)), wspec],
        out_shape=[jax.ShapeDtypeStruct((s, NA), MXU_DTYPE), jax.ShapeDtypeStruct((s, D_MODEL), MXU_DTYPE),
                   jax.ShapeDtypeStruct((8, 4096), F32), jax.ShapeDtypeStruct((8, 2048), F32),
                   jax.ShapeDtypeStruct((8, 2048), F32), jax.ShapeDtypeStruct((128, D_MODEL), F32),
                   jax.ShapeDtypeStruct((SGU_GROUPS, 128, 128), F32)],
        scratch_shapes=[pltpu.VMEM((t, D_MODEL), F32), pltpu.VMEM((t, D_MODEL), F32)],
        compiler_params=_cparams(("arbitrary",), 2 * t * NA * 6 + 16 * t * D_MODEL * 4),
    )(proj_a, dmerged, y_a, y_b, gate_b, sgu_g, sgu_beta, wm, wmt, bias_full)


def _out_call(merged, x, target, w_out, fnw, t):
    s = x.shape[0]
    nt = s // t

    def body(mg_ref, x_ref, t_ref, w_ref, fw_ref, dhb_ref, dmg_ref, loss_ref, dfw_ref):
        @pl.when(pl.program_id(0) == 0)
        def _():
            dfw_ref[...] = jnp.zeros_like(dfw_ref)

        h = x_ref[...] + _dot(mg_ref[...], w_ref[...])
        r = lax.rsqrt(jnp.mean(h * h, axis=-1, keepdims=True) + EPS)
        hn = h * r
        err = hn * fw_ref[...] - t_ref[...]
        loss_ref[...] = jnp.full(loss_ref.shape, 0.5 * jnp.sum(jnp.mean(err * err, axis=-1, keepdims=True)), F32)
        dy = err * (1.0 / D_MODEL)
        dfw_ref[...] += _fold8(dy * hn)
        gw = dy * fw_ref[...]
        dh = r * gw - h * (r * r * r) * jnp.mean(h * gw, axis=-1, keepdims=True)
        dhb = _c(dh)
        dhb_ref[...] = dhb
        dmg_ref[...] = _c(_dot_nt(dhb, w_ref[...]))

    row = pl.BlockSpec((t, D_MODEL), lambda i: (i, 0))
    return pl.pallas_call(
        body, name="out_proj_loss",
        grid=(nt,),
        in_specs=[row, row, row, pl.BlockSpec((D_MODEL, D_MODEL), lambda i: (0, 0)), pl.BlockSpec((1, D_MODEL), lambda i: (0, 0))],
        out_specs=[row, row, pl.BlockSpec((1, 8, 128), lambda i: (i, 0, 0)), pl.BlockSpec((8, D_MODEL), lambda i: (0, 0))],
        out_shape=[jax.ShapeDtypeStruct((s, D_MODEL), MXU_DTYPE),
                   jax.ShapeDtypeStruct((s, D_MODEL), MXU_DTYPE), jax.ShapeDtypeStruct((nt, 8, 128), F32),
                   jax.ShapeDtypeStruct((8, D_MODEL), F32)],
        compiler_params=_cparams(("arbitrary",), 2 * D_MODEL * D_MODEL * 2 + 2 * t * D_MODEL * 20),
    )(merged, x, target, w_out, fnw)


def _ssd_fwd_call(proj_b, dtb, alog, dsk, cw, cb, nw, t, ng):
    s = proj_b.shape[0]
    nt, nch = s // t, t // CHUNK

    def body(pb_ref, halo_ref, dtb_ref, al_ref, ds_ref, cw_ref, cb_ref, nw_ref, y_ref, yb_ref, hp_ref, pre_ref,
             dt_ref, acs_ref, ht_ref, prev_ref):
        i = pl.program_id(1)

        @pl.when(i == 0)
        def _():
            ht_ref[...] = jnp.zeros_like(ht_ref)

        for gi in range(ng):
            prev_ref[:, 768 * gi:768 * gi + 768] = jnp.where(i == 0, 0.0, halo_ref[:, FW_B * gi + XBC_O:FW_B * gi + DT_O])
        masks = _ssd_masks()
        tri, _ = _cumsum_mats(t)
        a_neg = -jnp.exp(al_ref[...])
        expand = _head_expand_mat()
        for gi in range(ng):
            fo, go, no = FW_B * gi, SSD_GW * gi, 128 * gi
            dt_n = _softplus(pb_ref[:, fo + DT_O:fo + DT_O + 128] + dtb_ref[:, no:no + 128])
            acs_n = _dot01_l(tri, dt_n * a_neg[:, no:no + 128], 3)
            dt_ref[:, go:go + 512] = _dot01_r(dt_n, expand, 2)
            acs_ref[:, go:go + 512] = _dot01_r(acs_n, expand, 2)

        def chunk(c, carry):
            rows = pl.ds(pl.multiple_of(c * CHUNK, CHUNK), CHUNK)
            for gi in range(ng):
                fo, co, go, no = FW_B * gi, 768 * gi, SSD_GW * gi, 128 * gi
                xbc = pb_ref[rows, fo + XBC_O:fo + DT_O]
                taps = _conv_taps(jnp.concatenate([prev_ref[:, co:co + 768], xbc], axis=0), CHUNK)
                prev_ref[:, co:co + 768] = xbc[CHUNK - 8:CHUNK]
                pre = cb_ref[:, co:co + 768]
                for k in range(4):
                    pre = pre + taps[k] * cw_ref[k:k + 1, co:co + 768]
                pre_ref[rows, co:co + 768] = pre
                act = pre * _sigmoid(pre)
                dt = dt_ref[rows, go:go + 512]
                acs = acs_ref[rows, go:go + 512]
                ht = ht_ref[gi]
                hp_ref[c, :, go:go + 512] = ht
                y, ht_new = _ssd_chunk_fwd(act[:, 0:512], act[:, 512:640], act[:, 640:768], dt, acs,
                                           ds_ref[:, go:go + 512], ht, masks)
                y_ref[rows, go:go + 512] = y
                ht_ref[gi] = ht_new
                zb = pb_ref[rows, fo:fo + 512]
                hh = y * (zb * _sigmoid(zb))
                rr = lax.rsqrt(jnp.mean(hh * hh, axis=-1, keepdims=True) + EPS)
                yb_ref[rows, go:go + 512] = _c(hh * rr * nw_ref[:, go:go + 512])
            return carry

        lax.fori_loop(0, nch, chunk, 0)

    gvec = lambda w: pl.BlockSpec((1, ng * w), lambda g, i: (0, g))
    return pl.pallas_call(
        body, name="ssd_fwd",
        grid=(SSD_GROUPS // ng, nt),
        in_specs=[pl.BlockSpec((t, ng * FW_B), lambda g, i: (i, g)),
                  pl.BlockSpec((8, ng * FW_B), lambda g, i: (jnp.maximum(i * (t // 8) - 1, 0), g)),
                  gvec(128), gvec(128), gvec(512),
                  pl.BlockSpec((4, ng * 768), lambda g, i: (0, g)), gvec(768), gvec(512)],
        out_specs=[pl.BlockSpec((t, ng * SSD_GW), lambda g, i: (i, g)), pl.BlockSpec((t, ng * SSD_GW), lambda g, i: (i, g)),
                   pl.BlockSpec((nch, SSD_STATE, ng * SSD_GW), lambda g, i: (i, 0, g)),
                   pl.BlockSpec((t, ng * 768), lambda g, i: (i, g)),
                   pl.BlockSpec((t, ng * SSD_GW), lambda g, i: (i, g)), pl.BlockSpec((t, ng * SSD_GW), lambda g, i: (i, g))],
        out_shape=[jax.ShapeDtypeStruct((s, D_MODEL), F32), jax.ShapeDtypeStruct((s, D_MODEL), MXU_DTYPE),
                   jax.ShapeDtypeStruct((s // CHUNK, SSD_STATE, D_MODEL), F32),
                   jax.ShapeDtypeStruct((s, SSD_GROUPS * 768), F32),
                   jax.ShapeDtypeStruct((s, D_MODEL), F32), jax.ShapeDtypeStruct((s, D_MODEL), F32)],
        scratch_shapes=[pltpu.VMEM((ng, SSD_STATE, SSD_GW), F32), pltpu.VMEM((8, ng * 768), F32)],
        compiler_params=_cparams(("parallel", "arbitrary"), ng * (2 * t * FW_B * 4 + 16 * t * SSD_GW * 4) + 16 * 1024 * 1024),
    )(proj_b, proj_b, dtb, alog, dsk, cw, cb, nw)


def _ssd_bwd_call(proj_b, pre_all, dt_all, acs_all, dyb, y, hprev, dtb, alog, dsk, cw, nw, t, ng):
    s = proj_b.shape[0]
    nt, nch = s // t, t // CHUNK

    def body(pb_ref, pre_ref, dt_ref, acs_ref, dyb_ref, y_ref, hp_ref, dtb_ref, al_ref, ds_ref, cw_ref, nw_ref,
             dpb_ref, a512_ref, a768_ref, dht_ref, nxt_ref, q_ref, p1_ref):
        i = pl.program_id(1)

        @pl.when(i == 0)
        def _():
            dht_ref[...] = jnp.zeros_like(dht_ref)
            nxt_ref[...] = jnp.zeros_like(nxt_ref)
            a512_ref[...] = jnp.zeros_like(a512_ref)
            a768_ref[...] = jnp.zeros_like(a768_ref)

        _, trit = _cumsum_mats(t)
        a_neg = -jnp.exp(al_ref[...])
        masks = _ssd_masks()

        def chunk(cc, carry):
            c = nch - 1 - cc
            rows = pl.ds(pl.multiple_of(c * CHUNK, CHUNK), CHUNK)
            for gi in range(ng):
                fo, co, go, bo = FW_B * gi, 768 * gi, SSD_GW * gi, BW_B * gi
                pre = pre_ref[rows, co:co + 768]
                sp = _sigmoid(pre)
                act = pre * sp
                zb = pb_ref[rows, fo:fo + 512]
                yv = y_ref[rows, go:go + 512]
                sgz = _sigmoid(zb)
                sz = zb * sgz
                hh = yv * sz
                rr = lax.rsqrt(jnp.mean(hh * hh, axis=-1, keepdims=True) + EPS)
                dyb = dyb_ref[rows, go:go + 512].astype(F32)
                a512_ref[gi, 0] += _fold8(dyb * (hh * rr))
                tt = dyb * nw_ref[:, go:go + 512]
                dhh = rr * tt - hh * (rr * rr * rr) * jnp.mean(hh * tt, axis=-1, keepdims=True)
                dpb_ref[rows, bo:bo + 512] = _c(dhh * yv * (sgz * (1.0 + zb * (1.0 - sgz))))
                dxs, d_b, d_c, dht_prev, dyxs, qq, p1 = _ssd_chunk_bwd(
                    act[:, 0:512], act[:, 512:640], act[:, 640:768], dt_ref[rows, go:go + 512], acs_ref[rows, go:go + 512],
                    ds_ref[:, go:go + 512], hp_ref[c, :, go:go + 512], dht_ref[gi], dhh * sz, masks)
                dht_ref[gi] = dht_prev
                q_ref[rows, go:go + 512] = qq
                p1_ref[rows, go:go + 512] = p1
                a512_ref[gi, 1] += _fold8(dyxs)
                dpre = jnp.concatenate([dxs, d_b, d_c], axis=1) * (sp * (1.0 + pre * (1.0 - sp)))
                xbc = pb_ref[rows, fo + XBC_O:fo + DT_O]
                a768_ref[gi, 4] += _fold8(dpre)
                a768_ref[gi, 3] += _fold8(dpre * xbc)
                dpad = jnp.concatenate([dpre, nxt_ref[:, co:co + 768]], axis=0)
                dx = dpre * cw_ref[3:4, co:co + 768]
                for k in range(3):
                    d_k = pltpu.roll(dpad, CHUNK + 8 - (3 - k), 0)[0:CHUNK]
                    dx = dx + d_k * cw_ref[k:k + 1, co:co + 768]
                    a768_ref[gi, k] += _fold8(d_k * xbc)
                nxt_ref[:, co:co + 768] = dpre[0:8]
                dpb_ref[rows, bo + 512:bo + 1280] = _c(dx)
            return carry

        lax.fori_loop(0, nch, chunk, 0)

        rsel = _c(jnp.where(lax.shift_right_logical(_iota((SSD_GW, 128), 0), 6) == _iota((SSD_GW, 128), 1), 1.0, 0.0))
        for gi in range(ng):
            fo, go, bo, no = FW_B * gi, SSD_GW * gi, BW_B * gi, 128 * gi
            ddt, dadt = _ssd_finish_dt(q_ref[:, go:go + 512], p1_ref[:, go:go + 512], dt_ref[:, go:go + 512],
                                       a_neg[:, go:go + 512], trit, masks[3])
            sig_n = _sigmoid(pb_ref[:, fo + DT_O:fo + DT_O + 128] + dtb_ref[:, no:no + 128])
            ddtr_n = _dot01_r(ddt, rsel, 2) * sig_n
            dpb_ref[:, bo + DT_O:bo + DT_O + 128] = _c(ddtr_n)
            a512_ref[gi, 2] += _fold8(dadt)
            a512_ref[gi, 3, :, 0:128] += _fold8(ddtr_n)

    gvec = lambda w: pl.BlockSpec((1, ng * w), lambda g, i: (0, g))
    rev = lambda w: pl.BlockSpec((t, ng * w), lambda g, i: (nt - 1 - i, g))
    return pl.pallas_call(
        body, name="ssd_bwd",
        grid=(SSD_GROUPS // ng, nt),
        in_specs=[rev(FW_B), rev(768), rev(SSD_GW), rev(SSD_GW), rev(SSD_GW), rev(SSD_GW),
                  pl.BlockSpec((nch, SSD_STATE, ng * SSD_GW), lambda g, i: (nt - 1 - i, 0, g)),
                  gvec(128), gvec(512), gvec(512),
                  pl.BlockSpec((4, ng * 768), lambda g, i: (0, g)), gvec(512)],
        out_specs=[rev(BW_B),
                   pl.BlockSpec((ng, 4, 8, 512), lambda g, i: (g, 0, 0, 0)),
                   pl.BlockSpec((ng, 5, 8, 768), lambda g, i: (g, 0, 0, 0))],
        out_shape=[jax.ShapeDtypeStruct((s, SSD_GROUPS * BW_B), MXU_DTYPE),
                   jax.ShapeDtypeStruct((SSD_GROUPS, 4, 8, 512), F32),
                   jax.ShapeDtypeStruct((SSD_GROUPS, 5, 8, 768), F32)],
        scratch_shapes=[pltpu.VMEM((ng, SSD_STATE, SSD_GW), F32), pltpu.VMEM((8, ng * 768), F32),
                        pltpu.VMEM((t, ng * SSD_GW), F32), pltpu.VMEM((t, ng * SSD_GW), F32)],
        compiler_params=_cparams(("parallel", "arbitrary"), ng * (2 * t * FW_B * 4 + 18 * t * SSD_GW * 4) + 16 * 1024 * 1024),
    )(proj_b, pre_all, dt_all, acs_all, dyb, y, hprev, dtb, alog, dsk, cw, nw)


def _rows_call(body, ins, outs, tr, name):
    r = ins[0].shape[0]
    spec = lambda a: pl.BlockSpec((tr, a.shape[1]), lambda i: (i, 0))
    est = 2 * tr * sum(a.shape[1] * jnp.dtype(a.dtype).itemsize for a in list(ins) + list(outs))
    return pl.pallas_call(
        body, name=name, grid=(r // tr,),
        in_specs=[spec(a) for a in ins], out_specs=[spec(o) for o in outs], out_shape=list(outs),
        compiler_params=_cparams(("parallel",), est),
    )(*ins)


def _add_pair(a, b, tr, name):
    def body(a_ref, b_ref, o_ref):
        o_ref[...] = a_ref[...] + b_ref[...]

    return _rows_call(body, [a, b], [jax.ShapeDtypeStruct(a.shape, F32)], tr, name)[0]


def _rs_add(p, sib, place, tr, name):
    _, r, c = p.shape
    half = r // 2
    nb = half // tr

    def body(pl_ref, p_ref, s_ref, b_ref, own_ref):
        v = p_ref[0] + s_ref[0]
        b_ref[0] = v.astype(jnp.bfloat16)

        @pl.when(pl.program_id(1) == pl_ref[0])
        def _():
            own_ref[...] = v

    return pl.pallas_call(
        body, name=name,
        grid_spec=pltpu.PrefetchScalarGridSpec(
            num_scalar_prefetch=1, grid=(nb, 4),
            in_specs=[pl.BlockSpec((1, tr, c), lambda i, k, pr: (k, pr[1] * nb + i, 0)),
                      pl.BlockSpec((1, tr, c), lambda i, k, pr: (k, i, 0))],
            out_specs=[pl.BlockSpec((1, tr, c), lambda i, k, pr: (k, i, 0)),
                       pl.BlockSpec((tr, c), lambda i, k, pr: (i, 0))]),
        out_shape=[jax.ShapeDtypeStruct((4, half, c), jnp.bfloat16), jax.ShapeDtypeStruct((half, c), F32)],
        compiler_params=_cparams(("parallel", "arbitrary"), 2 * tr * c * 14),
    )(place, p, sib)


WIN_STEP = 3840
WIN_W = 3968


def _rs_add_windows(dw, sib, place, tr, name):
    r = dw.shape[0]
    half = r // 2
    nb = half // tr
    tail = WIN_W - WIN_STEP

    def body(pl_ref, pm_ref, pt_ref, s_ref, b_ref, own_ref):
        vm = pm_ref[...].astype(F32) + s_ref[0, :, 0:WIN_STEP].astype(F32)
        vt = pt_ref[...].astype(F32) + s_ref[0, :, WIN_STEP:WIN_W].astype(F32)
        b_ref[0, :, 0:WIN_STEP] = vm.astype(jnp.bfloat16)
        b_ref[0, :, WIN_STEP:WIN_W] = vt.astype(jnp.bfloat16)

        @pl.when(pl.program_id(1) == pl_ref[0])
        def _():
            own_ref[:, 0:WIN_STEP] = vm
            own_ref[:, WIN_STEP:WIN_W] = vt

    return pl.pallas_call(
        body, name=name,
        grid_spec=pltpu.PrefetchScalarGridSpec(
            num_scalar_prefetch=1, grid=(nb, 4),
            in_specs=[pl.BlockSpec((tr, WIN_STEP), lambda i, k, pr: (pr[1] * nb + i, k)),
                      pl.BlockSpec((tr, tail), lambda i, k, pr: (pr[1] * nb + i, (WIN_STEP // tail) * (k + 1))),
                      pl.BlockSpec((1, tr, WIN_W), lambda i, k, pr: (k, i, 0))],
            out_specs=[pl.BlockSpec((1, tr, WIN_W), lambda i, k, pr: (k, i, 0)),
                       pl.BlockSpec((tr, WIN_W), lambda i, k, pr: (i, 0))]),
        out_shape=[jax.ShapeDtypeStruct((4, half, WIN_W), jnp.bfloat16), jax.ShapeDtypeStruct((half, WIN_W), F32)],
        compiler_params=_cparams(("parallel", "arbitrary"), 2 * tr * WIN_W * 14),
    )(place, dw, dw, sib)


def _sum_own_recv(own, recv, tr, name):
    r, c = own.shape

    def body(o_ref, r_ref, out_ref):
        v = o_ref[...]
        for j in range(3):
            v = v + r_ref[j].astype(F32)
        out_ref[...] = v

    return pl.pallas_call(
        body, name=name, grid=(r // tr,),
        in_specs=[pl.BlockSpec((tr, c), lambda i: (i, 0)), pl.BlockSpec((3, tr, c), lambda i: (0, i, 0))],
        out_specs=pl.BlockSpec((tr, c), lambda i: (i, 0)),
        out_shape=jax.ShapeDtypeStruct((r, c), F32),
        compiler_params=_cparams(("parallel",), 2 * tr * c * 14),
    )(own, recv)


def _sum_slots(stack, name):
    n, r, w = stack.shape

    def body(a_ref, out_ref):
        v = a_ref[0]
        for k in range(1, n):
            v = v + a_ref[k]
        out_ref[...] = v

    return pl.pallas_call(
        body, name=name, grid=(1,),
        in_specs=[pl.BlockSpec((n, r, w), lambda i: (0, 0, 0))],
        out_specs=pl.BlockSpec((r, w), lambda i: (0, 0)),
        out_shape=jax.ShapeDtypeStruct((r, w), F32),
        compiler_params=_cparams(("arbitrary",), 2 * (n + 1) * r * w * 4),
    )(stack)


def _adamw(w, g, m, v, tr, name):
    def body(w_ref, g_ref, m_ref, v_ref, d_ref, nm_ref, nv_ref):
        d_ref[...], nm_ref[...], nv_ref[...] = _adam_math(w_ref[...], g_ref[...], m_ref[...], v_ref[...])

    o = jax.ShapeDtypeStruct(w.shape, F32)
    return _rows_call(body, [w, g, m, v], [o, o, o], tr, name)


def _adam_math(w, g, m, v):
    nm = ADAM_B1 * m + (1.0 - ADAM_B1) * g
    nv = ADAM_B2 * v + (1.0 - ADAM_B2) * (g * g)
    m_hat = nm / (1.0 - ADAM_B1 ** ADAM_STEP)
    v_hat = nv / (1.0 - ADAM_B2 ** ADAM_STEP)
    return -ADAM_LR * (m_hat / (jnp.sqrt(v_hat) + ADAM_EPS) + ADAM_WD * w), nm, nv


def _adamw_halves(w, g_own, g_sib, m, v, place, tr, name):
    r, c = w.shape

    def body(pl_ref, w_ref, go_ref, gs_ref, m_ref, v_ref, g_ref, d_ref, nm_ref, nv_ref):
        first = pl_ref[1] == 0
        own, sib = go_ref[...], gs_ref[...]
        g = jnp.concatenate([jnp.where(first, own, sib), jnp.where(first, sib, own)], axis=1)
        g_ref[...] = g
        d_ref[...], nm_ref[...], nv_ref[...] = _adam_math(w_ref[...], g, m_ref[...], v_ref[...])

    full = pl.BlockSpec((tr, c), lambda i, pr: (i, 0))
    half = pl.BlockSpec((tr, c // 2), lambda i, pr: (i, 0))
    o = jax.ShapeDtypeStruct((r, c), F32)
    return pl.pallas_call(
        body, name=name,
        grid_spec=pltpu.PrefetchScalarGridSpec(num_scalar_prefetch=1, grid=(r // tr,),
                                               in_specs=[full, half, half, full, full], out_specs=[full] * 4),
        out_shape=[o] * 4,
        compiler_params=_cparams(("parallel",), 2 * tr * c * 4 * 8),
    )(place, w, g_own, g_sib, m, v)


ANY = pl.BlockSpec(memory_space=pl.ANY)


def _place():
    x, y, c = lax.axis_index("x"), lax.axis_index("y"), lax.axis_index("c")
    others = [(1 - x, y), (x, 1 - y), (1 - x, 1 - y)]
    return x, y, c, 2 * x + y, others


def _remote(src, dst, send, recv, k, to):
    return pltpu.make_async_remote_copy(src_ref=src, dst_ref=dst, send_sem=send.at[k], recv_sem=recv.at[k],
                                        device_id=to, device_id_type=MESH)


def _norm_gather_call(x, norm_w, win_b, wout_b, cw8, tm):
    s = x.shape[0]
    ni = s // tm
    h_in, h_out = win_b.shape[0] // 2, wout_b.shape[0] // 2
    q_in = h_in // 2

    def body(x_ref, w_ref, win, wout, cw, xn_ref, xnt_ref, g_in, g_out, g_cw, send, recv):
        i = pl.program_id(0)

        def direct():
            xx, yy, c, me, others = _place()
            mi, mo = pl.ds(c * h_in, h_in), pl.ds(c * h_out, h_out)
            cps = [_remote(win.at[mi], g_in.at[me, mi], send, recv, j, (*others[j], c)) for j in range(2)]
            cps += [_remote(wout.at[mo], g_out.at[me, mo], send, recv, 7 + j, (*chip, c)) for j, chip in enumerate(others)]
            cps += [_remote(cw, g_cw.at[me], send, recv, 13 + j, (*chip, c)) for j, chip in enumerate(others)]
            return cps

        @pl.when(i == 0)
        def _():
            for cp in direct():
                cp.start()

        xv = x_ref[...]
        r = lax.rsqrt(jnp.mean(xv * xv, axis=-1, keepdims=True) + EPS)
        xn = xv * r * w_ref[...]
        xn_ref[...] = _c(xn)
        xnt_ref[...] = _c(xn.T)

        @pl.when(i == ni - 1)
        def _():
            xx, yy, c, me, others = _place()
            sib = (xx, yy, 1 - c)
            kx, ky, kd = (2 * chip[0] + chip[1] for chip in others)
            mi, ti = pl.ds(c * h_in, h_in), pl.ds((1 - c) * h_in, h_in)
            quarter = [pl.ds(c * h_in, q_in), pl.ds(c * h_in + q_in, q_in)]
            started = []

            def go(cp):
                cp.start()
                started.append(cp)

            _remote(g_in.at[kx, mi], g_in.at[kx, mi], send, recv, 0, (*others[0], c)).wait_recv()
            go(_remote(g_in.at[kx, quarter[0]], g_in.at[kx, quarter[0]], send, recv, 2, (*others[1], c)))
            go(_remote(g_in.at[kx, mi], g_in.at[kx, mi], send, recv, 4, sib))
            _remote(g_in.at[ky, mi], g_in.at[ky, mi], send, recv, 1, (*others[1], c)).wait_recv()
            go(_remote(g_in.at[ky, quarter[1]], g_in.at[ky, quarter[1]], send, recv, 3, (*others[0], c)))
            go(_remote(g_in.at[ky, mi], g_in.at[ky, mi], send, recv, 5, sib))
            mo, to = pl.ds(c * h_out, h_out), pl.ds((1 - c) * h_out, h_out)
            for j, chip in enumerate(others):
                kj = 2 * chip[0] + chip[1]
                _remote(g_out.at[kj, mo], g_out.at[kj, mo], send, recv, 7 + j, (*chip, c)).wait_recv()
                go(_remote(g_out.at[kj, mo], g_out.at[kj, mo], send, recv, 10 + j, sib))
            _remote(g_in.at[kd, quarter[0]], g_in.at[kd, quarter[0]], send, recv, 2, (*others[1], c)).wait_recv()
            _remote(g_in.at[kd, quarter[1]], g_in.at[kd, quarter[1]], send, recv, 3, (*others[0], c)).wait_recv()
            go(_remote(g_in.at[kd, mi], g_in.at[kd, mi], send, recv, 6, sib))
            for k_src, sem in ((kx, 4), (ky, 5), (kd, 6)):
                _remote(g_in.at[k_src, ti], g_in.at[k_src, ti], send, recv, sem, sib).wait_recv()
            for j, chip in enumerate(others):
                kj = 2 * chip[0] + chip[1]
                _remote(g_out.at[kj, to], g_out.at[kj, to], send, recv, 10 + j, sib).wait_recv()
                _remote(cw, g_cw.at[kj], send, recv, 13 + j, (*chip, c)).wait_recv()
            for cp in direct() + started:
                cp.wait_send()

    outs = [jax.ShapeDtypeStruct((s, D_MODEL), MXU_DTYPE), jax.ShapeDtypeStruct((D_MODEL, s), MXU_DTYPE)]
    outs += [jax.ShapeDtypeStruct((4,) + a.shape, a.dtype) for a in (win_b, wout_b, cw8)]
    return pl.pallas_call(
        body, name="rmsnorm_gather_weights",
        grid=(ni,),
        in_specs=[pl.BlockSpec((tm, D_MODEL), lambda i: (i, 0)), pl.BlockSpec((1, D_MODEL), lambda i: (0, 0)), ANY, ANY, ANY],
        out_specs=[pl.BlockSpec((tm, D_MODEL), lambda i: (i, 0)), pl.BlockSpec((D_MODEL, tm), lambda i: (0, i)), ANY, ANY, ANY],
        out_shape=outs,
        scratch_shapes=[pltpu.SemaphoreType.DMA((16,)), pltpu.SemaphoreType.DMA((16,))],
        compiler_params=_cparams(("arbitrary",), 2 * tm * D_MODEL * 12),
    )(x, norm_w, win_b, wout_b, cw8)


def _dw_out_rs_call(a, b, dw, *, tn, tk):
    m, k = a.shape
    n = b.shape[1]
    nj, nk = n // tn, k // tk
    half = dw.shape[0] // 2

    def body(a_ref, b_ref, pin, o_ref, sib_in, send, recv):
        j, kk = pl.program_id(0), pl.program_id(1)

        def copies():
            x, y, c, me, others = _place()
            rows = pl.ds((1 - c) * half, half)
            return [_remote(pin.at[rows, pl.ds(WIN_STEP * w, WIN_W)], sib_in.at[w], send, recv, w, (x, y, 1 - c)) for w in range(4)]

        @pl.when((j == 0) & (kk == 0))
        def _():
            for cp in copies():
                cp.start()

        @pl.when(kk == 0)
        def _():
            o_ref[...] = jnp.zeros_like(o_ref)

        o_ref[...] += _dot(a_ref[...], b_ref[...])

        @pl.when((j == nj - 1) & (kk == nk - 1))
        def _():
            cps = copies()
            for cp in cps:
                cp.wait_recv()
            for cp in cps:
                cp.wait_send()

    isz = jnp.dtype(a.dtype).itemsize
    est = 2 * (m * tk + tk * tn) * isz + 2 * m * tn * 4
    outs = [jax.ShapeDtypeStruct((m, n), F32), jax.ShapeDtypeStruct((4, half, WIN_W), dw.dtype)]
    return pl.pallas_call(
        body, name="dw_out_rs_sibling",
        grid=(nj, nk),
        in_specs=[pl.BlockSpec((m, tk), lambda j, kk: (0, kk)), pl.BlockSpec((tk, tn), lambda j, kk: (kk, j)), ANY],
        out_specs=[pl.BlockSpec((m, tn), lambda j, kk: (0, j)), ANY],
        out_shape=outs,
        scratch_shapes=[pltpu.SemaphoreType.DMA((4,)), pltpu.SemaphoreType.DMA((4,))],
        compiler_params=_cparams(("arbitrary", "arbitrary"), est),
    )(a, b, dw)


def _rs_sibling_call(p_out, vsmall):
    def body(pout, vs, sib_out, sib_v, send, recv):
        x, y, c, me, others = _place()
        sib = (x, y, 1 - c)
        half = pout.shape[1] // 2
        cps = [_remote(pout.at[:, pl.ds((1 - c) * half, half)], sib_out, send, recv, 0, sib),
               _remote(vs, sib_v, send, recv, 1, sib)]
        for cp in cps:
            cp.start()
        for cp in cps:
            cp.wait_recv()
        for cp in cps:
            cp.wait_send()

    outs = [jax.ShapeDtypeStruct((4, p_out.shape[1] // 2, p_out.shape[2]), p_out.dtype),
            jax.ShapeDtypeStruct(vsmall.shape, vsmall.dtype)]
    return pl.pallas_call(
        body, name="rs_sibling",
        in_specs=[ANY] * 2, out_specs=[ANY] * 2, out_shape=outs,
        scratch_shapes=[pltpu.SemaphoreType.DMA((2,)), pltpu.SemaphoreType.DMA((2,))],
    )(p_out, vsmall)


def _rs_join_call(f_in, f_out, nw8):
    def body(fin, fout, nw, sib_in, full_out, all_nw, send, recv):
        x, y, c, me, others = _place()
        sib = (x, y, 1 - c)
        half = fout.shape[0]
        cps = [_remote(fin, sib_in, send, recv, 0, sib),
               _remote(fout, full_out.at[pl.ds(c * half, half)], send, recv, 1, sib)]
        mine = 4 * x + 2 * y + c
        peers = []
        for r in range(1, 8):
            px, py, pc = (1 - x if r & 4 else x), (1 - y if r & 2 else y), (1 - c if r & 1 else c)
            peers.append((r, (px, py, pc), 4 * px + 2 * py + pc))
            cps.append(_remote(nw, all_nw.at[mine], send, recv, 1 + r, (px, py, pc)))
        for cp in cps:
            cp.start()
        cps[0].wait_recv()
        _remote(fout, full_out.at[pl.ds((1 - c) * half, half)], send, recv, 1, sib).wait_recv()
        for r, peer, idx in peers:
            _remote(nw, all_nw.at[idx], send, recv, 1 + r, peer).wait_recv()
        for cp in cps:
            cp.wait_send()

    outs = [jax.ShapeDtypeStruct(f_in.shape, F32), jax.ShapeDtypeStruct((2 * f_out.shape[0], f_out.shape[1]), F32),
            jax.ShapeDtypeStruct((8,) + nw8.shape, F32)]
    return pl.pallas_call(
        body, name="rs_join",
        in_specs=[ANY] * 3, out_specs=[ANY] * 3, out_shape=outs,
        scratch_shapes=[pltpu.SemaphoreType.DMA((9,)), pltpu.SemaphoreType.DMA((9,))],
    )(f_in, f_out, nw8)


def _pack(arrs):
    parts = []
    for a in arrs:
        f = a.reshape(-1).astype(F32)
        pad = (-f.shape[0]) % 1024
        parts.append(jnp.pad(f, (0, pad)).reshape(-1, 128))
    return jnp.concatenate(parts, axis=0)


def _unpack(packed, shapes):
    out, row = [], 0
    for shp in shapes:
        n = 1
        for d in shp:
            n *= d
        rows = (n + 1023) // 1024 * 8
        out.append(packed[row:row + rows].reshape(-1)[:n].reshape(shp))
        row += rows
    return out


def _expand_heads(v32):
    return jnp.repeat(v32.reshape(32), HEADDIM).reshape(1, D_MODEL)


def kernel(x, norm_w, w_in, gate_b, sgu_norm_g, sgu_norm_b, sgu_w, sgu_b, conv_w, conv_b, dt_bias, A_log, D_skip, ssd_norm_w, w_out, final_norm_w, loss_target, m_norm_w, m_w_in, m_gate_b, m_sgu_norm_g, m_sgu_norm_b, m_sgu_w, m_sgu_b, m_conv_w, m_conv_b, m_dt_bias, m_A_log, m_D_skip, m_ssd_norm_w, m_w_out, m_final_norm_w, v_norm_w, v_w_in, v_gate_b, v_sgu_norm_g, v_sgu_norm_b, v_sgu_w, v_sgu_b, v_conv_w, v_conv_b, v_dt_bias, v_A_log, v_D_skip, v_ssd_norm_w, v_w_out, v_final_norm_w):
    s = x.shape[1]
    x2 = x.reshape(s, D_MODEL)
    tgt = loss_target.reshape(s, D_MODEL)
    t_ssd, t_tok, t_out, t_row = min(T_SSD, s), min(T_TOK, s), min(T_OUT, s), min(T_ROW, s)
    tm_mm, tk_dw = min(TM_MM, s), min(TK_DW, s)
    chip = 2 * lax.axis_index("x") + lax.axis_index("y")

    cw8 = jnp.pad(conv_w[0], ((0, 4), (0, 0)))
    win_b, wout_b = _c(w_in[0]), _c(w_out[0])
    xn, xnt, g_in, g_out, g_cw = _norm_gather_call(x2, norm_w, win_b, wout_b, cw8, t_row)
    g_in = lax.dynamic_update_index_in_dim(g_in, win_b, chip, 0)
    g_out = lax.dynamic_update_index_in_dim(g_out, wout_b, chip, 0)
    g_cw = lax.dynamic_update_index_in_dim(g_cw, cw8, chip, 0)
    wt = jnp.transpose(g_in, (0, 2, 1)).reshape(IN_W, D_MODEL)
    w_out_full = g_out.reshape(D_MODEL, D_MODEL)
    conv_w_full = jnp.transpose(g_cw[:, 0:4, :], (1, 0, 2)).reshape(4, 3072)

    wt_a = jnp.concatenate([wt[0:6144], wt[11296:15392]], axis=0)
    per_group = lambda lo, n: wt[lo:lo + SSD_GROUPS * n].reshape(SSD_GROUPS, n, D_MODEL)
    wt_b = jnp.concatenate([per_group(6144, 512), per_group(8192, 512), per_group(10240, 128), per_group(10752, 128),
                            jnp.pad(per_group(11264, 8), ((0, 0), (0, 120), (0, 0)))], axis=1)
    wt_b = wt_b.reshape(SSD_GROUPS * BW_B, D_MODEL)

    def group_cols(full_xs, full_bc):
        parts = []
        for g in range(SSD_GROUPS):
            parts += [full_xs[:, 512 * g:512 * g + 512], full_bc[:, 128 * g:128 * g + 128], full_bc[:, 512 + 128 * g:512 + 128 * g + 128]]
        return jnp.concatenate(parts, axis=1)

    cw_g = group_cols(conv_w_full[:, 0:2048], conv_w_full[:, 2048:3072])
    cb_g = group_cols(conv_b[:, 0:2048], conv_b[:, 2048:3072])
    alog_e, dsk_e = _expand_heads(A_log), _expand_heads(D_skip)
    narrow = lambda v32: jnp.pad(v32.reshape(SSD_GROUPS, 8), ((0, 0), (0, 120))).reshape(1, SSD_GROUPS * 128)
    dtb_n, alog_n = narrow(dt_bias), narrow(A_log)

    pos_chunk = jnp.arange(SGU_BLOCK) // CHUNK
    smask = pos_chunk[None, :] <= pos_chunk[:, None]
    wm_f = jnp.where(smask[None], sgu_w[0], 0.0)
    wm = _c(wm_f)
    wmt = _c(jnp.transpose(wm_f, (0, 2, 1)))
    bias_full = jnp.repeat(sgu_b[0].T, D_MODEL // SGU_GROUPS, axis=1)
    fnw = final_norm_w.reshape(1, D_MODEL)

    proj_a = _mm(xn, wt_a, tm=tm_mm, tn=2048, tk=D_MODEL, name="in_proj_a", out_dtype=MXU_DTYPE, b_is_t=True)
    proj_b = _mm(xn, wt_b, tm=tm_mm, tn=BW_B, tk=D_MODEL, name="in_proj_b", b_is_t=True)
    y_ssd, y_b, hprev, pre_all, dt_all, acs_all = _ssd_fwd_call(proj_b, dtb_n, alog_n, dsk_e, cw_g, cb_g, ssd_norm_w, t_ssd, NG_SSD)
    y_a, merged, merged_t = _tok_fwd_call(proj_a, y_b, gate_b, sgu_norm_g, sgu_norm_b, wm, bias_full, t_tok)
    dh_b, dmerged, loss_t, dfw8 = _out_call(merged, x2, tgt, w_out_full, fnw, t_out)

    dproj_a, dy_b, dgb8, dgam8, dbeta8, dbfull, dws = _tok_bwd_call(
        proj_a, dmerged, y_a, y_b, gate_b, sgu_norm_g, sgu_norm_b, wm, wmt, bias_full, t_tok)
    dproj_b, a512, a768 = _ssd_bwd_call(proj_b, pre_all, dt_all, acs_all, dy_b, y_ssd, hprev, dtb_n, alog_e, dsk_e, cw_g,
                                        ssd_norm_w, t_ssd, NG_SSD)
    dw_uvz = _mm(xnt, dproj_a, tm=D_MODEL, tn=1024, tk=min(2 * tk_dw, s), name="dw_in_uvz", n=6144, out_dtype=MXU_DTYPE)
    dw_gate = _mm(xnt, dproj_a, tm=D_MODEL, tn=1024, tk=min(2 * tk_dw, s), name="dw_in_gate", col0=6, n=4096,
                  out_dtype=MXU_DTYPE)
    dw_zb, dw_xs, dw_bm, dw_cm, dw_dt = _dw_groups(xnt, dproj_b, tk=tk_dw, out_dtype=MXU_DTYPE)

    dw_dt32 = jnp.concatenate([dw_dt[:, 128 * g:128 * g + 8] for g in range(SSD_GROUPS)], axis=1)
    dw_ref = jnp.concatenate([dw_uvz, dw_zb, dw_xs, dw_bm, dw_cm, dw_dt32, dw_gate,
                              jnp.zeros((D_MODEL, 3 * WIN_STEP + WIN_W - IN_W), MXU_DTYPE)], axis=1)
    dw_out_p, sib_i = _dw_out_rs_call(merged_t, dh_b, dw_ref, tn=1024, tk=tk_dw)
    p_out = dw_out_p.reshape(4, D_MODEL // 4, D_MODEL)

    s512 = jnp.sum(a512, axis=2)
    heads = lambda v: jnp.sum(v.reshape(32, HEADDIM), axis=1).reshape(1, 32)
    d_ssd_nw = s512[:, 0].reshape(1, D_MODEL)
    d_dskip = heads(s512[:, 1].reshape(D_MODEL))
    d_alog = heads(s512[:, 2].reshape(D_MODEL)) * (1.0 / HEADDIM) * (-jnp.exp(A_log))
    d_dtb = s512[:, 3, 0:8].reshape(1, 32)
    s768 = jnp.sum(a768, axis=2)
    ungroup = lambda v: jnp.concatenate([v[g, :, 0:512] for g in range(4)] + [v[g, :, 512:640] for g in range(4)]
                                        + [v[g, :, 640:768] for g in range(4)], axis=1)
    d_cw = ungroup(s768[:, 0:4])
    d_cb = ungroup(s768[:, 4:5])
    d_sgu_b = jnp.sum(dbfull.reshape(128, SGU_GROUPS, 128), axis=2).T.reshape(1, SGU_GROUPS, 128)
    d_sgu_w = jnp.where(smask[None], dws, 0.0).reshape(1, SGU_GROUPS, 128, 128)
    fold = lambda a8: jnp.sum(a8, axis=0, keepdims=True)
    small_local = [fold(dgb8), fold(dgam8), fold(dbeta8), d_sgu_w, d_sgu_b, d_cw, d_cb,
                   d_dtb, d_alog, d_dskip, d_ssd_nw, fold(dfw8).reshape(D_MODEL), jnp.sum(loss_t[:, 0, 0]).reshape(1)]
    small_shapes = [a.shape for a in small_local]
    v_local = _pack(small_local)

    core = lax.axis_index("c")
    place = jnp.stack([chip, core]).astype(jnp.int32)
    hr_o = D_MODEL // 8
    sib_o, sib_v = _rs_sibling_call(p_out, v_local)
    s1b_i, o_i = _rs_add_windows(dw_ref, sib_i, place, 256, "rs_add_in")
    s1b_o, o_o = _rs_add(p_out, sib_o, place, 256, "rs_add_out")
    chip_v = _add_pair(v_local, sib_v, v_local.shape[0], "ar_add_small")
    dxn, r_i, r_o, abs_v = _dx_rs_call(dproj_a, wt_a, dproj_b, wt_b, s1b_i, s1b_o, chip_v, tm=tm_mm)
    grad_x, dnw8 = _gradx_call(x2, dxn, dh_b, norm_w, t_row)
    abs_v = lax.dynamic_update_index_in_dim(abs_v, chip_v, chip, 0)
    f_i = _sum_own_recv(o_i, r_i, 256, "rs_sum_in")
    f_o = _sum_own_recv(o_o, r_o, 256, "rs_sum_out")
    sib_f_i, g_w_out, all_nw = _rs_join_call(f_i, f_o, dnw8)
    g_w_out = lax.dynamic_update_slice_in_dim(g_w_out, f_o, core * hr_o, axis=0)
    all_nw = lax.dynamic_update_index_in_dim(all_nw, dnw8, 2 * chip + core, 0)
    g_nw = fold(_sum_slots(all_nw, "ar_sum_norm_w"))
    total_v = _sum_slots(abs_v, "ar_sum_small")
    (g_gb, g_gam, g_beta, g_sw, g_sb, g_cw_full, g_cb, g_dtb, g_alog, g_dsk, g_snw, g_fnw, loss1) = _unpack(total_v, small_shapes)
    g_cw_shard = lax.dynamic_slice(g_cw_full, (0, chip * 768), (4, 768)).reshape(1, 4, 768)
    loss = loss1.reshape(())

    shard_t = lambda win: lax.dynamic_slice_in_dim(win, 8 * chip, SHARD_W, axis=1).T
    g_w_in, d_win, nm_win, nv_win = (a.T for a in _adamw_halves(w_in[0].T, shard_t(f_i), shard_t(sib_f_i), m_w_in[0].T,
                                                                v_w_in[0].T, place, 296, "adamw_w_in"))
    d_wout, nm_wout, nv_wout = _adamw(w_out[0], g_w_out, m_w_out[0], v_w_out[0], 128, "adamw_w_out")
    small_w = [norm_w, gate_b, sgu_norm_g, sgu_norm_b, sgu_w, sgu_b, conv_w, conv_b, dt_bias, A_log, D_skip, ssd_norm_w, final_norm_w]
    small_m = [m_norm_w, m_gate_b, m_sgu_norm_g, m_sgu_norm_b, m_sgu_w, m_sgu_b, m_conv_w, m_conv_b, m_dt_bias, m_A_log, m_D_skip, m_ssd_norm_w, m_final_norm_w]
    small_v = [v_norm_w, v_gate_b, v_sgu_norm_g, v_sgu_norm_b, v_sgu_w, v_sgu_b, v_conv_w, v_conv_b, v_dt_bias, v_A_log, v_D_skip, v_ssd_norm_w, v_final_norm_w]
    small_g = [g_nw, g_gb, g_gam, g_beta, g_sw, g_sb, g_cw_shard, g_cb, g_dtb, g_alog, g_dsk, g_snw, g_fnw]
    shapes_w = [a.shape for a in small_w]
    small_g = [a.reshape(shp) for a, shp in zip(small_g, shapes_w)]
    pw = _pack(small_w)
    pd, pm, pv = _adamw(pw, _pack(small_g), _pack(small_m), _pack(small_v), pw.shape[0], "adamw_small")
    d_small, nm_small, nv_small = _unpack(pd, shapes_w), _unpack(pm, shapes_w), _unpack(pv, shapes_w)

    def with_big(small, win, wout):
        o = list(small)
        return o[0:1] + [win.reshape(1, D_MODEL, SHARD_W)] + o[1:12] + [wout.reshape(1, D_MODEL // 4, D_MODEL)] + o[12:13]

    grads = with_big(small_g, g_w_in, g_w_out)
    deltas = with_big(d_small, d_win, d_wout)
    new_m = with_big(nm_small, nm_win, nm_wout)
    new_v = with_big(nv_small, nv_win, nv_wout)
    return (loss, grad_x.reshape(1, s, D_MODEL), *grads, *deltas, *new_m, *new_v)
```

```python
import functools

import jax
import jax.numpy as jnp
from jax import lax
from jax.experimental import pallas as pl
from jax.experimental.pallas import tpu as pltpu

F32 = jnp.float32
MXU_DTYPE = jnp.bfloat16

D_MODEL = 2048
EPS = 1e-5
CHUNK = 64
SGU_BLOCK = 128
SGU_GROUPS = 16
SSD_GROUPS = 4
SSD_GW = 512
SSD_STATE = 128
HEADDIM = 64
IN_W = 15392
SHARD_W = IN_W // 4
BW_B = 1408
FW_B = BW_B
XBC_O, DT_O = 512, 1280
NA = 10240

ADAM_LR = 0.001
ADAM_B1 = 0.9
ADAM_B2 = 0.999
ADAM_EPS = 1e-08
ADAM_WD = 0.01
ADAM_STEP = 10

T_SSD = 256
NG_SSD = 4
T_TOK = 128
T_OUT = 512
T_ROW = 512
TM_MM = 1024
TK_DW = 1024
VMEM_CAP = 60 * 1024 * 1024
MESH = pl.DeviceIdType.MESH


def _cparams(sem, est_bytes):
    lim = int(min(VMEM_CAP, max(32 * 1024 * 1024, est_bytes + 12 * 1024 * 1024)))
    return pltpu.CompilerParams(dimension_semantics=sem, vmem_limit_bytes=lim)


def _c(x):
    return x.astype(MXU_DTYPE)


def _dot(a, b):
    return jnp.dot(a, b, preferred_element_type=F32)


def _dot_nt(a, b):
    return lax.dot_general(a, b, (((1,), (1,)), ((), ())), preferred_element_type=F32)


def _dot_tn(a, b):
    return lax.dot_general(a, b, (((0,), (0,)), ((), ())), preferred_element_type=F32)


def _split(x, n):
    parts, r = [], x
    for _ in range(n):
        p = _c(r)
        parts.append(p)
        r = r - p.astype(F32)
    return parts


def _dot01_l(m01, x, n):
    acc = None
    for p in _split(x, n):
        t = _dot(m01, p)
        acc = t if acc is None else acc + t
    return acc


def _dot01_r(x, m01, n):
    acc = None
    for p in _split(x, n):
        t = _dot(p, m01)
        acc = t if acc is None else acc + t
    return acc


def _sigmoid(x):
    return 0.5 * jnp.tanh(0.5 * x) + 0.5


def _fold8(x):
    r, w = x.shape
    return jnp.sum(x.reshape(r // 8, 8, w), axis=0)


def _iota(shape, dim):
    return lax.broadcasted_iota(jnp.int32, shape, dim)


def _ssd_masks():
    l64 = _iota((CHUNK, SSD_GW), 0)
    s64 = jnp.bitwise_and(_iota((CHUNK, SSD_GW), 1), CHUNK - 1)
    diag = l64 == s64
    causal = l64 >= s64
    row_last = l64 == CHUNK - 1
    r4 = lax.shift_right_logical(_iota((256, 256), 0), 6)
    c4 = lax.shift_right_logical(_iota((256, 256), 1), 6)
    mask4 = r4 == c4
    return diag, causal, row_last, mask4


def _cumsum_mats(t):
    r, c = _iota((t, t), 0), _iota((t, t), 1)
    same = lax.shift_right_logical(r, 6) == lax.shift_right_logical(c, 6)
    tri = _c(jnp.where(same, jnp.where(c <= r, 1.0, 0.0), 0.0))
    trit = _c(jnp.where(same, jnp.where(c >= r, 1.0, 0.0), 0.0))
    return tri, trit


def _head_expand_mat():
    return _c(jnp.where(_iota((128, SSD_GW), 0) == lax.shift_right_logical(_iota((128, SSD_GW), 1), 6), 1.0, 0.0))


def _ssd_common(xs, bm, cm, dt, acs, masks):
    diag, causal, row_last, mask4 = masks
    row_e = jnp.sum(jnp.where(diag, acs, 0.0), axis=0, keepdims=True)
    seg = acs - row_e
    lm = jnp.exp(jnp.where(causal, seg, -1e30))
    bb, cb = _c(bm), _c(cm)
    brep = jnp.concatenate([bb] * 8, axis=0)
    cbrep = _dot_nt(cb, brep)
    m = cbrep * lm
    xdt = xs * dt
    acs_last = jnp.sum(jnp.where(row_last, acs, 0.0), axis=0, keepdims=True)
    dec = jnp.exp(acs_last - acs)
    eacs = jnp.exp(acs)
    cd = jnp.exp(acs_last)
    return dict(lm=lm, bb=bb, cb=cb, brep=brep, m=m, xdt=xdt, dec=dec, eacs=eacs, cd=cd)


def _blockdiag4(xb, mask4):
    return jnp.where(mask4, jnp.concatenate([xb] * 4, axis=0), jnp.zeros((), xb.dtype))


def _ssd_chunk_fwd(xs, bm, cm, dt, acs, d_skip, ht, masks):
    q = _ssd_common(xs, bm, cm, dt, acs, masks)
    mask4 = masks[3]
    mb, xdtb = _c(q["m"]), _c(q["xdt"])
    yd = []
    for blk in range(2):
        sl = slice(256 * blk, 256 * blk + 256)
        yd.append(_dot(mb[:, sl], _blockdiag4(xdtb[:, sl], mask4)))
    y_diag = jnp.concatenate(yd, axis=1)
    p = _dot(q["cb"], _c(ht))
    y = y_diag + p * q["eacs"] + xs * d_skip
    st = _dot_tn(q["bb"], _c(q["xdt"] * q["dec"]))
    return y, ht * q["cd"] + st


def _ssd_chunk_bwd(xs, bm, cm, dt, acs, d_skip, hprev, dht, dy, masks):
    diag, causal, row_last, mask4 = masks
    q = _ssd_common(xs, bm, cm, dt, acs, masks)
    lm, bb, cb, brep, m, xdt, dec, eacs, cd = (q[k] for k in ("lm", "bb", "cb", "brep", "m", "xdt", "dec", "eacs", "cd"))
    hb = _c(hprev)
    yoff = _dot(cb, hb) * eacs
    dyb = _c(dy)
    dpb = _c(dy * eacs)
    d_c = _dot_nt(dpb, hb)
    dh_y = _dot_tn(cb, dpb)
    mb, xdtb = _c(m), _c(xdt)
    dm_parts, dxdt_parts = [], []
    for blk in range(2):
        sl = slice(256 * blk, 256 * blk + 256)
        bd = _blockdiag4(xdtb[:, sl], mask4)
        dm_parts.append(_dot_nt(dyb[:, sl], bd))
        dxf = jnp.where(mask4, _dot_tn(mb[:, sl], dyb[:, sl]), 0.0)
        dxdt_parts.append(dxf[0:64] + dxf[64:128] + dxf[128:192] + dxf[192:256])
    dm = jnp.concatenate(dm_parts, axis=1)
    dxdt = jnp.concatenate(dxdt_parts, axis=1)
    dcbb = _c(dm * lm)
    g = dm * m
    d_c = d_c + _dot(dcbb, brep)
    dbrep = _dot_tn(dcbb, cb)
    d_b = dbrep[0:64]
    for r in range(1, 8):
        d_b = d_b + dbrep[64 * r:64 * r + 64]
    dhtb = _c(dht)
    dxd = _dot(bb, dhtb)
    xd = xdt * dec
    dxdt = dxdt + dxd * dec
    tq = dxd * xd
    d_b = d_b + _dot_nt(_c(xd), dhtb)
    dcd = jnp.sum(dht * hprev, axis=0, keepdims=True)
    col_g = jnp.sum(g, axis=0, keepdims=True)
    last = jnp.sum(tq, axis=0, keepdims=True) + dcd * cd
    qq = g - jnp.where(diag, col_g, 0.0) + dy * yoff - tq + jnp.where(row_last, last, 0.0)
    dxs = dxdt * dt + dy * d_skip
    return dxs, d_b, d_c, dht * cd + dh_y, dy * xs, qq, dxdt * xs


def _ssd_finish_dt(qq, p1, dt, a_neg, trit, mask4):
    bd4 = _c(jnp.where(mask4, 1.0, 0.0))
    dacs = jnp.concatenate([_dot01_r(qq[:, 256 * b:256 * b + 256], bd4, 2) for b in range(2)], axis=1)
    da = _dot01_l(trit, dacs, 2)
    return p1 + da * (a_neg * (1.0 / HEADDIM)), da * dt


def _softplus(x):
    return jnp.maximum(x, 0.0) + jnp.log(1.0 + jnp.exp(-jnp.abs(x)))


def _conv_taps(xpad, t):
    taps = []
    for k in range(4):
        sh = 3 - k
        v = xpad if sh == 0 else pltpu.roll(xpad, sh, 0)
        taps.append(v[8:8 + t])
    return taps


def _mm(a, b, *, tm, tn, tk, name, out_dtype=F32, col0=0, n=None, b_is_t=False):
    m, k = a.shape
    n = b.shape[0 if b_is_t else 1] if n is None else n
    nk = k // tk
    assert m % tm == 0 and n % tn == 0 and k % tk == 0, (a.shape, b.shape, tm, tn, tk)
    dot = _dot_nt if b_is_t else _dot
    via_acc = nk > 1 and out_dtype != F32

    def body(a_ref, b_ref, o_ref, *acc):
        if nk == 1:
            o_ref[...] = dot(a_ref[...], b_ref[...]).astype(out_dtype)
            return
        acc_ref = acc[0] if via_acc else o_ref

        @pl.when(pl.program_id(2) == 0)
        def _():
            acc_ref[...] = jnp.zeros_like(acc_ref)

        acc_ref[...] += dot(a_ref[...], b_ref[...])
        if via_acc:
            @pl.when(pl.program_id(2) == nk - 1)
            def _():
                o_ref[...] = acc_ref[...].astype(out_dtype)

    isz = jnp.dtype(a.dtype).itemsize
    est = 2 * (tm * tk + tk * tn) * isz + 3 * tm * tn * 4
    return pl.pallas_call(
        body, name=name,
        grid=(m // tm, n // tn, nk),
        in_specs=[pl.BlockSpec((tm, tk), lambda i, j, kk: (i, kk)),
                  pl.BlockSpec((tn, tk), lambda i, j, kk: (j + col0, kk)) if b_is_t
                  else pl.BlockSpec((tk, tn), lambda i, j, kk: (kk, j + col0))],
        out_specs=pl.BlockSpec((tm, tn), lambda i, j, kk: (i, j)),
        out_shape=jax.ShapeDtypeStruct((m, n), out_dtype),
        scratch_shapes=[pltpu.VMEM((tm, tn), F32)] if via_acc else [],
        compiler_params=_cparams(("parallel", "parallel", "arbitrary"), est),
    )(a, b)


def _dw_groups(xnt, dpb, *, tk, out_dtype):
    m, k = xnt.shape
    nk = k // tk

    def body(a_ref, b_ref, zb_ref, xs_ref, bm_ref, cm_ref, dt_ref, acc_ref):
        @pl.when(pl.program_id(1) == 0)
        def _():
            acc_ref[...] = jnp.zeros_like(acc_ref)

        acc_ref[...] += _dot(a_ref[...], b_ref[...])

        @pl.when(pl.program_id(1) == nk - 1)
        def _():
            for o_ref, lo, hi in ((zb_ref, 0, 512), (xs_ref, 512, 1024), (bm_ref, 1024, 1152), (cm_ref, 1152, 1280),
                                  (dt_ref, 1280, 1408)):
                o_ref[...] = acc_ref[:, lo:hi].astype(out_dtype)

    isz, osz = jnp.dtype(xnt.dtype).itemsize, jnp.dtype(out_dtype).itemsize
    est = 2 * (m * tk + tk * BW_B) * isz + m * BW_B * (4 + 2 * osz)
    piece = lambda w: pl.BlockSpec((m, w), lambda g, kk: (0, g))
    return pl.pallas_call(
        body, name="dw_in_b",
        grid=(SSD_GROUPS, nk),
        in_specs=[pl.BlockSpec((m, tk), lambda g, kk: (0, kk)), pl.BlockSpec((tk, BW_B), lambda g, kk: (kk, g))],
        out_specs=[piece(512), piece(512), piece(128), piece(128), piece(128)],
        out_shape=[jax.ShapeDtypeStruct((m, w), out_dtype) for w in (2048, 2048, 512, 512, 512)],
        scratch_shapes=[pltpu.VMEM((m, BW_B), F32)],
        compiler_params=_cparams(("parallel", "arbitrary"), est),
    )(xnt, dpb)


def _dx_rs_call(dpa, wta, dpb, wtb, sb_in, sb_out, chip_v, *, tm):
    s = dpa.shape[0]
    tka, tkb = 1024, BW_B
    nka, nkb = dpa.shape[1] // tka, dpb.shape[1] // tkb
    ni, nk = s // tm, nka + nkb

    def body(a_ref, wa_ref, b_ref, wb_ref, sbin, sbout, cv, o_ref, rc_in, rc_out, abs_v, send, recv):
        i, kk = pl.program_id(0), pl.program_id(1)

        def copies():
            x, y, c, me, others = _place()
            sends, recvs = [], []
            for j, chip in enumerate(others):
                kj = 2 * chip[0] + chip[1]
                to = (*chip, c)
                sends += [_remote(sbin.at[kj], rc_in.at[j], send, recv, j, to),
                          _remote(sbout.at[kj], rc_out.at[j], send, recv, 3 + j, to),
                          _remote(cv, abs_v.at[me], send, recv, 6 + j, to)]
                recvs += [sends[-3], sends[-2], _remote(cv, abs_v.at[kj], send, recv, 6 + j, to)]
            return sends, recvs

        @pl.when((i == 0) & (kk == 0))
        def _():
            for cp in copies()[0]:
                cp.start()

        @pl.when(kk == 0)
        def _():
            o_ref[...] = jnp.zeros_like(o_ref)

        @pl.when(kk < nka)
        def _():
            o_ref[...] += _dot(a_ref[...], wa_ref[...])

        @pl.when(kk >= nka)
        def _():
            o_ref[...] += _dot(b_ref[...], wb_ref[...])

        @pl.when((i == ni - 1) & (kk == nk - 1))
        def _():
            sends, recvs = copies()
            for cp in recvs:
                cp.wait_recv()
            for cp in sends:
                cp.wait_send()

    isz = jnp.dtype(dpa.dtype).itemsize
    est = 2 * isz * (tm * tka + tka * D_MODEL + tm * tkb + tkb * D_MODEL) + 2 * tm * D_MODEL * 4
    outs = [jax.ShapeDtypeStruct((s, D_MODEL), F32),
            jax.ShapeDtypeStruct((3,) + sb_in.shape[1:], sb_in.dtype), jax.ShapeDtypeStruct((3,) + sb_out.shape[1:], sb_out.dtype),
            jax.ShapeDtypeStruct((4,) + chip_v.shape, F32)]
    return pl.pallas_call(
        body, name="dx_matmul_rs_chips",
        grid=(ni, nk),
        in_specs=[
            pl.BlockSpec((tm, tka), lambda i, kk: (i, jnp.minimum(kk, nka - 1))),
            pl.BlockSpec((tka, D_MODEL), lambda i, kk: (jnp.minimum(kk, nka - 1), 0)),
            pl.BlockSpec((tm, tkb), lambda i, kk: (i, jnp.maximum(kk - nka, 0))),
            pl.BlockSpec((tkb, D_MODEL), lambda i, kk: (jnp.maximum(kk - nka, 0), 0)),
            ANY, ANY, ANY,
        ],
        out_specs=[pl.BlockSpec((tm, D_MODEL), lambda i, kk: (i, 0)), ANY, ANY, ANY],
        out_shape=outs,
        scratch_shapes=[pltpu.SemaphoreType.DMA((9,)), pltpu.SemaphoreType.DMA((9,))],
        compiler_params=_cparams(("arbitrary", "arbitrary"), est),
    )(dpa, wta, dpb, wtb, sb_in, sb_out, chip_v)


def _gradx_call(x, dxn, dh, norm_w, tm):
    s = x.shape[0]

    def body(x_ref, g_ref, dh_ref, w_ref, gx_ref, dw_ref):
        @pl.when(pl.program_id(0) == 0)
        def _():
            dw_ref[...] = jnp.zeros_like(dw_ref)

        xv, gv = x_ref[...], g_ref[...]
        r = lax.rsqrt(jnp.mean(xv * xv, axis=-1, keepdims=True) + EPS)
        gw = gv * w_ref[...]
        gx_ref[...] = r * gw - xv * (r * r * r) * jnp.mean(xv * gw, axis=-1, keepdims=True) + dh_ref[...].astype(F32)
        dw_ref[...] += _fold8(gv * (xv * r))

    row = pl.BlockSpec((tm, D_MODEL), lambda i: (i, 0))
    return pl.pallas_call(
        body, name="grad_x",
        grid=(s // tm,),
        in_specs=[row, row, row, pl.BlockSpec((1, D_MODEL), lambda i: (0, 0))],
        out_specs=[row, pl.BlockSpec((8, D_MODEL), lambda i: (0, 0))],
        out_shape=[jax.ShapeDtypeStruct((s, D_MODEL), F32), jax.ShapeDtypeStruct((8, D_MODEL), F32)],
        compiler_params=_cparams(("arbitrary",), 2 * tm * D_MODEL * 16),
    )(x, dxn, dh, norm_w)


def _layernorm_stats(v):
    mu = jnp.mean(v, axis=-1, keepdims=True)
    vc = v - mu
    var = jnp.mean(vc * vc, axis=-1, keepdims=True)
    return vc * lax.rsqrt(var + EPS), lax.rsqrt(var + EPS)


def _tok_fwd_call(proj_a, y_b, gate_b, sgu_g, sgu_beta, wm, bias_full, t):
    s = proj_a.shape[0]

    def body(pa_ref, yb_ref, gb_ref, g_ref, be_ref, wm_ref, bf_ref, ya_ref, mg_ref, mgt_ref, mix_ref):
        u = pa_ref[:, 0:2048].astype(F32)
        v = pa_ref[:, 2048:4096].astype(F32)
        za = pa_ref[:, 4096:6144].astype(F32)
        xhat, _ = _layernorm_stats(v)
        vnb = _c(xhat * g_ref[...] + be_ref[...])
        for gi in range(SGU_GROUPS):
            sl = slice(128 * gi, 128 * gi + 128)
            mix_ref[:, sl] = _dot(wm_ref[gi], vnb[:, sl])
        mixed = mix_ref[...] + bf_ref[...]
        y_a = u * mixed * (za * _sigmoid(za))
        g0 = _sigmoid(pa_ref[:, 6144:8192].astype(F32) + gb_ref[:, 0:2048])
        g1 = _sigmoid(pa_ref[:, 8192:10240].astype(F32) + gb_ref[:, 2048:4096])
        merged = g0 * y_a + g1 * yb_ref[...].astype(F32)
        ya_ref[...] = _c(y_a)
        mg_ref[...] = _c(merged)
        mgt_ref[...] = _c(merged.T)

    row = pl.BlockSpec((t, D_MODEL), lambda i: (i, 0))
    vec = lambda w: pl.BlockSpec((1, w), lambda i: (0, 0))
    return pl.pallas_call(
        body, name="tok_fwd",
        grid=(s // t,),
        in_specs=[pl.BlockSpec((t, NA), lambda i: (i, 0)), row, vec(4096), vec(2048), vec(2048),
                  pl.BlockSpec((SGU_GROUPS, 128, 128), lambda i: (0, 0, 0)), pl.BlockSpec((128, D_MODEL), lambda i: (0, 0))],
        out_specs=[row, row, pl.BlockSpec((D_MODEL, t), lambda i: (0, i))],
        out_shape=[jax.ShapeDtypeStruct((s, D_MODEL), MXU_DTYPE), jax.ShapeDtypeStruct((s, D_MODEL), MXU_DTYPE),
                   jax.ShapeDtypeStruct((D_MODEL, s), MXU_DTYPE)],
        scratch_shapes=[pltpu.VMEM((t, D_MODEL), F32)],
        compiler_params=_cparams(("parallel",), 2 * t * NA * 4 + 12 * t * D_MODEL * 4),
    )(proj_a, y_b, gate_b, sgu_g, sgu_beta, wm, bias_full)


def _tok_bwd_call(proj_a, dmerged, y_a, y_b, gate_b, sgu_g, sgu_beta, wm, wmt, bias_full, t):
    s = proj_a.shape[0]

    def body(pa_ref, dm_ref, ya_ref, yb_ref, gb_ref, g_ref, be_ref, wm_ref, wmt_ref, bf_ref,
             dpa_ref, dyb_ref, dgb_ref, dgam_ref, dbeta_ref, dbf_ref, dws_ref, mix_ref, dvn_ref):
        @pl.when(pl.program_id(0) == 0)
        def _():
            dgb_ref[...] = jnp.zeros_like(dgb_ref)
            dgam_ref[...] = jnp.zeros_like(dgam_ref)
            dbeta_ref[...] = jnp.zeros_like(dbeta_ref)
            dbf_ref[...] = jnp.zeros_like(dbf_ref)
            dws_ref[...] = jnp.zeros_like(dws_ref)

        u = pa_ref[:, 0:2048].astype(F32)
        v = pa_ref[:, 2048:4096].astype(F32)
        za = pa_ref[:, 4096:6144].astype(F32)
        xhat, rstd = _layernorm_stats(v)
        vnb = _c(xhat * g_ref[...] + be_ref[...])
        for gi in range(SGU_GROUPS):
            sl = slice(128 * gi, 128 * gi + 128)
            mix_ref[:, sl] = _dot(wm_ref[gi], vnb[:, sl])
        mixed = mix_ref[...] + bf_ref[...]
        sig = _sigmoid(za)
        sz = za * sig
        dm = dm_ref[...].astype(F32)
        y_a = ya_ref[...].astype(F32)
        g0 = _sigmoid(pa_ref[:, 6144:8192].astype(F32) + gb_ref[:, 0:2048])
        g1 = _sigmoid(pa_ref[:, 8192:10240].astype(F32) + gb_ref[:, 2048:4096])
        dgl0 = dm * y_a * g0 * (1.0 - g0)
        dgl1 = dm * yb_ref[...].astype(F32) * g1 * (1.0 - g1)
        dyb_ref[...] = _c(dm * g1)
        dya = dm * g0
        dpa_ref[:, 6144:8192] = _c(dgl0)
        dpa_ref[:, 8192:10240] = _c(dgl1)
        dgb_ref[:, 0:2048] += _fold8(dgl0)
        dgb_ref[:, 2048:4096] += _fold8(dgl1)
        dpa_ref[:, 0:2048] = _c(dya * mixed * sz)
        dpa_ref[:, 4096:6144] = _c(dya * (u * mixed) * (sig * (1.0 + za * (1.0 - sig))))
        dmixed = dya * u * sz
        dbf_ref[...] += dmixed
        dmb = _c(dmixed)
        for gi in range(SGU_GROUPS):
            sl = slice(128 * gi, 128 * gi + 128)
            dvn_ref[:, sl] = _dot(wmt_ref[gi], dmb[:, sl])
            dws_ref[gi] += _dot_nt(dmb[:, sl], vnb[:, sl])
        dvn = dvn_ref[...]
        dgam_ref[...] += _fold8(dvn * xhat)
        dbeta_ref[...] += _fold8(dvn)
        dxh = dvn * g_ref[...]
        dv = rstd * (dxh - jnp.mean(dxh, axis=-1, keepdims=True) - xhat * jnp.mean(dxh * xhat, axis=-1, keepdims=True))
        dpa_ref[:, 2048:4096] = _c(dv)

    row = pl.BlockSpec((t, D_MODEL), lambda i: (i, 0))
    vec = lambda w: pl.BlockSpec((1, w), lambda i: (0, 0))
    acc = lambda w: pl.BlockSpec((8, w), lambda i: (0, 0))
    wspec = pl.BlockSpec((SGU_GROUPS, 128, 128), lambda i: (0, 0, 0))
    return pl.pallas_call(
        body, name="tok_bwd",
        grid=(s // t,),
        in_specs=[pl.BlockSpec((t, NA), lambda i: (i, 0)), row, row, row, vec(4096), vec(2048), vec(2048),
                  wspec, wspec, pl.BlockSpec((128, D_MODEL), lambda i: (0, 0))],
        out_specs=[pl.BlockSpec((t, NA), lambda i: (i, 0)), row, acc(4096), acc(2048), acc(2048),
                   pl.BlockSpec((128, D_MODEL), lambda i: (0, 0)), wspec],
        out_shape=[jax.ShapeDtypeStruct((s, NA), MXU_DTYPE), jax.ShapeDtypeStruct((s, D_MODEL), MXU_DTYPE),
                   jax.ShapeDtypeStruct((8, 4096), F32), jax.ShapeDtypeStruct((8, 2048), F32),
                   jax.ShapeDtypeStruct((8, 2048), F32), jax.ShapeDtypeStruct((128, D_MODEL), F32),
                   jax.ShapeDtypeStruct((SGU_GROUPS, 128, 128), F32)],
        scratch_shapes=[pltpu.VMEM((t, D_MODEL), F32), pltpu.VMEM((t, D_MODEL), F32)],
        compiler_params=_cparams(("arbitrary",), 2 * t * NA * 6 + 16 * t * D_MODEL * 4),
    )(proj_a, dmerged, y_a, y_b, gate_b, sgu_g, sgu_beta, wm, wmt, bias_full)


def _out_call(merged, x, target, w_out, fnw, t):
    s = x.shape[0]
    nt = s // t

    def body(mg_ref, x_ref, t_ref, w_ref, fw_ref, dhb_ref, dmg_ref, loss_ref, dfw_ref):
        @pl.when(pl.program_id(0) == 0)
        def _():
            dfw_ref[...] = jnp.zeros_like(dfw_ref)

        h = x_ref[...] + _dot(mg_ref[...], w_ref[...])
        r = lax.rsqrt(jnp.mean(h * h, axis=-1, keepdims=True) + EPS)
        hn = h * r
        err = hn * fw_ref[...] - t_ref[...]
        loss_ref[...] = jnp.full(loss_ref.shape, 0.5 * jnp.sum(jnp.mean(err * err, axis=-1, keepdims=True)), F32)
        dy = err * (1.0 / D_MODEL)
        dfw_ref[...] += _fold8(dy * hn)
        gw = dy * fw_ref[...]
        dh = r * gw - h * (r * r * r) * jnp.mean(h * gw, axis=-1, keepdims=True)
        dhb = _c(dh)
        dhb_ref[...] = dhb
        dmg_ref[...] = _c(_dot_nt(dhb, w_ref[...]))

    row = pl.BlockSpec((t, D_MODEL), lambda i: (i, 0))
    return pl.pallas_call(
        body, name="out_proj_loss",
        grid=(nt,),
        in_specs=[row, row, row, pl.BlockSpec((D_MODEL, D_MODEL), lambda i: (0, 0), pipeline_mode=pl.Buffered(1)),
                  pl.BlockSpec((1, D_MODEL), lambda i: (0, 0))],
        out_specs=[row, row, pl.BlockSpec((1, 8, 128), lambda i: (i, 0, 0)), pl.BlockSpec((8, D_MODEL), lambda i: (0, 0))],
        out_shape=[jax.ShapeDtypeStruct((s, D_MODEL), MXU_DTYPE),
                   jax.ShapeDtypeStruct((s, D_MODEL), MXU_DTYPE), jax.ShapeDtypeStruct((nt, 8, 128), F32),
                   jax.ShapeDtypeStruct((8, D_MODEL), F32)],
        compiler_params=_cparams(("arbitrary",), D_MODEL * D_MODEL * 2 + 2 * t * D_MODEL * 16 + 5 * t * D_MODEL * 4),
    )(merged, x, target, w_out, fnw)


def _ssd_fwd_call(proj_b, dtb, alog, dsk, cw, cb, nw, t, ng):
    s = proj_b.shape[0]
    nt, nch = s // t, t // CHUNK

    def body(pb_ref, halo_ref, dtb_ref, al_ref, ds_ref, cw_ref, cb_ref, nw_ref, y_ref, yb_ref, hp_ref, pre_ref,
             dt_ref, acs_ref, ht_ref, prev_ref):
        i = pl.program_id(1)

        @pl.when(i == 0)
        def _():
            ht_ref[...] = jnp.zeros_like(ht_ref)

        for gi in range(ng):
            prev_ref[:, 768 * gi:768 * gi + 768] = jnp.where(i == 0, 0.0, halo_ref[:, FW_B * gi + XBC_O:FW_B * gi + DT_O])
        masks = _ssd_masks()
        tri, _ = _cumsum_mats(t)
        a_neg = -jnp.exp(al_ref[...])
        expand = _head_expand_mat()
        for gi in range(ng):
            fo, go, no = FW_B * gi, SSD_GW * gi, 128 * gi
            dt_n = _softplus(pb_ref[:, fo + DT_O:fo + DT_O + 128] + dtb_ref[:, no:no + 128])
            acs_n = _dot01_l(tri, dt_n * a_neg[:, no:no + 128], 3)
            dt_ref[:, go:go + 512] = _dot01_r(dt_n, expand, 2)
            acs_ref[:, go:go + 512] = _dot01_r(acs_n, expand, 2)

        def chunk(c, carry):
            rows = pl.ds(pl.multiple_of(c * CHUNK, CHUNK), CHUNK)
            for gi in range(ng):
                fo, co, go, no = FW_B * gi, 768 * gi, SSD_GW * gi, 128 * gi
                xbc = pb_ref[rows, fo + XBC_O:fo + DT_O]
                taps = _conv_taps(jnp.concatenate([prev_ref[:, co:co + 768], xbc], axis=0), CHUNK)
                prev_ref[:, co:co + 768] = xbc[CHUNK - 8:CHUNK]
                pre = cb_ref[:, co:co + 768]
                for k in range(4):
                    pre = pre + taps[k] * cw_ref[k:k + 1, co:co + 768]
                pre_ref[rows, co:co + 768] = pre
                act = pre * _sigmoid(pre)
                dt = dt_ref[rows, go:go + 512]
                acs = acs_ref[rows, go:go + 512]
                ht = ht_ref[gi]
                hp_ref[c, :, go:go + 512] = ht
                y, ht_new = _ssd_chunk_fwd(act[:, 0:512], act[:, 512:640], act[:, 640:768], dt, acs,
                                           ds_ref[:, go:go + 512], ht, masks)
                y_ref[rows, go:go + 512] = y
                ht_ref[gi] = ht_new
                zb = pb_ref[rows, fo:fo + 512]
                hh = y * (zb * _sigmoid(zb))
                rr = lax.rsqrt(jnp.mean(hh * hh, axis=-1, keepdims=True) + EPS)
                yb_ref[rows, go:go + 512] = _c(hh * rr * nw_ref[:, go:go + 512])
            return carry

        lax.fori_loop(0, nch, chunk, 0)

    gvec = lambda w: pl.BlockSpec((1, ng * w), lambda g, i: (0, g))
    return pl.pallas_call(
        body, name="ssd_fwd",
        grid=(SSD_GROUPS // ng, nt),
        in_specs=[pl.BlockSpec((t, ng * FW_B), lambda g, i: (i, g)),
                  pl.BlockSpec((8, ng * FW_B), lambda g, i: (jnp.maximum(i * (t // 8) - 1, 0), g)),
                  gvec(128), gvec(128), gvec(512),
                  pl.BlockSpec((4, ng * 768), lambda g, i: (0, g)), gvec(768), gvec(512)],
        out_specs=[pl.BlockSpec((t, ng * SSD_GW), lambda g, i: (i, g)), pl.BlockSpec((t, ng * SSD_GW), lambda g, i: (i, g)),
                   pl.BlockSpec((nch, SSD_STATE, ng * SSD_GW), lambda g, i: (i, 0, g)),
                   pl.BlockSpec((t, ng * 768), lambda g, i: (i, g)),
                   pl.BlockSpec((t, ng * SSD_GW), lambda g, i: (i, g)), pl.BlockSpec((t, ng * SSD_GW), lambda g, i: (i, g))],
        out_shape=[jax.ShapeDtypeStruct((s, D_MODEL), F32), jax.ShapeDtypeStruct((s, D_MODEL), MXU_DTYPE),
                   jax.ShapeDtypeStruct((s // CHUNK, SSD_STATE, D_MODEL), F32),
                   jax.ShapeDtypeStruct((s, SSD_GROUPS * 768), F32),
                   jax.ShapeDtypeStruct((s, D_MODEL), F32), jax.ShapeDtypeStruct((s, D_MODEL), F32)],
        scratch_shapes=[pltpu.VMEM((ng, SSD_STATE, SSD_GW), F32), pltpu.VMEM((8, ng * 768), F32)],
        compiler_params=_cparams(("parallel", "arbitrary"), ng * (2 * t * FW_B * 4 + 16 * t * SSD_GW * 4) + 16 * 1024 * 1024),
    )(proj_b, proj_b, dtb, alog, dsk, cw, cb, nw)


def _ssd_bwd_call(proj_b, pre_all, dt_all, acs_all, dyb, y, hprev, dtb, alog, dsk, cw, nw, t, ng):
    s = proj_b.shape[0]
    nt, nch = s // t, t // CHUNK

    def body(pb_ref, pre_ref, dt_ref, acs_ref, dyb_ref, y_ref, hp_ref, dtb_ref, al_ref, ds_ref, cw_ref, nw_ref,
             dpb_ref, a512_ref, a768_ref, dht_ref, nxt_ref, q_ref, p1_ref):
        i = pl.program_id(1)

        @pl.when(i == 0)
        def _():
            dht_ref[...] = jnp.zeros_like(dht_ref)
            nxt_ref[...] = jnp.zeros_like(nxt_ref)
            a512_ref[...] = jnp.zeros_like(a512_ref)
            a768_ref[...] = jnp.zeros_like(a768_ref)

        _, trit = _cumsum_mats(t)
        a_neg = -jnp.exp(al_ref[...])
        masks = _ssd_masks()

        def chunk(cc, carry):
            c = nch - 1 - cc
            rows = pl.ds(pl.multiple_of(c * CHUNK, CHUNK), CHUNK)
            for gi in range(ng):
                fo, co, go, bo = FW_B * gi, 768 * gi, SSD_GW * gi, BW_B * gi
                pre = pre_ref[rows, co:co + 768]
                sp = _sigmoid(pre)
                act = pre * sp
                zb = pb_ref[rows, fo:fo + 512]
                yv = y_ref[rows, go:go + 512]
                sgz = _sigmoid(zb)
                sz = zb * sgz
                hh = yv * sz
                rr = lax.rsqrt(jnp.mean(hh * hh, axis=-1, keepdims=True) + EPS)
                dyb = dyb_ref[rows, go:go + 512].astype(F32)
                a512_ref[gi, 0] += _fold8(dyb * (hh * rr))
                tt = dyb * nw_ref[:, go:go + 512]
                dhh = rr * tt - hh * (rr * rr * rr) * jnp.mean(hh * tt, axis=-1, keepdims=True)
                dpb_ref[rows, bo:bo + 512] = _c(dhh * yv * (sgz * (1.0 + zb * (1.0 - sgz))))
                dxs, d_b, d_c, dht_prev, dyxs, qq, p1 = _ssd_chunk_bwd(
                    act[:, 0:512], act[:, 512:640], act[:, 640:768], dt_ref[rows, go:go + 512], acs_ref[rows, go:go + 512],
                    ds_ref[:, go:go + 512], hp_ref[c, :, go:go + 512], dht_ref[gi], dhh * sz, masks)
                dht_ref[gi] = dht_prev
                q_ref[rows, go:go + 512] = qq
                p1_ref[rows, go:go + 512] = p1
                a512_ref[gi, 1] += _fold8(dyxs)
                dpre = jnp.concatenate([dxs, d_b, d_c], axis=1) * (sp * (1.0 + pre * (1.0 - sp)))
                xbc = pb_ref[rows, fo + XBC_O:fo + DT_O]
                a768_ref[gi, 4] += _fold8(dpre)
                a768_ref[gi, 3] += _fold8(dpre * xbc)
                dpad = jnp.concatenate([dpre, nxt_ref[:, co:co + 768]], axis=0)
                dx = dpre * cw_ref[3:4, co:co + 768]
                for k in range(3):
                    d_k = pltpu.roll(dpad, CHUNK + 8 - (3 - k), 0)[0:CHUNK]
                    dx = dx + d_k * cw_ref[k:k + 1, co:co + 768]
                    a768_ref[gi, k] += _fold8(d_k * xbc)
                nxt_ref[:, co:co + 768] = dpre[0:8]
                dpb_ref[rows, bo + 512:bo + 1280] = _c(dx)
            return carry

        lax.fori_loop(0, nch, chunk, 0)

        rsel = _c(jnp.where(lax.shift_right_logical(_iota((SSD_GW, 128), 0), 6) == _iota((SSD_GW, 128), 1), 1.0, 0.0))
        for gi in range(ng):
            fo, go, bo, no = FW_B * gi, SSD_GW * gi, BW_B * gi, 128 * gi
            ddt, dadt = _ssd_finish_dt(q_ref[:, go:go + 512], p1_ref[:, go:go + 512], dt_ref[:, go:go + 512],
                                       a_neg[:, go:go + 512], trit, masks[3])
            sig_n = _sigmoid(pb_ref[:, fo + DT_O:fo + DT_O + 128] + dtb_ref[:, no:no + 128])
            ddtr_n = _dot01_r(ddt, rsel, 2) * sig_n
            dpb_ref[:, bo + DT_O:bo + DT_O + 128] = _c(ddtr_n)
            a512_ref[gi, 2] += _fold8(dadt)
            a512_ref[gi, 3, :, 0:128] += _fold8(ddtr_n)

    gvec = lambda w: pl.BlockSpec((1, ng * w), lambda g, i: (0, g))
    rev = lambda w: pl.BlockSpec((t, ng * w), lambda g, i: (nt - 1 - i, g))
    return pl.pallas_call(
        body, name="ssd_bwd",
        grid=(SSD_GROUPS // ng, nt),
        in_specs=[rev(FW_B), rev(768), rev(SSD_GW), rev(SSD_GW), rev(SSD_GW), rev(SSD_GW),
                  pl.BlockSpec((nch, SSD_STATE, ng * SSD_GW), lambda g, i: (nt - 1 - i, 0, g)),
                  gvec(128), gvec(512), gvec(512),
                  pl.BlockSpec((4, ng * 768), lambda g, i: (0, g)), gvec(512)],
        out_specs=[rev(BW_B),
                   pl.BlockSpec((ng, 4, 8, 512), lambda g, i: (g, 0, 0, 0)),
                   pl.BlockSpec((ng, 5, 8, 768), lambda g, i: (g, 0, 0, 0))],
        out_shape=[jax.ShapeDtypeStruct((s, SSD_GROUPS * BW_B), MXU_DTYPE),
                   jax.ShapeDtypeStruct((SSD_GROUPS, 4, 8, 512), F32),
                   jax.ShapeDtypeStruct((SSD_GROUPS, 5, 8, 768), F32)],
        scratch_shapes=[pltpu.VMEM((ng, SSD_STATE, SSD_GW), F32), pltpu.VMEM((8, ng * 768), F32),
                        pltpu.VMEM((t, ng * SSD_GW), F32), pltpu.VMEM((t, ng * SSD_GW), F32)],
        compiler_params=_cparams(("parallel", "arbitrary"), ng * (2 * t * FW_B * 4 + 18 * t * SSD_GW * 4) + 16 * 1024 * 1024),
    )(proj_b, pre_all, dt_all, acs_all, dyb, y, hprev, dtb, alog, dsk, cw, nw)


def _rows_call(body, ins, outs, tr, name):
    r = ins[0].shape[0]
    spec = lambda a: pl.BlockSpec((tr, a.shape[1]), lambda i: (i, 0))
    est = 2 * tr * sum(a.shape[1] * jnp.dtype(a.dtype).itemsize for a in list(ins) + list(outs))
    return pl.pallas_call(
        body, name=name, grid=(r // tr,),
        in_specs=[spec(a) for a in ins], out_specs=[spec(o) for o in outs], out_shape=list(outs),
        compiler_params=_cparams(("parallel",), est),
    )(*ins)


def _add_pair(a, b, tr, name):
    def body(a_ref, b_ref, o_ref):
        o_ref[...] = a_ref[...] + b_ref[...]

    return _rows_call(body, [a, b], [jax.ShapeDtypeStruct(a.shape, F32)], tr, name)[0]


def _rs_add(p, sib, place, tr, name):
    _, r, c = p.shape
    half = r // 2
    nb = half // tr

    def body(pl_ref, p_ref, s_ref, b_ref, own_ref):
        v = p_ref[0] + s_ref[0]
        b_ref[0] = v.astype(jnp.bfloat16)

        @pl.when(pl.program_id(1) == pl_ref[0])
        def _():
            own_ref[...] = v

    return pl.pallas_call(
        body, name=name,
        grid_spec=pltpu.PrefetchScalarGridSpec(
            num_scalar_prefetch=1, grid=(nb, 4),
            in_specs=[pl.BlockSpec((1, tr, c), lambda i, k, pr: (k, pr[1] * nb + i, 0)),
                      pl.BlockSpec((1, tr, c), lambda i, k, pr: (k, i, 0))],
            out_specs=[pl.BlockSpec((1, tr, c), lambda i, k, pr: (k, i, 0)),
                       pl.BlockSpec((tr, c), lambda i, k, pr: (i, 0))]),
        out_shape=[jax.ShapeDtypeStruct((4, half, c), jnp.bfloat16), jax.ShapeDtypeStruct((half, c), F32)],
        compiler_params=_cparams(("parallel", "arbitrary"), 2 * tr * c * 14),
    )(place, p, sib)


WIN_STEP = 3840
WIN_W = 3968


def _rs_add_windows(dw, sib, place, tr, name):
    r = dw.shape[0]
    half = r // 2
    nb = half // tr
    tail = WIN_W - WIN_STEP

    def body(pl_ref, pm_ref, pt_ref, s_ref, b_ref, own_ref):
        vm = pm_ref[...].astype(F32) + s_ref[0, :, 0:WIN_STEP].astype(F32)
        vt = pt_ref[...].astype(F32) + s_ref[0, :, WIN_STEP:WIN_W].astype(F32)
        b_ref[0, :, 0:WIN_STEP] = vm.astype(jnp.bfloat16)
        b_ref[0, :, WIN_STEP:WIN_W] = vt.astype(jnp.bfloat16)

        @pl.when(pl.program_id(1) == pl_ref[0])
        def _():
            own_ref[:, 0:WIN_STEP] = vm
            own_ref[:, WIN_STEP:WIN_W] = vt

    return pl.pallas_call(
        body, name=name,
        grid_spec=pltpu.PrefetchScalarGridSpec(
            num_scalar_prefetch=1, grid=(nb, 4),
            in_specs=[pl.BlockSpec((tr, WIN_STEP), lambda i, k, pr: (pr[1] * nb + i, k)),
                      pl.BlockSpec((tr, tail), lambda i, k, pr: (pr[1] * nb + i, (WIN_STEP // tail) * (k + 1))),
                      pl.BlockSpec((1, tr, WIN_W), lambda i, k, pr: (k, i, 0))],
            out_specs=[pl.BlockSpec((1, tr, WIN_W), lambda i, k, pr: (k, i, 0)),
                       pl.BlockSpec((tr, WIN_W), lambda i, k, pr: (i, 0))]),
        out_shape=[jax.ShapeDtypeStruct((4, half, WIN_W), jnp.bfloat16), jax.ShapeDtypeStruct((half, WIN_W), F32)],
        compiler_params=_cparams(("parallel", "arbitrary"), 2 * tr * WIN_W * 14),
    )(place, dw, dw, sib)


def _sum_own_recv(own, recv, tr, name):
    r, c = own.shape

    def body(o_ref, r_ref, out_ref):
        v = o_ref[...]
        for j in range(3):
            v = v + r_ref[j].astype(F32)
        out_ref[...] = v

    return pl.pallas_call(
        body, name=name, grid=(r // tr,),
        in_specs=[pl.BlockSpec((tr, c), lambda i: (i, 0)), pl.BlockSpec((3, tr, c), lambda i: (0, i, 0))],
        out_specs=pl.BlockSpec((tr, c), lambda i: (i, 0)),
        out_shape=jax.ShapeDtypeStruct((r, c), F32),
        compiler_params=_cparams(("parallel",), 2 * tr * c * 14),
    )(own, recv)


def _sum_slots(stack, name):
    n, r, w = stack.shape

    def body(a_ref, out_ref):
        v = a_ref[0]
        for k in range(1, n):
            v = v + a_ref[k]
        out_ref[...] = v

    return pl.pallas_call(
        body, name=name, grid=(1,),
        in_specs=[pl.BlockSpec((n, r, w), lambda i: (0, 0, 0))],
        out_specs=pl.BlockSpec((r, w), lambda i: (0, 0)),
        out_shape=jax.ShapeDtypeStruct((r, w), F32),
        compiler_params=_cparams(("arbitrary",), 2 * (n + 1) * r * w * 4),
    )(stack)


def _adamw(w, g, m, v, tr, name):
    def body(w_ref, g_ref, m_ref, v_ref, d_ref, nm_ref, nv_ref):
        d_ref[...], nm_ref[...], nv_ref[...] = _adam_math(w_ref[...], g_ref[...], m_ref[...], v_ref[...])

    o = jax.ShapeDtypeStruct(w.shape, F32)
    return _rows_call(body, [w, g, m, v], [o, o, o], tr, name)


def _adam_math(w, g, m, v):
    nm = ADAM_B1 * m + (1.0 - ADAM_B1) * g
    nv = ADAM_B2 * v + (1.0 - ADAM_B2) * (g * g)
    m_hat = nm / (1.0 - ADAM_B1 ** ADAM_STEP)
    v_hat = nv / (1.0 - ADAM_B2 ** ADAM_STEP)
    return -ADAM_LR * (m_hat / (jnp.sqrt(v_hat) + ADAM_EPS) + ADAM_WD * w), nm, nv


def _adamw_halves(w, g_own, g_sib, m, v, place, tr, name):
    r, c = w.shape

    def body(pl_ref, w_ref, go_ref, gs_ref, m_ref, v_ref, g_ref, d_ref, nm_ref, nv_ref):
        first = pl_ref[1] == 0
        own, sib = go_ref[...], gs_ref[...]
        g = jnp.concatenate([jnp.where(first, own, sib), jnp.where(first, sib, own)], axis=1)
        g_ref[...] = g
        d_ref[...], nm_ref[...], nv_ref[...] = _adam_math(w_ref[...], g, m_ref[...], v_ref[...])

    full = pl.BlockSpec((tr, c), lambda i, pr: (i, 0))
    half = pl.BlockSpec((tr, c // 2), lambda i, pr: (i, 0))
    o = jax.ShapeDtypeStruct((r, c), F32)
    return pl.pallas_call(
        body, name=name,
        grid_spec=pltpu.PrefetchScalarGridSpec(num_scalar_prefetch=1, grid=(r // tr,),
                                               in_specs=[full, half, half, full, full], out_specs=[full] * 4),
        out_shape=[o] * 4,
        compiler_params=_cparams(("parallel",), 2 * tr * c * 4 * 8),
    )(place, w, g_own, g_sib, m, v)


ANY = pl.BlockSpec(memory_space=pl.ANY)


def _place():
    x, y, c = lax.axis_index("x"), lax.axis_index("y"), lax.axis_index("c")
    others = [(1 - x, y), (x, 1 - y), (1 - x, 1 - y)]
    return x, y, c, 2 * x + y, others


def _remote(src, dst, send, recv, k, to):
    return pltpu.make_async_remote_copy(src_ref=src, dst_ref=dst, send_sem=send.at[k], recv_sem=recv.at[k],
                                        device_id=to, device_id_type=MESH)


def _norm_gather_call(x, norm_w, win_b, wout_b, cw8, tm):
    s = x.shape[0]
    ni = s // tm
    h_in, h_out = win_b.shape[0] // 2, wout_b.shape[0] // 2
    q_in = h_in // 2

    def body(x_ref, w_ref, win, wout, cw, xn_ref, xnt_ref, g_in, g_out, g_cw, send, recv):
        i = pl.program_id(0)

        def direct():
            xx, yy, c, me, others = _place()
            mi, mo = pl.ds(c * h_in, h_in), pl.ds(c * h_out, h_out)
            cps = [_remote(win.at[mi], g_in.at[me, mi], send, recv, j, (*others[j], c)) for j in range(2)]
            cps += [_remote(wout.at[mo], g_out.at[me, mo], send, recv, 7 + j, (*chip, c)) for j, chip in enumerate(others)]
            cps += [_remote(cw, g_cw.at[me], send, recv, 13 + j, (*chip, c)) for j, chip in enumerate(others)]
            return cps

        @pl.when(i == 0)
        def _():
            for cp in direct():
                cp.start()

        xv = x_ref[...]
        r = lax.rsqrt(jnp.mean(xv * xv, axis=-1, keepdims=True) + EPS)
        xn = xv * r * w_ref[...]
        xn_ref[...] = _c(xn)
        xnt_ref[...] = _c(xn.T)

        @pl.when(i == ni - 1)
        def _():
            xx, yy, c, me, others = _place()
            sib = (xx, yy, 1 - c)
            kx, ky, kd = (2 * chip[0] + chip[1] for chip in others)
            mi, ti = pl.ds(c * h_in, h_in), pl.ds((1 - c) * h_in, h_in)
            quarter = [pl.ds(c * h_in, q_in), pl.ds(c * h_in + q_in, q_in)]
            started = []

            def go(cp):
                cp.start()
                started.append(cp)

            _remote(g_in.at[kx, mi], g_in.at[kx, mi], send, recv, 0, (*others[0], c)).wait_recv()
            go(_remote(g_in.at[kx, quarter[0]], g_in.at[kx, quarter[0]], send, recv, 2, (*others[1], c)))
            go(_remote(g_in.at[kx, mi], g_in.at[kx, mi], send, recv, 4, sib))
            _remote(g_in.at[ky, mi], g_in.at[ky, mi], send, recv, 1, (*others[1], c)).wait_recv()
            go(_remote(g_in.at[ky, quarter[1]], g_in.at[ky, quarter[1]], send, recv, 3, (*others[0], c)))
            go(_remote(g_in.at[ky, mi], g_in.at[ky, mi], send, recv, 5, sib))
            mo, to = pl.ds(c * h_out, h_out), pl.ds((1 - c) * h_out, h_out)
            for j, chip in enumerate(others):
                kj = 2 * chip[0] + chip[1]
                _remote(g_out.at[kj, mo], g_out.at[kj, mo], send, recv, 7 + j, (*chip, c)).wait_recv()
                go(_remote(g_out.at[kj, mo], g_out.at[kj, mo], send, recv, 10 + j, sib))
            _remote(g_in.at[kd, quarter[0]], g_in.at[kd, quarter[0]], send, recv, 2, (*others[1], c)).wait_recv()
            _remote(g_in.at[kd, quarter[1]], g_in.at[kd, quarter[1]], send, recv, 3, (*others[0], c)).wait_recv()
            go(_remote(g_in.at[kd, mi], g_in.at[kd, mi], send, recv, 6, sib))
            for k_src, sem in ((kx, 4), (ky, 5), (kd, 6)):
                _remote(g_in.at[k_src, ti], g_in.at[k_src, ti], send, recv, sem, sib).wait_recv()
            for j, chip in enumerate(others):
                kj = 2 * chip[0] + chip[1]
                _remote(g_out.at[kj, to], g_out.at[kj, to], send, recv, 10 + j, sib).wait_recv()
                _remote(cw, g_cw.at[kj], send, recv, 13 + j, (*chip, c)).wait_recv()
            for cp in direct() + started:
                cp.wait_send()

    outs = [jax.ShapeDtypeStruct((s, D_MODEL), MXU_DTYPE), jax.ShapeDtypeStruct((D_MODEL, s), MXU_DTYPE)]
    outs += [jax.ShapeDtypeStruct((4,) + a.shape, a.dtype) for a in (win_b, wout_b, cw8)]
    return pl.pallas_call(
        body, name="rmsnorm_gather_weights",
        grid=(ni,),
        in_specs=[pl.BlockSpec((tm, D_MODEL), lambda i: (i, 0)), pl.BlockSpec((1, D_MODEL), lambda i: (0, 0)), ANY, ANY, ANY],
        out_specs=[pl.BlockSpec((tm, D_MODEL), lambda i: (i, 0)), pl.BlockSpec((D_MODEL, tm), lambda i: (0, i)), ANY, ANY, ANY],
        out_shape=outs,
        scratch_shapes=[pltpu.SemaphoreType.DMA((16,)), pltpu.SemaphoreType.DMA((16,))],
        compiler_params=_cparams(("arbitrary",), 2 * tm * D_MODEL * 12),
    )(x, norm_w, win_b, wout_b, cw8)


def _dw_out_rs_call(a, b, dw, *, tn, tk):
    m, k = a.shape
    n = b.shape[1]
    nj, nk = n // tn, k // tk
    half = dw.shape[0] // 2

    def body(a_ref, b_ref, pin, o_ref, sib_in, send, recv):
        j, kk = pl.program_id(0), pl.program_id(1)

        def copies():
            x, y, c, me, others = _place()
            rows = pl.ds((1 - c) * half, half)
            return [_remote(pin.at[rows, pl.ds(WIN_STEP * w, WIN_W)], sib_in.at[w], send, recv, w, (x, y, 1 - c)) for w in range(4)]

        @pl.when((j == 0) & (kk == 0))
        def _():
            for cp in copies():
                cp.start()

        @pl.when(kk == 0)
        def _():
            o_ref[...] = jnp.zeros_like(o_ref)

        o_ref[...] += _dot(a_ref[...], b_ref[...])

        @pl.when((j == nj - 1) & (kk == nk - 1))
        def _():
            cps = copies()
            for cp in cps:
                cp.wait_recv()
            for cp in cps:
                cp.wait_send()

    isz = jnp.dtype(a.dtype).itemsize
    est = 2 * (m * tk + tk * tn) * isz + 2 * m * tn * 4
    outs = [jax.ShapeDtypeStruct((m, n), F32), jax.ShapeDtypeStruct((4, half, WIN_W), dw.dtype)]
    return pl.pallas_call(
        body, name="dw_out_rs_sibling",
        grid=(nj, nk),
        in_specs=[pl.BlockSpec((m, tk), lambda j, kk: (0, kk)), pl.BlockSpec((tk, tn), lambda j, kk: (kk, j)), ANY],
        out_specs=[pl.BlockSpec((m, tn), lambda j, kk: (0, j)), ANY],
        out_shape=outs,
        scratch_shapes=[pltpu.SemaphoreType.DMA((4,)), pltpu.SemaphoreType.DMA((4,))],
        compiler_params=_cparams(("arbitrary", "arbitrary"), est),
    )(a, b, dw)


def _rs_sibling_call(p_out, vsmall):
    def body(pout, vs, sib_out, sib_v, send, recv):
        x, y, c, me, others = _place()
        sib = (x, y, 1 - c)
        half = pout.shape[1] // 2
        cps = [_remote(pout.at[:, pl.ds((1 - c) * half, half)], sib_out, send, recv, 0, sib),
               _remote(vs, sib_v, send, recv, 1, sib)]
        for cp in cps:
            cp.start()
        for cp in cps:
            cp.wait_recv()
        for cp in cps:
            cp.wait_send()

    outs = [jax.ShapeDtypeStruct((4, p_out.shape[1] // 2, p_out.shape[2]), p_out.dtype),
            jax.ShapeDtypeStruct(vsmall.shape, vsmall.dtype)]
    return pl.pallas_call(
        body, name="rs_sibling",
        in_specs=[ANY] * 2, out_specs=[ANY] * 2, out_shape=outs,
        scratch_shapes=[pltpu.SemaphoreType.DMA((2,)), pltpu.SemaphoreType.DMA((2,))],
    )(p_out, vsmall)


def _rs_join_call(f_in, f_out, nw8):
    def body(fin, fout, nw, sib_in, full_out, all_nw, send, recv):
        x, y, c, me, others = _place()
        sib = (x, y, 1 - c)
        half = fout.shape[0]
        cps = [_remote(fin, sib_in, send, recv, 0, sib),
               _remote(fout, full_out.at[pl.ds(c * half, half)], send, recv, 1, sib)]
        mine = 4 * x + 2 * y + c
        peers = []
        for r in range(1, 8):
            px, py, pc = (1 - x if r & 4 else x), (1 - y if r & 2 else y), (1 - c if r & 1 else c)
            peers.append((r, (px, py, pc), 4 * px + 2 * py + pc))
            cps.append(_remote(nw, all_nw.at[mine], send, recv, 1 + r, (px, py, pc)))
        for cp in cps:
            cp.start()
        cps[0].wait_recv()
        _remote(fout, full_out.at[pl.ds((1 - c) * half, half)], send, recv, 1, sib).wait_recv()
        for r, peer, idx in peers:
            _remote(nw, all_nw.at[idx], send, recv, 1 + r, peer).wait_recv()
        for cp in cps:
            cp.wait_send()

    outs = [jax.ShapeDtypeStruct(f_in.shape, F32), jax.ShapeDtypeStruct((2 * f_out.shape[0], f_out.shape[1]), F32),
            jax.ShapeDtypeStruct((8,) + nw8.shape, F32)]
    return pl.pallas_call(
        body, name="rs_join",
        in_specs=[ANY] * 3, out_specs=[ANY] * 3, out_shape=outs,
        scratch_shapes=[pltpu.SemaphoreType.DMA((9,)), pltpu.SemaphoreType.DMA((9,))],
    )(f_in, f_out, nw8)


def _pack(arrs):
    parts = []
    for a in arrs:
        f = a.reshape(-1).astype(F32)
        pad = (-f.shape[0]) % 1024
        parts.append(jnp.pad(f, (0, pad)).reshape(-1, 128))
    return jnp.concatenate(parts, axis=0)


def _unpack(packed, shapes):
    out, row = [], 0
    for shp in shapes:
        n = 1
        for d in shp:
            n *= d
        rows = (n + 1023) // 1024 * 8
        out.append(packed[row:row + rows].reshape(-1)[:n].reshape(shp))
        row += rows
    return out


def _expand_heads(v32):
    return jnp.repeat(v32.reshape(32), HEADDIM).reshape(1, D_MODEL)


def kernel(x, norm_w, w_in, gate_b, sgu_norm_g, sgu_norm_b, sgu_w, sgu_b, conv_w, conv_b, dt_bias, A_log, D_skip, ssd_norm_w, w_out, final_norm_w, loss_target, m_norm_w, m_w_in, m_gate_b, m_sgu_norm_g, m_sgu_norm_b, m_sgu_w, m_sgu_b, m_conv_w, m_conv_b, m_dt_bias, m_A_log, m_D_skip, m_ssd_norm_w, m_w_out, m_final_norm_w, v_norm_w, v_w_in, v_gate_b, v_sgu_norm_g, v_sgu_norm_b, v_sgu_w, v_sgu_b, v_conv_w, v_conv_b, v_dt_bias, v_A_log, v_D_skip, v_ssd_norm_w, v_w_out, v_final_norm_w):
    s = x.shape[1]
    x2 = x.reshape(s, D_MODEL)
    tgt = loss_target.reshape(s, D_MODEL)
    t_ssd, t_tok, t_out, t_row = min(T_SSD, s), min(T_TOK, s), min(T_OUT, s), min(T_ROW, s)
    tm_mm, tk_dw = min(TM_MM, s), min(TK_DW, s)
    chip = 2 * lax.axis_index("x") + lax.axis_index("y")

    cw8 = jnp.pad(conv_w[0], ((0, 4), (0, 0)))
    win_b, wout_b = _c(w_in[0]), _c(w_out[0])
    xn, xnt, g_in, g_out, g_cw = _norm_gather_call(x2, norm_w, win_b, wout_b, cw8, t_row)
    g_in = lax.dynamic_update_index_in_dim(g_in, win_b, chip, 0)
    g_out = lax.dynamic_update_index_in_dim(g_out, wout_b, chip, 0)
    g_cw = lax.dynamic_update_index_in_dim(g_cw, cw8, chip, 0)
    wt = jnp.transpose(g_in, (0, 2, 1)).reshape(IN_W, D_MODEL)
    w_out_full = g_out.reshape(D_MODEL, D_MODEL)
    conv_w_full = jnp.transpose(g_cw[:, 0:4, :], (1, 0, 2)).reshape(4, 3072)

    wt_a = jnp.concatenate([wt[0:6144], wt[11296:15392]], axis=0)
    per_group = lambda lo, n: wt[lo:lo + SSD_GROUPS * n].reshape(SSD_GROUPS, n, D_MODEL)
    wt_b = jnp.concatenate([per_group(6144, 512), per_group(8192, 512), per_group(10240, 128), per_group(10752, 128),
                            jnp.pad(per_group(11264, 8), ((0, 0), (0, 120), (0, 0)))], axis=1)
    wt_b = wt_b.reshape(SSD_GROUPS * BW_B, D_MODEL)

    def group_cols(full_xs, full_bc):
        parts = []
        for g in range(SSD_GROUPS):
            parts += [full_xs[:, 512 * g:512 * g + 512], full_bc[:, 128 * g:128 * g + 128], full_bc[:, 512 + 128 * g:512 + 128 * g + 128]]
        return jnp.concatenate(parts, axis=1)

    cw_g = group_cols(conv_w_full[:, 0:2048], conv_w_full[:, 2048:3072])
    cb_g = group_cols(conv_b[:, 0:2048], conv_b[:, 2048:3072])
    alog_e, dsk_e = _expand_heads(A_log), _expand_heads(D_skip)
    narrow = lambda v32: jnp.pad(v32.reshape(SSD_GROUPS, 8), ((0, 0), (0, 120))).reshape(1, SSD_GROUPS * 128)
    dtb_n, alog_n = narrow(dt_bias), narrow(A_log)

    pos_chunk = jnp.arange(SGU_BLOCK) // CHUNK
    smask = pos_chunk[None, :] <= pos_chunk[:, None]
    wm_f = jnp.where(smask[None], sgu_w[0], 0.0)
    wm = _c(wm_f)
    wmt = _c(jnp.transpose(wm_f, (0, 2, 1)))
    bias_full = jnp.repeat(sgu_b[0].T, D_MODEL // SGU_GROUPS, axis=1)
    fnw = final_norm_w.reshape(1, D_MODEL)

    proj_a = _mm(xn, wt_a, tm=tm_mm, tn=2048, tk=D_MODEL, name="in_proj_a", out_dtype=MXU_DTYPE, b_is_t=True)
    proj_b = _mm(xn, wt_b, tm=tm_mm, tn=BW_B, tk=D_MODEL, name="in_proj_b", b_is_t=True)
    y_ssd, y_b, hprev, pre_all, dt_all, acs_all = _ssd_fwd_call(proj_b, dtb_n, alog_n, dsk_e, cw_g, cb_g, ssd_norm_w, t_ssd, NG_SSD)
    y_a, merged, merged_t = _tok_fwd_call(proj_a, y_b, gate_b, sgu_norm_g, sgu_norm_b, wm, bias_full, t_tok)
    dh_b, dmerged, loss_t, dfw8 = _out_call(merged, x2, tgt, w_out_full, fnw, t_out)

    dproj_a, dy_b, dgb8, dgam8, dbeta8, dbfull, dws = _tok_bwd_call(
        proj_a, dmerged, y_a, y_b, gate_b, sgu_norm_g, sgu_norm_b, wm, wmt, bias_full, t_tok)
    dproj_b, a512, a768 = _ssd_bwd_call(proj_b, pre_all, dt_all, acs_all, dy_b, y_ssd, hprev, dtb_n, alog_e, dsk_e, cw_g,
                                        ssd_norm_w, t_ssd, NG_SSD)
    dw_uvz = _mm(xnt, dproj_a, tm=D_MODEL, tn=1024, tk=min(2 * tk_dw, s), name="dw_in_uvz", n=6144, out_dtype=MXU_DTYPE)
    dw_gate = _mm(xnt, dproj_a, tm=D_MODEL, tn=1024, tk=min(2 * tk_dw, s), name="dw_in_gate", col0=6, n=4096,
                  out_dtype=MXU_DTYPE)
    dw_zb, dw_xs, dw_bm, dw_cm, dw_dt = _dw_groups(xnt, dproj_b, tk=tk_dw, out_dtype=MXU_DTYPE)

    dw_dt32 = jnp.concatenate([dw_dt[:, 128 * g:128 * g + 8] for g in range(SSD_GROUPS)], axis=1)
    dw_ref = jnp.concatenate([dw_uvz, dw_zb, dw_xs, dw_bm, dw_cm, dw_dt32, dw_gate,
                              jnp.zeros((D_MODEL, 3 * WIN_STEP + WIN_W - IN_W), MXU_DTYPE)], axis=1)
    dw_out_p, sib_i = _dw_out_rs_call(merged_t, dh_b, dw_ref, tn=1024, tk=tk_dw)
    p_out = dw_out_p.reshape(4, D_MODEL // 4, D_MODEL)

    s512 = jnp.sum(a512, axis=2)
    heads = lambda v: jnp.sum(v.reshape(32, HEADDIM), axis=1).reshape(1, 32)
    d_ssd_nw = s512[:, 0].reshape(1, D_MODEL)
    d_dskip = heads(s512[:, 1].reshape(D_MODEL))
    d_alog = heads(s512[:, 2].reshape(D_MODEL)) * (1.0 / HEADDIM) * (-jnp.exp(A_log))
    d_dtb = s512[:, 3, 0:8].reshape(1, 32)
    s768 = jnp.sum(a768, axis=2)
    ungroup = lambda v: jnp.concatenate([v[g, :, 0:512] for g in range(4)] + [v[g, :, 512:640] for g in range(4)]
                                        + [v[g, :, 640:768] for g in range(4)], axis=1)
    d_cw = ungroup(s768[:, 0:4])
    d_cb = ungroup(s768[:, 4:5])
    d_sgu_b = jnp.sum(dbfull.reshape(128, SGU_GROUPS, 128), axis=2).T.reshape(1, SGU_GROUPS, 128)
    d_sgu_w = jnp.where(smask[None], dws, 0.0).reshape(1, SGU_GROUPS, 128, 128)
    fold = lambda a8: jnp.sum(a8, axis=0, keepdims=True)
    small_local = [fold(dgb8), fold(dgam8), fold(dbeta8), d_sgu_w, d_sgu_b, d_cw, d_cb,
                   d_dtb, d_alog, d_dskip, d_ssd_nw, fold(dfw8).reshape(D_MODEL), jnp.sum(loss_t[:, 0, 0]).reshape(1)]
    small_shapes = [a.shape for a in small_local]
    v_local = _pack(small_local)

    core = lax.axis_index("c")
    place = jnp.stack([chip, core]).astype(jnp.int32)
    hr_o = D_MODEL // 8
    sib_o, sib_v = _rs_sibling_call(p_out, v_local)
    s1b_i, o_i = _rs_add_windows(dw_ref, sib_i, place, 256, "rs_add_in")
    s1b_o, o_o = _rs_add(p_out, sib_o, place, 256, "rs_add_out")
    chip_v = _add_pair(v_local, sib_v, v_local.shape[0], "ar_add_small")
    dxn, r_i, r_o, abs_v = _dx_rs_call(dproj_a, wt_a, dproj_b, wt_b, s1b_i, s1b_o, chip_v, tm=tm_mm)
    grad_x, dnw8 = _gradx_call(x2, dxn, dh_b, norm_w, t_row)
    abs_v = lax.dynamic_update_index_in_dim(abs_v, chip_v, chip, 0)
    f_i = _sum_own_recv(o_i, r_i, 256, "rs_sum_in")
    f_o = _sum_own_recv(o_o, r_o, 256, "rs_sum_out")
    sib_f_i, g_w_out, all_nw = _rs_join_call(f_i, f_o, dnw8)
    g_w_out = lax.dynamic_update_slice_in_dim(g_w_out, f_o, core * hr_o, axis=0)
    all_nw = lax.dynamic_update_index_in_dim(all_nw, dnw8, 2 * chip + core, 0)
    g_nw = fold(_sum_slots(all_nw, "ar_sum_norm_w"))
    total_v = _sum_slots(abs_v, "ar_sum_small")
    (g_gb, g_gam, g_beta, g_sw, g_sb, g_cw_full, g_cb, g_dtb, g_alog, g_dsk, g_snw, g_fnw, loss1) = _unpack(total_v, small_shapes)
    g_cw_shard = lax.dynamic_slice(g_cw_full, (0, chip * 768), (4, 768)).reshape(1, 4, 768)
    loss = loss1.reshape(())

    shard_t = lambda win: lax.dynamic_slice_in_dim(win, 8 * chip, SHARD_W, axis=1).T
    g_w_in, d_win, nm_win, nv_win = (a.T for a in _adamw_halves(w_in[0].T, shard_t(f_i), shard_t(sib_f_i), m_w_in[0].T,
                                                                v_w_in[0].T, place, 296, "adamw_w_in"))
    d_wout, nm_wout, nv_wout = _adamw(w_out[0], g_w_out, m_w_out[0], v_w_out[0], 128, "adamw_w_out")
    small_w = [norm_w, gate_b, sgu_norm_g, sgu_norm_b, sgu_w, sgu_b, conv_w, conv_b, dt_bias, A_log, D_skip, ssd_norm_w, final_norm_w]
    small_m = [m_norm_w, m_gate_b, m_sgu_norm_g, m_sgu_norm_b, m_sgu_w, m_sgu_b, m_conv_w, m_conv_b, m_dt_bias, m_A_log, m_D_skip, m_ssd_norm_w, m_final_norm_w]
    small_v = [v_norm_w, v_gate_b, v_sgu_norm_g, v_sgu_norm_b, v_sgu_w, v_sgu_b, v_conv_w, v_conv_b, v_dt_bias, v_A_log, v_D_skip, v_ssd_norm_w, v_final_norm_w]
    small_g = [g_nw, g_gb, g_gam, g_beta, g_sw, g_sb, g_cw_shard, g_cb, g_dtb, g_alog, g_dsk, g_snw, g_fnw]
    shapes_w = [a.shape for a in small_w]
    small_g = [a.reshape(shp) for a, shp in zip(small_g, shapes_w)]
    pw = _pack(small_w)
    pd, pm, pv = _adamw(pw, _pack(small_g), _pack(small_m), _pack(small_v), pw.shape[0], "adamw_small")
    d_small, nm_small, nv_small = _unpack(pd, shapes_w), _unpack(pm, shapes_w), _unpack(pv, shapes_w)

    def with_big(small, win, wout):
        o = list(small)
        return o[0:1] + [win.reshape(1, D_MODEL, SHARD_W)] + o[1:12] + [wout.reshape(1, D_MODEL // 4, D_MODEL)] + o[12:13]

    grads = with_big(small_g, g_w_in, g_w_out)
    deltas = with_big(d_small, d_win, d_wout)
    new_m = with_big(nm_small, nm_win, nm_wout)
    new_v = with_big(nv_small, nv_win, nv_wout)
    return (loss, grad_x.reshape(1, s, D_MODEL), *grads, *deltas, *new_m, *new_v)
```

```python
import functools

import jax
import jax.numpy as jnp
from jax import lax
from jax.experimental import pallas as pl
from jax.experimental.pallas import tpu as pltpu

F32 = jnp.float32
MXU_DTYPE = jnp.bfloat16

D_MODEL = 2048
EPS = 1e-5
CHUNK = 64
SGU_BLOCK = 128
SGU_GROUPS = 16
SSD_GROUPS = 4
SSD_GW = 512
SSD_STATE = 128
HEADDIM = 64
IN_W = 15392
SHARD_W = IN_W // 4
BW_B = 1408
FW_B = BW_B
XBC_O, DT_O = 512, 1280
NA = 10240

ADAM_LR = 0.001
ADAM_B1 = 0.9
ADAM_B2 = 0.999
ADAM_EPS = 1e-08
ADAM_WD = 0.01
ADAM_STEP = 10

T_SSD = 256
NG_SSD = 4
T_TOK = 128
T_OUT = 512
T_ROW = 512
TM_MM = 1024
TK_DW = 1024
VMEM_CAP = 60 * 1024 * 1024
MESH = pl.DeviceIdType.MESH


def _cparams(sem, est_bytes):
    lim = int(min(VMEM_CAP, max(32 * 1024 * 1024, est_bytes + 12 * 1024 * 1024)))
    return pltpu.CompilerParams(dimension_semantics=sem, vmem_limit_bytes=lim)


def _c(x):
    return x.astype(MXU_DTYPE)


def _dot(a, b):
    return jnp.dot(a, b, preferred_element_type=F32)


def _dot_nt(a, b):
    return lax.dot_general(a, b, (((1,), (1,)), ((), ())), preferred_element_type=F32)


def _dot_tn(a, b):
    return lax.dot_general(a, b, (((0,), (0,)), ((), ())), preferred_element_type=F32)


def _split(x, n):
    parts, r = [], x
    for _ in range(n):
        p = _c(r)
        parts.append(p)
        r = r - p.astype(F32)
    return parts


def _dot01_l(m01, x, n):
    acc = None
    for p in _split(x, n):
        t = _dot(m01, p)
        acc = t if acc is None else acc + t
    return acc


def _dot01_r(x, m01, n):
    acc = None
    for p in _split(x, n):
        t = _dot(p, m01)
        acc = t if acc is None else acc + t
    return acc


def _sigmoid(x):
    return 0.5 * jnp.tanh(0.5 * x) + 0.5


def _fold8(x):
    r, w = x.shape
    return jnp.sum(x.reshape(r // 8, 8, w), axis=0)


def _iota(shape, dim):
    return lax.broadcasted_iota(jnp.int32, shape, dim)


def _ssd_masks():
    l64 = _iota((CHUNK, SSD_GW), 0)
    s64 = jnp.bitwise_and(_iota((CHUNK, SSD_GW), 1), CHUNK - 1)
    diag = l64 == s64
    causal = l64 >= s64
    row_last = l64 == CHUNK - 1
    r4 = lax.shift_right_logical(_iota((256, 256), 0), 6)
    c4 = lax.shift_right_logical(_iota((256, 256), 1), 6)
    mask4 = r4 == c4
    return diag, causal, row_last, mask4


def _cumsum_mats(t):
    r, c = _iota((t, t), 0), _iota((t, t), 1)
    same = lax.shift_right_logical(r, 6) == lax.shift_right_logical(c, 6)
    tri = _c(jnp.where(same, jnp.where(c <= r, 1.0, 0.0), 0.0))
    trit = _c(jnp.where(same, jnp.where(c >= r, 1.0, 0.0), 0.0))
    return tri, trit


def _head_expand_mat():
    return _c(jnp.where(_iota((128, SSD_GW), 0) == lax.shift_right_logical(_iota((128, SSD_GW), 1), 6), 1.0, 0.0))


def _ssd_common(xs, bm, cm, dt, acs, masks):
    diag, causal, row_last, mask4 = masks
    row_e = jnp.sum(jnp.where(diag, acs, 0.0), axis=0, keepdims=True)
    seg = acs - row_e
    lm = jnp.exp(jnp.where(causal, seg, -1e30))
    bb, cb = _c(bm), _c(cm)
    brep = jnp.concatenate([bb] * 8, axis=0)
    cbrep = _dot_nt(cb, brep)
    m = cbrep * lm
    xdt = xs * dt
    acs_last = jnp.sum(jnp.where(row_last, acs, 0.0), axis=0, keepdims=True)
    dec = jnp.exp(acs_last - acs)
    eacs = jnp.exp(acs)
    cd = jnp.exp(acs_last)
    return dict(lm=lm, bb=bb, cb=cb, brep=brep, m=m, xdt=xdt, dec=dec, eacs=eacs, cd=cd)


def _blockdiag4(xb, mask4):
    return jnp.where(mask4, jnp.concatenate([xb] * 4, axis=0), jnp.zeros((), xb.dtype))


def _ssd_chunk_fwd(xs, bm, cm, dt, acs, d_skip, ht, masks):
    q = _ssd_common(xs, bm, cm, dt, acs, masks)
    mask4 = masks[3]
    mb, xdtb = _c(q["m"]), _c(q["xdt"])
    yd = []
    for blk in range(2):
        sl = slice(256 * blk, 256 * blk + 256)
        yd.append(_dot(mb[:, sl], _blockdiag4(xdtb[:, sl], mask4)))
    y_diag = jnp.concatenate(yd, axis=1)
    p = _dot(q["cb"], _c(ht))
    y = y_diag + p * q["eacs"] + xs * d_skip
    st = _dot_tn(q["bb"], _c(q["xdt"] * q["dec"]))
    return y, ht * q["cd"] + st


def _ssd_chunk_bwd(xs, bm, cm, dt, acs, d_skip, hprev, dht, dy, masks):
    diag, causal, row_last, mask4 = masks
    q = _ssd_common(xs, bm, cm, dt, acs, masks)
    lm, bb, cb, brep, m, xdt, dec, eacs, cd = (q[k] for k in ("lm", "bb", "cb", "brep", "m", "xdt", "dec", "eacs", "cd"))
    hb = _c(hprev)
    yoff = _dot(cb, hb) * eacs
    dyb = _c(dy)
    dpb = _c(dy * eacs)
    d_c = _dot_nt(dpb, hb)
    dh_y = _dot_tn(cb, dpb)
    mb, xdtb = _c(m), _c(xdt)
    dm_parts, dxdt_parts = [], []
    for blk in range(2):
        sl = slice(256 * blk, 256 * blk + 256)
        bd = _blockdiag4(xdtb[:, sl], mask4)
        dm_parts.append(_dot_nt(dyb[:, sl], bd))
        dxf = jnp.where(mask4, _dot_tn(mb[:, sl], dyb[:, sl]), 0.0)
        dxdt_parts.append(dxf[0:64] + dxf[64:128] + dxf[128:192] + dxf[192:256])
    dm = jnp.concatenate(dm_parts, axis=1)
    dxdt = jnp.concatenate(dxdt_parts, axis=1)
    dcbb = _c(dm * lm)
    g = dm * m
    d_c = d_c + _dot(dcbb, brep)
    dbrep = _dot_tn(dcbb, cb)
    d_b = dbrep[0:64]
    for r in range(1, 8):
        d_b = d_b + dbrep[64 * r:64 * r + 64]
    dhtb = _c(dht)
    dxd = _dot(bb, dhtb)
    xd = xdt * dec
    dxdt = dxdt + dxd * dec
    tq = dxd * xd
    d_b = d_b + _dot_nt(_c(xd), dhtb)
    dcd = jnp.sum(dht * hprev, axis=0, keepdims=True)
    col_g = jnp.sum(g, axis=0, keepdims=True)
    last = jnp.sum(tq, axis=0, keepdims=True) + dcd * cd
    qq = g - jnp.where(diag, col_g, 0.0) + dy * yoff - tq + jnp.where(row_last, last, 0.0)
    dxs = dxdt * dt + dy * d_skip
    return dxs, d_b, d_c, dht * cd + dh_y, dy * xs, qq, dxdt * xs


def _ssd_finish_dt(qq, p1, dt, a_neg, trit, mask4):
    bd4 = _c(jnp.where(mask4, 1.0, 0.0))
    dacs = jnp.concatenate([_dot01_r(qq[:, 256 * b:256 * b + 256], bd4, 2) for b in range(2)], axis=1)
    da = _dot01_l(trit, dacs, 2)
    return p1 + da * (a_neg * (1.0 / HEADDIM)), da * dt


def _softplus(x):
    return jnp.maximum(x, 0.0) + jnp.log(1.0 + jnp.exp(-jnp.abs(x)))


def _conv_taps(xpad, t):
    taps = []
    for k in range(4):
        sh = 3 - k
        v = xpad if sh == 0 else pltpu.roll(xpad, sh, 0)
        taps.append(v[8:8 + t])
    return taps


def _mm(a, b, *, tm, tn, tk, name, out_dtype=F32, col0=0, n=None, b_is_t=False):
    m, k = a.shape
    n = b.shape[0 if b_is_t else 1] if n is None else n
    nk = k // tk
    assert m % tm == 0 and n % tn == 0 and k % tk == 0, (a.shape, b.shape, tm, tn, tk)
    dot = _dot_nt if b_is_t else _dot
    via_acc = nk > 1 and out_dtype != F32

    def body(a_ref, b_ref, o_ref, *acc):
        if nk == 1:
            o_ref[...] = dot(a_ref[...], b_ref[...]).astype(out_dtype)
            return
        acc_ref = acc[0] if via_acc else o_ref

        @pl.when(pl.program_id(2) == 0)
        def _():
            acc_ref[...] = jnp.zeros_like(acc_ref)

        acc_ref[...] += dot(a_ref[...], b_ref[...])
        if via_acc:
            @pl.when(pl.program_id(2) == nk - 1)
            def _():
                o_ref[...] = acc_ref[...].astype(out_dtype)

    isz = jnp.dtype(a.dtype).itemsize
    est = 2 * (tm * tk + tk * tn) * isz + 3 * tm * tn * 4
    return pl.pallas_call(
        body, name=name,
        grid=(m // tm, n // tn, nk),
        in_specs=[pl.BlockSpec((tm, tk), lambda i, j, kk: (i, kk)),
                  pl.BlockSpec((tn, tk), lambda i, j, kk: (j + col0, kk)) if b_is_t
                  else pl.BlockSpec((tk, tn), lambda i, j, kk: (kk, j + col0))],
        out_specs=pl.BlockSpec((tm, tn), lambda i, j, kk: (i, j)),
        out_shape=jax.ShapeDtypeStruct((m, n), out_dtype),
        scratch_shapes=[pltpu.VMEM((tm, tn), F32)] if via_acc else [],
        compiler_params=_cparams(("parallel", "parallel", "arbitrary"), est),
    )(a, b)


def _dw_groups(xnt, dpb, *, tk, out_dtype):
    m, k = xnt.shape
    nk = k // tk

    def body(a_ref, b_ref, zb_ref, xs_ref, bm_ref, cm_ref, dt_ref, acc_ref):
        @pl.when(pl.program_id(1) == 0)
        def _():
            acc_ref[...] = jnp.zeros_like(acc_ref)

        acc_ref[...] += _dot(a_ref[...], b_ref[...])

        @pl.when(pl.program_id(1) == nk - 1)
        def _():
            for o_ref, lo, hi in ((zb_ref, 0, 512), (xs_ref, 512, 1024), (bm_ref, 1024, 1152), (cm_ref, 1152, 1280),
                                  (dt_ref, 1280, 1408)):
                o_ref[...] = acc_ref[:, lo:hi].astype(out_dtype)

    isz, osz = jnp.dtype(xnt.dtype).itemsize, jnp.dtype(out_dtype).itemsize
    est = 2 * (m * tk + tk * BW_B) * isz + m * BW_B * (4 + 2 * osz)
    piece = lambda w: pl.BlockSpec((m, w), lambda g, kk: (0, g))
    return pl.pallas_call(
        body, name="dw_in_b",
        grid=(SSD_GROUPS, nk),
        in_specs=[pl.BlockSpec((m, tk), lambda g, kk: (0, kk)), pl.BlockSpec((tk, BW_B), lambda g, kk: (kk, g))],
        out_specs=[piece(512), piece(512), piece(128), piece(128), piece(128)],
        out_shape=[jax.ShapeDtypeStruct((m, w), out_dtype) for w in (2048, 2048, 512, 512, 512)],
        scratch_shapes=[pltpu.VMEM((m, BW_B), F32)],
        compiler_params=_cparams(("parallel", "arbitrary"), est),
    )(xnt, dpb)


def _dx_rs_call(dpa, wta, dpb, wtb, sb_in, sb_out, chip_v, *, tm):
    s = dpa.shape[0]
    tka, tkb = 1024, BW_B
    nka, nkb = dpa.shape[1] // tka, dpb.shape[1] // tkb
    ni, nk = s // tm, nka + nkb

    def body(a_ref, wa_ref, b_ref, wb_ref, sbin, sbout, cv, o_ref, rc_in, rc_out, abs_v, send, recv):
        i, kk = pl.program_id(0), pl.program_id(1)

        def copies():
            x, y, c, me, others = _place()
            sends, recvs = [], []
            for j, chip in enumerate(others):
                kj = 2 * chip[0] + chip[1]
                to = (*chip, c)
                sends += [_remote(sbin.at[kj], rc_in.at[j], send, recv, j, to),
                          _remote(sbout.at[kj], rc_out.at[j], send, recv, 3 + j, to),
                          _remote(cv, abs_v.at[me], send, recv, 6 + j, to)]
                recvs += [sends[-3], sends[-2], _remote(cv, abs_v.at[kj], send, recv, 6 + j, to)]
            return sends, recvs

        @pl.when((i == 0) & (kk == 0))
        def _():
            for cp in copies()[0]:
                cp.start()

        @pl.when(kk == 0)
        def _():
            o_ref[...] = jnp.zeros_like(o_ref)

        @pl.when(kk < nka)
        def _():
            o_ref[...] += _dot(a_ref[...], wa_ref[...])

        @pl.when(kk >= nka)
        def _():
            o_ref[...] += _dot(b_ref[...], wb_ref[...])

        @pl.when((i == ni - 1) & (kk == nk - 1))
        def _():
            sends, recvs = copies()
            for cp in recvs:
                cp.wait_recv()
            for cp in sends:
                cp.wait_send()

    isz = jnp.dtype(dpa.dtype).itemsize
    est = 2 * isz * (tm * tka + tka * D_MODEL + tm * tkb + tkb * D_MODEL) + 2 * tm * D_MODEL * 4
    outs = [jax.ShapeDtypeStruct((s, D_MODEL), F32),
            jax.ShapeDtypeStruct((3,) + sb_in.shape[1:], sb_in.dtype), jax.ShapeDtypeStruct((3,) + sb_out.shape[1:], sb_out.dtype),
            jax.ShapeDtypeStruct((4,) + chip_v.shape, F32)]
    return pl.pallas_call(
        body, name="dx_matmul_rs_chips",
        grid=(ni, nk),
        in_specs=[
            pl.BlockSpec((tm, tka), lambda i, kk: (i, jnp.minimum(kk, nka - 1))),
            pl.BlockSpec((tka, D_MODEL), lambda i, kk: (jnp.minimum(kk, nka - 1), 0)),
            pl.BlockSpec((tm, tkb), lambda i, kk: (i, jnp.maximum(kk - nka, 0))),
            pl.BlockSpec((tkb, D_MODEL), lambda i, kk: (jnp.maximum(kk - nka, 0), 0)),
            ANY, ANY, ANY,
        ],
        out_specs=[pl.BlockSpec((tm, D_MODEL), lambda i, kk: (i, 0)), ANY, ANY, ANY],
        out_shape=outs,
        scratch_shapes=[pltpu.SemaphoreType.DMA((9,)), pltpu.SemaphoreType.DMA((9,))],
        compiler_params=_cparams(("arbitrary", "arbitrary"), est),
    )(dpa, wta, dpb, wtb, sb_in, sb_out, chip_v)


def _gradx_call(x, dxn, dh, norm_w, tm):
    s = x.shape[0]

    def body(x_ref, g_ref, dh_ref, w_ref, gx_ref, dw_ref):
        @pl.when(pl.program_id(0) == 0)
        def _():
            dw_ref[...] = jnp.zeros_like(dw_ref)

        xv, gv = x_ref[...], g_ref[...]
        r = lax.rsqrt(jnp.mean(xv * xv, axis=-1, keepdims=True) + EPS)
        gw = gv * w_ref[...]
        gx_ref[...] = r * gw - xv * (r * r * r) * jnp.mean(xv * gw, axis=-1, keepdims=True) + dh_ref[...].astype(F32)
        dw_ref[...] += _fold8(gv * (xv * r))

    row = pl.BlockSpec((tm, D_MODEL), lambda i: (i, 0))
    return pl.pallas_call(
        body, name="grad_x",
        grid=(s // tm,),
        in_specs=[row, row, row, pl.BlockSpec((1, D_MODEL), lambda i: (0, 0))],
        out_specs=[row, pl.BlockSpec((8, D_MODEL), lambda i: (0, 0))],
        out_shape=[jax.ShapeDtypeStruct((s, D_MODEL), F32), jax.ShapeDtypeStruct((8, D_MODEL), F32)],
        compiler_params=_cparams(("arbitrary",), 2 * tm * D_MODEL * 16),
    )(x, dxn, dh, norm_w)


def _layernorm_stats(v):
    mu = jnp.mean(v, axis=-1, keepdims=True)
    vc = v - mu
    var = jnp.mean(vc * vc, axis=-1, keepdims=True)
    return vc * lax.rsqrt(var + EPS), lax.rsqrt(var + EPS)


def _tok_fwd_call(proj_a, y_b, gate_b, sgu_g, sgu_beta, wm, bias_full, t):
    s = proj_a.shape[0]

    def body(pa_ref, yb_ref, gb_ref, g_ref, be_ref, wm_ref, bf_ref, ya_ref, mg_ref, mgt_ref, mix_ref):
        u = pa_ref[:, 0:2048].astype(F32)
        v = pa_ref[:, 2048:4096].astype(F32)
        za = pa_ref[:, 4096:6144].astype(F32)
        xhat, _ = _layernorm_stats(v)
        vnb = _c(xhat * g_ref[...] + be_ref[...])
        for gi in range(SGU_GROUPS):
            sl = slice(128 * gi, 128 * gi + 128)
            mix_ref[:, sl] = _dot(wm_ref[gi], vnb[:, sl])
        mixed = mix_ref[...] + bf_ref[...]
        y_a = u * mixed * (za * _sigmoid(za))
        g0 = _sigmoid(pa_ref[:, 6144:8192].astype(F32) + gb_ref[:, 0:2048])
        g1 = _sigmoid(pa_ref[:, 8192:10240].astype(F32) + gb_ref[:, 2048:4096])
        merged = g0 * y_a + g1 * yb_ref[...].astype(F32)
        ya_ref[...] = _c(y_a)
        mg_ref[...] = _c(merged)
        mgt_ref[...] = _c(merged.T)

    row = pl.BlockSpec((t, D_MODEL), lambda i: (i, 0))
    vec = lambda w: pl.BlockSpec((1, w), lambda i: (0, 0))
    return pl.pallas_call(
        body, name="tok_fwd",
        grid=(s // t,),
        in_specs=[pl.BlockSpec((t, NA), lambda i: (i, 0)), row, vec(4096), vec(2048), vec(2048),
                  pl.BlockSpec((SGU_GROUPS, 128, 128), lambda i: (0, 0, 0)), pl.BlockSpec((128, D_MODEL), lambda i: (0, 0))],
        out_specs=[row, row, pl.BlockSpec((D_MODEL, t), lambda i: (0, i))],
        out_shape=[jax.ShapeDtypeStruct((s, D_MODEL), MXU_DTYPE), jax.ShapeDtypeStruct((s, D_MODEL), MXU_DTYPE),
                   jax.ShapeDtypeStruct((D_MODEL, s), MXU_DTYPE)],
        scratch_shapes=[pltpu.VMEM((t, D_MODEL), F32)],
        compiler_params=_cparams(("parallel",), 2 * t * NA * 4 + 12 * t * D_MODEL * 4),
    )(proj_a, y_b, gate_b, sgu_g, sgu_beta, wm, bias_full)


def _tok_bwd_call(proj_a, dmerged, y_a, y_b, gate_b, sgu_g, sgu_beta, wm, wmt, bias_full, t):
    s = proj_a.shape[0]

    def body(pa_ref, dm_ref, ya_ref, yb_ref, gb_ref, g_ref, be_ref, wm_ref, wmt_ref, bf_ref,
             dpa_ref, dyb_ref, dgb_ref, dgam_ref, dbeta_ref, dbf_ref, dws_ref, mix_ref, dvn_ref):
        @pl.when(pl.program_id(0) == 0)
        def _():
            dgb_ref[...] = jnp.zeros_like(dgb_ref)
            dgam_ref[...] = jnp.zeros_like(dgam_ref)
            dbeta_ref[...] = jnp.zeros_like(dbeta_ref)
            dbf_ref[...] = jnp.zeros_like(dbf_ref)
            dws_ref[...] = jnp.zeros_like(dws_ref)

        u = pa_ref[:, 0:2048].astype(F32)
        v = pa_ref[:, 2048:4096].astype(F32)
        za = pa_ref[:, 4096:6144].astype(F32)
        xhat, rstd = _layernorm_stats(v)
        vnb = _c(xhat * g_ref[...] + be_ref[...])
        for gi in range(SGU_GROUPS):
            sl = slice(128 * gi, 128 * gi + 128)
            mix_ref[:, sl] = _dot(wm_ref[gi], vnb[:, sl])
        mixed = mix_ref[...] + bf_ref[...]
        sig = _sigmoid(za)
        sz = za * sig
        dm = dm_ref[...].astype(F32)
        y_a = ya_ref[...].astype(F32)
        g0 = _sigmoid(pa_ref[:, 6144:8192].astype(F32) + gb_ref[:, 0:2048])
        g1 = _sigmoid(pa_ref[:, 8192:10240].astype(F32) + gb_ref[:, 2048:4096])
        dgl0 = dm * y_a * g0 * (1.0 - g0)
        dgl1 = dm * yb_ref[...].astype(F32) * g1 * (1.0 - g1)
        dyb_ref[...] = _c(dm * g1)
        dya = dm * g0
        dpa_ref[:, 6144:8192] = _c(dgl0)
        dpa_ref[:, 8192:10240] = _c(dgl1)
        dgb_ref[:, 0:2048] += _fold8(dgl0)
        dgb_ref[:, 2048:4096] += _fold8(dgl1)
        dpa_ref[:, 0:2048] = _c(dya * mixed * sz)
        dpa_ref[:, 4096:6144] = _c(dya * (u * mixed) * (sig * (1.0 + za * (1.0 - sig))))
        dmixed = dya * u * sz
        dbf_ref[...] += dmixed
        dmb = _c(dmixed)
        for gi in range(SGU_GROUPS):
            sl = slice(128 * gi, 128 * gi + 128)
            dvn_ref[:, sl] = _dot(wmt_ref[gi], dmb[:, sl])
            dws_ref[gi] += _dot_nt(dmb[:, sl], vnb[:, sl])
        dvn = dvn_ref[...]
        dgam_ref[...] += _fold8(dvn * xhat)
        dbeta_ref[...] += _fold8(dvn)
        dxh = dvn * g_ref[...]
        dv = rstd * (dxh - jnp.mean(dxh, axis=-1, keepdims=True) - xhat * jnp.mean(dxh * xhat, axis=-1, keepdims=True))
        dpa_ref[:, 2048:4096] = _c(dv)

    row = pl.BlockSpec((t, D_MODEL), lambda i: (i, 0))
    vec = lambda w: pl.BlockSpec((1, w), lambda i: (0, 0))
    acc = lambda w: pl.BlockSpec((8, w), lambda i: (0, 0))
    wspec = pl.BlockSpec((SGU_GROUPS, 128, 128), lambda i: (0, 0, 0))
    return pl.pallas_call(
        body, name="tok_bwd",
        grid=(s // t,),
        in_specs=[pl.BlockSpec((t, NA), lambda i: (i, 0)), row, row, row, vec(4096), vec(2048), vec(2048),
                  wspec, wspec, pl.BlockSpec((128, D_MODEL), lambda i: (0, 0))],
        out_specs=[pl.BlockSpec((t, NA), lambda i: (i, 0)), row, acc(4096), acc(2048), acc(2048),
                   pl.BlockSpec((128, D_MODEL), lambda i: (0, 0)), wspec],
        out_shape=[jax.ShapeDtypeStruct((s, NA), MXU_DTYPE), jax.ShapeDtypeStruct((s, D_MODEL), MXU_DTYPE),
                   jax.ShapeDtypeStruct((8, 4096), F32), jax.ShapeDtypeStruct((8, 2048), F32),
                   jax.ShapeDtypeStruct((8, 2048), F32), jax.ShapeDtypeStruct((128, D_MODEL), F32),
                   jax.ShapeDtypeStruct((SGU_GROUPS, 128, 128), F32)],
        scratch_shapes=[pltpu.VMEM((t, D_MODEL), F32), pltpu.VMEM((t, D_MODEL), F32)],
        compiler_params=_cparams(("arbitrary",), 2 * t * NA * 6 + 16 * t * D_MODEL * 4),
    )(proj_a, dmerged, y_a, y_b, gate_b, sgu_g, sgu_beta, wm, wmt, bias_full)


def _out_call(merged, x, target, w_out, fnw, t):
    s = x.shape[0]
    nt = s // t

    def body(mg_ref, x_ref, t_ref, w_ref, fw_ref, dhb_ref, dmg_ref, loss_ref, dfw_ref):
        @pl.when(pl.program_id(0) == 0)
        def _():
            dfw_ref[...] = jnp.zeros_like(dfw_ref)

        h = x_ref[...] + _dot(mg_ref[...], w_ref[...])
        r = lax.rsqrt(jnp.mean(h * h, axis=-1, keepdims=True) + EPS)
        hn = h * r
        err = hn * fw_ref[...] - t_ref[...]
        loss_ref[...] = jnp.full(loss_ref.shape, 0.5 * jnp.sum(jnp.mean(err * err, axis=-1, keepdims=True)), F32)
        dy = err * (1.0 / D_MODEL)
        dfw_ref[...] += _fold8(dy * hn)
        gw = dy * fw_ref[...]
        dh = r * gw - h * (r * r * r) * jnp.mean(h * gw, axis=-1, keepdims=True)
        dhb = _c(dh)
        dhb_ref[...] = dhb
        dmg_ref[...] = _c(_dot_nt(dhb, w_ref[...]))

    row = pl.BlockSpec((t, D_MODEL), lambda i: (i, 0))
    return pl.pallas_call(
        body, name="out_proj_loss",
        grid=(nt,),
        in_specs=[row, row, row, pl.BlockSpec((D_MODEL, D_MODEL), lambda i: (0, 0), pipeline_mode=pl.Buffered(1)),
                  pl.BlockSpec((1, D_MODEL), lambda i: (0, 0))],
        out_specs=[row, row, pl.BlockSpec((1, 8, 128), lambda i: (i, 0, 0)), pl.BlockSpec((8, D_MODEL), lambda i: (0, 0))],
        out_shape=[jax.ShapeDtypeStruct((s, D_MODEL), MXU_DTYPE),
                   jax.ShapeDtypeStruct((s, D_MODEL), MXU_DTYPE), jax.ShapeDtypeStruct((nt, 8, 128), F32),
                   jax.ShapeDtypeStruct((8, D_MODEL), F32)],
        compiler_params=_cparams(("arbitrary",), D_MODEL * D_MODEL * 2 + 2 * t * D_MODEL * 16 + 5 * t * D_MODEL * 4),
    )(merged, x, target, w_out, fnw)


def _ssd_fwd_call(proj_b, dtb, alog, dsk, cw, cb, nw, t, ng):
    s = proj_b.shape[0]
    nt, nch = s // t, t // CHUNK

    def body(pb_ref, halo_ref, dtb_ref, al_ref, ds_ref, cw_ref, cb_ref, nw_ref, y_ref, yb_ref, hp_ref, pre_ref,
             dt_ref, acs_ref, ht_ref, prev_ref):
        i = pl.program_id(1)

        @pl.when(i == 0)
        def _():
            ht_ref[...] = jnp.zeros_like(ht_ref)

        for gi in range(ng):
            prev_ref[:, 768 * gi:768 * gi + 768] = jnp.where(i == 0, 0.0, halo_ref[:, FW_B * gi + XBC_O:FW_B * gi + DT_O])
        masks = _ssd_masks()
        tri, _ = _cumsum_mats(t)
        a_neg = -jnp.exp(al_ref[...])
        expand = _head_expand_mat()
        for gi in range(ng):
            fo, go, no = FW_B * gi, SSD_GW * gi, 128 * gi
            dt_n = _softplus(pb_ref[:, fo + DT_O:fo + DT_O + 128] + dtb_ref[:, no:no + 128])
            acs_n = _dot01_l(tri, dt_n * a_neg[:, no:no + 128], 3)
            dt_ref[:, go:go + 512] = _dot01_r(dt_n, expand, 2)
            acs_ref[:, go:go + 512] = _dot01_r(acs_n, expand, 2)

        def chunk(c, carry):
            rows = pl.ds(pl.multiple_of(c * CHUNK, CHUNK), CHUNK)
            for gi in range(ng):
                fo, co, go, no = FW_B * gi, 768 * gi, SSD_GW * gi, 128 * gi
                xbc = pb_ref[rows, fo + XBC_O:fo + DT_O]
                taps = _conv_taps(jnp.concatenate([prev_ref[:, co:co + 768], xbc], axis=0), CHUNK)
                prev_ref[:, co:co + 768] = xbc[CHUNK - 8:CHUNK]
                pre = cb_ref[:, co:co + 768]
                for k in range(4):
                    pre = pre + taps[k] * cw_ref[k:k + 1, co:co + 768]
                pre_ref[rows, co:co + 768] = pre
                act = pre * _sigmoid(pre)
                dt = dt_ref[rows, go:go + 512]
                acs = acs_ref[rows, go:go + 512]
                ht = ht_ref[gi]
                hp_ref[c, :, go:go + 512] = ht
                y, ht_new = _ssd_chunk_fwd(act[:, 0:512], act[:, 512:640], act[:, 640:768], dt, acs,
                                           ds_ref[:, go:go + 512], ht, masks)
                y_ref[rows, go:go + 512] = y
                ht_ref[gi] = ht_new
                zb = pb_ref[rows, fo:fo + 512]
                hh = y * (zb * _sigmoid(zb))
                rr = lax.rsqrt(jnp.mean(hh * hh, axis=-1, keepdims=True) + EPS)
                yb_ref[rows, go:go + 512] = _c(hh * rr * nw_ref[:, go:go + 512])
            return carry

        lax.fori_loop(0, nch, chunk, 0)

    gvec = lambda w: pl.BlockSpec((1, ng * w), lambda g, i: (0, g))
    return pl.pallas_call(
        body, name="ssd_fwd",
        grid=(SSD_GROUPS // ng, nt),
        in_specs=[pl.BlockSpec((t, ng * FW_B), lambda g, i: (i, g)),
                  pl.BlockSpec((8, ng * FW_B), lambda g, i: (jnp.maximum(i * (t // 8) - 1, 0), g)),
                  gvec(128), gvec(128), gvec(512),
                  pl.BlockSpec((4, ng * 768), lambda g, i: (0, g)), gvec(768), gvec(512)],
        out_specs=[pl.BlockSpec((t, ng * SSD_GW), lambda g, i: (i, g)), pl.BlockSpec((t, ng * SSD_GW), lambda g, i: (i, g)),
                   pl.BlockSpec((nch, SSD_STATE, ng * SSD_GW), lambda g, i: (i, 0, g)),
                   pl.BlockSpec((t, ng * 768), lambda g, i: (i, g)),
                   pl.BlockSpec((t, ng * SSD_GW), lambda g, i: (i, g)), pl.BlockSpec((t, ng * SSD_GW), lambda g, i: (i, g))],
        out_shape=[jax.ShapeDtypeStruct((s, D_MODEL), F32), jax.ShapeDtypeStruct((s, D_MODEL), MXU_DTYPE),
                   jax.ShapeDtypeStruct((s // CHUNK, SSD_STATE, D_MODEL), F32),
                   jax.ShapeDtypeStruct((s, SSD_GROUPS * 768), F32),
                   jax.ShapeDtypeStruct((s, D_MODEL), F32), jax.ShapeDtypeStruct((s, D_MODEL), F32)],
        scratch_shapes=[pltpu.VMEM((ng, SSD_STATE, SSD_GW), F32), pltpu.VMEM((8, ng * 768), F32)],
        compiler_params=_cparams(("parallel", "arbitrary"), ng * (2 * t * FW_B * 4 + 16 * t * SSD_GW * 4) + 16 * 1024 * 1024),
    )(proj_b, proj_b, dtb, alog, dsk, cw, cb, nw)


def _ssd_bwd_call(proj_b, pre_all, dt_all, acs_all, dyb, y, hprev, dtb, alog, dsk, cw, nw, t, ng):
    s = proj_b.shape[0]
    nt, nch = s // t, t // CHUNK

    def body(pb_ref, pre_ref, dt_ref, acs_ref, dyb_ref, y_ref, hp_ref, dtb_ref, al_ref, ds_ref, cw_ref, nw_ref,
             dpb_ref, a512_ref, a768_ref, dht_ref, nxt_ref, q_ref, p1_ref):
        i = pl.program_id(1)

        @pl.when(i == 0)
        def _():
            dht_ref[...] = jnp.zeros_like(dht_ref)
            nxt_ref[...] = jnp.zeros_like(nxt_ref)
            a512_ref[...] = jnp.zeros_like(a512_ref)
            a768_ref[...] = jnp.zeros_like(a768_ref)

        _, trit = _cumsum_mats(t)
        a_neg = -jnp.exp(al_ref[...])
        masks = _ssd_masks()

        def chunk(cc, carry):
            c = nch - 1 - cc
            rows = pl.ds(pl.multiple_of(c * CHUNK, CHUNK), CHUNK)
            for gi in range(ng):
                fo, co, go, bo = FW_B * gi, 768 * gi, SSD_GW * gi, BW_B * gi
                pre = pre_ref[rows, co:co + 768]
                sp = _sigmoid(pre)
                act = pre * sp
                zb = pb_ref[rows, fo:fo + 512]
                yv = y_ref[rows, go:go + 512]
                sgz = _sigmoid(zb)
                sz = zb * sgz
                hh = yv * sz
                rr = lax.rsqrt(jnp.mean(hh * hh, axis=-1, keepdims=True) + EPS)
                dyb = dyb_ref[rows, go:go + 512].astype(F32)
                a512_ref[gi, 0] += _fold8(dyb * (hh * rr))
                tt = dyb * nw_ref[:, go:go + 512]
                dhh = rr * tt - hh * (rr * rr * rr) * jnp.mean(hh * tt, axis=-1, keepdims=True)
                dpb_ref[rows, bo:bo + 512] = _c(dhh * yv * (sgz * (1.0 + zb * (1.0 - sgz))))
                dxs, d_b, d_c, dht_prev, dyxs, qq, p1 = _ssd_chunk_bwd(
                    act[:, 0:512], act[:, 512:640], act[:, 640:768], dt_ref[rows, go:go + 512], acs_ref[rows, go:go + 512],
                    ds_ref[:, go:go + 512], hp_ref[c, :, go:go + 512], dht_ref[gi], dhh * sz, masks)
                dht_ref[gi] = dht_prev
                q_ref[rows, go:go + 512] = qq
                p1_ref[rows, go:go + 512] = p1
                a512_ref[gi, 1] += _fold8(dyxs)
                dpre = jnp.concatenate([dxs, d_b, d_c], axis=1) * (sp * (1.0 + pre * (1.0 - sp)))
                xbc = pb_ref[rows, fo + XBC_O:fo + DT_O]
                a768_ref[gi, 4] += _fold8(dpre)
                a768_ref[gi, 3] += _fold8(dpre * xbc)
                dpad = jnp.concatenate([dpre, nxt_ref[:, co:co + 768]], axis=0)
                dx = dpre * cw_ref[3:4, co:co + 768]
                for k in range(3):
                    d_k = pltpu.roll(dpad, CHUNK + 8 - (3 - k), 0)[0:CHUNK]
                    dx = dx + d_k * cw_ref[k:k + 1, co:co + 768]
                    a768_ref[gi, k] += _fold8(d_k * xbc)
                nxt_ref[:, co:co + 768] = dpre[0:8]
                dpb_ref[rows, bo + 512:bo + 1280] = _c(dx)
            return carry

        lax.fori_loop(0, nch, chunk, 0)

        rsel = _c(jnp.where(lax.shift_right_logical(_iota((SSD_GW, 128), 0), 6) == _iota((SSD_GW, 128), 1), 1.0, 0.0))
        for gi in range(ng):
            fo, go, bo, no = FW_B * gi, SSD_GW * gi, BW_B * gi, 128 * gi
            ddt, dadt = _ssd_finish_dt(q_ref[:, go:go + 512], p1_ref[:, go:go + 512], dt_ref[:, go:go + 512],
                                       a_neg[:, go:go + 512], trit, masks[3])
            sig_n = _sigmoid(pb_ref[:, fo + DT_O:fo + DT_O + 128] + dtb_ref[:, no:no + 128])
            ddtr_n = _dot01_r(ddt, rsel, 2) * sig_n
            dpb_ref[:, bo + DT_O:bo + DT_O + 128] = _c(ddtr_n)
            a512_ref[gi, 2] += _fold8(dadt)
            a512_ref[gi, 3, :, 0:128] += _fold8(ddtr_n)

    gvec = lambda w: pl.BlockSpec((1, ng * w), lambda g, i: (0, g))
    rev = lambda w: pl.BlockSpec((t, ng * w), lambda g, i: (nt - 1 - i, g))
    return pl.pallas_call(
        body, name="ssd_bwd",
        grid=(SSD_GROUPS // ng, nt),
        in_specs=[rev(FW_B), rev(768), rev(SSD_GW), rev(SSD_GW), rev(SSD_GW), rev(SSD_GW),
                  pl.BlockSpec((nch, SSD_STATE, ng * SSD_GW), lambda g, i: (nt - 1 - i, 0, g)),
                  gvec(128), gvec(512), gvec(512),
                  pl.BlockSpec((4, ng * 768), lambda g, i: (0, g)), gvec(512)],
        out_specs=[rev(BW_B),
                   pl.BlockSpec((ng, 4, 8, 512), lambda g, i: (g, 0, 0, 0)),
                   pl.BlockSpec((ng, 5, 8, 768), lambda g, i: (g, 0, 0, 0))],
        out_shape=[jax.ShapeDtypeStruct((s, SSD_GROUPS * BW_B), MXU_DTYPE),
                   jax.ShapeDtypeStruct((SSD_GROUPS, 4, 8, 512), F32),
                   jax.ShapeDtypeStruct((SSD_GROUPS, 5, 8, 768), F32)],
        scratch_shapes=[pltpu.VMEM((ng, SSD_STATE, SSD_GW), F32), pltpu.VMEM((8, ng * 768), F32),
                        pltpu.VMEM((t, ng * SSD_GW), F32), pltpu.VMEM((t, ng * SSD_GW), F32)],
        compiler_params=_cparams(("parallel", "arbitrary"), ng * (2 * t * FW_B * 4 + 18 * t * SSD_GW * 4) + 16 * 1024 * 1024),
    )(proj_b, pre_all, dt_all, acs_all, dyb, y, hprev, dtb, alog, dsk, cw, nw)


def _rows_call(body, ins, outs, tr, name):
    r = ins[0].shape[0]
    spec = lambda a: pl.BlockSpec((tr, a.shape[1]), lambda i: (i, 0))
    est = 2 * tr * sum(a.shape[1] * jnp.dtype(a.dtype).itemsize for a in list(ins) + list(outs))
    return pl.pallas_call(
        body, name=name, grid=(r // tr,),
        in_specs=[spec(a) for a in ins], out_specs=[spec(o) for o in outs], out_shape=list(outs),
        compiler_params=_cparams(("parallel",), est),
    )(*ins)


def _add_pair(a, b, tr, name):
    def body(a_ref, b_ref, o_ref):
        o_ref[...] = a_ref[...] + b_ref[...]

    return _rows_call(body, [a, b], [jax.ShapeDtypeStruct(a.shape, F32)], tr, name)[0]


def _rs_add(p, sib, place, tr, name):
    _, r, c = p.shape
    half = r // 2
    nb = half // tr

    def body(pl_ref, p_ref, s_ref, b_ref, own_ref):
        v = p_ref[0] + s_ref[0]
        b_ref[0] = v.astype(jnp.bfloat16)

        @pl.when(pl.program_id(1) == pl_ref[0])
        def _():
            own_ref[...] = v

    return pl.pallas_call(
        body, name=name,
        grid_spec=pltpu.PrefetchScalarGridSpec(
            num_scalar_prefetch=1, grid=(nb, 4),
            in_specs=[pl.BlockSpec((1, tr, c), lambda i, k, pr: (k, pr[1] * nb + i, 0)),
                      pl.BlockSpec((1, tr, c), lambda i, k, pr: (k, i, 0))],
            out_specs=[pl.BlockSpec((1, tr, c), lambda i, k, pr: (k, i, 0)),
                       pl.BlockSpec((tr, c), lambda i, k, pr: (i, 0))]),
        out_shape=[jax.ShapeDtypeStruct((4, half, c), jnp.bfloat16), jax.ShapeDtypeStruct((half, c), F32)],
        compiler_params=_cparams(("parallel", "arbitrary"), 2 * tr * c * 14),
    )(place, p, sib)


WIN_STEP = 3840
WIN_W = 3968


def _rs_add_windows(dw, sib, place, tr, name):
    r = dw.shape[0]
    half = r // 2
    nb = half // tr
    tail = WIN_W - WIN_STEP

    def body(pl_ref, pm_ref, pt_ref, s_ref, b_ref, own_ref):
        vm = pm_ref[...].astype(F32) + s_ref[0, :, 0:WIN_STEP].astype(F32)
        vt = pt_ref[...].astype(F32) + s_ref[0, :, WIN_STEP:WIN_W].astype(F32)
        b_ref[0, :, 0:WIN_STEP] = vm.astype(jnp.bfloat16)
        b_ref[0, :, WIN_STEP:WIN_W] = vt.astype(jnp.bfloat16)

        @pl.when(pl.program_id(1) == pl_ref[0])
        def _():
            own_ref[:, 0:WIN_STEP] = vm
            own_ref[:, WIN_STEP:WIN_W] = vt

    return pl.pallas_call(
        body, name=name,
        grid_spec=pltpu.PrefetchScalarGridSpec(
            num_scalar_prefetch=1, grid=(nb, 4),
            in_specs=[pl.BlockSpec((tr, WIN_STEP), lambda i, k, pr: (pr[1] * nb + i, k)),
                      pl.BlockSpec((tr, tail), lambda i, k, pr: (pr[1] * nb + i, (WIN_STEP // tail) * (k + 1))),
                      pl.BlockSpec((1, tr, WIN_W), lambda i, k, pr: (k, i, 0))],
            out_specs=[pl.BlockSpec((1, tr, WIN_W), lambda i, k, pr: (k, i, 0)),
                       pl.BlockSpec((tr, WIN_W), lambda i, k, pr: (i, 0))]),
        out_shape=[jax.ShapeDtypeStruct((4, half, WIN_W), jnp.bfloat16), jax.ShapeDtypeStruct((half, WIN_W), F32)],
        compiler_params=_cparams(("parallel", "arbitrary"), 2 * tr * WIN_W * 14),
    )(place, dw, dw, sib)


def _sum_own_recv(own, recv, tr, name):
    r, c = own.shape

    def body(o_ref, r_ref, out_ref):
        v = o_ref[...]
        for j in range(3):
            v = v + r_ref[j].astype(F32)
        out_ref[...] = v

    return pl.pallas_call(
        body, name=name, grid=(r // tr,),
        in_specs=[pl.BlockSpec((tr, c), lambda i: (i, 0)), pl.BlockSpec((3, tr, c), lambda i: (0, i, 0))],
        out_specs=pl.BlockSpec((tr, c), lambda i: (i, 0)),
        out_shape=jax.ShapeDtypeStruct((r, c), F32),
        compiler_params=_cparams(("parallel",), 2 * tr * c * 14),
    )(own, recv)


def _sum_slots(stack, name):
    n, r, w = stack.shape

    def body(a_ref, out_ref):
        v = a_ref[0]
        for k in range(1, n):
            v = v + a_ref[k]
        out_ref[...] = v

    return pl.pallas_call(
        body, name=name, grid=(1,),
        in_specs=[pl.BlockSpec((n, r, w), lambda i: (0, 0, 0))],
        out_specs=pl.BlockSpec((r, w), lambda i: (0, 0)),
        out_shape=jax.ShapeDtypeStruct((r, w), F32),
        compiler_params=_cparams(("arbitrary",), 2 * (n + 1) * r * w * 4),
    )(stack)


def _adamw(w, g, m, v, tr, name):
    def body(w_ref, g_ref, m_ref, v_ref, d_ref, nm_ref, nv_ref):
        d_ref[...], nm_ref[...], nv_ref[...] = _adam_math(w_ref[...], g_ref[...], m_ref[...], v_ref[...])

    o = jax.ShapeDtypeStruct(w.shape, F32)
    return _rows_call(body, [w, g, m, v], [o, o, o], tr, name)


def _adam_math(w, g, m, v):
    nm = ADAM_B1 * m + (1.0 - ADAM_B1) * g
    nv = ADAM_B2 * v + (1.0 - ADAM_B2) * (g * g)
    m_hat = nm / (1.0 - ADAM_B1 ** ADAM_STEP)
    v_hat = nv / (1.0 - ADAM_B2 ** ADAM_STEP)
    return -ADAM_LR * (m_hat / (jnp.sqrt(v_hat) + ADAM_EPS) + ADAM_WD * w), nm, nv


def _adamw_halves(w, g_own, g_sib, m, v, place, tr, name):
    r, c = w.shape

    def body(pl_ref, w_ref, go_ref, gs_ref, m_ref, v_ref, g_ref, d_ref, nm_ref, nv_ref):
        first = pl_ref[1] == 0
        own, sib = go_ref[...], gs_ref[...]
        g = jnp.concatenate([jnp.where(first, own, sib), jnp.where(first, sib, own)], axis=1)
        g_ref[...] = g
        d_ref[...], nm_ref[...], nv_ref[...] = _adam_math(w_ref[...], g, m_ref[...], v_ref[...])

    full = pl.BlockSpec((tr, c), lambda i, pr: (i, 0))
    half = pl.BlockSpec((tr, c // 2), lambda i, pr: (i, 0))
    o = jax.ShapeDtypeStruct((r, c), F32)
    return pl.pallas_call(
        body, name=name,
        grid_spec=pltpu.PrefetchScalarGridSpec(num_scalar_prefetch=1, grid=(r // tr,),
                                               in_specs=[full, half, half, full, full], out_specs=[full] * 4),
        out_shape=[o] * 4,
        compiler_params=_cparams(("parallel",), 2 * tr * c * 4 * 8),
    )(place, w, g_own, g_sib, m, v)


ANY = pl.BlockSpec(memory_space=pl.ANY)


def _place():
    x, y, c = lax.axis_index("x"), lax.axis_index("y"), lax.axis_index("c")
    others = [(1 - x, y), (x, 1 - y), (1 - x, 1 - y)]
    return x, y, c, 2 * x + y, others


def _remote(src, dst, send, recv, k, to):
    return pltpu.make_async_remote_copy(src_ref=src, dst_ref=dst, send_sem=send.at[k], recv_sem=recv.at[k],
                                        device_id=to, device_id_type=MESH)


def _norm_gather_call(x, norm_w, win_b, wout_b, cw8, tm):
    s = x.shape[0]
    ni = s // tm
    h_in, h_out = win_b.shape[0] // 2, wout_b.shape[0] // 2
    q_in = h_in // 2

    def body(x_ref, w_ref, win, wout, cw, xn_ref, xnt_ref, g_in, g_out, g_cw, send, recv):
        i = pl.program_id(0)

        def direct():
            xx, yy, c, me, others = _place()
            mi, mo = pl.ds(c * h_in, h_in), pl.ds(c * h_out, h_out)
            cps = [_remote(win.at[mi], g_in.at[me, mi], send, recv, j, (*others[j], c)) for j in range(2)]
            cps += [_remote(wout.at[mo], g_out.at[me, mo], send, recv, 7 + j, (*chip, c)) for j, chip in enumerate(others)]
            cps += [_remote(cw, g_cw.at[me], send, recv, 13 + j, (*chip, c)) for j, chip in enumerate(others)]
            return cps

        @pl.when(i == 0)
        def _():
            for cp in direct():
                cp.start()

        xv = x_ref[...]
        r = lax.rsqrt(jnp.mean(xv * xv, axis=-1, keepdims=True) + EPS)
        xn = xv * r * w_ref[...]
        xn_ref[...] = _c(xn)
        xnt_ref[...] = _c(xn.T)

        @pl.when(i == ni - 1)
        def _():
            xx, yy, c, me, others = _place()
            sib = (xx, yy, 1 - c)
            kx, ky, kd = (2 * chip[0] + chip[1] for chip in others)
            mi, ti = pl.ds(c * h_in, h_in), pl.ds((1 - c) * h_in, h_in)
            quarter = [pl.ds(c * h_in, q_in), pl.ds(c * h_in + q_in, q_in)]
            started = []

            def go(cp):
                cp.start()
                started.append(cp)

            _remote(g_in.at[kx, mi], g_in.at[kx, mi], send, recv, 0, (*others[0], c)).wait_recv()
            go(_remote(g_in.at[kx, quarter[0]], g_in.at[kx, quarter[0]], send, recv, 2, (*others[1], c)))
            go(_remote(g_in.at[kx, mi], g_in.at[kx, mi], send, recv, 4, sib))
            _remote(g_in.at[ky, mi], g_in.at[ky, mi], send, recv, 1, (*others[1], c)).wait_recv()
            go(_remote(g_in.at[ky, quarter[1]], g_in.at[ky, quarter[1]], send, recv, 3, (*others[0], c)))
            go(_remote(g_in.at[ky, mi], g_in.at[ky, mi], send, recv, 5, sib))
            mo, to = pl.ds(c * h_out, h_out), pl.ds((1 - c) * h_out, h_out)
            for j, chip in enumerate(others):
                kj = 2 * chip[0] + chip[1]
                _remote(g_out.at[kj, mo], g_out.at[kj, mo], send, recv, 7 + j, (*chip, c)).wait_recv()
                go(_remote(g_out.at[kj, mo], g_out.at[kj, mo], send, recv, 10 + j, sib))
            _remote(g_in.at[kd, quarter[0]], g_in.at[kd, quarter[0]], send, recv, 2, (*others[1], c)).wait_recv()
            _remote(g_in.at[kd, quarter[1]], g_in.at[kd, quarter[1]], send, recv, 3, (*others[0], c)).wait_recv()
            go(_remote(g_in.at[kd, mi], g_in.at[kd, mi], send, recv, 6, sib))
            for k_src, sem in ((kx, 4), (ky, 5), (kd, 6)):
                _remote(g_in.at[k_src, ti], g_in.at[k_src, ti], send, recv, sem, sib).wait_recv()
            for j, chip in enumerate(others):
                kj = 2 * chip[0] + chip[1]
                _remote(g_out.at[kj, to], g_out.at[kj, to], send, recv, 10 + j, sib).wait_recv()
                _remote(cw, g_cw.at[kj], send, recv, 13 + j, (*chip, c)).wait_recv()
            for cp in direct() + started:
                cp.wait_send()

    outs = [jax.ShapeDtypeStruct((s, D_MODEL), MXU_DTYPE), jax.ShapeDtypeStruct((D_MODEL, s), MXU_DTYPE)]
    outs += [jax.ShapeDtypeStruct((4,) + a.shape, a.dtype) for a in (win_b, wout_b, cw8)]
    return pl.pallas_call(
        body, name="rmsnorm_gather_weights",
        grid=(ni,),
        in_specs=[pl.BlockSpec((tm, D_MODEL), lambda i: (i, 0)), pl.BlockSpec((1, D_MODEL), lambda i: (0, 0)), ANY, ANY, ANY],
        out_specs=[pl.BlockSpec((tm, D_MODEL), lambda i: (i, 0)), pl.BlockSpec((D_MODEL, tm), lambda i: (0, i)), ANY, ANY, ANY],
        out_shape=outs,
        scratch_shapes=[pltpu.SemaphoreType.DMA((16,)), pltpu.SemaphoreType.DMA((16,))],
        compiler_params=_cparams(("arbitrary",), 2 * tm * D_MODEL * 12),
    )(x, norm_w, win_b, wout_b, cw8)


def _dw_out_rs_call(a, b, dw, *, tn, tk):
    m, k = a.shape
    n = b.shape[1]
    nj, nk = n // tn, k // tk
    half = dw.shape[0] // 2

    def body(a_ref, b_ref, pin, o_ref, sib_in, send, recv):
        j, kk = pl.program_id(0), pl.program_id(1)

        def copies():
            x, y, c, me, others = _place()
            rows = pl.ds((1 - c) * half, half)
            return [_remote(pin.at[rows, pl.ds(WIN_STEP * w, WIN_W)], sib_in.at[w], send, recv, w, (x, y, 1 - c)) for w in range(4)]

        @pl.when((j == 0) & (kk == 0))
        def _():
            for cp in copies():
                cp.start()

        @pl.when(kk == 0)
        def _():
            o_ref[...] = jnp.zeros_like(o_ref)

        o_ref[...] += _dot(a_ref[...], b_ref[...])

        @pl.when((j == nj - 1) & (kk == nk - 1))
        def _():
            cps = copies()
            for cp in cps:
                cp.wait_recv()
            for cp in cps:
                cp.wait_send()

    isz = jnp.dtype(a.dtype).itemsize
    est = 2 * (m * tk + tk * tn) * isz + 2 * m * tn * 4
    outs = [jax.ShapeDtypeStruct((m, n), F32), jax.ShapeDtypeStruct((4, half, WIN_W), dw.dtype)]
    return pl.pallas_call(
        body, name="dw_out_rs_sibling",
        grid=(nj, nk),
        in_specs=[pl.BlockSpec((m, tk), lambda j, kk: (0, kk)), pl.BlockSpec((tk, tn), lambda j, kk: (kk, j)), ANY],
        out_specs=[pl.BlockSpec((m, tn), lambda j, kk: (0, j)), ANY],
        out_shape=outs,
        scratch_shapes=[pltpu.SemaphoreType.DMA((4,)), pltpu.SemaphoreType.DMA((4,))],
        compiler_params=_cparams(("arbitrary", "arbitrary"), est),
    )(a, b, dw)


def _rs_sibling_call(p_out, vsmall):
    def body(pout, vs, sib_out, sib_v, send, recv):
        x, y, c, me, others = _place()
        sib = (x, y, 1 - c)
        half = pout.shape[1] // 2
        cps = [_remote(pout.at[:, pl.ds((1 - c) * half, half)], sib_out, send, recv, 0, sib),
               _remote(vs, sib_v, send, recv, 1, sib)]
        for cp in cps:
            cp.start()
        for cp in cps:
            cp.wait_recv()
        for cp in cps:
            cp.wait_send()

    outs = [jax.ShapeDtypeStruct((4, p_out.shape[1] // 2, p_out.shape[2]), p_out.dtype),
            jax.ShapeDtypeStruct(vsmall.shape, vsmall.dtype)]
    return pl.pallas_call(
        body, name="rs_sibling",
        in_specs=[ANY] * 2, out_specs=[ANY] * 2, out_shape=outs,
        scratch_shapes=[pltpu.SemaphoreType.DMA((2,)), pltpu.SemaphoreType.DMA((2,))],
    )(p_out, vsmall)


def _rs_join_call(f_in, f_out, nw8):
    def body(fin, fout, nw, sib_in, full_out, all_nw, send, recv):
        x, y, c, me, others = _place()
        sib = (x, y, 1 - c)
        half = fout.shape[0]
        cps = [_remote(fin, sib_in, send, recv, 0, sib),
               _remote(fout, full_out.at[pl.ds(c * half, half)], send, recv, 1, sib)]
        mine = 4 * x + 2 * y + c
        peers = []
        for r in range(1, 8):
            px, py, pc = (1 - x if r & 4 else x), (1 - y if r & 2 else y), (1 - c if r & 1 else c)
            peers.append((r, (px, py, pc), 4 * px + 2 * py + pc))
            cps.append(_remote(nw, all_nw.at[mine], send, recv, 1 + r, (px, py, pc)))
        for cp in cps:
            cp.start()
        cps[0].wait_recv()
        _remote(fout, full_out.at[pl.ds((1 - c) * half, half)], send, recv, 1, sib).wait_recv()
        for r, peer, idx in peers:
            _remote(nw, all_nw.at[idx], send, recv, 1 + r, peer).wait_recv()
        for cp in cps:
            cp.wait_send()

    outs = [jax.ShapeDtypeStruct(f_in.shape, F32), jax.ShapeDtypeStruct((2 * f_out.shape[0], f_out.shape[1]), F32),
            jax.ShapeDtypeStruct((8,) + nw8.shape, F32)]
    return pl.pallas_call(
        body, name="rs_join",
        in_specs=[ANY] * 3, out_specs=[ANY] * 3, out_shape=outs,
        scratch_shapes=[pltpu.SemaphoreType.DMA((9,)), pltpu.SemaphoreType.DMA((9,))],
    )(f_in, f_out, nw8)


def _pack(arrs):
    parts = []
    for a in arrs:
        f = a.reshape(-1).astype(F32)
        pad = (-f.shape[0]) % 1024
        parts.append(jnp.pad(f, (0, pad)).reshape(-1, 128))
    return jnp.concatenate(parts, axis=0)


def _unpack(packed, shapes):
    out, row = [], 0
    for shp in shapes:
        n = 1
        for d in shp:
            n *= d
        rows = (n + 1023) // 1024 * 8
        out.append(packed[row:row + rows].reshape(-1)[:n].reshape(shp))
        row += rows
    return out


def _expand_heads(v32):
    return jnp.repeat(v32.reshape(32), HEADDIM).reshape(1, D_MODEL)


def kernel(x, norm_w, w_in, gate_b, sgu_norm_g, sgu_norm_b, sgu_w, sgu_b, conv_w, conv_b, dt_bias, A_log, D_skip, ssd_norm_w, w_out, final_norm_w, loss_target, m_norm_w, m_w_in, m_gate_b, m_sgu_norm_g, m_sgu_norm_b, m_sgu_w, m_sgu_b, m_conv_w, m_conv_b, m_dt_bias, m_A_log, m_D_skip, m_ssd_norm_w, m_w_out, m_final_norm_w, v_norm_w, v_w_in, v_gate_b, v_sgu_norm_g, v_sgu_norm_b, v_sgu_w, v_sgu_b, v_conv_w, v_conv_b, v_dt_bias, v_A_log, v_D_skip, v_ssd_norm_w, v_w_out, v_final_norm_w):
    s = x.shape[1]
    x2 = x.reshape(s, D_MODEL)
    tgt = loss_target.reshape(s, D_MODEL)
    t_ssd, t_tok, t_out, t_row = min(T_SSD, s), min(T_TOK, s), min(T_OUT, s), min(T_ROW, s)
    tm_mm, tk_dw = min(TM_MM, s), min(TK_DW, s)
    chip = 2 * lax.axis_index("x") + lax.axis_index("y")

    cw8 = jnp.pad(conv_w[0], ((0, 4), (0, 0)))
    win_b, wout_b = _c(w_in[0]), _c(w_out[0])
    xn, xnt, g_in, g_out, g_cw = _norm_gather_call(x2, norm_w, win_b, wout_b, cw8, t_row)
    g_in = lax.dynamic_update_index_in_dim(g_in, win_b, chip, 0)
    g_out = lax.dynamic_update_index_in_dim(g_out, wout_b, chip, 0)
    g_cw = lax.dynamic_update_index_in_dim(g_cw, cw8, chip, 0)
    wt = jnp.transpose(g_in, (0, 2, 1)).reshape(IN_W, D_MODEL)
    w_out_full = g_out.reshape(D_MODEL, D_MODEL)
    conv_w_full = jnp.transpose(g_cw[:, 0:4, :], (1, 0, 2)).reshape(4, 3072)

    wt_a = jnp.concatenate([wt[0:6144], wt[11296:15392]], axis=0)
    per_group = lambda lo, n: wt[lo:lo + SSD_GROUPS * n].reshape(SSD_GROUPS, n, D_MODEL)
    wt_b = jnp.concatenate([per_group(6144, 512), per_group(8192, 512), per_group(10240, 128), per_group(10752, 128),
                            jnp.pad(per_group(11264, 8), ((0, 0), (0, 120), (0, 0)))], axis=1)
    wt_b = wt_b.reshape(SSD_GROUPS * BW_B, D_MODEL)

    def group_cols(full_xs, full_bc):
        parts = []
        for g in range(SSD_GROUPS):
            parts += [full_xs[:, 512 * g:512 * g + 512], full_bc[:, 128 * g:128 * g + 128], full_bc[:, 512 + 128 * g:512 + 128 * g + 128]]
        return jnp.concatenate(parts, axis=1)

    cw_g = group_cols(conv_w_full[:, 0:2048], conv_w_full[:, 2048:3072])
    cb_g = group_cols(conv_b[:, 0:2048], conv_b[:, 2048:3072])
    alog_e, dsk_e = _expand_heads(A_log), _expand_heads(D_skip)
    narrow = lambda v32: jnp.pad(v32.reshape(SSD_GROUPS, 8), ((0, 0), (0, 120))).reshape(1, SSD_GROUPS * 128)
    dtb_n, alog_n = narrow(dt_bias), narrow(A_log)

    pos_chunk = jnp.arange(SGU_BLOCK) // CHUNK
    smask = pos_chunk[None, :] <= pos_chunk[:, None]
    wm_f = jnp.where(smask[None], sgu_w[0], 0.0)
    wm = _c(wm_f)
    wmt = _c(jnp.transpose(wm_f, (0, 2, 1)))
    bias_full = jnp.repeat(sgu_b[0].T, D_MODEL // SGU_GROUPS, axis=1)
    fnw = final_norm_w.reshape(1, D_MODEL)

    proj_a = _mm(xn, wt_a, tm=min(2 * tm_mm, s), tn=2048, tk=D_MODEL, name="in_proj_a", out_dtype=MXU_DTYPE, b_is_t=True)
    proj_b = _mm(xn, wt_b, tm=tm_mm, tn=BW_B, tk=D_MODEL, name="in_proj_b", b_is_t=True)
    y_ssd, y_b, hprev, pre_all, dt_all, acs_all = _ssd_fwd_call(proj_b, dtb_n, alog_n, dsk_e, cw_g, cb_g, ssd_norm_w, t_ssd, NG_SSD)
    y_a, merged, merged_t = _tok_fwd_call(proj_a, y_b, gate_b, sgu_norm_g, sgu_norm_b, wm, bias_full, t_tok)
    dh_b, dmerged, loss_t, dfw8 = _out_call(merged, x2, tgt, w_out_full, fnw, t_out)

    dproj_a, dy_b, dgb8, dgam8, dbeta8, dbfull, dws = _tok_bwd_call(
        proj_a, dmerged, y_a, y_b, gate_b, sgu_norm_g, sgu_norm_b, wm, wmt, bias_full, t_tok)
    dproj_b, a512, a768 = _ssd_bwd_call(proj_b, pre_all, dt_all, acs_all, dy_b, y_ssd, hprev, dtb_n, alog_e, dsk_e, cw_g,
                                        ssd_norm_w, t_ssd, NG_SSD)
    dw_uvz = _mm(xnt, dproj_a, tm=D_MODEL, tn=1024, tk=min(2 * tk_dw, s), name="dw_in_uvz", n=6144, out_dtype=MXU_DTYPE)
    dw_gate = _mm(xnt, dproj_a, tm=D_MODEL, tn=1024, tk=min(2 * tk_dw, s), name="dw_in_gate", col0=6, n=4096,
                  out_dtype=MXU_DTYPE)
    dw_zb, dw_xs, dw_bm, dw_cm, dw_dt = _dw_groups(xnt, dproj_b, tk=tk_dw, out_dtype=MXU_DTYPE)

    dw_dt32 = jnp.concatenate([dw_dt[:, 128 * g:128 * g + 8] for g in range(SSD_GROUPS)], axis=1)
    dw_ref = jnp.concatenate([dw_uvz, dw_zb, dw_xs, dw_bm, dw_cm, dw_dt32, dw_gate,
                              jnp.zeros((D_MODEL, 3 * WIN_STEP + WIN_W - IN_W), MXU_DTYPE)], axis=1)
    dw_out_p, sib_i = _dw_out_rs_call(merged_t, dh_b, dw_ref, tn=1024, tk=tk_dw)
    p_out = dw_out_p.reshape(4, D_MODEL // 4, D_MODEL)

    s512 = jnp.sum(a512, axis=2)
    heads = lambda v: jnp.sum(v.reshape(32, HEADDIM), axis=1).reshape(1, 32)
    d_ssd_nw = s512[:, 0].reshape(1, D_MODEL)
    d_dskip = heads(s512[:, 1].reshape(D_MODEL))
    d_alog = heads(s512[:, 2].reshape(D_MODEL)) * (1.0 / HEADDIM) * (-jnp.exp(A_log))
    d_dtb = s512[:, 3, 0:8].reshape(1, 32)
    s768 = jnp.sum(a768, axis=2)
    ungroup = lambda v: jnp.concatenate([v[g, :, 0:512] for g in range(4)] + [v[g, :, 512:640] for g in range(4)]
                                        + [v[g, :, 640:768] for g in range(4)], axis=1)
    d_cw = ungroup(s768[:, 0:4])
    d_cb = ungroup(s768[:, 4:5])
    d_sgu_b = jnp.sum(dbfull.reshape(128, SGU_GROUPS, 128), axis=2).T.reshape(1, SGU_GROUPS, 128)
    d_sgu_w = jnp.where(smask[None], dws, 0.0).reshape(1, SGU_GROUPS, 128, 128)
    fold = lambda a8: jnp.sum(a8, axis=0, keepdims=True)
    small_local = [fold(dgb8), fold(dgam8), fold(dbeta8), d_sgu_w, d_sgu_b, d_cw, d_cb,
                   d_dtb, d_alog, d_dskip, d_ssd_nw, fold(dfw8).reshape(D_MODEL), jnp.sum(loss_t[:, 0, 0]).reshape(1)]
    small_shapes = [a.shape for a in small_local]
    v_local = _pack(small_local)

    core = lax.axis_index("c")
    place = jnp.stack([chip, core]).astype(jnp.int32)
    hr_o = D_MODEL // 8
    sib_o, sib_v = _rs_sibling_call(p_out, v_local)
    s1b_i, o_i = _rs_add_windows(dw_ref, sib_i, place, 256, "rs_add_in")
    s1b_o, o_o = _rs_add(p_out, sib_o, place, 256, "rs_add_out")
    chip_v = _add_pair(v_local, sib_v, v_local.shape[0], "ar_add_small")
    dxn, r_i, r_o, abs_v = _dx_rs_call(dproj_a, wt_a, dproj_b, wt_b, s1b_i, s1b_o, chip_v, tm=tm_mm)
    grad_x, dnw8 = _gradx_call(x2, dxn, dh_b, norm_w, t_row)
    abs_v = lax.dynamic_update_index_in_dim(abs_v, chip_v, chip, 0)
    f_i = _sum_own_recv(o_i, r_i, 256, "rs_sum_in")
    f_o = _sum_own_recv(o_o, r_o, 256, "rs_sum_out")
    sib_f_i, g_w_out, all_nw = _rs_join_call(f_i, f_o, dnw8)
    g_w_out = lax.dynamic_update_slice_in_dim(g_w_out, f_o, core * hr_o, axis=0)
    all_nw = lax.dynamic_update_index_in_dim(all_nw, dnw8, 2 * chip + core, 0)
    g_nw = fold(_sum_slots(all_nw, "ar_sum_norm_w"))
    total_v = _sum_slots(abs_v, "ar_sum_small")
    (g_gb, g_gam, g_beta, g_sw, g_sb, g_cw_full, g_cb, g_dtb, g_alog, g_dsk, g_snw, g_fnw, loss1) = _unpack(total_v, small_shapes)
    g_cw_shard = lax.dynamic_slice(g_cw_full, (0, chip * 768), (4, 768)).reshape(1, 4, 768)
    loss = loss1.reshape(())

    shard_t = lambda win: lax.dynamic_slice_in_dim(win, 8 * chip, SHARD_W, axis=1).T
    g_w_in, d_win, nm_win, nv_win = (a.T for a in _adamw_halves(w_in[0].T, shard_t(f_i), shard_t(sib_f_i), m_w_in[0].T,
                                                                v_w_in[0].T, place, 296, "adamw_w_in"))
    d_wout, nm_wout, nv_wout = _adamw(w_out[0], g_w_out, m_w_out[0], v_w_out[0], 128, "adamw_w_out")
    small_w = [norm_w, gate_b, sgu_norm_g, sgu_norm_b, sgu_w, sgu_b, conv_w, conv_b, dt_bias, A_log, D_skip, ssd_norm_w, final_norm_w]
    small_m = [m_norm_w, m_gate_b, m_sgu_norm_g, m_sgu_norm_b, m_sgu_w, m_sgu_b, m_conv_w, m_conv_b, m_dt_bias, m_A_log, m_D_skip, m_ssd_norm_w, m_final_norm_w]
    small_v = [v_norm_w, v_gate_b, v_sgu_norm_g, v_sgu_norm_b, v_sgu_w, v_sgu_b, v_conv_w, v_conv_b, v_dt_bias, v_A_log, v_D_skip, v_ssd_norm_w, v_final_norm_w]
    small_g = [g_nw, g_gb, g_gam, g_beta, g_sw, g_sb, g_cw_shard, g_cb, g_dtb, g_alog, g_dsk, g_snw, g_fnw]
    shapes_w = [a.shape for a in small_w]
    small_g = [a.reshape(shp) for a, shp in zip(small_g, shapes_w)]
    pw = _pack(small_w)
    pd, pm, pv = _adamw(pw, _pack(small_g), _pack(small_m), _pack(small_v), pw.shape[0], "adamw_small")
    d_small, nm_small, nv_small = _unpack(pd, shapes_w), _unpack(pm, shapes_w), _unpack(pv, shapes_w)

    def with_big(small, win, wout):
        o = list(small)
        return o[0:1] + [win.reshape(1, D_MODEL, SHARD_W)] + o[1:12] + [wout.reshape(1, D_MODEL // 4, D_MODEL)] + o[12:13]

    grads = with_big(small_g, g_w_in, g_w_out)
    deltas = with_big(d_small, d_win, d_wout)
    new_m = with_big(nm_small, nm_win, nm_wout)
    new_v = with_big(nv_small, nv_win, nv_wout)
    return (loss, grad_x.reshape(1, s, D_MODEL), *grads, *deltas, *new_m, *new_v)
```

```python
import functools

import jax
import jax.numpy as jnp
from jax import lax
from jax.experimental import pallas as pl
from jax.experimental.pallas import tpu as pltpu

F32 = jnp.float32
MXU_DTYPE = jnp.bfloat16

D_MODEL = 2048
EPS = 1e-5
CHUNK = 64
SGU_BLOCK = 128
SGU_GROUPS = 16
SSD_GROUPS = 4
SSD_GW = 512
SSD_STATE = 128
HEADDIM = 64
IN_W = 15392
SHARD_W = IN_W // 4
BW_B = 1408
FW_B = BW_B
XBC_O, DT_O = 512, 1280
NA = 10240
DW_ZB_O, DW_DT_O = 6144, 11264
DW_ALL_W = 15488

ADAM_LR = 0.001
ADAM_B1 = 0.9
ADAM_B2 = 0.999
ADAM_EPS = 1e-08
ADAM_WD = 0.01
ADAM_STEP = 10

T_SSD = 256
NG_SSD = 4
T_TOK = 128
T_OUT = 512
T_ROW = 512
TM_MM = 1024
TK_DW = 1024
VMEM_CAP = 60 * 1024 * 1024
MESH = pl.DeviceIdType.MESH


def _cparams(sem, est_bytes):
    lim = int(min(VMEM_CAP, max(32 * 1024 * 1024, est_bytes + 12 * 1024 * 1024)))
    return pltpu.CompilerParams(dimension_semantics=sem, vmem_limit_bytes=lim)


def _c(x):
    return x.astype(MXU_DTYPE)


def _dot(a, b):
    return jnp.dot(a, b, preferred_element_type=F32)


def _dot_nt(a, b):
    return lax.dot_general(a, b, (((1,), (1,)), ((), ())), preferred_element_type=F32)


def _dot_tn(a, b):
    return lax.dot_general(a, b, (((0,), (0,)), ((), ())), preferred_element_type=F32)


def _split(x, n):
    parts, r = [], x
    for _ in range(n):
        p = _c(r)
        parts.append(p)
        r = r - p.astype(F32)
    return parts


def _dot01_l(m01, x, n):
    acc = None
    for p in _split(x, n):
        t = _dot(m01, p)
        acc = t if acc is None else acc + t
    return acc


def _dot01_r(x, m01, n):
    acc = None
    for p in _split(x, n):
        t = _dot(p, m01)
        acc = t if acc is None else acc + t
    return acc


def _sigmoid(x):
    return 0.5 * jnp.tanh(0.5 * x) + 0.5


def _fold8(x):
    r, w = x.shape
    return jnp.sum(x.reshape(r // 8, 8, w), axis=0)


def _iota(shape, dim):
    return lax.broadcasted_iota(jnp.int32, shape, dim)


def _ssd_masks():
    l64 = _iota((CHUNK, SSD_GW), 0)
    s64 = jnp.bitwise_and(_iota((CHUNK, SSD_GW), 1), CHUNK - 1)
    diag = l64 == s64
    causal = l64 >= s64
    row_last = l64 == CHUNK - 1
    r4 = lax.shift_right_logical(_iota((256, 256), 0), 6)
    c4 = lax.shift_right_logical(_iota((256, 256), 1), 6)
    mask4 = r4 == c4
    return diag, causal, row_last, mask4


def _cumsum_mats(t):
    r, c = _iota((t, t), 0), _iota((t, t), 1)
    same = lax.shift_right_logical(r, 6) == lax.shift_right_logical(c, 6)
    tri = _c(jnp.where(same, jnp.where(c <= r, 1.0, 0.0), 0.0))
    trit = _c(jnp.where(same, jnp.where(c >= r, 1.0, 0.0), 0.0))
    return tri, trit


def _head_expand_mat():
    return _c(jnp.where(_iota((128, SSD_GW), 0) == lax.shift_right_logical(_iota((128, SSD_GW), 1), 6), 1.0, 0.0))


def _ssd_common(xs, bm, cm, dt, acs, masks):
    diag, causal, row_last, mask4 = masks
    row_e = jnp.sum(jnp.where(diag, acs, 0.0), axis=0, keepdims=True)
    seg = acs - row_e
    lm = jnp.exp(jnp.where(causal, seg, -1e30))
    bb, cb = _c(bm), _c(cm)
    brep = jnp.concatenate([bb] * 8, axis=0)
    cbrep = _dot_nt(cb, brep)
    m = cbrep * lm
    xdt = xs * dt
    acs_last = jnp.sum(jnp.where(row_last, acs, 0.0), axis=0, keepdims=True)
    dec = jnp.exp(acs_last - acs)
    eacs = jnp.exp(acs)
    cd = jnp.exp(acs_last)
    return dict(lm=lm, bb=bb, cb=cb, brep=brep, m=m, xdt=xdt, dec=dec, eacs=eacs, cd=cd)


def _blockdiag4(xb, mask4):
    return jnp.where(mask4, jnp.concatenate([xb] * 4, axis=0), jnp.zeros((), xb.dtype))


def _ssd_chunk_fwd(xs, bm, cm, dt, acs, d_skip, ht, masks):
    q = _ssd_common(xs, bm, cm, dt, acs, masks)
    mask4 = masks[3]
    mb, xdtb = _c(q["m"]), _c(q["xdt"])
    yd = []
    for blk in range(2):
        sl = slice(256 * blk, 256 * blk + 256)
        yd.append(_dot(mb[:, sl], _blockdiag4(xdtb[:, sl], mask4)))
    y_diag = jnp.concatenate(yd, axis=1)
    p = _dot(q["cb"], _c(ht))
    y = y_diag + p * q["eacs"] + xs * d_skip
    st = _dot_tn(q["bb"], _c(q["xdt"] * q["dec"]))
    return y, ht * q["cd"] + st


def _ssd_chunk_bwd(xs, bm, cm, dt, acs, d_skip, hprev, dht, dy, masks):
    diag, causal, row_last, mask4 = masks
    q = _ssd_common(xs, bm, cm, dt, acs, masks)
    lm, bb, cb, brep, m, xdt, dec, eacs, cd = (q[k] for k in ("lm", "bb", "cb", "brep", "m", "xdt", "dec", "eacs", "cd"))
    hb = _c(hprev)
    yoff = _dot(cb, hb) * eacs
    dyb = _c(dy)
    dpb = _c(dy * eacs)
    d_c = _dot_nt(dpb, hb)
    dh_y = _dot_tn(cb, dpb)
    mb, xdtb = _c(m), _c(xdt)
    dm_parts, dxdt_parts = [], []
    for blk in range(2):
        sl = slice(256 * blk, 256 * blk + 256)
        bd = _blockdiag4(xdtb[:, sl], mask4)
        dm_parts.append(_dot_nt(dyb[:, sl], bd))
        dxf = jnp.where(mask4, _dot_tn(mb[:, sl], dyb[:, sl]), 0.0)
        dxdt_parts.append(dxf[0:64] + dxf[64:128] + dxf[128:192] + dxf[192:256])
    dm = jnp.concatenate(dm_parts, axis=1)
    dxdt = jnp.concatenate(dxdt_parts, axis=1)
    dcbb = _c(dm * lm)
    g = dm * m
    d_c = d_c + _dot(dcbb, brep)
    dbrep = _dot_tn(dcbb, cb)
    d_b = dbrep[0:64]
    for r in range(1, 8):
        d_b = d_b + dbrep[64 * r:64 * r + 64]
    dhtb = _c(dht)
    dxd = _dot(bb, dhtb)
    xd = xdt * dec
    dxdt = dxdt + dxd * dec
    tq = dxd * xd
    d_b = d_b + _dot_nt(_c(xd), dhtb)
    dcd = jnp.sum(dht * hprev, axis=0, keepdims=True)
    col_g = jnp.sum(g, axis=0, keepdims=True)
    last = jnp.sum(tq, axis=0, keepdims=True) + dcd * cd
    qq = g - jnp.where(diag, col_g, 0.0) + dy * yoff - tq + jnp.where(row_last, last, 0.0)
    dxs = dxdt * dt + dy * d_skip
    return dxs, d_b, d_c, dht * cd + dh_y, dy * xs, qq, dxdt * xs


def _ssd_finish_dt(qq, p1, dt, a_neg, trit, mask4):
    bd4 = _c(jnp.where(mask4, 1.0, 0.0))
    dacs = jnp.concatenate([_dot01_r(qq[:, 256 * b:256 * b + 256], bd4, 2) for b in range(2)], axis=1)
    da = _dot01_l(trit, dacs, 2)
    return p1 + da * (a_neg * (1.0 / HEADDIM)), da * dt


def _softplus(x):
    return jnp.maximum(x, 0.0) + jnp.log(1.0 + jnp.exp(-jnp.abs(x)))


def _conv_taps(xpad, t):
    taps = []
    for k in range(4):
        sh = 3 - k
        v = xpad if sh == 0 else pltpu.roll(xpad, sh, 0)
        taps.append(v[8:8 + t])
    return taps


def _mm(a, b, *, tm, tn, tk, name, out_dtype=F32, col0=0, n=None, b_is_t=False, into=None):
    m, k = a.shape
    n = b.shape[0 if b_is_t else 1] if n is None else n
    nk = k // tk
    assert m % tm == 0 and n % tn == 0 and k % tk == 0, (a.shape, b.shape, tm, tn, tk)
    dot = _dot_nt if b_is_t else _dot
    via_acc = nk > 1 and out_dtype != F32

    def body(a_ref, b_ref, *rest):
        o_ref, acc = (rest[-2], rest[-1:]) if via_acc else (rest[-1], ())
        if nk == 1:
            o_ref[...] = dot(a_ref[...], b_ref[...]).astype(out_dtype)
            return
        acc_ref = acc[0] if via_acc else o_ref

        @pl.when(pl.program_id(2) == 0)
        def _():
            acc_ref[...] = jnp.zeros_like(acc_ref)

        acc_ref[...] += dot(a_ref[...], b_ref[...])
        if via_acc:
            @pl.when(pl.program_id(2) == nk - 1)
            def _():
                o_ref[...] = acc_ref[...].astype(out_dtype)

    isz = jnp.dtype(a.dtype).itemsize
    est = 2 * (tm * tk + tk * tn) * isz + 3 * tm * tn * 4
    return pl.pallas_call(
        body, name=name,
        grid=(m // tm, n // tn, nk),
        in_specs=[pl.BlockSpec((tm, tk), lambda i, j, kk: (i, kk)),
                  pl.BlockSpec((tn, tk), lambda i, j, kk: (j + col0, kk)) if b_is_t
                  else pl.BlockSpec((tk, tn), lambda i, j, kk: (kk, j + col0))]
                 + ([] if into is None else [pl.BlockSpec(memory_space=pl.ANY)]),
        out_specs=pl.BlockSpec((tm, tn), lambda i, j, kk: (i, j)),
        out_shape=jax.ShapeDtypeStruct((m, n) if into is None else into.shape, out_dtype),
        input_output_aliases={} if into is None else {2: 0},
        scratch_shapes=[pltpu.VMEM((tm, tn), F32)] if via_acc else [],
        compiler_params=_cparams(("parallel", "parallel", "arbitrary"), est),
    )(*((a, b) if into is None else (a, b, into)))


def _dw_groups(xnt, dpb, *, tk, out_dtype):
    m, k = xnt.shape
    nk = k // tk
    pieces = ((0, 512, DW_ZB_O), (512, 512, DW_ZB_O + 2048), (1024, 128, DW_ZB_O + 4096), (1152, 128, DW_ZB_O + 4608))

    def body(a_ref, b_ref, all_ref, dt_ref, acc_ref, stage_ref, sems):
        @pl.when(pl.program_id(1) == 0)
        def _():
            acc_ref[...] = jnp.zeros_like(acc_ref)

        acc_ref[...] += _dot(a_ref[...], b_ref[...])

        @pl.when(pl.program_id(1) == nk - 1)
        def _():
            stage_ref[...] = acc_ref[:, 0:DT_O].astype(out_dtype)
            dt_ref[...] = acc_ref[:, DT_O:BW_B].astype(out_dtype)
            for g in range(SSD_GROUPS):
                @pl.when(pl.program_id(0) == g)
                def _():
                    copies = [pltpu.make_async_copy(stage_ref.at[:, lo:lo + w], all_ref.at[:, base + w * g:base + w * (g + 1)],
                                                    sems.at[i]) for i, (lo, w, base) in enumerate(pieces)]
                    for c in copies:
                        c.start()
                    for c in copies:
                        c.wait()

    isz, osz = jnp.dtype(xnt.dtype).itemsize, jnp.dtype(out_dtype).itemsize
    est = 2 * (m * tk + tk * BW_B) * isz + m * BW_B * (4 + 2 * osz)
    return pl.pallas_call(
        body, name="dw_in_b",
        grid=(SSD_GROUPS, nk),
        in_specs=[pl.BlockSpec((m, tk), lambda g, kk: (0, kk)), pl.BlockSpec((tk, BW_B), lambda g, kk: (kk, g))],
        out_specs=[pl.BlockSpec(memory_space=pl.ANY), pl.BlockSpec((m, 128), lambda g, kk: (0, g))],
        out_shape=[jax.ShapeDtypeStruct((m, DW_ALL_W), out_dtype), jax.ShapeDtypeStruct((m, 512), out_dtype)],
        scratch_shapes=[pltpu.VMEM((m, BW_B), F32), pltpu.VMEM((m, DT_O), out_dtype), pltpu.SemaphoreType.DMA((4,))],
        compiler_params=_cparams(("arbitrary", "arbitrary"), est),
    )(xnt, dpb)


def _dx_rs_call(dpa, wta, dpb, wtb, sb_in, sb_out, chip_v, *, tm):
    s = dpa.shape[0]
    tka, tkb = 1024, BW_B
    nka, nkb = dpa.shape[1] // tka, dpb.shape[1] // tkb
    ni, nk = s // tm, nka + nkb

    def body(a_ref, wa_ref, b_ref, wb_ref, sbin, sbout, cv, o_ref, rc_in, rc_out, abs_v, send, recv):
        i, kk = pl.program_id(0), pl.program_id(1)

        def copies():
            x, y, c, me, others = _place()
            sends, recvs = [], []
            for j, chip in enumerate(others):
                kj = 2 * chip[0] + chip[1]
                to = (*chip, c)
                sends += [_remote(sbin.at[kj], rc_in.at[j], send, recv, j, to),
                          _remote(sbout.at[kj], rc_out.at[j], send, recv, 3 + j, to),
                          _remote(cv, abs_v.at[me], send, recv, 6 + j, to)]
                recvs += [sends[-3], sends[-2], _remote(cv, abs_v.at[kj], send, recv, 6 + j, to)]
            return sends, recvs

        @pl.when((i == 0) & (kk == 0))
        def _():
            for cp in copies()[0]:
                cp.start()

        @pl.when(kk == 0)
        def _():
            o_ref[...] = jnp.zeros_like(o_ref)

        @pl.when(kk < nka)
        def _():
            o_ref[...] += _dot(a_ref[...], wa_ref[...])

        @pl.when(kk >= nka)
        def _():
            o_ref[...] += _dot(b_ref[...], wb_ref[...])

        @pl.when((i == ni - 1) & (kk == nk - 1))
        def _():
            sends, recvs = copies()
            for cp in recvs:
                cp.wait_recv()
            for cp in sends:
                cp.wait_send()

    isz = jnp.dtype(dpa.dtype).itemsize
    est = 2 * isz * (tm * tka + tka * D_MODEL + tm * tkb + tkb * D_MODEL) + 2 * tm * D_MODEL * 4
    outs = [jax.ShapeDtypeStruct((s, D_MODEL), F32),
            jax.ShapeDtypeStruct((3,) + sb_in.shape[1:], sb_in.dtype), jax.ShapeDtypeStruct((3,) + sb_out.shape[1:], sb_out.dtype),
            jax.ShapeDtypeStruct((4,) + chip_v.shape, F32)]
    return pl.pallas_call(
        body, name="dx_matmul_rs_chips",
        grid=(ni, nk),
        in_specs=[
            pl.BlockSpec((tm, tka), lambda i, kk: (i, jnp.minimum(kk, nka - 1))),
            pl.BlockSpec((tka, D_MODEL), lambda i, kk: (jnp.minimum(kk, nka - 1), 0)),
            pl.BlockSpec((tm, tkb), lambda i, kk: (i, jnp.maximum(kk - nka, 0))),
            pl.BlockSpec((tkb, D_MODEL), lambda i, kk: (jnp.maximum(kk - nka, 0), 0)),
            ANY, ANY, ANY,
        ],
        out_specs=[pl.BlockSpec((tm, D_MODEL), lambda i, kk: (i, 0)), ANY, ANY, ANY],
        out_shape=outs,
        scratch_shapes=[pltpu.SemaphoreType.DMA((9,)), pltpu.SemaphoreType.DMA((9,))],
        compiler_params=_cparams(("arbitrary", "arbitrary"), est),
    )(dpa, wta, dpb, wtb, sb_in, sb_out, chip_v)


def _gradx_call(x, dxn, dh, norm_w, tm):
    s = x.shape[0]

    def body(x_ref, g_ref, dh_ref, w_ref, gx_ref, dw_ref):
        @pl.when(pl.program_id(0) == 0)
        def _():
            dw_ref[...] = jnp.zeros_like(dw_ref)

        xv, gv = x_ref[...], g_ref[...]
        r = lax.rsqrt(jnp.mean(xv * xv, axis=-1, keepdims=True) + EPS)
        gw = gv * w_ref[...]
        gx_ref[...] = r * gw - xv * (r * r * r) * jnp.mean(xv * gw, axis=-1, keepdims=True) + dh_ref[...].astype(F32)
        dw_ref[...] += _fold8(gv * (xv * r))

    row = pl.BlockSpec((tm, D_MODEL), lambda i: (i, 0))
    return pl.pallas_call(
        body, name="grad_x",
        grid=(s // tm,),
        in_specs=[row, row, row, pl.BlockSpec((1, D_MODEL), lambda i: (0, 0))],
        out_specs=[row, pl.BlockSpec((8, D_MODEL), lambda i: (0, 0))],
        out_shape=[jax.ShapeDtypeStruct((s, D_MODEL), F32), jax.ShapeDtypeStruct((8, D_MODEL), F32)],
        compiler_params=_cparams(("arbitrary",), 2 * tm * D_MODEL * 16),
    )(x, dxn, dh, norm_w)


def _layernorm_stats(v):
    mu = jnp.mean(v, axis=-1, keepdims=True)
    vc = v - mu
    var = jnp.mean(vc * vc, axis=-1, keepdims=True)
    return vc * lax.rsqrt(var + EPS), lax.rsqrt(var + EPS)


def _tok_fwd_call(proj_a, y_b, gate_b, sgu_g, sgu_beta, wm, bias_full, t):
    s = proj_a.shape[0]

    def body(pa_ref, yb_ref, gb_ref, g_ref, be_ref, wm_ref, bf_ref, ya_ref, mg_ref, mgt_ref, mix_ref):
        u = pa_ref[:, 0:2048].astype(F32)
        v = pa_ref[:, 2048:4096].astype(F32)
        za = pa_ref[:, 4096:6144].astype(F32)
        xhat, _ = _layernorm_stats(v)
        vnb = _c(xhat * g_ref[...] + be_ref[...])
        for gi in range(SGU_GROUPS):
            sl = slice(128 * gi, 128 * gi + 128)
            mix_ref[:, sl] = _dot(wm_ref[gi], vnb[:, sl])
        mixed = mix_ref[...] + bf_ref[...]
        y_a = u * mixed * (za * _sigmoid(za))
        g0 = _sigmoid(pa_ref[:, 6144:8192].astype(F32) + gb_ref[:, 0:2048])
        g1 = _sigmoid(pa_ref[:, 8192:10240].astype(F32) + gb_ref[:, 2048:4096])
        merged = g0 * y_a + g1 * yb_ref[...].astype(F32)
        ya_ref[...] = _c(y_a)
        mg_ref[...] = _c(merged)
        mgt_ref[...] = _c(merged.T)

    row = pl.BlockSpec((t, D_MODEL), lambda i: (i, 0))
    vec = lambda w: pl.BlockSpec((1, w), lambda i: (0, 0))
    return pl.pallas_call(
        body, name="tok_fwd",
        grid=(s // t,),
        in_specs=[pl.BlockSpec((t, NA), lambda i: (i, 0)), row, vec(4096), vec(2048), vec(2048),
                  pl.BlockSpec((SGU_GROUPS, 128, 128), lambda i: (0, 0, 0)), pl.BlockSpec((128, D_MODEL), lambda i: (0, 0))],
        out_specs=[row, row, pl.BlockSpec((D_MODEL, t), lambda i: (0, i))],
        out_shape=[jax.ShapeDtypeStruct((s, D_MODEL), MXU_DTYPE), jax.ShapeDtypeStruct((s, D_MODEL), MXU_DTYPE),
                   jax.ShapeDtypeStruct((D_MODEL, s), MXU_DTYPE)],
        scratch_shapes=[pltpu.VMEM((t, D_MODEL), F32)],
        compiler_params=_cparams(("parallel",), 2 * t * NA * 4 + 12 * t * D_MODEL * 4),
    )(proj_a, y_b, gate_b, sgu_g, sgu_beta, wm, bias_full)


def _tok_bwd_call(proj_a, dmerged, y_a, y_b, gate_b, sgu_g, sgu_beta, wm, wmt, bias_full, t):
    s = proj_a.shape[0]

    def body(pa_ref, dm_ref, ya_ref, yb_ref, gb_ref, g_ref, be_ref, wm_ref, wmt_ref, bf_ref,
             dpa_ref, dyb_ref, dgb_ref, dgam_ref, dbeta_ref, dbf_ref, dws_ref, mix_ref, dvn_ref):
        @pl.when(pl.program_id(0) == 0)
        def _():
            dgb_ref[...] = jnp.zeros_like(dgb_ref)
            dgam_ref[...] = jnp.zeros_like(dgam_ref)
            dbeta_ref[...] = jnp.zeros_like(dbeta_ref)
            dbf_ref[...] = jnp.zeros_like(dbf_ref)
            dws_ref[...] = jnp.zeros_like(dws_ref)

        u = pa_ref[:, 0:2048].astype(F32)
        v = pa_ref[:, 2048:4096].astype(F32)
        za = pa_ref[:, 4096:6144].astype(F32)
        xhat, rstd = _layernorm_stats(v)
        vnb = _c(xhat * g_ref[...] + be_ref[...])
        for gi in range(SGU_GROUPS):
            sl = slice(128 * gi, 128 * gi + 128)
            mix_ref[:, sl] = _dot(wm_ref[gi], vnb[:, sl])
        mixed = mix_ref[...] + bf_ref[...]
        sig = _sigmoid(za)
        sz = za * sig
        dm = dm_ref[...].astype(F32)
        y_a = ya_ref[...].astype(F32)
        g0 = _sigmoid(pa_ref[:, 6144:8192].astype(F32) + gb_ref[:, 0:2048])
        g1 = _sigmoid(pa_ref[:, 8192:10240].astype(F32) + gb_ref[:, 2048:4096])
        dgl0 = dm * y_a * g0 * (1.0 - g0)
        dgl1 = dm * yb_ref[...].astype(F32) * g1 * (1.0 - g1)
        dyb_ref[...] = _c(dm * g1)
        dya = dm * g0
        dpa_ref[:, 6144:8192] = _c(dgl0)
        dpa_ref[:, 8192:10240] = _c(dgl1)
        dgb_ref[:, 0:2048] += _fold8(dgl0)
        dgb_ref[:, 2048:4096] += _fold8(dgl1)
        dpa_ref[:, 0:2048] = _c(dya * mixed * sz)
        dpa_ref[:, 4096:6144] = _c(dya * (u * mixed) * (sig * (1.0 + za * (1.0 - sig))))
        dmixed = dya * u * sz
        dbf_ref[...] += dmixed
        dmb = _c(dmixed)
        for gi in range(SGU_GROUPS):
            sl = slice(128 * gi, 128 * gi + 128)
            dvn_ref[:, sl] = _dot(wmt_ref[gi], dmb[:, sl])
            dws_ref[gi] += _dot_nt(dmb[:, sl], vnb[:, sl])
        dvn = dvn_ref[...]
        dgam_ref[...] += _fold8(dvn * xhat)
        dbeta_ref[...] += _fold8(dvn)
        dxh = dvn * g_ref[...]
        dv = rstd * (dxh - jnp.mean(dxh, axis=-1, keepdims=True) - xhat * jnp.mean(dxh * xhat, axis=-1, keepdims=True))
        dpa_ref[:, 2048:4096] = _c(dv)

    row = pl.BlockSpec((t, D_MODEL), lambda i: (i, 0))
    vec = lambda w: pl.BlockSpec((1, w), lambda i: (0, 0))
    acc = lambda w: pl.BlockSpec((8, w), lambda i: (0, 0))
    wspec = pl.BlockSpec((SGU_GROUPS, 128, 128), lambda i: (0, 0, 0))
    return pl.pallas_call(
        body, name="tok_bwd",
        grid=(s // t,),
        in_specs=[pl.BlockSpec((t, NA), lambda i: (i, 0)), row, row, row, vec(4096), vec(2048), vec(2048),
                  wspec, wspec, pl.BlockSpec((128, D_MODEL), lambda i: (0, 0))],
        out_specs=[pl.BlockSpec((t, NA), lambda i: (i, 0)), row, acc(4096), acc(2048), acc(2048),
                   pl.BlockSpec((128, D_MODEL), lambda i: (0, 0)), wspec],
        out_shape=[jax.ShapeDtypeStruct((s, NA), MXU_DTYPE), jax.ShapeDtypeStruct((s, D_MODEL), MXU_DTYPE),
                   jax.ShapeDtypeStruct((8, 4096), F32), jax.ShapeDtypeStruct((8, 2048), F32),
                   jax.ShapeDtypeStruct((8, 2048), F32), jax.ShapeDtypeStruct((128, D_MODEL), F32),
                   jax.ShapeDtypeStruct((SGU_GROUPS, 128, 128), F32)],
        scratch_shapes=[pltpu.VMEM((t, D_MODEL), F32), pltpu.VMEM((t, D_MODEL), F32)],
        compiler_params=_cparams(("arbitrary",), 2 * t * NA * 6 + 16 * t * D_MODEL * 4),
    )(proj_a, dmerged, y_a, y_b, gate_b, sgu_g, sgu_beta, wm, wmt, bias_full)


def _out_call(merged, x, target, w_out, fnw, t):
    s = x.shape[0]
    nt = s // t

    def body(mg_ref, x_ref, t_ref, w_ref, fw_ref, dhb_ref, dmg_ref, loss_ref, dfw_ref):
        @pl.when(pl.program_id(0) == 0)
        def _():
            dfw_ref[...] = jnp.zeros_like(dfw_ref)

        h = x_ref[...] + _dot(mg_ref[...], w_ref[...])
        r = lax.rsqrt(jnp.mean(h * h, axis=-1, keepdims=True) + EPS)
        hn = h * r
        err = hn * fw_ref[...] - t_ref[...]
        loss_ref[...] = jnp.full(loss_ref.shape, 0.5 * jnp.sum(jnp.mean(err * err, axis=-1, keepdims=True)), F32)
        dy = err * (1.0 / D_MODEL)
        dfw_ref[...] += _fold8(dy * hn)
        gw = dy * fw_ref[...]
        dh = r * gw - h * (r * r * r) * jnp.mean(h * gw, axis=-1, keepdims=True)
        dhb = _c(dh)
        dhb_ref[...] = dhb
        dmg_ref[...] = _c(_dot_nt(dhb, w_ref[...]))

    row = pl.BlockSpec((t, D_MODEL), lambda i: (i, 0))
    return pl.pallas_call(
        body, name="out_proj_loss",
        grid=(nt,),
        in_specs=[row, row, row, pl.BlockSpec((D_MODEL, D_MODEL), lambda i: (0, 0), pipeline_mode=pl.Buffered(1)),
                  pl.BlockSpec((1, D_MODEL), lambda i: (0, 0))],
        out_specs=[row, row, pl.BlockSpec((1, 8, 128), lambda i: (i, 0, 0)), pl.BlockSpec((8, D_MODEL), lambda i: (0, 0))],
        out_shape=[jax.ShapeDtypeStruct((s, D_MODEL), MXU_DTYPE),
                   jax.ShapeDtypeStruct((s, D_MODEL), MXU_DTYPE), jax.ShapeDtypeStruct((nt, 8, 128), F32),
                   jax.ShapeDtypeStruct((8, D_MODEL), F32)],
        compiler_params=_cparams(("arbitrary",), D_MODEL * D_MODEL * 2 + 2 * t * D_MODEL * 16 + 5 * t * D_MODEL * 4),
    )(merged, x, target, w_out, fnw)


def _ssd_fwd_call(proj_b, dtb, alog, dsk, cw, cb, nw, t, ng):
    s = proj_b.shape[0]
    nt, nch = s // t, t // CHUNK

    def body(pb_ref, halo_ref, dtb_ref, al_ref, ds_ref, cw_ref, cb_ref, nw_ref, y_ref, yb_ref, hp_ref, pre_ref,
             dt_ref, acs_ref, ht_ref, prev_ref):
        i = pl.program_id(1)

        @pl.when(i == 0)
        def _():
            ht_ref[...] = jnp.zeros_like(ht_ref)

        for gi in range(ng):
            prev_ref[:, 768 * gi:768 * gi + 768] = jnp.where(i == 0, 0.0, halo_ref[:, FW_B * gi + XBC_O:FW_B * gi + DT_O])
        masks = _ssd_masks()
        tri, _ = _cumsum_mats(t)
        a_neg = -jnp.exp(al_ref[...])
        expand = _head_expand_mat()
        for gi in range(ng):
            fo, go, no = FW_B * gi, SSD_GW * gi, 128 * gi
            dt_n = _softplus(pb_ref[:, fo + DT_O:fo + DT_O + 128] + dtb_ref[:, no:no + 128])
            acs_n = _dot01_l(tri, dt_n * a_neg[:, no:no + 128], 3)
            dt_ref[:, go:go + 512] = _dot01_r(dt_n, expand, 2)
            acs_ref[:, go:go + 512] = _dot01_r(acs_n, expand, 2)

        def chunk(c, carry):
            rows = pl.ds(pl.multiple_of(c * CHUNK, CHUNK), CHUNK)
            for gi in range(ng):
                fo, co, go, no = FW_B * gi, 768 * gi, SSD_GW * gi, 128 * gi
                xbc = pb_ref[rows, fo + XBC_O:fo + DT_O]
                taps = _conv_taps(jnp.concatenate([prev_ref[:, co:co + 768], xbc], axis=0), CHUNK)
                prev_ref[:, co:co + 768] = xbc[CHUNK - 8:CHUNK]
                pre = cb_ref[:, co:co + 768]
                for k in range(4):
                    pre = pre + taps[k] * cw_ref[k:k + 1, co:co + 768]
                pre_ref[rows, co:co + 768] = pre
                act = pre * _sigmoid(pre)
                dt = dt_ref[rows, go:go + 512]
                acs = acs_ref[rows, go:go + 512]
                ht = ht_ref[gi]
                hp_ref[c, :, go:go + 512] = ht
                y, ht_new = _ssd_chunk_fwd(act[:, 0:512], act[:, 512:640], act[:, 640:768], dt, acs,
                                           ds_ref[:, go:go + 512], ht, masks)
                y_ref[rows, go:go + 512] = y
                ht_ref[gi] = ht_new
                zb = pb_ref[rows, fo:fo + 512]
                hh = y * (zb * _sigmoid(zb))
                rr = lax.rsqrt(jnp.mean(hh * hh, axis=-1, keepdims=True) + EPS)
                yb_ref[rows, go:go + 512] = _c(hh * rr * nw_ref[:, go:go + 512])
            return carry

        lax.fori_loop(0, nch, chunk, 0)

    gvec = lambda w: pl.BlockSpec((1, ng * w), lambda g, i: (0, g))
    return pl.pallas_call(
        body, name="ssd_fwd",
        grid=(SSD_GROUPS // ng, nt),
        in_specs=[pl.BlockSpec((t, ng * FW_B), lambda g, i: (i, g)),
                  pl.BlockSpec((8, ng * FW_B), lambda g, i: (jnp.maximum(i * (t // 8) - 1, 0), g)),
                  gvec(128), gvec(128), gvec(512),
                  pl.BlockSpec((4, ng * 768), lambda g, i: (0, g)), gvec(768), gvec(512)],
        out_specs=[pl.BlockSpec((t, ng * SSD_GW), lambda g, i: (i, g)), pl.BlockSpec((t, ng * SSD_GW), lambda g, i: (i, g)),
                   pl.BlockSpec((nch, SSD_STATE, ng * SSD_GW), lambda g, i: (i, 0, g)),
                   pl.BlockSpec((t, ng * 768), lambda g, i: (i, g)),
                   pl.BlockSpec((t, ng * SSD_GW), lambda g, i: (i, g)), pl.BlockSpec((t, ng * SSD_GW), lambda g, i: (i, g))],
        out_shape=[jax.ShapeDtypeStruct((s, D_MODEL), F32), jax.ShapeDtypeStruct((s, D_MODEL), MXU_DTYPE),
                   jax.ShapeDtypeStruct((s // CHUNK, SSD_STATE, D_MODEL), F32),
                   jax.ShapeDtypeStruct((s, SSD_GROUPS * 768), F32),
                   jax.ShapeDtypeStruct((s, D_MODEL), F32), jax.ShapeDtypeStruct((s, D_MODEL), F32)],
        scratch_shapes=[pltpu.VMEM((ng, SSD_STATE, SSD_GW), F32), pltpu.VMEM((8, ng * 768), F32)],
        compiler_params=_cparams(("parallel", "arbitrary"), ng * (2 * t * FW_B * 4 + 16 * t * SSD_GW * 4) + 16 * 1024 * 1024),
    )(proj_b, proj_b, dtb, alog, dsk, cw, cb, nw)


def _ssd_bwd_call(proj_b, pre_all, dt_all, acs_all, dyb, y, hprev, dtb, alog, dsk, cw, nw, t, ng):
    s = proj_b.shape[0]
    nt, nch = s // t, t // CHUNK

    def body(pb_ref, pre_ref, dt_ref, acs_ref, dyb_ref, y_ref, hp_ref, dtb_ref, al_ref, ds_ref, cw_ref, nw_ref,
             dpb_ref, a512_ref, a768_ref, dht_ref, nxt_ref, q_ref, p1_ref):
        i = pl.program_id(1)

        @pl.when(i == 0)
        def _():
            dht_ref[...] = jnp.zeros_like(dht_ref)
            nxt_ref[...] = jnp.zeros_like(nxt_ref)
            a512_ref[...] = jnp.zeros_like(a512_ref)
            a768_ref[...] = jnp.zeros_like(a768_ref)

        _, trit = _cumsum_mats(t)
        a_neg = -jnp.exp(al_ref[...])
        masks = _ssd_masks()

        def chunk(cc, carry):
            c = nch - 1 - cc
            rows = pl.ds(pl.multiple_of(c * CHUNK, CHUNK), CHUNK)
            for gi in range(ng):
                fo, co, go, bo = FW_B * gi, 768 * gi, SSD_GW * gi, BW_B * gi
                pre = pre_ref[rows, co:co + 768]
                sp = _sigmoid(pre)
                act = pre * sp
                zb = pb_ref[rows, fo:fo + 512]
                yv = y_ref[rows, go:go + 512]
                sgz = _sigmoid(zb)
                sz = zb * sgz
                hh = yv * sz
                rr = lax.rsqrt(jnp.mean(hh * hh, axis=-1, keepdims=True) + EPS)
                dyb = dyb_ref[rows, go:go + 512].astype(F32)
                a512_ref[gi, 0] += _fold8(dyb * (hh * rr))
                tt = dyb * nw_ref[:, go:go + 512]
                dhh = rr * tt - hh * (rr * rr * rr) * jnp.mean(hh * tt, axis=-1, keepdims=True)
                dpb_ref[rows, bo:bo + 512] = _c(dhh * yv * (sgz * (1.0 + zb * (1.0 - sgz))))
                dxs, d_b, d_c, dht_prev, dyxs, qq, p1 = _ssd_chunk_bwd(
                    act[:, 0:512], act[:, 512:640], act[:, 640:768], dt_ref[rows, go:go + 512], acs_ref[rows, go:go + 512],
                    ds_ref[:, go:go + 512], hp_ref[c, :, go:go + 512], dht_ref[gi], dhh * sz, masks)
                dht_ref[gi] = dht_prev
                q_ref[rows, go:go + 512] = qq
                p1_ref[rows, go:go + 512] = p1
                a512_ref[gi, 1] += _fold8(dyxs)
                dpre = jnp.concatenate([dxs, d_b, d_c], axis=1) * (sp * (1.0 + pre * (1.0 - sp)))
                xbc = pb_ref[rows, fo + XBC_O:fo + DT_O]
                a768_ref[gi, 4] += _fold8(dpre)
                a768_ref[gi, 3] += _fold8(dpre * xbc)
                dpad = jnp.concatenate([dpre, nxt_ref[:, co:co + 768]], axis=0)
                dx = dpre * cw_ref[3:4, co:co + 768]
                for k in range(3):
                    d_k = pltpu.roll(dpad, CHUNK + 8 - (3 - k), 0)[0:CHUNK]
                    dx = dx + d_k * cw_ref[k:k + 1, co:co + 768]
                    a768_ref[gi, k] += _fold8(d_k * xbc)
                nxt_ref[:, co:co + 768] = dpre[0:8]
                dpb_ref[rows, bo + 512:bo + 1280] = _c(dx)
            return carry

        lax.fori_loop(0, nch, chunk, 0)

        rsel = _c(jnp.where(lax.shift_right_logical(_iota((SSD_GW, 128), 0), 6) == _iota((SSD_GW, 128), 1), 1.0, 0.0))
        for gi in range(ng):
            fo, go, bo, no = FW_B * gi, SSD_GW * gi, BW_B * gi, 128 * gi
            ddt, dadt = _ssd_finish_dt(q_ref[:, go:go + 512], p1_ref[:, go:go + 512], dt_ref[:, go:go + 512],
                                       a_neg[:, go:go + 512], trit, masks[3])
            sig_n = _sigmoid(pb_ref[:, fo + DT_O:fo + DT_O + 128] + dtb_ref[:, no:no + 128])
            ddtr_n = _dot01_r(ddt, rsel, 2) * sig_n
            dpb_ref[:, bo + DT_O:bo + DT_O + 128] = _c(ddtr_n)
            a512_ref[gi, 2] += _fold8(dadt)
            a512_ref[gi, 3, :, 0:128] += _fold8(ddtr_n)

    gvec = lambda w: pl.BlockSpec((1, ng * w), lambda g, i: (0, g))
    rev = lambda w: pl.BlockSpec((t, ng * w), lambda g, i: (nt - 1 - i, g))
    return pl.pallas_call(
        body, name="ssd_bwd",
        grid=(SSD_GROUPS // ng, nt),
        in_specs=[rev(FW_B), rev(768), rev(SSD_GW), rev(SSD_GW), rev(SSD_GW), rev(SSD_GW),
                  pl.BlockSpec((nch, SSD_STATE, ng * SSD_GW), lambda g, i: (nt - 1 - i, 0, g)),
                  gvec(128), gvec(512), gvec(512),
                  pl.BlockSpec((4, ng * 768), lambda g, i: (0, g)), gvec(512)],
        out_specs=[rev(BW_B),
                   pl.BlockSpec((ng, 4, 8, 512), lambda g, i: (g, 0, 0, 0)),
                   pl.BlockSpec((ng, 5, 8, 768), lambda g, i: (g, 0, 0, 0))],
        out_shape=[jax.ShapeDtypeStruct((s, SSD_GROUPS * BW_B), MXU_DTYPE),
                   jax.ShapeDtypeStruct((SSD_GROUPS, 4, 8, 512), F32),
                   jax.ShapeDtypeStruct((SSD_GROUPS, 5, 8, 768), F32)],
        scratch_shapes=[pltpu.VMEM((ng, SSD_STATE, SSD_GW), F32), pltpu.VMEM((8, ng * 768), F32),
                        pltpu.VMEM((t, ng * SSD_GW), F32), pltpu.VMEM((t, ng * SSD_GW), F32)],
        compiler_params=_cparams(("parallel", "arbitrary"), ng * (2 * t * FW_B * 4 + 18 * t * SSD_GW * 4) + 16 * 1024 * 1024),
    )(proj_b, pre_all, dt_all, acs_all, dyb, y, hprev, dtb, alog, dsk, cw, nw)


def _rows_call(body, ins, outs, tr, name):
    r = ins[0].shape[0]
    spec = lambda a: pl.BlockSpec((tr, a.shape[1]), lambda i: (i, 0))
    est = 2 * tr * sum(a.shape[1] * jnp.dtype(a.dtype).itemsize for a in list(ins) + list(outs))
    return pl.pallas_call(
        body, name=name, grid=(r // tr,),
        in_specs=[spec(a) for a in ins], out_specs=[spec(o) for o in outs], out_shape=list(outs),
        compiler_params=_cparams(("parallel",), est),
    )(*ins)


def _add_pair(a, b, tr, name):
    def body(a_ref, b_ref, o_ref):
        o_ref[...] = a_ref[...] + b_ref[...]

    return _rows_call(body, [a, b], [jax.ShapeDtypeStruct(a.shape, F32)], tr, name)[0]


def _rs_add(p, sib, place, tr, name):
    _, r, c = p.shape
    half = r // 2
    nb = half // tr

    def body(pl_ref, p_ref, s_ref, b_ref, own_ref):
        v = p_ref[0] + s_ref[0]
        b_ref[0] = v.astype(jnp.bfloat16)

        @pl.when(pl.program_id(1) == pl_ref[0])
        def _():
            own_ref[...] = v

    return pl.pallas_call(
        body, name=name,
        grid_spec=pltpu.PrefetchScalarGridSpec(
            num_scalar_prefetch=1, grid=(nb, 4),
            in_specs=[pl.BlockSpec((1, tr, c), lambda i, k, pr: (k, pr[1] * nb + i, 0)),
                      pl.BlockSpec((1, tr, c), lambda i, k, pr: (k, i, 0))],
            out_specs=[pl.BlockSpec((1, tr, c), lambda i, k, pr: (k, i, 0)),
                       pl.BlockSpec((tr, c), lambda i, k, pr: (i, 0))]),
        out_shape=[jax.ShapeDtypeStruct((4, half, c), jnp.bfloat16), jax.ShapeDtypeStruct((half, c), F32)],
        compiler_params=_cparams(("parallel", "arbitrary"), 2 * tr * c * 14),
    )(place, p, sib)


WIN_STEP = 3840
WIN_W = 3968


def _rs_add_windows(dw, sib, place, tr, name):
    r = dw.shape[0]
    half = r // 2
    nb = half // tr
    tail = WIN_W - WIN_STEP

    def body(pl_ref, pm_ref, pt_ref, s_ref, b_ref, own_ref):
        vm = pm_ref[...].astype(F32) + s_ref[0, :, 0:WIN_STEP].astype(F32)
        vt = pt_ref[...].astype(F32) + s_ref[0, :, WIN_STEP:WIN_W].astype(F32)
        b_ref[0, :, 0:WIN_STEP] = vm.astype(jnp.bfloat16)
        b_ref[0, :, WIN_STEP:WIN_W] = vt.astype(jnp.bfloat16)

        @pl.when(pl.program_id(1) == pl_ref[0])
        def _():
            own_ref[:, 0:WIN_STEP] = vm
            own_ref[:, WIN_STEP:WIN_W] = vt

    return pl.pallas_call(
        body, name=name,
        grid_spec=pltpu.PrefetchScalarGridSpec(
            num_scalar_prefetch=1, grid=(nb, 4),
            in_specs=[pl.BlockSpec((tr, WIN_STEP), lambda i, k, pr: (pr[1] * nb + i, k)),
                      pl.BlockSpec((tr, tail), lambda i, k, pr: (pr[1] * nb + i, (WIN_STEP // tail) * (k + 1))),
                      pl.BlockSpec((1, tr, WIN_W), lambda i, k, pr: (k, i, 0))],
            out_specs=[pl.BlockSpec((1, tr, WIN_W), lambda i, k, pr: (k, i, 0)),
                       pl.BlockSpec((tr, WIN_W), lambda i, k, pr: (i, 0))]),
        out_shape=[jax.ShapeDtypeStruct((4, half, WIN_W), jnp.bfloat16), jax.ShapeDtypeStruct((half, WIN_W), F32)],
        compiler_params=_cparams(("parallel", "arbitrary"), 2 * tr * WIN_W * 14),
    )(place, dw, dw, sib)


def _sum_own_recv(own, recv, tr, name):
    r, c = own.shape

    def body(o_ref, r_ref, out_ref):
        v = o_ref[...]
        for j in range(3):
            v = v + r_ref[j].astype(F32)
        out_ref[...] = v

    return pl.pallas_call(
        body, name=name, grid=(r // tr,),
        in_specs=[pl.BlockSpec((tr, c), lambda i: (i, 0)), pl.BlockSpec((3, tr, c), lambda i: (0, i, 0))],
        out_specs=pl.BlockSpec((tr, c), lambda i: (i, 0)),
        out_shape=jax.ShapeDtypeStruct((r, c), F32),
        compiler_params=_cparams(("parallel",), 2 * tr * c * 14),
    )(own, recv)


def _sum_slots(stack, name):
    n, r, w = stack.shape

    def body(a_ref, out_ref):
        v = a_ref[0]
        for k in range(1, n):
            v = v + a_ref[k]
        out_ref[...] = v

    return pl.pallas_call(
        body, name=name, grid=(1,),
        in_specs=[pl.BlockSpec((n, r, w), lambda i: (0, 0, 0))],
        out_specs=pl.BlockSpec((r, w), lambda i: (0, 0)),
        out_shape=jax.ShapeDtypeStruct((r, w), F32),
        compiler_params=_cparams(("arbitrary",), 2 * (n + 1) * r * w * 4),
    )(stack)


def _adamw(w, g, m, v, tr, name):
    def body(w_ref, g_ref, m_ref, v_ref, d_ref, nm_ref, nv_ref):
        d_ref[...], nm_ref[...], nv_ref[...] = _adam_math(w_ref[...], g_ref[...], m_ref[...], v_ref[...])

    o = jax.ShapeDtypeStruct(w.shape, F32)
    return _rows_call(body, [w, g, m, v], [o, o, o], tr, name)


def _adam_math(w, g, m, v):
    nm = ADAM_B1 * m + (1.0 - ADAM_B1) * g
    nv = ADAM_B2 * v + (1.0 - ADAM_B2) * (g * g)
    m_hat = nm / (1.0 - ADAM_B1 ** ADAM_STEP)
    v_hat = nv / (1.0 - ADAM_B2 ** ADAM_STEP)
    return -ADAM_LR * (m_hat / (jnp.sqrt(v_hat) + ADAM_EPS) + ADAM_WD * w), nm, nv


def _adamw_halves(w, g_own, g_sib, m, v, place, tr, name):
    r, c = w.shape

    def body(pl_ref, w_ref, go_ref, gs_ref, m_ref, v_ref, g_ref, d_ref, nm_ref, nv_ref):
        first = pl_ref[1] == 0
        own, sib = go_ref[...], gs_ref[...]
        g = jnp.concatenate([jnp.where(first, own, sib), jnp.where(first, sib, own)], axis=1)
        g_ref[...] = g
        d_ref[...], nm_ref[...], nv_ref[...] = _adam_math(w_ref[...], g, m_ref[...], v_ref[...])

    full = pl.BlockSpec((tr, c), lambda i, pr: (i, 0))
    half = pl.BlockSpec((tr, c // 2), lambda i, pr: (i, 0))
    o = jax.ShapeDtypeStruct((r, c), F32)
    return pl.pallas_call(
        body, name=name,
        grid_spec=pltpu.PrefetchScalarGridSpec(num_scalar_prefetch=1, grid=(r // tr,),
                                               in_specs=[full, half, half, full, full], out_specs=[full] * 4),
        out_shape=[o] * 4,
        compiler_params=_cparams(("parallel",), 2 * tr * c * 4 * 8),
    )(place, w, g_own, g_sib, m, v)


ANY = pl.BlockSpec(memory_space=pl.ANY)


def _place():
    x, y, c = lax.axis_index("x"), lax.axis_index("y"), lax.axis_index("c")
    others = [(1 - x, y), (x, 1 - y), (1 - x, 1 - y)]
    return x, y, c, 2 * x + y, others


def _remote(src, dst, send, recv, k, to):
    return pltpu.make_async_remote_copy(src_ref=src, dst_ref=dst, send_sem=send.at[k], recv_sem=recv.at[k],
                                        device_id=to, device_id_type=MESH)


def _norm_gather_call(x, norm_w, win_b, wout_b, cw8, tm):
    s = x.shape[0]
    ni = s // tm
    h_in, h_out = win_b.shape[0] // 2, wout_b.shape[0] // 2
    q_in = h_in // 2

    def body(x_ref, w_ref, win, wout, cw, xn_ref, xnt_ref, g_in, g_out, g_cw, send, recv):
        i = pl.program_id(0)

        def direct():
            xx, yy, c, me, others = _place()
            mi, mo = pl.ds(c * h_in, h_in), pl.ds(c * h_out, h_out)
            cps = [_remote(win.at[mi], g_in.at[me, mi], send, recv, j, (*others[j], c)) for j in range(2)]
            cps += [_remote(wout.at[mo], g_out.at[me, mo], send, recv, 7 + j, (*chip, c)) for j, chip in enumerate(others)]
            cps += [_remote(cw, g_cw.at[me], send, recv, 13 + j, (*chip, c)) for j, chip in enumerate(others)]
            return cps

        @pl.when(i == 0)
        def _():
            for cp in direct():
                cp.start()

        xv = x_ref[...]
        r = lax.rsqrt(jnp.mean(xv * xv, axis=-1, keepdims=True) + EPS)
        xn = xv * r * w_ref[...]
        xn_ref[...] = _c(xn)
        xnt_ref[...] = _c(xn.T)

        @pl.when(i == ni - 1)
        def _():
            xx, yy, c, me, others = _place()
            sib = (xx, yy, 1 - c)
            kx, ky, kd = (2 * chip[0] + chip[1] for chip in others)
            mi, ti = pl.ds(c * h_in, h_in), pl.ds((1 - c) * h_in, h_in)
            quarter = [pl.ds(c * h_in, q_in), pl.ds(c * h_in + q_in, q_in)]
            started = []

            def go(cp):
                cp.start()
                started.append(cp)

            _remote(g_in.at[kx, mi], g_in.at[kx, mi], send, recv, 0, (*others[0], c)).wait_recv()
            go(_remote(g_in.at[kx, quarter[0]], g_in.at[kx, quarter[0]], send, recv, 2, (*others[1], c)))
            go(_remote(g_in.at[kx, mi], g_in.at[kx, mi], send, recv, 4, sib))
            _remote(g_in.at[ky, mi], g_in.at[ky, mi], send, recv, 1, (*others[1], c)).wait_recv()
            go(_remote(g_in.at[ky, quarter[1]], g_in.at[ky, quarter[1]], send, recv, 3, (*others[0], c)))
            go(_remote(g_in.at[ky, mi], g_in.at[ky, mi], send, recv, 5, sib))
            mo, to = pl.ds(c * h_out, h_out), pl.ds((1 - c) * h_out, h_out)
            for j, chip in enumerate(others):
                kj = 2 * chip[0] + chip[1]
                _remote(g_out.at[kj, mo], g_out.at[kj, mo], send, recv, 7 + j, (*chip, c)).wait_recv()
                go(_remote(g_out.at[kj, mo], g_out.at[kj, mo], send, recv, 10 + j, sib))
            _remote(g_in.at[kd, quarter[0]], g_in.at[kd, quarter[0]], send, recv, 2, (*others[1], c)).wait_recv()
            _remote(g_in.at[kd, quarter[1]], g_in.at[kd, quarter[1]], send, recv, 3, (*others[0], c)).wait_recv()
            go(_remote(g_in.at[kd, mi], g_in.at[kd, mi], send, recv, 6, sib))
            for k_src, sem in ((kx, 4), (ky, 5), (kd, 6)):
                _remote(g_in.at[k_src, ti], g_in.at[k_src, ti], send, recv, sem, sib).wait_recv()
            for j, chip in enumerate(others):
                kj = 2 * chip[0] + chip[1]
                _remote(g_out.at[kj, to], g_out.at[kj, to], send, recv, 10 + j, sib).wait_recv()
                _remote(cw, g_cw.at[kj], send, recv, 13 + j, (*chip, c)).wait_recv()
            for cp in direct() + started:
                cp.wait_send()

    outs = [jax.ShapeDtypeStruct((s, D_MODEL), MXU_DTYPE), jax.ShapeDtypeStruct((D_MODEL, s), MXU_DTYPE)]
    outs += [jax.ShapeDtypeStruct((4,) + a.shape, a.dtype) for a in (win_b, wout_b, cw8)]
    return pl.pallas_call(
        body, name="rmsnorm_gather_weights",
        grid=(ni,),
        in_specs=[pl.BlockSpec((tm, D_MODEL), lambda i: (i, 0)), pl.BlockSpec((1, D_MODEL), lambda i: (0, 0)), ANY, ANY, ANY],
        out_specs=[pl.BlockSpec((tm, D_MODEL), lambda i: (i, 0)), pl.BlockSpec((D_MODEL, tm), lambda i: (0, i)), ANY, ANY, ANY],
        out_shape=outs,
        scratch_shapes=[pltpu.SemaphoreType.DMA((16,)), pltpu.SemaphoreType.DMA((16,))],
        compiler_params=_cparams(("arbitrary",), 2 * tm * D_MODEL * 12),
    )(x, norm_w, win_b, wout_b, cw8)


def _dw_out_rs_call(a, b, dw, *, tn, tk):
    m, k = a.shape
    n = b.shape[1]
    nj, nk = n // tn, k // tk
    half = dw.shape[0] // 2

    def body(a_ref, b_ref, pin, o_ref, sib_in, send, recv):
        j, kk = pl.program_id(0), pl.program_id(1)

        def copies():
            x, y, c, me, others = _place()
            rows = pl.ds((1 - c) * half, half)
            return [_remote(pin.at[rows, pl.ds(WIN_STEP * w, WIN_W)], sib_in.at[w], send, recv, w, (x, y, 1 - c)) for w in range(4)]

        @pl.when((j == 0) & (kk == 0))
        def _():
            for cp in copies():
                cp.start()

        @pl.when(kk == 0)
        def _():
            o_ref[...] = jnp.zeros_like(o_ref)

        o_ref[...] += _dot(a_ref[...], b_ref[...])

        @pl.when((j == nj - 1) & (kk == nk - 1))
        def _():
            cps = copies()
            for cp in cps:
                cp.wait_recv()
            for cp in cps:
                cp.wait_send()

    isz = jnp.dtype(a.dtype).itemsize
    est = 2 * (m * tk + tk * tn) * isz + 2 * m * tn * 4
    outs = [jax.ShapeDtypeStruct((m, n), F32), jax.ShapeDtypeStruct((4, half, WIN_W), dw.dtype)]
    return pl.pallas_call(
        body, name="dw_out_rs_sibling",
        grid=(nj, nk),
        in_specs=[pl.BlockSpec((m, tk), lambda j, kk: (0, kk)), pl.BlockSpec((tk, tn), lambda j, kk: (kk, j)), ANY],
        out_specs=[pl.BlockSpec((m, tn), lambda j, kk: (0, j)), ANY],
        out_shape=outs,
        scratch_shapes=[pltpu.SemaphoreType.DMA((4,)), pltpu.SemaphoreType.DMA((4,))],
        compiler_params=_cparams(("arbitrary", "arbitrary"), est),
    )(a, b, dw)


def _rs_sibling_call(p_out, vsmall):
    def body(pout, vs, sib_out, sib_v, send, recv):
        x, y, c, me, others = _place()
        sib = (x, y, 1 - c)
        half = pout.shape[1] // 2
        cps = [_remote(pout.at[:, pl.ds((1 - c) * half, half)], sib_out, send, recv, 0, sib),
               _remote(vs, sib_v, send, recv, 1, sib)]
        for cp in cps:
            cp.start()
        for cp in cps:
            cp.wait_recv()
        for cp in cps:
            cp.wait_send()

    outs = [jax.ShapeDtypeStruct((4, p_out.shape[1] // 2, p_out.shape[2]), p_out.dtype),
            jax.ShapeDtypeStruct(vsmall.shape, vsmall.dtype)]
    return pl.pallas_call(
        body, name="rs_sibling",
        in_specs=[ANY] * 2, out_specs=[ANY] * 2, out_shape=outs,
        scratch_shapes=[pltpu.SemaphoreType.DMA((2,)), pltpu.SemaphoreType.DMA((2,))],
    )(p_out, vsmall)


def _rs_join_call(f_in, f_out, nw8):
    def body(fin, fout, nw, sib_in, full_out, all_nw, send, recv):
        x, y, c, me, others = _place()
        sib = (x, y, 1 - c)
        half = fout.shape[0]
        cps = [_remote(fin, sib_in, send, recv, 0, sib),
               _remote(fout, full_out.at[pl.ds(c * half, half)], send, recv, 1, sib)]
        mine = 4 * x + 2 * y + c
        peers = []
        for r in range(1, 8):
            px, py, pc = (1 - x if r & 4 else x), (1 - y if r & 2 else y), (1 - c if r & 1 else c)
            peers.append((r, (px, py, pc), 4 * px + 2 * py + pc))
            cps.append(_remote(nw, all_nw.at[mine], send, recv, 1 + r, (px, py, pc)))
        for cp in cps:
            cp.start()
        cps[0].wait_recv()
        _remote(fout, full_out.at[pl.ds((1 - c) * half, half)], send, recv, 1, sib).wait_recv()
        for r, peer, idx in peers:
            _remote(nw, all_nw.at[idx], send, recv, 1 + r, peer).wait_recv()
        for cp in cps:
            cp.wait_send()

    outs = [jax.ShapeDtypeStruct(f_in.shape, F32), jax.ShapeDtypeStruct((2 * f_out.shape[0], f_out.shape[1]), F32),
            jax.ShapeDtypeStruct((8,) + nw8.shape, F32)]
    return pl.pallas_call(
        body, name="rs_join",
        in_specs=[ANY] * 3, out_specs=[ANY] * 3, out_shape=outs,
        scratch_shapes=[pltpu.SemaphoreType.DMA((9,)), pltpu.SemaphoreType.DMA((9,))],
    )(f_in, f_out, nw8)


def _pack(arrs):
    parts = []
    for a in arrs:
        f = a.reshape(-1).astype(F32)
        pad = (-f.shape[0]) % 1024
        parts.append(jnp.pad(f, (0, pad)).reshape(-1, 128))
    return jnp.concatenate(parts, axis=0)


def _unpack(packed, shapes):
    out, row = [], 0
    for shp in shapes:
        n = 1
        for d in shp:
            n *= d
        rows = (n + 1023) // 1024 * 8
        out.append(packed[row:row + rows].reshape(-1)[:n].reshape(shp))
        row += rows
    return out


def _expand_heads(v32):
    return jnp.repeat(v32.reshape(32), HEADDIM).reshape(1, D_MODEL)


def kernel(x, norm_w, w_in, gate_b, sgu_norm_g, sgu_norm_b, sgu_w, sgu_b, conv_w, conv_b, dt_bias, A_log, D_skip, ssd_norm_w, w_out, final_norm_w, loss_target, m_norm_w, m_w_in, m_gate_b, m_sgu_norm_g, m_sgu_norm_b, m_sgu_w, m_sgu_b, m_conv_w, m_conv_b, m_dt_bias, m_A_log, m_D_skip, m_ssd_norm_w, m_w_out, m_final_norm_w, v_norm_w, v_w_in, v_gate_b, v_sgu_norm_g, v_sgu_norm_b, v_sgu_w, v_sgu_b, v_conv_w, v_conv_b, v_dt_bias, v_A_log, v_D_skip, v_ssd_norm_w, v_w_out, v_final_norm_w):
    s = x.shape[1]
    x2 = x.reshape(s, D_MODEL)
    tgt = loss_target.reshape(s, D_MODEL)
    t_ssd, t_tok, t_out, t_row = min(T_SSD, s), min(T_TOK, s), min(T_OUT, s), min(T_ROW, s)
    tm_mm, tk_dw = min(TM_MM, s), min(TK_DW, s)
    chip = 2 * lax.axis_index("x") + lax.axis_index("y")

    cw8 = jnp.pad(conv_w[0], ((0, 4), (0, 0)))
    win_b, wout_b = _c(w_in[0]), _c(w_out[0])
    xn, xnt, g_in, g_out, g_cw = _norm_gather_call(x2, norm_w, win_b, wout_b, cw8, t_row)
    g_in = lax.dynamic_update_index_in_dim(g_in, win_b, chip, 0)
    g_out = lax.dynamic_update_index_in_dim(g_out, wout_b, chip, 0)
    g_cw = lax.dynamic_update_index_in_dim(g_cw, cw8, chip, 0)
    wt = jnp.transpose(g_in, (0, 2, 1)).reshape(IN_W, D_MODEL)
    w_out_full = g_out.reshape(D_MODEL, D_MODEL)
    conv_w_full = jnp.transpose(g_cw[:, 0:4, :], (1, 0, 2)).reshape(4, 3072)

    wt_a = jnp.concatenate([wt[0:6144], wt[11296:15392]], axis=0)
    per_group = lambda lo, n: wt[lo:lo + SSD_GROUPS * n].reshape(SSD_GROUPS, n, D_MODEL)
    wt_b = jnp.concatenate([per_group(6144, 512), per_group(8192, 512), per_group(10240, 128), per_group(10752, 128),
                            jnp.pad(per_group(11264, 8), ((0, 0), (0, 120), (0, 0)))], axis=1)
    wt_b = wt_b.reshape(SSD_GROUPS * BW_B, D_MODEL)

    def group_cols(full_xs, full_bc):
        parts = []
        for g in range(SSD_GROUPS):
            parts += [full_xs[:, 512 * g:512 * g + 512], full_bc[:, 128 * g:128 * g + 128], full_bc[:, 512 + 128 * g:512 + 128 * g + 128]]
        return jnp.concatenate(parts, axis=1)

    cw_g = group_cols(conv_w_full[:, 0:2048], conv_w_full[:, 2048:3072])
    cb_g = group_cols(conv_b[:, 0:2048], conv_b[:, 2048:3072])
    alog_e, dsk_e = _expand_heads(A_log), _expand_heads(D_skip)
    narrow = lambda v32: jnp.pad(v32.reshape(SSD_GROUPS, 8), ((0, 0), (0, 120))).reshape(1, SSD_GROUPS * 128)
    dtb_n, alog_n = narrow(dt_bias), narrow(A_log)

    pos_chunk = jnp.arange(SGU_BLOCK) // CHUNK
    smask = pos_chunk[None, :] <= pos_chunk[:, None]
    wm_f = jnp.where(smask[None], sgu_w[0], 0.0)
    wm = _c(wm_f)
    wmt = _c(jnp.transpose(wm_f, (0, 2, 1)))
    bias_full = jnp.repeat(sgu_b[0].T, D_MODEL // SGU_GROUPS, axis=1)
    fnw = final_norm_w.reshape(1, D_MODEL)

    proj_a = _mm(xn, wt_a, tm=tm_mm, tn=2048, tk=D_MODEL, name="in_proj_a", out_dtype=MXU_DTYPE, b_is_t=True)
    proj_b = _mm(xn, wt_b, tm=tm_mm, tn=BW_B, tk=D_MODEL, name="in_proj_b", b_is_t=True)
    y_ssd, y_b, hprev, pre_all, dt_all, acs_all = _ssd_fwd_call(proj_b, dtb_n, alog_n, dsk_e, cw_g, cb_g, ssd_norm_w, t_ssd, NG_SSD)
    y_a, merged, merged_t = _tok_fwd_call(proj_a, y_b, gate_b, sgu_norm_g, sgu_norm_b, wm, bias_full, t_tok)
    dh_b, dmerged, loss_t, dfw8 = _out_call(merged, x2, tgt, w_out_full, fnw, t_out)

    dproj_a, dy_b, dgb8, dgam8, dbeta8, dbfull, dws = _tok_bwd_call(
        proj_a, dmerged, y_a, y_b, gate_b, sgu_norm_g, sgu_norm_b, wm, wmt, bias_full, t_tok)
    dproj_b, a512, a768 = _ssd_bwd_call(proj_b, pre_all, dt_all, acs_all, dy_b, y_ssd, hprev, dtb_n, alog_e, dsk_e, cw_g,
                                        ssd_norm_w, t_ssd, NG_SSD)
    dw_all, dw_dt = _dw_groups(xnt, dproj_b, tk=tk_dw, out_dtype=MXU_DTYPE)
    dw_all = _mm(xnt, dproj_a, tm=D_MODEL, tn=1024, tk=min(2 * tk_dw, s), name="dw_in_uvz", n=DW_ZB_O, out_dtype=MXU_DTYPE,
                 into=dw_all)
    dw_gate = _mm(xnt, dproj_a, tm=D_MODEL, tn=1024, tk=min(2 * tk_dw, s), name="dw_in_gate", col0=6, n=4096,
                  out_dtype=MXU_DTYPE)

    dw_dt32 = jnp.concatenate([dw_dt[:, 128 * g:128 * g + 8] for g in range(SSD_GROUPS)], axis=1)
    dw_tail = jnp.concatenate([dw_dt32, dw_gate, jnp.zeros((D_MODEL, DW_ALL_W - IN_W), MXU_DTYPE)], axis=1)
    dw_ref = lax.dynamic_update_slice_in_dim(dw_all, dw_tail, DW_DT_O, axis=1)
    dw_out_p, sib_i = _dw_out_rs_call(merged_t, dh_b, dw_ref, tn=1024, tk=tk_dw)
    p_out = dw_out_p.reshape(4, D_MODEL // 4, D_MODEL)

    s512 = jnp.sum(a512, axis=2)
    heads = lambda v: jnp.sum(v.reshape(32, HEADDIM), axis=1).reshape(1, 32)
    d_ssd_nw = s512[:, 0].reshape(1, D_MODEL)
    d_dskip = heads(s512[:, 1].reshape(D_MODEL))
    d_alog = heads(s512[:, 2].reshape(D_MODEL)) * (1.0 / HEADDIM) * (-jnp.exp(A_log))
    d_dtb = s512[:, 3, 0:8].reshape(1, 32)
    s768 = jnp.sum(a768, axis=2)
    ungroup = lambda v: jnp.concatenate([v[g, :, 0:512] for g in range(4)] + [v[g, :, 512:640] for g in range(4)]
                                        + [v[g, :, 640:768] for g in range(4)], axis=1)
    d_cw = ungroup(s768[:, 0:4])
    d_cb = ungroup(s768[:, 4:5])
    d_sgu_b = jnp.sum(dbfull.reshape(128, SGU_GROUPS, 128), axis=2).T.reshape(1, SGU_GROUPS, 128)
    d_sgu_w = jnp.where(smask[None], dws, 0.0).reshape(1, SGU_GROUPS, 128, 128)
    fold = lambda a8: jnp.sum(a8, axis=0, keepdims=True)
    small_local = [fold(dgb8), fold(dgam8), fold(dbeta8), d_sgu_w, d_sgu_b, d_cw, d_cb,
                   d_dtb, d_alog, d_dskip, d_ssd_nw, fold(dfw8).reshape(D_MODEL), jnp.sum(loss_t[:, 0, 0]).reshape(1)]
    small_shapes = [a.shape for a in small_local]
    v_local = _pack(small_local)

    core = lax.axis_index("c")
    place = jnp.stack([chip, core]).astype(jnp.int32)
    hr_o = D_MODEL // 8
    sib_o, sib_v = _rs_sibling_call(p_out, v_local)
    s1b_i, o_i = _rs_add_windows(dw_ref, sib_i, place, 256, "rs_add_in")
    s1b_o, o_o = _rs_add(p_out, sib_o, place, 256, "rs_add_out")
    chip_v = _add_pair(v_local, sib_v, v_local.shape[0], "ar_add_small")
    dxn, r_i, r_o, abs_v = _dx_rs_call(dproj_a, wt_a, dproj_b, wt_b, s1b_i, s1b_o, chip_v, tm=tm_mm)
    grad_x, dnw8 = _gradx_call(x2, dxn, dh_b, norm_w, t_row)
    abs_v = lax.dynamic_update_index_in_dim(abs_v, chip_v, chip, 0)
    f_i = _sum_own_recv(o_i, r_i, 256, "rs_sum_in")
    f_o = _sum_own_recv(o_o, r_o, 256, "rs_sum_out")
    sib_f_i, g_w_out, all_nw = _rs_join_call(f_i, f_o, dnw8)
    g_w_out = lax.dynamic_update_slice_in_dim(g_w_out, f_o, core * hr_o, axis=0)
    all_nw = lax.dynamic_update_index_in_dim(all_nw, dnw8, 2 * chip + core, 0)
    g_nw = fold(_sum_slots(all_nw, "ar_sum_norm_w"))
    total_v = _sum_slots(abs_v, "ar_sum_small")
    (g_gb, g_gam, g_beta, g_sw, g_sb, g_cw_full, g_cb, g_dtb, g_alog, g_dsk, g_snw, g_fnw, loss1) = _unpack(total_v, small_shapes)
    g_cw_shard = lax.dynamic_slice(g_cw_full, (0, chip * 768), (4, 768)).reshape(1, 4, 768)
    loss = loss1.reshape(())

    shard_t = lambda win: lax.dynamic_slice_in_dim(win, 8 * chip, SHARD_W, axis=1).T
    g_w_in, d_win, nm_win, nv_win = (a.T for a in _adamw_halves(w_in[0].T, shard_t(f_i), shard_t(sib_f_i), m_w_in[0].T,
                                                                v_w_in[0].T, place, 296, "adamw_w_in"))
    d_wout, nm_wout, nv_wout = _adamw(w_out[0], g_w_out, m_w_out[0], v_w_out[0], 128, "adamw_w_out")
    small_w = [norm_w, gate_b, sgu_norm_g, sgu_norm_b, sgu_w, sgu_b, conv_w, conv_b, dt_bias, A_log, D_skip, ssd_norm_w, final_norm_w]
    small_m = [m_norm_w, m_gate_b, m_sgu_norm_g, m_sgu_norm_b, m_sgu_w, m_sgu_b, m_conv_w, m_conv_b, m_dt_bias, m_A_log, m_D_skip, m_ssd_norm_w, m_final_norm_w]
    small_v = [v_norm_w, v_gate_b, v_sgu_norm_g, v_sgu_norm_b, v_sgu_w, v_sgu_b, v_conv_w, v_conv_b, v_dt_bias, v_A_log, v_D_skip, v_ssd_norm_w, v_final_norm_w]
    small_g = [g_nw, g_gb, g_gam, g_beta, g_sw, g_sb, g_cw_shard, g_cb, g_dtb, g_alog, g_dsk, g_snw, g_fnw]
    shapes_w = [a.shape for a in small_w]
    small_g = [a.reshape(shp) for a, shp in zip(small_g, shapes_w)]
    pw = _pack(small_w)
    pd, pm, pv = _adamw(pw, _pack(small_g), _pack(small_m), _pack(small_v), pw.shape[0], "adamw_small")
    d_small, nm_small, nv_small = _unpack(pd, shapes_w), _unpack(pm, shapes_w), _unpack(pv, shapes_w)

    def with_big(small, win, wout):
        o = list(small)
        return o[0:1] + [win.reshape(1, D_MODEL, SHARD_W)] + o[1:12] + [wout.reshape(1, D_MODEL // 4, D_MODEL)] + o[12:13]

    grads = with_big(small_g, g_w_in, g_w_out)
    deltas = with_big(d_small, d_win, d_wout)
    new_m = with_big(nm_small, nm_win, nm_wout)
    new_v = with_big(nv_small, nv_win, nv_wout)
    return (loss, grad_x.reshape(1, s, D_MODEL), *grads, *deltas, *new_m, *new_v)
```
